```python
import math
import jax, jax.numpy as jnp
from jax import lax
import numpy as np

D_MODEL = 2048
BATCH = 16
SEQ = 256
DEPTH = 2
DEC_BATCH = 4
DEC_SEQ = 1024
PAST_LEN = 512

GRID_W = 64
EPS = 1e-6
D_FF = 4 * D_MODEL

HG_HEADS = 8
HG_DK = 128
HG_DV = 128
HG_F = HG_HEADS * HG_DK
HG_V = HG_HEADS * HG_DV
HG_CHUNK = 16

CM_GROUPS = 8
CM_GROUP_DIM = 128
CM_W = CM_GROUPS * CM_GROUP_DIM
CM_CHUNK = 128

GDN_HEADS = 8
GDN_DK = 128
GDN_DV = 128
GDN_K = GDN_HEADS * GDN_DK
GDN_V = GDN_HEADS * GDN_DV
GDN_QKV = 2 * GDN_K + GDN_V
GDN_CHUNK = 64
CONV_K = 3

IN_SIZES = (HG_F, HG_V, HG_V, HG_F, HG_F,
            CM_W, CM_W,
            GDN_QKV, GDN_V, 2 * GDN_HEADS, 2 * GDN_HEADS,
            3 * D_MODEL)
IN_DIM = sum(IN_SIZES)

kernel_name = 'hybrid_hgrn2_gmlp_gdn_diffusion_step'


def rmsnorm(x, g):
    xf = x.astype(jnp.float32)
    y = xf * lax.rsqrt(jnp.mean(jnp.square(xf), axis=-1, keepdims=True) + EPS)
    return (y * g.astype(jnp.float32)).astype(x.dtype)


def l2norm(x):
    return x * lax.rsqrt(jnp.sum(jnp.square(x), axis=-1, keepdims=True) + EPS)


def rev(t):
    return jnp.flip(t, axis=1)


def hgrn2_scan(q, k, v, log_f, s0):
    B, L, H, DK = q.shape
    DV = v.shape[-1]
    C = HG_CHUNK
    N = L // C
    q, k, v, log_f = (t.reshape(B, N, C, H, t.shape[-1]) for t in (q, k, v, log_f))
    b = jnp.cumsum(log_f, axis=2)
    causal = jnp.tril(jnp.ones((C, C), dtype=bool))[:, :, None, None]
    decay = jnp.exp(jnp.where(causal, b[:, :, :, None] - b[:, :, None, :], -jnp.inf))
    scores = jnp.einsum('bntHd,bnsHd,bntsHd->bnHts', q, k, decay)
    o_intra = jnp.einsum('bnHts,bnsHv->bntHv', scores, v)
    b_last = b[:, :, -1]
    q_dec = q * jnp.exp(b)
    k_dec = k * jnp.exp(b_last[:, :, None] - b)

    def step(S, xs):
        qd, kd, vv, bl = xs
        o = jnp.einsum('bcHd,bHdv->bcHv', qd, S)
        S = S * jnp.exp(bl)[..., None] + jnp.einsum('bcHd,bcHv->bHdv', kd, vv)
        return S, o

    xs = tuple(jnp.moveaxis(t, 1, 0) for t in (q_dec, k_dec, v, b_last))
    s_T, o_inter = lax.scan(step, s0.astype(jnp.float32), xs)
    o = o_intra + jnp.moveaxis(o_inter, 0, 1)
    return o.reshape(B, L, H, DV), s_T


def gdn_scan(q, k, v, g, beta, s0):
    B, L, H, DK = q.shape
    DV = v.shape[-1]
    C = GDN_CHUNK
    N = L // C
    to_chunks = lambda t: jnp.moveaxis(t.reshape((B, N, C) + t.shape[2:]), 3, 2)
    q, k, v, g, beta = (to_chunks(t) for t in (q, k, v, g, beta))
    gc = jnp.cumsum(g, axis=-1)
    idx = jnp.arange(C)
    lower_incl = idx[:, None] >= idx[None, :]
    strict = idx[:, None] > idx[None, :]
    decay = jnp.exp(jnp.where(lower_incl, gc[..., :, None] - gc[..., None, :], -jnp.inf))
    kb = k * beta[..., None]
    m = jnp.where(strict, jnp.einsum('bnhtd,bnhsd->bnhts', kb, k) * decay, 0.0)
    eye = jnp.broadcast_to(jnp.eye(C, dtype=m.dtype), m.shape)
    T = lax.linalg.triangular_solve(m, eye, left_side=True, lower=True, unit_diagonal=True)
    u = T @ (v * beta[..., None])
    w = T @ (kb * jnp.exp(gc)[..., None])
    attn = jnp.einsum('bnhtd,bnhsd->bnhts', q, k) * decay
    q_dec = q * jnp.exp(gc)[..., None]
    g_last = gc[..., -1]
    k_dec = k * jnp.exp(g_last[..., None] - gc)[..., None]

    def step(S, xs):
        qd, kd, uu, ww, at, gl = xs
        v_new = uu - jnp.einsum('bhcd,bhdv->bhcv', ww, S)
        o = jnp.einsum('bhcd,bhdv->bhcv', qd, S) + jnp.einsum('bhts,bhsv->bhtv', at, v_new)
        S = S * jnp.exp(gl)[..., None, None] + jnp.einsum('bhcd,bhcv->bhdv', kd, v_new)
        return S, o

    xs = tuple(jnp.moveaxis(t, 1, 0) for t in (q_dec, k_dec, u, w, attn, g_last))
    s_T, o = lax.scan(step, s0.astype(jnp.float32), xs)
    o = jnp.moveaxis(jnp.moveaxis(o, 0, 1), 2, 3).reshape(B, L, H, DV)
    return o, s_T


def hgrn2_branch(q_raw, i_raw, g_raw, ff_raw, fb_raw, lb, onorm_g, s0):
    B, L, _ = q_raw.shape
    heads = lambda t, d: t.reshape(B, L, HG_HEADS, d)
    q = heads(jax.nn.silu(q_raw.astype(jnp.float32)), HG_DK)
    v = heads(i_raw.astype(jnp.float32), HG_DV)

    def gates(z, lbd):
        f = lbd + (1.0 - lbd) * jax.nn.sigmoid(z.astype(jnp.float32))
        return heads(1.0 - f, HG_DK), heads(jnp.log(f), HG_DK)

    k_f, lf_f = gates(ff_raw, lb[0])
    k_b, lf_b = gates(fb_raw, lb[1])
    o_f, s_f = hgrn2_scan(q, k_f, v, lf_f, s0[:, 0])
    o_b, s_b = hgrn2_scan(rev(q), rev(k_b), rev(v), rev(lf_b), s0[:, 1])
    o = rmsnorm(o_f + rev(o_b), onorm_g).reshape(B, L, HG_V) * jax.nn.silu(g_raw.astype(jnp.float32))
    return o.astype(q_raw.dtype), jnp.stack([s_f, s_b], axis=1)


def chunk_mlp_branch(u_raw, v_raw, vnorm_g, ws, bs):
    B, L, _ = u_raw.shape
    N = L // CM_CHUNK
    u = jax.nn.gelu(u_raw, approximate=False)
    v = jax.nn.gelu(v_raw, approximate=False).reshape(B, L, CM_GROUPS, CM_GROUP_DIM)
    v = rmsnorm(v, vnorm_g.reshape(CM_GROUPS, CM_GROUP_DIM)).reshape(B, N, CM_CHUNK, CM_GROUPS, CM_GROUP_DIM)
    s = jnp.einsum('gpq,bnqgc->bnpgc', ws, v) + bs.T[:, :, None]
    return u * s.reshape(B, L, CM_W)


def short_conv(x, w, rows):
    B, L, Cc = x.shape
    p = CONV_K // 2
    if rows is not None:
        y = lax.conv_general_dilated(x.reshape(B, rows, GRID_W, Cc), w[:, :, None, :].astype(x.dtype),
                                     (1, 1), [(p, p), (p, p)], dimension_numbers=('NHWC', 'HWIO', 'NHWC'),
                                     feature_group_count=Cc)
        return y.reshape(B, L, Cc)
    return lax.conv_general_dilated(x, w[p][:, None, :].astype(x.dtype), (1,), [(p, p)],
                                    dimension_numbers=('NWC', 'WIO', 'NWC'), feature_group_count=Cc)


def gdn_branch(qkv_raw, g_raw, a_raw, b_raw, conv_w, A_log, dt_bias, onorm_g, s0, rows):
    B, L, _ = qkv_raw.shape
    qkv = jax.nn.silu(short_conv(qkv_raw, conv_w, rows).astype(jnp.float32))
    q, k, v = jnp.split(qkv, [GDN_K, 2 * GDN_K], axis=-1)
    q = l2norm(q.reshape(B, L, GDN_HEADS, GDN_DK)) * (GDN_DK ** -0.5)
    k = l2norm(k.reshape(B, L, GDN_HEADS, GDN_DK))
    v = v.reshape(B, L, GDN_HEADS, GDN_DV)
    a = a_raw.astype(jnp.float32).reshape(B, L, 2, GDN_HEADS)
    g = -jnp.exp(A_log.astype(jnp.float32)) * jax.nn.softplus(a + dt_bias.astype(jnp.float32))
    beta = jax.nn.sigmoid(b_raw.astype(jnp.float32).reshape(B, L, 2, GDN_HEADS))
    o_f, s_f = gdn_scan(q, k, v, g[:, :, 0], beta[:, :, 0], s0[:, 0])
    o_b, s_b = gdn_scan(rev(q), rev(k), rev(v), rev(g[:, :, 1]), rev(beta[:, :, 1]), s0[:, 1])
    o = rmsnorm(o_f + rev(o_b), onorm_g).reshape(B, L, GDN_V) * jax.nn.silu(g_raw.astype(jnp.float32))
    return o.astype(qkv_raw.dtype), jnp.stack([s_f, s_b], axis=1)


def block(x, mod, p, lb, s0_hg, s0_gdn, rows):
    shift1, scale1, gate1, shift2, scale2, gate2 = jnp.split(mod, 6, axis=-1)
    h = rmsnorm(x, p['norm1_g']) * (1.0 + scale1) + shift1
    (hq, hi, hgo, hff, hfb, cu, cv, gqkv, ggo, ga, gbeta, gates) = jnp.split(
        h @ p['w_in'], np.cumsum(IN_SIZES)[:-1].tolist(), axis=-1)
    o_a, s_hg = hgrn2_branch(hq, hi, hgo, hff, hfb, lb, p['hg_onorm_g'], s0_hg)
    o_b = chunk_mlp_branch(cu, cv, p['cm_vnorm_g'], p['cm_ws'], p['cm_bs'])
    o_c, s_gdn = gdn_branch(gqkv, ggo, ga, gbeta, p['gdn_conv'], p['gdn_A_log'], p['gdn_dt_bias'],
                            p['gdn_onorm_g'], s0_gdn, rows)
    g_a, g_b, g_c = jnp.split(jax.nn.sigmoid(gates), 3, axis=-1)
    merged = (g_a * (o_a @ p['w_br_hg']) + g_b * (o_b @ p['w_br_cm']) + g_c * (o_c @ p['w_br_gdn']))
    x = x + gate1 * (merged @ p['w_out'])
    h2 = rmsnorm(x, p['norm2_g']) * (1.0 + scale2) + shift2
    x = x + gate2 * (jnp.square(jax.nn.relu(h2 @ p['w_ff1'])) @ p['w_ff2'])
    return x, s_hg, s_gdn


def setup_inputs(seed: int = 0) -> dict:
    key = jax.random.key(seed)
    ks = iter(jax.random.split(key, 40))
    f32 = jnp.float32
    D = D_MODEL

    def nrm(shape, scale):
        return jax.random.normal(next(ks), shape, f32) * scale

    x_prompt = nrm((BATCH, SEQ, D), 1.0)
    x_sample = nrm((DEC_BATCH, DEC_SEQ, D), 1.0)
    c = nrm((DEC_BATCH, D), 1.0)
    state_hgrn = nrm((DEC_BATCH, DEPTH, 2, HG_HEADS, HG_DK, HG_DV), 0.5)
    state_gdn = nrm((DEC_BATCH, DEPTH, 2, GDN_HEADS, GDN_DK, GDN_DV), 0.1)
    c_ctx = nrm((D,), 1.0)
    norm1_g = 1.0 + nrm((DEPTH, D), 0.02)
    norm2_g = 1.0 + nrm((DEPTH, D), 0.02)
    w_mod = nrm((DEPTH, D, 6 * D), D ** -0.5)
    b_mod = nrm((DEPTH, 6 * D), 0.02)
    w_in = nrm((DEPTH, D, IN_DIM), D ** -0.5)
    hg_lb = nrm((DEPTH, 2, HG_F), 1.0)
    hg_onorm_g = 1.0 + nrm((DEPTH, HG_DV), 0.02)
    cm_vnorm_g = 1.0 + nrm((DEPTH, CM_W), 0.02)
    cm_ws = nrm((DEPTH, CM_GROUPS, CM_CHUNK, CM_CHUNK), CM_CHUNK ** -0.5)
    cm_bs = 1.0 + nrm((DEPTH, CM_GROUPS, CM_CHUNK), 0.02)
    gdn_conv = nrm((DEPTH, CONV_K, CONV_K, GDN_QKV), 0.5)
    gdn_A_log = jnp.log(jax.random.uniform(next(ks), (DEPTH, 2, GDN_HEADS), f32, 1.0, 16.0))
    dt = jnp.exp(jax.random.uniform(next(ks), (DEPTH, 2, GDN_HEADS), f32, math.log(1e-3), math.log(1e-1)))
    gdn_dt_bias = dt + jnp.log(-jnp.expm1(-dt))
    gdn_onorm_g = 1.0 + nrm((DEPTH, GDN_DV), 0.02)
    w_br_hg = nrm((DEPTH, HG_V, D), HG_V ** -0.5)
    w_br_cm = nrm((DEPTH, CM_W, D), CM_W ** -0.5)
    w_br_gdn = nrm((DEPTH, GDN_V, D), GDN_V ** -0.5)
    w_out = nrm((DEPTH, D, D), D ** -0.5)
    w_ff1 = nrm((DEPTH, D, D_FF), D ** -0.5)
    w_ff2 = nrm((DEPTH, D_FF, D), D_FF ** -0.5)
    final_g = 1.0 + nrm((D,), 0.02)
    return {'x_prompt': x_prompt, 'x_sample': x_sample, 'c': c,
            'state_hgrn': state_hgrn, 'state_gdn': state_gdn, 'c_ctx': c_ctx,
            'norm1_g': norm1_g, 'norm2_g': norm2_g, 'w_mod': w_mod, 'b_mod': b_mod, 'w_in': w_in,
            'hg_lb': hg_lb, 'hg_onorm_g': hg_onorm_g, 'cm_vnorm_g': cm_vnorm_g, 'cm_ws': cm_ws,
            'cm_bs': cm_bs, 'gdn_conv': gdn_conv, 'gdn_A_log': gdn_A_log, 'gdn_dt_bias': gdn_dt_bias,
            'gdn_onorm_g': gdn_onorm_g, 'w_br_hg': w_br_hg, 'w_br_cm': w_br_cm, 'w_br_gdn': w_br_gdn,
            'w_out': w_out, 'w_ff1': w_ff1, 'w_ff2': w_ff2, 'final_g': final_g}


def reference(x_prompt, x_sample, c, state_hgrn, state_gdn, c_ctx, norm1_g, norm2_g, w_mod, b_mod, w_in,
              hg_lb, hg_onorm_g, cm_vnorm_g, cm_ws, cm_bs, gdn_conv, gdn_A_log, gdn_dt_bias, gdn_onorm_g,
              w_br_hg, w_br_cm, w_br_gdn, w_out, w_ff1, w_ff2, final_g):
    lb_all = jnp.cumsum(jax.nn.softmax(hg_lb.astype(jnp.float32), axis=0), axis=0)
    lb_all = lb_all - lb_all[:1]
    rows = x_sample.shape[1] // GRID_W
    n_ctx = x_prompt.shape[0]
    zeros_hg = jnp.zeros((n_ctx, 2, HG_HEADS, HG_DK, HG_DV), jnp.float32)
    zeros_gdn = jnp.zeros((n_ctx, 2, GDN_HEADS, GDN_DK, GDN_DV), jnp.float32)
    xp = x_prompt
    xs = x_sample
    new_hg = []
    new_gdn = []
    for l in range(DEPTH):
        p = dict(norm1_g=norm1_g[l], norm2_g=norm2_g[l], w_in=w_in[l], hg_onorm_g=hg_onorm_g[l],
                 cm_vnorm_g=cm_vnorm_g[l], cm_ws=cm_ws[l], cm_bs=cm_bs[l], gdn_conv=gdn_conv[l],
                 gdn_A_log=gdn_A_log[l], gdn_dt_bias=gdn_dt_bias[l], gdn_onorm_g=gdn_onorm_g[l],
                 w_br_hg=w_br_hg[l], w_br_cm=w_br_cm[l], w_br_gdn=w_br_gdn[l], w_out=w_out[l],
                 w_ff1=w_ff1[l], w_ff2=w_ff2[l])
        mod_ctx = (jax.nn.silu(c_ctx) @ w_mod[l] + b_mod[l])[None, None, :]
        mod_lat = (jax.nn.silu(c) @ w_mod[l] + b_mod[l])[:, None, :]
        xp, s_hg, s_gdn = block(xp, mod_ctx, p, lb_all[l], zeros_hg, zeros_gdn, None)
        new_hg.append(s_hg)
        new_gdn.append(s_gdn)
        xs, _, _ = block(xs, mod_lat, p, lb_all[l], state_hgrn[:, l], state_gdn[:, l], rows)
    y_prompt = rmsnorm(xp, final_g)
    y_sample = rmsnorm(xs, final_g)
    new_state_hgrn = jnp.stack(new_hg, axis=1).astype(x_prompt.dtype)
    new_state_gdn = jnp.stack(new_gdn, axis=1).astype(x_prompt.dtype)
    return (y_prompt, y_sample, new_state_hgrn, new_state_gdn)
```

```python
import functools

import jax
import jax.numpy as jnp
from jax import lax
from jax.experimental import pallas as pl
from jax.experimental.pallas import tpu as pltpu

F32 = jnp.float32
BF16 = jnp.bfloat16

D_MODEL = 2048
BATCH = 16
SEQ = 256
DEPTH = 2
DEC_BATCH = 4
DEC_SEQ = 1024
GRID_W = 64
EPS = 1e-6
D_FF = 4 * D_MODEL
HEADS = 8
HEAD_DIM = 128
HG_F = HEADS * HEAD_DIM
CM_GROUPS = 8
CM_W = CM_GROUPS * HEAD_DIM
CM_CHUNK = 128
GDN_CHUNK = 64
CONV_K = 3

OFF_HQ, OFF_HI, OFF_HG, OFF_HFF, OFF_HFB = 0, 1024, 2048, 3072, 4096
OFF_CU, OFF_CV = 5120, 6144
OFF_GQ, OFF_GK, OFF_GV, OFF_GG = 7168, 8192, 9216, 10240
OFF_AB = 11264
OFF_GATES = 11296
IN_DIM = 17440
N_MAIN = OFF_AB

N_CTX_TOK = BATCH * SEQ
N_LAT_TOK = DEC_BATCH * DEC_SEQ
N_TOK = N_CTX_TOK + N_LAT_TOK
UNIT = DEC_SEQ
N_CTX_UNITS = N_CTX_TOK // UNIT
N_UNITS = N_TOK // UNIT
SEQ_PER_UNIT = UNIT // SEQ

LANES = 128
MOD_ROWS = 8
HG_BLOCK = 256
HG_SUB = 32
VMEM_LIMIT = 56 * 1024 * 1024

_NT = (((1,), (1,)), ((), ()))
_TN = (((0,), (0,)), ((), ()))
_NN = (((1,), (0,)), ((), ()))


def _dg(a, b, dims):
    return lax.dot_general(a, b, dims, preferred_element_type=F32)


def _split2(x):
    hi = x.astype(BF16)
    lo = (x - hi.astype(F32)).astype(BF16)
    return hi, lo


def _dot1(a, b, dims=_NN):
    return _dg(a.astype(BF16), b.astype(BF16), dims)


def _dot3(a, b, dims=_NN):
    ah, al = _split2(a)
    bh, bl = _split2(b)
    return _dg(ah, bh, dims) + (_dg(ah, bl, dims) + _dg(al, bh, dims))


def _dot01(m, x):
    hi = x.astype(BF16)
    r = x - hi.astype(F32)
    mid = r.astype(BF16)
    lo = (r - mid.astype(F32)).astype(BF16)
    return _dg(m, hi, _NN) + (_dg(m, mid, _NN) + _dg(m, lo, _NN))


def _sigmoid(x):
    return 1.0 / (1.0 + jnp.exp(-x))


def _silu(x):
    return x * _sigmoid(x)


def _gelu(x):
    return 0.5 * x * (1.0 + lax.erf(x * (2.0 ** -0.5)))


def _softplus(x):
    return jnp.maximum(x, 0.0) + jnp.log1p(jnp.exp(-jnp.abs(x)))


def _mod_row_of_tile(i, tm):
    return jnp.maximum(0, (i * tm - N_CTX_TOK) // DEC_SEQ + 1)


def _cparams(sem):
    return pltpu.CompilerParams(dimension_semantics=sem, vmem_limit_bytes=VMEM_LIMIT)


def _mod_kernel(c_ref, w_ref, b_ref, o_ref):
    s = _silu(c_ref[...])
    hi, lo = _split2(s)
    w = w_ref[...].astype(BF16)
    o_ref[...] = _dg(hi, w, _NN) + _dg(lo, w, _NN) + b_ref[...]


def _mod_call(cvec, w_mod, b_mod):
    tn = 1024
    n = 6 * D_MODEL
    return pl.pallas_call(
        _mod_kernel,
        grid=(DEPTH, n // tn),
        in_specs=[pl.BlockSpec((MOD_ROWS, D_MODEL), lambda l, j: (0, 0)),
                  pl.BlockSpec((None, D_MODEL, tn), lambda l, j: (l, 0, j)),
                  pl.BlockSpec((None, 1, tn), lambda l, j: (l, 0, j))],
        out_specs=pl.BlockSpec((None, MOD_ROWS, tn), lambda l, j: (l, 0, j)),
        out_shape=jax.ShapeDtypeStruct((DEPTH, MOD_ROWS, n), F32),
        compiler_params=_cparams(("arbitrary", "arbitrary")),
        name="modulation",
    )(cvec, w_mod, b_mod.reshape(DEPTH, 1, n))


def _normmod_kernel(x_ref, g_ref, sh_ref, sc_ref, o_ref):
    x = x_ref[...]
    y = x * lax.rsqrt(jnp.mean(x * x, axis=-1, keepdims=True) + EPS) * g_ref[...]
    o_ref[...] = (y * (1.0 + sc_ref[...]) + sh_ref[...]).astype(o_ref.dtype)


def _normmod_call(x, g, mod3, layer, k_shift, k_scale):
    tm = 256
    def mod_spec(k):
        return pl.BlockSpec((None, 1, D_MODEL),
                            lambda i: (layer * MOD_ROWS + _mod_row_of_tile(i, tm), 0, k))
    return pl.pallas_call(
        _normmod_kernel,
        grid=(N_TOK // tm,),
        in_specs=[pl.BlockSpec((tm, D_MODEL), lambda i: (i, 0)),
                  pl.BlockSpec((1, D_MODEL), lambda i: (0, 0)),
                  mod_spec(k_shift), mod_spec(k_scale)],
        out_specs=pl.BlockSpec((tm, D_MODEL), lambda i: (i, 0)),
        out_shape=jax.ShapeDtypeStruct((N_TOK, D_MODEL), BF16),
        compiler_params=_cparams(("arbitrary",)),
        name="norm_mod",
    )(x, g.reshape(1, D_MODEL), mod3, mod3)


def _final_norm_kernel(x_ref, g_ref, o_ref):
    x = x_ref[...]
    o_ref[...] = x * lax.rsqrt(jnp.mean(x * x, axis=-1, keepdims=True) + EPS) * g_ref[...]


def _final_norm_call(x, g):
    tm = 256
    return pl.pallas_call(
        _final_norm_kernel,
        grid=(N_TOK // tm,),
        in_specs=[pl.BlockSpec((tm, D_MODEL), lambda i: (i, 0)),
                  pl.BlockSpec((1, D_MODEL), lambda i: (0, 0))],
        out_specs=pl.BlockSpec((tm, D_MODEL), lambda i: (i, 0)),
        out_shape=jax.ShapeDtypeStruct((N_TOK, D_MODEL), F32),
        compiler_params=_cparams(("arbitrary",)),
        name="final_norm",
    )(x, g.reshape(1, D_MODEL))


def _mm_kernel(x_ref, w_ref, o_ref, *, act):
    acc = jnp.dot(x_ref[...], w_ref[...], preferred_element_type=F32)
    if act == "sigmoid":
        acc = _sigmoid(acc)
    elif act == "relu2":
        acc = jnp.square(jnp.maximum(acc, 0.0))
    o_ref[...] = acc.astype(o_ref.dtype)


def _mm_call(x, w, *, act, out_dtype, tm, tn, name):
    m, k = x.shape
    n = w.shape[1]
    return pl.pallas_call(
        functools.partial(_mm_kernel, act=act),
        grid=(m // tm, pl.cdiv(n, tn)),
        in_specs=[pl.BlockSpec((tm, k), lambda i, j: (i, 0)),
                  pl.BlockSpec((k, tn), lambda i, j: (0, j))],
        out_specs=pl.BlockSpec((tm, tn), lambda i, j: (i, j)),
        out_shape=jax.ShapeDtypeStruct((m, n), out_dtype),
        compiler_params=_cparams(("arbitrary", "arbitrary")),
        name=name,
    )(x, w)


def _mm_resid_kernel(x_ref, w_ref, r_ref, g_ref, o_ref):
    acc = jnp.dot(x_ref[...], w_ref[...], preferred_element_type=F32)
    o_ref[...] = r_ref[...] + g_ref[...] * acc


def _mm_resid_call(x, w, resid, mod3, layer, k_gate, *, tm, tn, name):
    m, k = x.shape
    n = w.shape[1]
    per_tile = D_MODEL // tn
    return pl.pallas_call(
        _mm_resid_kernel,
        grid=(m // tm, n // tn),
        in_specs=[pl.BlockSpec((tm, k), lambda i, j: (i, 0)),
                  pl.BlockSpec((k, tn), lambda i, j: (0, j)),
                  pl.BlockSpec((tm, tn), lambda i, j: (i, j)),
                  pl.BlockSpec((None, 1, tn),
                               lambda i, j: (layer * MOD_ROWS + _mod_row_of_tile(i, tm), 0,
                                             k_gate * per_tile + j))],
        out_specs=pl.BlockSpec((tm, tn), lambda i, j: (i, j)),
        out_shape=jax.ShapeDtypeStruct((m, n), F32),
        compiler_params=_cparams(("arbitrary", "arbitrary")),
        name=name,
    )(x, w, resid, mod3)


def _merge_kernel(oa_ref, ob_ref, oc_ref, wa_ref, wb_ref, wc_ref, ga_ref, gb_ref, gc_ref, o_ref):
    a = jnp.dot(oa_ref[...], wa_ref[...], preferred_element_type=F32)
    b = jnp.dot(ob_ref[...], wb_ref[...], preferred_element_type=F32)
    c = jnp.dot(oc_ref[...], wc_ref[...], preferred_element_type=F32)
    o_ref[...] = (ga_ref[...] * a + gb_ref[...] * b + gc_ref[...] * c).astype(o_ref.dtype)


def _merge_call(o_a, o_b, o_c, w_a, w_b, w_c, gates):
    tm, tn = 1024, 512
    nj = D_MODEL // tn
    br = pl.BlockSpec((tm, HG_F), lambda i, j: (i, 0))
    wt = pl.BlockSpec((HG_F, tn), lambda i, j: (0, j))
    def gate_spec(k):
        return pl.BlockSpec((tm, tn), lambda i, j: (i, k * nj + j))
    return pl.pallas_call(
        _merge_kernel,
        grid=(N_TOK // tm, nj),
        in_specs=[br, br, br, wt, wt, wt, gate_spec(0), gate_spec(1), gate_spec(2)],
        out_specs=pl.BlockSpec((tm, tn), lambda i, j: (i, j)),
        out_shape=jax.ShapeDtypeStruct((N_TOK, D_MODEL), BF16),
        compiler_params=_cparams(("arbitrary", "arbitrary")),
        name="branch_merge",
    )(o_a, o_b, o_c, w_a, w_b, w_c, gates, gates, gates)


def _head_rmsnorm_gate(o, g, og):
    y = o * lax.rsqrt(jnp.mean(o * o, axis=-1, keepdims=True) + EPS) * g
    return y * _silu(og)


def _hgrn_kernel(q_ref, i_ref, og_ref, ff_ref, fb_ref, lb_ref, g_ref, s0_ref, o_ref, s_ref,
                 acc_ref, b_ref, st_ref, stc_ref, km_ref):
    is_ctx = pl.program_id(0) < N_CTX_UNITS
    nblk = UNIT // HG_BLOCK
    nsub = HG_BLOCK // HG_SUB
    r = lax.broadcasted_iota(jnp.int32, (HG_BLOCK, HG_BLOCK), 0)
    c = lax.broadcasted_iota(jnp.int32, (HG_BLOCK, HG_BLOCK), 1)
    same = (r // HG_SUB) == (c // HG_SUB)
    row_sub = lax.broadcasted_iota(jnp.int32, (HG_BLOCK, HEAD_DIM), 0) // HG_SUB

    def expand(x):
        return jnp.broadcast_to(x[:, None, :], (nsub, HG_SUB, HEAD_DIM)).reshape(HG_BLOCK, HEAD_DIM)

    def run_dir(z_ref, d, fwd):
        tri = jnp.logical_and(same, (c <= r) if fwd else (c >= r))
        tri_bf = jnp.where(tri, 1.0, 0.0).astype(BF16)
        lb = lb_ref[d:d + 1, :]
        st_ref[...] = s0_ref[d].T

        def body(j, carry):
            blk = j if fwd else nblk - 1 - j
            rows = pl.ds(pl.multiple_of(blk * HG_BLOCK, HG_BLOCK), HG_BLOCK)
            q = _silu(q_ref[rows, :])
            v = i_ref[rows, :]
            f = lb + (1.0 - lb) * _sigmoid(z_ref[rows, :])
            lf = jnp.log(f)
            k = 1.0 - f
            b = _dot01(tri_bf, lf)
            b_ref[...] = b
            tot = b_ref[pl.ds((HG_SUB - 1) if fwd else 0, nsub, stride=HG_SUB), :]
            mid = b_ref[pl.ds(HG_SUB // 2, nsub, stride=HG_SUB), :]
            tot_f = expand(tot)
            mid_f = expand(mid)
            qt = q * jnp.exp(b - mid_f)
            kt = k * jnp.exp(mid_f - b)
            sc = jnp.where(tri, _dot3(qt, kt, _NT), 0.0)
            o = _dot1(sc, v)
            qd = (q * jnp.exp(b)).astype(BF16)
            kd = k * jnp.exp(tot_f - b)
            for s in range(nsub):
                km_ref[:, s * HEAD_DIM:(s + 1) * HEAD_DIM] = jnp.where(row_sub == s, kd, 0.0).astype(BF16)
            ut = _dg(v.astype(BF16), km_ref[...], _TN)
            st = jnp.where(is_ctx, 0.0, st_ref[...])
            for s in (range(nsub) if fwd else range(nsub - 1, -1, -1)):
                stc_ref[s] = st
                st = st * jnp.exp(tot[s:s + 1, :]) + ut[:, s * HEAD_DIM:(s + 1) * HEAD_DIM]
            st_ref[...] = st
            s_ref[blk, d] = st.T
            o_int = [_dg(qd[s * HG_SUB:(s + 1) * HG_SUB], stc_ref[s].astype(BF16), _NT)
                     for s in range(nsub)]
            o = o + jnp.concatenate(o_int, axis=0)
            if fwd:
                acc_ref[rows, :] = o
            else:
                acc_ref[rows, :] += o
            return carry

        lax.fori_loop(0, nblk, body, 0)

    run_dir(ff_ref, 0, True)
    run_dir(fb_ref, 1, False)
    o_ref[...] = _head_rmsnorm_gate(acc_ref[...], g_ref[...], og_ref[...]).astype(o_ref.dtype)


def _hgrn_call(z, lb, onorm_g, state, layer):
    def col(off):
        return pl.BlockSpec((UNIT, HEAD_DIM), lambda u, h: (u, off // HEAD_DIM + h))
    return pl.pallas_call(
        _hgrn_kernel,
        grid=(N_UNITS, HEADS),
        in_specs=[col(OFF_HQ), col(OFF_HI), col(OFF_HG), col(OFF_HFF), col(OFF_HFB),
                  pl.BlockSpec((2, HEAD_DIM), lambda u, h: (0, h)),
                  pl.BlockSpec((1, HEAD_DIM), lambda u, h: (0, 0)),
                  pl.BlockSpec((None, None, 2, None, HEAD_DIM, HEAD_DIM),
                               lambda u, h: (jnp.maximum(u - N_CTX_UNITS, 0), layer, 0, h, 0, 0))],
        out_specs=[pl.BlockSpec((UNIT, HEAD_DIM), lambda u, h: (u, h)),
                   pl.BlockSpec((None, SEQ_PER_UNIT, 2, None, HEAD_DIM, HEAD_DIM),
                                lambda u, h: (u, 0, 0, h, 0, 0))],
        out_shape=[jax.ShapeDtypeStruct((N_TOK, HG_F), BF16),
                   jax.ShapeDtypeStruct((N_UNITS, SEQ_PER_UNIT, 2, HEADS, HEAD_DIM, HEAD_DIM), F32)],
        scratch_shapes=[pltpu.VMEM((UNIT, HEAD_DIM), F32),
                        pltpu.VMEM((HG_BLOCK, HEAD_DIM), F32),
                        pltpu.VMEM((HEAD_DIM, HEAD_DIM), F32),
                        pltpu.VMEM((HG_BLOCK // HG_SUB, HEAD_DIM, HEAD_DIM), F32),
                        pltpu.VMEM((HG_BLOCK, (HG_BLOCK // HG_SUB) * HEAD_DIM), BF16)],
        compiler_params=_cparams(("arbitrary", "arbitrary")),
        name="hgrn2_scan",
    )(z, z, z, z, z, lb, onorm_g.reshape(1, HEAD_DIM), state)


def _gmlp_kernel(u_ref, v_ref, vn_ref, ws_ref, bs_ref, o_ref):
    tm = u_ref.shape[0]
    for g in range(CM_GROUPS):
        cols = slice(g * HEAD_DIM, (g + 1) * HEAD_DIM)
        vg = _gelu(v_ref[:, cols])
        vg = vg * lax.rsqrt(jnp.mean(vg * vg, axis=-1, keepdims=True) + EPS) * vn_ref[:, cols]
        ug = _gelu(u_ref[:, cols])
        w = ws_ref[g]
        bias = bs_ref[:, g:g + 1]
        for ch in range(tm // CM_CHUNK):
            rows = slice(ch * CM_CHUNK, (ch + 1) * CM_CHUNK)
            s = _dot3(w, vg[rows]) + bias
            o_ref[rows, cols] = (ug[rows] * s).astype(o_ref.dtype)


def _gmlp_call(z, vnorm_g, ws, bs_t):
    tm = 512
    return pl.pallas_call(
        _gmlp_kernel,
        grid=(N_TOK // tm,),
        in_specs=[pl.BlockSpec((tm, CM_W), lambda i: (i, OFF_CU // CM_W)),
                  pl.BlockSpec((tm, CM_W), lambda i: (i, OFF_CV // CM_W)),
                  pl.BlockSpec((1, CM_W), lambda i: (0, 0)),
                  pl.BlockSpec((CM_GROUPS, CM_CHUNK, CM_CHUNK), lambda i: (0, 0, 0)),
                  pl.BlockSpec((CM_CHUNK, CM_GROUPS), lambda i: (0, 0))],
        out_specs=pl.BlockSpec((tm, CM_W), lambda i: (i, 0)),
        out_shape=jax.ShapeDtypeStruct((N_TOK, CM_W), BF16),
        compiler_params=_cparams(("arbitrary",)),
        name="chunk_gmlp",
    )(z, z, vnorm_g.reshape(1, CM_W), ws, bs_t)


GP_BLOCK = 256


def _gdn_gates_kernel(ab_ref, alog_ref, dt_ref, col_ref, rowt_ref):
    ab = ab_ref[...]
    lane = lax.broadcasted_iota(jnp.int32, ab.shape, 1)
    g = jnp.where(lane < 2 * HEADS, -jnp.exp(alog_ref[...]) * _softplus(ab + dt_ref[...]), 0.0)
    r = lax.broadcasted_iota(jnp.int32, (GP_BLOCK, GP_BLOCK), 0)
    c = lax.broadcasted_iota(jnp.int32, (GP_BLOCK, GP_BLOCK), 1)
    same = (r // GDN_CHUNK) == (c // GDN_CHUNK)
    tri_f = jnp.where(jnp.logical_and(same, c <= r), 1.0, 0.0).astype(BF16)
    tri_b = jnp.where(jnp.logical_and(same, c >= r), 1.0, 0.0).astype(BF16)
    cf = _dot01(tri_f, g)
    cb = _dot01(tri_b, g)
    col = jnp.where(lane < HEADS, cf, jnp.where(lane < 2 * HEADS, cb, _sigmoid(ab)))
    col_ref[...] = col
    rowt_ref[...] = col.T


def _gdn_gates_call(ab, a_log, dt_bias):
    pad = lambda t: jnp.pad(t.reshape(1, 2 * HEADS), ((0, 0), (0, LANES - 2 * HEADS)))
    return pl.pallas_call(
        _gdn_gates_kernel,
        grid=(N_TOK // GP_BLOCK,),
        in_specs=[pl.BlockSpec((GP_BLOCK, LANES), lambda i: (i, 0)),
                  pl.BlockSpec((1, LANES), lambda i: (0, 0)),
                  pl.BlockSpec((1, LANES), lambda i: (0, 0))],
        out_specs=[pl.BlockSpec((GP_BLOCK, LANES), lambda i: (i, 0)),
                   pl.BlockSpec((LANES, GP_BLOCK), lambda i: (0, i))],
        out_shape=[jax.ShapeDtypeStruct((N_TOK, LANES), F32),
                   jax.ShapeDtypeStruct((LANES, N_TOK), F32)],
        compiler_params=_cparams(("arbitrary",)),
        name="gdn_gates",
    )(ab, pad(a_log), pad(dt_bias))


CONV_PAD = 72
GDN_PAIR = 2 * GDN_CHUNK


def _gdn_kernel(qr_ref, kr_ref, vr_ref, og_ref, col_ref, rowt_ref, cwq_ref, cwk_ref, cwv_ref,
                g_ref, s0_ref, o_ref, s_ref,
                xp_ref, q_s, k_s, v_s, u_s, w_s, qd_s, kd_s, att_s, acc_ref):
    is_ctx = pl.program_id(0) < N_CTX_UNITS
    head = pl.program_id(1)
    n_chunks = UNIT // GDN_CHUNK
    chunks_per_seq = SEQ // GDN_CHUNK

    t = lax.broadcasted_iota(jnp.int32, (UNIT, 1), 0)
    period = jnp.where(is_ctx, SEQ, GRID_W)
    pos = jnp.bitwise_and(t, period - 1)
    ok_left = pos != 0
    ok_right = pos != period - 1
    zeros_pad = jnp.zeros((CONV_PAD, HEAD_DIM), F32)
    xp_ref[0:CONV_PAD, :] = zeros_pad
    xp_ref[CONV_PAD + UNIT:CONV_PAD + UNIT + CONV_PAD, :] = zeros_pad

    def conv_silu(x_ref, w_ref):
        xp_ref[CONV_PAD:CONV_PAD + UNIT, :] = x_ref[...]
        acc = jnp.zeros((UNIT, HEAD_DIM), F32)
        for i in range(CONV_K):
            for j in range(CONV_K):
                w = w_ref[CONV_K * i + j:CONV_K * i + j + 1, :]
                if i != CONV_K // 2:
                    w = jnp.where(is_ctx, 0.0, w)
                start = CONV_PAD + (i - 1) * GRID_W + (j - 1)
                xs = xp_ref[start:start + UNIT, :]
                if j == 0:
                    xs = jnp.where(ok_left, xs, 0.0)
                elif j == CONV_K - 1:
                    xs = jnp.where(ok_right, xs, 0.0)
                acc = acc + xs * w
        return _silu(acc)

    def l2norm(x):
        return x * lax.rsqrt(jnp.sum(x * x, axis=-1, keepdims=True) + EPS)

    q_s[...] = l2norm(conv_silu(qr_ref, cwq_ref)) * (HEAD_DIM ** -0.5)
    k_s[...] = l2norm(conv_silu(kr_ref, cwk_ref))
    v_s[...] = conv_silu(vr_ref, cwv_ref)

    rr = lax.broadcasted_iota(jnp.int32, (GDN_CHUNK, GDN_CHUNK), 0)
    cc = lax.broadcasted_iota(jnp.int32, (GDN_CHUNK, GDN_CHUNK), 1)
    eye = jnp.where(rr == cc, 1.0, 0.0)
    same_blk = [(rr // b) == (cc // b) for b in (8, 16, 32, 64)]
    lane = lax.broadcasted_iota(jnp.int32, (GDN_PAIR, LANES), 1)
    sub8 = lax.broadcasted_iota(jnp.int32, (HEADS, GDN_PAIR), 0)

    def pick(x, j):
        return jnp.sum(jnp.where(lane[:x.shape[0]] == j, x, 0.0), axis=-1, keepdims=True)

    def phase1(p, carry):
        rows = pl.ds(pl.multiple_of(p * GDN_PAIR, GDN_PAIR), GDN_PAIR)
        q2, k2, v2 = q_s[rows, :], k_s[rows, :], v_s[rows, :]
        col = col_ref[rows, :]
        for half in range(2):
            sl = slice(half * GDN_CHUNK, (half + 1) * GDN_CHUNK)
            qc, kc, vc = q2[sl], k2[sl], v2[sl]
            kk = _dot3(kc, kc, _NT)
            qk = _dot1(qc, kc, _NT)
            out_rows = pl.ds(pl.multiple_of(p * GDN_PAIR + half * GDN_CHUNK, GDN_CHUNK), GDN_CHUNK)
            for d in range(2):
                fwd = d == 0
                gcol = pick(col, d * HEADS + head)[sl]
                beta = pick(col, (2 + d) * HEADS + head)[sl]
                grow8 = rowt_ref[d * HEADS:(d + 1) * HEADS, rows]
                grow = jnp.sum(jnp.where(sub8 == head, grow8, 0.0), axis=0, keepdims=True)[:, sl]
                incl = (cc <= rr) if fwd else (cc >= rr)
                strict = (cc < rr) if fwd else (cc > rr)
                decay = jnp.where(incl, jnp.exp(gcol - grow), 0.0)
                m = jnp.where(strict, beta * kk * decay, 0.0)
                pw = jnp.where(same_blk[0], -m, 0.0)
                tinv = eye + pw
                for _ in range(2):
                    pw = _dot3(pw, pw)
                    tinv = tinv + _dot3(tinv, pw)
                for lvl in range(1, len(same_blk)):
                    off = jnp.where(jnp.logical_and(same_blk[lvl], jnp.logical_not(same_blk[lvl - 1])), m, 0.0)
                    tinv = tinv - _dot3(tinv, _dot3(off, tinv))
                u_s[d, out_rows, :] = _dot3(tinv, vc * beta)
                w_s[d, out_rows, :] = _dot3(tinv, kc * (beta * jnp.exp(gcol)))
                att_s[d, out_rows, :] = qk * decay
                glast = gcol[GDN_CHUNK - 1:GDN_CHUNK] if fwd else gcol[0:1]
                qd_s[d, out_rows, :] = qc * jnp.exp(gcol)
                kd_s[d, out_rows, :] = kc * jnp.exp(glast - gcol)
        return carry

    lax.fori_loop(0, UNIT // GDN_PAIR, phase1, 0)

    lane1 = lax.broadcasted_iota(jnp.int32, (1, LANES), 1)

    def step(d, ch, s):
        fwd = d == 0
        rows = pl.ds(pl.multiple_of(ch * GDN_CHUNK, GDN_CHUNK), GDN_CHUNK)
        first = (ch % chunks_per_seq) == (0 if fwd else chunks_per_seq - 1)
        s = jnp.where(jnp.logical_and(is_ctx, first), 0.0, s)
        v_new = u_s[d, rows, :] - _dot3(w_s[d, rows, :], s)
        o = _dot3(qd_s[d, rows, :], s) + _dot1(att_s[d, rows, :], v_new)
        tile0 = pl.multiple_of(ch * GDN_CHUNK + (GDN_CHUNK - 8 if fwd else 0), 8)
        last = col_ref[pl.ds(tile0, 8), :][7:8] if fwd else col_ref[pl.ds(tile0, 8), :][0:1]
        glast = jnp.sum(jnp.where(lane1 == d * HEADS + head, last, 0.0), axis=-1, keepdims=True)
        s = s * jnp.exp(glast) + _dot3(kd_s[d, rows, :], v_new, _TN)
        s_ref[ch // chunks_per_seq, d] = s
        if fwd:
            acc_ref[rows, :] = o
        return s, o, rows

    def phase2a(i, s):
        s, _, _ = step(0, i, s)
        return s

    def phase2b(i, s):
        s, o, rows = step(1, n_chunks - 1 - i, s)
        acc_ref[rows, :] += o
        return s

    lax.fori_loop(0, n_chunks, phase2a, s0_ref[0])
    lax.fori_loop(0, n_chunks, phase2b, s0_ref[1])
    o_ref[...] = _head_rmsnorm_gate(acc_ref[...], g_ref[...], og_ref[...]).astype(o_ref.dtype)


def _gdn_call(z, col, rowt, conv_w, onorm_g, state, layer):
    def zcol(off):
        return pl.BlockSpec((UNIT, HEAD_DIM), lambda u, h: (u, off // HEAD_DIM + h))
    def wcol(part):
        return pl.BlockSpec((CONV_K * CONV_K, HEAD_DIM), lambda u, h: (0, part * HEADS + h))
    scr = lambda *shape: pltpu.VMEM(shape, F32)
    return pl.pallas_call(
        _gdn_kernel,
        grid=(N_UNITS, HEADS),
        in_specs=[zcol(OFF_GQ), zcol(OFF_GK), zcol(OFF_GV), zcol(OFF_GG),
                  pl.BlockSpec((UNIT, LANES), lambda u, h: (u, 0)),
                  pl.BlockSpec((LANES, UNIT), lambda u, h: (0, u)),
                  wcol(0), wcol(1), wcol(2),
                  pl.BlockSpec((1, HEAD_DIM), lambda u, h: (0, 0)),
                  pl.BlockSpec((None, None, 2, None, HEAD_DIM, HEAD_DIM),
                               lambda u, h: (jnp.maximum(u - N_CTX_UNITS, 0), layer, 0, h, 0, 0))],
        out_specs=[pl.BlockSpec((UNIT, HEAD_DIM), lambda u, h: (u, h)),
                   pl.BlockSpec((None, SEQ_PER_UNIT, 2, None, HEAD_DIM, HEAD_DIM),
                                lambda u, h: (u, 0, 0, h, 0, 0))],
        out_shape=[jax.ShapeDtypeStruct((N_TOK, HG_F), BF16),
                   jax.ShapeDtypeStruct((N_UNITS, SEQ_PER_UNIT, 2, HEADS, HEAD_DIM, HEAD_DIM), F32)],
        scratch_shapes=[scr(UNIT + 2 * CONV_PAD, HEAD_DIM),
                        scr(UNIT, HEAD_DIM), scr(UNIT, HEAD_DIM), scr(UNIT, HEAD_DIM),
                        scr(2, UNIT, HEAD_DIM), scr(2, UNIT, HEAD_DIM),
                        scr(2, UNIT, HEAD_DIM), scr(2, UNIT, HEAD_DIM),
                        scr(2, UNIT, GDN_CHUNK), scr(UNIT, HEAD_DIM)],
        compiler_params=_cparams(("arbitrary", "arbitrary")),
        name="gdn_scan",
    )(z, z, z, z, col, rowt, conv_w, conv_w, conv_w, onorm_g.reshape(1, HEAD_DIM), state)


def kernel(x_prompt, x_sample, c, state_hgrn, state_gdn, c_ctx, norm1_g, norm2_g, w_mod, b_mod, w_in, hg_lb, hg_onorm_g, cm_vnorm_g, cm_ws, cm_bs, gdn_conv, gdn_A_log, gdn_dt_bias, gdn_onorm_g, w_br_hg, w_br_cm, w_br_gdn, w_out, w_ff1, w_ff2, final_g):
    x = jnp.concatenate([x_prompt.reshape(N_CTX_TOK, D_MODEL), x_sample.reshape(N_LAT_TOK, D_MODEL)], axis=0)
    cvec = jnp.concatenate([c_ctx[None, :], c, jnp.zeros((MOD_ROWS - 1 - DEC_BATCH, D_MODEL), F32)], axis=0)
    mod3 = _mod_call(cvec, w_mod, b_mod).reshape(DEPTH * MOD_ROWS, 1, 6 * D_MODEL)

    lb_all = jnp.cumsum(jax.nn.softmax(hg_lb.astype(F32), axis=0), axis=0)
    lb_all = lb_all - lb_all[:1]

    new_hg, new_gdn = [], []
    for l in range(DEPTH):
        w_main = w_in[l, :, :N_MAIN].astype(BF16)
        w_ab = jnp.pad(w_in[l, :, OFF_AB:OFF_GATES], ((0, 0), (0, LANES - 4 * HEADS))).astype(BF16)
        w_gates = w_in[l, :, OFF_GATES:].astype(BF16)

        h = _normmod_call(x, norm1_g[l], mod3, l, 0, 1)
        z = _mm_call(h, w_main, act=None, out_dtype=F32, tm=1024, tn=1024, name="in_proj")
        ab = _mm_call(h, w_ab, act=None, out_dtype=F32, tm=1024, tn=LANES, name="in_proj_ab")
        gates = _mm_call(h, w_gates, act="sigmoid", out_dtype=F32, tm=1024, tn=1024, name="in_proj_gates")

        o_a, s_hg = _hgrn_call(z, lb_all[l], hg_onorm_g[l], state_hgrn, l)
        o_b = _gmlp_call(z, cm_vnorm_g[l], cm_ws[l], cm_bs[l].T)
        col, rowt = _gdn_gates_call(ab, gdn_A_log[l], gdn_dt_bias[l])
        o_c, s_gdn = _gdn_call(z, col, rowt, gdn_conv[l].reshape(CONV_K * CONV_K, 3 * HG_F),
                               gdn_onorm_g[l], state_gdn, l)
        new_hg.append(s_hg[:N_CTX_UNITS].reshape(BATCH, 2, HEADS, HEAD_DIM, HEAD_DIM))
        new_gdn.append(s_gdn[:N_CTX_UNITS].reshape(BATCH, 2, HEADS, HEAD_DIM, HEAD_DIM))

        merged = _merge_call(o_a, o_b, o_c, w_br_hg[l].astype(BF16), w_br_cm[l].astype(BF16),
                             w_br_gdn[l].astype(BF16), gates)
        x = _mm_resid_call(merged, w_out[l].astype(BF16), x, mod3, l, 2, tm=1024, tn=1024, name="out_proj")

        h2 = _normmod_call(x, norm2_g[l], mod3, l, 3, 4)
        up = _mm_call(h2, w_ff1[l].astype(BF16), act="relu2", out_dtype=BF16, tm=1024, tn=1024, name="ffn_up")
        x = _mm_resid_call(up, w_ff2[l].astype(BF16), x, mod3, l, 5, tm=512, tn=512, name="ffn_down")

    y = _final_norm_call(x, final_g)
    y_prompt = y[:N_CTX_TOK].reshape(BATCH, SEQ, D_MODEL)
    y_sample = y[N_CTX_TOK:].reshape(DEC_BATCH, DEC_SEQ, D_MODEL)
    return (y_prompt, y_sample, jnp.stack(new_hg, axis=1), jnp.stack(new_gdn, axis=1))
```

```python
import functools

import jax
import jax.numpy as jnp
from jax import lax
from jax.experimental import pallas as pl
from jax.experimental.pallas import tpu as pltpu

F32 = jnp.float32
BF16 = jnp.bfloat16

D_MODEL = 2048
BATCH = 16
SEQ = 256
DEPTH = 2
DEC_BATCH = 4
DEC_SEQ = 1024
GRID_W = 64
EPS = 1e-6
D_FF = 4 * D_MODEL
HEADS = 8
HEAD_DIM = 128
HG_F = HEADS * HEAD_DIM
CM_GROUPS = 8
CM_W = CM_GROUPS * HEAD_DIM
CM_CHUNK = 128
GDN_CHUNK = 64
CONV_K = 3

OFF_HQ, OFF_HI, OFF_HG, OFF_HFF, OFF_HFB = 0, 1024, 2048, 3072, 4096
OFF_CU, OFF_CV = 5120, 6144
OFF_GQ, OFF_GK, OFF_GV, OFF_GG = 7168, 8192, 9216, 10240
OFF_AB = 11264
OFF_GATES = 11296
IN_DIM = 17440
N_MAIN = OFF_AB

N_CTX_TOK = BATCH * SEQ
N_LAT_TOK = DEC_BATCH * DEC_SEQ
N_TOK = N_CTX_TOK + N_LAT_TOK
UNIT = DEC_SEQ
N_CTX_UNITS = N_CTX_TOK // UNIT
N_UNITS = N_TOK // UNIT
SEQ_PER_UNIT = UNIT // SEQ

LANES = 128
MOD_ROWS = 8
HG_BLOCK = 256
HG_SUB = 32
VMEM_LIMIT = 56 * 1024 * 1024

_NT = (((1,), (1,)), ((), ()))
_TN = (((0,), (0,)), ((), ()))
_NN = (((1,), (0,)), ((), ()))


def _dg(a, b, dims):
    return lax.dot_general(a, b, dims, preferred_element_type=F32)


def _split2(x):
    hi = x.astype(BF16)
    lo = (x - hi.astype(F32)).astype(BF16)
    return hi, lo


def _dot1(a, b, dims=_NN):
    return _dg(a.astype(BF16), b.astype(BF16), dims)


def _dot3(a, b, dims=_NN):
    ah, al = _split2(a)
    bh, bl = _split2(b)
    return _dg(ah, bh, dims) + (_dg(ah, bl, dims) + _dg(al, bh, dims))


_dot_inv = _dot1


def _dot01(m, x):
    hi = x.astype(BF16)
    r = x - hi.astype(F32)
    mid = r.astype(BF16)
    lo = (r - mid.astype(F32)).astype(BF16)
    return _dg(m, hi, _NN) + (_dg(m, mid, _NN) + _dg(m, lo, _NN))


def _sigmoid(x):
    return 1.0 / (1.0 + jnp.exp(-x))


def _silu(x):
    return x * _sigmoid(x)


def _gelu(x):
    return 0.5 * x * (1.0 + lax.erf(x * (2.0 ** -0.5)))


def _softplus(x):
    return jnp.maximum(x, 0.0) + jnp.log1p(jnp.exp(-jnp.abs(x)))


def _mod_row_of_tile(i, tm):
    return jnp.maximum(0, (i * tm - N_CTX_TOK) // DEC_SEQ + 1)


def _cparams(sem):
    return pltpu.CompilerParams(dimension_semantics=sem, vmem_limit_bytes=VMEM_LIMIT)


def _mod_kernel(c_ref, w_ref, b_ref, o_ref):
    s = _silu(c_ref[...])
    hi, lo = _split2(s)
    w = w_ref[...].astype(BF16)
    o_ref[...] = _dg(hi, w, _NN) + _dg(lo, w, _NN) + b_ref[...]


def _mod_call(cvec, w_mod, b_mod):
    tn = 1024
    n = 6 * D_MODEL
    return pl.pallas_call(
        _mod_kernel,
        grid=(DEPTH, n // tn),
        in_specs=[pl.BlockSpec((MOD_ROWS, D_MODEL), lambda l, j: (0, 0)),
                  pl.BlockSpec((None, D_MODEL, tn), lambda l, j: (l, 0, j)),
                  pl.BlockSpec((None, 1, tn), lambda l, j: (l, 0, j))],
        out_specs=pl.BlockSpec((None, MOD_ROWS, tn), lambda l, j: (l, 0, j)),
        out_shape=jax.ShapeDtypeStruct((DEPTH, MOD_ROWS, n), F32),
        compiler_params=_cparams(("arbitrary", "arbitrary")),
        name="modulation",
    )(cvec, w_mod, b_mod.reshape(DEPTH, 1, n))


def _normmod_kernel(x_ref, g_ref, sh_ref, sc_ref, o_ref):
    x = x_ref[...]
    y = x * lax.rsqrt(jnp.mean(x * x, axis=-1, keepdims=True) + EPS) * g_ref[...]
    o_ref[...] = (y * (1.0 + sc_ref[...]) + sh_ref[...]).astype(o_ref.dtype)


def _normmod_call(x, g, mod3, layer, k_shift, k_scale):
    tm = 256
    def mod_spec(k):
        return pl.BlockSpec((None, 1, D_MODEL),
                            lambda i: (layer * MOD_ROWS + _mod_row_of_tile(i, tm), 0, k))
    return pl.pallas_call(
        _normmod_kernel,
        grid=(N_TOK // tm,),
        in_specs=[pl.BlockSpec((tm, D_MODEL), lambda i: (i, 0)),
                  pl.BlockSpec((1, D_MODEL), lambda i: (0, 0)),
                  mod_spec(k_shift), mod_spec(k_scale)],
        out_specs=pl.BlockSpec((tm, D_MODEL), lambda i: (i, 0)),
        out_shape=jax.ShapeDtypeStruct((N_TOK, D_MODEL), BF16),
        compiler_params=_cparams(("arbitrary",)),
        name="norm_mod",
    )(x, g.reshape(1, D_MODEL), mod3, mod3)


def _final_norm_kernel(x_ref, g_ref, o_ref):
    x = x_ref[...]
    o_ref[...] = x * lax.rsqrt(jnp.mean(x * x, axis=-1, keepdims=True) + EPS) * g_ref[...]


def _final_norm_call(x, g):
    tm = 256
    return pl.pallas_call(
        _final_norm_kernel,
        grid=(N_TOK // tm,),
        in_specs=[pl.BlockSpec((tm, D_MODEL), lambda i: (i, 0)),
                  pl.BlockSpec((1, D_MODEL), lambda i: (0, 0))],
        out_specs=pl.BlockSpec((tm, D_MODEL), lambda i: (i, 0)),
        out_shape=jax.ShapeDtypeStruct((N_TOK, D_MODEL), F32),
        compiler_params=_cparams(("arbitrary",)),
        name="final_norm",
    )(x, g.reshape(1, D_MODEL))


def _mm_kernel(x_ref, w_ref, o_ref, *, act):
    acc = jnp.dot(x_ref[...], w_ref[...], preferred_element_type=F32)
    if act == "sigmoid":
        acc = _sigmoid(acc)
    elif act == "relu2":
        acc = jnp.square(jnp.maximum(acc, 0.0))
    o_ref[...] = acc.astype(o_ref.dtype)


def _mm_call(x, w, *, act, out_dtype, tm, tn, name):
    m, k = x.shape
    n = w.shape[1]
    return pl.pallas_call(
        functools.partial(_mm_kernel, act=act),
        grid=(m // tm, pl.cdiv(n, tn)),
        in_specs=[pl.BlockSpec((tm, k), lambda i, j: (i, 0)),
                  pl.BlockSpec((k, tn), lambda i, j: (0, j))],
        out_specs=pl.BlockSpec((tm, tn), lambda i, j: (i, j)),
        out_shape=jax.ShapeDtypeStruct((m, n), out_dtype),
        compiler_params=_cparams(("arbitrary", "arbitrary")),
        name=name,
    )(x, w)


def _mm_resid_kernel(x_ref, w_ref, r_ref, g_ref, o_ref):
    acc = jnp.dot(x_ref[...], w_ref[...], preferred_element_type=F32)
    o_ref[...] = r_ref[...] + g_ref[...] * acc


def _mm_resid_call(x, w, resid, mod3, layer, k_gate, *, tm, tn, name):
    m, k = x.shape
    n = w.shape[1]
    per_tile = D_MODEL // tn
    return pl.pallas_call(
        _mm_resid_kernel,
        grid=(m // tm, n // tn),
        in_specs=[pl.BlockSpec((tm, k), lambda i, j: (i, 0)),
                  pl.BlockSpec((k, tn), lambda i, j: (0, j)),
                  pl.BlockSpec((tm, tn), lambda i, j: (i, j)),
                  pl.BlockSpec((None, 1, tn),
                               lambda i, j: (layer * MOD_ROWS + _mod_row_of_tile(i, tm), 0,
                                             k_gate * per_tile + j))],
        out_specs=pl.BlockSpec((tm, tn), lambda i, j: (i, j)),
        out_shape=jax.ShapeDtypeStruct((m, n), F32),
        compiler_params=_cparams(("arbitrary", "arbitrary")),
        name=name,
    )(x, w, resid, mod3)


def _merge_kernel(oa_ref, ob_ref, oc_ref, wa_ref, wb_ref, wc_ref, ga_ref, gb_ref, gc_ref, o_ref):
    a = jnp.dot(oa_ref[...], wa_ref[...], preferred_element_type=F32)
    b = jnp.dot(ob_ref[...], wb_ref[...], preferred_element_type=F32)
    c = jnp.dot(oc_ref[...], wc_ref[...], preferred_element_type=F32)
    o_ref[...] = (ga_ref[...] * a + gb_ref[...] * b + gc_ref[...] * c).astype(o_ref.dtype)


def _merge_call(o_a, o_b, o_c, w_a, w_b, w_c, gates):
    tm, tn = 1024, 512
    nj = D_MODEL // tn
    br = pl.BlockSpec((tm, HG_F), lambda i, j: (i, 0))
    wt = pl.BlockSpec((HG_F, tn), lambda i, j: (0, j))
    def gate_spec(k):
        return pl.BlockSpec((tm, tn), lambda i, j: (i, k * nj + j))
    return pl.pallas_call(
        _merge_kernel,
        grid=(N_TOK // tm, nj),
        in_specs=[br, br, br, wt, wt, wt, gate_spec(0), gate_spec(1), gate_spec(2)],
        out_specs=pl.BlockSpec((tm, tn), lambda i, j: (i, j)),
        out_shape=jax.ShapeDtypeStruct((N_TOK, D_MODEL), BF16),
        compiler_params=_cparams(("arbitrary", "arbitrary")),
        name="branch_merge",
    )(o_a, o_b, o_c, w_a, w_b, w_c, gates, gates, gates)


def _head_rmsnorm_gate(o, g, og):
    y = o * lax.rsqrt(jnp.mean(o * o, axis=-1, keepdims=True) + EPS) * g
    return y * _silu(og)


def _hgrn_kernel(q_ref, i_ref, og_ref, ff_ref, fb_ref, lb_ref, g_ref, s0_ref, o_ref, s_ref,
                 acc_ref, b_ref, st_ref, stc_ref, km_ref):
    is_ctx = pl.program_id(0) < N_CTX_UNITS
    nblk = UNIT // HG_BLOCK
    nsub = HG_BLOCK // HG_SUB
    r = lax.broadcasted_iota(jnp.int32, (HG_BLOCK, HG_BLOCK), 0)
    c = lax.broadcasted_iota(jnp.int32, (HG_BLOCK, HG_BLOCK), 1)
    same = (r // HG_SUB) == (c // HG_SUB)
    row_sub = lax.broadcasted_iota(jnp.int32, (HG_BLOCK, HEAD_DIM), 0) // HG_SUB

    def expand(x):
        return jnp.broadcast_to(x[:, None, :], (nsub, HG_SUB, HEAD_DIM)).reshape(HG_BLOCK, HEAD_DIM)

    def run_dir(z_ref, d, fwd):
        tri = jnp.logical_and(same, (c <= r) if fwd else (c >= r))
        tri_bf = jnp.where(tri, 1.0, 0.0).astype(BF16)
        lb = lb_ref[d:d + 1, :]
        st_ref[...] = s0_ref[d].T

        def body(j, carry):
            blk = j if fwd else nblk - 1 - j
            rows = pl.ds(pl.multiple_of(blk * HG_BLOCK, HG_BLOCK), HG_BLOCK)
            q = _silu(q_ref[rows, :])
            v = i_ref[rows, :]
            f = lb + (1.0 - lb) * _sigmoid(z_ref[rows, :])
            lf = jnp.log(f)
            k = 1.0 - f
            b = _dot01(tri_bf, lf)
            b_ref[...] = b
            tot = b_ref[pl.ds((HG_SUB - 1) if fwd else 0, nsub, stride=HG_SUB), :]
            mid = b_ref[pl.ds(HG_SUB // 2, nsub, stride=HG_SUB), :]
            tot_f = expand(tot)
            mid_f = expand(mid)
            qt = q * jnp.exp(b - mid_f)
            kt = k * jnp.exp(mid_f - b)
            sc = jnp.where(tri, _dot3(qt, kt, _NT), 0.0)
            o = _dot1(sc, v)
            qd = (q * jnp.exp(b)).astype(BF16)
            kd = k * jnp.exp(tot_f - b)
            for s in range(nsub):
                km_ref[:, s * HEAD_DIM:(s + 1) * HEAD_DIM] = jnp.where(row_sub == s, kd, 0.0).astype(BF16)
            ut = _dg(v.astype(BF16), km_ref[...], _TN)
            st = jnp.where(is_ctx, 0.0, st_ref[...])
            for s in (range(nsub) if fwd else range(nsub - 1, -1, -1)):
                stc_ref[s] = st
                st = st * jnp.exp(tot[s:s + 1, :]) + ut[:, s * HEAD_DIM:(s + 1) * HEAD_DIM]
            st_ref[...] = st
            s_ref[blk, d] = st.T
            o_int = [_dg(qd[s * HG_SUB:(s + 1) * HG_SUB], stc_ref[s].astype(BF16), _NT)
                     for s in range(nsub)]
            o = o + jnp.concatenate(o_int, axis=0)
            if fwd:
                acc_ref[rows, :] = o
            else:
                acc_ref[rows, :] += o
            return carry

        lax.fori_loop(0, nblk, body, 0)

    run_dir(ff_ref, 0, True)
    run_dir(fb_ref, 1, False)
    o_ref[...] = _head_rmsnorm_gate(acc_ref[...], g_ref[...], og_ref[...]).astype(o_ref.dtype)


def _hgrn_call(z, lb, onorm_g, state, layer):
    def col(off):
        return pl.BlockSpec((UNIT, HEAD_DIM), lambda u, h: (u, off // HEAD_DIM + h))
    return pl.pallas_call(
        _hgrn_kernel,
        grid=(N_UNITS, HEADS),
        in_specs=[col(OFF_HQ), col(OFF_HI), col(OFF_HG), col(OFF_HFF), col(OFF_HFB),
                  pl.BlockSpec((2, HEAD_DIM), lambda u, h: (0, h)),
                  pl.BlockSpec((1, HEAD_DIM), lambda u, h: (0, 0)),
                  pl.BlockSpec((None, None, 2, None, HEAD_DIM, HEAD_DIM),
                               lambda u, h: (jnp.maximum(u - N_CTX_UNITS, 0), layer, 0, h, 0, 0))],
        out_specs=[pl.BlockSpec((UNIT, HEAD_DIM), lambda u, h: (u, h)),
                   pl.BlockSpec((None, SEQ_PER_UNIT, 2, None, HEAD_DIM, HEAD_DIM),
                                lambda u, h: (u, 0, 0, h, 0, 0))],
        out_shape=[jax.ShapeDtypeStruct((N_TOK, HG_F), BF16),
                   jax.ShapeDtypeStruct((N_UNITS, SEQ_PER_UNIT, 2, HEADS, HEAD_DIM, HEAD_DIM), F32)],
        scratch_shapes=[pltpu.VMEM((UNIT, HEAD_DIM), F32),
                        pltpu.VMEM((HG_BLOCK, HEAD_DIM), F32),
                        pltpu.VMEM((HEAD_DIM, HEAD_DIM), F32),
                        pltpu.VMEM((HG_BLOCK // HG_SUB, HEAD_DIM, HEAD_DIM), F32),
                        pltpu.VMEM((HG_BLOCK, (HG_BLOCK // HG_SUB) * HEAD_DIM), BF16)],
        compiler_params=_cparams(("arbitrary", "arbitrary")),
        name="hgrn2_scan",
    )(z, z, z, z, z, lb, onorm_g.reshape(1, HEAD_DIM), state)


def _gmlp_kernel(u_ref, v_ref, vn_ref, ws_ref, bs_ref, o_ref):
    tm = u_ref.shape[0]
    for g in range(CM_GROUPS):
        cols = slice(g * HEAD_DIM, (g + 1) * HEAD_DIM)
        vg = _gelu(v_ref[:, cols])
        vg = vg * lax.rsqrt(jnp.mean(vg * vg, axis=-1, keepdims=True) + EPS) * vn_ref[:, cols]
        ug = _gelu(u_ref[:, cols])
        w = ws_ref[g]
        bias = bs_ref[:, g:g + 1]
        for ch in range(tm // CM_CHUNK):
            rows = slice(ch * CM_CHUNK, (ch + 1) * CM_CHUNK)
            s = _dot3(w, vg[rows]) + bias
            o_ref[rows, cols] = (ug[rows] * s).astype(o_ref.dtype)


def _gmlp_call(z, vnorm_g, ws, bs_t):
    tm = 512
    return pl.pallas_call(
        _gmlp_kernel,
        grid=(N_TOK // tm,),
        in_specs=[pl.BlockSpec((tm, CM_W), lambda i: (i, OFF_CU // CM_W)),
                  pl.BlockSpec((tm, CM_W), lambda i: (i, OFF_CV // CM_W)),
                  pl.BlockSpec((1, CM_W), lambda i: (0, 0)),
                  pl.BlockSpec((CM_GROUPS, CM_CHUNK, CM_CHUNK), lambda i: (0, 0, 0)),
                  pl.BlockSpec((CM_CHUNK, CM_GROUPS), lambda i: (0, 0))],
        out_specs=pl.BlockSpec((tm, CM_W), lambda i: (i, 0)),
        out_shape=jax.ShapeDtypeStruct((N_TOK, CM_W), BF16),
        compiler_params=_cparams(("arbitrary",)),
        name="chunk_gmlp",
    )(z, z, vnorm_g.reshape(1, CM_W), ws, bs_t)


GP_BLOCK = 256


def _gdn_gates_kernel(ab_ref, alog_ref, dt_ref, col_ref, rowt_ref):
    ab = ab_ref[...]
    lane = lax.broadcasted_iota(jnp.int32, ab.shape, 1)
    g = jnp.where(lane < 2 * HEADS, -jnp.exp(alog_ref[...]) * _softplus(ab + dt_ref[...]), 0.0)
    r = lax.broadcasted_iota(jnp.int32, (GP_BLOCK, GP_BLOCK), 0)
    c = lax.broadcasted_iota(jnp.int32, (GP_BLOCK, GP_BLOCK), 1)
    same = (r // GDN_CHUNK) == (c // GDN_CHUNK)
    tri_f = jnp.where(jnp.logical_and(same, c <= r), 1.0, 0.0).astype(BF16)
    tri_b = jnp.where(jnp.logical_and(same, c >= r), 1.0, 0.0).astype(BF16)
    cf = _dot01(tri_f, g)
    cb = _dot01(tri_b, g)
    col = jnp.where(lane < HEADS, cf, jnp.where(lane < 2 * HEADS, cb, _sigmoid(ab)))
    col_ref[...] = col
    rowt_ref[...] = col.T


def _gdn_gates_call(ab, a_log, dt_bias):
    pad = lambda t: jnp.pad(t.reshape(1, 2 * HEADS), ((0, 0), (0, LANES - 2 * HEADS)))
    return pl.pallas_call(
        _gdn_gates_kernel,
        grid=(N_TOK // GP_BLOCK,),
        in_specs=[pl.BlockSpec((GP_BLOCK, LANES), lambda i: (i, 0)),
                  pl.BlockSpec((1, LANES), lambda i: (0, 0)),
                  pl.BlockSpec((1, LANES), lambda i: (0, 0))],
        out_specs=[pl.BlockSpec((GP_BLOCK, LANES), lambda i: (i, 0)),
                   pl.BlockSpec((LANES, GP_BLOCK), lambda i: (0, i))],
        out_shape=[jax.ShapeDtypeStruct((N_TOK, LANES), F32),
                   jax.ShapeDtypeStruct((LANES, N_TOK), F32)],
        compiler_params=_cparams(("arbitrary",)),
        name="gdn_gates",
    )(ab, pad(a_log), pad(dt_bias))


CONV_PAD = 72
GDN_PAIR = 2 * GDN_CHUNK


def _gdn_kernel(qr_ref, kr_ref, vr_ref, og_ref, col_ref, rowt_ref, cwq_ref, cwk_ref, cwv_ref,
                g_ref, s0_ref, o_ref, s_ref,
                xp_ref, q_s, k_s, v_s, u_s, wq_s, kd_s, att_s, acc_ref, accb_ref):
    is_ctx = pl.program_id(0) < N_CTX_UNITS
    head = pl.program_id(1)
    n_chunks = UNIT // GDN_CHUNK
    chunks_per_seq = SEQ // GDN_CHUNK

    t = lax.broadcasted_iota(jnp.int32, (UNIT, 1), 0)
    period = jnp.where(is_ctx, SEQ, GRID_W)
    pos = jnp.bitwise_and(t, period - 1)
    ok_left = pos != 0
    ok_right = pos != period - 1
    zeros_pad = jnp.zeros((CONV_PAD, HEAD_DIM), F32)
    xp_ref[0:CONV_PAD, :] = zeros_pad
    xp_ref[CONV_PAD + UNIT:CONV_PAD + UNIT + CONV_PAD, :] = zeros_pad

    def conv_silu(x_ref, w_ref):
        xp_ref[CONV_PAD:CONV_PAD + UNIT, :] = x_ref[...]
        acc = jnp.zeros((UNIT, HEAD_DIM), F32)
        for i in range(CONV_K):
            for j in range(CONV_K):
                w = w_ref[CONV_K * i + j:CONV_K * i + j + 1, :]
                if i != CONV_K // 2:
                    w = jnp.where(is_ctx, 0.0, w)
                start = CONV_PAD + (i - 1) * GRID_W + (j - 1)
                xs = xp_ref[start:start + UNIT, :]
                if j == 0:
                    xs = jnp.where(ok_left, xs, 0.0)
                elif j == CONV_K - 1:
                    xs = jnp.where(ok_right, xs, 0.0)
                acc = acc + xs * w
        return _silu(acc)

    def l2norm(x):
        return x * lax.rsqrt(jnp.sum(x * x, axis=-1, keepdims=True) + EPS)

    q_s[...] = l2norm(conv_silu(qr_ref, cwq_ref)) * (HEAD_DIM ** -0.5)
    k_s[...] = l2norm(conv_silu(kr_ref, cwk_ref))
    v_s[...] = conv_silu(vr_ref, cwv_ref)

    rr = lax.broadcasted_iota(jnp.int32, (GDN_CHUNK, GDN_CHUNK), 0)
    cc = lax.broadcasted_iota(jnp.int32, (GDN_CHUNK, GDN_CHUNK), 1)
    eye = jnp.where(rr == cc, 1.0, 0.0)
    same_blk = [(rr // b) == (cc // b) for b in (8, 16, 32, 64)]
    lane = lax.broadcasted_iota(jnp.int32, (GDN_PAIR, LANES), 1)
    sub8 = lax.broadcasted_iota(jnp.int32, (HEADS, GDN_PAIR), 0)

    def pick(x, j):
        return jnp.sum(jnp.where(lane[:x.shape[0]] == j, x, 0.0), axis=-1, keepdims=True)

    incl = (cc <= rr, cc >= rr)
    strict = (cc < rr, cc > rr)
    off_masks = [jnp.logical_and(same_blk[lvl], jnp.logical_not(same_blk[lvl - 1]))
                 for lvl in range(1, len(same_blk))]

    def phase1(p, carry):
        rows = pl.ds(pl.multiple_of(p * GDN_PAIR, GDN_PAIR), GDN_PAIR)
        q2, k2, v2 = q_s[rows, :], k_s[rows, :], v_s[rows, :]
        col = col_ref[rows, :]
        gcols = [pick(col, d * HEADS + head) for d in range(2)]
        betas = [pick(col, (2 + d) * HEADS + head) for d in range(2)]
        grows = [jnp.sum(jnp.where(sub8 == head, rowt_ref[d * HEADS:(d + 1) * HEADS, rows], 0.0),
                         axis=0, keepdims=True) for d in range(2)]
        inst = []
        for half in range(2):
            sl = slice(half * GDN_CHUNK, (half + 1) * GDN_CHUNK)
            qc, kc, vc = q2[sl], k2[sl], v2[sl]
            kk = _dot1(kc, kc, _NT)
            qk = _dot1(qc, kc, _NT)
            for d in range(2):
                gcol, beta = gcols[d][sl], betas[d][sl]
                decay = jnp.where(incl[d], jnp.exp(gcol - grows[d][:, sl]), 0.0)
                m = jnp.where(strict[d], beta * kk * decay, 0.0)
                inst.append((half, d, qc, kc, vc, qk, gcol, beta, decay, m))
        ms = [t[-1] for t in inst]
        pws = [jnp.where(same_blk[0], -m, 0.0) for m in ms]
        tinvs = [eye + pw for pw in pws]
        for _ in range(2):
            pws = [_dot_inv(pw, pw) for pw in pws]
            tinvs = [t + _dot_inv(t, pw) for t, pw in zip(tinvs, pws)]
        for mask in off_masks:
            tmp = [_dot_inv(jnp.where(mask, m, 0.0), t) for m, t in zip(ms, tinvs)]
            tinvs = [t - _dot_inv(t, x) for t, x in zip(tinvs, tmp)]
        for (half, d, qc, kc, vc, qk, gcol, beta, decay, m), tinv in zip(inst, tinvs):
            chunk0 = p * GDN_PAIR + half * GDN_CHUNK
            out_rows = pl.ds(pl.multiple_of(chunk0, GDN_CHUNK), GDN_CHUNK)
            eg = jnp.exp(gcol)
            uw = _dot1(tinv, jnp.concatenate([vc * beta, kc * (beta * eg)], axis=1))
            u_s[d, out_rows, :] = uw[:, :HEAD_DIM]
            glast = gcol[GDN_CHUNK - 1:GDN_CHUNK] if d == 0 else gcol[0:1]
            wq_s[d, pl.ds(pl.multiple_of(2 * chunk0, GDN_PAIR), GDN_PAIR), :] = jnp.concatenate(
                [uw[:, HEAD_DIM:], qc * eg], axis=0).astype(BF16)
            att_s[d, out_rows, :] = (qk * decay).astype(BF16)
            kd_s[d, out_rows, :] = (kc * jnp.exp(glast - gcol)).astype(BF16)
        return carry

    lax.fori_loop(0, UNIT // GDN_PAIR, phase1, 0)

    lane1 = lax.broadcasted_iota(jnp.int32, (1, LANES), 1)

    def phase2(i, carry):
        chs = (i, n_chunks - 1 - i)
        rows = [pl.ds(pl.multiple_of(ch * GDN_CHUNK, GDN_CHUNK), GDN_CHUNK) for ch in chs]
        firsts = [(chs[0] % chunks_per_seq) == 0, (chs[1] % chunks_per_seq) == chunks_per_seq - 1]
        ss = [jnp.where(jnp.logical_and(is_ctx, f), 0.0, s) for f, s in zip(firsts, carry)]
        ws = [_dg(wq_s[d, pl.ds(pl.multiple_of(2 * chs[d] * GDN_CHUNK, GDN_PAIR), GDN_PAIR), :],
                  ss[d].astype(BF16), _NN) for d in range(2)]
        v_new = [u_s[d, rows[d], :] - ws[d][:GDN_CHUNK] for d in range(2)]
        v_bf = [v.astype(BF16) for v in v_new]
        outs = [ws[d][GDN_CHUNK:] + _dg(att_s[d, rows[d], :], v_bf[d], _NN) for d in range(2)]
        new = []
        for d in range(2):
            tile0 = pl.multiple_of(chs[d] * GDN_CHUNK + (GDN_CHUNK - 8 if d == 0 else 0), 8)
            last = col_ref[pl.ds(tile0, 8), :]
            last = last[7:8] if d == 0 else last[0:1]
            glast = jnp.sum(jnp.where(lane1 == d * HEADS + head, last, 0.0), axis=-1, keepdims=True)
            s = ss[d] * jnp.exp(glast) + _dg(kd_s[d, rows[d], :], v_bf[d], _TN)
            s_ref[chs[d] // chunks_per_seq, d] = s
            new.append(s)
        acc_ref[rows[0], :] = outs[0]
        accb_ref[rows[1], :] = outs[1]
        return tuple(new)

    lax.fori_loop(0, n_chunks, phase2, (s0_ref[0], s0_ref[1]))
    o_ref[...] = _head_rmsnorm_gate(acc_ref[...] + accb_ref[...], g_ref[...], og_ref[...]).astype(o_ref.dtype)


def _gdn_call(z, col, rowt, conv_w, onorm_g, state, layer):
    def zcol(off):
        return pl.BlockSpec((UNIT, HEAD_DIM), lambda u, h: (u, off // HEAD_DIM + h))
    def wcol(part):
        return pl.BlockSpec((CONV_K * CONV_K, HEAD_DIM), lambda u, h: (0, part * HEADS + h))
    scr = lambda *shape: pltpu.VMEM(shape, F32)
    return pl.pallas_call(
        _gdn_kernel,
        grid=(N_UNITS, HEADS),
        in_specs=[zcol(OFF_GQ), zcol(OFF_GK), zcol(OFF_GV), zcol(OFF_GG),
                  pl.BlockSpec((UNIT, LANES), lambda u, h: (u, 0)),
                  pl.BlockSpec((LANES, UNIT), lambda u, h: (0, u)),
                  wcol(0), wcol(1), wcol(2),
                  pl.BlockSpec((1, HEAD_DIM), lambda u, h: (0, 0)),
                  pl.BlockSpec((None, None, 2, None, HEAD_DIM, HEAD_DIM),
                               lambda u, h: (jnp.maximum(u - N_CTX_UNITS, 0), layer, 0, h, 0, 0))],
        out_specs=[pl.BlockSpec((UNIT, HEAD_DIM), lambda u, h: (u, h)),
                   pl.BlockSpec((None, SEQ_PER_UNIT, 2, None, HEAD_DIM, HEAD_DIM),
                                lambda u, h: (u, 0, 0, h, 0, 0))],
        out_shape=[jax.ShapeDtypeStruct((N_TOK, HG_F), BF16),
                   jax.ShapeDtypeStruct((N_UNITS, SEQ_PER_UNIT, 2, HEADS, HEAD_DIM, HEAD_DIM), F32)],
        scratch_shapes=[scr(UNIT + 2 * CONV_PAD, HEAD_DIM),
                        scr(UNIT, HEAD_DIM), scr(UNIT, HEAD_DIM), scr(UNIT, HEAD_DIM),
                        scr(2, UNIT, HEAD_DIM),
                        pltpu.VMEM((2, 2 * UNIT, HEAD_DIM), BF16),
                        pltpu.VMEM((2, UNIT, HEAD_DIM), BF16),
                        pltpu.VMEM((2, UNIT, GDN_CHUNK), BF16),
                        scr(UNIT, HEAD_DIM), scr(UNIT, HEAD_DIM)],
        compiler_params=_cparams(("arbitrary", "arbitrary")),
        name="gdn_scan",
    )(z, z, z, z, col, rowt, conv_w, conv_w, conv_w, onorm_g.reshape(1, HEAD_DIM), state)


def kernel(x_prompt, x_sample, c, state_hgrn, state_gdn, c_ctx, norm1_g, norm2_g, w_mod, b_mod, w_in, hg_lb, hg_onorm_g, cm_vnorm_g, cm_ws, cm_bs, gdn_conv, gdn_A_log, gdn_dt_bias, gdn_onorm_g, w_br_hg, w_br_cm, w_br_gdn, w_out, w_ff1, w_ff2, final_g):
    x = jnp.concatenate([x_prompt.reshape(N_CTX_TOK, D_MODEL), x_sample.reshape(N_LAT_TOK, D_MODEL)], axis=0)
    cvec = jnp.concatenate([c_ctx[None, :], c, jnp.zeros((MOD_ROWS - 1 - DEC_BATCH, D_MODEL), F32)], axis=0)
    mod3 = _mod_call(cvec, w_mod, b_mod).reshape(DEPTH * MOD_ROWS, 1, 6 * D_MODEL)

    lb_all = jnp.cumsum(jax.nn.softmax(hg_lb.astype(F32), axis=0), axis=0)
    lb_all = lb_all - lb_all[:1]

    new_hg, new_gdn = [], []
    for l in range(DEPTH):
        w_main = w_in[l, :, :N_MAIN].astype(BF16)
        w_ab = jnp.pad(w_in[l, :, OFF_AB:OFF_GATES], ((0, 0), (0, LANES - 4 * HEADS))).astype(BF16)
        w_gates = w_in[l, :, OFF_GATES:].astype(BF16)

        h = _normmod_call(x, norm1_g[l], mod3, l, 0, 1)
        z = _mm_call(h, w_main, act=None, out_dtype=F32, tm=1024, tn=1024, name="in_proj")
        ab = _mm_call(h, w_ab, act=None, out_dtype=F32, tm=1024, tn=LANES, name="in_proj_ab")
        gates = _mm_call(h, w_gates, act="sigmoid", out_dtype=F32, tm=1024, tn=1024, name="in_proj_gates")

        o_a, s_hg = _hgrn_call(z, lb_all[l], hg_onorm_g[l], state_hgrn, l)
        o_b = _gmlp_call(z, cm_vnorm_g[l], cm_ws[l], cm_bs[l].T)
        col, rowt = _gdn_gates_call(ab, gdn_A_log[l], gdn_dt_bias[l])
        o_c, s_gdn = _gdn_call(z, col, rowt, gdn_conv[l].reshape(CONV_K * CONV_K, 3 * HG_F),
                               gdn_onorm_g[l], state_gdn, l)
        new_hg.append(s_hg[:N_CTX_UNITS].reshape(BATCH, 2, HEADS, HEAD_DIM, HEAD_DIM))
        new_gdn.append(s_gdn[:N_CTX_UNITS].reshape(BATCH, 2, HEADS, HEAD_DIM, HEAD_DIM))

        merged = _merge_call(o_a, o_b, o_c, w_br_hg[l].astype(BF16), w_br_cm[l].astype(BF16),
                             w_br_gdn[l].astype(BF16), gates)
        x = _mm_resid_call(merged, w_out[l].astype(BF16), x, mod3, l, 2, tm=1024, tn=1024, name="out_proj")

        h2 = _normmod_call(x, norm2_g[l], mod3, l, 3, 4)
        up = _mm_call(h2, w_ff1[l].astype(BF16), act="relu2", out_dtype=BF16, tm=1024, tn=1024, name="ffn_up")
        x = _mm_resid_call(up, w_ff2[l].astype(BF16), x, mod3, l, 5, tm=512, tn=512, name="ffn_down")

    y = _final_norm_call(x, final_g)
    y_prompt = y[:N_CTX_TOK].reshape(BATCH, SEQ, D_MODEL)
    y_sample = y[N_CTX_TOK:].reshape(DEC_BATCH, DEC_SEQ, D_MODEL)
    return (y_prompt, y_sample, jnp.stack(new_hg, axis=1), jnp.stack(new_gdn, axis=1))
```

```python
import functools

import jax
import jax.numpy as jnp
from jax import lax
from jax.experimental import pallas as pl
from jax.experimental.pallas import tpu as pltpu

F32 = jnp.float32
BF16 = jnp.bfloat16

D_MODEL = 2048
BATCH = 16
SEQ = 256
DEPTH = 2
DEC_BATCH = 4
DEC_SEQ = 1024
GRID_W = 64
EPS = 1e-6
D_FF = 4 * D_MODEL
HEADS = 8
HEAD_DIM = 128
HG_F = HEADS * HEAD_DIM
CM_GROUPS = 8
CM_W = CM_GROUPS * HEAD_DIM
CM_CHUNK = 128
GDN_CHUNK = 64
CONV_K = 3

OFF_HQ, OFF_HI, OFF_HG, OFF_HFF, OFF_HFB = 0, 1024, 2048, 3072, 4096
OFF_CU, OFF_CV = 5120, 6144
OFF_GQ, OFF_GK, OFF_GV, OFF_GG = 7168, 8192, 9216, 10240
OFF_AB = 11264
OFF_GATES = 11296
IN_DIM = 17440
N_MAIN = OFF_AB

N_CTX_TOK = BATCH * SEQ
N_LAT_TOK = DEC_BATCH * DEC_SEQ
N_TOK = N_CTX_TOK + N_LAT_TOK
UNIT = DEC_SEQ
N_CTX_UNITS = N_CTX_TOK // UNIT
N_UNITS = N_TOK // UNIT
SEQ_PER_UNIT = UNIT // SEQ

LANES = 128
MOD_ROWS = 8
HG_BLOCK = 256
HG_SUB = 32
HG_INST = 2 * (UNIT // HG_BLOCK)
VMEM_LIMIT = 56 * 1024 * 1024

_NT = (((1,), (1,)), ((), ()))
_TN = (((0,), (0,)), ((), ()))
_NN = (((1,), (0,)), ((), ()))


def _dg(a, b, dims):
    return lax.dot_general(a, b, dims, preferred_element_type=F32)


def _split2(x):
    hi = x.astype(BF16)
    lo = (x - hi.astype(F32)).astype(BF16)
    return hi, lo


def _dot1(a, b, dims=_NN):
    return _dg(a.astype(BF16), b.astype(BF16), dims)


def _dot3(a, b, dims=_NN):
    ah, al = _split2(a)
    bh, bl = _split2(b)
    return _dg(ah, bh, dims) + (_dg(ah, bl, dims) + _dg(al, bh, dims))


_dot_inv = _dot1


def _dot01(m, x):
    hi = x.astype(BF16)
    r = x - hi.astype(F32)
    mid = r.astype(BF16)
    lo = (r - mid.astype(F32)).astype(BF16)
    return _dg(m, hi, _NN) + (_dg(m, mid, _NN) + _dg(m, lo, _NN))


def _sigmoid(x):
    return 1.0 / (1.0 + jnp.exp(-x))


def _silu(x):
    return x * _sigmoid(x)


def _gelu(x):
    return 0.5 * x * (1.0 + lax.erf(x * (2.0 ** -0.5)))


def _softplus(x):
    return jnp.maximum(x, 0.0) + jnp.log1p(jnp.exp(-jnp.abs(x)))


def _mod_row_of_tile(i, tm):
    return jnp.maximum(0, (i * tm - N_CTX_TOK) // DEC_SEQ + 1)


def _cparams(sem):
    return pltpu.CompilerParams(dimension_semantics=sem, vmem_limit_bytes=VMEM_LIMIT)


def _mod_kernel(c_ref, w_ref, b_ref, o_ref):
    s = _silu(c_ref[...])
    hi, lo = _split2(s)
    w = w_ref[...].astype(BF16)
    o_ref[...] = _dg(hi, w, _NN) + _dg(lo, w, _NN) + b_ref[...]


def _mod_call(cvec, w_mod, b_mod):
    tn = 1024
    n = 6 * D_MODEL
    return pl.pallas_call(
        _mod_kernel,
        grid=(DEPTH, n // tn),
        in_specs=[pl.BlockSpec((MOD_ROWS, D_MODEL), lambda l, j: (0, 0)),
                  pl.BlockSpec((None, D_MODEL, tn), lambda l, j: (l, 0, j)),
                  pl.BlockSpec((None, 1, tn), lambda l, j: (l, 0, j))],
        out_specs=pl.BlockSpec((None, MOD_ROWS, tn), lambda l, j: (l, 0, j)),
        out_shape=jax.ShapeDtypeStruct((DEPTH, MOD_ROWS, n), F32),
        compiler_params=_cparams(("arbitrary", "arbitrary")),
        name="modulation",
    )(cvec, w_mod, b_mod.reshape(DEPTH, 1, n))


def _normmod_kernel(x_ref, g_ref, sh_ref, sc_ref, o_ref):
    x = x_ref[...]
    y = x * lax.rsqrt(jnp.mean(x * x, axis=-1, keepdims=True) + EPS) * g_ref[...]
    o_ref[...] = (y * (1.0 + sc_ref[...]) + sh_ref[...]).astype(o_ref.dtype)


def _normmod_call(x, g, mod3, layer, k_shift, k_scale):
    tm = 256
    def mod_spec(k):
        return pl.BlockSpec((None, 1, D_MODEL),
                            lambda i: (layer * MOD_ROWS + _mod_row_of_tile(i, tm), 0, k))
    return pl.pallas_call(
        _normmod_kernel,
        grid=(N_TOK // tm,),
        in_specs=[pl.BlockSpec((tm, D_MODEL), lambda i: (i, 0)),
                  pl.BlockSpec((1, D_MODEL), lambda i: (0, 0)),
                  mod_spec(k_shift), mod_spec(k_scale)],
        out_specs=pl.BlockSpec((tm, D_MODEL), lambda i: (i, 0)),
        out_shape=jax.ShapeDtypeStruct((N_TOK, D_MODEL), BF16),
        compiler_params=_cparams(("arbitrary",)),
        name="norm_mod",
    )(x, g.reshape(1, D_MODEL), mod3, mod3)


def _final_norm_kernel(x_ref, g_ref, o_ref):
    x = x_ref[...]
    o_ref[...] = x * lax.rsqrt(jnp.mean(x * x, axis=-1, keepdims=True) + EPS) * g_ref[...]


def _final_norm_call(x, g):
    tm = 256
    return pl.pallas_call(
        _final_norm_kernel,
        grid=(N_TOK // tm,),
        in_specs=[pl.BlockSpec((tm, D_MODEL), lambda i: (i, 0)),
                  pl.BlockSpec((1, D_MODEL), lambda i: (0, 0))],
        out_specs=pl.BlockSpec((tm, D_MODEL), lambda i: (i, 0)),
        out_shape=jax.ShapeDtypeStruct((N_TOK, D_MODEL), F32),
        compiler_params=_cparams(("arbitrary",)),
        name="final_norm",
    )(x, g.reshape(1, D_MODEL))


def _mm_kernel(x_ref, w_ref, o_ref, *, act):
    acc = jnp.dot(x_ref[...], w_ref[...], preferred_element_type=F32)
    if act == "sigmoid":
        acc = _sigmoid(acc)
    elif act == "relu2":
        acc = jnp.square(jnp.maximum(acc, 0.0))
    o_ref[...] = acc.astype(o_ref.dtype)


def _mm_call(x, w, *, act, out_dtype, tm, tn, name):
    m, k = x.shape
    n = w.shape[1]
    return pl.pallas_call(
        functools.partial(_mm_kernel, act=act),
        grid=(m // tm, pl.cdiv(n, tn)),
        in_specs=[pl.BlockSpec((tm, k), lambda i, j: (i, 0)),
                  pl.BlockSpec((k, tn), lambda i, j: (0, j))],
        out_specs=pl.BlockSpec((tm, tn), lambda i, j: (i, j)),
        out_shape=jax.ShapeDtypeStruct((m, n), out_dtype),
        compiler_params=_cparams(("arbitrary", "arbitrary")),
        name=name,
    )(x, w)


def _mm_resid_kernel(x_ref, w_ref, r_ref, g_ref, o_ref):
    acc = jnp.dot(x_ref[...], w_ref[...], preferred_element_type=F32)
    o_ref[...] = r_ref[...] + g_ref[...] * acc


def _mm_resid_call(x, w, resid, mod3, layer, k_gate, *, tm, tn, name):
    m, k = x.shape
    n = w.shape[1]
    per_tile = D_MODEL // tn
    return pl.pallas_call(
        _mm_resid_kernel,
        grid=(m // tm, n // tn),
        in_specs=[pl.BlockSpec((tm, k), lambda i, j: (i, 0)),
                  pl.BlockSpec((k, tn), lambda i, j: (0, j)),
                  pl.BlockSpec((tm, tn), lambda i, j: (i, j)),
                  pl.BlockSpec((None, 1, tn),
                               lambda i, j: (layer * MOD_ROWS + _mod_row_of_tile(i, tm), 0,
                                             k_gate * per_tile + j))],
        out_specs=pl.BlockSpec((tm, tn), lambda i, j: (i, j)),
        out_shape=jax.ShapeDtypeStruct((m, n), F32),
        compiler_params=_cparams(("arbitrary", "arbitrary")),
        name=name,
    )(x, w, resid, mod3)


def _merge_kernel(oa_ref, ob_ref, oc_ref, wa_ref, wb_ref, wc_ref, ga_ref, gb_ref, gc_ref, o_ref):
    a = jnp.dot(oa_ref[...], wa_ref[...], preferred_element_type=F32)
    b = jnp.dot(ob_ref[...], wb_ref[...], preferred_element_type=F32)
    c = jnp.dot(oc_ref[...], wc_ref[...], preferred_element_type=F32)
    o_ref[...] = (ga_ref[...] * a + gb_ref[...] * b + gc_ref[...] * c).astype(o_ref.dtype)


def _merge_call(o_a, o_b, o_c, w_a, w_b, w_c, gates):
    tm, tn = 1024, 512
    nj = D_MODEL // tn
    br = pl.BlockSpec((tm, HG_F), lambda i, j: (i, 0))
    wt = pl.BlockSpec((HG_F, tn), lambda i, j: (0, j))
    def gate_spec(k):
        return pl.BlockSpec((tm, tn), lambda i, j: (i, k * nj + j))
    return pl.pallas_call(
        _merge_kernel,
        grid=(N_TOK // tm, nj),
        in_specs=[br, br, br, wt, wt, wt, gate_spec(0), gate_spec(1), gate_spec(2)],
        out_specs=pl.BlockSpec((tm, tn), lambda i, j: (i, j)),
        out_shape=jax.ShapeDtypeStruct((N_TOK, D_MODEL), BF16),
        compiler_params=_cparams(("arbitrary", "arbitrary")),
        name="branch_merge",
    )(o_a, o_b, o_c, w_a, w_b, w_c, gates, gates, gates)


def _head_rmsnorm_gate(o, g, og):
    y = o * lax.rsqrt(jnp.mean(o * o, axis=-1, keepdims=True) + EPS) * g
    return y * _silu(og)


def _hgrn_kernel(q_ref, i_ref, og_ref, ff_ref, fb_ref, lb_ref, g_ref, s0_ref, o_ref, s_ref,
                 acc_ref, b_ref, stc_ref, km_ref):
    is_ctx = pl.program_id(0) < N_CTX_UNITS
    nblk = UNIT // HG_BLOCK
    nsub = HG_BLOCK // HG_SUB
    r = lax.broadcasted_iota(jnp.int32, (HG_BLOCK, HG_BLOCK), 0)
    c = lax.broadcasted_iota(jnp.int32, (HG_BLOCK, HG_BLOCK), 1)
    same = (r // HG_SUB) == (c // HG_SUB)
    row_sub = lax.broadcasted_iota(jnp.int32, (HG_BLOCK, HEAD_DIM), 0) // HG_SUB
    tri = (jnp.logical_and(same, c <= r), jnp.logical_and(same, c >= r))
    tri_bf = [jnp.where(t, 1.0, 0.0).astype(BF16) for t in tri]
    z_refs = (ff_ref, fb_ref)
    inst = [(blk, d) for blk in range(nblk) for d in range(2)]

    def expand(x):
        return jnp.broadcast_to(x[:, None, :], (nsub, HG_SUB, HEAD_DIM)).reshape(HG_BLOCK, HEAD_DIM)

    def blk_rows(blk):
        return slice(blk * HG_BLOCK, (blk + 1) * HG_BLOCK)

    qs = [_silu(q_ref[blk_rows(blk), :]) for blk in range(nblk)]
    vs = [i_ref[blk_rows(blk), :] for blk in range(nblk)]
    ks, lfs = [], []
    for blk, d in inst:
        lb = lb_ref[d:d + 1, :]
        f = lb + (1.0 - lb) * _sigmoid(z_refs[d][blk_rows(blk), :])
        lfs.append(jnp.log(f))
        ks.append(1.0 - f)
    bs = [_dot01(tri_bf[d], lf) for (blk, d), lf in zip(inst, lfs)]
    tots, qts, kts, qds = [], [], [], []
    for n, (blk, d) in enumerate(inst):
        b = bs[n]
        b_ref[n] = b
        tot = b_ref[n, pl.ds((HG_SUB - 1) if d == 0 else 0, nsub, stride=HG_SUB), :]
        mid_f = expand(b_ref[n, pl.ds(HG_SUB // 2, nsub, stride=HG_SUB), :])
        tots.append(tot)
        qts.append(qs[blk] * jnp.exp(b - mid_f))
        kts.append(ks[n] * jnp.exp(mid_f - b))
        qds.append((qs[blk] * jnp.exp(b)).astype(BF16))
        kd = ks[n] * jnp.exp(expand(tot) - b)
        for s in range(nsub):
            km_ref[n, :, s * HEAD_DIM:(s + 1) * HEAD_DIM] = jnp.where(row_sub == s, kd, 0.0).astype(BF16)
    scs = [jnp.where(tri[d], _dot3(qt, kt, _NT), 0.0) for (blk, d), qt, kt in zip(inst, qts, kts)]
    uts = [_dg(vs[blk].astype(BF16), km_ref[n], _TN) for n, (blk, d) in enumerate(inst)]
    outs = [_dot1(sc, vs[blk]) for (blk, d), sc in zip(inst, scs)]
    for d in range(2):
        st = s0_ref[d].T
        for blk in (range(nblk) if d == 0 else range(nblk - 1, -1, -1)):
            n = inst.index((blk, d))
            st = jnp.where(is_ctx, 0.0, st)
            for s in (range(nsub) if d == 0 else range(nsub - 1, -1, -1)):
                stc_ref[n, s] = st.astype(BF16)
                st = st * jnp.exp(tots[n][s:s + 1, :]) + uts[n][:, s * HEAD_DIM:(s + 1) * HEAD_DIM]
            s_ref[blk, d] = st.T
    for n, (blk, d) in enumerate(inst):
        o_int = [_dg(qds[n][s * HG_SUB:(s + 1) * HG_SUB], stc_ref[n, s], _NT) for s in range(nsub)]
        acc_ref[d, blk_rows(blk), :] = outs[n] + jnp.concatenate(o_int, axis=0)
    o_ref[...] = _head_rmsnorm_gate(acc_ref[0] + acc_ref[1], g_ref[...], og_ref[...]).astype(o_ref.dtype)


def _hgrn_call(z, lb, onorm_g, state, layer):
    def col(off):
        return pl.BlockSpec((UNIT, HEAD_DIM), lambda u, h: (u, off // HEAD_DIM + h))
    return pl.pallas_call(
        _hgrn_kernel,
        grid=(N_UNITS, HEADS),
        in_specs=[col(OFF_HQ), col(OFF_HI), col(OFF_HG), col(OFF_HFF), col(OFF_HFB),
                  pl.BlockSpec((2, HEAD_DIM), lambda u, h: (0, h)),
                  pl.BlockSpec((1, HEAD_DIM), lambda u, h: (0, 0)),
                  pl.BlockSpec((None, None, 2, None, HEAD_DIM, HEAD_DIM),
                               lambda u, h: (jnp.maximum(u - N_CTX_UNITS, 0), layer, 0, h, 0, 0))],
        out_specs=[pl.BlockSpec((UNIT, HEAD_DIM), lambda u, h: (u, h)),
                   pl.BlockSpec((None, SEQ_PER_UNIT, 2, None, HEAD_DIM, HEAD_DIM),
                                lambda u, h: (u, 0, 0, h, 0, 0))],
        out_shape=[jax.ShapeDtypeStruct((N_TOK, HG_F), BF16),
                   jax.ShapeDtypeStruct((N_UNITS, SEQ_PER_UNIT, 2, HEADS, HEAD_DIM, HEAD_DIM), F32)],
        scratch_shapes=[pltpu.VMEM((2, UNIT, HEAD_DIM), F32),
                        pltpu.VMEM((HG_INST, HG_BLOCK, HEAD_DIM), F32),
                        pltpu.VMEM((HG_INST, HG_BLOCK // HG_SUB, HEAD_DIM, HEAD_DIM), BF16),
                        pltpu.VMEM((HG_INST, HG_BLOCK, (HG_BLOCK // HG_SUB) * HEAD_DIM), BF16)],
        compiler_params=_cparams(("arbitrary", "arbitrary")),
        name="hgrn2_scan",
    )(z, z, z, z, z, lb, onorm_g.reshape(1, HEAD_DIM), state)


def _gmlp_kernel(u_ref, v_ref, vn_ref, ws_ref, bs_ref, o_ref):
    tm = u_ref.shape[0]
    for g in range(CM_GROUPS):
        cols = slice(g * HEAD_DIM, (g + 1) * HEAD_DIM)
        vg = _gelu(v_ref[:, cols])
        vg = vg * lax.rsqrt(jnp.mean(vg * vg, axis=-1, keepdims=True) + EPS) * vn_ref[:, cols]
        ug = _gelu(u_ref[:, cols])
        w = ws_ref[g]
        bias = bs_ref[:, g:g + 1]
        for ch in range(tm // CM_CHUNK):
            rows = slice(ch * CM_CHUNK, (ch + 1) * CM_CHUNK)
            s = _dot3(w, vg[rows]) + bias
            o_ref[rows, cols] = (ug[rows] * s).astype(o_ref.dtype)


def _gmlp_call(z, vnorm_g, ws, bs_t):
    tm = 512
    return pl.pallas_call(
        _gmlp_kernel,
        grid=(N_TOK // tm,),
        in_specs=[pl.BlockSpec((tm, CM_W), lambda i: (i, OFF_CU // CM_W)),
                  pl.BlockSpec((tm, CM_W), lambda i: (i, OFF_CV // CM_W)),
                  pl.BlockSpec((1, CM_W), lambda i: (0, 0)),
                  pl.BlockSpec((CM_GROUPS, CM_CHUNK, CM_CHUNK), lambda i: (0, 0, 0)),
                  pl.BlockSpec((CM_CHUNK, CM_GROUPS), lambda i: (0, 0))],
        out_specs=pl.BlockSpec((tm, CM_W), lambda i: (i, 0)),
        out_shape=jax.ShapeDtypeStruct((N_TOK, CM_W), BF16),
        compiler_params=_cparams(("arbitrary",)),
        name="chunk_gmlp",
    )(z, z, vnorm_g.reshape(1, CM_W), ws, bs_t)


GP_BLOCK = 256


def _gdn_gates_kernel(ab_ref, alog_ref, dt_ref, col_ref, rowt_ref):
    ab = ab_ref[...]
    lane = lax.broadcasted_iota(jnp.int32, ab.shape, 1)
    g = jnp.where(lane < 2 * HEADS, -jnp.exp(alog_ref[...]) * _softplus(ab + dt_ref[...]), 0.0)
    r = lax.broadcasted_iota(jnp.int32, (GP_BLOCK, GP_BLOCK), 0)
    c = lax.broadcasted_iota(jnp.int32, (GP_BLOCK, GP_BLOCK), 1)
    same = (r // GDN_CHUNK) == (c // GDN_CHUNK)
    tri_f = jnp.where(jnp.logical_and(same, c <= r), 1.0, 0.0).astype(BF16)
    tri_b = jnp.where(jnp.logical_and(same, c >= r), 1.0, 0.0).astype(BF16)
    cf = _dot01(tri_f, g)
    cb = _dot01(tri_b, g)
    col = jnp.where(lane < HEADS, cf, jnp.where(lane < 2 * HEADS, cb, _sigmoid(ab)))
    col_ref[...] = col
    rowt_ref[...] = col.T


def _gdn_gates_call(ab, a_log, dt_bias):
    pad = lambda t: jnp.pad(t.reshape(1, 2 * HEADS), ((0, 0), (0, LANES - 2 * HEADS)))
    return pl.pallas_call(
        _gdn_gates_kernel,
        grid=(N_TOK // GP_BLOCK,),
        in_specs=[pl.BlockSpec((GP_BLOCK, LANES), lambda i: (i, 0)),
                  pl.BlockSpec((1, LANES), lambda i: (0, 0)),
                  pl.BlockSpec((1, LANES), lambda i: (0, 0))],
        out_specs=[pl.BlockSpec((GP_BLOCK, LANES), lambda i: (i, 0)),
                   pl.BlockSpec((LANES, GP_BLOCK), lambda i: (0, i))],
        out_shape=[jax.ShapeDtypeStruct((N_TOK, LANES), F32),
                   jax.ShapeDtypeStruct((LANES, N_TOK), F32)],
        compiler_params=_cparams(("arbitrary",)),
        name="gdn_gates",
    )(ab, pad(a_log), pad(dt_bias))


CONV_PAD = 72
GDN_PAIR = 2 * GDN_CHUNK
GDN_STEP = 8 * GDN_CHUNK


def _gdn_kernel(qr_ref, kr_ref, vr_ref, og_ref, col_ref, rowt_ref, cwq_ref, cwk_ref, cwv_ref,
                g_ref, s0_ref, o_ref, s_ref,
                xp_ref, q_s, k_s, v_s, u_s, wq_s, kd_s, att_s, acc_ref, accb_ref):
    is_ctx = pl.program_id(0) < N_CTX_UNITS
    head = pl.program_id(1)
    n_chunks = UNIT // GDN_CHUNK
    chunks_per_seq = SEQ // GDN_CHUNK

    t = lax.broadcasted_iota(jnp.int32, (UNIT, 1), 0)
    period = jnp.where(is_ctx, SEQ, GRID_W)
    pos = jnp.bitwise_and(t, period - 1)
    ok_left = pos != 0
    ok_right = pos != period - 1
    zeros_pad = jnp.zeros((CONV_PAD, HEAD_DIM), F32)
    xp_ref[0:CONV_PAD, :] = zeros_pad
    xp_ref[CONV_PAD + UNIT:CONV_PAD + UNIT + CONV_PAD, :] = zeros_pad

    def conv_silu(x_ref, w_ref):
        xp_ref[CONV_PAD:CONV_PAD + UNIT, :] = x_ref[...]
        acc = jnp.zeros((UNIT, HEAD_DIM), F32)
        for i in range(CONV_K):
            for j in range(CONV_K):
                w = w_ref[CONV_K * i + j:CONV_K * i + j + 1, :]
                if i != CONV_K // 2:
                    w = jnp.where(is_ctx, 0.0, w)
                start = CONV_PAD + (i - 1) * GRID_W + (j - 1)
                xs = xp_ref[start:start + UNIT, :]
                if j == 0:
                    xs = jnp.where(ok_left, xs, 0.0)
                elif j == CONV_K - 1:
                    xs = jnp.where(ok_right, xs, 0.0)
                acc = acc + xs * w
        return _silu(acc)

    def l2norm(x):
        return x * lax.rsqrt(jnp.sum(x * x, axis=-1, keepdims=True) + EPS)

    q_s[...] = l2norm(conv_silu(qr_ref, cwq_ref)) * (HEAD_DIM ** -0.5)
    k_s[...] = l2norm(conv_silu(kr_ref, cwk_ref))
    v_s[...] = conv_silu(vr_ref, cwv_ref)

    rr = lax.broadcasted_iota(jnp.int32, (GDN_CHUNK, GDN_CHUNK), 0)
    cc = lax.broadcasted_iota(jnp.int32, (GDN_CHUNK, GDN_CHUNK), 1)
    eye = jnp.where(rr == cc, 1.0, 0.0)
    same_blk = [(rr // b) == (cc // b) for b in (8, 16, 32, 64)]
    lane = lax.broadcasted_iota(jnp.int32, (GDN_STEP, LANES), 1)
    sub8 = lax.broadcasted_iota(jnp.int32, (HEADS, GDN_STEP), 0)

    def pick(x, j):
        return jnp.sum(jnp.where(lane[:x.shape[0]] == j, x, 0.0), axis=-1, keepdims=True)

    incl = (cc <= rr, cc >= rr)
    strict = (cc < rr, cc > rr)
    off_masks = [jnp.logical_and(same_blk[lvl], jnp.logical_not(same_blk[lvl - 1]))
                 for lvl in range(1, len(same_blk))]

    def phase1(p, carry):
        rows = pl.ds(pl.multiple_of(p * GDN_STEP, GDN_STEP), GDN_STEP)
        q2, k2, v2 = q_s[rows, :], k_s[rows, :], v_s[rows, :]
        col = col_ref[rows, :]
        gcols = [pick(col, d * HEADS + head) for d in range(2)]
        betas = [pick(col, (2 + d) * HEADS + head) for d in range(2)]
        grows = [jnp.sum(jnp.where(sub8 == head, rowt_ref[d * HEADS:(d + 1) * HEADS, rows], 0.0),
                         axis=0, keepdims=True) for d in range(2)]
        inst = []
        for half in range(GDN_STEP // GDN_CHUNK):
            sl = slice(half * GDN_CHUNK, (half + 1) * GDN_CHUNK)
            qc, kc, vc = q2[sl], k2[sl], v2[sl]
            kk = _dot1(kc, kc, _NT)
            qk = _dot1(qc, kc, _NT)
            for d in range(2):
                gcol, beta = gcols[d][sl], betas[d][sl]
                decay = jnp.where(incl[d], jnp.exp(gcol - grows[d][:, sl]), 0.0)
                m = jnp.where(strict[d], beta * kk * decay, 0.0)
                inst.append((half, d, qc, kc, vc, qk, gcol, beta, decay, m))
        ms = [t[-1] for t in inst]
        pws = [jnp.where(same_blk[0], -m, 0.0) for m in ms]
        tinvs = [eye + pw for pw in pws]
        for _ in range(2):
            pws = [_dot_inv(pw, pw) for pw in pws]
            tinvs = [t + _dot_inv(t, pw) for t, pw in zip(tinvs, pws)]
        for mask in off_masks:
            tmp = [_dot_inv(jnp.where(mask, m, 0.0), t) for m, t in zip(ms, tinvs)]
            tinvs = [t - _dot_inv(t, x) for t, x in zip(tinvs, tmp)]
        for (half, d, qc, kc, vc, qk, gcol, beta, decay, m), tinv in zip(inst, tinvs):
            chunk0 = p * GDN_STEP + half * GDN_CHUNK
            out_rows = pl.ds(pl.multiple_of(chunk0, GDN_CHUNK), GDN_CHUNK)
            eg = jnp.exp(gcol)
            uw = _dot1(tinv, jnp.concatenate([vc * beta, kc * (beta * eg)], axis=1))
            u_s[d, out_rows, :] = uw[:, :HEAD_DIM]
            glast = gcol[GDN_CHUNK - 1:GDN_CHUNK] if d == 0 else gcol[0:1]
            wq_s[d, pl.ds(pl.multiple_of(2 * chunk0, GDN_PAIR), GDN_PAIR), :] = jnp.concatenate(
                [uw[:, HEAD_DIM:], qc * eg], axis=0).astype(BF16)
            att_s[d, out_rows, :] = (qk * decay).astype(BF16)
            kd_s[d, out_rows, :] = (kc * jnp.exp(glast - gcol)).astype(BF16)
        return carry

    lax.fori_loop(0, UNIT // GDN_STEP, phase1, 0)

    lane1 = lax.broadcasted_iota(jnp.int32, (1, LANES), 1)

    def phase2(i, carry):
        chs = (i, n_chunks - 1 - i)
        rows = [pl.ds(pl.multiple_of(ch * GDN_CHUNK, GDN_CHUNK), GDN_CHUNK) for ch in chs]
        firsts = [(chs[0] % chunks_per_seq) == 0, (chs[1] % chunks_per_seq) == chunks_per_seq - 1]
        ss = [jnp.where(jnp.logical_and(is_ctx, f), 0.0, s) for f, s in zip(firsts, carry)]
        ws = [_dg(wq_s[d, pl.ds(pl.multiple_of(2 * chs[d] * GDN_CHUNK, GDN_PAIR), GDN_PAIR), :],
                  ss[d].astype(BF16), _NN) for d in range(2)]
        v_new = [u_s[d, rows[d], :] - ws[d][:GDN_CHUNK] for d in range(2)]
        v_bf = [v.astype(BF16) for v in v_new]
        outs = [ws[d][GDN_CHUNK:] + _dg(att_s[d, rows[d], :], v_bf[d], _NN) for d in range(2)]
        new = []
        for d in range(2):
            tile0 = pl.multiple_of(chs[d] * GDN_CHUNK + (GDN_CHUNK - 8 if d == 0 else 0), 8)
            last = col_ref[pl.ds(tile0, 8), :]
            last = last[7:8] if d == 0 else last[0:1]
            glast = jnp.sum(jnp.where(lane1 == d * HEADS + head, last, 0.0), axis=-1, keepdims=True)
            s = ss[d] * jnp.exp(glast) + _dg(kd_s[d, rows[d], :], v_bf[d], _TN)
            s_ref[chs[d] // chunks_per_seq, d] = s
            new.append(s)
        acc_ref[rows[0], :] = outs[0]
        accb_ref[rows[1], :] = outs[1]
        return tuple(new)

    lax.fori_loop(0, n_chunks, phase2, (s0_ref[0], s0_ref[1]))
    o_ref[...] = _head_rmsnorm_gate(acc_ref[...] + accb_ref[...], g_ref[...], og_ref[...]).astype(o_ref.dtype)


def _gdn_call(z, col, rowt, conv_w, onorm_g, state, layer):
    def zcol(off):
        return pl.BlockSpec((UNIT, HEAD_DIM), lambda u, h: (u, off // HEAD_DIM + h))
    def wcol(part):
        return pl.BlockSpec((CONV_K * CONV_K, HEAD_DIM), lambda u, h: (0, part * HEADS + h))
    scr = lambda *shape: pltpu.VMEM(shape, F32)
    return pl.pallas_call(
        _gdn_kernel,
        grid=(N_UNITS, HEADS),
        in_specs=[zcol(OFF_GQ), zcol(OFF_GK), zcol(OFF_GV), zcol(OFF_GG),
                  pl.BlockSpec((UNIT, LANES), lambda u, h: (u, 0)),
                  pl.BlockSpec((LANES, UNIT), lambda u, h: (0, u)),
                  wcol(0), wcol(1), wcol(2),
                  pl.BlockSpec((1, HEAD_DIM), lambda u, h: (0, 0)),
                  pl.BlockSpec((None, None, 2, None, HEAD_DIM, HEAD_DIM),
                               lambda u, h: (jnp.maximum(u - N_CTX_UNITS, 0), layer, 0, h, 0, 0))],
        out_specs=[pl.BlockSpec((UNIT, HEAD_DIM), lambda u, h: (u, h)),
                   pl.BlockSpec((None, SEQ_PER_UNIT, 2, None, HEAD_DIM, HEAD_DIM),
                                lambda u, h: (u, 0, 0, h, 0, 0))],
        out_shape=[jax.ShapeDtypeStruct((N_TOK, HG_F), BF16),
                   jax.ShapeDtypeStruct((N_UNITS, SEQ_PER_UNIT, 2, HEADS, HEAD_DIM, HEAD_DIM), F32)],
        scratch_shapes=[scr(UNIT + 2 * CONV_PAD, HEAD_DIM),
                        scr(UNIT, HEAD_DIM), scr(UNIT, HEAD_DIM), scr(UNIT, HEAD_DIM),
                        scr(2, UNIT, HEAD_DIM),
                        pltpu.VMEM((2, 2 * UNIT, HEAD_DIM), BF16),
                        pltpu.VMEM((2, UNIT, HEAD_DIM), BF16),
                        pltpu.VMEM((2, UNIT, GDN_CHUNK), BF16),
                        scr(UNIT, HEAD_DIM), scr(UNIT, HEAD_DIM)],
        compiler_params=_cparams(("arbitrary", "arbitrary")),
        name="gdn_scan",
    )(z, z, z, z, col, rowt, conv_w, conv_w, conv_w, onorm_g.reshape(1, HEAD_DIM), state)


def kernel(x_prompt, x_sample, c, state_hgrn, state_gdn, c_ctx, norm1_g, norm2_g, w_mod, b_mod, w_in, hg_lb, hg_onorm_g, cm_vnorm_g, cm_ws, cm_bs, gdn_conv, gdn_A_log, gdn_dt_bias, gdn_onorm_g, w_br_hg, w_br_cm, w_br_gdn, w_out, w_ff1, w_ff2, final_g):
    x = jnp.concatenate([x_prompt.reshape(N_CTX_TOK, D_MODEL), x_sample.reshape(N_LAT_TOK, D_MODEL)], axis=0)
    cvec = jnp.concatenate([c_ctx[None, :], c, jnp.zeros((MOD_ROWS - 1 - DEC_BATCH, D_MODEL), F32)], axis=0)
    mod3 = _mod_call(cvec, w_mod, b_mod).reshape(DEPTH * MOD_ROWS, 1, 6 * D_MODEL)

    lb_all = jnp.cumsum(jax.nn.softmax(hg_lb.astype(F32), axis=0), axis=0)
    lb_all = lb_all - lb_all[:1]

    new_hg, new_gdn = [], []
    for l in range(DEPTH):
        w_main = w_in[l, :, :N_MAIN].astype(BF16)
        w_ab = jnp.pad(w_in[l, :, OFF_AB:OFF_GATES], ((0, 0), (0, LANES - 4 * HEADS))).astype(BF16)
        w_gates = w_in[l, :, OFF_GATES:].astype(BF16)

        h = _normmod_call(x, norm1_g[l], mod3, l, 0, 1)
        z = _mm_call(h, w_main, act=None, out_dtype=F32, tm=1024, tn=1024, name="in_proj")
        ab = _mm_call(h, w_ab, act=None, out_dtype=F32, tm=1024, tn=LANES, name="in_proj_ab")
        gates = _mm_call(h, w_gates, act="sigmoid", out_dtype=F32, tm=1024, tn=1024, name="in_proj_gates")

        o_a, s_hg = _hgrn_call(z, lb_all[l], hg_onorm_g[l], state_hgrn, l)
        o_b = _gmlp_call(z, cm_vnorm_g[l], cm_ws[l], cm_bs[l].T)
        col, rowt = _gdn_gates_call(ab, gdn_A_log[l], gdn_dt_bias[l])
        o_c, s_gdn = _gdn_call(z, col, rowt, gdn_conv[l].reshape(CONV_K * CONV_K, 3 * HG_F),
                               gdn_onorm_g[l], state_gdn, l)
        new_hg.append(s_hg[:N_CTX_UNITS].reshape(BATCH, 2, HEADS, HEAD_DIM, HEAD_DIM))
        new_gdn.append(s_gdn[:N_CTX_UNITS].reshape(BATCH, 2, HEADS, HEAD_DIM, HEAD_DIM))

        merged = _merge_call(o_a, o_b, o_c, w_br_hg[l].astype(BF16), w_br_cm[l].astype(BF16),
                             w_br_gdn[l].astype(BF16), gates)
        x = _mm_resid_call(merged, w_out[l].astype(BF16), x, mod3, l, 2, tm=1024, tn=1024, name="out_proj")

        h2 = _normmod_call(x, norm2_g[l], mod3, l, 3, 4)
        up = _mm_call(h2, w_ff1[l].astype(BF16), act="relu2", out_dtype=BF16, tm=1024, tn=1024, name="ffn_up")
        x = _mm_resid_call(up, w_ff2[l].astype(BF16), x, mod3, l, 5, tm=512, tn=512, name="ffn_down")

    y = _final_norm_call(x, final_g)
    y_prompt = y[:N_CTX_TOK].reshape(BATCH, SEQ, D_MODEL)
    y_sample = y[N_CTX_TOK:].reshape(DEC_BATCH, DEC_SEQ, D_MODEL)
    return (y_prompt, y_sample, jnp.stack(new_hg, axis=1), jnp.stack(new_gdn, axis=1))
```

```python
import functools

import jax
import jax.numpy as jnp
from jax import lax
from jax.experimental import pallas as pl
from jax.experimental.pallas import tpu as pltpu

F32 = jnp.float32
BF16 = jnp.bfloat16

D_MODEL = 2048
BATCH = 16
SEQ = 256
DEPTH = 2
DEC_BATCH = 4
DEC_SEQ = 1024
GRID_W = 64
EPS = 1e-6
D_FF = 4 * D_MODEL
HEADS = 8
HEAD_DIM = 128
HG_F = HEADS * HEAD_DIM
CM_GROUPS = 8
CM_W = CM_GROUPS * HEAD_DIM
CM_CHUNK = 128
GDN_CHUNK = 64
CONV_K = 3

OFF_HQ, OFF_HI, OFF_HG, OFF_HFF, OFF_HFB = 0, 1024, 2048, 3072, 4096
OFF_CU, OFF_CV = 5120, 6144
OFF_GQ, OFF_GK, OFF_GV, OFF_GG = 7168, 8192, 9216, 10240
OFF_AB = 11264
OFF_GATES = 11296
IN_DIM = 17440
N_MAIN = OFF_AB

N_CTX_TOK = BATCH * SEQ
N_LAT_TOK = DEC_BATCH * DEC_SEQ
N_TOK = N_CTX_TOK + N_LAT_TOK
UNIT = DEC_SEQ
N_CTX_UNITS = N_CTX_TOK // UNIT
N_UNITS = N_TOK // UNIT
SEQ_PER_UNIT = UNIT // SEQ

LANES = 128
MOD_ROWS = 8
HG_BLOCK = 256
HG_SUB = 32
HG_INST = 2 * (UNIT // HG_BLOCK)
VMEM_LIMIT = 56 * 1024 * 1024

_NT = (((1,), (1,)), ((), ()))
_TN = (((0,), (0,)), ((), ()))
_NN = (((1,), (0,)), ((), ()))


def _dg(a, b, dims):
    return lax.dot_general(a, b, dims, preferred_element_type=F32)


def _split2(x):
    hi = x.astype(BF16)
    lo = (x - hi.astype(F32)).astype(BF16)
    return hi, lo


def _dot1(a, b, dims=_NN):
    return _dg(a.astype(BF16), b.astype(BF16), dims)


def _dot3(a, b, dims=_NN):
    ah, al = _split2(a)
    bh, bl = _split2(b)
    return _dg(ah, bh, dims) + (_dg(ah, bl, dims) + _dg(al, bh, dims))


_dot_inv = _dot1


def _dot01(m, x):
    hi = x.astype(BF16)
    r = x - hi.astype(F32)
    mid = r.astype(BF16)
    lo = (r - mid.astype(F32)).astype(BF16)
    return _dg(m, hi, _NN) + (_dg(m, mid, _NN) + _dg(m, lo, _NN))


def _sigmoid(x):
    return 1.0 / (1.0 + jnp.exp(-x))


def _silu(x):
    return x * _sigmoid(x)


def _gelu(x):
    return 0.5 * x * (1.0 + lax.erf(x * (2.0 ** -0.5)))


def _softplus(x):
    return jnp.maximum(x, 0.0) + jnp.log1p(jnp.exp(-jnp.abs(x)))


def _mod_row_of_tile(i, tm):
    return jnp.maximum(0, (i * tm - N_CTX_TOK) // DEC_SEQ + 1)


def _cparams(sem):
    return pltpu.CompilerParams(dimension_semantics=sem, vmem_limit_bytes=VMEM_LIMIT)


def _mod_kernel(c_ref, w_ref, b_ref, o_ref):
    s = _silu(c_ref[...])
    hi, lo = _split2(s)
    w = w_ref[...].astype(BF16)
    o_ref[...] = _dg(hi, w, _NN) + _dg(lo, w, _NN) + b_ref[...]


def _mod_call(cvec, w_mod, b_mod):
    tn = 1024
    n = 6 * D_MODEL
    return pl.pallas_call(
        _mod_kernel,
        grid=(DEPTH, n // tn),
        in_specs=[pl.BlockSpec((MOD_ROWS, D_MODEL), lambda l, j: (0, 0)),
                  pl.BlockSpec((None, D_MODEL, tn), lambda l, j: (l, 0, j)),
                  pl.BlockSpec((None, 1, tn), lambda l, j: (l, 0, j))],
        out_specs=pl.BlockSpec((None, MOD_ROWS, tn), lambda l, j: (l, 0, j)),
        out_shape=jax.ShapeDtypeStruct((DEPTH, MOD_ROWS, n), F32),
        compiler_params=_cparams(("arbitrary", "arbitrary")),
        name="modulation",
    )(cvec, w_mod, b_mod.reshape(DEPTH, 1, n))


def _normmod_kernel(x_ref, g_ref, sh_ref, sc_ref, o_ref):
    x = x_ref[...]
    y = x * lax.rsqrt(jnp.mean(x * x, axis=-1, keepdims=True) + EPS) * g_ref[...]
    o_ref[...] = (y * (1.0 + sc_ref[...]) + sh_ref[...]).astype(o_ref.dtype)


def _normmod_call(x, g, mod3, layer, k_shift, k_scale):
    tm = 256
    def mod_spec(k):
        return pl.BlockSpec((None, 1, D_MODEL),
                            lambda i: (layer * MOD_ROWS + _mod_row_of_tile(i, tm), 0, k))
    return pl.pallas_call(
        _normmod_kernel,
        grid=(N_TOK // tm,),
        in_specs=[pl.BlockSpec((tm, D_MODEL), lambda i: (i, 0)),
                  pl.BlockSpec((1, D_MODEL), lambda i: (0, 0)),
                  mod_spec(k_shift), mod_spec(k_scale)],
        out_specs=pl.BlockSpec((tm, D_MODEL), lambda i: (i, 0)),
        out_shape=jax.ShapeDtypeStruct((N_TOK, D_MODEL), BF16),
        compiler_params=_cparams(("arbitrary",)),
        name="norm_mod",
    )(x, g.reshape(1, D_MODEL), mod3, mod3)


def _final_norm_kernel(x_ref, g_ref, o_ref):
    x = x_ref[...]
    o_ref[...] = x * lax.rsqrt(jnp.mean(x * x, axis=-1, keepdims=True) + EPS) * g_ref[...]


def _final_norm_call(x, g):
    tm = 256
    return pl.pallas_call(
        _final_norm_kernel,
        grid=(N_TOK // tm,),
        in_specs=[pl.BlockSpec((tm, D_MODEL), lambda i: (i, 0)),
                  pl.BlockSpec((1, D_MODEL), lambda i: (0, 0))],
        out_specs=pl.BlockSpec((tm, D_MODEL), lambda i: (i, 0)),
        out_shape=jax.ShapeDtypeStruct((N_TOK, D_MODEL), F32),
        compiler_params=_cparams(("arbitrary",)),
        name="final_norm",
    )(x, g.reshape(1, D_MODEL))


def _mm_kernel(x_ref, w_ref, o_ref, *, act):
    acc = jnp.dot(x_ref[...], w_ref[...], preferred_element_type=F32)
    if act == "sigmoid":
        acc = _sigmoid(acc)
    elif act == "relu2":
        acc = jnp.square(jnp.maximum(acc, 0.0))
    o_ref[...] = acc.astype(o_ref.dtype)


def _mm_call(x, w, *, act, out_dtype, tm, tn, name):
    m, k = x.shape
    n = w.shape[1]
    return pl.pallas_call(
        functools.partial(_mm_kernel, act=act),
        grid=(m // tm, pl.cdiv(n, tn)),
        in_specs=[pl.BlockSpec((tm, k), lambda i, j: (i, 0)),
                  pl.BlockSpec((k, tn), lambda i, j: (0, j))],
        out_specs=pl.BlockSpec((tm, tn), lambda i, j: (i, j)),
        out_shape=jax.ShapeDtypeStruct((m, n), out_dtype),
        compiler_params=_cparams(("arbitrary", "arbitrary")),
        name=name,
    )(x, w)


def _mm_resid_kernel(x_ref, w_ref, r_ref, g_ref, o_ref):
    acc = jnp.dot(x_ref[...], w_ref[...], preferred_element_type=F32)
    o_ref[...] = r_ref[...] + g_ref[...] * acc


def _mm_resid_call(x, w, resid, mod3, layer, k_gate, *, tm, tn, name):
    m, k = x.shape
    n = w.shape[1]
    per_tile = D_MODEL // tn
    return pl.pallas_call(
        _mm_resid_kernel,
        grid=(m // tm, n // tn),
        in_specs=[pl.BlockSpec((tm, k), lambda i, j: (i, 0)),
                  pl.BlockSpec((k, tn), lambda i, j: (0, j)),
                  pl.BlockSpec((tm, tn), lambda i, j: (i, j)),
                  pl.BlockSpec((None, 1, tn),
                               lambda i, j: (layer * MOD_ROWS + _mod_row_of_tile(i, tm), 0,
                                             k_gate * per_tile + j))],
        out_specs=pl.BlockSpec((tm, tn), lambda i, j: (i, j)),
        out_shape=jax.ShapeDtypeStruct((m, n), F32),
        compiler_params=_cparams(("arbitrary", "arbitrary")),
        name=name,
    )(x, w, resid, mod3)


def _merge_kernel(oa_ref, ob_ref, oc_ref, wa_ref, wb_ref, wc_ref, ga_ref, gb_ref, gc_ref, o_ref):
    a = jnp.dot(oa_ref[...], wa_ref[...], preferred_element_type=F32)
    b = jnp.dot(ob_ref[...], wb_ref[...], preferred_element_type=F32)
    c = jnp.dot(oc_ref[...], wc_ref[...], preferred_element_type=F32)
    o_ref[...] = (ga_ref[...] * a + gb_ref[...] * b + gc_ref[...] * c).astype(o_ref.dtype)


def _merge_call(o_a, o_b, o_c, w_a, w_b, w_c, gates):
    tm, tn = 1024, 512
    nj = D_MODEL // tn
    br = pl.BlockSpec((tm, HG_F), lambda i, j: (i, 0))
    wt = pl.BlockSpec((HG_F, tn), lambda i, j: (0, j))
    def gate_spec(k):
        return pl.BlockSpec((tm, tn), lambda i, j: (i, k * nj + j))
    return pl.pallas_call(
        _merge_kernel,
        grid=(N_TOK // tm, nj),
        in_specs=[br, br, br, wt, wt, wt, gate_spec(0), gate_spec(1), gate_spec(2)],
        out_specs=pl.BlockSpec((tm, tn), lambda i, j: (i, j)),
        out_shape=jax.ShapeDtypeStruct((N_TOK, D_MODEL), BF16),
        compiler_params=_cparams(("arbitrary", "arbitrary")),
        name="branch_merge",
    )(o_a, o_b, o_c, w_a, w_b, w_c, gates, gates, gates)


def _head_rmsnorm_gate(o, g, og):
    y = o * lax.rsqrt(jnp.mean(o * o, axis=-1, keepdims=True) + EPS) * g
    return y * _silu(og)


def _hgrn_kernel(q_ref, i_ref, og_ref, ff_ref, fb_ref, lb_ref, g_ref, s0_ref, o_ref, s_ref,
                 acc_ref, b_ref, stc_ref, km_ref):
    is_ctx = pl.program_id(0) < N_CTX_UNITS
    nblk = UNIT // HG_BLOCK
    nsub = HG_BLOCK // HG_SUB
    r = lax.broadcasted_iota(jnp.int32, (HG_BLOCK, HG_BLOCK), 0)
    c = lax.broadcasted_iota(jnp.int32, (HG_BLOCK, HG_BLOCK), 1)
    same = (r // HG_SUB) == (c // HG_SUB)
    row_sub = lax.broadcasted_iota(jnp.int32, (HG_BLOCK, HEAD_DIM), 0) // HG_SUB
    tri = (jnp.logical_and(same, c <= r), jnp.logical_and(same, c >= r))
    tri_bf = [jnp.where(t, 1.0, 0.0).astype(BF16) for t in tri]
    z_refs = (ff_ref, fb_ref)
    inst = [(blk, d) for blk in range(nblk) for d in range(2)]

    def expand(x):
        return jnp.broadcast_to(x[:, None, :], (nsub, HG_SUB, HEAD_DIM)).reshape(HG_BLOCK, HEAD_DIM)

    def blk_rows(blk):
        return slice(blk * HG_BLOCK, (blk + 1) * HG_BLOCK)

    qs = [_silu(q_ref[blk_rows(blk), :]) for blk in range(nblk)]
    vs = [i_ref[blk_rows(blk), :] for blk in range(nblk)]
    ks, lfs = [], []
    for blk, d in inst:
        lb = lb_ref[d:d + 1, :]
        f = lb + (1.0 - lb) * _sigmoid(z_refs[d][blk_rows(blk), :])
        lfs.append(jnp.log(f))
        ks.append(1.0 - f)
    bs = [_dot01(tri_bf[d], lf) for (blk, d), lf in zip(inst, lfs)]
    tots, qts, kts, qds = [], [], [], []
    for n, (blk, d) in enumerate(inst):
        b = bs[n]
        b_ref[n] = b
        tot = b_ref[n, pl.ds((HG_SUB - 1) if d == 0 else 0, nsub, stride=HG_SUB), :]
        mid_f = expand(b_ref[n, pl.ds(HG_SUB // 2, nsub, stride=HG_SUB), :])
        tots.append(tot)
        qts.append(qs[blk] * jnp.exp(b - mid_f))
        kts.append(ks[n] * jnp.exp(mid_f - b))
        qds.append((qs[blk] * jnp.exp(b)).astype(BF16))
        kd = ks[n] * jnp.exp(expand(tot) - b)
        for s in range(nsub):
            km_ref[n, :, s * HEAD_DIM:(s + 1) * HEAD_DIM] = jnp.where(row_sub == s, kd, 0.0).astype(BF16)
    scs = [jnp.where(tri[d], _dot3(qt, kt, _NT), 0.0) for (blk, d), qt, kt in zip(inst, qts, kts)]
    uts = [_dg(vs[blk].astype(BF16), km_ref[n], _TN) for n, (blk, d) in enumerate(inst)]
    outs = [_dot1(sc, vs[blk]) for (blk, d), sc in zip(inst, scs)]
    for d in range(2):
        st = s0_ref[d].T
        for blk in (range(nblk) if d == 0 else range(nblk - 1, -1, -1)):
            n = inst.index((blk, d))
            st = jnp.where(is_ctx, 0.0, st)
            for s in (range(nsub) if d == 0 else range(nsub - 1, -1, -1)):
                stc_ref[n, s] = st.astype(BF16)
                st = st * jnp.exp(tots[n][s:s + 1, :]) + uts[n][:, s * HEAD_DIM:(s + 1) * HEAD_DIM]
            s_ref[blk, d] = st.T
    for n, (blk, d) in enumerate(inst):
        o_int = [_dg(qds[n][s * HG_SUB:(s + 1) * HG_SUB], stc_ref[n, s], _NT) for s in range(nsub)]
        acc_ref[d, blk_rows(blk), :] = outs[n] + jnp.concatenate(o_int, axis=0)
    o_ref[...] = _head_rmsnorm_gate(acc_ref[0] + acc_ref[1], g_ref[...], og_ref[...]).astype(o_ref.dtype)


def _hgrn_call(z, lb, onorm_g, state, layer):
    def col(off):
        return pl.BlockSpec((UNIT, HEAD_DIM), lambda u, h: (u, off // HEAD_DIM + h))
    return pl.pallas_call(
        _hgrn_kernel,
        grid=(N_UNITS, HEADS),
        in_specs=[col(OFF_HQ), col(OFF_HI), col(OFF_HG), col(OFF_HFF), col(OFF_HFB),
                  pl.BlockSpec((2, HEAD_DIM), lambda u, h: (0, h)),
                  pl.BlockSpec((1, HEAD_DIM), lambda u, h: (0, 0)),
                  pl.BlockSpec((None, None, 2, None, HEAD_DIM, HEAD_DIM),
                               lambda u, h: (jnp.maximum(u - N_CTX_UNITS, 0), layer, 0, h, 0, 0))],
        out_specs=[pl.BlockSpec((UNIT, HEAD_DIM), lambda u, h: (u, h)),
                   pl.BlockSpec((None, SEQ_PER_UNIT, 2, None, HEAD_DIM, HEAD_DIM),
                                lambda u, h: (u, 0, 0, h, 0, 0))],
        out_shape=[jax.ShapeDtypeStruct((N_TOK, HG_F), BF16),
                   jax.ShapeDtypeStruct((N_UNITS, SEQ_PER_UNIT, 2, HEADS, HEAD_DIM, HEAD_DIM), F32)],
        scratch_shapes=[pltpu.VMEM((2, UNIT, HEAD_DIM), F32),
                        pltpu.VMEM((HG_INST, HG_BLOCK, HEAD_DIM), F32),
                        pltpu.VMEM((HG_INST, HG_BLOCK // HG_SUB, HEAD_DIM, HEAD_DIM), BF16),
                        pltpu.VMEM((HG_INST, HG_BLOCK, (HG_BLOCK // HG_SUB) * HEAD_DIM), BF16)],
        compiler_params=_cparams(("arbitrary", "arbitrary")),
        name="hgrn2_scan",
    )(z, z, z, z, z, lb, onorm_g.reshape(1, HEAD_DIM), state)


def _gmlp_kernel(u_ref, v_ref, vn_ref, ws_ref, bs_ref, o_ref):
    tm = u_ref.shape[0]
    for g in range(CM_GROUPS):
        cols = slice(g * HEAD_DIM, (g + 1) * HEAD_DIM)
        vg = _gelu(v_ref[:, cols])
        vg = vg * lax.rsqrt(jnp.mean(vg * vg, axis=-1, keepdims=True) + EPS) * vn_ref[:, cols]
        ug = _gelu(u_ref[:, cols])
        w = ws_ref[g]
        bias = bs_ref[:, g:g + 1]
        for ch in range(tm // CM_CHUNK):
            rows = slice(ch * CM_CHUNK, (ch + 1) * CM_CHUNK)
            s = _dot3(w, vg[rows]) + bias
            o_ref[rows, cols] = (ug[rows] * s).astype(o_ref.dtype)


def _gmlp_call(z, vnorm_g, ws, bs_t):
    tm = 512
    return pl.pallas_call(
        _gmlp_kernel,
        grid=(N_TOK // tm,),
        in_specs=[pl.BlockSpec((tm, CM_W), lambda i: (i, OFF_CU // CM_W)),
                  pl.BlockSpec((tm, CM_W), lambda i: (i, OFF_CV // CM_W)),
                  pl.BlockSpec((1, CM_W), lambda i: (0, 0)),
                  pl.BlockSpec((CM_GROUPS, CM_CHUNK, CM_CHUNK), lambda i: (0, 0, 0)),
                  pl.BlockSpec((CM_CHUNK, CM_GROUPS), lambda i: (0, 0))],
        out_specs=pl.BlockSpec((tm, CM_W), lambda i: (i, 0)),
        out_shape=jax.ShapeDtypeStruct((N_TOK, CM_W), BF16),
        compiler_params=_cparams(("arbitrary",)),
        name="chunk_gmlp",
    )(z, z, vnorm_g.reshape(1, CM_W), ws, bs_t)


GP_BLOCK = 256


def _gdn_gates_kernel(ab_ref, alog_ref, dt_ref, col_ref, rowt_ref):
    ab = ab_ref[...]
    lane = lax.broadcasted_iota(jnp.int32, ab.shape, 1)
    g = jnp.where(lane < 2 * HEADS, -jnp.exp(alog_ref[...]) * _softplus(ab + dt_ref[...]), 0.0)
    r = lax.broadcasted_iota(jnp.int32, (GP_BLOCK, GP_BLOCK), 0)
    c = lax.broadcasted_iota(jnp.int32, (GP_BLOCK, GP_BLOCK), 1)
    same = (r // GDN_CHUNK) == (c // GDN_CHUNK)
    tri_f = jnp.where(jnp.logical_and(same, c <= r), 1.0, 0.0).astype(BF16)
    tri_b = jnp.where(jnp.logical_and(same, c >= r), 1.0, 0.0).astype(BF16)
    cf = _dot01(tri_f, g)
    cb = _dot01(tri_b, g)
    col = jnp.where(lane < HEADS, cf, jnp.where(lane < 2 * HEADS, cb, _sigmoid(ab)))
    col_ref[...] = col
    rowt_ref[...] = col.T


def _gdn_gates_call(ab, a_log, dt_bias):
    pad = lambda t: jnp.pad(t.reshape(1, 2 * HEADS), ((0, 0), (0, LANES - 2 * HEADS)))
    return pl.pallas_call(
        _gdn_gates_kernel,
        grid=(N_TOK // GP_BLOCK,),
        in_specs=[pl.BlockSpec((GP_BLOCK, LANES), lambda i: (i, 0)),
                  pl.BlockSpec((1, LANES), lambda i: (0, 0)),
                  pl.BlockSpec((1, LANES), lambda i: (0, 0))],
        out_specs=[pl.BlockSpec((GP_BLOCK, LANES), lambda i: (i, 0)),
                   pl.BlockSpec((LANES, GP_BLOCK), lambda i: (0, i))],
        out_shape=[jax.ShapeDtypeStruct((N_TOK, LANES), F32),
                   jax.ShapeDtypeStruct((LANES, N_TOK), F32)],
        compiler_params=_cparams(("arbitrary",)),
        name="gdn_gates",
    )(ab, pad(a_log), pad(dt_bias))


CONV_PAD = 72
GDN_AQ = HEAD_DIM + GDN_CHUNK
GDN_STEP = 8 * GDN_CHUNK


def _gdn_kernel(qr_ref, kr_ref, vr_ref, og_ref, col_ref, rowt_ref, cwq_ref, cwk_ref, cwv_ref,
                g_ref, s0_ref, o_ref, s_ref,
                xp_ref, q_s, k_s, v_s, o_s, b_s, aq_s):
    is_ctx = pl.program_id(0) < N_CTX_UNITS
    head = pl.program_id(1)
    n_chunks = UNIT // GDN_CHUNK
    chunks_per_seq = SEQ // GDN_CHUNK

    t = lax.broadcasted_iota(jnp.int32, (UNIT, 1), 0)
    period = jnp.where(is_ctx, SEQ, GRID_W)
    pos = jnp.bitwise_and(t, period - 1)
    ok_left = pos != 0
    ok_right = pos != period - 1
    zeros_pad = jnp.zeros((CONV_PAD, HEAD_DIM), F32)
    xp_ref[0:CONV_PAD, :] = zeros_pad
    xp_ref[CONV_PAD + UNIT:CONV_PAD + UNIT + CONV_PAD, :] = zeros_pad

    def conv_silu(x_ref, w_ref):
        xp_ref[CONV_PAD:CONV_PAD + UNIT, :] = x_ref[...]
        acc = jnp.zeros((UNIT, HEAD_DIM), F32)
        for i in range(CONV_K):
            for j in range(CONV_K):
                w = w_ref[CONV_K * i + j:CONV_K * i + j + 1, :]
                if i != CONV_K // 2:
                    w = jnp.where(is_ctx, 0.0, w)
                start = CONV_PAD + (i - 1) * GRID_W + (j - 1)
                xs = xp_ref[start:start + UNIT, :]
                if j == 0:
                    xs = jnp.where(ok_left, xs, 0.0)
                elif j == CONV_K - 1:
                    xs = jnp.where(ok_right, xs, 0.0)
                acc = acc + xs * w
        return _silu(acc)

    def l2norm(x):
        return x * lax.rsqrt(jnp.sum(x * x, axis=-1, keepdims=True) + EPS)

    q_s[...] = l2norm(conv_silu(qr_ref, cwq_ref)) * (HEAD_DIM ** -0.5)
    k_s[...] = l2norm(conv_silu(kr_ref, cwk_ref))
    v_s[...] = conv_silu(vr_ref, cwv_ref)

    rr = lax.broadcasted_iota(jnp.int32, (GDN_CHUNK, GDN_CHUNK), 0)
    cc = lax.broadcasted_iota(jnp.int32, (GDN_CHUNK, GDN_CHUNK), 1)
    eye = jnp.where(rr == cc, 1.0, 0.0)
    same_blk = [(rr // b) == (cc // b) for b in (8, 16, 32, 64)]
    lane = lax.broadcasted_iota(jnp.int32, (GDN_STEP, LANES), 1)
    sub8 = lax.broadcasted_iota(jnp.int32, (HEADS, GDN_STEP), 0)

    def pick(x, j):
        return jnp.sum(jnp.where(lane[:x.shape[0]] == j, x, 0.0), axis=-1, keepdims=True)

    incl = (cc <= rr, cc >= rr)
    strict = (cc < rr, cc > rr)
    off_masks = [jnp.logical_and(same_blk[lvl], jnp.logical_not(same_blk[lvl - 1]))
                 for lvl in range(1, len(same_blk))]

    def phase1(p, carry):
        rows = pl.ds(pl.multiple_of(p * GDN_STEP, GDN_STEP), GDN_STEP)
        q2, k2, v2 = q_s[rows, :], k_s[rows, :], v_s[rows, :]
        col = col_ref[rows, :]
        gcols = [pick(col, d * HEADS + head) for d in range(2)]
        betas = [pick(col, (2 + d) * HEADS + head) for d in range(2)]
        grows = [jnp.sum(jnp.where(sub8 == head, rowt_ref[d * HEADS:(d + 1) * HEADS, rows], 0.0),
                         axis=0, keepdims=True) for d in range(2)]
        inst = []
        for half in range(GDN_STEP // GDN_CHUNK):
            sl = slice(half * GDN_CHUNK, (half + 1) * GDN_CHUNK)
            qc, kc, vc = q2[sl], k2[sl], v2[sl]
            kk = _dot1(kc, kc, _NT)
            qk = _dot1(qc, kc, _NT)
            for d in range(2):
                gcol, beta = gcols[d][sl], betas[d][sl]
                decay = jnp.where(incl[d], jnp.exp(gcol - grows[d][:, sl]), 0.0)
                m = jnp.where(strict[d], beta * kk * decay, 0.0)
                inst.append((half, d, qc, kc, vc, qk, gcol, beta, decay, m))
        ms = [t[-1] for t in inst]
        pws = [jnp.where(same_blk[0], -m, 0.0) for m in ms]
        tinvs = [eye + pw for pw in pws]
        for _ in range(2):
            pws = [_dot_inv(pw, pw) for pw in pws]
            tinvs = [t + _dot_inv(t, pw) for t, pw in zip(tinvs, pws)]
        for mask in off_masks:
            tmp = [_dot_inv(jnp.where(mask, m, 0.0), t) for m, t in zip(ms, tinvs)]
            tinvs = [t - _dot_inv(t, x) for t, x in zip(tinvs, tmp)]
        uws = [_dot1(tinv, jnp.concatenate([vc * beta, kc * (beta * jnp.exp(gcol))], axis=1)).astype(BF16)
               for (half, d, qc, kc, vc, qk, gcol, beta, decay, m), tinv in zip(inst, tinvs)]
        bas, ows = [], []
        for (half, d, qc, kc, vc, qk, gcol, beta, decay, m), uw in zip(inst, uws):
            glast = gcol[GDN_CHUNK - 1:GDN_CHUNK] if d == 0 else gcol[0:1]
            bas.append(_dg((kc * jnp.exp(glast - gcol)).astype(BF16), uw, _TN))
            ows.append(_dg((qk * decay).astype(BF16), uw, _NN))
        for (half, d, qc, kc, vc, qk, gcol, beta, decay, m), ba, ow in zip(inst, bas, ows):
            chunk0 = p * GDN_STEP + half * GDN_CHUNK
            o_s[d, pl.ds(pl.multiple_of(chunk0, GDN_CHUNK), GDN_CHUNK), :] = ow[:, :HEAD_DIM]
            b_s[d, pl.ds(pl.multiple_of(2 * chunk0, HEAD_DIM), HEAD_DIM), :] = ba[:, :HEAD_DIM]
            aq_s[d, pl.ds(pl.multiple_of(3 * chunk0, GDN_AQ), GDN_AQ), :] = jnp.concatenate(
                [ba[:, HEAD_DIM:], qc * jnp.exp(gcol) - ow[:, HEAD_DIM:]], axis=0).astype(BF16)
        return carry

    lax.fori_loop(0, UNIT // GDN_STEP, phase1, 0)

    lane1 = lax.broadcasted_iota(jnp.int32, (1, LANES), 1)

    def phase2(i, carry):
        chs = (i, n_chunks - 1 - i)
        firsts = [(chs[0] % chunks_per_seq) == 0, (chs[1] % chunks_per_seq) == chunks_per_seq - 1]
        ss = [jnp.where(jnp.logical_and(is_ctx, f), 0.0, s) for f, s in zip(firsts, carry)]
        rs = [_dg(aq_s[d, pl.ds(pl.multiple_of(chs[d] * GDN_AQ, GDN_AQ), GDN_AQ), :],
                  ss[d].astype(BF16), _NN) for d in range(2)]
        new = []
        for d in range(2):
            tile0 = pl.multiple_of(chs[d] * GDN_CHUNK + (GDN_CHUNK - 8 if d == 0 else 0), 8)
            last = col_ref[pl.ds(tile0, 8), :]
            last = last[7:8] if d == 0 else last[0:1]
            glast = jnp.sum(jnp.where(lane1 == d * HEADS + head, last, 0.0), axis=-1, keepdims=True)
            b = b_s[d, pl.ds(pl.multiple_of(chs[d] * HEAD_DIM, HEAD_DIM), HEAD_DIM), :]
            s = ss[d] * jnp.exp(glast) + (b - rs[d][:HEAD_DIM])
            s_ref[chs[d] // chunks_per_seq, d] = s
            new.append(s)
            rows = pl.ds(pl.multiple_of(chs[d] * GDN_CHUNK, GDN_CHUNK), GDN_CHUNK)
            o_s[d, rows, :] = o_s[d, rows, :] + rs[d][HEAD_DIM:]
        return tuple(new)

    lax.fori_loop(0, n_chunks, phase2, (s0_ref[0], s0_ref[1]))
    o_ref[...] = _head_rmsnorm_gate(o_s[0] + o_s[1], g_ref[...], og_ref[...]).astype(o_ref.dtype)


def _gdn_call(z, col, rowt, conv_w, onorm_g, state, layer):
    def zcol(off):
        return pl.BlockSpec((UNIT, HEAD_DIM), lambda u, h: (u, off // HEAD_DIM + h))
    def wcol(part):
        return pl.BlockSpec((CONV_K * CONV_K, HEAD_DIM), lambda u, h: (0, part * HEADS + h))
    scr = lambda *shape: pltpu.VMEM(shape, F32)
    return pl.pallas_call(
        _gdn_kernel,
        grid=(N_UNITS, HEADS),
        in_specs=[zcol(OFF_GQ), zcol(OFF_GK), zcol(OFF_GV), zcol(OFF_GG),
                  pl.BlockSpec((UNIT, LANES), lambda u, h: (u, 0)),
                  pl.BlockSpec((LANES, UNIT), lambda u, h: (0, u)),
                  wcol(0), wcol(1), wcol(2),
                  pl.BlockSpec((1, HEAD_DIM), lambda u, h: (0, 0)),
                  pl.BlockSpec((None, None, 2, None, HEAD_DIM, HEAD_DIM),
                               lambda u, h: (jnp.maximum(u - N_CTX_UNITS, 0), layer, 0, h, 0, 0))],
        out_specs=[pl.BlockSpec((UNIT, HEAD_DIM), lambda u, h: (u, h)),
                   pl.BlockSpec((None, SEQ_PER_UNIT, 2, None, HEAD_DIM, HEAD_DIM),
                                lambda u, h: (u, 0, 0, h, 0, 0))],
        out_shape=[jax.ShapeDtypeStruct((N_TOK, HG_F), BF16),
                   jax.ShapeDtypeStruct((N_UNITS, SEQ_PER_UNIT, 2, HEADS, HEAD_DIM, HEAD_DIM), F32)],
        scratch_shapes=[scr(UNIT + 2 * CONV_PAD, HEAD_DIM),
                        scr(UNIT, HEAD_DIM), scr(UNIT, HEAD_DIM), scr(UNIT, HEAD_DIM),
                        scr(2, UNIT, HEAD_DIM),
                        scr(2, (UNIT // GDN_CHUNK) * HEAD_DIM, HEAD_DIM),
                        pltpu.VMEM((2, (UNIT // GDN_CHUNK) * GDN_AQ, HEAD_DIM), BF16)],
        compiler_params=_cparams(("arbitrary", "arbitrary")),
        name="gdn_scan",
    )(z, z, z, z, col, rowt, conv_w, conv_w, conv_w, onorm_g.reshape(1, HEAD_DIM), state)


def kernel(x_prompt, x_sample, c, state_hgrn, state_gdn, c_ctx, norm1_g, norm2_g, w_mod, b_mod, w_in, hg_lb, hg_onorm_g, cm_vnorm_g, cm_ws, cm_bs, gdn_conv, gdn_A_log, gdn_dt_bias, gdn_onorm_g, w_br_hg, w_br_cm, w_br_gdn, w_out, w_ff1, w_ff2, final_g):
    x = jnp.concatenate([x_prompt.reshape(N_CTX_TOK, D_MODEL), x_sample.reshape(N_LAT_TOK, D_MODEL)], axis=0)
    cvec = jnp.concatenate([c_ctx[None, :], c, jnp.zeros((MOD_ROWS - 1 - DEC_BATCH, D_MODEL), F32)], axis=0)
    mod3 = _mod_call(cvec, w_mod, b_mod).reshape(DEPTH * MOD_ROWS, 1, 6 * D_MODEL)

    lb_all = jnp.cumsum(jax.nn.softmax(hg_lb.astype(F32), axis=0), axis=0)
    lb_all = lb_all - lb_all[:1]

    new_hg, new_gdn = [], []
    for l in range(DEPTH):
        w_main = w_in[l, :, :N_MAIN].astype(BF16)
        w_ab = jnp.pad(w_in[l, :, OFF_AB:OFF_GATES], ((0, 0), (0, LANES - 4 * HEADS))).astype(BF16)
        w_gates = w_in[l, :, OFF_GATES:].astype(BF16)

        h = _normmod_call(x, norm1_g[l], mod3, l, 0, 1)
        z = _mm_call(h, w_main, act=None, out_dtype=F32, tm=1024, tn=1024, name="in_proj")
        ab = _mm_call(h, w_ab, act=None, out_dtype=F32, tm=1024, tn=LANES, name="in_proj_ab")
        gates = _mm_call(h, w_gates, act="sigmoid", out_dtype=F32, tm=1024, tn=1024, name="in_proj_gates")

        o_a, s_hg = _hgrn_call(z, lb_all[l], hg_onorm_g[l], state_hgrn, l)
        o_b = _gmlp_call(z, cm_vnorm_g[l], cm_ws[l], cm_bs[l].T)
        col, rowt = _gdn_gates_call(ab, gdn_A_log[l], gdn_dt_bias[l])
        o_c, s_gdn = _gdn_call(z, col, rowt, gdn_conv[l].reshape(CONV_K * CONV_K, 3 * HG_F),
                               gdn_onorm_g[l], state_gdn, l)
        new_hg.append(s_hg[:N_CTX_UNITS].reshape(BATCH, 2, HEADS, HEAD_DIM, HEAD_DIM))
        new_gdn.append(s_gdn[:N_CTX_UNITS].reshape(BATCH, 2, HEADS, HEAD_DIM, HEAD_DIM))

        merged = _merge_call(o_a, o_b, o_c, w_br_hg[l].astype(BF16), w_br_cm[l].astype(BF16),
                             w_br_gdn[l].astype(BF16), gates)
        x = _mm_resid_call(merged, w_out[l].astype(BF16), x, mod3, l, 2, tm=1024, tn=1024, name="out_proj")

        h2 = _normmod_call(x, norm2_g[l], mod3, l, 3, 4)
        up = _mm_call(h2, w_ff1[l].astype(BF16), act="relu2", out_dtype=BF16, tm=1024, tn=1024, name="ffn_up")
        x = _mm_resid_call(up, w_ff2[l].astype(BF16), x, mod3, l, 5, tm=512, tn=512, name="ffn_down")

    y = _final_norm_call(x, final_g)
    y_prompt = y[:N_CTX_TOK].reshape(BATCH, SEQ, D_MODEL)
    y_sample = y[N_CTX_TOK:].reshape(DEC_BATCH, DEC_SEQ, D_MODEL)
    return (y_prompt, y_sample, jnp.stack(new_hg, axis=1), jnp.stack(new_gdn, axis=1))
```

```python
import functools

import jax
import jax.numpy as jnp
from jax import lax
from jax.experimental import pallas as pl
from jax.experimental.pallas import tpu as pltpu

F32 = jnp.float32
BF16 = jnp.bfloat16

D_MODEL = 2048
BATCH = 16
SEQ = 256
DEPTH = 2
DEC_BATCH = 4
DEC_SEQ = 1024
GRID_W = 64
EPS = 1e-6
D_FF = 4 * D_MODEL
HEADS = 8
HEAD_DIM = 128
HG_F = HEADS * HEAD_DIM
CM_GROUPS = 8
CM_W = CM_GROUPS * HEAD_DIM
CM_CHUNK = 128
GDN_CHUNK = 64
CONV_K = 3

OFF_HQ, OFF_HI, OFF_HG, OFF_HFF, OFF_HFB = 0, 1024, 2048, 3072, 4096
OFF_CU, OFF_CV = 5120, 6144
OFF_GQ, OFF_GK, OFF_GV, OFF_GG = 7168, 8192, 9216, 10240
OFF_AB = 11264
OFF_GATES = 11296
IN_DIM = 17440
N_MAIN = OFF_AB

N_CTX_TOK = BATCH * SEQ
N_LAT_TOK = DEC_BATCH * DEC_SEQ
N_TOK = N_CTX_TOK + N_LAT_TOK
UNIT = DEC_SEQ
N_CTX_UNITS = N_CTX_TOK // UNIT
N_UNITS = N_TOK // UNIT
SEQ_PER_UNIT = UNIT // SEQ

LANES = 128
MOD_ROWS = 8
HG_BLOCK = 256
HG_SUB = 32
HG_INST = 2 * (UNIT // HG_BLOCK)
VMEM_LIMIT = 56 * 1024 * 1024

_NT = (((1,), (1,)), ((), ()))
_TN = (((0,), (0,)), ((), ()))
_NN = (((1,), (0,)), ((), ()))


def _dg(a, b, dims):
    return lax.dot_general(a, b, dims, preferred_element_type=F32)


def _split2(x):
    hi = x.astype(BF16)
    lo = (x - hi.astype(F32)).astype(BF16)
    return hi, lo


def _dot1(a, b, dims=_NN):
    return _dg(a.astype(BF16), b.astype(BF16), dims)


def _dot3(a, b, dims=_NN):
    ah, al = _split2(a)
    bh, bl = _split2(b)
    return _dg(ah, bh, dims) + (_dg(ah, bl, dims) + _dg(al, bh, dims))


_dot_inv = _dot1


def _dot01(m, x):
    hi = x.astype(BF16)
    r = x - hi.astype(F32)
    mid = r.astype(BF16)
    lo = (r - mid.astype(F32)).astype(BF16)
    return _dg(m, hi, _NN) + (_dg(m, mid, _NN) + _dg(m, lo, _NN))


def _sigmoid(x):
    return 1.0 / (1.0 + jnp.exp(-x))


def _silu(x):
    return x * _sigmoid(x)


def _gelu(x):
    return 0.5 * x * (1.0 + lax.erf(x * (2.0 ** -0.5)))


def _softplus(x):
    return jnp.maximum(x, 0.0) + jnp.log1p(jnp.exp(-jnp.abs(x)))


def _mod_row_of_tile(i, tm):
    return jnp.maximum(0, (i * tm - N_CTX_TOK) // DEC_SEQ + 1)


def _cparams(sem):
    return pltpu.CompilerParams(dimension_semantics=sem, vmem_limit_bytes=VMEM_LIMIT)


def _mod_kernel(c_ref, w_ref, b_ref, o_ref):
    s = _silu(c_ref[...])
    hi, lo = _split2(s)
    w = w_ref[...].astype(BF16)
    o_ref[...] = _dg(hi, w, _NN) + _dg(lo, w, _NN) + b_ref[...]


def _mod_call(cvec, w_mod, b_mod):
    tn = 1024
    n = 6 * D_MODEL
    return pl.pallas_call(
        _mod_kernel,
        grid=(DEPTH, n // tn),
        in_specs=[pl.BlockSpec((MOD_ROWS, D_MODEL), lambda l, j: (0, 0)),
                  pl.BlockSpec((None, D_MODEL, tn), lambda l, j: (l, 0, j)),
                  pl.BlockSpec((None, 1, tn), lambda l, j: (l, 0, j))],
        out_specs=pl.BlockSpec((None, MOD_ROWS, tn), lambda l, j: (l, 0, j)),
        out_shape=jax.ShapeDtypeStruct((DEPTH, MOD_ROWS, n), F32),
        compiler_params=_cparams(("arbitrary", "arbitrary")),
        name="modulation",
    )(cvec, w_mod, b_mod.reshape(DEPTH, 1, n))


def _normmod_kernel(x_ref, g_ref, sh_ref, sc_ref, o_ref):
    x = x_ref[...]
    y = x * lax.rsqrt(jnp.mean(x * x, axis=-1, keepdims=True) + EPS) * g_ref[...]
    o_ref[...] = (y * (1.0 + sc_ref[...]) + sh_ref[...]).astype(o_ref.dtype)


def _normmod_call(x, g, mod3, layer, k_shift, k_scale):
    tm = 256
    def mod_spec(k):
        return pl.BlockSpec((None, 1, D_MODEL),
                            lambda i: (layer * MOD_ROWS + _mod_row_of_tile(i, tm), 0, k))
    return pl.pallas_call(
        _normmod_kernel,
        grid=(N_TOK // tm,),
        in_specs=[pl.BlockSpec((tm, D_MODEL), lambda i: (i, 0)),
                  pl.BlockSpec((1, D_MODEL), lambda i: (0, 0)),
                  mod_spec(k_shift), mod_spec(k_scale)],
        out_specs=pl.BlockSpec((tm, D_MODEL), lambda i: (i, 0)),
        out_shape=jax.ShapeDtypeStruct((N_TOK, D_MODEL), BF16),
        compiler_params=_cparams(("arbitrary",)),
        name="norm_mod",
    )(x, g.reshape(1, D_MODEL), mod3, mod3)


def _final_norm_kernel(x_ref, g_ref, o_ref):
    x = x_ref[...]
    o_ref[...] = x * lax.rsqrt(jnp.mean(x * x, axis=-1, keepdims=True) + EPS) * g_ref[...]


def _final_norm_call(x, g):
    tm = 256
    return pl.pallas_call(
        _final_norm_kernel,
        grid=(N_TOK // tm,),
        in_specs=[pl.BlockSpec((tm, D_MODEL), lambda i: (i, 0)),
                  pl.BlockSpec((1, D_MODEL), lambda i: (0, 0))],
        out_specs=pl.BlockSpec((tm, D_MODEL), lambda i: (i, 0)),
        out_shape=jax.ShapeDtypeStruct((N_TOK, D_MODEL), F32),
        compiler_params=_cparams(("arbitrary",)),
        name="final_norm",
    )(x, g.reshape(1, D_MODEL))


def _mm_kernel(*refs, act, lane_shift, has_resid):
    x_ref, w_ref = refs[:2]
    refs = refs[2:]
    if lane_shift:
        wo_ref, refs = refs[0], refs[1:]
    if has_resid:
        r_ref, g_ref = refs[:2]
        refs = refs[2:]
    o_ref, wbf_ref = refs
    tn = wbf_ref.shape[1]

    @pl.when(pl.program_id(1) == 0)
    def _cast_weights():
        if lane_shift:
            w = jnp.concatenate([w_ref[...], wo_ref[...]], axis=1)[:, lane_shift:lane_shift + tn]
        else:
            w = w_ref[...]
        wbf_ref[...] = w.astype(BF16)

    acc = jnp.dot(x_ref[...], wbf_ref[...], preferred_element_type=F32)
    if act == "sigmoid":
        acc = _sigmoid(acc)
    elif act == "relu2":
        acc = jnp.square(jnp.maximum(acc, 0.0))
    if has_resid:
        acc = r_ref[...] + g_ref[...] * acc
    o_ref[...] = acc.astype(o_ref.dtype)


def _mm_call(x, w_all, layer, *, col0, n, act, out_dtype, tm, tn, name, resid=None, mod3=None, k_gate=None):
    m, k = x.shape
    lane_shift = col0 % LANES
    blk0 = (col0 - lane_shift) // tn
    assert (col0 - lane_shift) % tn == 0 and n % tn == 0 and m % tm == 0
    in_specs = [pl.BlockSpec((tm, k), lambda j, i: (i, 0)),
                pl.BlockSpec((None, k, tn), lambda j, i: (layer, 0, blk0 + j))]
    args = [x, w_all]
    if lane_shift:
        in_specs.append(pl.BlockSpec((None, k, LANES), lambda j, i: (layer, 0, (blk0 + j + 1) * (tn // LANES))))
        args.append(w_all)
    if resid is not None:
        per_tile = D_MODEL // tn
        in_specs += [pl.BlockSpec((tm, tn), lambda j, i: (i, j)),
                     pl.BlockSpec((None, 1, tn),
                                  lambda j, i: (layer * MOD_ROWS + _mod_row_of_tile(i, tm), 0,
                                                k_gate * per_tile + j))]
        args += [resid, mod3]
    return pl.pallas_call(
        functools.partial(_mm_kernel, act=act, lane_shift=lane_shift, has_resid=resid is not None),
        grid=(n // tn, m // tm),
        in_specs=in_specs,
        out_specs=pl.BlockSpec((tm, tn), lambda j, i: (i, j)),
        out_shape=jax.ShapeDtypeStruct((m, n), out_dtype),
        scratch_shapes=[pltpu.VMEM((k, tn), BF16)],
        compiler_params=_cparams(("arbitrary", "arbitrary")),
        name=name,
    )(*args)


def _mm_bf16_kernel(x_ref, w_ref, r_ref, g_ref, o_ref):
    acc = jnp.dot(x_ref[...], w_ref[...], preferred_element_type=F32)
    o_ref[...] = r_ref[...] + g_ref[...] * acc


def _mm_resid_call(x, w, resid, mod3, layer, k_gate, *, tm, tn, name):
    m, k = x.shape
    n = w.shape[1]
    per_tile = D_MODEL // tn
    return pl.pallas_call(
        _mm_bf16_kernel,
        grid=(m // tm, n // tn),
        in_specs=[pl.BlockSpec((tm, k), lambda i, j: (i, 0)),
                  pl.BlockSpec((k, tn), lambda i, j: (0, j)),
                  pl.BlockSpec((tm, tn), lambda i, j: (i, j)),
                  pl.BlockSpec((None, 1, tn),
                               lambda i, j: (layer * MOD_ROWS + _mod_row_of_tile(i, tm), 0,
                                             k_gate * per_tile + j))],
        out_specs=pl.BlockSpec((tm, tn), lambda i, j: (i, j)),
        out_shape=jax.ShapeDtypeStruct((m, n), F32),
        compiler_params=_cparams(("arbitrary", "arbitrary")),
        name=name,
    )(x, w, resid, mod3)


def _merge_kernel(oa_ref, ob_ref, oc_ref, wa_ref, wb_ref, wc_ref, ga_ref, gb_ref, gc_ref, o_ref, wbf_ref):
    @pl.when(pl.program_id(1) == 0)
    def _cast_weights():
        wbf_ref[0] = wa_ref[...].astype(BF16)
        wbf_ref[1] = wb_ref[...].astype(BF16)
        wbf_ref[2] = wc_ref[...].astype(BF16)

    a = jnp.dot(oa_ref[...], wbf_ref[0], preferred_element_type=F32)
    b = jnp.dot(ob_ref[...], wbf_ref[1], preferred_element_type=F32)
    c = jnp.dot(oc_ref[...], wbf_ref[2], preferred_element_type=F32)
    o_ref[...] = (ga_ref[...] * a + gb_ref[...] * b + gc_ref[...] * c).astype(o_ref.dtype)


def _merge_call(o_a, o_b, o_c, w_a, w_b, w_c, gates, layer):
    tm, tn = 1024, 512
    nj = D_MODEL // tn
    br = pl.BlockSpec((tm, HG_F), lambda j, i: (i, 0))
    wt = pl.BlockSpec((None, HG_F, tn), lambda j, i: (layer, 0, j))
    def gate_spec(k):
        return pl.BlockSpec((tm, tn), lambda j, i: (i, k * nj + j))
    return pl.pallas_call(
        _merge_kernel,
        grid=(nj, N_TOK // tm),
        in_specs=[br, br, br, wt, wt, wt, gate_spec(0), gate_spec(1), gate_spec(2)],
        out_specs=pl.BlockSpec((tm, tn), lambda j, i: (i, j)),
        out_shape=jax.ShapeDtypeStruct((N_TOK, D_MODEL), BF16),
        scratch_shapes=[pltpu.VMEM((3, HG_F, tn), BF16)],
        compiler_params=_cparams(("arbitrary", "arbitrary")),
        name="branch_merge",
    )(o_a, o_b, o_c, w_a, w_b, w_c, gates, gates, gates)


def _head_rmsnorm_gate(o, g, og):
    y = o * lax.rsqrt(jnp.mean(o * o, axis=-1, keepdims=True) + EPS) * g
    return y * _silu(og)


def _hgrn_kernel(q_ref, i_ref, og_ref, ff_ref, fb_ref, lb_ref, g_ref, s0_ref, o_ref, s_ref,
                 acc_ref, b_ref, stc_ref, km_ref):
    is_ctx = pl.program_id(0) < N_CTX_UNITS
    nblk = UNIT // HG_BLOCK
    nsub = HG_BLOCK // HG_SUB
    r = lax.broadcasted_iota(jnp.int32, (HG_BLOCK, HG_BLOCK), 0)
    c = lax.broadcasted_iota(jnp.int32, (HG_BLOCK, HG_BLOCK), 1)
    same = (r // HG_SUB) == (c // HG_SUB)
    row_sub = lax.broadcasted_iota(jnp.int32, (HG_BLOCK, HEAD_DIM), 0) // HG_SUB
    tri = (jnp.logical_and(same, c <= r), jnp.logical_and(same, c >= r))
    tri_bf = [jnp.where(t, 1.0, 0.0).astype(BF16) for t in tri]
    z_refs = (ff_ref, fb_ref)
    inst = [(blk, d) for blk in range(nblk) for d in range(2)]

    def expand(x):
        return jnp.broadcast_to(x[:, None, :], (nsub, HG_SUB, HEAD_DIM)).reshape(HG_BLOCK, HEAD_DIM)

    def blk_rows(blk):
        return slice(blk * HG_BLOCK, (blk + 1) * HG_BLOCK)

    qs = [_silu(q_ref[blk_rows(blk), :]) for blk in range(nblk)]
    vs = [i_ref[blk_rows(blk), :] for blk in range(nblk)]
    ks, lfs = [], []
    for blk, d in inst:
        lb = lb_ref[d:d + 1, :]
        f = lb + (1.0 - lb) * _sigmoid(z_refs[d][blk_rows(blk), :])
        lfs.append(jnp.log(f))
        ks.append(1.0 - f)
    bs = [_dot01(tri_bf[d], lf) for (blk, d), lf in zip(inst, lfs)]
    tots, qts, kts, qds = [], [], [], []
    for n, (blk, d) in enumerate(inst):
        b = bs[n]
        b_ref[n] = b
        tot = b_ref[n, pl.ds((HG_SUB - 1) if d == 0 else 0, nsub, stride=HG_SUB), :]
        mid_f = expand(b_ref[n, pl.ds(HG_SUB // 2, nsub, stride=HG_SUB), :])
        tots.append(tot)
        qts.append(qs[blk] * jnp.exp(b - mid_f))
        kts.append(ks[n] * jnp.exp(mid_f - b))
        qds.append((qs[blk] * jnp.exp(b)).astype(BF16))
        kd = ks[n] * jnp.exp(expand(tot) - b)
        for s in range(nsub):
            km_ref[n, :, s * HEAD_DIM:(s + 1) * HEAD_DIM] = jnp.where(row_sub == s, kd, 0.0).astype(BF16)
    scs = [jnp.where(tri[d], _dot3(qt, kt, _NT), 0.0) for (blk, d), qt, kt in zip(inst, qts, kts)]
    uts = [_dg(vs[blk].astype(BF16), km_ref[n], _TN) for n, (blk, d) in enumerate(inst)]
    outs = [_dot1(sc, vs[blk]) for (blk, d), sc in zip(inst, scs)]
    for d in range(2):
        st = s0_ref[d].T
        for blk in (range(nblk) if d == 0 else range(nblk - 1, -1, -1)):
            n = inst.index((blk, d))
            st = jnp.where(is_ctx, 0.0, st)
            for s in (range(nsub) if d == 0 else range(nsub - 1, -1, -1)):
                stc_ref[n, s] = st.astype(BF16)
                st = st * jnp.exp(tots[n][s:s + 1, :]) + uts[n][:, s * HEAD_DIM:(s + 1) * HEAD_DIM]
            s_ref[blk, d] = st.T
    for n, (blk, d) in enumerate(inst):
        o_int = [_dg(qds[n][s * HG_SUB:(s + 1) * HG_SUB], stc_ref[n, s], _NT) for s in range(nsub)]
        acc_ref[d, blk_rows(blk), :] = outs[n] + jnp.concatenate(o_int, axis=0)
    o_ref[...] = _head_rmsnorm_gate(acc_ref[0] + acc_ref[1], g_ref[...], og_ref[...]).astype(o_ref.dtype)


def _hgrn_call(z, lb, onorm_g, state, layer):
    def col(off):
        return pl.BlockSpec((UNIT, HEAD_DIM), lambda u, h: (u, off // HEAD_DIM + h))
    return pl.pallas_call(
        _hgrn_kernel,
        grid=(N_UNITS, HEADS),
        in_specs=[col(OFF_HQ), col(OFF_HI), col(OFF_HG), col(OFF_HFF), col(OFF_HFB),
                  pl.BlockSpec((2, HEAD_DIM), lambda u, h: (0, h)),
                  pl.BlockSpec((1, HEAD_DIM), lambda u, h: (0, 0)),
                  pl.BlockSpec((None, None, 2, None, HEAD_DIM, HEAD_DIM),
                               lambda u, h: (jnp.maximum(u - N_CTX_UNITS, 0), layer, 0, h, 0, 0))],
        out_specs=[pl.BlockSpec((UNIT, HEAD_DIM), lambda u, h: (u, h)),
                   pl.BlockSpec((None, SEQ_PER_UNIT, 2, None, HEAD_DIM, HEAD_DIM),
                                lambda u, h: (u, 0, 0, h, 0, 0))],
        out_shape=[jax.ShapeDtypeStruct((N_TOK, HG_F), BF16),
                   jax.ShapeDtypeStruct((N_UNITS, SEQ_PER_UNIT, 2, HEADS, HEAD_DIM, HEAD_DIM), F32)],
        scratch_shapes=[pltpu.VMEM((2, UNIT, HEAD_DIM), F32),
                        pltpu.VMEM((HG_INST, HG_BLOCK, HEAD_DIM), F32),
                        pltpu.VMEM((HG_INST, HG_BLOCK // HG_SUB, HEAD_DIM, HEAD_DIM), BF16),
                        pltpu.VMEM((HG_INST, HG_BLOCK, (HG_BLOCK // HG_SUB) * HEAD_DIM), BF16)],
        compiler_params=_cparams(("arbitrary", "arbitrary")),
        name="hgrn2_scan",
    )(z, z, z, z, z, lb, onorm_g.reshape(1, HEAD_DIM), state)


def _gmlp_kernel(u_ref, v_ref, vn_ref, ws_ref, bs_ref, o_ref):
    tm = u_ref.shape[0]
    for g in range(CM_GROUPS):
        cols = slice(g * HEAD_DIM, (g + 1) * HEAD_DIM)
        vg = _gelu(v_ref[:, cols])
        vg = vg * lax.rsqrt(jnp.mean(vg * vg, axis=-1, keepdims=True) + EPS) * vn_ref[:, cols]
        ug = _gelu(u_ref[:, cols])
        w = ws_ref[g]
        bias = bs_ref[:, g:g + 1]
        for ch in range(tm // CM_CHUNK):
            rows = slice(ch * CM_CHUNK, (ch + 1) * CM_CHUNK)
            s = _dot3(w, vg[rows]) + bias
            o_ref[rows, cols] = (ug[rows] * s).astype(o_ref.dtype)


def _gmlp_call(z, vnorm_g, ws, bs_t):
    tm = 512
    return pl.pallas_call(
        _gmlp_kernel,
        grid=(N_TOK // tm,),
        in_specs=[pl.BlockSpec((tm, CM_W), lambda i: (i, OFF_CU // CM_W)),
                  pl.BlockSpec((tm, CM_W), lambda i: (i, OFF_CV // CM_W)),
                  pl.BlockSpec((1, CM_W), lambda i: (0, 0)),
                  pl.BlockSpec((CM_GROUPS, CM_CHUNK, CM_CHUNK), lambda i: (0, 0, 0)),
                  pl.BlockSpec((CM_CHUNK, CM_GROUPS), lambda i: (0, 0))],
        out_specs=pl.BlockSpec((tm, CM_W), lambda i: (i, 0)),
        out_shape=jax.ShapeDtypeStruct((N_TOK, CM_W), BF16),
        compiler_params=_cparams(("arbitrary",)),
        name="chunk_gmlp",
    )(z, z, vnorm_g.reshape(1, CM_W), ws, bs_t)


GP_BLOCK = 256


def _gdn_gates_kernel(ab_ref, alog_ref, dt_ref, col_ref, rowt_ref):
    ab = ab_ref[...]
    lane = lax.broadcasted_iota(jnp.int32, ab.shape, 1)
    g = jnp.where(lane < 2 * HEADS, -jnp.exp(alog_ref[...]) * _softplus(ab + dt_ref[...]), 0.0)
    r = lax.broadcasted_iota(jnp.int32, (GP_BLOCK, GP_BLOCK), 0)
    c = lax.broadcasted_iota(jnp.int32, (GP_BLOCK, GP_BLOCK), 1)
    same = (r // GDN_CHUNK) == (c // GDN_CHUNK)
    tri_f = jnp.where(jnp.logical_and(same, c <= r), 1.0, 0.0).astype(BF16)
    tri_b = jnp.where(jnp.logical_and(same, c >= r), 1.0, 0.0).astype(BF16)
    cf = _dot01(tri_f, g)
    cb = _dot01(tri_b, g)
    col = jnp.where(lane < HEADS, cf, jnp.where(lane < 2 * HEADS, cb, _sigmoid(ab)))
    col_ref[...] = col
    rowt_ref[...] = col.T


def _gdn_gates_call(ab, a_log, dt_bias):
    pad = lambda t: jnp.pad(t.reshape(1, 2 * HEADS), ((0, 0), (0, LANES - 2 * HEADS)))
    return pl.pallas_call(
        _gdn_gates_kernel,
        grid=(N_TOK // GP_BLOCK,),
        in_specs=[pl.BlockSpec((GP_BLOCK, LANES), lambda i: (i, 0)),
                  pl.BlockSpec((1, LANES), lambda i: (0, 0)),
                  pl.BlockSpec((1, LANES), lambda i: (0, 0))],
        out_specs=[pl.BlockSpec((GP_BLOCK, LANES), lambda i: (i, 0)),
                   pl.BlockSpec((LANES, GP_BLOCK), lambda i: (0, i))],
        out_shape=[jax.ShapeDtypeStruct((N_TOK, LANES), F32),
                   jax.ShapeDtypeStruct((LANES, N_TOK), F32)],
        compiler_params=_cparams(("arbitrary",)),
        name="gdn_gates",
    )(ab, pad(a_log), pad(dt_bias))


CONV_PAD = 72
GDN_AQ = HEAD_DIM + GDN_CHUNK
GDN_STEP = 8 * GDN_CHUNK


def _gdn_kernel(qr_ref, kr_ref, vr_ref, og_ref, col_ref, rowt_ref, cwq_ref, cwk_ref, cwv_ref,
                g_ref, s0_ref, o_ref, s_ref,
                xp_ref, q_s, k_s, v_s, o_s, b_s, aq_s):
    is_ctx = pl.program_id(0) < N_CTX_UNITS
    head = pl.program_id(1)
    n_chunks = UNIT // GDN_CHUNK
    chunks_per_seq = SEQ // GDN_CHUNK

    t = lax.broadcasted_iota(jnp.int32, (UNIT, 1), 0)
    period = jnp.where(is_ctx, SEQ, GRID_W)
    pos = jnp.bitwise_and(t, period - 1)
    ok_left = pos != 0
    ok_right = pos != period - 1
    zeros_pad = jnp.zeros((CONV_PAD, HEAD_DIM), F32)
    xp_ref[0:CONV_PAD, :] = zeros_pad
    xp_ref[CONV_PAD + UNIT:CONV_PAD + UNIT + CONV_PAD, :] = zeros_pad

    def conv_silu(x_ref, w_ref):
        xp_ref[CONV_PAD:CONV_PAD + UNIT, :] = x_ref[...]
        acc = jnp.zeros((UNIT, HEAD_DIM), F32)
        for i in range(CONV_K):
            for j in range(CONV_K):
                w = w_ref[CONV_K * i + j:CONV_K * i + j + 1, :]
                if i != CONV_K // 2:
                    w = jnp.where(is_ctx, 0.0, w)
                start = CONV_PAD + (i - 1) * GRID_W + (j - 1)
                xs = xp_ref[start:start + UNIT, :]
                if j == 0:
                    xs = jnp.where(ok_left, xs, 0.0)
                elif j == CONV_K - 1:
                    xs = jnp.where(ok_right, xs, 0.0)
                acc = acc + xs * w
        return _silu(acc)

    def l2norm(x):
        return x * lax.rsqrt(jnp.sum(x * x, axis=-1, keepdims=True) + EPS)

    q_s[...] = l2norm(conv_silu(qr_ref, cwq_ref)) * (HEAD_DIM ** -0.5)
    k_s[...] = l2norm(conv_silu(kr_ref, cwk_ref))
    v_s[...] = conv_silu(vr_ref, cwv_ref)

    rr = lax.broadcasted_iota(jnp.int32, (GDN_CHUNK, GDN_CHUNK), 0)
    cc = lax.broadcasted_iota(jnp.int32, (GDN_CHUNK, GDN_CHUNK), 1)
    eye = jnp.where(rr == cc, 1.0, 0.0)
    same_blk = [(rr // b) == (cc // b) for b in (8, 16, 32, 64)]
    lane = lax.broadcasted_iota(jnp.int32, (GDN_STEP, LANES), 1)
    sub8 = lax.broadcasted_iota(jnp.int32, (HEADS, GDN_STEP), 0)

    def pick(x, j):
        return jnp.sum(jnp.where(lane[:x.shape[0]] == j, x, 0.0), axis=-1, keepdims=True)

    incl = (cc <= rr, cc >= rr)
    strict = (cc < rr, cc > rr)
    off_masks = [jnp.logical_and(same_blk[lvl], jnp.logical_not(same_blk[lvl - 1]))
                 for lvl in range(1, len(same_blk))]

    def phase1(p, carry):
        rows = pl.ds(pl.multiple_of(p * GDN_STEP, GDN_STEP), GDN_STEP)
        q2, k2, v2 = q_s[rows, :], k_s[rows, :], v_s[rows, :]
        col = col_ref[rows, :]
        gcols = [pick(col, d * HEADS + head) for d in range(2)]
        betas = [pick(col, (2 + d) * HEADS + head) for d in range(2)]
        grows = [jnp.sum(jnp.where(sub8 == head, rowt_ref[d * HEADS:(d + 1) * HEADS, rows], 0.0),
                         axis=0, keepdims=True) for d in range(2)]
        inst = []
        for half in range(GDN_STEP // GDN_CHUNK):
            sl = slice(half * GDN_CHUNK, (half + 1) * GDN_CHUNK)
            qc, kc, vc = q2[sl], k2[sl], v2[sl]
            kk = _dot1(kc, kc, _NT)
            qk = _dot1(qc, kc, _NT)
            for d in range(2):
                gcol, beta = gcols[d][sl], betas[d][sl]
                decay = jnp.where(incl[d], jnp.exp(gcol - grows[d][:, sl]), 0.0)
                m = jnp.where(strict[d], beta * kk * decay, 0.0)
                inst.append((half, d, qc, kc, vc, qk, gcol, beta, decay, m))
        ms = [t[-1] for t in inst]
        pws = [jnp.where(same_blk[0], -m, 0.0) for m in ms]
        tinvs = [eye + pw for pw in pws]
        for _ in range(2):
            pws = [_dot_inv(pw, pw) for pw in pws]
            tinvs = [t + _dot_inv(t, pw) for t, pw in zip(tinvs, pws)]
        for mask in off_masks:
            tmp = [_dot_inv(jnp.where(mask, m, 0.0), t) for m, t in zip(ms, tinvs)]
            tinvs = [t - _dot_inv(t, x) for t, x in zip(tinvs, tmp)]
        uws = [_dot1(tinv, jnp.concatenate([vc * beta, kc * (beta * jnp.exp(gcol))], axis=1)).astype(BF16)
               for (half, d, qc, kc, vc, qk, gcol, beta, decay, m), tinv in zip(inst, tinvs)]
        bas, ows = [], []
        for (half, d, qc, kc, vc, qk, gcol, beta, decay, m), uw in zip(inst, uws):
            glast = gcol[GDN_CHUNK - 1:GDN_CHUNK] if d == 0 else gcol[0:1]
            bas.append(_dg((kc * jnp.exp(glast - gcol)).astype(BF16), uw, _TN))
            ows.append(_dg((qk * decay).astype(BF16), uw, _NN))
        for (half, d, qc, kc, vc, qk, gcol, beta, decay, m), ba, ow in zip(inst, bas, ows):
            chunk0 = p * GDN_STEP + half * GDN_CHUNK
            o_s[d, pl.ds(pl.multiple_of(chunk0, GDN_CHUNK), GDN_CHUNK), :] = ow[:, :HEAD_DIM]
            b_s[d, pl.ds(pl.multiple_of(2 * chunk0, HEAD_DIM), HEAD_DIM), :] = ba[:, :HEAD_DIM]
            aq_s[d, pl.ds(pl.multiple_of(3 * chunk0, GDN_AQ), GDN_AQ), :] = jnp.concatenate(
                [ba[:, HEAD_DIM:], qc * jnp.exp(gcol) - ow[:, HEAD_DIM:]], axis=0).astype(BF16)
        return carry

    lax.fori_loop(0, UNIT // GDN_STEP, phase1, 0)

    lane1 = lax.broadcasted_iota(jnp.int32, (1, LANES), 1)

    def phase2(i, carry):
        chs = (i, n_chunks - 1 - i)
        firsts = [(chs[0] % chunks_per_seq) == 0, (chs[1] % chunks_per_seq) == chunks_per_seq - 1]
        ss = [jnp.where(jnp.logical_and(is_ctx, f), 0.0, s) for f, s in zip(firsts, carry)]
        rs = [_dg(aq_s[d, pl.ds(pl.multiple_of(chs[d] * GDN_AQ, GDN_AQ), GDN_AQ), :],
                  ss[d].astype(BF16), _NN) for d in range(2)]
        new = []
        for d in range(2):
            tile0 = pl.multiple_of(chs[d] * GDN_CHUNK + (GDN_CHUNK - 8 if d == 0 else 0), 8)
            last = col_ref[pl.ds(tile0, 8), :]
            last = last[7:8] if d == 0 else last[0:1]
            glast = jnp.sum(jnp.where(lane1 == d * HEADS + head, last, 0.0), axis=-1, keepdims=True)
            b = b_s[d, pl.ds(pl.multiple_of(chs[d] * HEAD_DIM, HEAD_DIM), HEAD_DIM), :]
            s = ss[d] * jnp.exp(glast) + (b - rs[d][:HEAD_DIM])
            s_ref[chs[d] // chunks_per_seq, d] = s
            new.append(s)
            rows = pl.ds(pl.multiple_of(chs[d] * GDN_CHUNK, GDN_CHUNK), GDN_CHUNK)
            o_s[d, rows, :] = o_s[d, rows, :] + rs[d][HEAD_DIM:]
        return tuple(new)

    lax.fori_loop(0, n_chunks, phase2, (s0_ref[0], s0_ref[1]))
    o_ref[...] = _head_rmsnorm_gate(o_s[0] + o_s[1], g_ref[...], og_ref[...]).astype(o_ref.dtype)


def _gdn_call(z, col, rowt, conv_w, onorm_g, state, layer):
    def zcol(off):
        return pl.BlockSpec((UNIT, HEAD_DIM), lambda u, h: (u, off // HEAD_DIM + h))
    def wcol(part):
        return pl.BlockSpec((CONV_K * CONV_K, HEAD_DIM), lambda u, h: (0, part * HEADS + h))
    scr = lambda *shape: pltpu.VMEM(shape, F32)
    return pl.pallas_call(
        _gdn_kernel,
        grid=(N_UNITS, HEADS),
        in_specs=[zcol(OFF_GQ), zcol(OFF_GK), zcol(OFF_GV), zcol(OFF_GG),
                  pl.BlockSpec((UNIT, LANES), lambda u, h: (u, 0)),
                  pl.BlockSpec((LANES, UNIT), lambda u, h: (0, u)),
                  wcol(0), wcol(1), wcol(2),
                  pl.BlockSpec((1, HEAD_DIM), lambda u, h: (0, 0)),
                  pl.BlockSpec((None, None, 2, None, HEAD_DIM, HEAD_DIM),
                               lambda u, h: (jnp.maximum(u - N_CTX_UNITS, 0), layer, 0, h, 0, 0))],
        out_specs=[pl.BlockSpec((UNIT, HEAD_DIM), lambda u, h: (u, h)),
                   pl.BlockSpec((None, SEQ_PER_UNIT, 2, None, HEAD_DIM, HEAD_DIM),
                                lambda u, h: (u, 0, 0, h, 0, 0))],
        out_shape=[jax.ShapeDtypeStruct((N_TOK, HG_F), BF16),
                   jax.ShapeDtypeStruct((N_UNITS, SEQ_PER_UNIT, 2, HEADS, HEAD_DIM, HEAD_DIM), F32)],
        scratch_shapes=[scr(UNIT + 2 * CONV_PAD, HEAD_DIM),
                        scr(UNIT, HEAD_DIM), scr(UNIT, HEAD_DIM), scr(UNIT, HEAD_DIM),
                        scr(2, UNIT, HEAD_DIM),
                        scr(2, (UNIT // GDN_CHUNK) * HEAD_DIM, HEAD_DIM),
                        pltpu.VMEM((2, (UNIT // GDN_CHUNK) * GDN_AQ, HEAD_DIM), BF16)],
        compiler_params=_cparams(("arbitrary", "arbitrary")),
        name="gdn_scan",
    )(z, z, z, z, col, rowt, conv_w, conv_w, conv_w, onorm_g.reshape(1, HEAD_DIM), state)


def kernel(x_prompt, x_sample, c, state_hgrn, state_gdn, c_ctx, norm1_g, norm2_g, w_mod, b_mod, w_in, hg_lb, hg_onorm_g, cm_vnorm_g, cm_ws, cm_bs, gdn_conv, gdn_A_log, gdn_dt_bias, gdn_onorm_g, w_br_hg, w_br_cm, w_br_gdn, w_out, w_ff1, w_ff2, final_g):
    x = jnp.concatenate([x_prompt.reshape(N_CTX_TOK, D_MODEL), x_sample.reshape(N_LAT_TOK, D_MODEL)], axis=0)
    cvec = jnp.concatenate([c_ctx[None, :], c, jnp.zeros((MOD_ROWS - 1 - DEC_BATCH, D_MODEL), F32)], axis=0)
    mod3 = _mod_call(cvec, w_mod, b_mod).reshape(DEPTH * MOD_ROWS, 1, 6 * D_MODEL)

    lb_all = jnp.cumsum(jax.nn.softmax(hg_lb.astype(F32), axis=0), axis=0)
    lb_all = lb_all - lb_all[:1]

    new_hg, new_gdn = [], []
    for l in range(DEPTH):
        h = _normmod_call(x, norm1_g[l], mod3, l, 0, 1)
        z = _mm_call(h, w_in, l, col0=0, n=N_MAIN, act=None, out_dtype=F32, tm=1024, tn=1024, name="in_proj")
        ab = _mm_call(h, w_in, l, col0=OFF_AB, n=LANES, act=None, out_dtype=F32, tm=1024, tn=LANES,
                      name="in_proj_ab")
        gates = _mm_call(h, w_in, l, col0=OFF_GATES, n=3 * D_MODEL, act="sigmoid", out_dtype=F32,
                         tm=1024, tn=1024, name="in_proj_gates")

        o_a, s_hg = _hgrn_call(z, lb_all[l], hg_onorm_g[l], state_hgrn, l)
        o_b = _gmlp_call(z, cm_vnorm_g[l], cm_ws[l], cm_bs[l].T)
        col, rowt = _gdn_gates_call(ab, gdn_A_log[l], gdn_dt_bias[l])
        o_c, s_gdn = _gdn_call(z, col, rowt, gdn_conv[l].reshape(CONV_K * CONV_K, 3 * HG_F),
                               gdn_onorm_g[l], state_gdn, l)
        new_hg.append(s_hg[:N_CTX_UNITS].reshape(BATCH, 2, HEADS, HEAD_DIM, HEAD_DIM))
        new_gdn.append(s_gdn[:N_CTX_UNITS].reshape(BATCH, 2, HEADS, HEAD_DIM, HEAD_DIM))

        merged = _merge_call(o_a, o_b, o_c, w_br_hg, w_br_cm, w_br_gdn, gates, l)
        x = _mm_call(merged, w_out, l, col0=0, n=D_MODEL, act=None, out_dtype=F32, tm=1024, tn=1024,
                     name="out_proj", resid=x, mod3=mod3, k_gate=2)

        h2 = _normmod_call(x, norm2_g[l], mod3, l, 3, 4)
        up = _mm_call(h2, w_ff1, l, col0=0, n=D_FF, act="relu2", out_dtype=BF16, tm=1024, tn=1024, name="ffn_up")
        x = _mm_resid_call(up, w_ff2[l].astype(BF16), x, mod3, l, 5, tm=512, tn=512, name="ffn_down")

    y = _final_norm_call(x, final_g)
    y_prompt = y[:N_CTX_TOK].reshape(BATCH, SEQ, D_MODEL)
    y_sample = y[N_CTX_TOK:].reshape(DEC_BATCH, DEC_SEQ, D_MODEL)
    return (y_prompt, y_sample, jnp.stack(new_hg, axis=1), jnp.stack(new_gdn, axis=1))
```

```python
import functools

import jax
import jax.numpy as jnp
from jax import lax
from jax.experimental import pallas as pl
from jax.experimental.pallas import tpu as pltpu

F32 = jnp.float32
BF16 = jnp.bfloat16

D_MODEL = 2048
BATCH = 16
SEQ = 256
DEPTH = 2
DEC_BATCH = 4
DEC_SEQ = 1024
GRID_W = 64
EPS = 1e-6
D_FF = 4 * D_MODEL
HEADS = 8
HEAD_DIM = 128
HG_F = HEADS * HEAD_DIM
CM_GROUPS = 8
CM_W = CM_GROUPS * HEAD_DIM
CM_CHUNK = 128
GDN_CHUNK = 64
CONV_K = 3

OFF_HQ, OFF_HI, OFF_HG, OFF_HFF, OFF_HFB = 0, 1024, 2048, 3072, 4096
OFF_CU, OFF_CV = 5120, 6144
OFF_GQ, OFF_GK, OFF_GV, OFF_GG = 7168, 8192, 9216, 10240
OFF_AB = 11264
OFF_GATES = 11296
IN_DIM = 17440
N_MAIN = OFF_AB

N_CTX_TOK = BATCH * SEQ
N_LAT_TOK = DEC_BATCH * DEC_SEQ
N_TOK = N_CTX_TOK + N_LAT_TOK
UNIT = DEC_SEQ
N_CTX_UNITS = N_CTX_TOK // UNIT
N_UNITS = N_TOK // UNIT
SEQ_PER_UNIT = UNIT // SEQ

LANES = 128
MOD_ROWS = 8
HG_BLOCK = 256
HG_SUB = 32
HG_INST = 2 * (UNIT // HG_BLOCK)
VMEM_LIMIT = 56 * 1024 * 1024

_NT = (((1,), (1,)), ((), ()))
_TN = (((0,), (0,)), ((), ()))
_NN = (((1,), (0,)), ((), ()))


def _dg(a, b, dims):
    return lax.dot_general(a, b, dims, preferred_element_type=F32)


def _split2(x):
    hi = x.astype(BF16)
    lo = (x - hi.astype(F32)).astype(BF16)
    return hi, lo


def _dot1(a, b, dims=_NN):
    return _dg(a.astype(BF16), b.astype(BF16), dims)


def _dot3(a, b, dims=_NN):
    ah, al = _split2(a)
    bh, bl = _split2(b)
    return _dg(ah, bh, dims) + (_dg(ah, bl, dims) + _dg(al, bh, dims))


_dot_inv = _dot1


def _dot01(m, x):
    hi = x.astype(BF16)
    r = x - hi.astype(F32)
    mid = r.astype(BF16)
    lo = (r - mid.astype(F32)).astype(BF16)
    return _dg(m, hi, _NN) + (_dg(m, mid, _NN) + _dg(m, lo, _NN))


def _sigmoid(x):
    return 1.0 / (1.0 + jnp.exp(-x))


def _silu(x):
    return x * _sigmoid(x)


def _gelu(x):
    return 0.5 * x * (1.0 + lax.erf(x * (2.0 ** -0.5)))


def _softplus(x):
    return jnp.maximum(x, 0.0) + jnp.log1p(jnp.exp(-jnp.abs(x)))


def _mod_row_of_tile(i, tm):
    return jnp.maximum(0, (i * tm - N_CTX_TOK) // DEC_SEQ + 1)


def _cparams(sem):
    return pltpu.CompilerParams(dimension_semantics=sem, vmem_limit_bytes=VMEM_LIMIT)


def _mod_kernel(c_ref, w_ref, b_ref, o_ref):
    s = _silu(c_ref[...])
    hi, lo = _split2(s)
    w = w_ref[...].astype(BF16)
    o_ref[...] = _dg(hi, w, _NN) + _dg(lo, w, _NN) + b_ref[...]


def _mod_call(cvec, w_mod, b_mod):
    tn = 1024
    n = 6 * D_MODEL
    return pl.pallas_call(
        _mod_kernel,
        grid=(DEPTH, n // tn),
        in_specs=[pl.BlockSpec((MOD_ROWS, D_MODEL), lambda l, j: (0, 0)),
                  pl.BlockSpec((None, D_MODEL, tn), lambda l, j: (l, 0, j)),
                  pl.BlockSpec((None, 1, tn), lambda l, j: (l, 0, j))],
        out_specs=pl.BlockSpec((None, MOD_ROWS, tn), lambda l, j: (l, 0, j)),
        out_shape=jax.ShapeDtypeStruct((DEPTH, MOD_ROWS, n), F32),
        compiler_params=_cparams(("arbitrary", "arbitrary")),
        name="modulation",
    )(cvec, w_mod, b_mod.reshape(DEPTH, 1, n))


def _normmod_kernel(x_ref, g_ref, sh_ref, sc_ref, o_ref):
    x = x_ref[...]
    y = x * lax.rsqrt(jnp.mean(x * x, axis=-1, keepdims=True) + EPS) * g_ref[...]
    o_ref[...] = (y * (1.0 + sc_ref[...]) + sh_ref[...]).astype(o_ref.dtype)


def _normmod_call(x, g, mod3, layer, k_shift, k_scale):
    tm = 512
    def mod_spec(k):
        return pl.BlockSpec((None, 1, D_MODEL),
                            lambda i: (layer * MOD_ROWS + _mod_row_of_tile(i, tm), 0, k))
    return pl.pallas_call(
        _normmod_kernel,
        grid=(N_TOK // tm,),
        in_specs=[pl.BlockSpec((tm, D_MODEL), lambda i: (i, 0)),
                  pl.BlockSpec((1, D_MODEL), lambda i: (0, 0)),
                  mod_spec(k_shift), mod_spec(k_scale)],
        out_specs=pl.BlockSpec((tm, D_MODEL), lambda i: (i, 0)),
        out_shape=jax.ShapeDtypeStruct((N_TOK, D_MODEL), BF16),
        compiler_params=_cparams(("arbitrary",)),
        name="norm_mod",
    )(x, g.reshape(1, D_MODEL), mod3, mod3)


def _final_norm_kernel(x_ref, g_ref, o_ref):
    x = x_ref[...]
    o_ref[...] = x * lax.rsqrt(jnp.mean(x * x, axis=-1, keepdims=True) + EPS) * g_ref[...]


def _final_norm_call(x, g, row0, n_rows):
    tm = 512
    return pl.pallas_call(
        _final_norm_kernel,
        grid=(n_rows // tm,),
        in_specs=[pl.BlockSpec((tm, D_MODEL), lambda i: (row0 // tm + i, 0)),
                  pl.BlockSpec((1, D_MODEL), lambda i: (0, 0))],
        out_specs=pl.BlockSpec((tm, D_MODEL), lambda i: (i, 0)),
        out_shape=jax.ShapeDtypeStruct((n_rows, D_MODEL), F32),
        compiler_params=_cparams(("arbitrary",)),
        name="final_norm",
    )(x, g.reshape(1, D_MODEL))


def _mm_kernel(*refs, act, w_transposed, has_resid):
    x_ref, w_ref = refs[:2]
    refs = refs[2:]
    if has_resid:
        r_ref, g_ref = refs[:2]
        refs = refs[2:]
    o_ref, wbf_ref = refs

    @pl.when(pl.program_id(1) == 0)
    def _cast_weights():
        wbf_ref[...] = w_ref[...].reshape(wbf_ref.shape).astype(BF16)

    acc = lax.dot_general(x_ref[...], wbf_ref[...], _NT if w_transposed else _NN, preferred_element_type=F32)
    if act == "sigmoid":
        acc = _sigmoid(acc)
    elif act == "relu2":
        acc = jnp.square(jnp.maximum(acc, 0.0))
    if has_resid:
        acc = r_ref[...] + g_ref[...] * acc
    o_ref[...] = acc.astype(o_ref.dtype)


def _mm_call(x, w_all, layer, *, col0, n, act, out_dtype, tm, tn, name, w_transposed=False,
             resid=None, mod3=None, k_gate=None):
    m, k = x.shape
    assert n % tn == 0 and m % tm == 0
    if w_transposed:
        assert col0 % 8 == 0
        w_spec = pl.BlockSpec((pl.Element(1), pl.Element(tn), pl.Element(k)),
                              lambda j, i: (layer, pl.multiple_of(col0 + j * tn, 8), 0))
        w_scratch = pltpu.VMEM((tn, k), BF16)
    else:
        assert col0 % tn == 0
        w_spec = pl.BlockSpec((None, k, tn), lambda j, i: (layer, 0, col0 // tn + j))
        w_scratch = pltpu.VMEM((k, tn), BF16)
    in_specs = [pl.BlockSpec((tm, k), lambda j, i: (i, 0)), w_spec]
    args = [x, w_all]
    if resid is not None:
        per_tile = D_MODEL // tn
        in_specs += [pl.BlockSpec((tm, tn), lambda j, i: (i, j)),
                     pl.BlockSpec((None, 1, tn),
                                  lambda j, i: (layer * MOD_ROWS + _mod_row_of_tile(i, tm), 0,
                                                k_gate * per_tile + j))]
        args += [resid, mod3]
    return pl.pallas_call(
        functools.partial(_mm_kernel, act=act, w_transposed=w_transposed, has_resid=resid is not None),
        grid=(n // tn, m // tm),
        in_specs=in_specs,
        out_specs=pl.BlockSpec((tm, tn), lambda j, i: (i, j)),
        out_shape=jax.ShapeDtypeStruct((m, n), out_dtype),
        scratch_shapes=[w_scratch],
        compiler_params=_cparams(("arbitrary", "arbitrary")),
        name=name,
    )(*args)


def _mm_bf16_kernel(x_ref, w_ref, r_ref, g_ref, o_ref):
    acc = jnp.dot(x_ref[...], w_ref[...], preferred_element_type=F32)
    o_ref[...] = r_ref[...] + g_ref[...] * acc


def _mm_resid_call(x, w, resid, mod3, layer, k_gate, *, tm, tn, name):
    m, k = x.shape
    n = w.shape[2]
    per_tile = D_MODEL // tn
    return pl.pallas_call(
        _mm_bf16_kernel,
        grid=(m // tm, n // tn),
        in_specs=[pl.BlockSpec((tm, k), lambda i, j: (i, 0)),
                  pl.BlockSpec((None, k, tn), lambda i, j: (layer, 0, j)),
                  pl.BlockSpec((tm, tn), lambda i, j: (i, j)),
                  pl.BlockSpec((None, 1, tn),
                               lambda i, j: (layer * MOD_ROWS + _mod_row_of_tile(i, tm), 0,
                                             k_gate * per_tile + j))],
        out_specs=pl.BlockSpec((tm, tn), lambda i, j: (i, j)),
        out_shape=jax.ShapeDtypeStruct((m, n), F32),
        compiler_params=_cparams(("arbitrary", "arbitrary")),
        name=name,
    )(x, w, resid, mod3)


def _merge_kernel(oa_ref, ob_ref, oc_ref, wa_ref, wb_ref, wc_ref, ga_ref, gb_ref, gc_ref, o_ref, wbf_ref):
    @pl.when(pl.program_id(1) == 0)
    def _cast_weights():
        wbf_ref[0] = wa_ref[...].astype(BF16)
        wbf_ref[1] = wb_ref[...].astype(BF16)
        wbf_ref[2] = wc_ref[...].astype(BF16)

    a = jnp.dot(oa_ref[...], wbf_ref[0], preferred_element_type=F32)
    b = jnp.dot(ob_ref[...], wbf_ref[1], preferred_element_type=F32)
    c = jnp.dot(oc_ref[...], wbf_ref[2], preferred_element_type=F32)
    o_ref[...] = (ga_ref[...] * a + gb_ref[...] * b + gc_ref[...] * c).astype(o_ref.dtype)


def _merge_call(o_a, o_b, o_c, w_a, w_b, w_c, gates, layer):
    tm, tn = 1024, 512
    nj = D_MODEL // tn
    br = pl.BlockSpec((tm, HG_F), lambda j, i: (i, 0))
    wt = pl.BlockSpec((None, HG_F, tn), lambda j, i: (layer, 0, j))
    def gate_spec(k):
        return pl.BlockSpec((tm, tn), lambda j, i: (i, k * nj + j))
    return pl.pallas_call(
        _merge_kernel,
        grid=(nj, N_TOK // tm),
        in_specs=[br, br, br, wt, wt, wt, gate_spec(0), gate_spec(1), gate_spec(2)],
        out_specs=pl.BlockSpec((tm, tn), lambda j, i: (i, j)),
        out_shape=jax.ShapeDtypeStruct((N_TOK, D_MODEL), BF16),
        scratch_shapes=[pltpu.VMEM((3, HG_F, tn), BF16)],
        compiler_params=_cparams(("arbitrary", "arbitrary")),
        name="branch_merge",
    )(o_a, o_b, o_c, w_a, w_b, w_c, gates, gates, gates)


def _head_rmsnorm_gate(o, g, og):
    y = o * lax.rsqrt(jnp.mean(o * o, axis=-1, keepdims=True) + EPS) * g
    return y * _silu(og)


def _hgrn_kernel(q_ref, i_ref, og_ref, ff_ref, fb_ref, lb_ref, g_ref, s0_ref, o_ref, s_ref,
                 acc_ref, b_ref, stc_ref, km_ref):
    is_ctx = pl.program_id(0) < N_CTX_UNITS
    nblk = UNIT // HG_BLOCK
    nsub = HG_BLOCK // HG_SUB
    r = lax.broadcasted_iota(jnp.int32, (HG_BLOCK, HG_BLOCK), 0)
    c = lax.broadcasted_iota(jnp.int32, (HG_BLOCK, HG_BLOCK), 1)
    same = (r // HG_SUB) == (c // HG_SUB)
    row_sub = lax.broadcasted_iota(jnp.int32, (HG_BLOCK, HEAD_DIM), 0) // HG_SUB
    tri = (jnp.logical_and(same, c <= r), jnp.logical_and(same, c >= r))
    tri_bf = [jnp.where(t, 1.0, 0.0).astype(BF16) for t in tri]
    z_refs = (ff_ref, fb_ref)
    inst = [(blk, d) for blk in range(nblk) for d in range(2)]

    def expand(x):
        return jnp.broadcast_to(x[:, None, :], (nsub, HG_SUB, HEAD_DIM)).reshape(HG_BLOCK, HEAD_DIM)

    def blk_rows(blk):
        return slice(blk * HG_BLOCK, (blk + 1) * HG_BLOCK)

    qs = [_silu(q_ref[blk_rows(blk), :]) for blk in range(nblk)]
    vs = [i_ref[blk_rows(blk), :] for blk in range(nblk)]
    ks, lfs = [], []
    for blk, d in inst:
        lb = lb_ref[d:d + 1, :]
        f = lb + (1.0 - lb) * _sigmoid(z_refs[d][blk_rows(blk), :])
        lfs.append(jnp.log(f))
        ks.append(1.0 - f)
    bs = [_dot01(tri_bf[d], lf) for (blk, d), lf in zip(inst, lfs)]
    tots, qts, kts, qds = [], [], [], []
    for n, (blk, d) in enumerate(inst):
        b = bs[n]
        b_ref[n] = b
        tot = b_ref[n, pl.ds((HG_SUB - 1) if d == 0 else 0, nsub, stride=HG_SUB), :]
        mid_f = expand(b_ref[n, pl.ds(HG_SUB // 2, nsub, stride=HG_SUB), :])
        tots.append(tot)
        qts.append(qs[blk] * jnp.exp(b - mid_f))
        kts.append(ks[n] * jnp.exp(mid_f - b))
        qds.append((qs[blk] * jnp.exp(b)).astype(BF16))
        kd = ks[n] * jnp.exp(expand(tot) - b)
        for s in range(nsub):
            km_ref[n, :, s * HEAD_DIM:(s + 1) * HEAD_DIM] = jnp.where(row_sub == s, kd, 0.0).astype(BF16)
    scs = [jnp.where(tri[d], _dot3(qt, kt, _NT), 0.0) for (blk, d), qt, kt in zip(inst, qts, kts)]
    uts = [_dg(vs[blk].astype(BF16), km_ref[n], _TN) for n, (blk, d) in enumerate(inst)]
    outs = [_dot1(sc, vs[blk]) for (blk, d), sc in zip(inst, scs)]
    for d in range(2):
        st = s0_ref[d].T
        for blk in (range(nblk) if d == 0 else range(nblk - 1, -1, -1)):
            n = inst.index((blk, d))
            st = jnp.where(is_ctx, 0.0, st)
            for s in (range(nsub) if d == 0 else range(nsub - 1, -1, -1)):
                stc_ref[n, s] = st.astype(BF16)
                st = st * jnp.exp(tots[n][s:s + 1, :]) + uts[n][:, s * HEAD_DIM:(s + 1) * HEAD_DIM]
            s_ref[blk, d] = st.T
    for n, (blk, d) in enumerate(inst):
        o_int = [_dg(qds[n][s * HG_SUB:(s + 1) * HG_SUB], stc_ref[n, s], _NT) for s in range(nsub)]
        acc_ref[d, blk_rows(blk), :] = outs[n] + jnp.concatenate(o_int, axis=0)
    o_ref[...] = _head_rmsnorm_gate(acc_ref[0] + acc_ref[1], g_ref[...], og_ref[...]).astype(o_ref.dtype)


def _hgrn_call(z, lb, onorm_g, state, layer):
    def col(off):
        return pl.BlockSpec((UNIT, HEAD_DIM), lambda u, h: (u, off // HEAD_DIM + h))
    return pl.pallas_call(
        _hgrn_kernel,
        grid=(N_UNITS, HEADS),
        in_specs=[col(OFF_HQ), col(OFF_HI), col(OFF_HG), col(OFF_HFF), col(OFF_HFB),
                  pl.BlockSpec((2, HEAD_DIM), lambda u, h: (0, h)),
                  pl.BlockSpec((1, HEAD_DIM), lambda u, h: (0, 0)),
                  pl.BlockSpec((None, None, 2, None, HEAD_DIM, HEAD_DIM),
                               lambda u, h: (jnp.maximum(u - N_CTX_UNITS, 0), layer, 0, h, 0, 0))],
        out_specs=[pl.BlockSpec((UNIT, HEAD_DIM), lambda u, h: (u, h)),
                   pl.BlockSpec((None, SEQ_PER_UNIT, 2, None, HEAD_DIM, HEAD_DIM),
                                lambda u, h: (u, 0, 0, h, 0, 0))],
        out_shape=[jax.ShapeDtypeStruct((N_TOK, HG_F), BF16),
                   jax.ShapeDtypeStruct((N_UNITS, SEQ_PER_UNIT, 2, HEADS, HEAD_DIM, HEAD_DIM), F32)],
        scratch_shapes=[pltpu.VMEM((2, UNIT, HEAD_DIM), F32),
                        pltpu.VMEM((HG_INST, HG_BLOCK, HEAD_DIM), F32),
                        pltpu.VMEM((HG_INST, HG_BLOCK // HG_SUB, HEAD_DIM, HEAD_DIM), BF16),
                        pltpu.VMEM((HG_INST, HG_BLOCK, (HG_BLOCK // HG_SUB) * HEAD_DIM), BF16)],
        compiler_params=_cparams(("arbitrary", "arbitrary")),
        name="hgrn2_scan",
    )(z, z, z, z, z, lb, onorm_g.reshape(1, HEAD_DIM), state)


def _gmlp_kernel(u_ref, v_ref, vn_ref, ws_ref, bs_ref, o_ref):
    tm = u_ref.shape[0]
    for g in range(CM_GROUPS):
        cols = slice(g * HEAD_DIM, (g + 1) * HEAD_DIM)
        vg = _gelu(v_ref[:, cols])
        vg = vg * lax.rsqrt(jnp.mean(vg * vg, axis=-1, keepdims=True) + EPS) * vn_ref[:, cols]
        ug = _gelu(u_ref[:, cols])
        w = ws_ref[g]
        bias = bs_ref[:, g:g + 1]
        for ch in range(tm // CM_CHUNK):
            rows = slice(ch * CM_CHUNK, (ch + 1) * CM_CHUNK)
            s = _dot3(w, vg[rows]) + bias
            o_ref[rows, cols] = (ug[rows] * s).astype(o_ref.dtype)


def _gmlp_call(z, vnorm_g, ws, bs_t):
    tm = 512
    return pl.pallas_call(
        _gmlp_kernel,
        grid=(N_TOK // tm,),
        in_specs=[pl.BlockSpec((tm, CM_W), lambda i: (i, OFF_CU // CM_W)),
                  pl.BlockSpec((tm, CM_W), lambda i: (i, OFF_CV // CM_W)),
                  pl.BlockSpec((1, CM_W), lambda i: (0, 0)),
                  pl.BlockSpec((CM_GROUPS, CM_CHUNK, CM_CHUNK), lambda i: (0, 0, 0)),
                  pl.BlockSpec((CM_CHUNK, CM_GROUPS), lambda i: (0, 0))],
        out_specs=pl.BlockSpec((tm, CM_W), lambda i: (i, 0)),
        out_shape=jax.ShapeDtypeStruct((N_TOK, CM_W), BF16),
        compiler_params=_cparams(("arbitrary",)),
        name="chunk_gmlp",
    )(z, z, vnorm_g.reshape(1, CM_W), ws, bs_t)


GP_BLOCK = 256


def _gdn_gates_kernel(ab_ref, alog_ref, dt_ref, col_ref, rowt_ref):
    ab = ab_ref[...]
    lane = lax.broadcasted_iota(jnp.int32, ab.shape, 1)
    g = jnp.where(lane < 2 * HEADS, -jnp.exp(alog_ref[...]) * _softplus(ab + dt_ref[...]), 0.0)
    r = lax.broadcasted_iota(jnp.int32, (GP_BLOCK, GP_BLOCK), 0)
    c = lax.broadcasted_iota(jnp.int32, (GP_BLOCK, GP_BLOCK), 1)
    same = (r // GDN_CHUNK) == (c // GDN_CHUNK)
    tri_f = jnp.where(jnp.logical_and(same, c <= r), 1.0, 0.0).astype(BF16)
    tri_b = jnp.where(jnp.logical_and(same, c >= r), 1.0, 0.0).astype(BF16)
    cf = _dot01(tri_f, g)
    cb = _dot01(tri_b, g)
    col = jnp.where(lane < HEADS, cf, jnp.where(lane < 2 * HEADS, cb, _sigmoid(ab)))
    col_ref[...] = col
    rowt_ref[...] = col.T


def _gdn_gates_call(ab, a_log, dt_bias):
    pad = lambda t: jnp.pad(t.reshape(1, 2 * HEADS), ((0, 0), (0, LANES - 2 * HEADS)))
    return pl.pallas_call(
        _gdn_gates_kernel,
        grid=(N_TOK // GP_BLOCK,),
        in_specs=[pl.BlockSpec((GP_BLOCK, LANES), lambda i: (i, 0)),
                  pl.BlockSpec((1, LANES), lambda i: (0, 0)),
                  pl.BlockSpec((1, LANES), lambda i: (0, 0))],
        out_specs=[pl.BlockSpec((GP_BLOCK, LANES), lambda i: (i, 0)),
                   pl.BlockSpec((LANES, GP_BLOCK), lambda i: (0, i))],
        out_shape=[jax.ShapeDtypeStruct((N_TOK, LANES), F32),
                   jax.ShapeDtypeStruct((LANES, N_TOK), F32)],
        compiler_params=_cparams(("arbitrary",)),
        name="gdn_gates",
    )(ab, pad(a_log), pad(dt_bias))


CONV_PAD = 72
GDN_AQ = HEAD_DIM + GDN_CHUNK
GDN_STEP = 8 * GDN_CHUNK


def _gdn_kernel(qr_ref, kr_ref, vr_ref, og_ref, col_ref, rowt_ref, cwq_ref, cwk_ref, cwv_ref,
                g_ref, s0_ref, o_ref, s_ref,
                xp_ref, q_s, k_s, v_s, o_s, b_s, aq_s):
    is_ctx = pl.program_id(0) < N_CTX_UNITS
    head = pl.program_id(1)
    n_chunks = UNIT // GDN_CHUNK
    chunks_per_seq = SEQ // GDN_CHUNK

    t = lax.broadcasted_iota(jnp.int32, (UNIT, 1), 0)
    period = jnp.where(is_ctx, SEQ, GRID_W)
    pos = jnp.bitwise_and(t, period - 1)
    ok_left = pos != 0
    ok_right = pos != period - 1
    zeros_pad = jnp.zeros((CONV_PAD, HEAD_DIM), F32)
    xp_ref[0:CONV_PAD, :] = zeros_pad
    xp_ref[CONV_PAD + UNIT:CONV_PAD + UNIT + CONV_PAD, :] = zeros_pad

    def conv_silu(x_ref, w_ref):
        xp_ref[CONV_PAD:CONV_PAD + UNIT, :] = x_ref[...]
        acc = jnp.zeros((UNIT, HEAD_DIM), F32)
        for i in range(CONV_K):
            for j in range(CONV_K):
                w = w_ref[CONV_K * i + j:CONV_K * i + j + 1, :]
                if i != CONV_K // 2:
                    w = jnp.where(is_ctx, 0.0, w)
                start = CONV_PAD + (i - 1) * GRID_W + (j - 1)
                xs = xp_ref[start:start + UNIT, :]
                if j == 0:
                    xs = jnp.where(ok_left, xs, 0.0)
                elif j == CONV_K - 1:
                    xs = jnp.where(ok_right, xs, 0.0)
                acc = acc + xs * w
        return _silu(acc)

    def l2norm(x):
        return x * lax.rsqrt(jnp.sum(x * x, axis=-1, keepdims=True) + EPS)

    q_s[...] = l2norm(conv_silu(qr_ref, cwq_ref)) * (HEAD_DIM ** -0.5)
    k_s[...] = l2norm(conv_silu(kr_ref, cwk_ref))
    v_s[...] = conv_silu(vr_ref, cwv_ref)

    rr = lax.broadcasted_iota(jnp.int32, (GDN_CHUNK, GDN_CHUNK), 0)
    cc = lax.broadcasted_iota(jnp.int32, (GDN_CHUNK, GDN_CHUNK), 1)
    eye = jnp.where(rr == cc, 1.0, 0.0)
    same_blk = [(rr // b) == (cc // b) for b in (8, 16, 32, 64)]
    lane = lax.broadcasted_iota(jnp.int32, (GDN_STEP, LANES), 1)
    sub8 = lax.broadcasted_iota(jnp.int32, (HEADS, GDN_STEP), 0)

    def pick(x, j):
        return jnp.sum(jnp.where(lane[:x.shape[0]] == j, x, 0.0), axis=-1, keepdims=True)

    incl = (cc <= rr, cc >= rr)
    strict = (cc < rr, cc > rr)
    off_masks = [jnp.logical_and(same_blk[lvl], jnp.logical_not(same_blk[lvl - 1]))
                 for lvl in range(1, len(same_blk))]

    def phase1(p, carry):
        rows = pl.ds(pl.multiple_of(p * GDN_STEP, GDN_STEP), GDN_STEP)
        q2, k2, v2 = q_s[rows, :], k_s[rows, :], v_s[rows, :]
        col = col_ref[rows, :]
        gcols = [pick(col, d * HEADS + head) for d in range(2)]
        betas = [pick(col, (2 + d) * HEADS + head) for d in range(2)]
        grows = [jnp.sum(jnp.where(sub8 == head, rowt_ref[d * HEADS:(d + 1) * HEADS, rows], 0.0),
                         axis=0, keepdims=True) for d in range(2)]
        inst = []
        for half in range(GDN_STEP // GDN_CHUNK):
            sl = slice(half * GDN_CHUNK, (half + 1) * GDN_CHUNK)
            qc, kc, vc = q2[sl], k2[sl], v2[sl]
            kk = _dot1(kc, kc, _NT)
            qk = _dot1(qc, kc, _NT)
            for d in range(2):
                gcol, beta = gcols[d][sl], betas[d][sl]
                decay = jnp.where(incl[d], jnp.exp(gcol - grows[d][:, sl]), 0.0)
                m = jnp.where(strict[d], beta * kk * decay, 0.0)
                inst.append((half, d, qc, kc, vc, qk, gcol, beta, decay, m))
        ms = [t[-1] for t in inst]
        pws = [jnp.where(same_blk[0], -m, 0.0) for m in ms]
        tinvs = [eye + pw for pw in pws]
        for _ in range(2):
            pws = [_dot_inv(pw, pw) for pw in pws]
            tinvs = [t + _dot_inv(t, pw) for t, pw in zip(tinvs, pws)]
        for mask in off_masks:
            tmp = [_dot_inv(jnp.where(mask, m, 0.0), t) for m, t in zip(ms, tinvs)]
            tinvs = [t - _dot_inv(t, x) for t, x in zip(tinvs, tmp)]
        uws = [_dot1(tinv, jnp.concatenate([vc * beta, kc * (beta * jnp.exp(gcol))], axis=1)).astype(BF16)
               for (half, d, qc, kc, vc, qk, gcol, beta, decay, m), tinv in zip(inst, tinvs)]
        bas, ows = [], []
        for (half, d, qc, kc, vc, qk, gcol, beta, decay, m), uw in zip(inst, uws):
            glast = gcol[GDN_CHUNK - 1:GDN_CHUNK] if d == 0 else gcol[0:1]
            bas.append(_dg((kc * jnp.exp(glast - gcol)).astype(BF16), uw, _TN))
            ows.append(_dg((qk * decay).astype(BF16), uw, _NN))
        for (half, d, qc, kc, vc, qk, gcol, beta, decay, m), ba, ow in zip(inst, bas, ows):
            chunk0 = p * GDN_STEP + half * GDN_CHUNK
            o_s[d, pl.ds(pl.multiple_of(chunk0, GDN_CHUNK), GDN_CHUNK), :] = ow[:, :HEAD_DIM]
            b_s[d, pl.ds(pl.multiple_of(2 * chunk0, HEAD_DIM), HEAD_DIM), :] = ba[:, :HEAD_DIM]
            aq_s[d, pl.ds(pl.multiple_of(3 * chunk0, GDN_AQ), GDN_AQ), :] = jnp.concatenate(
                [ba[:, HEAD_DIM:], qc * jnp.exp(gcol) - ow[:, HEAD_DIM:]], axis=0).astype(BF16)
        return carry

    lax.fori_loop(0, UNIT // GDN_STEP, phase1, 0)

    lane1 = lax.broadcasted_iota(jnp.int32, (1, LANES), 1)

    def phase2(i, carry):
        chs = (i, n_chunks - 1 - i)
        firsts = [(chs[0] % chunks_per_seq) == 0, (chs[1] % chunks_per_seq) == chunks_per_seq - 1]
        ss = [jnp.where(jnp.logical_and(is_ctx, f), 0.0, s) for f, s in zip(firsts, carry)]
        rs = [_dg(aq_s[d, pl.ds(pl.multiple_of(chs[d] * GDN_AQ, GDN_AQ), GDN_AQ), :],
                  ss[d].astype(BF16), _NN) for d in range(2)]
        new = []
        for d in range(2):
            tile0 = pl.multiple_of(chs[d] * GDN_CHUNK + (GDN_CHUNK - 8 if d == 0 else 0), 8)
            last = col_ref[pl.ds(tile0, 8), :]
            last = last[7:8] if d == 0 else last[0:1]
            glast = jnp.sum(jnp.where(lane1 == d * HEADS + head, last, 0.0), axis=-1, keepdims=True)
            b = b_s[d, pl.ds(pl.multiple_of(chs[d] * HEAD_DIM, HEAD_DIM), HEAD_DIM), :]
            s = ss[d] * jnp.exp(glast) + (b - rs[d][:HEAD_DIM])
            s_ref[chs[d] // chunks_per_seq, d] = s
            new.append(s)
            rows = pl.ds(pl.multiple_of(chs[d] * GDN_CHUNK, GDN_CHUNK), GDN_CHUNK)
            o_s[d, rows, :] = o_s[d, rows, :] + rs[d][HEAD_DIM:]
        return tuple(new)

    lax.fori_loop(0, n_chunks, phase2, (s0_ref[0], s0_ref[1]))
    o_ref[...] = _head_rmsnorm_gate(o_s[0] + o_s[1], g_ref[...], og_ref[...]).astype(o_ref.dtype)


def _gdn_call(z, col, rowt, conv_w, onorm_g, state, layer):
    def zcol(off):
        return pl.BlockSpec((UNIT, HEAD_DIM), lambda u, h: (u, off // HEAD_DIM + h))
    def wcol(part):
        return pl.BlockSpec((CONV_K * CONV_K, HEAD_DIM), lambda u, h: (0, part * HEADS + h))
    scr = lambda *shape: pltpu.VMEM(shape, F32)
    return pl.pallas_call(
        _gdn_kernel,
        grid=(N_UNITS, HEADS),
        in_specs=[zcol(OFF_GQ), zcol(OFF_GK), zcol(OFF_GV), zcol(OFF_GG),
                  pl.BlockSpec((UNIT, LANES), lambda u, h: (u, 0)),
                  pl.BlockSpec((LANES, UNIT), lambda u, h: (0, u)),
                  wcol(0), wcol(1), wcol(2),
                  pl.BlockSpec((1, HEAD_DIM), lambda u, h: (0, 0)),
                  pl.BlockSpec((None, None, 2, None, HEAD_DIM, HEAD_DIM),
                               lambda u, h: (jnp.maximum(u - N_CTX_UNITS, 0), layer, 0, h, 0, 0))],
        out_specs=[pl.BlockSpec((UNIT, HEAD_DIM), lambda u, h: (u, h)),
                   pl.BlockSpec((None, SEQ_PER_UNIT, 2, None, HEAD_DIM, HEAD_DIM),
                                lambda u, h: (u, 0, 0, h, 0, 0))],
        out_shape=[jax.ShapeDtypeStruct((N_TOK, HG_F), BF16),
                   jax.ShapeDtypeStruct((N_UNITS, SEQ_PER_UNIT, 2, HEADS, HEAD_DIM, HEAD_DIM), F32)],
        scratch_shapes=[scr(UNIT + 2 * CONV_PAD, HEAD_DIM),
                        scr(UNIT, HEAD_DIM), scr(UNIT, HEAD_DIM), scr(UNIT, HEAD_DIM),
                        scr(2, UNIT, HEAD_DIM),
                        scr(2, (UNIT // GDN_CHUNK) * HEAD_DIM, HEAD_DIM),
                        pltpu.VMEM((2, (UNIT // GDN_CHUNK) * GDN_AQ, HEAD_DIM), BF16)],
        compiler_params=_cparams(("arbitrary", "arbitrary")),
        name="gdn_scan",
    )(z, z, z, z, col, rowt, conv_w, conv_w, conv_w, onorm_g.reshape(1, HEAD_DIM), state)


def kernel(x_prompt, x_sample, c, state_hgrn, state_gdn, c_ctx, norm1_g, norm2_g, w_mod, b_mod, w_in, hg_lb, hg_onorm_g, cm_vnorm_g, cm_ws, cm_bs, gdn_conv, gdn_A_log, gdn_dt_bias, gdn_onorm_g, w_br_hg, w_br_cm, w_br_gdn, w_out, w_ff1, w_ff2, final_g):
    x = jnp.concatenate([x_prompt.reshape(N_CTX_TOK, D_MODEL), x_sample.reshape(N_LAT_TOK, D_MODEL)], axis=0)
    cvec = jnp.concatenate([c_ctx[None, :], c, jnp.zeros((MOD_ROWS - 1 - DEC_BATCH, D_MODEL), F32)], axis=0)
    mod3 = _mod_call(cvec, w_mod, b_mod).reshape(DEPTH * MOD_ROWS, 1, 6 * D_MODEL)

    lb_all = jnp.cumsum(jax.nn.softmax(hg_lb.astype(F32), axis=0), axis=0)
    lb_all = lb_all - lb_all[:1]

    w_in_t = jnp.swapaxes(w_in, 1, 2)
    w_ff2_bf = w_ff2.astype(BF16)

    new_hg, new_gdn = [], []
    for l in range(DEPTH):
        h = _normmod_call(x, norm1_g[l], mod3, l, 0, 1)
        z = _mm_call(h, w_in_t, l, col0=0, n=N_MAIN, act=None, out_dtype=F32, tm=1024, tn=1024,
                     w_transposed=True, name="in_proj")
        ab = _mm_call(h, w_in_t, l, col0=OFF_AB, n=LANES, act=None, out_dtype=F32, tm=1024, tn=LANES,
                      w_transposed=True, name="in_proj_ab")
        gates = _mm_call(h, w_in_t, l, col0=OFF_GATES, n=3 * D_MODEL, act="sigmoid", out_dtype=F32,
                         tm=1024, tn=1024, w_transposed=True, name="in_proj_gates")

        o_a, s_hg = _hgrn_call(z, lb_all[l], hg_onorm_g[l], state_hgrn, l)
        o_b = _gmlp_call(z, cm_vnorm_g[l], cm_ws[l], cm_bs[l].T)
        col, rowt = _gdn_gates_call(ab, gdn_A_log[l], gdn_dt_bias[l])
        o_c, s_gdn = _gdn_call(z, col, rowt, gdn_conv[l].reshape(CONV_K * CONV_K, 3 * HG_F),
                               gdn_onorm_g[l], state_gdn, l)
        new_hg.append(s_hg[:N_CTX_UNITS].reshape(BATCH, 2, HEADS, HEAD_DIM, HEAD_DIM))
        new_gdn.append(s_gdn[:N_CTX_UNITS].reshape(BATCH, 2, HEADS, HEAD_DIM, HEAD_DIM))

        merged = _merge_call(o_a, o_b, o_c, w_br_hg, w_br_cm, w_br_gdn, gates, l)
        x = _mm_call(merged, w_out, l, col0=0, n=D_MODEL, act=None, out_dtype=F32, tm=1024, tn=1024,
                     name="out_proj", resid=x, mod3=mod3, k_gate=2)

        h2 = _normmod_call(x, norm2_g[l], mod3, l, 3, 4)
        up = _mm_call(h2, w_ff1, l, col0=0, n=D_FF, act="relu2", out_dtype=BF16, tm=1024, tn=1024, name="ffn_up")
        x = _mm_resid_call(up, w_ff2_bf, x, mod3, l, 5, tm=512, tn=512, name="ffn_down")

    y_prompt = _final_norm_call(x, final_g, 0, N_CTX_TOK).reshape(BATCH, SEQ, D_MODEL)
    y_sample = _final_norm_call(x, final_g, N_CTX_TOK, N_LAT_TOK).reshape(DEC_BATCH, DEC_SEQ, D_MODEL)
    return (y_prompt, y_sample, jnp.stack(new_hg, axis=1), jnp.stack(new_gdn, axis=1))
```

```python
import functools

import jax
import jax.numpy as jnp
from jax import lax
from jax.experimental import pallas as pl
from jax.experimental.pallas import tpu as pltpu

F32 = jnp.float32
BF16 = jnp.bfloat16

D_MODEL = 2048
BATCH = 16
SEQ = 256
DEPTH = 2
DEC_BATCH = 4
DEC_SEQ = 1024
GRID_W = 64
EPS = 1e-6
D_FF = 4 * D_MODEL
HEADS = 8
HEAD_DIM = 128
HG_F = HEADS * HEAD_DIM
CM_GROUPS = 8
CM_W = CM_GROUPS * HEAD_DIM
CM_CHUNK = 128
GDN_CHUNK = 64
CONV_K = 3

OFF_HQ, OFF_HI, OFF_HG, OFF_HFF, OFF_HFB = 0, 1024, 2048, 3072, 4096
OFF_CU, OFF_CV = 5120, 6144
OFF_GQ, OFF_GK, OFF_GV, OFF_GG = 7168, 8192, 9216, 10240
OFF_AB = 11264
OFF_GATES = 11296
IN_DIM = 17440
N_MAIN = OFF_AB

N_CTX_TOK = BATCH * SEQ
N_LAT_TOK = DEC_BATCH * DEC_SEQ
N_TOK = N_CTX_TOK + N_LAT_TOK
UNIT = DEC_SEQ
N_CTX_UNITS = N_CTX_TOK // UNIT
N_UNITS = N_TOK // UNIT
SEQ_PER_UNIT = UNIT // SEQ

LANES = 128
MOD_ROWS = 8
HG_BLOCK = 256
HG_SUB = 32
HG_INST = 2 * (UNIT // HG_BLOCK)
VMEM_LIMIT = 56 * 1024 * 1024

_NT = (((1,), (1,)), ((), ()))
_TN = (((0,), (0,)), ((), ()))
_NN = (((1,), (0,)), ((), ()))


def _dg(a, b, dims):
    return lax.dot_general(a, b, dims, preferred_element_type=F32)


def _split2(x):
    hi = x.astype(BF16)
    lo = (x - hi.astype(F32)).astype(BF16)
    return hi, lo


def _dot1(a, b, dims=_NN):
    return _dg(a.astype(BF16), b.astype(BF16), dims)


def _dot3(a, b, dims=_NN):
    ah, al = _split2(a)
    bh, bl = _split2(b)
    return _dg(ah, bh, dims) + (_dg(ah, bl, dims) + _dg(al, bh, dims))


_dot_inv = _dot1


def _dot01(m, x):
    hi = x.astype(BF16)
    r = x - hi.astype(F32)
    mid = r.astype(BF16)
    lo = (r - mid.astype(F32)).astype(BF16)
    return _dg(m, hi, _NN) + (_dg(m, mid, _NN) + _dg(m, lo, _NN))


def _dot01_2(m, x):
    hi, lo = _split2(x)
    return _dg(m, hi, _NN) + _dg(m, lo, _NN)


def _sigmoid(x):
    return 1.0 / (1.0 + jnp.exp(-x))


def _silu(x):
    return x * _sigmoid(x)


def _gelu(x):
    return 0.5 * x * (1.0 + lax.erf(x * (2.0 ** -0.5)))


def _softplus(x):
    return jnp.maximum(x, 0.0) + jnp.log1p(jnp.exp(-jnp.abs(x)))


def _mod_row_of_tile(i, tm):
    return jnp.maximum(0, (i * tm - N_CTX_TOK) // DEC_SEQ + 1)


def _cparams(sem):
    return pltpu.CompilerParams(dimension_semantics=sem, vmem_limit_bytes=VMEM_LIMIT)


def _mod_kernel(c_ref, w_ref, b_ref, o_ref):
    s = _silu(c_ref[...])
    hi, lo = _split2(s)
    w = w_ref[...].astype(BF16)
    o_ref[...] = _dg(hi, w, _NN) + _dg(lo, w, _NN) + b_ref[...]


def _mod_call(cvec, w_mod, b_mod):
    tn = 1024
    n = 6 * D_MODEL
    return pl.pallas_call(
        _mod_kernel,
        grid=(DEPTH, n // tn),
        in_specs=[pl.BlockSpec((MOD_ROWS, D_MODEL), lambda l, j: (0, 0)),
                  pl.BlockSpec((None, D_MODEL, tn), lambda l, j: (l, 0, j)),
                  pl.BlockSpec((None, 1, tn), lambda l, j: (l, 0, j))],
        out_specs=pl.BlockSpec((None, MOD_ROWS, tn), lambda l, j: (l, 0, j)),
        out_shape=jax.ShapeDtypeStruct((DEPTH, MOD_ROWS, n), F32),
        compiler_params=_cparams(("arbitrary", "arbitrary")),
        name="modulation",
    )(cvec, w_mod, b_mod.reshape(DEPTH, 1, n))


def _normmod_kernel(x_ref, g_ref, sh_ref, sc_ref, o_ref):
    x = x_ref[...]
    y = x * lax.rsqrt(jnp.mean(x * x, axis=-1, keepdims=True) + EPS) * g_ref[...]
    o_ref[...] = (y * (1.0 + sc_ref[...]) + sh_ref[...]).astype(o_ref.dtype)


def _normmod_call(x, g, mod3, layer, k_shift, k_scale):
    tm = 512
    def mod_spec(k):
        return pl.BlockSpec((None, 1, D_MODEL),
                            lambda i: (layer * MOD_ROWS + _mod_row_of_tile(i, tm), 0, k))
    return pl.pallas_call(
        _normmod_kernel,
        grid=(N_TOK // tm,),
        in_specs=[pl.BlockSpec((tm, D_MODEL), lambda i: (i, 0)),
                  pl.BlockSpec((1, D_MODEL), lambda i: (0, 0)),
                  mod_spec(k_shift), mod_spec(k_scale)],
        out_specs=pl.BlockSpec((tm, D_MODEL), lambda i: (i, 0)),
        out_shape=jax.ShapeDtypeStruct((N_TOK, D_MODEL), BF16),
        compiler_params=_cparams(("arbitrary",)),
        name="norm_mod",
    )(x, g.reshape(1, D_MODEL), mod3, mod3)


def _final_norm_kernel(x_ref, g_ref, o_ref):
    x = x_ref[...]
    o_ref[...] = x * lax.rsqrt(jnp.mean(x * x, axis=-1, keepdims=True) + EPS) * g_ref[...]


def _final_norm_call(x, g, row0, n_rows):
    tm = 512
    return pl.pallas_call(
        _final_norm_kernel,
        grid=(n_rows // tm,),
        in_specs=[pl.BlockSpec((tm, D_MODEL), lambda i: (row0 // tm + i, 0)),
                  pl.BlockSpec((1, D_MODEL), lambda i: (0, 0))],
        out_specs=pl.BlockSpec((tm, D_MODEL), lambda i: (i, 0)),
        out_shape=jax.ShapeDtypeStruct((n_rows, D_MODEL), F32),
        compiler_params=_cparams(("arbitrary",)),
        name="final_norm",
    )(x, g.reshape(1, D_MODEL))


def _mm_kernel(*refs, act, w_transposed, has_resid):
    x_ref, w_ref = refs[:2]
    refs = refs[2:]
    if has_resid:
        r_ref, g_ref = refs[:2]
        refs = refs[2:]
    o_ref, wbf_ref = refs

    @pl.when(pl.program_id(1) == 0)
    def _cast_weights():
        wbf_ref[...] = w_ref[...].reshape(wbf_ref.shape).astype(BF16)

    acc = lax.dot_general(x_ref[...], wbf_ref[...], _NT if w_transposed else _NN, preferred_element_type=F32)
    if act == "sigmoid":
        acc = _sigmoid(acc)
    elif act == "relu2":
        acc = jnp.square(jnp.maximum(acc, 0.0))
    if has_resid:
        acc = r_ref[...] + g_ref[...] * acc
    o_ref[...] = acc.astype(o_ref.dtype)


def _mm_call(x, w_all, layer, *, col0, n, act, out_dtype, tm, tn, name, w_transposed=False,
             resid=None, mod3=None, k_gate=None):
    m, k = x.shape
    assert n % tn == 0 and m % tm == 0
    if w_transposed:
        assert col0 % 8 == 0
        w_spec = pl.BlockSpec((pl.Element(1), pl.Element(tn), pl.Element(k)),
                              lambda j, i: (layer, pl.multiple_of(col0 + j * tn, 8), 0))
        w_scratch = pltpu.VMEM((tn, k), BF16)
    else:
        assert col0 % tn == 0
        w_spec = pl.BlockSpec((None, k, tn), lambda j, i: (layer, 0, col0 // tn + j))
        w_scratch = pltpu.VMEM((k, tn), BF16)
    in_specs = [pl.BlockSpec((tm, k), lambda j, i: (i, 0)), w_spec]
    args = [x, w_all]
    if resid is not None:
        per_tile = D_MODEL // tn
        in_specs += [pl.BlockSpec((tm, tn), lambda j, i: (i, j)),
                     pl.BlockSpec((None, 1, tn),
                                  lambda j, i: (layer * MOD_ROWS + _mod_row_of_tile(i, tm), 0,
                                                k_gate * per_tile + j))]
        args += [resid, mod3]
    return pl.pallas_call(
        functools.partial(_mm_kernel, act=act, w_transposed=w_transposed, has_resid=resid is not None),
        grid=(n // tn, m // tm),
        in_specs=in_specs,
        out_specs=pl.BlockSpec((tm, tn), lambda j, i: (i, j)),
        out_shape=jax.ShapeDtypeStruct((m, n), out_dtype),
        scratch_shapes=[w_scratch],
        compiler_params=_cparams(("arbitrary", "arbitrary")),
        name=name,
    )(*args)


def _mm_bf16_kernel(x_ref, w_ref, r_ref, g_ref, o_ref):
    acc = jnp.dot(x_ref[...], w_ref[...], preferred_element_type=F32)
    o_ref[...] = r_ref[...] + g_ref[...] * acc


def _mm_resid_call(x, w, resid, mod3, layer, k_gate, *, tm, tn, name):
    m, k = x.shape
    n = w.shape[2]
    per_tile = D_MODEL // tn
    return pl.pallas_call(
        _mm_bf16_kernel,
        grid=(m // tm, n // tn),
        in_specs=[pl.BlockSpec((tm, k), lambda i, j: (i, 0)),
                  pl.BlockSpec((None, k, tn), lambda i, j: (layer, 0, j)),
                  pl.BlockSpec((tm, tn), lambda i, j: (i, j)),
                  pl.BlockSpec((None, 1, tn),
                               lambda i, j: (layer * MOD_ROWS + _mod_row_of_tile(i, tm), 0,
                                             k_gate * per_tile + j))],
        out_specs=pl.BlockSpec((tm, tn), lambda i, j: (i, j)),
        out_shape=jax.ShapeDtypeStruct((m, n), F32),
        compiler_params=_cparams(("arbitrary", "arbitrary")),
        name=name,
    )(x, w, resid, mod3)


def _merge_kernel(oa_ref, ob_ref, oc_ref, wa_ref, wb_ref, wc_ref, ga_ref, gb_ref, gc_ref, o_ref, wbf_ref):
    @pl.when(pl.program_id(1) == 0)
    def _cast_weights():
        wbf_ref[0] = wa_ref[...].astype(BF16)
        wbf_ref[1] = wb_ref[...].astype(BF16)
        wbf_ref[2] = wc_ref[...].astype(BF16)

    a = jnp.dot(oa_ref[...], wbf_ref[0], preferred_element_type=F32)
    b = jnp.dot(ob_ref[...], wbf_ref[1], preferred_element_type=F32)
    c = jnp.dot(oc_ref[...], wbf_ref[2], preferred_element_type=F32)
    o_ref[...] = (ga_ref[...] * a + gb_ref[...] * b + gc_ref[...] * c).astype(o_ref.dtype)


def _merge_call(o_a, o_b, o_c, w_a, w_b, w_c, gates, layer):
    tm, tn = 1024, 512
    nj = D_MODEL // tn
    br = pl.BlockSpec((tm, HG_F), lambda j, i: (i, 0))
    wt = pl.BlockSpec((None, HG_F, tn), lambda j, i: (layer, 0, j))
    def gate_spec(k):
        return pl.BlockSpec((tm, tn), lambda j, i: (i, k * nj + j))
    return pl.pallas_call(
        _merge_kernel,
        grid=(nj, N_TOK // tm),
        in_specs=[br, br, br, wt, wt, wt, gate_spec(0), gate_spec(1), gate_spec(2)],
        out_specs=pl.BlockSpec((tm, tn), lambda j, i: (i, j)),
        out_shape=jax.ShapeDtypeStruct((N_TOK, D_MODEL), BF16),
        scratch_shapes=[pltpu.VMEM((3, HG_F, tn), BF16)],
        compiler_params=_cparams(("arbitrary", "arbitrary")),
        name="branch_merge",
    )(o_a, o_b, o_c, w_a, w_b, w_c, gates, gates, gates)


def _head_rmsnorm_gate(o, g, og):
    y = o * lax.rsqrt(jnp.mean(o * o, axis=-1, keepdims=True) + EPS) * g
    return y * _silu(og)


def _hgrn_kernel(q_ref, i_ref, og_ref, ff_ref, fb_ref, lb_ref, g_ref, s0_ref, o_ref, s_ref,
                 acc_ref, b_ref, stc_ref, km_ref):
    is_ctx = pl.program_id(0) < N_CTX_UNITS
    nblk = UNIT // HG_BLOCK
    nsub = HG_BLOCK // HG_SUB
    r = lax.broadcasted_iota(jnp.int32, (HG_BLOCK, HG_BLOCK), 0)
    c = lax.broadcasted_iota(jnp.int32, (HG_BLOCK, HG_BLOCK), 1)
    same = (r // HG_SUB) == (c // HG_SUB)
    row_sub = lax.broadcasted_iota(jnp.int32, (HG_BLOCK, HEAD_DIM), 0) // HG_SUB
    tri = (jnp.logical_and(same, c <= r), jnp.logical_and(same, c >= r))
    tri_bf = [jnp.where(t, 1.0, 0.0).astype(BF16) for t in tri]
    z_refs = (ff_ref, fb_ref)
    inst = [(blk, d) for blk in range(nblk) for d in range(2)]

    def expand(x):
        return jnp.broadcast_to(x[:, None, :], (nsub, HG_SUB, HEAD_DIM)).reshape(HG_BLOCK, HEAD_DIM)

    def blk_rows(blk):
        return slice(blk * HG_BLOCK, (blk + 1) * HG_BLOCK)

    qs = [_silu(q_ref[blk_rows(blk), :]) for blk in range(nblk)]
    vs = [i_ref[blk_rows(blk), :] for blk in range(nblk)]
    ks, lfs = [], []
    for blk, d in inst:
        lb = lb_ref[d:d + 1, :]
        f = lb + (1.0 - lb) * _sigmoid(z_refs[d][blk_rows(blk), :])
        lfs.append(jnp.log(f))
        ks.append(1.0 - f)
    bs = [_dot01_2(tri_bf[d], lf) for (blk, d), lf in zip(inst, lfs)]
    tots, qts, kts, qds = [], [], [], []
    for n, (blk, d) in enumerate(inst):
        b = bs[n]
        b_ref[n] = b
        tot = b_ref[n, pl.ds((HG_SUB - 1) if d == 0 else 0, nsub, stride=HG_SUB), :]
        mid_f = expand(b_ref[n, pl.ds(HG_SUB // 2, nsub, stride=HG_SUB), :])
        tots.append(tot)
        qts.append(qs[blk] * jnp.exp(b - mid_f))
        kts.append(ks[n] * jnp.exp(mid_f - b))
        qds.append((qs[blk] * jnp.exp(b)).astype(BF16))
        kd = ks[n] * jnp.exp(expand(tot) - b)
        for s in range(nsub):
            km_ref[n, :, s * HEAD_DIM:(s + 1) * HEAD_DIM] = jnp.where(row_sub == s, kd, 0.0).astype(BF16)
    scs = [jnp.where(tri[d], _dot1(qt, kt, _NT), 0.0) for (blk, d), qt, kt in zip(inst, qts, kts)]
    uts = [_dg(vs[blk].astype(BF16), km_ref[n], _TN) for n, (blk, d) in enumerate(inst)]
    outs = [_dot1(sc, vs[blk]) for (blk, d), sc in zip(inst, scs)]
    for d in range(2):
        st = s0_ref[d].T
        for blk in (range(nblk) if d == 0 else range(nblk - 1, -1, -1)):
            n = inst.index((blk, d))
            st = jnp.where(is_ctx, 0.0, st)
            for s in (range(nsub) if d == 0 else range(nsub - 1, -1, -1)):
                stc_ref[n, s] = st.astype(BF16)
                st = st * jnp.exp(tots[n][s:s + 1, :]) + uts[n][:, s * HEAD_DIM:(s + 1) * HEAD_DIM]
            s_ref[blk, d] = st.T
    for n, (blk, d) in enumerate(inst):
        o_int = [_dg(qds[n][s * HG_SUB:(s + 1) * HG_SUB], stc_ref[n, s], _NT) for s in range(nsub)]
        acc_ref[d, blk_rows(blk), :] = outs[n] + jnp.concatenate(o_int, axis=0)
    o_ref[...] = _head_rmsnorm_gate(acc_ref[0] + acc_ref[1], g_ref[...], og_ref[...]).astype(o_ref.dtype)


def _hgrn_call(z, lb, onorm_g, state, layer):
    def col(off):
        return pl.BlockSpec((UNIT, HEAD_DIM), lambda u, h: (u, off // HEAD_DIM + h))
    return pl.pallas_call(
        _hgrn_kernel,
        grid=(N_UNITS, HEADS),
        in_specs=[col(OFF_HQ), col(OFF_HI), col(OFF_HG), col(OFF_HFF), col(OFF_HFB),
                  pl.BlockSpec((2, HEAD_DIM), lambda u, h: (0, h)),
                  pl.BlockSpec((1, HEAD_DIM), lambda u, h: (0, 0)),
                  pl.BlockSpec((None, None, 2, None, HEAD_DIM, HEAD_DIM),
                               lambda u, h: (jnp.maximum(u - N_CTX_UNITS, 0), layer, 0, h, 0, 0))],
        out_specs=[pl.BlockSpec((UNIT, HEAD_DIM), lambda u, h: (u, h)),
                   pl.BlockSpec((None, SEQ_PER_UNIT, 2, None, HEAD_DIM, HEAD_DIM),
                                lambda u, h: (u, 0, 0, h, 0, 0))],
        out_shape=[jax.ShapeDtypeStruct((N_TOK, HG_F), BF16),
                   jax.ShapeDtypeStruct((N_UNITS, SEQ_PER_UNIT, 2, HEADS, HEAD_DIM, HEAD_DIM), F32)],
        scratch_shapes=[pltpu.VMEM((2, UNIT, HEAD_DIM), F32),
                        pltpu.VMEM((HG_INST, HG_BLOCK, HEAD_DIM), F32),
                        pltpu.VMEM((HG_INST, HG_BLOCK // HG_SUB, HEAD_DIM, HEAD_DIM), BF16),
                        pltpu.VMEM((HG_INST, HG_BLOCK, (HG_BLOCK // HG_SUB) * HEAD_DIM), BF16)],
        compiler_params=_cparams(("arbitrary", "arbitrary")),
        name="hgrn2_scan",
    )(z, z, z, z, z, lb, onorm_g.reshape(1, HEAD_DIM), state)


def _gmlp_kernel(u_ref, v_ref, vn_ref, ws_ref, bs_ref, o_ref):
    tm = u_ref.shape[0]
    for g in range(CM_GROUPS):
        cols = slice(g * HEAD_DIM, (g + 1) * HEAD_DIM)
        vg = _gelu(v_ref[:, cols])
        vg = vg * lax.rsqrt(jnp.mean(vg * vg, axis=-1, keepdims=True) + EPS) * vn_ref[:, cols]
        ug = _gelu(u_ref[:, cols])
        w = ws_ref[g]
        bias = bs_ref[:, g:g + 1]
        for ch in range(tm // CM_CHUNK):
            rows = slice(ch * CM_CHUNK, (ch + 1) * CM_CHUNK)
            s = _dot3(w, vg[rows]) + bias
            o_ref[rows, cols] = (ug[rows] * s).astype(o_ref.dtype)


def _gmlp_call(z, vnorm_g, ws, bs_t):
    tm = 512
    return pl.pallas_call(
        _gmlp_kernel,
        grid=(N_TOK // tm,),
        in_specs=[pl.BlockSpec((tm, CM_W), lambda i: (i, OFF_CU // CM_W)),
                  pl.BlockSpec((tm, CM_W), lambda i: (i, OFF_CV // CM_W)),
                  pl.BlockSpec((1, CM_W), lambda i: (0, 0)),
                  pl.BlockSpec((CM_GROUPS, CM_CHUNK, CM_CHUNK), lambda i: (0, 0, 0)),
                  pl.BlockSpec((CM_CHUNK, CM_GROUPS), lambda i: (0, 0))],
        out_specs=pl.BlockSpec((tm, CM_W), lambda i: (i, 0)),
        out_shape=jax.ShapeDtypeStruct((N_TOK, CM_W), BF16),
        compiler_params=_cparams(("arbitrary",)),
        name="chunk_gmlp",
    )(z, z, vnorm_g.reshape(1, CM_W), ws, bs_t)


GP_BLOCK = 256


def _gdn_gates_kernel(ab_ref, alog_ref, dt_ref, col_ref, rowt_ref):
    ab = ab_ref[...]
    lane = lax.broadcasted_iota(jnp.int32, ab.shape, 1)
    g = jnp.where(lane < 2 * HEADS, -jnp.exp(alog_ref[...]) * _softplus(ab + dt_ref[...]), 0.0)
    r = lax.broadcasted_iota(jnp.int32, (GP_BLOCK, GP_BLOCK), 0)
    c = lax.broadcasted_iota(jnp.int32, (GP_BLOCK, GP_BLOCK), 1)
    same = (r // GDN_CHUNK) == (c // GDN_CHUNK)
    tri_f = jnp.where(jnp.logical_and(same, c <= r), 1.0, 0.0).astype(BF16)
    tri_b = jnp.where(jnp.logical_and(same, c >= r), 1.0, 0.0).astype(BF16)
    cf = _dot01(tri_f, g)
    cb = _dot01(tri_b, g)
    col = jnp.where(lane < HEADS, cf, jnp.where(lane < 2 * HEADS, cb, _sigmoid(ab)))
    col_ref[...] = col
    rowt_ref[...] = col.T


def _gdn_gates_call(ab, a_log, dt_bias):
    pad = lambda t: jnp.pad(t.reshape(1, 2 * HEADS), ((0, 0), (0, LANES - 2 * HEADS)))
    return pl.pallas_call(
        _gdn_gates_kernel,
        grid=(N_TOK // GP_BLOCK,),
        in_specs=[pl.BlockSpec((GP_BLOCK, LANES), lambda i: (i, 0)),
                  pl.BlockSpec((1, LANES), lambda i: (0, 0)),
                  pl.BlockSpec((1, LANES), lambda i: (0, 0))],
        out_specs=[pl.BlockSpec((GP_BLOCK, LANES), lambda i: (i, 0)),
                   pl.BlockSpec((LANES, GP_BLOCK), lambda i: (0, i))],
        out_shape=[jax.ShapeDtypeStruct((N_TOK, LANES), F32),
                   jax.ShapeDtypeStruct((LANES, N_TOK), F32)],
        compiler_params=_cparams(("arbitrary",)),
        name="gdn_gates",
    )(ab, pad(a_log), pad(dt_bias))


CONV_PAD = 72
GDN_AQ = HEAD_DIM + GDN_CHUNK
GDN_STEP = 16 * GDN_CHUNK


def _gdn_kernel(qr_ref, kr_ref, vr_ref, og_ref, col_ref, rowt_ref, cwq_ref, cwk_ref, cwv_ref,
                g_ref, s0_ref, o_ref, s_ref,
                xp_ref, q_s, k_s, v_s, o_s, b_s, aq_s):
    is_ctx = pl.program_id(0) < N_CTX_UNITS
    head = pl.program_id(1)
    n_chunks = UNIT // GDN_CHUNK
    chunks_per_seq = SEQ // GDN_CHUNK

    t = lax.broadcasted_iota(jnp.int32, (UNIT, 1), 0)
    period = jnp.where(is_ctx, SEQ, GRID_W)
    pos = jnp.bitwise_and(t, period - 1)
    ok_left = pos != 0
    ok_right = pos != period - 1
    zeros_pad = jnp.zeros((CONV_PAD, HEAD_DIM), F32)
    xp_ref[0:CONV_PAD, :] = zeros_pad
    xp_ref[CONV_PAD + UNIT:CONV_PAD + UNIT + CONV_PAD, :] = zeros_pad

    def conv_silu(x_ref, w_ref):
        xp_ref[CONV_PAD:CONV_PAD + UNIT, :] = x_ref[...]
        acc = jnp.zeros((UNIT, HEAD_DIM), F32)
        for i in range(CONV_K):
            for j in range(CONV_K):
                w = w_ref[CONV_K * i + j:CONV_K * i + j + 1, :]
                if i != CONV_K // 2:
                    w = jnp.where(is_ctx, 0.0, w)
                start = CONV_PAD + (i - 1) * GRID_W + (j - 1)
                xs = xp_ref[start:start + UNIT, :]
                if j == 0:
                    xs = jnp.where(ok_left, xs, 0.0)
                elif j == CONV_K - 1:
                    xs = jnp.where(ok_right, xs, 0.0)
                acc = acc + xs * w
        return _silu(acc)

    def l2norm(x):
        return x * lax.rsqrt(jnp.sum(x * x, axis=-1, keepdims=True) + EPS)

    q_s[...] = l2norm(conv_silu(qr_ref, cwq_ref)) * (HEAD_DIM ** -0.5)
    k_s[...] = l2norm(conv_silu(kr_ref, cwk_ref))
    v_s[...] = conv_silu(vr_ref, cwv_ref)

    rr = lax.broadcasted_iota(jnp.int32, (GDN_CHUNK, GDN_CHUNK), 0)
    cc = lax.broadcasted_iota(jnp.int32, (GDN_CHUNK, GDN_CHUNK), 1)
    eye = jnp.where(rr == cc, 1.0, 0.0)
    same_blk = [(rr // b) == (cc // b) for b in (8, 16, 32, 64)]
    lane = lax.broadcasted_iota(jnp.int32, (GDN_STEP, LANES), 1)
    sub8 = lax.broadcasted_iota(jnp.int32, (HEADS, GDN_STEP), 0)

    def pick(x, j):
        return jnp.sum(jnp.where(lane[:x.shape[0]] == j, x, 0.0), axis=-1, keepdims=True)

    incl = (cc <= rr, cc >= rr)
    strict = (cc < rr, cc > rr)
    off_masks = [jnp.logical_and(same_blk[lvl], jnp.logical_not(same_blk[lvl - 1]))
                 for lvl in range(1, len(same_blk))]

    def phase1(p, carry):
        rows = pl.ds(pl.multiple_of(p * GDN_STEP, GDN_STEP), GDN_STEP)
        q2, k2, v2 = q_s[rows, :], k_s[rows, :], v_s[rows, :]
        col = col_ref[rows, :]
        gcols = [pick(col, d * HEADS + head) for d in range(2)]
        betas = [pick(col, (2 + d) * HEADS + head) for d in range(2)]
        grows = [jnp.sum(jnp.where(sub8 == head, rowt_ref[d * HEADS:(d + 1) * HEADS, rows], 0.0),
                         axis=0, keepdims=True) for d in range(2)]
        inst = []
        for half in range(GDN_STEP // GDN_CHUNK):
            sl = slice(half * GDN_CHUNK, (half + 1) * GDN_CHUNK)
            qc, kc, vc = q2[sl], k2[sl], v2[sl]
            kk = _dot1(kc, kc, _NT)
            qk = _dot1(qc, kc, _NT)
            for d in range(2):
                gcol, beta = gcols[d][sl], betas[d][sl]
                decay = jnp.where(incl[d], jnp.exp(gcol - grows[d][:, sl]), 0.0)
                m = jnp.where(strict[d], beta * kk * decay, 0.0)
                inst.append((half, d, qc, kc, vc, qk, gcol, beta, decay, m))
        ms = [t[-1] for t in inst]
        pws = [jnp.where(same_blk[0], -m, 0.0) for m in ms]
        tinvs = [eye + pw for pw in pws]
        for _ in range(2):
            pws = [_dot_inv(pw, pw) for pw in pws]
            tinvs = [t + _dot_inv(t, pw) for t, pw in zip(tinvs, pws)]
        for mask in off_masks:
            tmp = [_dot_inv(jnp.where(mask, m, 0.0), t) for m, t in zip(ms, tinvs)]
            tinvs = [t - _dot_inv(t, x) for t, x in zip(tinvs, tmp)]
        uws = [_dot1(tinv, jnp.concatenate([vc * beta, kc * (beta * jnp.exp(gcol))], axis=1)).astype(BF16)
               for (half, d, qc, kc, vc, qk, gcol, beta, decay, m), tinv in zip(inst, tinvs)]
        bas, ows = [], []
        for (half, d, qc, kc, vc, qk, gcol, beta, decay, m), uw in zip(inst, uws):
            glast = gcol[GDN_CHUNK - 1:GDN_CHUNK] if d == 0 else gcol[0:1]
            bas.append(_dg((kc * jnp.exp(glast - gcol)).astype(BF16), uw, _TN))
            ows.append(_dg((qk * decay).astype(BF16), uw, _NN))
        for (half, d, qc, kc, vc, qk, gcol, beta, decay, m), ba, ow in zip(inst, bas, ows):
            chunk0 = p * GDN_STEP + half * GDN_CHUNK
            o_s[d, pl.ds(pl.multiple_of(chunk0, GDN_CHUNK), GDN_CHUNK), :] = ow[:, :HEAD_DIM]
            b_s[d, pl.ds(pl.multiple_of(2 * chunk0, HEAD_DIM), HEAD_DIM), :] = ba[:, :HEAD_DIM]
            aq_s[d, pl.ds(pl.multiple_of(3 * chunk0, GDN_AQ), GDN_AQ), :] = jnp.concatenate(
                [ba[:, HEAD_DIM:], qc * jnp.exp(gcol) - ow[:, HEAD_DIM:]], axis=0).astype(BF16)
        return carry

    lax.fori_loop(0, UNIT // GDN_STEP, phase1, 0)

    lane1 = lax.broadcasted_iota(jnp.int32, (1, LANES), 1)

    def phase2(i, carry):
        chs = (i, n_chunks - 1 - i)
        firsts = [(chs[0] % chunks_per_seq) == 0, (chs[1] % chunks_per_seq) == chunks_per_seq - 1]
        ss = [jnp.where(jnp.logical_and(is_ctx, f), 0.0, s) for f, s in zip(firsts, carry)]
        rs = [_dg(aq_s[d, pl.ds(pl.multiple_of(chs[d] * GDN_AQ, GDN_AQ), GDN_AQ), :],
                  ss[d].astype(BF16), _NN) for d in range(2)]
        new = []
        for d in range(2):
            tile0 = pl.multiple_of(chs[d] * GDN_CHUNK + (GDN_CHUNK - 8 if d == 0 else 0), 8)
            last = col_ref[pl.ds(tile0, 8), :]
            last = last[7:8] if d == 0 else last[0:1]
            glast = jnp.sum(jnp.where(lane1 == d * HEADS + head, last, 0.0), axis=-1, keepdims=True)
            b = b_s[d, pl.ds(pl.multiple_of(chs[d] * HEAD_DIM, HEAD_DIM), HEAD_DIM), :]
            s = ss[d] * jnp.exp(glast) + (b - rs[d][:HEAD_DIM])
            s_ref[chs[d] // chunks_per_seq, d] = s
            new.append(s)
            rows = pl.ds(pl.multiple_of(chs[d] * GDN_CHUNK, GDN_CHUNK), GDN_CHUNK)
            o_s[d, rows, :] = o_s[d, rows, :] + rs[d][HEAD_DIM:]
        return tuple(new)

    lax.fori_loop(0, n_chunks, phase2, (s0_ref[0], s0_ref[1]))
    o_ref[...] = _head_rmsnorm_gate(o_s[0] + o_s[1], g_ref[...], og_ref[...]).astype(o_ref.dtype)


def _gdn_call(z, col, rowt, conv_w, onorm_g, state, layer):
    def zcol(off):
        return pl.BlockSpec((UNIT, HEAD_DIM), lambda u, h: (u, off // HEAD_DIM + h))
    def wcol(part):
        return pl.BlockSpec((CONV_K * CONV_K, HEAD_DIM), lambda u, h: (0, part * HEADS + h))
    scr = lambda *shape: pltpu.VMEM(shape, F32)
    return pl.pallas_call(
        _gdn_kernel,
        grid=(N_UNITS, HEADS),
        in_specs=[zcol(OFF_GQ), zcol(OFF_GK), zcol(OFF_GV), zcol(OFF_GG),
                  pl.BlockSpec((UNIT, LANES), lambda u, h: (u, 0)),
                  pl.BlockSpec((LANES, UNIT), lambda u, h: (0, u)),
                  wcol(0), wcol(1), wcol(2),
                  pl.BlockSpec((1, HEAD_DIM), lambda u, h: (0, 0)),
                  pl.BlockSpec((None, None, 2, None, HEAD_DIM, HEAD_DIM),
                               lambda u, h: (jnp.maximum(u - N_CTX_UNITS, 0), layer, 0, h, 0, 0))],
        out_specs=[pl.BlockSpec((UNIT, HEAD_DIM), lambda u, h: (u, h)),
                   pl.BlockSpec((None, SEQ_PER_UNIT, 2, None, HEAD_DIM, HEAD_DIM),
                                lambda u, h: (u, 0, 0, h, 0, 0))],
        out_shape=[jax.ShapeDtypeStruct((N_TOK, HG_F), BF16),
                   jax.ShapeDtypeStruct((N_UNITS, SEQ_PER_UNIT, 2, HEADS, HEAD_DIM, HEAD_DIM), F32)],
        scratch_shapes=[scr(UNIT + 2 * CONV_PAD, HEAD_DIM),
                        scr(UNIT, HEAD_DIM), scr(UNIT, HEAD_DIM), scr(UNIT, HEAD_DIM),
                        scr(2, UNIT, HEAD_DIM),
                        scr(2, (UNIT // GDN_CHUNK) * HEAD_DIM, HEAD_DIM),
                        pltpu.VMEM((2, (UNIT // GDN_CHUNK) * GDN_AQ, HEAD_DIM), BF16)],
        compiler_params=_cparams(("arbitrary", "arbitrary")),
        name="gdn_scan",
    )(z, z, z, z, col, rowt, conv_w, conv_w, conv_w, onorm_g.reshape(1, HEAD_DIM), state)


def kernel(x_prompt, x_sample, c, state_hgrn, state_gdn, c_ctx, norm1_g, norm2_g, w_mod, b_mod, w_in, hg_lb, hg_onorm_g, cm_vnorm_g, cm_ws, cm_bs, gdn_conv, gdn_A_log, gdn_dt_bias, gdn_onorm_g, w_br_hg, w_br_cm, w_br_gdn, w_out, w_ff1, w_ff2, final_g):
    x = jnp.concatenate([x_prompt.reshape(N_CTX_TOK, D_MODEL), x_sample.reshape(N_LAT_TOK, D_MODEL)], axis=0)
    cvec = jnp.concatenate([c_ctx[None, :], c, jnp.zeros((MOD_ROWS - 1 - DEC_BATCH, D_MODEL), F32)], axis=0)
    mod3 = _mod_call(cvec, w_mod, b_mod).reshape(DEPTH * MOD_ROWS, 1, 6 * D_MODEL)

    lb_all = jnp.cumsum(jax.nn.softmax(hg_lb.astype(F32), axis=0), axis=0)
    lb_all = lb_all - lb_all[:1]

    w_in_t = jnp.swapaxes(w_in, 1, 2)
    w_ff2_bf = w_ff2.astype(BF16)

    new_hg, new_gdn = [], []
    for l in range(DEPTH):
        h = _normmod_call(x, norm1_g[l], mod3, l, 0, 1)
        z = _mm_call(h, w_in_t, l, col0=0, n=N_MAIN, act=None, out_dtype=F32, tm=1024, tn=1024,
                     w_transposed=True, name="in_proj")
        ab = _mm_call(h, w_in_t, l, col0=OFF_AB, n=LANES, act=None, out_dtype=F32, tm=1024, tn=LANES,
                      w_transposed=True, name="in_proj_ab")
        gates = _mm_call(h, w_in_t, l, col0=OFF_GATES, n=3 * D_MODEL, act="sigmoid", out_dtype=F32,
                         tm=1024, tn=1024, w_transposed=True, name="in_proj_gates")

        o_a, s_hg = _hgrn_call(z, lb_all[l], hg_onorm_g[l], state_hgrn, l)
        o_b = _gmlp_call(z, cm_vnorm_g[l], cm_ws[l], cm_bs[l].T)
        col, rowt = _gdn_gates_call(ab, gdn_A_log[l], gdn_dt_bias[l])
        o_c, s_gdn = _gdn_call(z, col, rowt, gdn_conv[l].reshape(CONV_K * CONV_K, 3 * HG_F),
                               gdn_onorm_g[l], state_gdn, l)
        new_hg.append(s_hg[:N_CTX_UNITS].reshape(BATCH, 2, HEADS, HEAD_DIM, HEAD_DIM))
        new_gdn.append(s_gdn[:N_CTX_UNITS].reshape(BATCH, 2, HEADS, HEAD_DIM, HEAD_DIM))

        merged = _merge_call(o_a, o_b, o_c, w_br_hg, w_br_cm, w_br_gdn, gates, l)
        x = _mm_call(merged, w_out, l, col0=0, n=D_MODEL, act=None, out_dtype=F32, tm=1024, tn=1024,
                     name="out_proj", resid=x, mod3=mod3, k_gate=2)

        h2 = _normmod_call(x, norm2_g[l], mod3, l, 3, 4)
        up = _mm_call(h2, w_ff1, l, col0=0, n=D_FF, act="relu2", out_dtype=BF16, tm=1024, tn=1024, name="ffn_up")
        x = _mm_resid_call(up, w_ff2_bf, x, mod3, l, 5, tm=512, tn=512, name="ffn_down")

    y_prompt = _final_norm_call(x, final_g, 0, N_CTX_TOK).reshape(BATCH, SEQ, D_MODEL)
    y_sample = _final_norm_call(x, final_g, N_CTX_TOK, N_LAT_TOK).reshape(DEC_BATCH, DEC_SEQ, D_MODEL)
    return (y_prompt, y_sample, jnp.stack(new_hg, axis=1), jnp.stack(new_gdn, axis=1))
```

```python
import functools

import jax
import jax.numpy as jnp
from jax import lax
from jax.experimental import pallas as pl
from jax.experimental.pallas import tpu as pltpu

F32 = jnp.float32
BF16 = jnp.bfloat16

D_MODEL = 2048
BATCH = 16
SEQ = 256
DEPTH = 2
DEC_BATCH = 4
DEC_SEQ = 1024
GRID_W = 64
EPS = 1e-6
D_FF = 4 * D_MODEL
HEADS = 8
HEAD_DIM = 128
HG_F = HEADS * HEAD_DIM
CM_GROUPS = 8
CM_W = CM_GROUPS * HEAD_DIM
CM_CHUNK = 128
GDN_CHUNK = 64
CONV_K = 3

OFF_HQ, OFF_HI, OFF_HG, OFF_HFF, OFF_HFB = 0, 1024, 2048, 3072, 4096
OFF_CU, OFF_CV = 5120, 6144
OFF_GQ, OFF_GK, OFF_GV, OFF_GG = 7168, 8192, 9216, 10240
OFF_AB = 11264
OFF_GATES = 11296
IN_DIM = 17440
N_MAIN = OFF_AB

N_CTX_TOK = BATCH * SEQ
N_LAT_TOK = DEC_BATCH * DEC_SEQ
N_TOK = N_CTX_TOK + N_LAT_TOK
UNIT = DEC_SEQ
N_CTX_UNITS = N_CTX_TOK // UNIT
N_UNITS = N_TOK // UNIT
SEQ_PER_UNIT = UNIT // SEQ

LANES = 128
MOD_ROWS = 8
HG_BLOCK = 256
HG_SUB = 32
HG_INST = 2 * (UNIT // HG_BLOCK)
VMEM_LIMIT = 56 * 1024 * 1024

_NT = (((1,), (1,)), ((), ()))
_TN = (((0,), (0,)), ((), ()))
_NN = (((1,), (0,)), ((), ()))


def _dg(a, b, dims):
    return lax.dot_general(a, b, dims, preferred_element_type=F32)


def _split2(x):
    hi = x.astype(BF16)
    lo = (x - hi.astype(F32)).astype(BF16)
    return hi, lo


def _dot1(a, b, dims=_NN):
    return _dg(a.astype(BF16), b.astype(BF16), dims)


def _dot3(a, b, dims=_NN):
    ah, al = _split2(a)
    bh, bl = _split2(b)
    return _dg(ah, bh, dims) + (_dg(ah, bl, dims) + _dg(al, bh, dims))


_dot_inv = _dot1


def _dot01(m, x):
    hi = x.astype(BF16)
    r = x - hi.astype(F32)
    mid = r.astype(BF16)
    lo = (r - mid.astype(F32)).astype(BF16)
    return _dg(m, hi, _NN) + (_dg(m, mid, _NN) + _dg(m, lo, _NN))


def _dot01_2(m, x):
    hi, lo = _split2(x)
    return _dg(m, hi, _NN) + _dg(m, lo, _NN)


def _sigmoid(x):
    return 1.0 / (1.0 + jnp.exp(-x))


def _silu(x):
    return x * _sigmoid(x)


def _gelu(x):
    return 0.5 * x * (1.0 + lax.erf(x * (2.0 ** -0.5)))


def _softplus(x):
    return jnp.maximum(x, 0.0) + jnp.log1p(jnp.exp(-jnp.abs(x)))


def _mod_row_of_tile(i, tm):
    return jnp.maximum(0, (i * tm - N_CTX_TOK) // DEC_SEQ + 1)


def _cparams(sem):
    return pltpu.CompilerParams(dimension_semantics=sem, vmem_limit_bytes=VMEM_LIMIT)


def _mod_kernel(c_ref, w_ref, b_ref, o_ref):
    s = _silu(c_ref[...])
    hi, lo = _split2(s)
    w = w_ref[...].astype(BF16)
    o_ref[...] = _dg(hi, w, _NN) + _dg(lo, w, _NN) + b_ref[...]


def _mod_call(cvec, w_mod, b_mod):
    tn = 1024
    n = 6 * D_MODEL
    return pl.pallas_call(
        _mod_kernel,
        grid=(DEPTH, n // tn),
        in_specs=[pl.BlockSpec((MOD_ROWS, D_MODEL), lambda l, j: (0, 0)),
                  pl.BlockSpec((None, D_MODEL, tn), lambda l, j: (l, 0, j)),
                  pl.BlockSpec((None, 1, tn), lambda l, j: (l, 0, j))],
        out_specs=pl.BlockSpec((None, MOD_ROWS, tn), lambda l, j: (l, 0, j)),
        out_shape=jax.ShapeDtypeStruct((DEPTH, MOD_ROWS, n), F32),
        compiler_params=_cparams(("arbitrary", "arbitrary")),
        name="modulation",
    )(cvec, w_mod, b_mod.reshape(DEPTH, 1, n))


def _normmod_kernel(x_ref, g_ref, sh_ref, sc_ref, o_ref):
    x = x_ref[...]
    y = x * lax.rsqrt(jnp.mean(x * x, axis=-1, keepdims=True) + EPS) * g_ref[...]
    o_ref[...] = (y * (1.0 + sc_ref[...]) + sh_ref[...]).astype(o_ref.dtype)


def _normmod_call(x, g, mod3, layer, k_shift, k_scale):
    tm = 512
    def mod_spec(k):
        return pl.BlockSpec((None, 1, D_MODEL),
                            lambda i: (layer * MOD_ROWS + _mod_row_of_tile(i, tm), 0, k))
    return pl.pallas_call(
        _normmod_kernel,
        grid=(N_TOK // tm,),
        in_specs=[pl.BlockSpec((tm, D_MODEL), lambda i: (i, 0)),
                  pl.BlockSpec((1, D_MODEL), lambda i: (0, 0)),
                  mod_spec(k_shift), mod_spec(k_scale)],
        out_specs=pl.BlockSpec((tm, D_MODEL), lambda i: (i, 0)),
        out_shape=jax.ShapeDtypeStruct((N_TOK, D_MODEL), BF16),
        compiler_params=_cparams(("arbitrary",)),
        name="norm_mod",
    )(x, g.reshape(1, D_MODEL), mod3, mod3)


def _final_norm_kernel(x_ref, g_ref, o_ref):
    x = x_ref[...]
    o_ref[...] = x * lax.rsqrt(jnp.mean(x * x, axis=-1, keepdims=True) + EPS) * g_ref[...]


def _final_norm_call(x, g, row0, n_rows):
    tm = 512
    return pl.pallas_call(
        _final_norm_kernel,
        grid=(n_rows // tm,),
        in_specs=[pl.BlockSpec((tm, D_MODEL), lambda i: (row0 // tm + i, 0)),
                  pl.BlockSpec((1, D_MODEL), lambda i: (0, 0))],
        out_specs=pl.BlockSpec((tm, D_MODEL), lambda i: (i, 0)),
        out_shape=jax.ShapeDtypeStruct((n_rows, D_MODEL), F32),
        compiler_params=_cparams(("arbitrary",)),
        name="final_norm",
    )(x, g.reshape(1, D_MODEL))


def _mm_kernel(*refs, act, w_transposed, has_resid):
    x_ref, w_ref = refs[:2]
    refs = refs[2:]
    if has_resid:
        r_ref, g_ref = refs[:2]
        refs = refs[2:]
    o_ref, wbf_ref = refs

    @pl.when(pl.program_id(1) == 0)
    def _cast_weights():
        wbf_ref[...] = w_ref[...].reshape(wbf_ref.shape).astype(BF16)

    acc = lax.dot_general(x_ref[...], wbf_ref[...], _NT if w_transposed else _NN, preferred_element_type=F32)
    if act == "sigmoid":
        acc = _sigmoid(acc)
    elif act == "relu2":
        acc = jnp.square(jnp.maximum(acc, 0.0))
    if has_resid:
        acc = r_ref[...] + g_ref[...] * acc
    o_ref[...] = acc.astype(o_ref.dtype)


def _mm_call(x, w_all, layer, *, col0, n, act, out_dtype, tm, tn, name, w_transposed=False,
             resid=None, mod3=None, k_gate=None):
    m, k = x.shape
    assert n % tn == 0 and m % tm == 0
    if w_transposed:
        assert col0 % 8 == 0
        w_spec = pl.BlockSpec((pl.Element(1), pl.Element(tn), pl.Element(k)),
                              lambda j, i: (layer, pl.multiple_of(col0 + j * tn, 8), 0))
        w_scratch = pltpu.VMEM((tn, k), BF16)
    else:
        assert col0 % tn == 0
        w_spec = pl.BlockSpec((None, k, tn), lambda j, i: (layer, 0, col0 // tn + j))
        w_scratch = pltpu.VMEM((k, tn), BF16)
    in_specs = [pl.BlockSpec((tm, k), lambda j, i: (i, 0)), w_spec]
    args = [x, w_all]
    if resid is not None:
        per_tile = D_MODEL // tn
        in_specs += [pl.BlockSpec((tm, tn), lambda j, i: (i, j)),
                     pl.BlockSpec((None, 1, tn),
                                  lambda j, i: (layer * MOD_ROWS + _mod_row_of_tile(i, tm), 0,
                                                k_gate * per_tile + j))]
        args += [resid, mod3]
    return pl.pallas_call(
        functools.partial(_mm_kernel, act=act, w_transposed=w_transposed, has_resid=resid is not None),
        grid=(n // tn, m // tm),
        in_specs=in_specs,
        out_specs=pl.BlockSpec((tm, tn), lambda j, i: (i, j)),
        out_shape=jax.ShapeDtypeStruct((m, n), out_dtype),
        scratch_shapes=[w_scratch],
        compiler_params=_cparams(("arbitrary", "arbitrary")),
        name=name,
    )(*args)


def _mm_bf16_kernel(x_ref, w_ref, r_ref, g_ref, o_ref):
    acc = jnp.dot(x_ref[...], w_ref[...], preferred_element_type=F32)
    o_ref[...] = r_ref[...] + g_ref[...] * acc


def _mm_resid_call(x, w, resid, mod3, layer, k_gate, *, tm, tn, name):
    m, k = x.shape
    n = w.shape[2]
    per_tile = D_MODEL // tn
    return pl.pallas_call(
        _mm_bf16_kernel,
        grid=(m // tm, n // tn),
        in_specs=[pl.BlockSpec((tm, k), lambda i, j: (i, 0)),
                  pl.BlockSpec((None, k, tn), lambda i, j: (layer, 0, j)),
                  pl.BlockSpec((tm, tn), lambda i, j: (i, j)),
                  pl.BlockSpec((None, 1, tn),
                               lambda i, j: (layer * MOD_ROWS + _mod_row_of_tile(i, tm), 0,
                                             k_gate * per_tile + j))],
        out_specs=pl.BlockSpec((tm, tn), lambda i, j: (i, j)),
        out_shape=jax.ShapeDtypeStruct((m, n), F32),
        compiler_params=_cparams(("arbitrary", "arbitrary")),
        name=name,
    )(x, w, resid, mod3)


def _adaln(x, g, shift, scale):
    y = x * lax.rsqrt(jnp.mean(x * x, axis=-1, keepdims=True) + EPS) * g
    return y * (1.0 + scale) + shift


def _outproj_norm_kernel(m_ref, w_ref, x_ref, gate_ref, g_ref, sh_ref, sc_ref, xo_ref, h_ref, wbf_ref):
    @pl.when(pl.program_id(0) == 0)
    def _cast_weights():
        wbf_ref[...] = w_ref[...].astype(BF16)

    acc = jnp.dot(m_ref[...], wbf_ref[...], preferred_element_type=F32)
    xn = x_ref[...] + gate_ref[...] * acc
    xo_ref[...] = xn
    h_ref[...] = _adaln(xn, g_ref[...], sh_ref[...], sc_ref[...]).astype(h_ref.dtype)


def _outproj_norm_call(merged, w_out, x, norm_g, mod3, layer):
    tm = 512
    def mod_spec(k):
        return pl.BlockSpec((None, 1, D_MODEL),
                            lambda i: (layer * MOD_ROWS + _mod_row_of_tile(i, tm), 0, k))
    row = pl.BlockSpec((tm, D_MODEL), lambda i: (i, 0))
    return pl.pallas_call(
        _outproj_norm_kernel,
        grid=(N_TOK // tm,),
        in_specs=[row,
                  pl.BlockSpec((None, D_MODEL, D_MODEL), lambda i: (layer, 0, 0), pipeline_mode=pl.Buffered(1)),
                  row, mod_spec(2),
                  pl.BlockSpec((1, D_MODEL), lambda i: (0, 0)), mod_spec(3), mod_spec(4)],
        out_specs=[row, row],
        out_shape=[jax.ShapeDtypeStruct((N_TOK, D_MODEL), F32),
                   jax.ShapeDtypeStruct((N_TOK, D_MODEL), BF16)],
        scratch_shapes=[pltpu.VMEM((D_MODEL, D_MODEL), BF16)],
        compiler_params=_cparams(("arbitrary",)),
        name="out_proj_norm",
    )(merged, w_out, x, mod3, norm_g.reshape(1, D_MODEL), mod3, mod3)


def _ffn_down_norm_kernel(x_ref, w_ref, r_ref, gate_ref, g_ref, sh_ref, sc_ref, o_ref, h_ref, row_ref):
    j = pl.program_id(1)
    tn = o_ref.shape[1]
    acc = jnp.dot(x_ref[...], w_ref[...], preferred_element_type=F32)
    xn = r_ref[...] + gate_ref[...] * acc
    o_ref[...] = xn
    row_ref[:, pl.ds(pl.multiple_of(j * tn, tn), tn)] = xn

    @pl.when(j == pl.num_programs(1) - 1)
    def _next_layer_norm():
        h_ref[...] = _adaln(row_ref[...], g_ref[...], sh_ref[...], sc_ref[...]).astype(h_ref.dtype)


def _ffn_down_norm_call(up, w_bf, resid, mod3, layer, next_norm_g):
    tm, tn = 512, 512
    m, k = up.shape
    per_tile = D_MODEL // tn
    def mod_row(i):
        return _mod_row_of_tile(i, tm)
    def next_mod(kk):
        return pl.BlockSpec((None, 1, D_MODEL), lambda i, j: ((layer + 1) * MOD_ROWS + mod_row(i), 0, kk))
    return pl.pallas_call(
        _ffn_down_norm_kernel,
        grid=(m // tm, D_MODEL // tn),
        in_specs=[pl.BlockSpec((tm, k), lambda i, j: (i, 0)),
                  pl.BlockSpec((None, k, tn), lambda i, j: (layer, 0, j)),
                  pl.BlockSpec((tm, tn), lambda i, j: (i, j)),
                  pl.BlockSpec((None, 1, tn), lambda i, j: (layer * MOD_ROWS + mod_row(i), 0, 5 * per_tile + j)),
                  pl.BlockSpec((1, D_MODEL), lambda i, j: (0, 0)), next_mod(0), next_mod(1)],
        out_specs=[pl.BlockSpec((tm, tn), lambda i, j: (i, j)),
                   pl.BlockSpec((tm, D_MODEL), lambda i, j: (i, 0))],
        out_shape=[jax.ShapeDtypeStruct((m, D_MODEL), F32),
                   jax.ShapeDtypeStruct((m, D_MODEL), BF16)],
        scratch_shapes=[pltpu.VMEM((tm, D_MODEL), F32)],
        compiler_params=_cparams(("arbitrary", "arbitrary")),
        name="ffn_down_norm",
    )(up, w_bf, resid, mod3, next_norm_g.reshape(1, D_MODEL), mod3, mod3)


def _merge_kernel(oa_ref, ob_ref, oc_ref, wa_ref, wb_ref, wc_ref, ga_ref, gb_ref, gc_ref, o_ref, wbf_ref):
    @pl.when(pl.program_id(1) == 0)
    def _cast_weights():
        wbf_ref[0] = wa_ref[...].astype(BF16)
        wbf_ref[1] = wb_ref[...].astype(BF16)
        wbf_ref[2] = wc_ref[...].astype(BF16)

    a = jnp.dot(oa_ref[...], wbf_ref[0], preferred_element_type=F32)
    b = jnp.dot(ob_ref[...], wbf_ref[1], preferred_element_type=F32)
    c = jnp.dot(oc_ref[...], wbf_ref[2], preferred_element_type=F32)
    o_ref[...] = (ga_ref[...] * a + gb_ref[...] * b + gc_ref[...] * c).astype(o_ref.dtype)


def _merge_call(o_a, o_b, o_c, w_a, w_b, w_c, gates, layer):
    tm, tn = 1024, 512
    nj = D_MODEL // tn
    br = pl.BlockSpec((tm, HG_F), lambda j, i: (i, 0))
    wt = pl.BlockSpec((None, HG_F, tn), lambda j, i: (layer, 0, j))
    def gate_spec(k):
        return pl.BlockSpec((tm, tn), lambda j, i: (i, k * nj + j))
    return pl.pallas_call(
        _merge_kernel,
        grid=(nj, N_TOK // tm),
        in_specs=[br, br, br, wt, wt, wt, gate_spec(0), gate_spec(1), gate_spec(2)],
        out_specs=pl.BlockSpec((tm, tn), lambda j, i: (i, j)),
        out_shape=jax.ShapeDtypeStruct((N_TOK, D_MODEL), BF16),
        scratch_shapes=[pltpu.VMEM((3, HG_F, tn), BF16)],
        compiler_params=_cparams(("arbitrary", "arbitrary")),
        name="branch_merge",
    )(o_a, o_b, o_c, w_a, w_b, w_c, gates, gates, gates)


def _head_rmsnorm_gate(o, g, og):
    y = o * lax.rsqrt(jnp.mean(o * o, axis=-1, keepdims=True) + EPS) * g
    return y * _silu(og)


def _hgrn_kernel(q_ref, i_ref, og_ref, ff_ref, fb_ref, lb_ref, g_ref, s0_ref, o_ref, s_ref,
                 acc_ref, b_ref, stc_ref, km_ref):
    is_ctx = pl.program_id(0) < N_CTX_UNITS
    nblk = UNIT // HG_BLOCK
    nsub = HG_BLOCK // HG_SUB
    r = lax.broadcasted_iota(jnp.int32, (HG_BLOCK, HG_BLOCK), 0)
    c = lax.broadcasted_iota(jnp.int32, (HG_BLOCK, HG_BLOCK), 1)
    same = (r // HG_SUB) == (c // HG_SUB)
    row_sub = lax.broadcasted_iota(jnp.int32, (HG_BLOCK, HEAD_DIM), 0) // HG_SUB
    tri = (jnp.logical_and(same, c <= r), jnp.logical_and(same, c >= r))
    tri_bf = [jnp.where(t, 1.0, 0.0).astype(BF16) for t in tri]
    z_refs = (ff_ref, fb_ref)
    inst = [(blk, d) for blk in range(nblk) for d in range(2)]

    def expand(x):
        return jnp.broadcast_to(x[:, None, :], (nsub, HG_SUB, HEAD_DIM)).reshape(HG_BLOCK, HEAD_DIM)

    def blk_rows(blk):
        return slice(blk * HG_BLOCK, (blk + 1) * HG_BLOCK)

    qs = [_silu(q_ref[blk_rows(blk), :]) for blk in range(nblk)]
    vs = [i_ref[blk_rows(blk), :] for blk in range(nblk)]
    ks, lfs = [], []
    for blk, d in inst:
        lb = lb_ref[d:d + 1, :]
        f = lb + (1.0 - lb) * _sigmoid(z_refs[d][blk_rows(blk), :])
        lfs.append(jnp.log(f))
        ks.append(1.0 - f)
    bs = [_dot01_2(tri_bf[d], lf) for (blk, d), lf in zip(inst, lfs)]
    tots, qts, kts, qds = [], [], [], []
    for n, (blk, d) in enumerate(inst):
        b = bs[n]
        b_ref[n] = b
        tot = b_ref[n, pl.ds((HG_SUB - 1) if d == 0 else 0, nsub, stride=HG_SUB), :]
        mid_f = expand(b_ref[n, pl.ds(HG_SUB // 2, nsub, stride=HG_SUB), :])
        tots.append(tot)
        qts.append(qs[blk] * jnp.exp(b - mid_f))
        kts.append(ks[n] * jnp.exp(mid_f - b))
        qds.append((qs[blk] * jnp.exp(b)).astype(BF16))
        kd = ks[n] * jnp.exp(expand(tot) - b)
        for s in range(nsub):
            km_ref[n, :, s * HEAD_DIM:(s + 1) * HEAD_DIM] = jnp.where(row_sub == s, kd, 0.0).astype(BF16)
    scs = [jnp.where(tri[d], _dot1(qt, kt, _NT), 0.0) for (blk, d), qt, kt in zip(inst, qts, kts)]
    uts = [_dg(vs[blk].astype(BF16), km_ref[n], _TN) for n, (blk, d) in enumerate(inst)]
    outs = [_dot1(sc, vs[blk]) for (blk, d), sc in zip(inst, scs)]
    for d in range(2):
        st = s0_ref[d].T
        for blk in (range(nblk) if d == 0 else range(nblk - 1, -1, -1)):
            n = inst.index((blk, d))
            st = jnp.where(is_ctx, 0.0, st)
            for s in (range(nsub) if d == 0 else range(nsub - 1, -1, -1)):
                stc_ref[n, s] = st.astype(BF16)
                st = st * jnp.exp(tots[n][s:s + 1, :]) + uts[n][:, s * HEAD_DIM:(s + 1) * HEAD_DIM]
            s_ref[blk, d] = st.T
    for n, (blk, d) in enumerate(inst):
        o_int = [_dg(qds[n][s * HG_SUB:(s + 1) * HG_SUB], stc_ref[n, s], _NT) for s in range(nsub)]
        acc_ref[d, blk_rows(blk), :] = outs[n] + jnp.concatenate(o_int, axis=0)
    o_ref[...] = _head_rmsnorm_gate(acc_ref[0] + acc_ref[1], g_ref[...], og_ref[...]).astype(o_ref.dtype)


def _hgrn_call(z, lb, onorm_g, state, layer):
    def col(off):
        return pl.BlockSpec((UNIT, HEAD_DIM), lambda u, h: (u, off // HEAD_DIM + h))
    return pl.pallas_call(
        _hgrn_kernel,
        grid=(N_UNITS, HEADS),
        in_specs=[col(OFF_HQ), col(OFF_HI), col(OFF_HG), col(OFF_HFF), col(OFF_HFB),
                  pl.BlockSpec((2, HEAD_DIM), lambda u, h: (0, h)),
                  pl.BlockSpec((1, HEAD_DIM), lambda u, h: (0, 0)),
                  pl.BlockSpec((None, None, 2, None, HEAD_DIM, HEAD_DIM),
                               lambda u, h: (jnp.maximum(u - N_CTX_UNITS, 0), layer, 0, h, 0, 0))],
        out_specs=[pl.BlockSpec((UNIT, HEAD_DIM), lambda u, h: (u, h)),
                   pl.BlockSpec((None, SEQ_PER_UNIT, 2, None, HEAD_DIM, HEAD_DIM),
                                lambda u, h: (u, 0, 0, h, 0, 0))],
        out_shape=[jax.ShapeDtypeStruct((N_TOK, HG_F), BF16),
                   jax.ShapeDtypeStruct((N_UNITS, SEQ_PER_UNIT, 2, HEADS, HEAD_DIM, HEAD_DIM), F32)],
        scratch_shapes=[pltpu.VMEM((2, UNIT, HEAD_DIM), F32),
                        pltpu.VMEM((HG_INST, HG_BLOCK, HEAD_DIM), F32),
                        pltpu.VMEM((HG_INST, HG_BLOCK // HG_SUB, HEAD_DIM, HEAD_DIM), BF16),
                        pltpu.VMEM((HG_INST, HG_BLOCK, (HG_BLOCK // HG_SUB) * HEAD_DIM), BF16)],
        compiler_params=_cparams(("arbitrary", "arbitrary")),
        name="hgrn2_scan",
    )(z, z, z, z, z, lb, onorm_g.reshape(1, HEAD_DIM), state)


def _gmlp_kernel(u_ref, v_ref, vn_ref, ws_ref, bs_ref, o_ref):
    tm = u_ref.shape[0]
    for g in range(CM_GROUPS):
        cols = slice(g * HEAD_DIM, (g + 1) * HEAD_DIM)
        vg = _gelu(v_ref[:, cols])
        vg = vg * lax.rsqrt(jnp.mean(vg * vg, axis=-1, keepdims=True) + EPS) * vn_ref[:, cols]
        ug = _gelu(u_ref[:, cols])
        w = ws_ref[g]
        bias = bs_ref[:, g:g + 1]
        for ch in range(tm // CM_CHUNK):
            rows = slice(ch * CM_CHUNK, (ch + 1) * CM_CHUNK)
            s = _dot3(w, vg[rows]) + bias
            o_ref[rows, cols] = (ug[rows] * s).astype(o_ref.dtype)


def _gmlp_call(z, vnorm_g, ws, bs_t):
    tm = 512
    return pl.pallas_call(
        _gmlp_kernel,
        grid=(N_TOK // tm,),
        in_specs=[pl.BlockSpec((tm, CM_W), lambda i: (i, OFF_CU // CM_W)),
                  pl.BlockSpec((tm, CM_W), lambda i: (i, OFF_CV // CM_W)),
                  pl.BlockSpec((1, CM_W), lambda i: (0, 0)),
                  pl.BlockSpec((CM_GROUPS, CM_CHUNK, CM_CHUNK), lambda i: (0, 0, 0)),
                  pl.BlockSpec((CM_CHUNK, CM_GROUPS), lambda i: (0, 0))],
        out_specs=pl.BlockSpec((tm, CM_W), lambda i: (i, 0)),
        out_shape=jax.ShapeDtypeStruct((N_TOK, CM_W), BF16),
        compiler_params=_cparams(("arbitrary",)),
        name="chunk_gmlp",
    )(z, z, vnorm_g.reshape(1, CM_W), ws, bs_t)


GP_BLOCK = 256


def _gdn_gates_kernel(ab_ref, alog_ref, dt_ref, col_ref, rowt_ref):
    ab = ab_ref[...]
    lane = lax.broadcasted_iota(jnp.int32, ab.shape, 1)
    g = jnp.where(lane < 2 * HEADS, -jnp.exp(alog_ref[...]) * _softplus(ab + dt_ref[...]), 0.0)
    r = lax.broadcasted_iota(jnp.int32, (GP_BLOCK, GP_BLOCK), 0)
    c = lax.broadcasted_iota(jnp.int32, (GP_BLOCK, GP_BLOCK), 1)
    same = (r // GDN_CHUNK) == (c // GDN_CHUNK)
    tri_f = jnp.where(jnp.logical_and(same, c <= r), 1.0, 0.0).astype(BF16)
    tri_b = jnp.where(jnp.logical_and(same, c >= r), 1.0, 0.0).astype(BF16)
    cf = _dot01(tri_f, g)
    cb = _dot01(tri_b, g)
    col = jnp.where(lane < HEADS, cf, jnp.where(lane < 2 * HEADS, cb, _sigmoid(ab)))
    col_ref[...] = col
    rowt_ref[...] = col.T


def _gdn_gates_call(ab, a_log, dt_bias):
    pad = lambda t: jnp.pad(t.reshape(1, 2 * HEADS), ((0, 0), (0, LANES - 2 * HEADS)))
    return pl.pallas_call(
        _gdn_gates_kernel,
        grid=(N_TOK // GP_BLOCK,),
        in_specs=[pl.BlockSpec((GP_BLOCK, LANES), lambda i: (i, 0)),
                  pl.BlockSpec((1, LANES), lambda i: (0, 0)),
                  pl.BlockSpec((1, LANES), lambda i: (0, 0))],
        out_specs=[pl.BlockSpec((GP_BLOCK, LANES), lambda i: (i, 0)),
                   pl.BlockSpec((LANES, GP_BLOCK), lambda i: (0, i))],
        out_shape=[jax.ShapeDtypeStruct((N_TOK, LANES), F32),
                   jax.ShapeDtypeStruct((LANES, N_TOK), F32)],
        compiler_params=_cparams(("arbitrary",)),
        name="gdn_gates",
    )(ab, pad(a_log), pad(dt_bias))


CONV_PAD = 72
GDN_AQ = HEAD_DIM + GDN_CHUNK
GDN_STEP = 16 * GDN_CHUNK


def _gdn_kernel(qr_ref, kr_ref, vr_ref, og_ref, col_ref, rowt_ref, cwq_ref, cwk_ref, cwv_ref,
                g_ref, s0_ref, o_ref, s_ref,
                xp_ref, q_s, k_s, v_s, o_s, b_s, aq_s):
    is_ctx = pl.program_id(0) < N_CTX_UNITS
    head = pl.program_id(1)
    n_chunks = UNIT // GDN_CHUNK
    chunks_per_seq = SEQ // GDN_CHUNK

    t = lax.broadcasted_iota(jnp.int32, (UNIT, 1), 0)
    period = jnp.where(is_ctx, SEQ, GRID_W)
    pos = jnp.bitwise_and(t, period - 1)
    ok_left = pos != 0
    ok_right = pos != period - 1
    zeros_pad = jnp.zeros((CONV_PAD, HEAD_DIM), F32)
    xp_ref[0:CONV_PAD, :] = zeros_pad
    xp_ref[CONV_PAD + UNIT:CONV_PAD + UNIT + CONV_PAD, :] = zeros_pad

    def conv_silu(x_ref, w_ref):
        xp_ref[CONV_PAD:CONV_PAD + UNIT, :] = x_ref[...]
        acc = jnp.zeros((UNIT, HEAD_DIM), F32)
        for i in range(CONV_K):
            for j in range(CONV_K):
                w = w_ref[CONV_K * i + j:CONV_K * i + j + 1, :]
                if i != CONV_K // 2:
                    w = jnp.where(is_ctx, 0.0, w)
                start = CONV_PAD + (i - 1) * GRID_W + (j - 1)
                xs = xp_ref[start:start + UNIT, :]
                if j == 0:
                    xs = jnp.where(ok_left, xs, 0.0)
                elif j == CONV_K - 1:
                    xs = jnp.where(ok_right, xs, 0.0)
                acc = acc + xs * w
        return _silu(acc)

    def l2norm(x):
        return x * lax.rsqrt(jnp.sum(x * x, axis=-1, keepdims=True) + EPS)

    q_s[...] = l2norm(conv_silu(qr_ref, cwq_ref)) * (HEAD_DIM ** -0.5)
    k_s[...] = l2norm(conv_silu(kr_ref, cwk_ref))
    v_s[...] = conv_silu(vr_ref, cwv_ref)

    rr = lax.broadcasted_iota(jnp.int32, (GDN_CHUNK, GDN_CHUNK), 0)
    cc = lax.broadcasted_iota(jnp.int32, (GDN_CHUNK, GDN_CHUNK), 1)
    eye = jnp.where(rr == cc, 1.0, 0.0)
    same_blk = [(rr // b) == (cc // b) for b in (8, 16, 32, 64)]
    lane = lax.broadcasted_iota(jnp.int32, (GDN_STEP, LANES), 1)
    sub8 = lax.broadcasted_iota(jnp.int32, (HEADS, GDN_STEP), 0)

    def pick(x, j):
        return jnp.sum(jnp.where(lane[:x.shape[0]] == j, x, 0.0), axis=-1, keepdims=True)

    incl = (cc <= rr, cc >= rr)
    strict = (cc < rr, cc > rr)
    off_masks = [jnp.logical_and(same_blk[lvl], jnp.logical_not(same_blk[lvl - 1]))
                 for lvl in range(1, len(same_blk))]

    def phase1(p, carry):
        rows = pl.ds(pl.multiple_of(p * GDN_STEP, GDN_STEP), GDN_STEP)
        q2, k2, v2 = q_s[rows, :], k_s[rows, :], v_s[rows, :]
        col = col_ref[rows, :]
        gcols = [pick(col, d * HEADS + head) for d in range(2)]
        betas = [pick(col, (2 + d) * HEADS + head) for d in range(2)]
        grows = [jnp.sum(jnp.where(sub8 == head, rowt_ref[d * HEADS:(d + 1) * HEADS, rows], 0.0),
                         axis=0, keepdims=True) for d in range(2)]
        inst = []
        for half in range(GDN_STEP // GDN_CHUNK):
            sl = slice(half * GDN_CHUNK, (half + 1) * GDN_CHUNK)
            qc, kc, vc = q2[sl], k2[sl], v2[sl]
            kk = _dot1(kc, kc, _NT)
            qk = _dot1(qc, kc, _NT)
            for d in range(2):
                gcol, beta = gcols[d][sl], betas[d][sl]
                decay = jnp.where(incl[d], jnp.exp(gcol - grows[d][:, sl]), 0.0)
                m = jnp.where(strict[d], beta * kk * decay, 0.0)
                inst.append((half, d, qc, kc, vc, qk, gcol, beta, decay, m))
        ms = [t[-1] for t in inst]
        pws = [jnp.where(same_blk[0], -m, 0.0) for m in ms]
        tinvs = [eye + pw for pw in pws]
        for _ in range(2):
            pws = [_dot_inv(pw, pw) for pw in pws]
            tinvs = [t + _dot_inv(t, pw) for t, pw in zip(tinvs, pws)]
        for mask in off_masks:
            tmp = [_dot_inv(jnp.where(mask, m, 0.0), t) for m, t in zip(ms, tinvs)]
            tinvs = [t - _dot_inv(t, x) for t, x in zip(tinvs, tmp)]
        uws = [_dot1(tinv, jnp.concatenate([vc * beta, kc * (beta * jnp.exp(gcol))], axis=1)).astype(BF16)
               for (half, d, qc, kc, vc, qk, gcol, beta, decay, m), tinv in zip(inst, tinvs)]
        bas, ows = [], []
        for (half, d, qc, kc, vc, qk, gcol, beta, decay, m), uw in zip(inst, uws):
            glast = gcol[GDN_CHUNK - 1:GDN_CHUNK] if d == 0 else gcol[0:1]
            bas.append(_dg((kc * jnp.exp(glast - gcol)).astype(BF16), uw, _TN))
            ows.append(_dg((qk * decay).astype(BF16), uw, _NN))
        for (half, d, qc, kc, vc, qk, gcol, beta, decay, m), ba, ow in zip(inst, bas, ows):
            chunk0 = p * GDN_STEP + half * GDN_CHUNK
            o_s[d, pl.ds(pl.multiple_of(chunk0, GDN_CHUNK), GDN_CHUNK), :] = ow[:, :HEAD_DIM]
            b_s[d, pl.ds(pl.multiple_of(2 * chunk0, HEAD_DIM), HEAD_DIM), :] = ba[:, :HEAD_DIM]
            aq_s[d, pl.ds(pl.multiple_of(3 * chunk0, GDN_AQ), GDN_AQ), :] = jnp.concatenate(
                [ba[:, HEAD_DIM:], qc * jnp.exp(gcol) - ow[:, HEAD_DIM:]], axis=0).astype(BF16)
        return carry

    lax.fori_loop(0, UNIT // GDN_STEP, phase1, 0)

    lane1 = lax.broadcasted_iota(jnp.int32, (1, LANES), 1)

    def advance(chains, states):
        rs = [_dg(aq_s[d, pl.ds(pl.multiple_of(ch * GDN_AQ, GDN_AQ), GDN_AQ), :], s.astype(BF16), _NN)
              for (d, ch), s in zip(chains, states)]
        new = []
        for (d, ch), s, r in zip(chains, states, rs):
            tile0 = pl.multiple_of(ch * GDN_CHUNK + (GDN_CHUNK - 8 if d == 0 else 0), 8)
            last = col_ref[pl.ds(tile0, 8), :]
            last = last[7:8] if d == 0 else last[0:1]
            glast = jnp.sum(jnp.where(lane1 == d * HEADS + head, last, 0.0), axis=-1, keepdims=True)
            b = b_s[d, pl.ds(pl.multiple_of(ch * HEAD_DIM, HEAD_DIM), HEAD_DIM), :]
            new.append(s * jnp.exp(glast) + (b - r[:HEAD_DIM]))
            rows = pl.ds(pl.multiple_of(ch * GDN_CHUNK, GDN_CHUNK), GDN_CHUNK)
            o_s[d, rows, :] = o_s[d, rows, :] + r[HEAD_DIM:]
        return tuple(new)

    @pl.when(is_ctx)
    def _context_unit():
        def body(i, states):
            chains = [(d, sq * chunks_per_seq + (i if d == 0 else chunks_per_seq - 1 - i))
                      for sq in range(SEQ_PER_UNIT) for d in range(2)]
            return advance(chains, states)
        zero = jnp.zeros((HEAD_DIM, HEAD_DIM), F32)
        final = lax.fori_loop(0, chunks_per_seq, body, (zero,) * (2 * SEQ_PER_UNIT))
        for sq in range(SEQ_PER_UNIT):
            for d in range(2):
                s_ref[sq, d] = final[2 * sq + d]

    @pl.when(jnp.logical_not(is_ctx))
    def _latent_unit():
        def body(i, states):
            return advance([(0, i), (1, n_chunks - 1 - i)], states)
        final = lax.fori_loop(0, n_chunks, body, (s0_ref[0], s0_ref[1]))
        for sq in range(SEQ_PER_UNIT):
            for d in range(2):
                s_ref[sq, d] = final[d]

    o_ref[...] = _head_rmsnorm_gate(o_s[0] + o_s[1], g_ref[...], og_ref[...]).astype(o_ref.dtype)


def _gdn_call(z, col, rowt, conv_w, onorm_g, state, layer):
    def zcol(off):
        return pl.BlockSpec((UNIT, HEAD_DIM), lambda u, h: (u, off // HEAD_DIM + h))
    def wcol(part):
        return pl.BlockSpec((CONV_K * CONV_K, HEAD_DIM), lambda u, h: (0, part * HEADS + h))
    scr = lambda *shape: pltpu.VMEM(shape, F32)
    return pl.pallas_call(
        _gdn_kernel,
        grid=(N_UNITS, HEADS),
        in_specs=[zcol(OFF_GQ), zcol(OFF_GK), zcol(OFF_GV), zcol(OFF_GG),
                  pl.BlockSpec((UNIT, LANES), lambda u, h: (u, 0)),
                  pl.BlockSpec((LANES, UNIT), lambda u, h: (0, u)),
                  wcol(0), wcol(1), wcol(2),
                  pl.BlockSpec((1, HEAD_DIM), lambda u, h: (0, 0)),
                  pl.BlockSpec((None, None, 2, None, HEAD_DIM, HEAD_DIM),
                               lambda u, h: (jnp.maximum(u - N_CTX_UNITS, 0), layer, 0, h, 0, 0))],
        out_specs=[pl.BlockSpec((UNIT, HEAD_DIM), lambda u, h: (u, h)),
                   pl.BlockSpec((None, SEQ_PER_UNIT, 2, None, HEAD_DIM, HEAD_DIM),
                                lambda u, h: (u, 0, 0, h, 0, 0))],
        out_shape=[jax.ShapeDtypeStruct((N_TOK, HG_F), BF16),
                   jax.ShapeDtypeStruct((N_UNITS, SEQ_PER_UNIT, 2, HEADS, HEAD_DIM, HEAD_DIM), F32)],
        scratch_shapes=[scr(UNIT + 2 * CONV_PAD, HEAD_DIM),
                        scr(UNIT, HEAD_DIM), scr(UNIT, HEAD_DIM), scr(UNIT, HEAD_DIM),
                        scr(2, UNIT, HEAD_DIM),
                        scr(2, (UNIT // GDN_CHUNK) * HEAD_DIM, HEAD_DIM),
                        pltpu.VMEM((2, (UNIT // GDN_CHUNK) * GDN_AQ, HEAD_DIM), BF16)],
        compiler_params=_cparams(("arbitrary", "arbitrary")),
        name="gdn_scan",
    )(z, z, z, z, col, rowt, conv_w, conv_w, conv_w, onorm_g.reshape(1, HEAD_DIM), state)


def kernel(x_prompt, x_sample, c, state_hgrn, state_gdn, c_ctx, norm1_g, norm2_g, w_mod, b_mod, w_in, hg_lb, hg_onorm_g, cm_vnorm_g, cm_ws, cm_bs, gdn_conv, gdn_A_log, gdn_dt_bias, gdn_onorm_g, w_br_hg, w_br_cm, w_br_gdn, w_out, w_ff1, w_ff2, final_g):
    x = jnp.concatenate([x_prompt.reshape(N_CTX_TOK, D_MODEL), x_sample.reshape(N_LAT_TOK, D_MODEL)], axis=0)
    cvec = jnp.concatenate([c_ctx[None, :], c, jnp.zeros((MOD_ROWS - 1 - DEC_BATCH, D_MODEL), F32)], axis=0)
    mod3 = _mod_call(cvec, w_mod, b_mod).reshape(DEPTH * MOD_ROWS, 1, 6 * D_MODEL)

    lb_all = jnp.cumsum(jax.nn.softmax(hg_lb.astype(F32), axis=0), axis=0)
    lb_all = lb_all - lb_all[:1]

    w_in_t = jnp.swapaxes(w_in, 1, 2)
    w_ff2_bf = w_ff2.astype(BF16)

    new_hg, new_gdn = [], []
    h = _normmod_call(x, norm1_g[0], mod3, 0, 0, 1)
    for l in range(DEPTH):
        z = _mm_call(h, w_in_t, l, col0=0, n=N_MAIN, act=None, out_dtype=F32, tm=1024, tn=1024,
                     w_transposed=True, name="in_proj")
        ab = _mm_call(h, w_in_t, l, col0=OFF_AB, n=LANES, act=None, out_dtype=F32, tm=1024, tn=LANES,
                      w_transposed=True, name="in_proj_ab")
        gates = _mm_call(h, w_in_t, l, col0=OFF_GATES, n=3 * D_MODEL, act="sigmoid", out_dtype=F32,
                         tm=1024, tn=1024, w_transposed=True, name="in_proj_gates")

        o_a, s_hg = _hgrn_call(z, lb_all[l], hg_onorm_g[l], state_hgrn, l)
        o_b = _gmlp_call(z, cm_vnorm_g[l], cm_ws[l], cm_bs[l].T)
        col, rowt = _gdn_gates_call(ab, gdn_A_log[l], gdn_dt_bias[l])
        o_c, s_gdn = _gdn_call(z, col, rowt, gdn_conv[l].reshape(CONV_K * CONV_K, 3 * HG_F),
                               gdn_onorm_g[l], state_gdn, l)
        new_hg.append(s_hg[:N_CTX_UNITS].reshape(BATCH, 2, HEADS, HEAD_DIM, HEAD_DIM))
        new_gdn.append(s_gdn[:N_CTX_UNITS].reshape(BATCH, 2, HEADS, HEAD_DIM, HEAD_DIM))

        merged = _merge_call(o_a, o_b, o_c, w_br_hg, w_br_cm, w_br_gdn, gates, l)
        x, h2 = _outproj_norm_call(merged, w_out, x, norm2_g[l], mod3, l)
        up = _mm_call(h2, w_ff1, l, col0=0, n=D_FF, act="relu2", out_dtype=BF16, tm=1024, tn=1024, name="ffn_up")
        if l + 1 < DEPTH:
            x, h = _ffn_down_norm_call(up, w_ff2_bf, x, mod3, l, norm1_g[l + 1])
        else:
            x = _mm_resid_call(up, w_ff2_bf, x, mod3, l, 5, tm=512, tn=512, name="ffn_down")

    y_prompt = _final_norm_call(x, final_g, 0, N_CTX_TOK).reshape(BATCH, SEQ, D_MODEL)
    y_sample = _final_norm_call(x, final_g, N_CTX_TOK, N_LAT_TOK).reshape(DEC_BATCH, DEC_SEQ, D_MODEL)
    return (y_prompt, y_sample, jnp.stack(new_hg, axis=1), jnp.stack(new_gdn, axis=1))
```

```python
import functools

import jax
import jax.numpy as jnp
from jax import lax
from jax.experimental import pallas as pl
from jax.experimental.pallas import tpu as pltpu

F32 = jnp.float32
BF16 = jnp.bfloat16

D_MODEL = 2048
BATCH = 16
SEQ = 256
DEPTH = 2
DEC_BATCH = 4
DEC_SEQ = 1024
GRID_W = 64
EPS = 1e-6
D_FF = 4 * D_MODEL
HEADS = 8
HEAD_DIM = 128
HG_F = HEADS * HEAD_DIM
CM_GROUPS = 8
CM_W = CM_GROUPS * HEAD_DIM
CM_CHUNK = 128
GDN_CHUNK = 64
CONV_K = 3

OFF_HQ, OFF_HI, OFF_HG, OFF_HFF, OFF_HFB = 0, 1024, 2048, 3072, 4096
OFF_CU, OFF_CV = 5120, 6144
OFF_GQ, OFF_GK, OFF_GV, OFF_GG = 7168, 8192, 9216, 10240
OFF_AB = 11264
OFF_GATES = 11296
IN_DIM = 17440
N_MAIN = OFF_AB

N_CTX_TOK = BATCH * SEQ
N_LAT_TOK = DEC_BATCH * DEC_SEQ
N_TOK = N_CTX_TOK + N_LAT_TOK
UNIT = DEC_SEQ
N_CTX_UNITS = N_CTX_TOK // UNIT
N_UNITS = N_TOK // UNIT
SEQ_PER_UNIT = UNIT // SEQ

LANES = 128
MOD_ROWS = 8
HG_BLOCK = 256
HG_SUB = 32
HG_INST = 2 * (UNIT // HG_BLOCK)
VMEM_LIMIT = 56 * 1024 * 1024

_NT = (((1,), (1,)), ((), ()))
_TN = (((0,), (0,)), ((), ()))
_NN = (((1,), (0,)), ((), ()))


def _dg(a, b, dims):
    return lax.dot_general(a, b, dims, preferred_element_type=F32)


def _split2(x):
    hi = x.astype(BF16)
    lo = (x - hi.astype(F32)).astype(BF16)
    return hi, lo


def _dot1(a, b, dims=_NN):
    return _dg(a.astype(BF16), b.astype(BF16), dims)


def _dot3(a, b, dims=_NN):
    ah, al = _split2(a)
    bh, bl = _split2(b)
    return _dg(ah, bh, dims) + (_dg(ah, bl, dims) + _dg(al, bh, dims))


_dot_inv = _dot1


def _dot01(m, x):
    hi = x.astype(BF16)
    r = x - hi.astype(F32)
    mid = r.astype(BF16)
    lo = (r - mid.astype(F32)).astype(BF16)
    return _dg(m, hi, _NN) + (_dg(m, mid, _NN) + _dg(m, lo, _NN))


def _dot01_2(m, x):
    hi, lo = _split2(x)
    return _dg(m, hi, _NN) + _dg(m, lo, _NN)


def _sigmoid(x):
    return 1.0 / (1.0 + jnp.exp(-x))


def _silu(x):
    return x * _sigmoid(x)


def _gelu(x):
    return 0.5 * x * (1.0 + lax.erf(x * (2.0 ** -0.5)))


def _softplus(x):
    return jnp.maximum(x, 0.0) + jnp.log1p(jnp.exp(-jnp.abs(x)))


def _mod_row_of_tile(i, tm):
    return jnp.maximum(0, (i * tm - N_CTX_TOK) // DEC_SEQ + 1)


def _cparams(sem):
    return pltpu.CompilerParams(dimension_semantics=sem, vmem_limit_bytes=VMEM_LIMIT)


def _mod_kernel(c_ref, w_ref, b_ref, o_ref):
    s = _silu(c_ref[...])
    hi, lo = _split2(s)
    w = w_ref[...].astype(BF16)
    o_ref[...] = _dg(hi, w, _NN) + _dg(lo, w, _NN) + b_ref[...]


def _mod_call(cvec, w_mod, b_mod):
    tn = 1024
    n = 6 * D_MODEL
    return pl.pallas_call(
        _mod_kernel,
        grid=(DEPTH, n // tn),
        in_specs=[pl.BlockSpec((MOD_ROWS, D_MODEL), lambda l, j: (0, 0)),
                  pl.BlockSpec((None, D_MODEL, tn), lambda l, j: (l, 0, j)),
                  pl.BlockSpec((None, 1, tn), lambda l, j: (l, 0, j))],
        out_specs=pl.BlockSpec((None, MOD_ROWS, tn), lambda l, j: (l, 0, j)),
        out_shape=jax.ShapeDtypeStruct((DEPTH, MOD_ROWS, n), F32),
        compiler_params=_cparams(("arbitrary", "arbitrary")),
        name="modulation",
    )(cvec, w_mod, b_mod.reshape(DEPTH, 1, n))


def _normmod_kernel(x_ref, g_ref, sh_ref, sc_ref, o_ref):
    x = x_ref[...]
    y = x * lax.rsqrt(jnp.mean(x * x, axis=-1, keepdims=True) + EPS) * g_ref[...]
    o_ref[...] = (y * (1.0 + sc_ref[...]) + sh_ref[...]).astype(o_ref.dtype)


def _normmod_call(x, g, mod3, layer, k_shift, k_scale):
    tm = 512
    def mod_spec(k):
        return pl.BlockSpec((None, 1, D_MODEL),
                            lambda i: (layer * MOD_ROWS + _mod_row_of_tile(i, tm), 0, k))
    return pl.pallas_call(
        _normmod_kernel,
        grid=(N_TOK // tm,),
        in_specs=[pl.BlockSpec((tm, D_MODEL), lambda i: (i, 0)),
                  pl.BlockSpec((1, D_MODEL), lambda i: (0, 0)),
                  mod_spec(k_shift), mod_spec(k_scale)],
        out_specs=pl.BlockSpec((tm, D_MODEL), lambda i: (i, 0)),
        out_shape=jax.ShapeDtypeStruct((N_TOK, D_MODEL), BF16),
        compiler_params=_cparams(("arbitrary",)),
        name="norm_mod",
    )(x, g.reshape(1, D_MODEL), mod3, mod3)


def _final_norm_kernel(x_ref, g_ref, o_ref):
    x = x_ref[...]
    o_ref[...] = x * lax.rsqrt(jnp.mean(x * x, axis=-1, keepdims=True) + EPS) * g_ref[...]


def _final_norm_call(x, g, row0, n_rows):
    tm = 512
    return pl.pallas_call(
        _final_norm_kernel,
        grid=(n_rows // tm,),
        in_specs=[pl.BlockSpec((tm, D_MODEL), lambda i: (row0 // tm + i, 0)),
                  pl.BlockSpec((1, D_MODEL), lambda i: (0, 0))],
        out_specs=pl.BlockSpec((tm, D_MODEL), lambda i: (i, 0)),
        out_shape=jax.ShapeDtypeStruct((n_rows, D_MODEL), F32),
        compiler_params=_cparams(("arbitrary",)),
        name="final_norm",
    )(x, g.reshape(1, D_MODEL))


def _mm_kernel(x_ref, w_ref, o_ref, wbf_ref, *, act, w_transposed):
    @pl.when(pl.program_id(1) == 0)
    def _cast_weights():
        wbf_ref[...] = w_ref[...].reshape(wbf_ref.shape).astype(BF16)

    acc = lax.dot_general(x_ref[...], wbf_ref[...], _NT if w_transposed else _NN, preferred_element_type=F32)
    if act == "sigmoid":
        acc = _sigmoid(acc)
    elif act == "relu2":
        acc = jnp.square(jnp.maximum(acc, 0.0))
    if len(o_ref.shape) == 3:
        for c in range(o_ref.shape[0]):
            o_ref[c] = acc[:, c * LANES:(c + 1) * LANES].astype(o_ref.dtype)
    else:
        o_ref[...] = acc.astype(o_ref.dtype)


def _mm_call(x, w_all, layer, *, col0, n, act, out_dtype, tm, tn, name, w_transposed=False,
             lane_tile_major=False):
    m, k = x.shape
    assert n % tn == 0 and m % tm == 0
    if w_transposed:
        assert col0 % 8 == 0
        w_spec = pl.BlockSpec((pl.Element(1), pl.Element(tn), pl.Element(k)),
                              lambda j, i: (layer, pl.multiple_of(col0 + j * tn, 8), 0))
        w_scratch = pltpu.VMEM((tn, k), BF16)
    else:
        assert col0 % tn == 0
        w_spec = pl.BlockSpec((None, k, tn), lambda j, i: (layer, 0, col0 // tn + j))
        w_scratch = pltpu.VMEM((k, tn), BF16)
    if lane_tile_major:
        out_spec = pl.BlockSpec((tn // LANES, tm, LANES), lambda j, i: (j, i, 0))
        out_shape = jax.ShapeDtypeStruct((n // LANES, m, LANES), out_dtype)
    else:
        out_spec = pl.BlockSpec((tm, tn), lambda j, i: (i, j))
        out_shape = jax.ShapeDtypeStruct((m, n), out_dtype)
    return pl.pallas_call(
        functools.partial(_mm_kernel, act=act, w_transposed=w_transposed),
        grid=(n // tn, m // tm),
        in_specs=[pl.BlockSpec((tm, k), lambda j, i: (i, 0)), w_spec],
        out_specs=out_spec,
        out_shape=out_shape,
        scratch_shapes=[w_scratch],
        compiler_params=_cparams(("arbitrary", "arbitrary")),
        name=name,
    )(x, w_all)


def _mm_bf16_kernel(x_ref, w_ref, r_ref, g_ref, o_ref):
    acc = jnp.dot(x_ref[...], w_ref[...], preferred_element_type=F32)
    o_ref[...] = r_ref[...] + g_ref[...] * acc


def _mm_resid_call(x, w, resid, mod3, layer, k_gate, *, tm, tn, name):
    m, k = x.shape
    n = w.shape[2]
    per_tile = D_MODEL // tn
    return pl.pallas_call(
        _mm_bf16_kernel,
        grid=(m // tm, n // tn),
        in_specs=[pl.BlockSpec((tm, k), lambda i, j: (i, 0)),
                  pl.BlockSpec((None, k, tn), lambda i, j: (layer, 0, j)),
                  pl.BlockSpec((tm, tn), lambda i, j: (i, j)),
                  pl.BlockSpec((None, 1, tn),
                               lambda i, j: (layer * MOD_ROWS + _mod_row_of_tile(i, tm), 0,
                                             k_gate * per_tile + j))],
        out_specs=pl.BlockSpec((tm, tn), lambda i, j: (i, j)),
        out_shape=jax.ShapeDtypeStruct((m, n), F32),
        compiler_params=_cparams(("arbitrary", "arbitrary")),
        name=name,
    )(x, w, resid, mod3)


def _adaln(x, g, shift, scale):
    y = x * lax.rsqrt(jnp.mean(x * x, axis=-1, keepdims=True) + EPS) * g
    return y * (1.0 + scale) + shift


def _outproj_norm_kernel(m_ref, w_ref, x_ref, gate_ref, g_ref, sh_ref, sc_ref, xo_ref, h_ref, wbf_ref):
    @pl.when(pl.program_id(0) == 0)
    def _cast_weights():
        wbf_ref[...] = w_ref[...].astype(BF16)

    acc = jnp.dot(m_ref[...], wbf_ref[...], preferred_element_type=F32)
    xn = x_ref[...] + gate_ref[...] * acc
    xo_ref[...] = xn
    h_ref[...] = _adaln(xn, g_ref[...], sh_ref[...], sc_ref[...]).astype(h_ref.dtype)


def _outproj_norm_call(merged, w_out, x, norm_g, mod3, layer):
    tm = 512
    def mod_spec(k):
        return pl.BlockSpec((None, 1, D_MODEL),
                            lambda i: (layer * MOD_ROWS + _mod_row_of_tile(i, tm), 0, k))
    row = pl.BlockSpec((tm, D_MODEL), lambda i: (i, 0))
    return pl.pallas_call(
        _outproj_norm_kernel,
        grid=(N_TOK // tm,),
        in_specs=[row,
                  pl.BlockSpec((None, D_MODEL, D_MODEL), lambda i: (layer, 0, 0), pipeline_mode=pl.Buffered(1)),
                  row, mod_spec(2),
                  pl.BlockSpec((1, D_MODEL), lambda i: (0, 0)), mod_spec(3), mod_spec(4)],
        out_specs=[row, row],
        out_shape=[jax.ShapeDtypeStruct((N_TOK, D_MODEL), F32),
                   jax.ShapeDtypeStruct((N_TOK, D_MODEL), BF16)],
        scratch_shapes=[pltpu.VMEM((D_MODEL, D_MODEL), BF16)],
        compiler_params=_cparams(("arbitrary",)),
        name="out_proj_norm",
    )(merged, w_out, x, mod3, norm_g.reshape(1, D_MODEL), mod3, mod3)


def _ffn_down_norm_kernel(x_ref, w_ref, r_ref, gate_ref, g_ref, sh_ref, sc_ref, o_ref, h_ref, row_ref):
    j = pl.program_id(1)
    tn = o_ref.shape[1]
    acc = jnp.dot(x_ref[...], w_ref[...], preferred_element_type=F32)
    xn = r_ref[...] + gate_ref[...] * acc
    o_ref[...] = xn
    row_ref[:, pl.ds(pl.multiple_of(j * tn, tn), tn)] = xn

    @pl.when(j == pl.num_programs(1) - 1)
    def _next_layer_norm():
        h_ref[...] = _adaln(row_ref[...], g_ref[...], sh_ref[...], sc_ref[...]).astype(h_ref.dtype)


def _ffn_down_norm_call(up, w_bf, resid, mod3, layer, next_norm_g):
    tm, tn = 512, 512
    m, k = up.shape
    per_tile = D_MODEL // tn
    def mod_row(i):
        return _mod_row_of_tile(i, tm)
    def next_mod(kk):
        return pl.BlockSpec((None, 1, D_MODEL), lambda i, j: ((layer + 1) * MOD_ROWS + mod_row(i), 0, kk))
    return pl.pallas_call(
        _ffn_down_norm_kernel,
        grid=(m // tm, D_MODEL // tn),
        in_specs=[pl.BlockSpec((tm, k), lambda i, j: (i, 0)),
                  pl.BlockSpec((None, k, tn), lambda i, j: (layer, 0, j)),
                  pl.BlockSpec((tm, tn), lambda i, j: (i, j)),
                  pl.BlockSpec((None, 1, tn), lambda i, j: (layer * MOD_ROWS + mod_row(i), 0, 5 * per_tile + j)),
                  pl.BlockSpec((1, D_MODEL), lambda i, j: (0, 0)), next_mod(0), next_mod(1)],
        out_specs=[pl.BlockSpec((tm, tn), lambda i, j: (i, j)),
                   pl.BlockSpec((tm, D_MODEL), lambda i, j: (i, 0))],
        out_shape=[jax.ShapeDtypeStruct((m, D_MODEL), F32),
                   jax.ShapeDtypeStruct((m, D_MODEL), BF16)],
        scratch_shapes=[pltpu.VMEM((tm, D_MODEL), F32)],
        compiler_params=_cparams(("arbitrary", "arbitrary")),
        name="ffn_down_norm",
    )(up, w_bf, resid, mod3, next_norm_g.reshape(1, D_MODEL), mod3, mod3)


def _merge_kernel(oa_ref, ob_ref, oc_ref, wa_ref, wb_ref, wc_ref, ga_ref, gb_ref, gc_ref, o_ref, wbf_ref):
    @pl.when(pl.program_id(1) == 0)
    def _cast_weights():
        wbf_ref[0] = wa_ref[...].astype(BF16)
        wbf_ref[1] = wb_ref[...].astype(BF16)
        wbf_ref[2] = wc_ref[...].astype(BF16)

    def rows_of(ref):
        return jnp.concatenate([ref[t] for t in range(ref.shape[0])], axis=1)

    a = jnp.dot(rows_of(oa_ref), wbf_ref[0], preferred_element_type=F32)
    b = jnp.dot(rows_of(ob_ref), wbf_ref[1], preferred_element_type=F32)
    c = jnp.dot(rows_of(oc_ref), wbf_ref[2], preferred_element_type=F32)
    o_ref[...] = (ga_ref[...] * a + gb_ref[...] * b + gc_ref[...] * c).astype(o_ref.dtype)


def _merge_call(o_a, o_b, o_c, w_a, w_b, w_c, gates, layer):
    tm, tn = 1024, 512
    nj = D_MODEL // tn
    br = pl.BlockSpec((HEADS, tm, HEAD_DIM), lambda j, i: (0, i, 0))
    wt = pl.BlockSpec((None, HG_F, tn), lambda j, i: (layer, 0, j))
    def gate_spec(k):
        return pl.BlockSpec((tm, tn), lambda j, i: (i, k * nj + j))
    return pl.pallas_call(
        _merge_kernel,
        grid=(nj, N_TOK // tm),
        in_specs=[br, br, br, wt, wt, wt, gate_spec(0), gate_spec(1), gate_spec(2)],
        out_specs=pl.BlockSpec((tm, tn), lambda j, i: (i, j)),
        out_shape=jax.ShapeDtypeStruct((N_TOK, D_MODEL), BF16),
        scratch_shapes=[pltpu.VMEM((3, HG_F, tn), BF16)],
        compiler_params=_cparams(("arbitrary", "arbitrary")),
        name="branch_merge",
    )(o_a, o_b, o_c, w_a, w_b, w_c, gates, gates, gates)


def _head_rmsnorm_gate(o, g, og):
    y = o * lax.rsqrt(jnp.mean(o * o, axis=-1, keepdims=True) + EPS) * g
    return y * _silu(og)


def _hgrn_kernel(q_ref, i_ref, og_ref, ff_ref, fb_ref, lb_ref, g_ref, s0_ref, o_ref, s_ref,
                 acc_ref, b_ref, stc_ref, km_ref):
    is_ctx = pl.program_id(0) < N_CTX_UNITS
    nblk = UNIT // HG_BLOCK
    nsub = HG_BLOCK // HG_SUB
    r = lax.broadcasted_iota(jnp.int32, (HG_BLOCK, HG_BLOCK), 0)
    c = lax.broadcasted_iota(jnp.int32, (HG_BLOCK, HG_BLOCK), 1)
    same = (r // HG_SUB) == (c // HG_SUB)
    row_sub = lax.broadcasted_iota(jnp.int32, (HG_BLOCK, HEAD_DIM), 0) // HG_SUB
    tri = (jnp.logical_and(same, c <= r), jnp.logical_and(same, c >= r))
    tri_bf = [jnp.where(t, 1.0, 0.0).astype(BF16) for t in tri]
    z_refs = (ff_ref, fb_ref)
    inst = [(blk, d) for blk in range(nblk) for d in range(2)]

    def expand(x):
        return jnp.broadcast_to(x[:, None, :], (nsub, HG_SUB, HEAD_DIM)).reshape(HG_BLOCK, HEAD_DIM)

    def blk_rows(blk):
        return slice(blk * HG_BLOCK, (blk + 1) * HG_BLOCK)

    qs = [_silu(q_ref[blk_rows(blk), :]) for blk in range(nblk)]
    vs = [i_ref[blk_rows(blk), :] for blk in range(nblk)]
    ks, lfs = [], []
    for blk, d in inst:
        lb = lb_ref[d:d + 1, :]
        f = lb + (1.0 - lb) * _sigmoid(z_refs[d][blk_rows(blk), :])
        lfs.append(jnp.log(f))
        ks.append(1.0 - f)
    bs = [_dot01_2(tri_bf[d], lf) for (blk, d), lf in zip(inst, lfs)]
    tots, qts, kts, qds = [], [], [], []
    for n, (blk, d) in enumerate(inst):
        b = bs[n]
        b_ref[n] = b
        tot = b_ref[n, pl.ds((HG_SUB - 1) if d == 0 else 0, nsub, stride=HG_SUB), :]
        mid_f = expand(b_ref[n, pl.ds(HG_SUB // 2, nsub, stride=HG_SUB), :])
        tots.append(tot)
        qts.append(qs[blk] * jnp.exp(b - mid_f))
        kts.append(ks[n] * jnp.exp(mid_f - b))
        qds.append((qs[blk] * jnp.exp(b)).astype(BF16))
        kd = ks[n] * jnp.exp(expand(tot) - b)
        for s in range(nsub):
            km_ref[n, :, s * HEAD_DIM:(s + 1) * HEAD_DIM] = jnp.where(row_sub == s, kd, 0.0).astype(BF16)
    scs = [jnp.where(tri[d], _dot1(qt, kt, _NT), 0.0) for (blk, d), qt, kt in zip(inst, qts, kts)]
    uts = [_dg(vs[blk].astype(BF16), km_ref[n], _TN) for n, (blk, d) in enumerate(inst)]
    outs = [_dot1(sc, vs[blk]) for (blk, d), sc in zip(inst, scs)]
    for d in range(2):
        st = s0_ref[d].T
        for blk in (range(nblk) if d == 0 else range(nblk - 1, -1, -1)):
            n = inst.index((blk, d))
            st = jnp.where(is_ctx, 0.0, st)
            for s in (range(nsub) if d == 0 else range(nsub - 1, -1, -1)):
                stc_ref[n, s] = st.astype(BF16)
                st = st * jnp.exp(tots[n][s:s + 1, :]) + uts[n][:, s * HEAD_DIM:(s + 1) * HEAD_DIM]
            s_ref[blk, d] = st.T
    for n, (blk, d) in enumerate(inst):
        o_int = [_dg(qds[n][s * HG_SUB:(s + 1) * HG_SUB], stc_ref[n, s], _NT) for s in range(nsub)]
        acc_ref[d, blk_rows(blk), :] = outs[n] + jnp.concatenate(o_int, axis=0)
    o_ref[...] = _head_rmsnorm_gate(acc_ref[0] + acc_ref[1], g_ref[...], og_ref[...]).astype(o_ref.dtype)


def _hgrn_call(z, lb, onorm_g, state, layer):
    def col(off):
        return pl.BlockSpec((None, UNIT, HEAD_DIM), lambda u, h: (off // HEAD_DIM + h, u, 0))
    return pl.pallas_call(
        _hgrn_kernel,
        grid=(N_UNITS, HEADS),
        in_specs=[col(OFF_HQ), col(OFF_HI), col(OFF_HG), col(OFF_HFF), col(OFF_HFB),
                  pl.BlockSpec((2, HEAD_DIM), lambda u, h: (0, h)),
                  pl.BlockSpec((1, HEAD_DIM), lambda u, h: (0, 0)),
                  pl.BlockSpec((None, None, 2, None, HEAD_DIM, HEAD_DIM),
                               lambda u, h: (jnp.maximum(u - N_CTX_UNITS, 0), layer, 0, h, 0, 0))],
        out_specs=[pl.BlockSpec((None, UNIT, HEAD_DIM), lambda u, h: (h, u, 0)),
                   pl.BlockSpec((None, SEQ_PER_UNIT, 2, None, HEAD_DIM, HEAD_DIM),
                                lambda u, h: (u, 0, 0, h, 0, 0))],
        out_shape=[jax.ShapeDtypeStruct((HEADS, N_TOK, HEAD_DIM), BF16),
                   jax.ShapeDtypeStruct((N_UNITS, SEQ_PER_UNIT, 2, HEADS, HEAD_DIM, HEAD_DIM), F32)],
        scratch_shapes=[pltpu.VMEM((2, UNIT, HEAD_DIM), F32),
                        pltpu.VMEM((HG_INST, HG_BLOCK, HEAD_DIM), F32),
                        pltpu.VMEM((HG_INST, HG_BLOCK // HG_SUB, HEAD_DIM, HEAD_DIM), BF16),
                        pltpu.VMEM((HG_INST, HG_BLOCK, (HG_BLOCK // HG_SUB) * HEAD_DIM), BF16)],
        compiler_params=_cparams(("arbitrary", "arbitrary")),
        name="hgrn2_scan",
    )(z, z, z, z, z, lb, onorm_g.reshape(1, HEAD_DIM), state)


def _gmlp_kernel(u_ref, v_ref, vn_ref, ws_ref, bs_ref, o_ref):
    tm = u_ref.shape[1]
    for g in range(CM_GROUPS):
        cols = slice(g * HEAD_DIM, (g + 1) * HEAD_DIM)
        vg = _gelu(v_ref[g])
        vg = vg * lax.rsqrt(jnp.mean(vg * vg, axis=-1, keepdims=True) + EPS) * vn_ref[:, cols]
        ug = _gelu(u_ref[g])
        w = ws_ref[g]
        bias = bs_ref[:, g:g + 1]
        for ch in range(tm // CM_CHUNK):
            rows = slice(ch * CM_CHUNK, (ch + 1) * CM_CHUNK)
            s = _dot3(w, vg[rows]) + bias
            o_ref[g, rows, :] = (ug[rows] * s).astype(o_ref.dtype)


def _gmlp_call(z, vnorm_g, ws, bs_t):
    tm = 512
    return pl.pallas_call(
        _gmlp_kernel,
        grid=(N_TOK // tm,),
        in_specs=[pl.BlockSpec((CM_GROUPS, tm, HEAD_DIM), lambda i: (OFF_CU // CM_W, i, 0)),
                  pl.BlockSpec((CM_GROUPS, tm, HEAD_DIM), lambda i: (OFF_CV // CM_W, i, 0)),
                  pl.BlockSpec((1, CM_W), lambda i: (0, 0)),
                  pl.BlockSpec((CM_GROUPS, CM_CHUNK, CM_CHUNK), lambda i: (0, 0, 0)),
                  pl.BlockSpec((CM_CHUNK, CM_GROUPS), lambda i: (0, 0))],
        out_specs=pl.BlockSpec((CM_GROUPS, tm, HEAD_DIM), lambda i: (0, i, 0)),
        out_shape=jax.ShapeDtypeStruct((CM_GROUPS, N_TOK, HEAD_DIM), BF16),
        compiler_params=_cparams(("arbitrary",)),
        name="chunk_gmlp",
    )(z, z, vnorm_g.reshape(1, CM_W), ws, bs_t)


GP_BLOCK = 256


def _gdn_gates_kernel(ab_ref, alog_ref, dt_ref, col_ref, rowt_ref):
    ab = ab_ref[...]
    lane = lax.broadcasted_iota(jnp.int32, ab.shape, 1)
    g = jnp.where(lane < 2 * HEADS, -jnp.exp(alog_ref[...]) * _softplus(ab + dt_ref[...]), 0.0)
    r = lax.broadcasted_iota(jnp.int32, (GP_BLOCK, GP_BLOCK), 0)
    c = lax.broadcasted_iota(jnp.int32, (GP_BLOCK, GP_BLOCK), 1)
    same = (r // GDN_CHUNK) == (c // GDN_CHUNK)
    tri_f = jnp.where(jnp.logical_and(same, c <= r), 1.0, 0.0).astype(BF16)
    tri_b = jnp.where(jnp.logical_and(same, c >= r), 1.0, 0.0).astype(BF16)
    cf = _dot01(tri_f, g)
    cb = _dot01(tri_b, g)
    col = jnp.where(lane < HEADS, cf, jnp.where(lane < 2 * HEADS, cb, _sigmoid(ab)))
    col_ref[...] = col
    rowt_ref[...] = col.T


def _gdn_gates_call(ab, a_log, dt_bias):
    pad = lambda t: jnp.pad(t.reshape(1, 2 * HEADS), ((0, 0), (0, LANES - 2 * HEADS)))
    return pl.pallas_call(
        _gdn_gates_kernel,
        grid=(N_TOK // GP_BLOCK,),
        in_specs=[pl.BlockSpec((GP_BLOCK, LANES), lambda i: (i, 0)),
                  pl.BlockSpec((1, LANES), lambda i: (0, 0)),
                  pl.BlockSpec((1, LANES), lambda i: (0, 0))],
        out_specs=[pl.BlockSpec((GP_BLOCK, LANES), lambda i: (i, 0)),
                   pl.BlockSpec((LANES, GP_BLOCK), lambda i: (0, i))],
        out_shape=[jax.ShapeDtypeStruct((N_TOK, LANES), F32),
                   jax.ShapeDtypeStruct((LANES, N_TOK), F32)],
        compiler_params=_cparams(("arbitrary",)),
        name="gdn_gates",
    )(ab, pad(a_log), pad(dt_bias))


CONV_PAD = 72
GDN_AQ = HEAD_DIM + GDN_CHUNK
GDN_STEP = 16 * GDN_CHUNK


def _gdn_kernel(qr_ref, kr_ref, vr_ref, og_ref, col_ref, rowt_ref, cwq_ref, cwk_ref, cwv_ref,
                g_ref, s0_ref, o_ref, s_ref,
                xp_ref, q_s, k_s, v_s, o_s, b_s, aq_s):
    is_ctx = pl.program_id(0) < N_CTX_UNITS
    head = pl.program_id(1)
    n_chunks = UNIT // GDN_CHUNK
    chunks_per_seq = SEQ // GDN_CHUNK

    t = lax.broadcasted_iota(jnp.int32, (UNIT, 1), 0)
    period = jnp.where(is_ctx, SEQ, GRID_W)
    pos = jnp.bitwise_and(t, period - 1)
    ok_left = pos != 0
    ok_right = pos != period - 1
    zeros_pad = jnp.zeros((CONV_PAD, HEAD_DIM), F32)
    xp_ref[0:CONV_PAD, :] = zeros_pad
    xp_ref[CONV_PAD + UNIT:CONV_PAD + UNIT + CONV_PAD, :] = zeros_pad

    def conv_silu(x_ref, w_ref):
        xp_ref[CONV_PAD:CONV_PAD + UNIT, :] = x_ref[...]
        acc = jnp.zeros((UNIT, HEAD_DIM), F32)
        for i in range(CONV_K):
            for j in range(CONV_K):
                w = w_ref[CONV_K * i + j:CONV_K * i + j + 1, :]
                if i != CONV_K // 2:
                    w = jnp.where(is_ctx, 0.0, w)
                start = CONV_PAD + (i - 1) * GRID_W + (j - 1)
                xs = xp_ref[start:start + UNIT, :]
                if j == 0:
                    xs = jnp.where(ok_left, xs, 0.0)
                elif j == CONV_K - 1:
                    xs = jnp.where(ok_right, xs, 0.0)
                acc = acc + xs * w
        return _silu(acc)

    def l2norm(x):
        return x * lax.rsqrt(jnp.sum(x * x, axis=-1, keepdims=True) + EPS)

    q_s[...] = l2norm(conv_silu(qr_ref, cwq_ref)) * (HEAD_DIM ** -0.5)
    k_s[...] = l2norm(conv_silu(kr_ref, cwk_ref))
    v_s[...] = conv_silu(vr_ref, cwv_ref)

    rr = lax.broadcasted_iota(jnp.int32, (GDN_CHUNK, GDN_CHUNK), 0)
    cc = lax.broadcasted_iota(jnp.int32, (GDN_CHUNK, GDN_CHUNK), 1)
    eye = jnp.where(rr == cc, 1.0, 0.0)
    same_blk = [(rr // b) == (cc // b) for b in (8, 16, 32, 64)]
    lane = lax.broadcasted_iota(jnp.int32, (GDN_STEP, LANES), 1)
    sub8 = lax.broadcasted_iota(jnp.int32, (HEADS, GDN_STEP), 0)

    def pick(x, j):
        return jnp.sum(jnp.where(lane[:x.shape[0]] == j, x, 0.0), axis=-1, keepdims=True)

    incl = (cc <= rr, cc >= rr)
    strict = (cc < rr, cc > rr)
    off_masks = [jnp.logical_and(same_blk[lvl], jnp.logical_not(same_blk[lvl - 1]))
                 for lvl in range(1, len(same_blk))]

    def phase1(p, carry):
        rows = pl.ds(pl.multiple_of(p * GDN_STEP, GDN_STEP), GDN_STEP)
        q2, k2, v2 = q_s[rows, :], k_s[rows, :], v_s[rows, :]
        col = col_ref[rows, :]
        gcols = [pick(col, d * HEADS + head) for d in range(2)]
        betas = [pick(col, (2 + d) * HEADS + head) for d in range(2)]
        grows = [jnp.sum(jnp.where(sub8 == head, rowt_ref[d * HEADS:(d + 1) * HEADS, rows], 0.0),
                         axis=0, keepdims=True) for d in range(2)]
        inst = []
        for half in range(GDN_STEP // GDN_CHUNK):
            sl = slice(half * GDN_CHUNK, (half + 1) * GDN_CHUNK)
            qc, kc, vc = q2[sl], k2[sl], v2[sl]
            kk = _dot1(kc, kc, _NT)
            qk = _dot1(qc, kc, _NT)
            for d in range(2):
                gcol, beta = gcols[d][sl], betas[d][sl]
                decay = jnp.where(incl[d], jnp.exp(gcol - grows[d][:, sl]), 0.0)
                m = jnp.where(strict[d], beta * kk * decay, 0.0)
                inst.append((half, d, qc, kc, vc, qk, gcol, beta, decay, m))
        ms = [t[-1] for t in inst]
        pws = [jnp.where(same_blk[0], -m, 0.0) for m in ms]
        tinvs = [eye + pw for pw in pws]
        for _ in range(2):
            pws = [_dot_inv(pw, pw) for pw in pws]
            tinvs = [t + _dot_inv(t, pw) for t, pw in zip(tinvs, pws)]
        for mask in off_masks:
            tmp = [_dot_inv(jnp.where(mask, m, 0.0), t) for m, t in zip(ms, tinvs)]
            tinvs = [t - _dot_inv(t, x) for t, x in zip(tinvs, tmp)]
        uws = [_dot1(tinv, jnp.concatenate([vc * beta, kc * (beta * jnp.exp(gcol))], axis=1)).astype(BF16)
               for (half, d, qc, kc, vc, qk, gcol, beta, decay, m), tinv in zip(inst, tinvs)]
        bas, ows = [], []
        for (half, d, qc, kc, vc, qk, gcol, beta, decay, m), uw in zip(inst, uws):
            glast = gcol[GDN_CHUNK - 1:GDN_CHUNK] if d == 0 else gcol[0:1]
            bas.append(_dg((kc * jnp.exp(glast - gcol)).astype(BF16), uw, _TN))
            ows.append(_dg((qk * decay).astype(BF16), uw, _NN))
        for (half, d, qc, kc, vc, qk, gcol, beta, decay, m), ba, ow in zip(inst, bas, ows):
            chunk0 = p * GDN_STEP + half * GDN_CHUNK
            o_s[d, pl.ds(pl.multiple_of(chunk0, GDN_CHUNK), GDN_CHUNK), :] = ow[:, :HEAD_DIM]
            b_s[d, pl.ds(pl.multiple_of(2 * chunk0, HEAD_DIM), HEAD_DIM), :] = ba[:, :HEAD_DIM]
            aq_s[d, pl.ds(pl.multiple_of(3 * chunk0, GDN_AQ), GDN_AQ), :] = jnp.concatenate(
                [ba[:, HEAD_DIM:], qc * jnp.exp(gcol) - ow[:, HEAD_DIM:]], axis=0).astype(BF16)
        return carry

    lax.fori_loop(0, UNIT // GDN_STEP, phase1, 0)

    lane1 = lax.broadcasted_iota(jnp.int32, (1, LANES), 1)

    def advance(chains, states):
        rs = [_dg(aq_s[d, pl.ds(pl.multiple_of(ch * GDN_AQ, GDN_AQ), GDN_AQ), :], s.astype(BF16), _NN)
              for (d, ch), s in zip(chains, states)]
        new = []
        for (d, ch), s, r in zip(chains, states, rs):
            tile0 = pl.multiple_of(ch * GDN_CHUNK + (GDN_CHUNK - 8 if d == 0 else 0), 8)
            last = col_ref[pl.ds(tile0, 8), :]
            last = last[7:8] if d == 0 else last[0:1]
            glast = jnp.sum(jnp.where(lane1 == d * HEADS + head, last, 0.0), axis=-1, keepdims=True)
            b = b_s[d, pl.ds(pl.multiple_of(ch * HEAD_DIM, HEAD_DIM), HEAD_DIM), :]
            new.append(s * jnp.exp(glast) + (b - r[:HEAD_DIM]))
            rows = pl.ds(pl.multiple_of(ch * GDN_CHUNK, GDN_CHUNK), GDN_CHUNK)
            o_s[d, rows, :] = o_s[d, rows, :] + r[HEAD_DIM:]
        return tuple(new)

    @pl.when(is_ctx)
    def _context_unit():
        def body(i, states):
            chains = [(d, sq * chunks_per_seq + (i if d == 0 else chunks_per_seq - 1 - i))
                      for sq in range(SEQ_PER_UNIT) for d in range(2)]
            return advance(chains, states)
        zero = jnp.zeros((HEAD_DIM, HEAD_DIM), F32)
        final = lax.fori_loop(0, chunks_per_seq, body, (zero,) * (2 * SEQ_PER_UNIT))
        for sq in range(SEQ_PER_UNIT):
            for d in range(2):
                s_ref[sq, d] = final[2 * sq + d]

    @pl.when(jnp.logical_not(is_ctx))
    def _latent_unit():
        def body(i, states):
            return advance([(0, i), (1, n_chunks - 1 - i)], states)
        final = lax.fori_loop(0, n_chunks, body, (s0_ref[0], s0_ref[1]))
        for sq in range(SEQ_PER_UNIT):
            for d in range(2):
                s_ref[sq, d] = final[d]

    o_ref[...] = _head_rmsnorm_gate(o_s[0] + o_s[1], g_ref[...], og_ref[...]).astype(o_ref.dtype)


def _gdn_call(z, col, rowt, conv_w, onorm_g, state, layer):
    def zcol(off):
        return pl.BlockSpec((None, UNIT, HEAD_DIM), lambda u, h: (off // HEAD_DIM + h, u, 0))
    def wcol(part):
        return pl.BlockSpec((CONV_K * CONV_K, HEAD_DIM), lambda u, h: (0, part * HEADS + h))
    scr = lambda *shape: pltpu.VMEM(shape, F32)
    return pl.pallas_call(
        _gdn_kernel,
        grid=(N_UNITS, HEADS),
        in_specs=[zcol(OFF_GQ), zcol(OFF_GK), zcol(OFF_GV), zcol(OFF_GG),
                  pl.BlockSpec((UNIT, LANES), lambda u, h: (u, 0)),
                  pl.BlockSpec((LANES, UNIT), lambda u, h: (0, u)),
                  wcol(0), wcol(1), wcol(2),
                  pl.BlockSpec((1, HEAD_DIM), lambda u, h: (0, 0)),
                  pl.BlockSpec((None, None, 2, None, HEAD_DIM, HEAD_DIM),
                               lambda u, h: (jnp.maximum(u - N_CTX_UNITS, 0), layer, 0, h, 0, 0))],
        out_specs=[pl.BlockSpec((None, UNIT, HEAD_DIM), lambda u, h: (h, u, 0)),
                   pl.BlockSpec((None, SEQ_PER_UNIT, 2, None, HEAD_DIM, HEAD_DIM),
                                lambda u, h: (u, 0, 0, h, 0, 0))],
        out_shape=[jax.ShapeDtypeStruct((HEADS, N_TOK, HEAD_DIM), BF16),
                   jax.ShapeDtypeStruct((N_UNITS, SEQ_PER_UNIT, 2, HEADS, HEAD_DIM, HEAD_DIM), F32)],
        scratch_shapes=[scr(UNIT + 2 * CONV_PAD, HEAD_DIM),
                        scr(UNIT, HEAD_DIM), scr(UNIT, HEAD_DIM), scr(UNIT, HEAD_DIM),
                        scr(2, UNIT, HEAD_DIM),
                        scr(2, (UNIT // GDN_CHUNK) * HEAD_DIM, HEAD_DIM),
                        pltpu.VMEM((2, (UNIT // GDN_CHUNK) * GDN_AQ, HEAD_DIM), BF16)],
        compiler_params=_cparams(("arbitrary", "arbitrary")),
        name="gdn_scan",
    )(z, z, z, z, col, rowt, conv_w, conv_w, conv_w, onorm_g.reshape(1, HEAD_DIM), state)


def kernel(x_prompt, x_sample, c, state_hgrn, state_gdn, c_ctx, norm1_g, norm2_g, w_mod, b_mod, w_in, hg_lb, hg_onorm_g, cm_vnorm_g, cm_ws, cm_bs, gdn_conv, gdn_A_log, gdn_dt_bias, gdn_onorm_g, w_br_hg, w_br_cm, w_br_gdn, w_out, w_ff1, w_ff2, final_g):
    x = jnp.concatenate([x_prompt.reshape(N_CTX_TOK, D_MODEL), x_sample.reshape(N_LAT_TOK, D_MODEL)], axis=0)
    cvec = jnp.concatenate([c_ctx[None, :], c, jnp.zeros((MOD_ROWS - 1 - DEC_BATCH, D_MODEL), F32)], axis=0)
    mod3 = _mod_call(cvec, w_mod, b_mod).reshape(DEPTH * MOD_ROWS, 1, 6 * D_MODEL)

    lb_all = jnp.cumsum(jax.nn.softmax(hg_lb.astype(F32), axis=0), axis=0)
    lb_all = lb_all - lb_all[:1]

    w_in_t = jnp.swapaxes(w_in, 1, 2)
    w_ff2_bf = w_ff2.astype(BF16)

    new_hg, new_gdn = [], []
    h = _normmod_call(x, norm1_g[0], mod3, 0, 0, 1)
    for l in range(DEPTH):
        z = _mm_call(h, w_in_t, l, col0=0, n=N_MAIN, act=None, out_dtype=F32, tm=1024, tn=1024,
                     w_transposed=True, lane_tile_major=True, name="in_proj")
        ab = _mm_call(h, w_in_t, l, col0=OFF_AB, n=LANES, act=None, out_dtype=F32, tm=1024, tn=LANES,
                      w_transposed=True, name="in_proj_ab")
        gates = _mm_call(h, w_in_t, l, col0=OFF_GATES, n=3 * D_MODEL, act="sigmoid", out_dtype=F32,
                         tm=1024, tn=1024, w_transposed=True, name="in_proj_gates")

        o_a, s_hg = _hgrn_call(z, lb_all[l], hg_onorm_g[l], state_hgrn, l)
        o_b = _gmlp_call(z, cm_vnorm_g[l], cm_ws[l], cm_bs[l].T)
        col, rowt = _gdn_gates_call(ab, gdn_A_log[l], gdn_dt_bias[l])
        o_c, s_gdn = _gdn_call(z, col, rowt, gdn_conv[l].reshape(CONV_K * CONV_K, 3 * HG_F),
                               gdn_onorm_g[l], state_gdn, l)
        new_hg.append(s_hg[:N_CTX_UNITS].reshape(BATCH, 2, HEADS, HEAD_DIM, HEAD_DIM))
        new_gdn.append(s_gdn[:N_CTX_UNITS].reshape(BATCH, 2, HEADS, HEAD_DIM, HEAD_DIM))

        merged = _merge_call(o_a, o_b, o_c, w_br_hg, w_br_cm, w_br_gdn, gates, l)
        x, h2 = _outproj_norm_call(merged, w_out, x, norm2_g[l], mod3, l)
        up = _mm_call(h2, w_ff1, l, col0=0, n=D_FF, act="relu2", out_dtype=BF16, tm=1024, tn=1024, name="ffn_up")
        if l + 1 < DEPTH:
            x, h = _ffn_down_norm_call(up, w_ff2_bf, x, mod3, l, norm1_g[l + 1])
        else:
            x = _mm_resid_call(up, w_ff2_bf, x, mod3, l, 5, tm=512, tn=512, name="ffn_down")

    y_prompt = _final_norm_call(x, final_g, 0, N_CTX_TOK).reshape(BATCH, SEQ, D_MODEL)
    y_sample = _final_norm_call(x, final_g, N_CTX_TOK, N_LAT_TOK).reshape(DEC_BATCH, DEC_SEQ, D_MODEL)
    return (y_prompt, y_sample, jnp.stack(new_hg, axis=1), jnp.stack(new_gdn, axis=1))
```

```python
import functools

import jax
import jax.numpy as jnp
from jax import lax
from jax.experimental import pallas as pl
from jax.experimental.pallas import tpu as pltpu

F32 = jnp.float32
BF16 = jnp.bfloat16

D_MODEL = 2048
BATCH = 16
SEQ = 256
DEPTH = 2
DEC_BATCH = 4
DEC_SEQ = 1024
GRID_W = 64
EPS = 1e-6
D_FF = 4 * D_MODEL
HEADS = 8
HEAD_DIM = 128
HG_F = HEADS * HEAD_DIM
CM_GROUPS = 8
CM_W = CM_GROUPS * HEAD_DIM
CM_CHUNK = 128
GDN_CHUNK = 64
CONV_K = 3

OFF_HQ, OFF_HI, OFF_HG, OFF_HFF, OFF_HFB = 0, 1024, 2048, 3072, 4096
OFF_CU, OFF_CV = 5120, 6144
OFF_GQ, OFF_GK, OFF_GV, OFF_GG = 7168, 8192, 9216, 10240
OFF_AB = 11264
OFF_GATES = 11296
IN_DIM = 17440
N_MAIN = OFF_AB

N_CTX_TOK = BATCH * SEQ
N_LAT_TOK = DEC_BATCH * DEC_SEQ
N_TOK = N_CTX_TOK + N_LAT_TOK
UNIT = DEC_SEQ
N_CTX_UNITS = N_CTX_TOK // UNIT
N_UNITS = N_TOK // UNIT
N_LAT_UNITS = N_UNITS - N_CTX_UNITS
SEQ_PER_UNIT = UNIT // SEQ

LANES = 128
MOD_ROWS = 8
HG_BLOCK = 256
HG_SUB = 32
HG_INST = 2 * (UNIT // HG_BLOCK)
VMEM_LIMIT = 56 * 1024 * 1024

_NT = (((1,), (1,)), ((), ()))
_TN = (((0,), (0,)), ((), ()))
_NN = (((1,), (0,)), ((), ()))


def _dg(a, b, dims):
    return lax.dot_general(a, b, dims, preferred_element_type=F32)


def _split2(x):
    hi = x.astype(BF16)
    lo = (x - hi.astype(F32)).astype(BF16)
    return hi, lo


def _dot1(a, b, dims=_NN):
    return _dg(a.astype(BF16), b.astype(BF16), dims)


def _dot3(a, b, dims=_NN):
    ah, al = _split2(a)
    bh, bl = _split2(b)
    return _dg(ah, bh, dims) + (_dg(ah, bl, dims) + _dg(al, bh, dims))


_dot_inv = _dot1


def _dot01(m, x):
    hi = x.astype(BF16)
    r = x - hi.astype(F32)
    mid = r.astype(BF16)
    lo = (r - mid.astype(F32)).astype(BF16)
    return _dg(m, hi, _NN) + (_dg(m, mid, _NN) + _dg(m, lo, _NN))


def _dot01_2(m, x):
    hi, lo = _split2(x)
    return _dg(m, hi, _NN) + _dg(m, lo, _NN)


def _sigmoid(x):
    return 1.0 / (1.0 + jnp.exp(-x))


def _silu(x):
    return x * _sigmoid(x)


def _gelu(x):
    return 0.5 * x * (1.0 + lax.erf(x * (2.0 ** -0.5)))


def _softplus(x):
    return jnp.maximum(x, 0.0) + jnp.log1p(jnp.exp(-jnp.abs(x)))


def _mod_row_of_tile(i, tm):
    return jnp.maximum(0, (i * tm - N_CTX_TOK) // DEC_SEQ + 1)


assert N_LAT_UNITS == N_CTX_UNITS


def _scan_unit(g):
    return jnp.where(g % 2 == 0, N_CTX_UNITS + g // 2, g // 2)


def _is_ctx_step(g):
    return g % 2 == 1


def _state_out_spec(layer):
    return pl.BlockSpec((SEQ_PER_UNIT, None, 2, None, HEAD_DIM, HEAD_DIM),
                        lambda h, g: (g // 2, layer, 0, h, 0, 0))


def _state_in_spec(layer):
    return pl.BlockSpec((None, None, 2, None, HEAD_DIM, HEAD_DIM),
                        lambda h, g: (g // 2, layer, 0, h, 0, 0))


_STATE_SHAPE = jax.ShapeDtypeStruct((BATCH, DEPTH, 2, HEADS, HEAD_DIM, HEAD_DIM), F32)


def _without_arg(kernel, pos):
    def wrapped(*refs):
        return kernel(*refs[:pos], *refs[pos + 1:])
    return wrapped


def _cparams(sem):
    return pltpu.CompilerParams(dimension_semantics=sem, vmem_limit_bytes=VMEM_LIMIT)


def _mod_kernel(c_ref, w_ref, b_ref, o_ref):
    s = _silu(c_ref[...])
    hi, lo = _split2(s)
    w = w_ref[...].astype(BF16)
    o_ref[...] = _dg(hi, w, _NN) + _dg(lo, w, _NN) + b_ref[...]


def _mod_call(cvec, w_mod, b_mod):
    tn = 1024
    n = 6 * D_MODEL
    return pl.pallas_call(
        _mod_kernel,
        grid=(DEPTH, n // tn),
        in_specs=[pl.BlockSpec((MOD_ROWS, D_MODEL), lambda l, j: (0, 0)),
                  pl.BlockSpec((None, D_MODEL, tn), lambda l, j: (l, 0, j)),
                  pl.BlockSpec((None, 1, tn), lambda l, j: (l, 0, j))],
        out_specs=pl.BlockSpec((None, MOD_ROWS, tn), lambda l, j: (l, 0, j)),
        out_shape=jax.ShapeDtypeStruct((DEPTH, MOD_ROWS, n), F32),
        compiler_params=_cparams(("arbitrary", "arbitrary")),
        name="modulation",
    )(cvec, w_mod, b_mod.reshape(DEPTH, 1, n))


def _adaln(x, g, shift, scale):
    y = x * lax.rsqrt(jnp.mean(x * x, axis=-1, keepdims=True) + EPS) * g
    return y * (1.0 + scale) + shift


def _residual_specs(x, tm, idx):
    if not isinstance(x, tuple):
        return [pl.BlockSpec((tm, D_MODEL), lambda *g: (idx(*g), 0))], [x]
    n_ctx = N_CTX_TOK // tm
    return ([pl.BlockSpec((tm, D_MODEL), lambda *g: (jnp.minimum(idx(*g), n_ctx - 1), 0)),
             pl.BlockSpec((tm, D_MODEL), lambda *g: (jnp.maximum(idx(*g) - n_ctx, 0), 0))], list(x))


def _residual_tile(refs, tile, tm):
    if len(refs) == 1:
        return refs[0][...]
    return jnp.where(tile < N_CTX_TOK // tm, refs[0][...], refs[1][...])


def _normmod_kernel(*refs, n_x):
    x_refs, (g_ref, sh_ref, sc_ref, o_ref) = refs[:n_x], refs[n_x:]
    x = _residual_tile(x_refs, pl.program_id(0), o_ref.shape[0])
    o_ref[...] = _adaln(x, g_ref[...], sh_ref[...], sc_ref[...]).astype(o_ref.dtype)


def _normmod_call(x, g, mod3, layer, k_shift, k_scale):
    tm = 512
    def mod_spec(k):
        return pl.BlockSpec((None, 1, D_MODEL),
                            lambda i: (layer * MOD_ROWS + _mod_row_of_tile(i, tm), 0, k))
    x_specs, x_args = _residual_specs(x, tm, lambda i: i)
    return pl.pallas_call(
        functools.partial(_normmod_kernel, n_x=len(x_args)),
        grid=(N_TOK // tm,),
        in_specs=x_specs + [pl.BlockSpec((1, D_MODEL), lambda i: (0, 0)), mod_spec(k_shift), mod_spec(k_scale)],
        out_specs=pl.BlockSpec((tm, D_MODEL), lambda i: (i, 0)),
        out_shape=jax.ShapeDtypeStruct((N_TOK, D_MODEL), BF16),
        compiler_params=_cparams(("arbitrary",)),
        name="norm_mod",
    )(*x_args, g.reshape(1, D_MODEL), mod3, mod3)


def _final_norm_kernel(x_ref, g_ref, o_ref):
    x = x_ref[...]
    o_ref[...] = x * lax.rsqrt(jnp.mean(x * x, axis=-1, keepdims=True) + EPS) * g_ref[...]


def _final_norm_call(x, g, row0, n_rows):
    tm = 512
    return pl.pallas_call(
        _final_norm_kernel,
        grid=(n_rows // tm,),
        in_specs=[pl.BlockSpec((tm, D_MODEL), lambda i: (row0 // tm + i, 0)),
                  pl.BlockSpec((1, D_MODEL), lambda i: (0, 0))],
        out_specs=pl.BlockSpec((tm, D_MODEL), lambda i: (i, 0)),
        out_shape=jax.ShapeDtypeStruct((n_rows, D_MODEL), F32),
        compiler_params=_cparams(("arbitrary",)),
        name="final_norm",
    )(x, g.reshape(1, D_MODEL))


def _mm_kernel(x_ref, w_ref, o_ref, wbf_ref, *, act, w_transposed):
    @pl.when(pl.program_id(1) == 0)
    def _cast_weights():
        wbf_ref[...] = w_ref[...].reshape(wbf_ref.shape).astype(BF16)

    acc = lax.dot_general(x_ref[...], wbf_ref[...], _NT if w_transposed else _NN, preferred_element_type=F32)
    if act == "sigmoid":
        acc = _sigmoid(acc)
    elif act == "relu2":
        acc = jnp.square(jnp.maximum(acc, 0.0))
    if len(o_ref.shape) == 3:
        for c in range(o_ref.shape[0]):
            o_ref[c] = acc[:, c * LANES:(c + 1) * LANES].astype(o_ref.dtype)
    else:
        o_ref[...] = acc.astype(o_ref.dtype)


def _mm_call(x, w_all, layer, *, col0, n, act, out_dtype, tm, tn, name, w_transposed=False,
             lane_tile_major=False):
    m, k = x.shape
    assert n % tn == 0 and m % tm == 0
    if w_transposed:
        assert col0 % 8 == 0
        w_spec = pl.BlockSpec((pl.Element(1), pl.Element(tn), pl.Element(k)),
                              lambda j, i: (layer, pl.multiple_of(col0 + j * tn, 8), 0))
        w_scratch = pltpu.VMEM((tn, k), BF16)
    else:
        assert col0 % tn == 0
        w_spec = pl.BlockSpec((None, k, tn), lambda j, i: (layer, 0, col0 // tn + j))
        w_scratch = pltpu.VMEM((k, tn), BF16)
    if lane_tile_major:
        out_spec = pl.BlockSpec((tn // LANES, tm, LANES), lambda j, i: (j, i, 0))
        out_shape = jax.ShapeDtypeStruct((n // LANES, m, LANES), out_dtype)
    else:
        out_spec = pl.BlockSpec((tm, tn), lambda j, i: (i, j))
        out_shape = jax.ShapeDtypeStruct((m, n), out_dtype)
    return pl.pallas_call(
        functools.partial(_mm_kernel, act=act, w_transposed=w_transposed),
        grid=(n // tn, m // tm),
        in_specs=[pl.BlockSpec((tm, k), lambda j, i: (i, 0)), w_spec],
        out_specs=out_spec,
        out_shape=out_shape,
        scratch_shapes=[w_scratch],
        compiler_params=_cparams(("arbitrary", "arbitrary")),
        name=name,
    )(x, w_all)


def _mm_bf16_kernel(x_ref, w_ref, r_ref, g_ref, o_ref):
    acc = jnp.dot(x_ref[...], w_ref[...], preferred_element_type=F32)
    o_ref[...] = r_ref[...] + g_ref[...] * acc


def _mm_resid_call(x, w, resid, mod3, layer, k_gate, *, tm, tn, name):
    m, k = x.shape
    n = w.shape[2]
    per_tile = D_MODEL // tn
    return pl.pallas_call(
        _mm_bf16_kernel,
        grid=(m // tm, n // tn),
        in_specs=[pl.BlockSpec((tm, k), lambda i, j: (i, 0)),
                  pl.BlockSpec((None, k, tn), lambda i, j: (layer, 0, j)),
                  pl.BlockSpec((tm, tn), lambda i, j: (i, j)),
                  pl.BlockSpec((None, 1, tn),
                               lambda i, j: (layer * MOD_ROWS + _mod_row_of_tile(i, tm), 0,
                                             k_gate * per_tile + j))],
        out_specs=pl.BlockSpec((tm, tn), lambda i, j: (i, j)),
        out_shape=jax.ShapeDtypeStruct((m, n), F32),
        compiler_params=_cparams(("arbitrary", "arbitrary")),
        name=name,
    )(x, w, resid, mod3)


def _outproj_norm_kernel(*refs, n_x):
    m_ref, w_ref = refs[:2]
    x_refs = refs[2:2 + n_x]
    gate_ref, g_ref, sh_ref, sc_ref, xo_ref, h_ref, wbf_ref = refs[2 + n_x:]

    @pl.when(pl.program_id(0) == 0)
    def _cast_weights():
        wbf_ref[...] = w_ref[...].astype(BF16)

    acc = jnp.dot(m_ref[...], wbf_ref[...], preferred_element_type=F32)
    xn = _residual_tile(x_refs, pl.program_id(0), xo_ref.shape[0]) + gate_ref[...] * acc
    xo_ref[...] = xn
    h_ref[...] = _adaln(xn, g_ref[...], sh_ref[...], sc_ref[...]).astype(h_ref.dtype)


def _outproj_norm_call(merged, w_out, x, norm_g, mod3, layer):
    tm = 256 if isinstance(x, tuple) else 512
    def mod_spec(k):
        return pl.BlockSpec((None, 1, D_MODEL),
                            lambda i: (layer * MOD_ROWS + _mod_row_of_tile(i, tm), 0, k))
    row = pl.BlockSpec((tm, D_MODEL), lambda i: (i, 0))
    x_specs, x_args = _residual_specs(x, tm, lambda i: i)
    return pl.pallas_call(
        functools.partial(_outproj_norm_kernel, n_x=len(x_args)),
        grid=(N_TOK // tm,),
        in_specs=[row,
                  pl.BlockSpec((None, D_MODEL, D_MODEL), lambda i: (layer, 0, 0), pipeline_mode=pl.Buffered(1))]
                 + x_specs + [mod_spec(2), pl.BlockSpec((1, D_MODEL), lambda i: (0, 0)), mod_spec(3), mod_spec(4)],
        out_specs=[row, row],
        out_shape=[jax.ShapeDtypeStruct((N_TOK, D_MODEL), F32),
                   jax.ShapeDtypeStruct((N_TOK, D_MODEL), BF16)],
        scratch_shapes=[pltpu.VMEM((D_MODEL, D_MODEL), BF16)],
        compiler_params=_cparams(("arbitrary",)),
        name="out_proj_norm",
    )(merged, w_out, *x_args, mod3, norm_g.reshape(1, D_MODEL), mod3, mod3)


def _ffn_down_norm_kernel(x_ref, w_ref, r_ref, gate_ref, g_ref, sh_ref, sc_ref, o_ref, h_ref, row_ref):
    j = pl.program_id(1)
    tn = o_ref.shape[1]
    acc = jnp.dot(x_ref[...], w_ref[...], preferred_element_type=F32)
    xn = r_ref[...] + gate_ref[...] * acc
    o_ref[...] = xn
    row_ref[:, pl.ds(pl.multiple_of(j * tn, tn), tn)] = xn

    @pl.when(j == pl.num_programs(1) - 1)
    def _next_layer_norm():
        h_ref[...] = _adaln(row_ref[...], g_ref[...], sh_ref[...], sc_ref[...]).astype(h_ref.dtype)


def _ffn_down_norm_call(up, w_bf, resid, mod3, layer, next_norm_g):
    tm, tn = 512, 512
    m, k = up.shape
    per_tile = D_MODEL // tn
    def mod_row(i):
        return _mod_row_of_tile(i, tm)
    def next_mod(kk):
        return pl.BlockSpec((None, 1, D_MODEL), lambda i, j: ((layer + 1) * MOD_ROWS + mod_row(i), 0, kk))
    return pl.pallas_call(
        _ffn_down_norm_kernel,
        grid=(m // tm, D_MODEL // tn),
        in_specs=[pl.BlockSpec((tm, k), lambda i, j: (i, 0)),
                  pl.BlockSpec((None, k, tn), lambda i, j: (layer, 0, j)),
                  pl.BlockSpec((tm, tn), lambda i, j: (i, j)),
                  pl.BlockSpec((None, 1, tn), lambda i, j: (layer * MOD_ROWS + mod_row(i), 0, 5 * per_tile + j)),
                  pl.BlockSpec((1, D_MODEL), lambda i, j: (0, 0)), next_mod(0), next_mod(1)],
        out_specs=[pl.BlockSpec((tm, tn), lambda i, j: (i, j)),
                   pl.BlockSpec((tm, D_MODEL), lambda i, j: (i, 0))],
        out_shape=[jax.ShapeDtypeStruct((m, D_MODEL), F32),
                   jax.ShapeDtypeStruct((m, D_MODEL), BF16)],
        scratch_shapes=[pltpu.VMEM((tm, D_MODEL), F32)],
        compiler_params=_cparams(("arbitrary", "arbitrary")),
        name="ffn_down_norm",
    )(up, w_bf, resid, mod3, next_norm_g.reshape(1, D_MODEL), mod3, mod3)


def _merge_kernel(oa_ref, ob_ref, oc_ref, wa_ref, wb_ref, wc_ref, ga_ref, gb_ref, gc_ref, o_ref, wbf_ref):
    @pl.when(pl.program_id(1) == 0)
    def _cast_weights():
        wbf_ref[0] = wa_ref[...].astype(BF16)
        wbf_ref[1] = wb_ref[...].astype(BF16)
        wbf_ref[2] = wc_ref[...].astype(BF16)

    def rows_of(ref):
        return jnp.concatenate([ref[t] for t in range(ref.shape[0])], axis=1)

    a = jnp.dot(rows_of(oa_ref), wbf_ref[0], preferred_element_type=F32)
    b = jnp.dot(rows_of(ob_ref), wbf_ref[1], preferred_element_type=F32)
    c = jnp.dot(rows_of(oc_ref), wbf_ref[2], preferred_element_type=F32)
    o_ref[...] = (ga_ref[...] * a + gb_ref[...] * b + gc_ref[...] * c).astype(o_ref.dtype)


def _merge_call(o_a, o_b, o_c, w_a, w_b, w_c, gates, layer):
    tm, tn = 1024, 512
    nj = D_MODEL // tn
    br = pl.BlockSpec((HEADS, tm, HEAD_DIM), lambda j, i: (0, i, 0))
    wt = pl.BlockSpec((None, HG_F, tn), lambda j, i: (layer, 0, j))
    def gate_spec(k):
        return pl.BlockSpec((tm, tn), lambda j, i: (i, k * nj + j))
    return pl.pallas_call(
        _merge_kernel,
        grid=(nj, N_TOK // tm),
        in_specs=[br, br, br, wt, wt, wt, gate_spec(0), gate_spec(1), gate_spec(2)],
        out_specs=pl.BlockSpec((tm, tn), lambda j, i: (i, j)),
        out_shape=jax.ShapeDtypeStruct((N_TOK, D_MODEL), BF16),
        scratch_shapes=[pltpu.VMEM((3, HG_F, tn), BF16)],
        compiler_params=_cparams(("arbitrary", "arbitrary")),
        name="branch_merge",
    )(o_a, o_b, o_c, w_a, w_b, w_c, gates, gates, gates)


def _head_rmsnorm_gate(o, g, og):
    y = o * lax.rsqrt(jnp.mean(o * o, axis=-1, keepdims=True) + EPS) * g
    return y * _silu(og)


def _hgrn_kernel(q_ref, i_ref, og_ref, ff_ref, fb_ref, lb_ref, g_ref, s0_ref, o_ref, s_ref,
                 acc_ref, b_ref, stc_ref, km_ref):
    is_ctx = _is_ctx_step(pl.program_id(1))
    nblk = UNIT // HG_BLOCK
    nsub = HG_BLOCK // HG_SUB
    r = lax.broadcasted_iota(jnp.int32, (HG_BLOCK, HG_BLOCK), 0)
    c = lax.broadcasted_iota(jnp.int32, (HG_BLOCK, HG_BLOCK), 1)
    same = (r // HG_SUB) == (c // HG_SUB)
    row_sub = lax.broadcasted_iota(jnp.int32, (HG_BLOCK, HEAD_DIM), 0) // HG_SUB
    tri = (jnp.logical_and(same, c <= r), jnp.logical_and(same, c >= r))
    tri_bf = [jnp.where(t, 1.0, 0.0).astype(BF16) for t in tri]
    z_refs = (ff_ref, fb_ref)
    inst = [(blk, d) for blk in range(nblk) for d in range(2)]

    def expand(x):
        return jnp.broadcast_to(x[:, None, :], (nsub, HG_SUB, HEAD_DIM)).reshape(HG_BLOCK, HEAD_DIM)

    def blk_rows(blk):
        return slice(blk * HG_BLOCK, (blk + 1) * HG_BLOCK)

    qs = [_silu(q_ref[blk_rows(blk), :]) for blk in range(nblk)]
    vs = [i_ref[blk_rows(blk), :] for blk in range(nblk)]
    ks, lfs = [], []
    for blk, d in inst:
        lb = lb_ref[d:d + 1, :]
        f = lb + (1.0 - lb) * _sigmoid(z_refs[d][blk_rows(blk), :])
        lfs.append(jnp.log(f))
        ks.append(1.0 - f)
    bs = [_dot01_2(tri_bf[d], lf) for (blk, d), lf in zip(inst, lfs)]
    tots, qts, kts, qds = [], [], [], []
    for n, (blk, d) in enumerate(inst):
        b = bs[n]
        b_ref[n] = b
        tot = b_ref[n, pl.ds((HG_SUB - 1) if d == 0 else 0, nsub, stride=HG_SUB), :]
        mid_f = expand(b_ref[n, pl.ds(HG_SUB // 2, nsub, stride=HG_SUB), :])
        tots.append(tot)
        qts.append(qs[blk] * jnp.exp(b - mid_f))
        kts.append(ks[n] * jnp.exp(mid_f - b))
        qds.append((qs[blk] * jnp.exp(b)).astype(BF16))
        kd = ks[n] * jnp.exp(expand(tot) - b)
        for s in range(nsub):
            km_ref[n, :, s * HEAD_DIM:(s + 1) * HEAD_DIM] = jnp.where(row_sub == s, kd, 0.0).astype(BF16)
    scs = [jnp.where(tri[d], _dot1(qt, kt, _NT), 0.0) for (blk, d), qt, kt in zip(inst, qts, kts)]
    uts = [_dg(vs[blk].astype(BF16), km_ref[n], _TN) for n, (blk, d) in enumerate(inst)]
    outs = [_dot1(sc, vs[blk]) for (blk, d), sc in zip(inst, scs)]
    for d in range(2):
        st = s0_ref[d].T
        for blk in (range(nblk) if d == 0 else range(nblk - 1, -1, -1)):
            n = inst.index((blk, d))
            st = jnp.where(is_ctx, 0.0, st)
            for s in (range(nsub) if d == 0 else range(nsub - 1, -1, -1)):
                stc_ref[n, s] = st.astype(BF16)
                st = st * jnp.exp(tots[n][s:s + 1, :]) + uts[n][:, s * HEAD_DIM:(s + 1) * HEAD_DIM]
            s_ref[blk, d] = st.T
    for n, (blk, d) in enumerate(inst):
        o_int = [_dg(qds[n][s * HG_SUB:(s + 1) * HG_SUB], stc_ref[n, s], _NT) for s in range(nsub)]
        acc_ref[d, blk_rows(blk), :] = outs[n] + jnp.concatenate(o_int, axis=0)
    o_ref[...] = _head_rmsnorm_gate(acc_ref[0] + acc_ref[1], g_ref[...], og_ref[...]).astype(o_ref.dtype)


def _hgrn_call(z, lb, onorm_g, state, layer, new_states):
    def col(off):
        return pl.BlockSpec((None, UNIT, HEAD_DIM), lambda h, g: (off // HEAD_DIM + h, _scan_unit(g), 0))
    in_specs = [col(OFF_HQ), col(OFF_HI), col(OFF_HG), col(OFF_HFF), col(OFF_HFB),
                pl.BlockSpec((2, HEAD_DIM), lambda h, g: (0, h)),
                pl.BlockSpec((1, HEAD_DIM), lambda h, g: (0, 0)),
                _state_in_spec(layer)]
    args = [z, z, z, z, z, lb, onorm_g.reshape(1, HEAD_DIM), state]
    kernel, aliases = _hgrn_kernel, {}
    if new_states is not None:
        kernel, aliases = _without_arg(_hgrn_kernel, len(args)), {len(args): 1}
        in_specs.append(pl.BlockSpec(memory_space=pl.ANY))
        args.append(new_states)
    return pl.pallas_call(
        kernel,
        grid=(HEADS, N_UNITS),
        in_specs=in_specs,
        out_specs=[pl.BlockSpec((None, UNIT, HEAD_DIM), lambda h, g: (h, _scan_unit(g), 0)), _state_out_spec(layer)],
        out_shape=[jax.ShapeDtypeStruct((HEADS, N_TOK, HEAD_DIM), BF16), _STATE_SHAPE],
        input_output_aliases=aliases,
        scratch_shapes=[pltpu.VMEM((2, UNIT, HEAD_DIM), F32),
                        pltpu.VMEM((HG_INST, HG_BLOCK, HEAD_DIM), F32),
                        pltpu.VMEM((HG_INST, HG_BLOCK // HG_SUB, HEAD_DIM, HEAD_DIM), BF16),
                        pltpu.VMEM((HG_INST, HG_BLOCK, (HG_BLOCK // HG_SUB) * HEAD_DIM), BF16)],
        compiler_params=_cparams(("arbitrary", "arbitrary")),
        name="hgrn2_scan",
    )(*args)


def _gmlp_kernel(u_ref, v_ref, vn_ref, ws_ref, bs_ref, o_ref):
    tm = u_ref.shape[1]
    for g in range(CM_GROUPS):
        cols = slice(g * HEAD_DIM, (g + 1) * HEAD_DIM)
        vg = _gelu(v_ref[g])
        vg = vg * lax.rsqrt(jnp.mean(vg * vg, axis=-1, keepdims=True) + EPS) * vn_ref[:, cols]
        ug = _gelu(u_ref[g])
        w = ws_ref[g]
        bias = bs_ref[:, g:g + 1]
        for ch in range(tm // CM_CHUNK):
            rows = slice(ch * CM_CHUNK, (ch + 1) * CM_CHUNK)
            s = _dot3(w, vg[rows]) + bias
            o_ref[g, rows, :] = (ug[rows] * s).astype(o_ref.dtype)


def _gmlp_call(z, vnorm_g, ws, bs_t):
    tm = 512
    return pl.pallas_call(
        _gmlp_kernel,
        grid=(N_TOK // tm,),
        in_specs=[pl.BlockSpec((CM_GROUPS, tm, HEAD_DIM), lambda i: (OFF_CU // CM_W, i, 0)),
                  pl.BlockSpec((CM_GROUPS, tm, HEAD_DIM), lambda i: (OFF_CV // CM_W, i, 0)),
                  pl.BlockSpec((1, CM_W), lambda i: (0, 0)),
                  pl.BlockSpec((CM_GROUPS, CM_CHUNK, CM_CHUNK), lambda i: (0, 0, 0)),
                  pl.BlockSpec((CM_CHUNK, CM_GROUPS), lambda i: (0, 0))],
        out_specs=pl.BlockSpec((CM_GROUPS, tm, HEAD_DIM), lambda i: (0, i, 0)),
        out_shape=jax.ShapeDtypeStruct((CM_GROUPS, N_TOK, HEAD_DIM), BF16),
        compiler_params=_cparams(("arbitrary",)),
        name="chunk_gmlp",
    )(z, z, vnorm_g.reshape(1, CM_W), ws, bs_t)


GP_BLOCK = 256


def _gdn_gates_kernel(ab_ref, alog_ref, dt_ref, col_ref, rowt_ref):
    ab = ab_ref[...]
    lane = lax.broadcasted_iota(jnp.int32, ab.shape, 1)
    g = jnp.where(lane < 2 * HEADS, -jnp.exp(alog_ref[...]) * _softplus(ab + dt_ref[...]), 0.0)
    r = lax.broadcasted_iota(jnp.int32, (GP_BLOCK, GP_BLOCK), 0)
    c = lax.broadcasted_iota(jnp.int32, (GP_BLOCK, GP_BLOCK), 1)
    same = (r // GDN_CHUNK) == (c // GDN_CHUNK)
    tri_f = jnp.where(jnp.logical_and(same, c <= r), 1.0, 0.0).astype(BF16)
    tri_b = jnp.where(jnp.logical_and(same, c >= r), 1.0, 0.0).astype(BF16)
    cf = _dot01(tri_f, g)
    cb = _dot01(tri_b, g)
    col = jnp.where(lane < HEADS, cf, jnp.where(lane < 2 * HEADS, cb, _sigmoid(ab)))
    col_ref[...] = col
    rowt_ref[...] = col.T


def _gdn_gates_call(ab, a_log, dt_bias):
    pad = lambda t: jnp.pad(t.reshape(1, 2 * HEADS), ((0, 0), (0, LANES - 2 * HEADS)))
    return pl.pallas_call(
        _gdn_gates_kernel,
        grid=(N_TOK // GP_BLOCK,),
        in_specs=[pl.BlockSpec((GP_BLOCK, LANES), lambda i: (i, 0)),
                  pl.BlockSpec((1, LANES), lambda i: (0, 0)),
                  pl.BlockSpec((1, LANES), lambda i: (0, 0))],
        out_specs=[pl.BlockSpec((GP_BLOCK, LANES), lambda i: (i, 0)),
                   pl.BlockSpec((LANES, GP_BLOCK), lambda i: (0, i))],
        out_shape=[jax.ShapeDtypeStruct((N_TOK, LANES), F32),
                   jax.ShapeDtypeStruct((LANES, N_TOK), F32)],
        compiler_params=_cparams(("arbitrary",)),
        name="gdn_gates",
    )(ab, pad(a_log), pad(dt_bias))


CONV_PAD = 72
GDN_AQ = HEAD_DIM + GDN_CHUNK
GDN_STEP = 16 * GDN_CHUNK


def _gdn_kernel(qr_ref, kr_ref, vr_ref, og_ref, col_ref, rowt_ref, cwq_ref, cwk_ref, cwv_ref,
                g_ref, s0_ref, o_ref, s_ref,
                xp_ref, q_s, k_s, v_s, o_s, b_s, aq_s):
    is_ctx = _is_ctx_step(pl.program_id(1))
    head = pl.program_id(0)
    n_chunks = UNIT // GDN_CHUNK
    chunks_per_seq = SEQ // GDN_CHUNK

    t = lax.broadcasted_iota(jnp.int32, (UNIT, 1), 0)
    period = jnp.where(is_ctx, SEQ, GRID_W)
    pos = jnp.bitwise_and(t, period - 1)
    ok_left = pos != 0
    ok_right = pos != period - 1
    zeros_pad = jnp.zeros((CONV_PAD, HEAD_DIM), F32)
    xp_ref[0:CONV_PAD, :] = zeros_pad
    xp_ref[CONV_PAD + UNIT:CONV_PAD + UNIT + CONV_PAD, :] = zeros_pad

    def conv_silu(x_ref, w_ref):
        xp_ref[CONV_PAD:CONV_PAD + UNIT, :] = x_ref[...]
        acc = jnp.zeros((UNIT, HEAD_DIM), F32)
        for i in range(CONV_K):
            for j in range(CONV_K):
                w = w_ref[CONV_K * i + j:CONV_K * i + j + 1, :]
                if i != CONV_K // 2:
                    w = jnp.where(is_ctx, 0.0, w)
                start = CONV_PAD + (i - 1) * GRID_W + (j - 1)
                xs = xp_ref[start:start + UNIT, :]
                if j == 0:
                    xs = jnp.where(ok_left, xs, 0.0)
                elif j == CONV_K - 1:
                    xs = jnp.where(ok_right, xs, 0.0)
                acc = acc + xs * w
        return _silu(acc)

    def l2norm(x):
        return x * lax.rsqrt(jnp.sum(x * x, axis=-1, keepdims=True) + EPS)

    q_s[...] = l2norm(conv_silu(qr_ref, cwq_ref)) * (HEAD_DIM ** -0.5)
    k_s[...] = l2norm(conv_silu(kr_ref, cwk_ref))
    v_s[...] = conv_silu(vr_ref, cwv_ref)

    rr = lax.broadcasted_iota(jnp.int32, (GDN_CHUNK, GDN_CHUNK), 0)
    cc = lax.broadcasted_iota(jnp.int32, (GDN_CHUNK, GDN_CHUNK), 1)
    eye = jnp.where(rr == cc, 1.0, 0.0)
    same_blk = [(rr // b) == (cc // b) for b in (8, 16, 32, 64)]
    lane = lax.broadcasted_iota(jnp.int32, (GDN_STEP, LANES), 1)
    sub8 = lax.broadcasted_iota(jnp.int32, (HEADS, GDN_STEP), 0)

    def pick(x, j):
        return jnp.sum(jnp.where(lane[:x.shape[0]] == j, x, 0.0), axis=-1, keepdims=True)

    incl = (cc <= rr, cc >= rr)
    strict = (cc < rr, cc > rr)
    off_masks = [jnp.logical_and(same_blk[lvl], jnp.logical_not(same_blk[lvl - 1]))
                 for lvl in range(1, len(same_blk))]

    def phase1(p, carry):
        rows = pl.ds(pl.multiple_of(p * GDN_STEP, GDN_STEP), GDN_STEP)
        q2, k2, v2 = q_s[rows, :], k_s[rows, :], v_s[rows, :]
        col = col_ref[rows, :]
        gcols = [pick(col, d * HEADS + head) for d in range(2)]
        betas = [pick(col, (2 + d) * HEADS + head) for d in range(2)]
        grows = [jnp.sum(jnp.where(sub8 == head, rowt_ref[d * HEADS:(d + 1) * HEADS, rows], 0.0),
                         axis=0, keepdims=True) for d in range(2)]
        inst = []
        for half in range(GDN_STEP // GDN_CHUNK):
            sl = slice(half * GDN_CHUNK, (half + 1) * GDN_CHUNK)
            qc, kc, vc = q2[sl], k2[sl], v2[sl]
            kk = _dot1(kc, kc, _NT)
            qk = _dot1(qc, kc, _NT)
            for d in range(2):
                gcol, beta = gcols[d][sl], betas[d][sl]
                decay = jnp.where(incl[d], jnp.exp(gcol - grows[d][:, sl]), 0.0)
                m = jnp.where(strict[d], beta * kk * decay, 0.0)
                inst.append((half, d, qc, kc, vc, qk, gcol, beta, decay, m))
        ms = [t[-1] for t in inst]
        pws = [jnp.where(same_blk[0], -m, 0.0) for m in ms]
        tinvs = [eye + pw for pw in pws]
        for _ in range(2):
            pws = [_dot_inv(pw, pw) for pw in pws]
            tinvs = [t + _dot_inv(t, pw) for t, pw in zip(tinvs, pws)]
        for mask in off_masks:
            tmp = [_dot_inv(jnp.where(mask, m, 0.0), t) for m, t in zip(ms, tinvs)]
            tinvs = [t - _dot_inv(t, x) for t, x in zip(tinvs, tmp)]
        uws = [_dot1(tinv, jnp.concatenate([vc * beta, kc * (beta * jnp.exp(gcol))], axis=1)).astype(BF16)
               for (half, d, qc, kc, vc, qk, gcol, beta, decay, m), tinv in zip(inst, tinvs)]
        bas, ows = [], []
        for (half, d, qc, kc, vc, qk, gcol, beta, decay, m), uw in zip(inst, uws):
            glast = gcol[GDN_CHUNK - 1:GDN_CHUNK] if d == 0 else gcol[0:1]
            bas.append(_dg((kc * jnp.exp(glast - gcol)).astype(BF16), uw, _TN))
            ows.append(_dg((qk * decay).astype(BF16), uw, _NN))
        for (half, d, qc, kc, vc, qk, gcol, beta, decay, m), ba, ow in zip(inst, bas, ows):
            chunk0 = p * GDN_STEP + half * GDN_CHUNK
            o_s[d, pl.ds(pl.multiple_of(chunk0, GDN_CHUNK), GDN_CHUNK), :] = ow[:, :HEAD_DIM]
            b_s[d, pl.ds(pl.multiple_of(2 * chunk0, HEAD_DIM), HEAD_DIM), :] = ba[:, :HEAD_DIM]
            aq_s[d, pl.ds(pl.multiple_of(3 * chunk0, GDN_AQ), GDN_AQ), :] = jnp.concatenate(
                [ba[:, HEAD_DIM:], qc * jnp.exp(gcol) - ow[:, HEAD_DIM:]], axis=0).astype(BF16)
        return carry

    lax.fori_loop(0, UNIT // GDN_STEP, phase1, 0)

    lane1 = lax.broadcasted_iota(jnp.int32, (1, LANES), 1)

    def advance(chains, states):
        rs = [_dg(aq_s[d, pl.ds(pl.multiple_of(ch * GDN_AQ, GDN_AQ), GDN_AQ), :], s.astype(BF16), _NN)
              for (d, ch), s in zip(chains, states)]
        new = []
        for (d, ch), s, r in zip(chains, states, rs):
            tile0 = pl.multiple_of(ch * GDN_CHUNK + (GDN_CHUNK - 8 if d == 0 else 0), 8)
            last = col_ref[pl.ds(tile0, 8), :]
            last = last[7:8] if d == 0 else last[0:1]
            glast = jnp.sum(jnp.where(lane1 == d * HEADS + head, last, 0.0), axis=-1, keepdims=True)
            b = b_s[d, pl.ds(pl.multiple_of(ch * HEAD_DIM, HEAD_DIM), HEAD_DIM), :]
            new.append(s * jnp.exp(glast) + (b - r[:HEAD_DIM]))
            rows = pl.ds(pl.multiple_of(ch * GDN_CHUNK, GDN_CHUNK), GDN_CHUNK)
            o_s[d, rows, :] = o_s[d, rows, :] + r[HEAD_DIM:]
        return tuple(new)

    @pl.when(is_ctx)
    def _context_unit():
        def body(i, states):
            chains = [(d, sq * chunks_per_seq + (i if d == 0 else chunks_per_seq - 1 - i))
                      for sq in range(SEQ_PER_UNIT) for d in range(2)]
            return advance(chains, states)
        zero = jnp.zeros((HEAD_DIM, HEAD_DIM), F32)
        final = lax.fori_loop(0, chunks_per_seq, body, (zero,) * (2 * SEQ_PER_UNIT))
        for sq in range(SEQ_PER_UNIT):
            for d in range(2):
                s_ref[sq, d] = final[2 * sq + d]

    @pl.when(jnp.logical_not(is_ctx))
    def _latent_unit():
        def body(i, states):
            return advance([(0, i), (1, n_chunks - 1 - i)], states)
        final = lax.fori_loop(0, n_chunks, body, (s0_ref[0], s0_ref[1]))
        for sq in range(SEQ_PER_UNIT):
            for d in range(2):
                s_ref[sq, d] = final[d]

    o_ref[...] = _head_rmsnorm_gate(o_s[0] + o_s[1], g_ref[...], og_ref[...]).astype(o_ref.dtype)


def _gdn_call(z, col, rowt, conv_w, onorm_g, state, layer, new_states):
    def zcol(off):
        return pl.BlockSpec((None, UNIT, HEAD_DIM), lambda h, g: (off // HEAD_DIM + h, _scan_unit(g), 0))
    def wcol(part):
        return pl.BlockSpec((CONV_K * CONV_K, HEAD_DIM), lambda h, g: (0, part * HEADS + h))
    scr = lambda *shape: pltpu.VMEM(shape, F32)
    in_specs = [zcol(OFF_GQ), zcol(OFF_GK), zcol(OFF_GV), zcol(OFF_GG),
                pl.BlockSpec((UNIT, LANES), lambda h, g: (_scan_unit(g), 0)),
                pl.BlockSpec((LANES, UNIT), lambda h, g: (0, _scan_unit(g))),
                wcol(0), wcol(1), wcol(2),
                pl.BlockSpec((1, HEAD_DIM), lambda h, g: (0, 0)),
                _state_in_spec(layer)]
    args = [z, z, z, z, col, rowt, conv_w, conv_w, conv_w, onorm_g.reshape(1, HEAD_DIM), state]
    kernel, aliases = _gdn_kernel, {}
    if new_states is not None:
        kernel, aliases = _without_arg(_gdn_kernel, len(args)), {len(args): 1}
        in_specs.append(pl.BlockSpec(memory_space=pl.ANY))
        args.append(new_states)
    return pl.pallas_call(
        kernel,
        grid=(HEADS, N_UNITS),
        in_specs=in_specs,
        out_specs=[pl.BlockSpec((None, UNIT, HEAD_DIM), lambda h, g: (h, _scan_unit(g), 0)), _state_out_spec(layer)],
        out_shape=[jax.ShapeDtypeStruct((HEADS, N_TOK, HEAD_DIM), BF16), _STATE_SHAPE],
        input_output_aliases=aliases,
        scratch_shapes=[scr(UNIT + 2 * CONV_PAD, HEAD_DIM),
                        scr(UNIT, HEAD_DIM), scr(UNIT, HEAD_DIM), scr(UNIT, HEAD_DIM),
                        scr(2, UNIT, HEAD_DIM),
                        scr(2, (UNIT // GDN_CHUNK) * HEAD_DIM, HEAD_DIM),
                        pltpu.VMEM((2, (UNIT // GDN_CHUNK) * GDN_AQ, HEAD_DIM), BF16)],
        compiler_params=_cparams(("arbitrary", "arbitrary")),
        name="gdn_scan",
    )(*args)


def kernel(x_prompt, x_sample, c, state_hgrn, state_gdn, c_ctx, norm1_g, norm2_g, w_mod, b_mod, w_in, hg_lb, hg_onorm_g, cm_vnorm_g, cm_ws, cm_bs, gdn_conv, gdn_A_log, gdn_dt_bias, gdn_onorm_g, w_br_hg, w_br_cm, w_br_gdn, w_out, w_ff1, w_ff2, final_g):
    x = (x_prompt.reshape(N_CTX_TOK, D_MODEL), x_sample.reshape(N_LAT_TOK, D_MODEL))
    cvec = jnp.concatenate([c_ctx[None, :], c, jnp.zeros((MOD_ROWS - 1 - DEC_BATCH, D_MODEL), F32)], axis=0)
    mod3 = _mod_call(cvec, w_mod, b_mod).reshape(DEPTH * MOD_ROWS, 1, 6 * D_MODEL)

    lb_all = jnp.cumsum(jax.nn.softmax(hg_lb.astype(F32), axis=0), axis=0)
    lb_all = lb_all - lb_all[:1]

    w_in_t = jnp.swapaxes(w_in, 1, 2)
    w_ff2_bf = w_ff2.astype(BF16)

    new_hg = new_gdn = None
    h = _normmod_call(x, norm1_g[0], mod3, 0, 0, 1)
    for l in range(DEPTH):
        z = _mm_call(h, w_in_t, l, col0=0, n=N_MAIN, act=None, out_dtype=F32, tm=1024, tn=1024,
                     w_transposed=True, lane_tile_major=True, name="in_proj")
        ab = _mm_call(h, w_in_t, l, col0=OFF_AB, n=LANES, act=None, out_dtype=F32, tm=1024, tn=LANES,
                      w_transposed=True, name="in_proj_ab")
        gates = _mm_call(h, w_in_t, l, col0=OFF_GATES, n=3 * D_MODEL, act="sigmoid", out_dtype=F32,
                         tm=1024, tn=1024, w_transposed=True, name="in_proj_gates")

        o_a, new_hg = _hgrn_call(z, lb_all[l], hg_onorm_g[l], state_hgrn, l, new_hg)
        o_b = _gmlp_call(z, cm_vnorm_g[l], cm_ws[l], cm_bs[l].T)
        col, rowt = _gdn_gates_call(ab, gdn_A_log[l], gdn_dt_bias[l])
        o_c, new_gdn = _gdn_call(z, col, rowt, gdn_conv[l].reshape(CONV_K * CONV_K, 3 * HG_F),
                                 gdn_onorm_g[l], state_gdn, l, new_gdn)

        merged = _merge_call(o_a, o_b, o_c, w_br_hg, w_br_cm, w_br_gdn, gates, l)
        x, h2 = _outproj_norm_call(merged, w_out, x, norm2_g[l], mod3, l)
        up = _mm_call(h2, w_ff1, l, col0=0, n=D_FF, act="relu2", out_dtype=BF16, tm=1024, tn=1024, name="ffn_up")
        if l + 1 < DEPTH:
            x, h = _ffn_down_norm_call(up, w_ff2_bf, x, mod3, l, norm1_g[l + 1])
        else:
            x = _mm_resid_call(up, w_ff2_bf, x, mod3, l, 5, tm=512, tn=512, name="ffn_down")

    y_prompt = _final_norm_call(x, final_g, 0, N_CTX_TOK).reshape(BATCH, SEQ, D_MODEL)
    y_sample = _final_norm_call(x, final_g, N_CTX_TOK, N_LAT_TOK).reshape(DEC_BATCH, DEC_SEQ, D_MODEL)
    return (y_prompt, y_sample, new_hg, new_gdn)
```

```python
import functools

import jax
import jax.numpy as jnp
from jax import lax
from jax.experimental import pallas as pl
from jax.experimental.pallas import tpu as pltpu

F32 = jnp.float32
BF16 = jnp.bfloat16

D_MODEL = 2048
BATCH = 16
SEQ = 256
DEPTH = 2
DEC_BATCH = 4
DEC_SEQ = 1024
GRID_W = 64
EPS = 1e-6
D_FF = 4 * D_MODEL
HEADS = 8
HEAD_DIM = 128
HG_F = HEADS * HEAD_DIM
CM_GROUPS = 8
CM_W = CM_GROUPS * HEAD_DIM
CM_CHUNK = 128
GDN_CHUNK = 64
CONV_K = 3

OFF_HQ, OFF_HI, OFF_HG, OFF_HFF, OFF_HFB = 0, 1024, 2048, 3072, 4096
OFF_CU, OFF_CV = 5120, 6144
OFF_GQ, OFF_GK, OFF_GV, OFF_GG = 7168, 8192, 9216, 10240
OFF_AB = 11264
OFF_GATES = 11296
IN_DIM = 17440
N_MAIN = OFF_AB

N_CTX_TOK = BATCH * SEQ
N_LAT_TOK = DEC_BATCH * DEC_SEQ
N_TOK = N_CTX_TOK + N_LAT_TOK
UNIT = DEC_SEQ
N_CTX_UNITS = N_CTX_TOK // UNIT
N_UNITS = N_TOK // UNIT
N_LAT_UNITS = N_UNITS - N_CTX_UNITS
SEQ_PER_UNIT = UNIT // SEQ

LANES = 128
MOD_ROWS = 8
HG_BLOCK = 256
HG_SUB = 32
HG_INST = 2 * (UNIT // HG_BLOCK)
VMEM_LIMIT = 56 * 1024 * 1024

_NT = (((1,), (1,)), ((), ()))
_TN = (((0,), (0,)), ((), ()))
_NN = (((1,), (0,)), ((), ()))


def _dg(a, b, dims):
    return lax.dot_general(a, b, dims, preferred_element_type=F32)


def _split2(x):
    hi = x.astype(BF16)
    lo = (x - hi.astype(F32)).astype(BF16)
    return hi, lo


def _dot1(a, b, dims=_NN):
    return _dg(a.astype(BF16), b.astype(BF16), dims)


def _dot3(a, b, dims=_NN):
    ah, al = _split2(a)
    bh, bl = _split2(b)
    return _dg(ah, bh, dims) + (_dg(ah, bl, dims) + _dg(al, bh, dims))


_dot_inv = _dot1


def _dot01(m, x):
    hi = x.astype(BF16)
    r = x - hi.astype(F32)
    mid = r.astype(BF16)
    lo = (r - mid.astype(F32)).astype(BF16)
    return _dg(m, hi, _NN) + (_dg(m, mid, _NN) + _dg(m, lo, _NN))


def _dot01_2(m, x):
    hi, lo = _split2(x)
    return _dg(m, hi, _NN) + _dg(m, lo, _NN)


def _sigmoid(x):
    return 1.0 / (1.0 + jnp.exp(-x))


def _silu(x):
    return x * _sigmoid(x)


def _gelu(x):
    return 0.5 * x * (1.0 + lax.erf(x * (2.0 ** -0.5)))


def _softplus(x):
    return jnp.maximum(x, 0.0) + jnp.log1p(jnp.exp(-jnp.abs(x)))


def _mod_row_of_tile(i, tm):
    return jnp.maximum(0, (i * tm - N_CTX_TOK) // DEC_SEQ + 1)


assert N_LAT_UNITS == N_CTX_UNITS


def _scan_unit(g):
    return jnp.where(g % 2 == 0, N_CTX_UNITS + g // 2, g // 2)


def _is_ctx_step(g):
    return g % 2 == 1


def _state_out_spec(layer):
    return pl.BlockSpec((SEQ_PER_UNIT, None, 2, None, HEAD_DIM, HEAD_DIM),
                        lambda h, g: (g // 2, layer, 0, h, 0, 0))


def _state_in_spec(layer):
    return pl.BlockSpec((None, None, 2, None, HEAD_DIM, HEAD_DIM),
                        lambda h, g: (g // 2, layer, 0, h, 0, 0))


_STATE_SHAPE = jax.ShapeDtypeStruct((BATCH, DEPTH, 2, HEADS, HEAD_DIM, HEAD_DIM), F32)


def _without_arg(kernel, pos):
    def wrapped(*refs):
        return kernel(*refs[:pos], *refs[pos + 1:])
    return wrapped


def _cparams(sem):
    return pltpu.CompilerParams(dimension_semantics=sem, vmem_limit_bytes=VMEM_LIMIT)


def _mod_kernel(c_ref, w_ref, b_ref, o_ref):
    s = _silu(c_ref[...])
    hi, lo = _split2(s)
    w = w_ref[...].astype(BF16)
    o_ref[...] = _dg(hi, w, _NN) + _dg(lo, w, _NN) + b_ref[...]


def _mod_call(cvec, w_mod, b_mod):
    tn = 1024
    n = 6 * D_MODEL
    return pl.pallas_call(
        _mod_kernel,
        grid=(DEPTH, n // tn),
        in_specs=[pl.BlockSpec((MOD_ROWS, D_MODEL), lambda l, j: (0, 0)),
                  pl.BlockSpec((None, D_MODEL, tn), lambda l, j: (l, 0, j)),
                  pl.BlockSpec((None, 1, tn), lambda l, j: (l, 0, j))],
        out_specs=pl.BlockSpec((None, MOD_ROWS, tn), lambda l, j: (l, 0, j)),
        out_shape=jax.ShapeDtypeStruct((DEPTH, MOD_ROWS, n), F32),
        compiler_params=_cparams(("arbitrary", "arbitrary")),
        name="modulation",
    )(cvec, w_mod, b_mod.reshape(DEPTH, 1, n))


def _adaln(x, g, shift, scale):
    y = x * lax.rsqrt(jnp.mean(x * x, axis=-1, keepdims=True) + EPS) * g
    return y * (1.0 + scale) + shift


def _residual_specs(x, tm, idx):
    if not isinstance(x, tuple):
        return [pl.BlockSpec((tm, D_MODEL), lambda *g: (idx(*g), 0))], [x]
    n_ctx = N_CTX_TOK // tm
    return ([pl.BlockSpec((tm, D_MODEL), lambda *g: (jnp.minimum(idx(*g), n_ctx - 1), 0)),
             pl.BlockSpec((tm, D_MODEL), lambda *g: (jnp.maximum(idx(*g) - n_ctx, 0), 0))], list(x))


def _residual_tile(refs, tile, tm):
    if len(refs) == 1:
        return refs[0][...]
    return jnp.where(tile < N_CTX_TOK // tm, refs[0][...], refs[1][...])


def _normmod_kernel(*refs, n_x):
    x_refs, (g_ref, sh_ref, sc_ref, o_ref) = refs[:n_x], refs[n_x:]
    x = _residual_tile(x_refs, pl.program_id(0), o_ref.shape[0])
    o_ref[...] = _adaln(x, g_ref[...], sh_ref[...], sc_ref[...]).astype(o_ref.dtype)


def _normmod_call(x, g, mod3, layer, k_shift, k_scale):
    tm = 512
    def mod_spec(k):
        return pl.BlockSpec((None, 1, D_MODEL),
                            lambda i: (layer * MOD_ROWS + _mod_row_of_tile(i, tm), 0, k))
    x_specs, x_args = _residual_specs(x, tm, lambda i: i)
    return pl.pallas_call(
        functools.partial(_normmod_kernel, n_x=len(x_args)),
        grid=(N_TOK // tm,),
        in_specs=x_specs + [pl.BlockSpec((1, D_MODEL), lambda i: (0, 0)), mod_spec(k_shift), mod_spec(k_scale)],
        out_specs=pl.BlockSpec((tm, D_MODEL), lambda i: (i, 0)),
        out_shape=jax.ShapeDtypeStruct((N_TOK, D_MODEL), BF16),
        compiler_params=_cparams(("arbitrary",)),
        name="norm_mod",
    )(*x_args, g.reshape(1, D_MODEL), mod3, mod3)


def _final_norm_kernel(x_ref, g_ref, o_ref):
    x = x_ref[...]
    o_ref[...] = x * lax.rsqrt(jnp.mean(x * x, axis=-1, keepdims=True) + EPS) * g_ref[...]


def _final_norm_call(x, g, row0, n_rows):
    tm = 512
    return pl.pallas_call(
        _final_norm_kernel,
        grid=(n_rows // tm,),
        in_specs=[pl.BlockSpec((tm, D_MODEL), lambda i: (row0 // tm + i, 0)),
                  pl.BlockSpec((1, D_MODEL), lambda i: (0, 0))],
        out_specs=pl.BlockSpec((tm, D_MODEL), lambda i: (i, 0)),
        out_shape=jax.ShapeDtypeStruct((n_rows, D_MODEL), F32),
        compiler_params=_cparams(("arbitrary",)),
        name="final_norm",
    )(x, g.reshape(1, D_MODEL))


def _mm_kernel(x_ref, w_ref, o_ref, wbf_ref, *, act, w_transposed):
    @pl.when(pl.program_id(1) == 0)
    def _cast_weights():
        wbf_ref[...] = w_ref[...].reshape(wbf_ref.shape).astype(BF16)

    acc = lax.dot_general(x_ref[...], wbf_ref[...], _NT if w_transposed else _NN, preferred_element_type=F32)
    if act == "sigmoid":
        acc = _sigmoid(acc)
    elif act == "relu2":
        acc = jnp.square(jnp.maximum(acc, 0.0))
    if len(o_ref.shape) == 3:
        for c in range(o_ref.shape[0]):
            o_ref[c] = acc[:, c * LANES:(c + 1) * LANES].astype(o_ref.dtype)
    else:
        o_ref[...] = acc.astype(o_ref.dtype)


def _mm_call(x, w_all, layer, *, col0, n, act, out_dtype, tm, tn, name, w_transposed=False,
             lane_tile_major=False):
    m, k = x.shape
    assert n % tn == 0 and m % tm == 0
    if w_transposed:
        assert col0 % 8 == 0
        w_spec = pl.BlockSpec((pl.Element(1), pl.Element(tn), pl.Element(k)),
                              lambda j, i: (layer, pl.multiple_of(col0 + j * tn, 8), 0))
        w_scratch = pltpu.VMEM((tn, k), BF16)
    else:
        assert col0 % tn == 0
        w_spec = pl.BlockSpec((None, k, tn), lambda j, i: (layer, 0, col0 // tn + j))
        w_scratch = pltpu.VMEM((k, tn), BF16)
    if lane_tile_major:
        out_spec = pl.BlockSpec((tn // LANES, tm, LANES), lambda j, i: (j, i, 0))
        out_shape = jax.ShapeDtypeStruct((n // LANES, m, LANES), out_dtype)
    else:
        out_spec = pl.BlockSpec((tm, tn), lambda j, i: (i, j))
        out_shape = jax.ShapeDtypeStruct((m, n), out_dtype)
    return pl.pallas_call(
        functools.partial(_mm_kernel, act=act, w_transposed=w_transposed),
        grid=(n // tn, m // tm),
        in_specs=[pl.BlockSpec((tm, k), lambda j, i: (i, 0)), w_spec],
        out_specs=out_spec,
        out_shape=out_shape,
        scratch_shapes=[w_scratch],
        compiler_params=_cparams(("arbitrary", "arbitrary")),
        name=name,
    )(x, w_all)


def _mm_bf16_kernel(x_ref, w_ref, r_ref, g_ref, o_ref):
    acc = jnp.dot(x_ref[...], w_ref[...], preferred_element_type=F32)
    o_ref[...] = r_ref[...] + g_ref[...] * acc


def _mm_resid_call(x, w, resid, mod3, layer, k_gate, *, tm, tn, name):
    m, k = x.shape
    n = w.shape[2]
    per_tile = D_MODEL // tn
    return pl.pallas_call(
        _mm_bf16_kernel,
        grid=(m // tm, n // tn),
        in_specs=[pl.BlockSpec((tm, k), lambda i, j: (i, 0)),
                  pl.BlockSpec((None, k, tn), lambda i, j: (layer, 0, j)),
                  pl.BlockSpec((tm, tn), lambda i, j: (i, j)),
                  pl.BlockSpec((None, 1, tn),
                               lambda i, j: (layer * MOD_ROWS + _mod_row_of_tile(i, tm), 0,
                                             k_gate * per_tile + j))],
        out_specs=pl.BlockSpec((tm, tn), lambda i, j: (i, j)),
        out_shape=jax.ShapeDtypeStruct((m, n), F32),
        compiler_params=_cparams(("arbitrary", "arbitrary")),
        name=name,
    )(x, w, resid, mod3)


def _outproj_norm_kernel(*refs, n_x):
    m_ref, w_ref = refs[:2]
    x_refs = refs[2:2 + n_x]
    gate_ref, g_ref, sh_ref, sc_ref, xo_ref, h_ref, wbf_ref = refs[2 + n_x:]

    @pl.when(pl.program_id(0) == 0)
    def _cast_weights():
        wbf_ref[...] = w_ref[...].astype(BF16)

    acc = jnp.dot(m_ref[...], wbf_ref[...], preferred_element_type=F32)
    xn = _residual_tile(x_refs, pl.program_id(0), xo_ref.shape[0]) + gate_ref[...] * acc
    xo_ref[...] = xn
    h_ref[...] = _adaln(xn, g_ref[...], sh_ref[...], sc_ref[...]).astype(h_ref.dtype)


def _outproj_norm_call(merged, w_out, x, norm_g, mod3, layer):
    tm = 256 if isinstance(x, tuple) else 512
    def mod_spec(k):
        return pl.BlockSpec((None, 1, D_MODEL),
                            lambda i: (layer * MOD_ROWS + _mod_row_of_tile(i, tm), 0, k))
    row = pl.BlockSpec((tm, D_MODEL), lambda i: (i, 0))
    x_specs, x_args = _residual_specs(x, tm, lambda i: i)
    return pl.pallas_call(
        functools.partial(_outproj_norm_kernel, n_x=len(x_args)),
        grid=(N_TOK // tm,),
        in_specs=[row,
                  pl.BlockSpec((None, D_MODEL, D_MODEL), lambda i: (layer, 0, 0), pipeline_mode=pl.Buffered(1))]
                 + x_specs + [mod_spec(2), pl.BlockSpec((1, D_MODEL), lambda i: (0, 0)), mod_spec(3), mod_spec(4)],
        out_specs=[row, row],
        out_shape=[jax.ShapeDtypeStruct((N_TOK, D_MODEL), F32),
                   jax.ShapeDtypeStruct((N_TOK, D_MODEL), BF16)],
        scratch_shapes=[pltpu.VMEM((D_MODEL, D_MODEL), BF16)],
        compiler_params=_cparams(("arbitrary",)),
        name="out_proj_norm",
    )(merged, w_out, *x_args, mod3, norm_g.reshape(1, D_MODEL), mod3, mod3)


def _ffn_down_norm_kernel(x_ref, w_ref, r_ref, gate_ref, g_ref, sh_ref, sc_ref, o_ref, h_ref, row_ref):
    j = pl.program_id(1)
    tn = o_ref.shape[1]
    acc = jnp.dot(x_ref[...], w_ref[...], preferred_element_type=F32)
    xn = r_ref[...] + gate_ref[...] * acc
    o_ref[...] = xn
    row_ref[:, pl.ds(pl.multiple_of(j * tn, tn), tn)] = xn

    @pl.when(j == pl.num_programs(1) - 1)
    def _next_layer_norm():
        h_ref[...] = _adaln(row_ref[...], g_ref[...], sh_ref[...], sc_ref[...]).astype(h_ref.dtype)


def _ffn_down_norm_call(up, w_bf, resid, mod3, layer, next_norm_g):
    tm, tn = 512, 512
    m, k = up.shape
    per_tile = D_MODEL // tn
    def mod_row(i):
        return _mod_row_of_tile(i, tm)
    def next_mod(kk):
        return pl.BlockSpec((None, 1, D_MODEL), lambda i, j: ((layer + 1) * MOD_ROWS + mod_row(i), 0, kk))
    return pl.pallas_call(
        _ffn_down_norm_kernel,
        grid=(m // tm, D_MODEL // tn),
        in_specs=[pl.BlockSpec((tm, k), lambda i, j: (i, 0)),
                  pl.BlockSpec((None, k, tn), lambda i, j: (layer, 0, j)),
                  pl.BlockSpec((tm, tn), lambda i, j: (i, j)),
                  pl.BlockSpec((None, 1, tn), lambda i, j: (layer * MOD_ROWS + mod_row(i), 0, 5 * per_tile + j)),
                  pl.BlockSpec((1, D_MODEL), lambda i, j: (0, 0)), next_mod(0), next_mod(1)],
        out_specs=[pl.BlockSpec((tm, tn), lambda i, j: (i, j)),
                   pl.BlockSpec((tm, D_MODEL), lambda i, j: (i, 0))],
        out_shape=[jax.ShapeDtypeStruct((m, D_MODEL), F32),
                   jax.ShapeDtypeStruct((m, D_MODEL), BF16)],
        scratch_shapes=[pltpu.VMEM((tm, D_MODEL), F32)],
        compiler_params=_cparams(("arbitrary", "arbitrary")),
        name="ffn_down_norm",
    )(up, w_bf, resid, mod3, next_norm_g.reshape(1, D_MODEL), mod3, mod3)


def _gated_merge_kernel(h_ref, wga_ref, wgb_ref, wgc_ref, oa_ref, ob_ref, oc_ref, wa_ref, wb_ref, wc_ref,
                        o_ref, wg_bf, wbr_bf):
    @pl.when(pl.program_id(1) == 0)
    def _cast_weights():
        for t, ref in enumerate((wga_ref, wgb_ref, wgc_ref)):
            wg_bf[t] = ref[...].reshape(wg_bf.shape[1:]).astype(BF16)
        for t, ref in enumerate((wa_ref, wb_ref, wc_ref)):
            wbr_bf[t] = ref[...].astype(BF16)

    def rows_of(ref):
        return jnp.concatenate([ref[t] for t in range(ref.shape[0])], axis=1)

    h = h_ref[...]
    acc = None
    for t, br_ref in enumerate((oa_ref, ob_ref, oc_ref)):
        gate = _sigmoid(lax.dot_general(h, wg_bf[t], _NT, preferred_element_type=F32))
        term = gate * jnp.dot(rows_of(br_ref), wbr_bf[t], preferred_element_type=F32)
        acc = term if acc is None else acc + term
    o_ref[...] = acc.astype(o_ref.dtype)


def _gated_merge_call(h, w_in_t, o_a, o_b, o_c, w_a, w_b, w_c, layer):
    tm, tn = 1024, 256
    def gate_w(b):
        return pl.BlockSpec((pl.Element(1), pl.Element(tn), pl.Element(D_MODEL)),
                            lambda j, i: (layer, pl.multiple_of(OFF_GATES + b * D_MODEL + j * tn, 8), 0))
    br = pl.BlockSpec((HEADS, tm, HEAD_DIM), lambda j, i: (0, i, 0))
    wt = pl.BlockSpec((None, HG_F, tn), lambda j, i: (layer, 0, j))
    return pl.pallas_call(
        _gated_merge_kernel,
        grid=(D_MODEL // tn, N_TOK // tm),
        in_specs=[pl.BlockSpec((tm, D_MODEL), lambda j, i: (i, 0)), gate_w(0), gate_w(1), gate_w(2),
                  br, br, br, wt, wt, wt],
        out_specs=pl.BlockSpec((tm, tn), lambda j, i: (i, j)),
        out_shape=jax.ShapeDtypeStruct((N_TOK, D_MODEL), BF16),
        scratch_shapes=[pltpu.VMEM((3, tn, D_MODEL), BF16), pltpu.VMEM((3, HG_F, tn), BF16)],
        compiler_params=_cparams(("arbitrary", "arbitrary")),
        name="gated_merge",
    )(h, w_in_t, w_in_t, w_in_t, o_a, o_b, o_c, w_a, w_b, w_c)


def _head_rmsnorm_gate(o, g, og):
    y = o * lax.rsqrt(jnp.mean(o * o, axis=-1, keepdims=True) + EPS) * g
    return y * _silu(og)


def _hgrn_kernel(q_ref, i_ref, og_ref, ff_ref, fb_ref, lb_ref, g_ref, s0_ref, o_ref, s_ref,
                 acc_ref, b_ref, stc_ref, km_ref):
    is_ctx = _is_ctx_step(pl.program_id(1))
    nblk = UNIT // HG_BLOCK
    nsub = HG_BLOCK // HG_SUB
    r = lax.broadcasted_iota(jnp.int32, (HG_BLOCK, HG_BLOCK), 0)
    c = lax.broadcasted_iota(jnp.int32, (HG_BLOCK, HG_BLOCK), 1)
    same = (r // HG_SUB) == (c // HG_SUB)
    row_sub = lax.broadcasted_iota(jnp.int32, (HG_BLOCK, HEAD_DIM), 0) // HG_SUB
    tri = (jnp.logical_and(same, c <= r), jnp.logical_and(same, c >= r))
    tri_bf = [jnp.where(t, 1.0, 0.0).astype(BF16) for t in tri]
    z_refs = (ff_ref, fb_ref)
    inst = [(blk, d) for blk in range(nblk) for d in range(2)]

    def expand(x):
        return jnp.broadcast_to(x[:, None, :], (nsub, HG_SUB, HEAD_DIM)).reshape(HG_BLOCK, HEAD_DIM)

    def blk_rows(blk):
        return slice(blk * HG_BLOCK, (blk + 1) * HG_BLOCK)

    qs = [_silu(q_ref[blk_rows(blk), :]) for blk in range(nblk)]
    vs = [i_ref[blk_rows(blk), :] for blk in range(nblk)]
    ks, lfs = [], []
    for blk, d in inst:
        lb = lb_ref[d:d + 1, :]
        f = lb + (1.0 - lb) * _sigmoid(z_refs[d][blk_rows(blk), :])
        lfs.append(jnp.log(f))
        ks.append(1.0 - f)
    bs = [_dot01_2(tri_bf[d], lf) for (blk, d), lf in zip(inst, lfs)]
    tots, qts, kts, qds = [], [], [], []
    for n, (blk, d) in enumerate(inst):
        b = bs[n]
        b_ref[n] = b
        tot = b_ref[n, pl.ds((HG_SUB - 1) if d == 0 else 0, nsub, stride=HG_SUB), :]
        mid_f = expand(b_ref[n, pl.ds(HG_SUB // 2, nsub, stride=HG_SUB), :])
        tots.append(tot)
        qts.append(qs[blk] * jnp.exp(b - mid_f))
        kts.append(ks[n] * jnp.exp(mid_f - b))
        qds.append((qs[blk] * jnp.exp(b)).astype(BF16))
        kd = ks[n] * jnp.exp(expand(tot) - b)
        for s in range(nsub):
            km_ref[n, :, s * HEAD_DIM:(s + 1) * HEAD_DIM] = jnp.where(row_sub == s, kd, 0.0).astype(BF16)
    scs = [jnp.where(tri[d], _dot1(qt, kt, _NT), 0.0) for (blk, d), qt, kt in zip(inst, qts, kts)]
    uts = [_dg(vs[blk].astype(BF16), km_ref[n], _TN) for n, (blk, d) in enumerate(inst)]
    outs = [_dot1(sc, vs[blk]) for (blk, d), sc in zip(inst, scs)]
    for d in range(2):
        st = s0_ref[d].T
        for blk in (range(nblk) if d == 0 else range(nblk - 1, -1, -1)):
            n = inst.index((blk, d))
            st = jnp.where(is_ctx, 0.0, st)
            for s in (range(nsub) if d == 0 else range(nsub - 1, -1, -1)):
                stc_ref[n, s] = st.astype(BF16)
                st = st * jnp.exp(tots[n][s:s + 1, :]) + uts[n][:, s * HEAD_DIM:(s + 1) * HEAD_DIM]
            s_ref[blk, d] = st.T
    for n, (blk, d) in enumerate(inst):
        o_int = [_dg(qds[n][s * HG_SUB:(s + 1) * HG_SUB], stc_ref[n, s], _NT) for s in range(nsub)]
        acc_ref[d, blk_rows(blk), :] = outs[n] + jnp.concatenate(o_int, axis=0)
    o_ref[...] = _head_rmsnorm_gate(acc_ref[0] + acc_ref[1], g_ref[...], og_ref[...]).astype(o_ref.dtype)


def _hgrn_call(z, lb, onorm_g, state, layer, new_states):
    def col(off):
        return pl.BlockSpec((None, UNIT, HEAD_DIM), lambda h, g: (off // HEAD_DIM + h, _scan_unit(g), 0))
    in_specs = [col(OFF_HQ), col(OFF_HI), col(OFF_HG), col(OFF_HFF), col(OFF_HFB),
                pl.BlockSpec((2, HEAD_DIM), lambda h, g: (0, h)),
                pl.BlockSpec((1, HEAD_DIM), lambda h, g: (0, 0)),
                _state_in_spec(layer)]
    args = [z, z, z, z, z, lb, onorm_g.reshape(1, HEAD_DIM), state]
    kernel, aliases = _hgrn_kernel, {}
    if new_states is not None:
        kernel, aliases = _without_arg(_hgrn_kernel, len(args)), {len(args): 1}
        in_specs.append(pl.BlockSpec(memory_space=pl.ANY))
        args.append(new_states)
    return pl.pallas_call(
        kernel,
        grid=(HEADS, N_UNITS),
        in_specs=in_specs,
        out_specs=[pl.BlockSpec((None, UNIT, HEAD_DIM), lambda h, g: (h, _scan_unit(g), 0)), _state_out_spec(layer)],
        out_shape=[jax.ShapeDtypeStruct((HEADS, N_TOK, HEAD_DIM), BF16), _STATE_SHAPE],
        input_output_aliases=aliases,
        scratch_shapes=[pltpu.VMEM((2, UNIT, HEAD_DIM), F32),
                        pltpu.VMEM((HG_INST, HG_BLOCK, HEAD_DIM), F32),
                        pltpu.VMEM((HG_INST, HG_BLOCK // HG_SUB, HEAD_DIM, HEAD_DIM), BF16),
                        pltpu.VMEM((HG_INST, HG_BLOCK, (HG_BLOCK // HG_SUB) * HEAD_DIM), BF16)],
        compiler_params=_cparams(("arbitrary", "arbitrary")),
        name="hgrn2_scan",
    )(*args)


def _gmlp_kernel(u_ref, v_ref, vn_ref, ws_ref, bs_ref, o_ref):
    tm = u_ref.shape[1]
    for g in range(CM_GROUPS):
        cols = slice(g * HEAD_DIM, (g + 1) * HEAD_DIM)
        vg = _gelu(v_ref[g])
        vg = vg * lax.rsqrt(jnp.mean(vg * vg, axis=-1, keepdims=True) + EPS) * vn_ref[:, cols]
        ug = _gelu(u_ref[g])
        w = ws_ref[g]
        bias = bs_ref[:, g:g + 1]
        for ch in range(tm // CM_CHUNK):
            rows = slice(ch * CM_CHUNK, (ch + 1) * CM_CHUNK)
            s = _dot3(w, vg[rows]) + bias
            o_ref[g, rows, :] = (ug[rows] * s).astype(o_ref.dtype)


def _gmlp_call(z, vnorm_g, ws, bs_t):
    tm = 512
    return pl.pallas_call(
        _gmlp_kernel,
        grid=(N_TOK // tm,),
        in_specs=[pl.BlockSpec((CM_GROUPS, tm, HEAD_DIM), lambda i: (OFF_CU // CM_W, i, 0)),
                  pl.BlockSpec((CM_GROUPS, tm, HEAD_DIM), lambda i: (OFF_CV // CM_W, i, 0)),
                  pl.BlockSpec((1, CM_W), lambda i: (0, 0)),
                  pl.BlockSpec((CM_GROUPS, CM_CHUNK, CM_CHUNK), lambda i: (0, 0, 0)),
                  pl.BlockSpec((CM_CHUNK, CM_GROUPS), lambda i: (0, 0))],
        out_specs=pl.BlockSpec((CM_GROUPS, tm, HEAD_DIM), lambda i: (0, i, 0)),
        out_shape=jax.ShapeDtypeStruct((CM_GROUPS, N_TOK, HEAD_DIM), BF16),
        compiler_params=_cparams(("arbitrary",)),
        name="chunk_gmlp",
    )(z, z, vnorm_g.reshape(1, CM_W), ws, bs_t)


GP_BLOCK = 256


def _gdn_gates_kernel(ab_ref, alog_ref, dt_ref, col_ref, rowt_ref):
    ab = ab_ref[...]
    lane = lax.broadcasted_iota(jnp.int32, ab.shape, 1)
    g = jnp.where(lane < 2 * HEADS, -jnp.exp(alog_ref[...]) * _softplus(ab + dt_ref[...]), 0.0)
    r = lax.broadcasted_iota(jnp.int32, (GP_BLOCK, GP_BLOCK), 0)
    c = lax.broadcasted_iota(jnp.int32, (GP_BLOCK, GP_BLOCK), 1)
    same = (r // GDN_CHUNK) == (c // GDN_CHUNK)
    tri_f = jnp.where(jnp.logical_and(same, c <= r), 1.0, 0.0).astype(BF16)
    tri_b = jnp.where(jnp.logical_and(same, c >= r), 1.0, 0.0).astype(BF16)
    cf = _dot01(tri_f, g)
    cb = _dot01(tri_b, g)
    col = jnp.where(lane < HEADS, cf, jnp.where(lane < 2 * HEADS, cb, _sigmoid(ab)))
    col_ref[...] = col
    rowt_ref[...] = col.T


def _gdn_gates_call(ab, a_log, dt_bias):
    pad = lambda t: jnp.pad(t.reshape(1, 2 * HEADS), ((0, 0), (0, LANES - 2 * HEADS)))
    return pl.pallas_call(
        _gdn_gates_kernel,
        grid=(N_TOK // GP_BLOCK,),
        in_specs=[pl.BlockSpec((GP_BLOCK, LANES), lambda i: (i, 0)),
                  pl.BlockSpec((1, LANES), lambda i: (0, 0)),
                  pl.BlockSpec((1, LANES), lambda i: (0, 0))],
        out_specs=[pl.BlockSpec((GP_BLOCK, LANES), lambda i: (i, 0)),
                   pl.BlockSpec((LANES, GP_BLOCK), lambda i: (0, i))],
        out_shape=[jax.ShapeDtypeStruct((N_TOK, LANES), F32),
                   jax.ShapeDtypeStruct((LANES, N_TOK), F32)],
        compiler_params=_cparams(("arbitrary",)),
        name="gdn_gates",
    )(ab, pad(a_log), pad(dt_bias))


CONV_PAD = 72
GDN_AQ = HEAD_DIM + GDN_CHUNK
GDN_STEP = 16 * GDN_CHUNK


def _gdn_kernel(qr_ref, kr_ref, vr_ref, og_ref, col_ref, rowt_ref, cwq_ref, cwk_ref, cwv_ref,
                g_ref, s0_ref, o_ref, s_ref,
                xp_ref, q_s, k_s, v_s, o_s, b_s, aq_s):
    is_ctx = _is_ctx_step(pl.program_id(1))
    head = pl.program_id(0)
    n_chunks = UNIT // GDN_CHUNK
    chunks_per_seq = SEQ // GDN_CHUNK

    t = lax.broadcasted_iota(jnp.int32, (UNIT, 1), 0)
    period = jnp.where(is_ctx, SEQ, GRID_W)
    pos = jnp.bitwise_and(t, period - 1)
    ok_left = pos != 0
    ok_right = pos != period - 1
    zeros_pad = jnp.zeros((CONV_PAD, HEAD_DIM), F32)
    xp_ref[0:CONV_PAD, :] = zeros_pad
    xp_ref[CONV_PAD + UNIT:CONV_PAD + UNIT + CONV_PAD, :] = zeros_pad

    def conv_silu(x_ref, w_ref):
        xp_ref[CONV_PAD:CONV_PAD + UNIT, :] = x_ref[...]
        acc = jnp.zeros((UNIT, HEAD_DIM), F32)
        for i in range(CONV_K):
            for j in range(CONV_K):
                w = w_ref[CONV_K * i + j:CONV_K * i + j + 1, :]
                if i != CONV_K // 2:
                    w = jnp.where(is_ctx, 0.0, w)
                start = CONV_PAD + (i - 1) * GRID_W + (j - 1)
                xs = xp_ref[start:start + UNIT, :]
                if j == 0:
                    xs = jnp.where(ok_left, xs, 0.0)
                elif j == CONV_K - 1:
                    xs = jnp.where(ok_right, xs, 0.0)
                acc = acc + xs * w
        return _silu(acc)

    def l2norm(x):
        return x * lax.rsqrt(jnp.sum(x * x, axis=-1, keepdims=True) + EPS)

    q_s[...] = l2norm(conv_silu(qr_ref, cwq_ref)) * (HEAD_DIM ** -0.5)
    k_s[...] = l2norm(conv_silu(kr_ref, cwk_ref))
    v_s[...] = conv_silu(vr_ref, cwv_ref)

    rr = lax.broadcasted_iota(jnp.int32, (GDN_CHUNK, GDN_CHUNK), 0)
    cc = lax.broadcasted_iota(jnp.int32, (GDN_CHUNK, GDN_CHUNK), 1)
    eye = jnp.where(rr == cc, 1.0, 0.0)
    same_blk = [(rr // b) == (cc // b) for b in (8, 16, 32, 64)]
    lane = lax.broadcasted_iota(jnp.int32, (GDN_STEP, LANES), 1)
    sub8 = lax.broadcasted_iota(jnp.int32, (HEADS, GDN_STEP), 0)

    def pick(x, j):
        return jnp.sum(jnp.where(lane[:x.shape[0]] == j, x, 0.0), axis=-1, keepdims=True)

    incl = (cc <= rr, cc >= rr)
    strict = (cc < rr, cc > rr)
    off_masks = [jnp.logical_and(same_blk[lvl], jnp.logical_not(same_blk[lvl - 1]))
                 for lvl in range(1, len(same_blk))]

    def phase1(p, carry):
        rows = pl.ds(pl.multiple_of(p * GDN_STEP, GDN_STEP), GDN_STEP)
        q2, k2, v2 = q_s[rows, :], k_s[rows, :], v_s[rows, :]
        col = col_ref[rows, :]
        gcols = [pick(col, d * HEADS + head) for d in range(2)]
        betas = [pick(col, (2 + d) * HEADS + head) for d in range(2)]
        grows = [jnp.sum(jnp.where(sub8 == head, rowt_ref[d * HEADS:(d + 1) * HEADS, rows], 0.0),
                         axis=0, keepdims=True) for d in range(2)]
        inst = []
        for half in range(GDN_STEP // GDN_CHUNK):
            sl = slice(half * GDN_CHUNK, (half + 1) * GDN_CHUNK)
            qc, kc, vc = q2[sl], k2[sl], v2[sl]
            kk = _dot1(kc, kc, _NT)
            qk = _dot1(qc, kc, _NT)
            for d in range(2):
                gcol, beta = gcols[d][sl], betas[d][sl]
                decay = jnp.where(incl[d], jnp.exp(gcol - grows[d][:, sl]), 0.0)
                m = jnp.where(strict[d], beta * kk * decay, 0.0)
                inst.append((half, d, qc, kc, vc, qk, gcol, beta, decay, m))
        ms = [t[-1] for t in inst]
        pws = [jnp.where(same_blk[0], -m, 0.0) for m in ms]
        tinvs = [eye + pw for pw in pws]
        for _ in range(2):
            pws = [_dot_inv(pw, pw) for pw in pws]
            tinvs = [t + _dot_inv(t, pw) for t, pw in zip(tinvs, pws)]
        for mask in off_masks:
            tmp = [_dot_inv(jnp.where(mask, m, 0.0), t) for m, t in zip(ms, tinvs)]
            tinvs = [t - _dot_inv(t, x) for t, x in zip(tinvs, tmp)]
        uws = [_dot1(tinv, jnp.concatenate([vc * beta, kc * (beta * jnp.exp(gcol))], axis=1)).astype(BF16)
               for (half, d, qc, kc, vc, qk, gcol, beta, decay, m), tinv in zip(inst, tinvs)]
        bas, ows = [], []
        for (half, d, qc, kc, vc, qk, gcol, beta, decay, m), uw in zip(inst, uws):
            glast = gcol[GDN_CHUNK - 1:GDN_CHUNK] if d == 0 else gcol[0:1]
            bas.append(_dg((kc * jnp.exp(glast - gcol)).astype(BF16), uw, _TN))
            ows.append(_dg((qk * decay).astype(BF16), uw, _NN))
        for (half, d, qc, kc, vc, qk, gcol, beta, decay, m), ba, ow in zip(inst, bas, ows):
            chunk0 = p * GDN_STEP + half * GDN_CHUNK
            o_s[d, pl.ds(pl.multiple_of(chunk0, GDN_CHUNK), GDN_CHUNK), :] = ow[:, :HEAD_DIM]
            b_s[d, pl.ds(pl.multiple_of(2 * chunk0, HEAD_DIM), HEAD_DIM), :] = ba[:, :HEAD_DIM]
            aq_s[d, pl.ds(pl.multiple_of(3 * chunk0, GDN_AQ), GDN_AQ), :] = jnp.concatenate(
                [ba[:, HEAD_DIM:], qc * jnp.exp(gcol) - ow[:, HEAD_DIM:]], axis=0).astype(BF16)
        return carry

    lax.fori_loop(0, UNIT // GDN_STEP, phase1, 0)

    lane1 = lax.broadcasted_iota(jnp.int32, (1, LANES), 1)

    def advance(chains, states):
        rs = [_dg(aq_s[d, pl.ds(pl.multiple_of(ch * GDN_AQ, GDN_AQ), GDN_AQ), :], s.astype(BF16), _NN)
              for (d, ch), s in zip(chains, states)]
        new = []
        for (d, ch), s, r in zip(chains, states, rs):
            tile0 = pl.multiple_of(ch * GDN_CHUNK + (GDN_CHUNK - 8 if d == 0 else 0), 8)
            last = col_ref[pl.ds(tile0, 8), :]
            last = last[7:8] if d == 0 else last[0:1]
            glast = jnp.sum(jnp.where(lane1 == d * HEADS + head, last, 0.0), axis=-1, keepdims=True)
            b = b_s[d, pl.ds(pl.multiple_of(ch * HEAD_DIM, HEAD_DIM), HEAD_DIM), :]
            new.append(s * jnp.exp(glast) + (b - r[:HEAD_DIM]))
            rows = pl.ds(pl.multiple_of(ch * GDN_CHUNK, GDN_CHUNK), GDN_CHUNK)
            o_s[d, rows, :] = o_s[d, rows, :] + r[HEAD_DIM:]
        return tuple(new)

    @pl.when(is_ctx)
    def _context_unit():
        def body(i, states):
            chains = [(d, sq * chunks_per_seq + (i if d == 0 else chunks_per_seq - 1 - i))
                      for sq in range(SEQ_PER_UNIT) for d in range(2)]
            return advance(chains, states)
        zero = jnp.zeros((HEAD_DIM, HEAD_DIM), F32)
        final = lax.fori_loop(0, chunks_per_seq, body, (zero,) * (2 * SEQ_PER_UNIT))
        for sq in range(SEQ_PER_UNIT):
            for d in range(2):
                s_ref[sq, d] = final[2 * sq + d]

    @pl.when(jnp.logical_not(is_ctx))
    def _latent_unit():
        def body(i, states):
            return advance([(0, i), (1, n_chunks - 1 - i)], states)
        final = lax.fori_loop(0, n_chunks, body, (s0_ref[0], s0_ref[1]))
        for sq in range(SEQ_PER_UNIT):
            for d in range(2):
                s_ref[sq, d] = final[d]

    o_ref[...] = _head_rmsnorm_gate(o_s[0] + o_s[1], g_ref[...], og_ref[...]).astype(o_ref.dtype)


def _gdn_call(z, col, rowt, conv_w, onorm_g, state, layer, new_states):
    def zcol(off):
        return pl.BlockSpec((None, UNIT, HEAD_DIM), lambda h, g: (off // HEAD_DIM + h, _scan_unit(g), 0))
    def wcol(part):
        return pl.BlockSpec((CONV_K * CONV_K, HEAD_DIM), lambda h, g: (0, part * HEADS + h))
    scr = lambda *shape: pltpu.VMEM(shape, F32)
    in_specs = [zcol(OFF_GQ), zcol(OFF_GK), zcol(OFF_GV), zcol(OFF_GG),
                pl.BlockSpec((UNIT, LANES), lambda h, g: (_scan_unit(g), 0)),
                pl.BlockSpec((LANES, UNIT), lambda h, g: (0, _scan_unit(g))),
                wcol(0), wcol(1), wcol(2),
                pl.BlockSpec((1, HEAD_DIM), lambda h, g: (0, 0)),
                _state_in_spec(layer)]
    args = [z, z, z, z, col, rowt, conv_w, conv_w, conv_w, onorm_g.reshape(1, HEAD_DIM), state]
    kernel, aliases = _gdn_kernel, {}
    if new_states is not None:
        kernel, aliases = _without_arg(_gdn_kernel, len(args)), {len(args): 1}
        in_specs.append(pl.BlockSpec(memory_space=pl.ANY))
        args.append(new_states)
    return pl.pallas_call(
        kernel,
        grid=(HEADS, N_UNITS),
        in_specs=in_specs,
        out_specs=[pl.BlockSpec((None, UNIT, HEAD_DIM), lambda h, g: (h, _scan_unit(g), 0)), _state_out_spec(layer)],
        out_shape=[jax.ShapeDtypeStruct((HEADS, N_TOK, HEAD_DIM), BF16), _STATE_SHAPE],
        input_output_aliases=aliases,
        scratch_shapes=[scr(UNIT + 2 * CONV_PAD, HEAD_DIM),
                        scr(UNIT, HEAD_DIM), scr(UNIT, HEAD_DIM), scr(UNIT, HEAD_DIM),
                        scr(2, UNIT, HEAD_DIM),
                        scr(2, (UNIT // GDN_CHUNK) * HEAD_DIM, HEAD_DIM),
                        pltpu.VMEM((2, (UNIT // GDN_CHUNK) * GDN_AQ, HEAD_DIM), BF16)],
        compiler_params=_cparams(("arbitrary", "arbitrary")),
        name="gdn_scan",
    )(*args)


def kernel(x_prompt, x_sample, c, state_hgrn, state_gdn, c_ctx, norm1_g, norm2_g, w_mod, b_mod, w_in, hg_lb, hg_onorm_g, cm_vnorm_g, cm_ws, cm_bs, gdn_conv, gdn_A_log, gdn_dt_bias, gdn_onorm_g, w_br_hg, w_br_cm, w_br_gdn, w_out, w_ff1, w_ff2, final_g):
    x = (x_prompt.reshape(N_CTX_TOK, D_MODEL), x_sample.reshape(N_LAT_TOK, D_MODEL))
    cvec = jnp.concatenate([c_ctx[None, :], c, jnp.zeros((MOD_ROWS - 1 - DEC_BATCH, D_MODEL), F32)], axis=0)
    mod3 = _mod_call(cvec, w_mod, b_mod).reshape(DEPTH * MOD_ROWS, 1, 6 * D_MODEL)

    lb_all = jnp.cumsum(jax.nn.softmax(hg_lb.astype(F32), axis=0), axis=0)
    lb_all = lb_all - lb_all[:1]

    w_in_t = jnp.swapaxes(w_in, 1, 2)
    w_ff2_bf = w_ff2.astype(BF16)

    new_hg = new_gdn = None
    h = _normmod_call(x, norm1_g[0], mod3, 0, 0, 1)
    for l in range(DEPTH):
        z = _mm_call(h, w_in_t, l, col0=0, n=N_MAIN, act=None, out_dtype=F32, tm=1024, tn=1024,
                     w_transposed=True, lane_tile_major=True, name="in_proj")
        ab = _mm_call(h, w_in_t, l, col0=OFF_AB, n=LANES, act=None, out_dtype=F32, tm=1024, tn=LANES,
                      w_transposed=True, name="in_proj_ab")
        o_a, new_hg = _hgrn_call(z, lb_all[l], hg_onorm_g[l], state_hgrn, l, new_hg)
        o_b = _gmlp_call(z, cm_vnorm_g[l], cm_ws[l], cm_bs[l].T)
        col, rowt = _gdn_gates_call(ab, gdn_A_log[l], gdn_dt_bias[l])
        o_c, new_gdn = _gdn_call(z, col, rowt, gdn_conv[l].reshape(CONV_K * CONV_K, 3 * HG_F),
                                 gdn_onorm_g[l], state_gdn, l, new_gdn)

        merged = _gated_merge_call(h, w_in_t, o_a, o_b, o_c, w_br_hg, w_br_cm, w_br_gdn, l)
        x, h2 = _outproj_norm_call(merged, w_out, x, norm2_g[l], mod3, l)
        up = _mm_call(h2, w_ff1, l, col0=0, n=D_FF, act="relu2", out_dtype=BF16, tm=1024, tn=1024, name="ffn_up")
        if l + 1 < DEPTH:
            x, h = _ffn_down_norm_call(up, w_ff2_bf, x, mod3, l, norm1_g[l + 1])
        else:
            x = _mm_resid_call(up, w_ff2_bf, x, mod3, l, 5, tm=512, tn=512, name="ffn_down")

    y_prompt = _final_norm_call(x, final_g, 0, N_CTX_TOK).reshape(BATCH, SEQ, D_MODEL)
    y_sample = _final_norm_call(x, final_g, N_CTX_TOK, N_LAT_TOK).reshape(DEC_BATCH, DEC_SEQ, D_MODEL)
    return (y_prompt, y_sample, new_hg, new_gdn)
```

```python
import functools

import jax
import jax.numpy as jnp
from jax import lax
from jax.experimental import pallas as pl
from jax.experimental.pallas import tpu as pltpu

F32 = jnp.float32
BF16 = jnp.bfloat16

D_MODEL = 2048
BATCH = 16
SEQ = 256
DEPTH = 2
DEC_BATCH = 4
DEC_SEQ = 1024
GRID_W = 64
EPS = 1e-6
D_FF = 4 * D_MODEL
HEADS = 8
HEAD_DIM = 128
HG_F = HEADS * HEAD_DIM
CM_GROUPS = 8
CM_W = CM_GROUPS * HEAD_DIM
CM_CHUNK = 128
GDN_CHUNK = 64
CONV_K = 3

OFF_HQ, OFF_HI, OFF_HG, OFF_HFF, OFF_HFB = 0, 1024, 2048, 3072, 4096
OFF_CU, OFF_CV = 5120, 6144
OFF_GQ, OFF_GK, OFF_GV, OFF_GG = 7168, 8192, 9216, 10240
OFF_AB = 11264
OFF_GATES = 11296
IN_DIM = 17440
N_MAIN = OFF_AB

N_CTX_TOK = BATCH * SEQ
N_LAT_TOK = DEC_BATCH * DEC_SEQ
N_TOK = N_CTX_TOK + N_LAT_TOK
UNIT = DEC_SEQ
N_CTX_UNITS = N_CTX_TOK // UNIT
N_UNITS = N_TOK // UNIT
N_LAT_UNITS = N_UNITS - N_CTX_UNITS
SEQ_PER_UNIT = UNIT // SEQ

LANES = 128
MOD_ROWS = 8
HG_BLOCK = 256
HG_SUB = 32
HG_INST = 2 * (UNIT // HG_BLOCK)
VMEM_LIMIT = 56 * 1024 * 1024

_NT = (((1,), (1,)), ((), ()))
_TN = (((0,), (0,)), ((), ()))
_NN = (((1,), (0,)), ((), ()))


def _dg(a, b, dims):
    return lax.dot_general(a, b, dims, preferred_element_type=F32)


def _split2(x):
    hi = x.astype(BF16)
    lo = (x - hi.astype(F32)).astype(BF16)
    return hi, lo


def _dot1(a, b, dims=_NN):
    return _dg(a.astype(BF16), b.astype(BF16), dims)


def _dot3(a, b, dims=_NN):
    ah, al = _split2(a)
    bh, bl = _split2(b)
    return _dg(ah, bh, dims) + (_dg(ah, bl, dims) + _dg(al, bh, dims))


_dot_inv = _dot1


def _dot01(m, x):
    hi = x.astype(BF16)
    r = x - hi.astype(F32)
    mid = r.astype(BF16)
    lo = (r - mid.astype(F32)).astype(BF16)
    return _dg(m, hi, _NN) + (_dg(m, mid, _NN) + _dg(m, lo, _NN))


def _dot01_2(m, x):
    hi, lo = _split2(x)
    return _dg(m, hi, _NN) + _dg(m, lo, _NN)


def _sigmoid(x):
    return 1.0 / (1.0 + jnp.exp(-x))


def _silu(x):
    return x * _sigmoid(x)


def _gelu(x):
    return 0.5 * x * (1.0 + lax.erf(x * (2.0 ** -0.5)))


def _softplus(x):
    return jnp.maximum(x, 0.0) + jnp.log1p(jnp.exp(-jnp.abs(x)))


def _mod_row_of_tile(i, tm):
    return jnp.maximum(0, (i * tm - N_CTX_TOK) // DEC_SEQ + 1)


assert N_LAT_UNITS == N_CTX_UNITS


def _scan_unit(g):
    return jnp.where(g % 2 == 0, N_CTX_UNITS + g // 2, g // 2)


def _is_ctx_step(g):
    return g % 2 == 1


def _state_out_spec(layer):
    return pl.BlockSpec((SEQ_PER_UNIT, None, 2, None, HEAD_DIM, HEAD_DIM),
                        lambda h, g: (g // 2, layer, 0, h, 0, 0))


def _state_in_spec(layer):
    return pl.BlockSpec((None, None, 2, None, HEAD_DIM, HEAD_DIM),
                        lambda h, g: (g // 2, layer, 0, h, 0, 0))


_STATE_SHAPE = jax.ShapeDtypeStruct((BATCH, DEPTH, 2, HEADS, HEAD_DIM, HEAD_DIM), F32)


def _without_arg(kernel, pos):
    def wrapped(*refs):
        return kernel(*refs[:pos], *refs[pos + 1:])
    return wrapped


def _cparams(sem):
    return pltpu.CompilerParams(dimension_semantics=sem, vmem_limit_bytes=VMEM_LIMIT)


def _mod_kernel(c_ref, w_ref, b_ref, o_ref):
    s = _silu(c_ref[...])
    hi, lo = _split2(s)
    w = w_ref[...].astype(BF16)
    o_ref[...] = _dg(hi, w, _NN) + _dg(lo, w, _NN) + b_ref[...]


def _mod_call(cvec, w_mod, b_mod):
    tn = 1024
    n = 6 * D_MODEL
    return pl.pallas_call(
        _mod_kernel,
        grid=(DEPTH, n // tn),
        in_specs=[pl.BlockSpec((MOD_ROWS, D_MODEL), lambda l, j: (0, 0)),
                  pl.BlockSpec((None, D_MODEL, tn), lambda l, j: (l, 0, j)),
                  pl.BlockSpec((None, 1, tn), lambda l, j: (l, 0, j))],
        out_specs=pl.BlockSpec((None, MOD_ROWS, tn), lambda l, j: (l, 0, j)),
        out_shape=jax.ShapeDtypeStruct((DEPTH, MOD_ROWS, n), F32),
        compiler_params=_cparams(("arbitrary", "arbitrary")),
        name="modulation",
    )(cvec, w_mod, b_mod.reshape(DEPTH, 1, n))


def _adaln(x, g, shift, scale):
    y = x * lax.rsqrt(jnp.mean(x * x, axis=-1, keepdims=True) + EPS) * g
    return y * (1.0 + scale) + shift


def _residual_specs(x, tm, idx):
    if not isinstance(x, tuple):
        return [pl.BlockSpec((tm, D_MODEL), lambda *g: (idx(*g), 0))], [x]
    n_ctx = N_CTX_TOK // tm
    return ([pl.BlockSpec((tm, D_MODEL), lambda *g: (jnp.minimum(idx(*g), n_ctx - 1), 0)),
             pl.BlockSpec((tm, D_MODEL), lambda *g: (jnp.maximum(idx(*g) - n_ctx, 0), 0))], list(x))


def _residual_tile(refs, tile, tm):
    if len(refs) == 1:
        return refs[0][...]
    return jnp.where(tile < N_CTX_TOK // tm, refs[0][...], refs[1][...])


def _normmod_kernel(*refs, n_x):
    x_refs, (g_ref, sh_ref, sc_ref, o_ref) = refs[:n_x], refs[n_x:]
    x = _residual_tile(x_refs, pl.program_id(0), o_ref.shape[0])
    o_ref[...] = _adaln(x, g_ref[...], sh_ref[...], sc_ref[...]).astype(o_ref.dtype)


def _normmod_call(x, g, mod3, layer, k_shift, k_scale):
    tm = 512
    def mod_spec(k):
        return pl.BlockSpec((None, 1, D_MODEL),
                            lambda i: (layer * MOD_ROWS + _mod_row_of_tile(i, tm), 0, k))
    x_specs, x_args = _residual_specs(x, tm, lambda i: i)
    return pl.pallas_call(
        functools.partial(_normmod_kernel, n_x=len(x_args)),
        grid=(N_TOK // tm,),
        in_specs=x_specs + [pl.BlockSpec((1, D_MODEL), lambda i: (0, 0)), mod_spec(k_shift), mod_spec(k_scale)],
        out_specs=pl.BlockSpec((tm, D_MODEL), lambda i: (i, 0)),
        out_shape=jax.ShapeDtypeStruct((N_TOK, D_MODEL), BF16),
        compiler_params=_cparams(("arbitrary",)),
        name="norm_mod",
    )(*x_args, g.reshape(1, D_MODEL), mod3, mod3)


def _final_norm_kernel(x_ref, g_ref, o_ref):
    x = x_ref[...]
    o_ref[...] = x * lax.rsqrt(jnp.mean(x * x, axis=-1, keepdims=True) + EPS) * g_ref[...]


def _final_norm_call(x, g, row0, n_rows):
    tm = 512
    return pl.pallas_call(
        _final_norm_kernel,
        grid=(n_rows // tm,),
        in_specs=[pl.BlockSpec((tm, D_MODEL), lambda i: (row0 // tm + i, 0)),
                  pl.BlockSpec((1, D_MODEL), lambda i: (0, 0))],
        out_specs=pl.BlockSpec((tm, D_MODEL), lambda i: (i, 0)),
        out_shape=jax.ShapeDtypeStruct((n_rows, D_MODEL), F32),
        compiler_params=_cparams(("arbitrary",)),
        name="final_norm",
    )(x, g.reshape(1, D_MODEL))


def _mm_kernel(x_ref, w_ref, o_ref, wbf_ref, *, act, w_transposed):
    @pl.when(pl.program_id(1) == 0)
    def _cast_weights():
        wbf_ref[...] = w_ref[...].reshape(wbf_ref.shape).astype(BF16)

    acc = lax.dot_general(x_ref[...], wbf_ref[...], _NT if w_transposed else _NN, preferred_element_type=F32)
    if act == "sigmoid":
        acc = _sigmoid(acc)
    elif act == "relu2":
        acc = jnp.square(jnp.maximum(acc, 0.0))
    if len(o_ref.shape) == 3:
        for c in range(o_ref.shape[0]):
            o_ref[c] = acc[:, c * LANES:(c + 1) * LANES].astype(o_ref.dtype)
    else:
        o_ref[...] = acc.astype(o_ref.dtype)


def _mm_call(x, w_all, layer, *, col0, n, act, out_dtype, tm, tn, name, w_transposed=False,
             lane_tile_major=False):
    m, k = x.shape
    assert n % tn == 0 and m % tm == 0
    if w_transposed:
        assert col0 % 8 == 0
        w_spec = pl.BlockSpec((pl.Element(1), pl.Element(tn), pl.Element(k)),
                              lambda j, i: (layer, pl.multiple_of(col0 + j * tn, 8), 0))
        w_scratch = pltpu.VMEM((tn, k), BF16)
    else:
        assert col0 % tn == 0
        w_spec = pl.BlockSpec((None, k, tn), lambda j, i: (layer, 0, col0 // tn + j))
        w_scratch = pltpu.VMEM((k, tn), BF16)
    if lane_tile_major:
        out_spec = pl.BlockSpec((tn // LANES, tm, LANES), lambda j, i: (j, i, 0))
        out_shape = jax.ShapeDtypeStruct((n // LANES, m, LANES), out_dtype)
    else:
        out_spec = pl.BlockSpec((tm, tn), lambda j, i: (i, j))
        out_shape = jax.ShapeDtypeStruct((m, n), out_dtype)
    return pl.pallas_call(
        functools.partial(_mm_kernel, act=act, w_transposed=w_transposed),
        grid=(n // tn, m // tm),
        in_specs=[pl.BlockSpec((tm, k), lambda j, i: (i, 0)), w_spec],
        out_specs=out_spec,
        out_shape=out_shape,
        scratch_shapes=[w_scratch],
        compiler_params=_cparams(("arbitrary", "arbitrary")),
        name=name,
    )(x, w_all)


def _mm_bf16_kernel(x_ref, w_ref, r_ref, g_ref, o_ref):
    acc = jnp.dot(x_ref[...], w_ref[...], preferred_element_type=F32)
    o_ref[...] = r_ref[...] + g_ref[...] * acc


def _mm_resid_call(x, w, resid, mod3, layer, k_gate, *, tm, tn, name):
    m, k = x.shape
    n = w.shape[2]
    per_tile = D_MODEL // tn
    return pl.pallas_call(
        _mm_bf16_kernel,
        grid=(m // tm, n // tn),
        in_specs=[pl.BlockSpec((tm, k), lambda i, j: (i, 0)),
                  pl.BlockSpec((None, k, tn), lambda i, j: (layer, 0, j)),
                  pl.BlockSpec((tm, tn), lambda i, j: (i, j)),
                  pl.BlockSpec((None, 1, tn),
                               lambda i, j: (layer * MOD_ROWS + _mod_row_of_tile(i, tm), 0,
                                             k_gate * per_tile + j))],
        out_specs=pl.BlockSpec((tm, tn), lambda i, j: (i, j)),
        out_shape=jax.ShapeDtypeStruct((m, n), F32),
        compiler_params=_cparams(("arbitrary", "arbitrary")),
        name=name,
    )(x, w, resid, mod3)


def _outproj_norm_kernel(*refs, n_x):
    m_ref, w_ref = refs[:2]
    x_refs = refs[2:2 + n_x]
    gate_ref, g_ref, sh_ref, sc_ref, xo_ref, h_ref, wbf_ref = refs[2 + n_x:]

    @pl.when(pl.program_id(0) == 0)
    def _cast_weights():
        wbf_ref[...] = w_ref[...].astype(BF16)

    acc = jnp.dot(m_ref[...], wbf_ref[...], preferred_element_type=F32)
    xn = _residual_tile(x_refs, pl.program_id(0), xo_ref.shape[0]) + gate_ref[...] * acc
    xo_ref[...] = xn
    h_ref[...] = _adaln(xn, g_ref[...], sh_ref[...], sc_ref[...]).astype(h_ref.dtype)


def _outproj_norm_call(merged, w_out, x, norm_g, mod3, layer):
    tm = 256 if isinstance(x, tuple) else 512
    def mod_spec(k):
        return pl.BlockSpec((None, 1, D_MODEL),
                            lambda i: (layer * MOD_ROWS + _mod_row_of_tile(i, tm), 0, k))
    row = pl.BlockSpec((tm, D_MODEL), lambda i: (i, 0))
    x_specs, x_args = _residual_specs(x, tm, lambda i: i)
    return pl.pallas_call(
        functools.partial(_outproj_norm_kernel, n_x=len(x_args)),
        grid=(N_TOK // tm,),
        in_specs=[row,
                  pl.BlockSpec((None, D_MODEL, D_MODEL), lambda i: (layer, 0, 0), pipeline_mode=pl.Buffered(1))]
                 + x_specs + [mod_spec(2), pl.BlockSpec((1, D_MODEL), lambda i: (0, 0)), mod_spec(3), mod_spec(4)],
        out_specs=[row, row],
        out_shape=[jax.ShapeDtypeStruct((N_TOK, D_MODEL), F32),
                   jax.ShapeDtypeStruct((N_TOK, D_MODEL), BF16)],
        scratch_shapes=[pltpu.VMEM((D_MODEL, D_MODEL), BF16)],
        compiler_params=_cparams(("arbitrary",)),
        name="out_proj_norm",
    )(merged, w_out, *x_args, mod3, norm_g.reshape(1, D_MODEL), mod3, mod3)


def _ffn_down_norm_kernel(x_ref, w_ref, r_ref, gate_ref, g_ref, sh_ref, sc_ref, o_ref, h_ref, row_ref):
    j = pl.program_id(1)
    tn = o_ref.shape[1]
    acc = jnp.dot(x_ref[...], w_ref[...], preferred_element_type=F32)
    xn = r_ref[...] + gate_ref[...] * acc
    o_ref[...] = xn
    row_ref[:, pl.ds(pl.multiple_of(j * tn, tn), tn)] = xn

    @pl.when(j == pl.num_programs(1) - 1)
    def _next_layer_norm():
        h_ref[...] = _adaln(row_ref[...], g_ref[...], sh_ref[...], sc_ref[...]).astype(h_ref.dtype)


def _ffn_down_norm_call(up, w_bf, resid, mod3, layer, next_norm_g):
    tm, tn = 512, 512
    m, k = up.shape
    per_tile = D_MODEL // tn
    def mod_row(i):
        return _mod_row_of_tile(i, tm)
    def next_mod(kk):
        return pl.BlockSpec((None, 1, D_MODEL), lambda i, j: ((layer + 1) * MOD_ROWS + mod_row(i), 0, kk))
    return pl.pallas_call(
        _ffn_down_norm_kernel,
        grid=(m // tm, D_MODEL // tn),
        in_specs=[pl.BlockSpec((tm, k), lambda i, j: (i, 0)),
                  pl.BlockSpec((None, k, tn), lambda i, j: (layer, 0, j)),
                  pl.BlockSpec((tm, tn), lambda i, j: (i, j)),
                  pl.BlockSpec((None, 1, tn), lambda i, j: (layer * MOD_ROWS + mod_row(i), 0, 5 * per_tile + j)),
                  pl.BlockSpec((1, D_MODEL), lambda i, j: (0, 0)), next_mod(0), next_mod(1)],
        out_specs=[pl.BlockSpec((tm, tn), lambda i, j: (i, j)),
                   pl.BlockSpec((tm, D_MODEL), lambda i, j: (i, 0))],
        out_shape=[jax.ShapeDtypeStruct((m, D_MODEL), F32),
                   jax.ShapeDtypeStruct((m, D_MODEL), BF16)],
        scratch_shapes=[pltpu.VMEM((tm, D_MODEL), F32)],
        compiler_params=_cparams(("arbitrary", "arbitrary")),
        name="ffn_down_norm",
    )(up, w_bf, resid, mod3, next_norm_g.reshape(1, D_MODEL), mod3, mod3)


def _gated_merge_kernel(h_ref, wga_ref, wgb_ref, wgc_ref, oa_ref, ob_ref, oc_ref, wa_ref, wb_ref, wc_ref,
                        o_ref, wg_bf, wbr_bf):
    @pl.when(pl.program_id(1) == 0)
    def _cast_weights():
        for t, ref in enumerate((wga_ref, wgb_ref, wgc_ref)):
            wg_bf[t] = ref[...].reshape(wg_bf.shape[1:]).astype(BF16)
        for t, ref in enumerate((wa_ref, wb_ref, wc_ref)):
            wbr_bf[t] = ref[...].astype(BF16)

    def rows_of(ref):
        return jnp.concatenate([ref[t] for t in range(ref.shape[0])], axis=1)

    h = h_ref[...]
    acc = None
    for t, br_ref in enumerate((oa_ref, ob_ref, oc_ref)):
        gate = _sigmoid(lax.dot_general(h, wg_bf[t], _NT, preferred_element_type=F32))
        term = gate * jnp.dot(rows_of(br_ref), wbr_bf[t], preferred_element_type=F32)
        acc = term if acc is None else acc + term
    o_ref[...] = acc.astype(o_ref.dtype)


def _gated_merge_call(h, w_in_t, o_a, o_b, o_c, w_a, w_b, w_c, layer):
    tm, tn = 1024, 256
    def gate_w(b):
        return pl.BlockSpec((pl.Element(1), pl.Element(tn), pl.Element(D_MODEL)),
                            lambda j, i: (layer, pl.multiple_of(OFF_GATES + b * D_MODEL + j * tn, 8), 0))
    br = pl.BlockSpec((HEADS, tm, HEAD_DIM), lambda j, i: (0, i, 0))
    wt = pl.BlockSpec((None, HG_F, tn), lambda j, i: (layer, 0, j))
    return pl.pallas_call(
        _gated_merge_kernel,
        grid=(D_MODEL // tn, N_TOK // tm),
        in_specs=[pl.BlockSpec((tm, D_MODEL), lambda j, i: (i, 0)), gate_w(0), gate_w(1), gate_w(2),
                  br, br, br, wt, wt, wt],
        out_specs=pl.BlockSpec((tm, tn), lambda j, i: (i, j)),
        out_shape=jax.ShapeDtypeStruct((N_TOK, D_MODEL), BF16),
        scratch_shapes=[pltpu.VMEM((3, tn, D_MODEL), BF16), pltpu.VMEM((3, HG_F, tn), BF16)],
        compiler_params=_cparams(("arbitrary", "arbitrary")),
        name="gated_merge",
    )(h, w_in_t, w_in_t, w_in_t, o_a, o_b, o_c, w_a, w_b, w_c)


def _head_rmsnorm_gate(o, g, og):
    y = o * lax.rsqrt(jnp.mean(o * o, axis=-1, keepdims=True) + EPS) * g
    return y * _silu(og)


def _hgrn_kernel(q_ref, i_ref, og_ref, ff_ref, fb_ref, lb_ref, g_ref, s0_ref, o_ref, s_ref,
                 acc_ref, b_ref, stc_ref, km_ref):
    is_ctx = _is_ctx_step(pl.program_id(1))
    nblk = UNIT // HG_BLOCK
    nsub = HG_BLOCK // HG_SUB
    r = lax.broadcasted_iota(jnp.int32, (HG_BLOCK, HG_BLOCK), 0)
    c = lax.broadcasted_iota(jnp.int32, (HG_BLOCK, HG_BLOCK), 1)
    same = (r // HG_SUB) == (c // HG_SUB)
    row_sub = lax.broadcasted_iota(jnp.int32, (HG_BLOCK, HEAD_DIM), 0) // HG_SUB
    tri = (jnp.logical_and(same, c <= r), jnp.logical_and(same, c >= r))
    tri_bf = [jnp.where(t, 1.0, 0.0).astype(BF16) for t in tri]
    z_refs = (ff_ref, fb_ref)
    inst = [(blk, d) for blk in range(nblk) for d in range(2)]

    def expand(x):
        return jnp.broadcast_to(x[:, None, :], (nsub, HG_SUB, HEAD_DIM)).reshape(HG_BLOCK, HEAD_DIM)

    def blk_rows(blk):
        return slice(blk * HG_BLOCK, (blk + 1) * HG_BLOCK)

    qs = [_silu(q_ref[blk_rows(blk), :]) for blk in range(nblk)]
    vs = [i_ref[blk_rows(blk), :] for blk in range(nblk)]
    ks, lfs = [], []
    for blk, d in inst:
        lb = lb_ref[d:d + 1, :]
        f = lb + (1.0 - lb) * _sigmoid(z_refs[d][blk_rows(blk), :])
        lfs.append(jnp.log(f))
        ks.append(1.0 - f)
    bs = [_dot01_2(tri_bf[d], lf) for (blk, d), lf in zip(inst, lfs)]
    tots, qts, kts, qds = [], [], [], []
    for n, (blk, d) in enumerate(inst):
        b = bs[n]
        b_ref[n] = b
        tot = b_ref[n, pl.ds((HG_SUB - 1) if d == 0 else 0, nsub, stride=HG_SUB), :]
        mid_f = expand(b_ref[n, pl.ds(HG_SUB // 2, nsub, stride=HG_SUB), :])
        tots.append(tot)
        qts.append(qs[blk] * jnp.exp(b - mid_f))
        kts.append(ks[n] * jnp.exp(mid_f - b))
        qds.append((qs[blk] * jnp.exp(b)).astype(BF16))
        kd = ks[n] * jnp.exp(expand(tot) - b)
        for s in range(nsub):
            km_ref[n, :, s * HEAD_DIM:(s + 1) * HEAD_DIM] = jnp.where(row_sub == s, kd, 0.0).astype(BF16)
    scs = [jnp.where(tri[d], _dot1(qt, kt, _NT), 0.0) for (blk, d), qt, kt in zip(inst, qts, kts)]
    uts = [_dg(vs[blk].astype(BF16), km_ref[n], _TN) for n, (blk, d) in enumerate(inst)]
    outs = [_dot1(sc, vs[blk]) for (blk, d), sc in zip(inst, scs)]
    for d in range(2):
        st = s0_ref[d].T
        for blk in (range(nblk) if d == 0 else range(nblk - 1, -1, -1)):
            n = inst.index((blk, d))
            st = jnp.where(is_ctx, 0.0, st)
            for s in (range(nsub) if d == 0 else range(nsub - 1, -1, -1)):
                stc_ref[n, s] = st.astype(BF16)
                st = st * jnp.exp(tots[n][s:s + 1, :]) + uts[n][:, s * HEAD_DIM:(s + 1) * HEAD_DIM]
            s_ref[blk, d] = st.T
    for n, (blk, d) in enumerate(inst):
        o_int = [_dg(qds[n][s * HG_SUB:(s + 1) * HG_SUB], stc_ref[n, s], _NT) for s in range(nsub)]
        acc_ref[d, blk_rows(blk), :] = outs[n] + jnp.concatenate(o_int, axis=0)
    o_ref[...] = _head_rmsnorm_gate(acc_ref[0] + acc_ref[1], g_ref[...], og_ref[...]).astype(o_ref.dtype)


def _hgrn_call(z, lb, onorm_g, state, layer, new_states):
    def col(off):
        return pl.BlockSpec((None, UNIT, HEAD_DIM), lambda h, g: (off // HEAD_DIM + h, _scan_unit(g), 0))
    in_specs = [col(OFF_HQ), col(OFF_HI), col(OFF_HG), col(OFF_HFF), col(OFF_HFB),
                pl.BlockSpec((2, HEAD_DIM), lambda h, g: (0, h)),
                pl.BlockSpec((1, HEAD_DIM), lambda h, g: (0, 0)),
                _state_in_spec(layer)]
    args = [z, z, z, z, z, lb, onorm_g.reshape(1, HEAD_DIM), state]
    kernel, aliases = _hgrn_kernel, {}
    if new_states is not None:
        kernel, aliases = _without_arg(_hgrn_kernel, len(args)), {len(args): 1}
        in_specs.append(pl.BlockSpec(memory_space=pl.ANY))
        args.append(new_states)
    return pl.pallas_call(
        kernel,
        grid=(HEADS, N_UNITS),
        in_specs=in_specs,
        out_specs=[pl.BlockSpec((None, UNIT, HEAD_DIM), lambda h, g: (h, _scan_unit(g), 0)), _state_out_spec(layer)],
        out_shape=[jax.ShapeDtypeStruct((HEADS, N_TOK, HEAD_DIM), BF16), _STATE_SHAPE],
        input_output_aliases=aliases,
        scratch_shapes=[pltpu.VMEM((2, UNIT, HEAD_DIM), F32),
                        pltpu.VMEM((HG_INST, HG_BLOCK, HEAD_DIM), F32),
                        pltpu.VMEM((HG_INST, HG_BLOCK // HG_SUB, HEAD_DIM, HEAD_DIM), BF16),
                        pltpu.VMEM((HG_INST, HG_BLOCK, (HG_BLOCK // HG_SUB) * HEAD_DIM), BF16)],
        compiler_params=_cparams(("arbitrary", "arbitrary")),
        name="hgrn2_scan",
    )(*args)


def _gmlp_kernel(u_ref, v_ref, vn_ref, ws_ref, bs_ref, o_ref):
    tm = u_ref.shape[1]
    for g in range(CM_GROUPS):
        cols = slice(g * HEAD_DIM, (g + 1) * HEAD_DIM)
        vg = _gelu(v_ref[g])
        vg = vg * lax.rsqrt(jnp.mean(vg * vg, axis=-1, keepdims=True) + EPS) * vn_ref[:, cols]
        ug = _gelu(u_ref[g])
        w = ws_ref[g]
        bias = bs_ref[:, g:g + 1]
        for ch in range(tm // CM_CHUNK):
            rows = slice(ch * CM_CHUNK, (ch + 1) * CM_CHUNK)
            s = _dot3(w, vg[rows]) + bias
            o_ref[g, rows, :] = (ug[rows] * s).astype(o_ref.dtype)


def _gmlp_call(z, vnorm_g, ws, bs_t):
    tm = 512
    return pl.pallas_call(
        _gmlp_kernel,
        grid=(N_TOK // tm,),
        in_specs=[pl.BlockSpec((CM_GROUPS, tm, HEAD_DIM), lambda i: (OFF_CU // CM_W, i, 0)),
                  pl.BlockSpec((CM_GROUPS, tm, HEAD_DIM), lambda i: (OFF_CV // CM_W, i, 0)),
                  pl.BlockSpec((1, CM_W), lambda i: (0, 0)),
                  pl.BlockSpec((CM_GROUPS, CM_CHUNK, CM_CHUNK), lambda i: (0, 0, 0)),
                  pl.BlockSpec((CM_CHUNK, CM_GROUPS), lambda i: (0, 0))],
        out_specs=pl.BlockSpec((CM_GROUPS, tm, HEAD_DIM), lambda i: (0, i, 0)),
        out_shape=jax.ShapeDtypeStruct((CM_GROUPS, N_TOK, HEAD_DIM), BF16),
        compiler_params=_cparams(("arbitrary",)),
        name="chunk_gmlp",
    )(z, z, vnorm_g.reshape(1, CM_W), ws, bs_t)


GP_BLOCK = 256


def _gdn_gates_kernel(ab_ref, alog_ref, dt_ref, col_ref, rowt_ref):
    ab = ab_ref[...]
    lane = lax.broadcasted_iota(jnp.int32, ab.shape, 1)
    g = jnp.where(lane < 2 * HEADS, -jnp.exp(alog_ref[...]) * _softplus(ab + dt_ref[...]), 0.0)
    r = lax.broadcasted_iota(jnp.int32, (GP_BLOCK, GP_BLOCK), 0)
    c = lax.broadcasted_iota(jnp.int32, (GP_BLOCK, GP_BLOCK), 1)
    same = (r // GDN_CHUNK) == (c // GDN_CHUNK)
    tri_f = jnp.where(jnp.logical_and(same, c <= r), 1.0, 0.0).astype(BF16)
    tri_b = jnp.where(jnp.logical_and(same, c >= r), 1.0, 0.0).astype(BF16)
    cf = _dot01(tri_f, g)
    cb = _dot01(tri_b, g)
    col = jnp.where(lane < HEADS, cf, jnp.where(lane < 2 * HEADS, cb, _sigmoid(ab)))
    col_ref[...] = col
    rowt_ref[...] = col.T


def _gdn_gates_call(ab, a_log, dt_bias):
    pad = lambda t: jnp.pad(t.reshape(1, 2 * HEADS), ((0, 0), (0, LANES - 2 * HEADS)))
    return pl.pallas_call(
        _gdn_gates_kernel,
        grid=(N_TOK // GP_BLOCK,),
        in_specs=[pl.BlockSpec((GP_BLOCK, LANES), lambda i: (i, 0)),
                  pl.BlockSpec((1, LANES), lambda i: (0, 0)),
                  pl.BlockSpec((1, LANES), lambda i: (0, 0))],
        out_specs=[pl.BlockSpec((GP_BLOCK, LANES), lambda i: (i, 0)),
                   pl.BlockSpec((LANES, GP_BLOCK), lambda i: (0, i))],
        out_shape=[jax.ShapeDtypeStruct((N_TOK, LANES), F32),
                   jax.ShapeDtypeStruct((LANES, N_TOK), F32)],
        compiler_params=_cparams(("arbitrary",)),
        name="gdn_gates",
    )(ab, pad(a_log), pad(dt_bias))


CONV_PAD = 72
GDN_AQ = HEAD_DIM + GDN_CHUNK
GDN_STEP = 16 * GDN_CHUNK


def _gdn_kernel(qr_ref, kr_ref, vr_ref, og_ref, col_ref, rowt_ref, cwq_ref, cwk_ref, cwv_ref,
                g_ref, s0_ref, o_ref, s_ref,
                xp_ref, cv_ref, q_s, k_s, v_s, o_s, b_s, aq_s):
    is_ctx = _is_ctx_step(pl.program_id(1))
    head = pl.program_id(0)
    n_chunks = UNIT // GDN_CHUNK
    chunks_per_seq = SEQ // GDN_CHUNK

    t = lax.broadcasted_iota(jnp.int32, (UNIT, 1), 0)
    period = jnp.where(is_ctx, SEQ, GRID_W)
    pos = jnp.bitwise_and(t, period - 1)
    ok_left = pos != 0
    ok_right = pos != period - 1
    zeros_pad = jnp.zeros((CONV_PAD, HEAD_DIM), F32)
    xp_ref[0:CONV_PAD, :] = zeros_pad
    xp_ref[CONV_PAD + UNIT:CONV_PAD + UNIT + CONV_PAD, :] = zeros_pad

    def conv_silu(x_ref, w_ref):
        mid = CONV_K // 2
        x = x_ref[...]
        xp_ref[CONV_PAD:CONV_PAD + UNIT, :] = x
        taps = (jnp.where(ok_left, xp_ref[CONV_PAD - 1:CONV_PAD - 1 + UNIT, :], 0.0), x,
                jnp.where(ok_right, xp_ref[CONV_PAD + 1:CONV_PAD + 1 + UNIT, :], 0.0))
        cv_ref[...] = sum(taps[j] * w_ref[CONV_K * mid + j:CONV_K * mid + j + 1, :] for j in range(CONV_K))

        @pl.when(jnp.logical_not(is_ctx))
        def _grid_rows():
            acc = cv_ref[...]
            for j in range(CONV_K):
                xp_ref[CONV_PAD:CONV_PAD + UNIT, :] = taps[j]
                for i in (0, CONV_K - 1):
                    start = CONV_PAD + (i - mid) * GRID_W
                    acc = acc + xp_ref[start:start + UNIT, :] * w_ref[CONV_K * i + j:CONV_K * i + j + 1, :]
            cv_ref[...] = acc

        return _silu(cv_ref[...])

    def l2norm(x):
        return x * lax.rsqrt(jnp.sum(x * x, axis=-1, keepdims=True) + EPS)

    q_s[...] = l2norm(conv_silu(qr_ref, cwq_ref)) * (HEAD_DIM ** -0.5)
    k_s[...] = l2norm(conv_silu(kr_ref, cwk_ref))
    v_s[...] = conv_silu(vr_ref, cwv_ref)

    rr = lax.broadcasted_iota(jnp.int32, (GDN_CHUNK, GDN_CHUNK), 0)
    cc = lax.broadcasted_iota(jnp.int32, (GDN_CHUNK, GDN_CHUNK), 1)
    eye = jnp.where(rr == cc, 1.0, 0.0)
    same_blk = [(rr // b) == (cc // b) for b in (8, 16, 32, 64)]
    lane = lax.broadcasted_iota(jnp.int32, (GDN_STEP, LANES), 1)
    sub8 = lax.broadcasted_iota(jnp.int32, (HEADS, GDN_STEP), 0)

    def pick(x, j):
        return jnp.sum(jnp.where(lane[:x.shape[0]] == j, x, 0.0), axis=-1, keepdims=True)

    incl = (cc <= rr, cc >= rr)
    strict = (cc < rr, cc > rr)
    off_masks = [jnp.logical_and(same_blk[lvl], jnp.logical_not(same_blk[lvl - 1]))
                 for lvl in range(1, len(same_blk))]

    def phase1(p, carry):
        rows = pl.ds(pl.multiple_of(p * GDN_STEP, GDN_STEP), GDN_STEP)
        q2, k2, v2 = q_s[rows, :], k_s[rows, :], v_s[rows, :]
        col = col_ref[rows, :]
        gcols = [pick(col, d * HEADS + head) for d in range(2)]
        betas = [pick(col, (2 + d) * HEADS + head) for d in range(2)]
        grows = [jnp.sum(jnp.where(sub8 == head, rowt_ref[d * HEADS:(d + 1) * HEADS, rows], 0.0),
                         axis=0, keepdims=True) for d in range(2)]
        inst = []
        for half in range(GDN_STEP // GDN_CHUNK):
            sl = slice(half * GDN_CHUNK, (half + 1) * GDN_CHUNK)
            qc, kc, vc = q2[sl], k2[sl], v2[sl]
            kk = _dot1(kc, kc, _NT)
            qk = _dot1(qc, kc, _NT)
            for d in range(2):
                gcol, beta = gcols[d][sl], betas[d][sl]
                decay = jnp.where(incl[d], jnp.exp(gcol - grows[d][:, sl]), 0.0)
                m = jnp.where(strict[d], beta * kk * decay, 0.0)
                inst.append((half, d, qc, kc, vc, qk, gcol, beta, decay, m))
        ms = [t[-1] for t in inst]
        pws = [jnp.where(same_blk[0], -m, 0.0) for m in ms]
        tinvs = [eye + pw for pw in pws]
        for _ in range(2):
            pws = [_dot_inv(pw, pw) for pw in pws]
            tinvs = [t + _dot_inv(t, pw) for t, pw in zip(tinvs, pws)]
        for mask in off_masks:
            tmp = [_dot_inv(jnp.where(mask, m, 0.0), t) for m, t in zip(ms, tinvs)]
            tinvs = [t - _dot_inv(t, x) for t, x in zip(tinvs, tmp)]
        uws = [_dot1(tinv, jnp.concatenate([vc * beta, kc * (beta * jnp.exp(gcol))], axis=1)).astype(BF16)
               for (half, d, qc, kc, vc, qk, gcol, beta, decay, m), tinv in zip(inst, tinvs)]
        bas, ows = [], []
        for (half, d, qc, kc, vc, qk, gcol, beta, decay, m), uw in zip(inst, uws):
            glast = gcol[GDN_CHUNK - 1:GDN_CHUNK] if d == 0 else gcol[0:1]
            bas.append(_dg((kc * jnp.exp(glast - gcol)).astype(BF16), uw, _TN))
            ows.append(_dg((qk * decay).astype(BF16), uw, _NN))
        for (half, d, qc, kc, vc, qk, gcol, beta, decay, m), ba, ow in zip(inst, bas, ows):
            chunk0 = p * GDN_STEP + half * GDN_CHUNK
            o_s[d, pl.ds(pl.multiple_of(chunk0, GDN_CHUNK), GDN_CHUNK), :] = ow[:, :HEAD_DIM]
            b_s[d, pl.ds(pl.multiple_of(2 * chunk0, HEAD_DIM), HEAD_DIM), :] = ba[:, :HEAD_DIM]
            aq_s[d, pl.ds(pl.multiple_of(3 * chunk0, GDN_AQ), GDN_AQ), :] = jnp.concatenate(
                [ba[:, HEAD_DIM:], qc * jnp.exp(gcol) - ow[:, HEAD_DIM:]], axis=0).astype(BF16)
        return carry

    lax.fori_loop(0, UNIT // GDN_STEP, phase1, 0)

    lane1 = lax.broadcasted_iota(jnp.int32, (1, LANES), 1)

    def advance(chains, states):
        rs = [_dg(aq_s[d, pl.ds(pl.multiple_of(ch * GDN_AQ, GDN_AQ), GDN_AQ), :], s.astype(BF16), _NN)
              for (d, ch), s in zip(chains, states)]
        new = []
        for (d, ch), s, r in zip(chains, states, rs):
            tile0 = pl.multiple_of(ch * GDN_CHUNK + (GDN_CHUNK - 8 if d == 0 else 0), 8)
            last = col_ref[pl.ds(tile0, 8), :]
            last = last[7:8] if d == 0 else last[0:1]
            glast = jnp.sum(jnp.where(lane1 == d * HEADS + head, last, 0.0), axis=-1, keepdims=True)
            b = b_s[d, pl.ds(pl.multiple_of(ch * HEAD_DIM, HEAD_DIM), HEAD_DIM), :]
            new.append(s * jnp.exp(glast) + (b - r[:HEAD_DIM]))
            rows = pl.ds(pl.multiple_of(ch * GDN_CHUNK, GDN_CHUNK), GDN_CHUNK)
            o_s[d, rows, :] = o_s[d, rows, :] + r[HEAD_DIM:]
        return tuple(new)

    @pl.when(is_ctx)
    def _context_unit():
        def body(i, states):
            chains = [(d, sq * chunks_per_seq + (i if d == 0 else chunks_per_seq - 1 - i))
                      for sq in range(SEQ_PER_UNIT) for d in range(2)]
            return advance(chains, states)
        zero = jnp.zeros((HEAD_DIM, HEAD_DIM), F32)
        final = lax.fori_loop(0, chunks_per_seq, body, (zero,) * (2 * SEQ_PER_UNIT))
        for sq in range(SEQ_PER_UNIT):
            for d in range(2):
                s_ref[sq, d] = final[2 * sq + d]

    @pl.when(jnp.logical_not(is_ctx))
    def _latent_unit():
        def body(i, states):
            return advance([(0, i), (1, n_chunks - 1 - i)], states)
        final = lax.fori_loop(0, n_chunks, body, (s0_ref[0], s0_ref[1]))
        for sq in range(SEQ_PER_UNIT):
            for d in range(2):
                s_ref[sq, d] = final[d]

    o_ref[...] = _head_rmsnorm_gate(o_s[0] + o_s[1], g_ref[...], og_ref[...]).astype(o_ref.dtype)


def _gdn_call(z, col, rowt, conv_w, onorm_g, state, layer, new_states):
    def zcol(off):
        return pl.BlockSpec((None, UNIT, HEAD_DIM), lambda h, g: (off // HEAD_DIM + h, _scan_unit(g), 0))
    def wcol(part):
        return pl.BlockSpec((CONV_K * CONV_K, HEAD_DIM), lambda h, g: (0, part * HEADS + h))
    scr = lambda *shape: pltpu.VMEM(shape, F32)
    in_specs = [zcol(OFF_GQ), zcol(OFF_GK), zcol(OFF_GV), zcol(OFF_GG),
                pl.BlockSpec((UNIT, LANES), lambda h, g: (_scan_unit(g), 0)),
                pl.BlockSpec((LANES, UNIT), lambda h, g: (0, _scan_unit(g))),
                wcol(0), wcol(1), wcol(2),
                pl.BlockSpec((1, HEAD_DIM), lambda h, g: (0, 0)),
                _state_in_spec(layer)]
    args = [z, z, z, z, col, rowt, conv_w, conv_w, conv_w, onorm_g.reshape(1, HEAD_DIM), state]
    kernel, aliases = _gdn_kernel, {}
    if new_states is not None:
        kernel, aliases = _without_arg(_gdn_kernel, len(args)), {len(args): 1}
        in_specs.append(pl.BlockSpec(memory_space=pl.ANY))
        args.append(new_states)
    return pl.pallas_call(
        kernel,
        grid=(HEADS, N_UNITS),
        in_specs=in_specs,
        out_specs=[pl.BlockSpec((None, UNIT, HEAD_DIM), lambda h, g: (h, _scan_unit(g), 0)), _state_out_spec(layer)],
        out_shape=[jax.ShapeDtypeStruct((HEADS, N_TOK, HEAD_DIM), BF16), _STATE_SHAPE],
        input_output_aliases=aliases,
        scratch_shapes=[scr(UNIT + 2 * CONV_PAD, HEAD_DIM), scr(UNIT, HEAD_DIM),
                        scr(UNIT, HEAD_DIM), scr(UNIT, HEAD_DIM), scr(UNIT, HEAD_DIM),
                        scr(2, UNIT, HEAD_DIM),
                        scr(2, (UNIT // GDN_CHUNK) * HEAD_DIM, HEAD_DIM),
                        pltpu.VMEM((2, (UNIT // GDN_CHUNK) * GDN_AQ, HEAD_DIM), BF16)],
        compiler_params=_cparams(("arbitrary", "arbitrary")),
        name="gdn_scan",
    )(*args)


def kernel(x_prompt, x_sample, c, state_hgrn, state_gdn, c_ctx, norm1_g, norm2_g, w_mod, b_mod, w_in, hg_lb, hg_onorm_g, cm_vnorm_g, cm_ws, cm_bs, gdn_conv, gdn_A_log, gdn_dt_bias, gdn_onorm_g, w_br_hg, w_br_cm, w_br_gdn, w_out, w_ff1, w_ff2, final_g):
    x = (x_prompt.reshape(N_CTX_TOK, D_MODEL), x_sample.reshape(N_LAT_TOK, D_MODEL))
    cvec = jnp.concatenate([c_ctx[None, :], c, jnp.zeros((MOD_ROWS - 1 - DEC_BATCH, D_MODEL), F32)], axis=0)
    mod3 = _mod_call(cvec, w_mod, b_mod).reshape(DEPTH * MOD_ROWS, 1, 6 * D_MODEL)

    lb_all = jnp.cumsum(jax.nn.softmax(hg_lb.astype(F32), axis=0), axis=0)
    lb_all = lb_all - lb_all[:1]

    w_in_t = jnp.swapaxes(w_in, 1, 2)
    w_ff2_bf = w_ff2.astype(BF16)

    new_hg = new_gdn = None
    h = _normmod_call(x, norm1_g[0], mod3, 0, 0, 1)
    for l in range(DEPTH):
        z = _mm_call(h, w_in_t, l, col0=0, n=N_MAIN, act=None, out_dtype=F32, tm=1024, tn=1024,
                     w_transposed=True, lane_tile_major=True, name="in_proj")
        ab = _mm_call(h, w_in_t, l, col0=OFF_AB, n=LANES, act=None, out_dtype=F32, tm=1024, tn=LANES,
                      w_transposed=True, name="in_proj_ab")
        o_a, new_hg = _hgrn_call(z, lb_all[l], hg_onorm_g[l], state_hgrn, l, new_hg)
        o_b = _gmlp_call(z, cm_vnorm_g[l], cm_ws[l], cm_bs[l].T)
        col, rowt = _gdn_gates_call(ab, gdn_A_log[l], gdn_dt_bias[l])
        o_c, new_gdn = _gdn_call(z, col, rowt, gdn_conv[l].reshape(CONV_K * CONV_K, 3 * HG_F),
                                 gdn_onorm_g[l], state_gdn, l, new_gdn)

        merged = _gated_merge_call(h, w_in_t, o_a, o_b, o_c, w_br_hg, w_br_cm, w_br_gdn, l)
        x, h2 = _outproj_norm_call(merged, w_out, x, norm2_g[l], mod3, l)
        up = _mm_call(h2, w_ff1, l, col0=0, n=D_FF, act="relu2", out_dtype=BF16, tm=1024, tn=1024, name="ffn_up")
        if l + 1 < DEPTH:
            x, h = _ffn_down_norm_call(up, w_ff2_bf, x, mod3, l, norm1_g[l + 1])
        else:
            x = _mm_resid_call(up, w_ff2_bf, x, mod3, l, 5, tm=512, tn=512, name="ffn_down")

    y_prompt = _final_norm_call(x, final_g, 0, N_CTX_TOK).reshape(BATCH, SEQ, D_MODEL)
    y_sample = _final_norm_call(x, final_g, N_CTX_TOK, N_LAT_TOK).reshape(DEC_BATCH, DEC_SEQ, D_MODEL)
    return (y_prompt, y_sample, new_hg, new_gdn)
```

```python
import functools

import jax
import jax.numpy as jnp
from jax import lax
from jax.experimental import pallas as pl
from jax.experimental.pallas import tpu as pltpu

F32 = jnp.float32
BF16 = jnp.bfloat16

D_MODEL = 2048
BATCH = 16
SEQ = 256
DEPTH = 2
DEC_BATCH = 4
DEC_SEQ = 1024
GRID_W = 64
EPS = 1e-6
D_FF = 4 * D_MODEL
HEADS = 8
HEAD_DIM = 128
HG_F = HEADS * HEAD_DIM
CM_GROUPS = 8
CM_W = CM_GROUPS * HEAD_DIM
CM_CHUNK = 128
GDN_CHUNK = 64
CONV_K = 3

OFF_HQ, OFF_HI, OFF_HG, OFF_HFF, OFF_HFB = 0, 1024, 2048, 3072, 4096
OFF_CU, OFF_CV = 5120, 6144
OFF_GQ, OFF_GK, OFF_GV, OFF_GG = 7168, 8192, 9216, 10240
OFF_AB = 11264
OFF_GATES = 11296
IN_DIM = 17440
N_MAIN = OFF_AB

N_CTX_TOK = BATCH * SEQ
N_LAT_TOK = DEC_BATCH * DEC_SEQ
N_TOK = N_CTX_TOK + N_LAT_TOK
UNIT = DEC_SEQ
N_CTX_UNITS = N_CTX_TOK // UNIT
N_UNITS = N_TOK // UNIT
N_LAT_UNITS = N_UNITS - N_CTX_UNITS
SEQ_PER_UNIT = UNIT // SEQ

LANES = 128
MOD_ROWS = 8
HG_BLOCK = 256
HG_SUB = 32
HG_INST = 2 * (UNIT // HG_BLOCK)
VMEM_LIMIT = 56 * 1024 * 1024

_NT = (((1,), (1,)), ((), ()))
_TN = (((0,), (0,)), ((), ()))
_NN = (((1,), (0,)), ((), ()))


def _dg(a, b, dims):
    return lax.dot_general(a, b, dims, preferred_element_type=F32)


def _split2(x):
    hi = x.astype(BF16)
    lo = (x - hi.astype(F32)).astype(BF16)
    return hi, lo


def _dot1(a, b, dims=_NN):
    return _dg(a.astype(BF16), b.astype(BF16), dims)


def _dot3(a, b, dims=_NN):
    ah, al = _split2(a)
    bh, bl = _split2(b)
    return _dg(ah, bh, dims) + (_dg(ah, bl, dims) + _dg(al, bh, dims))


_dot_inv = _dot1


def _dot01(m, x):
    hi = x.astype(BF16)
    r = x - hi.astype(F32)
    mid = r.astype(BF16)
    lo = (r - mid.astype(F32)).astype(BF16)
    return _dg(m, hi, _NN) + (_dg(m, mid, _NN) + _dg(m, lo, _NN))


def _dot01_2(m, x):
    hi, lo = _split2(x)
    return _dg(m, hi, _NN) + _dg(m, lo, _NN)


def _sigmoid(x):
    return 1.0 / (1.0 + jnp.exp(-x))


def _silu(x):
    return x * _sigmoid(x)


def _gelu(x):
    return 0.5 * x * (1.0 + lax.erf(x * (2.0 ** -0.5)))


def _softplus(x):
    return jnp.maximum(x, 0.0) + jnp.log1p(jnp.exp(-jnp.abs(x)))


def _mod_row_of_tile(i, tm):
    return jnp.maximum(0, (i * tm - N_CTX_TOK) // DEC_SEQ + 1)


assert N_LAT_UNITS == N_CTX_UNITS


def _scan_unit(g):
    return jnp.where(g % 2 == 0, N_CTX_UNITS + g // 2, g // 2)


def _is_ctx_step(g):
    return g % 2 == 1


def _state_out_spec(layer):
    return pl.BlockSpec((SEQ_PER_UNIT, None, 2, None, HEAD_DIM, HEAD_DIM),
                        lambda h, g: (g // 2, layer, 0, h, 0, 0))


def _state_in_spec(layer):
    return pl.BlockSpec((None, None, 2, None, HEAD_DIM, HEAD_DIM),
                        lambda h, g: (g // 2, layer, 0, h, 0, 0))


_STATE_SHAPE = jax.ShapeDtypeStruct((BATCH, DEPTH, 2, HEADS, HEAD_DIM, HEAD_DIM), F32)


def _without_arg(kernel, pos):
    def wrapped(*refs):
        return kernel(*refs[:pos], *refs[pos + 1:])
    return wrapped


def _cparams(sem):
    return pltpu.CompilerParams(dimension_semantics=sem, vmem_limit_bytes=VMEM_LIMIT)


def _mod_kernel(c_ref, w_ref, b_ref, o_ref):
    s = _silu(c_ref[...])
    hi, lo = _split2(s)
    w = w_ref[...].astype(BF16)
    o_ref[...] = _dg(hi, w, _NN) + _dg(lo, w, _NN) + b_ref[...]


def _mod_call(cvec, w_mod, b_mod):
    tn = 1024
    n = 6 * D_MODEL
    return pl.pallas_call(
        _mod_kernel,
        grid=(DEPTH, n // tn),
        in_specs=[pl.BlockSpec((MOD_ROWS, D_MODEL), lambda l, j: (0, 0)),
                  pl.BlockSpec((None, D_MODEL, tn), lambda l, j: (l, 0, j)),
                  pl.BlockSpec((None, 1, tn), lambda l, j: (l, 0, j))],
        out_specs=pl.BlockSpec((None, MOD_ROWS, tn), lambda l, j: (l, 0, j)),
        out_shape=jax.ShapeDtypeStruct((DEPTH, MOD_ROWS, n), F32),
        compiler_params=_cparams(("arbitrary", "arbitrary")),
        name="modulation",
    )(cvec, w_mod, b_mod.reshape(DEPTH, 1, n))


def _adaln(x, g, shift, scale):
    y = x * lax.rsqrt(jnp.mean(x * x, axis=-1, keepdims=True) + EPS) * g
    return y * (1.0 + scale) + shift


def _residual_specs(x, tm, idx):
    if not isinstance(x, tuple):
        return [pl.BlockSpec((tm, D_MODEL), lambda *g: (idx(*g), 0))], [x]
    n_ctx = N_CTX_TOK // tm
    return ([pl.BlockSpec((tm, D_MODEL), lambda *g: (jnp.minimum(idx(*g), n_ctx - 1), 0)),
             pl.BlockSpec((tm, D_MODEL), lambda *g: (jnp.maximum(idx(*g) - n_ctx, 0), 0))], list(x))


def _residual_tile(refs, tile, tm):
    if len(refs) == 1:
        return refs[0][...]
    return jnp.where(tile < N_CTX_TOK // tm, refs[0][...], refs[1][...])


def _normmod_kernel(*refs, n_x):
    x_refs, (g_ref, sh_ref, sc_ref, o_ref) = refs[:n_x], refs[n_x:]
    x = _residual_tile(x_refs, pl.program_id(0), o_ref.shape[0])
    o_ref[...] = _adaln(x, g_ref[...], sh_ref[...], sc_ref[...]).astype(o_ref.dtype)


def _normmod_call(x, g, mod3, layer, k_shift, k_scale):
    tm = 512
    def mod_spec(k):
        return pl.BlockSpec((None, 1, D_MODEL),
                            lambda i: (layer * MOD_ROWS + _mod_row_of_tile(i, tm), 0, k))
    x_specs, x_args = _residual_specs(x, tm, lambda i: i)
    return pl.pallas_call(
        functools.partial(_normmod_kernel, n_x=len(x_args)),
        grid=(N_TOK // tm,),
        in_specs=x_specs + [pl.BlockSpec((1, D_MODEL), lambda i: (0, 0)), mod_spec(k_shift), mod_spec(k_scale)],
        out_specs=pl.BlockSpec((tm, D_MODEL), lambda i: (i, 0)),
        out_shape=jax.ShapeDtypeStruct((N_TOK, D_MODEL), BF16),
        compiler_params=_cparams(("arbitrary",)),
        name="norm_mod",
    )(*x_args, g.reshape(1, D_MODEL), mod3, mod3)


def _final_norm_kernel(x_ref, g_ref, o_ref):
    x = x_ref[...]
    o_ref[...] = x * lax.rsqrt(jnp.mean(x * x, axis=-1, keepdims=True) + EPS) * g_ref[...]


def _final_norm_call(x, g, row0, n_rows):
    tm = 512
    return pl.pallas_call(
        _final_norm_kernel,
        grid=(n_rows // tm,),
        in_specs=[pl.BlockSpec((tm, D_MODEL), lambda i: (row0 // tm + i, 0)),
                  pl.BlockSpec((1, D_MODEL), lambda i: (0, 0))],
        out_specs=pl.BlockSpec((tm, D_MODEL), lambda i: (i, 0)),
        out_shape=jax.ShapeDtypeStruct((n_rows, D_MODEL), F32),
        compiler_params=_cparams(("arbitrary",)),
        name="final_norm",
    )(x, g.reshape(1, D_MODEL))


def _mm_kernel(x_ref, w_ref, o_ref, wbf_ref, *, act, w_transposed):
    @pl.when(pl.program_id(1) == 0)
    def _cast_weights():
        wbf_ref[...] = w_ref[...].reshape(wbf_ref.shape).astype(BF16)

    acc = lax.dot_general(x_ref[...], wbf_ref[...], _NT if w_transposed else _NN, preferred_element_type=F32)
    if act == "sigmoid":
        acc = _sigmoid(acc)
    elif act == "relu2":
        acc = jnp.square(jnp.maximum(acc, 0.0))
    if len(o_ref.shape) == 3:
        for c in range(o_ref.shape[0]):
            o_ref[c] = acc[:, c * LANES:(c + 1) * LANES].astype(o_ref.dtype)
    else:
        o_ref[...] = acc.astype(o_ref.dtype)


def _mm_call(x, w_all, layer, *, col0, n, act, out_dtype, tm, tn, name, w_transposed=False,
             lane_tile_major=False):
    m, k = x.shape
    assert n % tn == 0 and m % tm == 0
    if w_transposed:
        assert col0 % 8 == 0
        w_spec = pl.BlockSpec((pl.Element(1), pl.Element(tn), pl.Element(k)),
                              lambda j, i: (layer, pl.multiple_of(col0 + j * tn, 8), 0))
        w_scratch = pltpu.VMEM((tn, k), BF16)
    else:
        assert col0 % tn == 0
        w_spec = pl.BlockSpec((None, k, tn), lambda j, i: (layer, 0, col0 // tn + j))
        w_scratch = pltpu.VMEM((k, tn), BF16)
    if lane_tile_major:
        out_spec = pl.BlockSpec((tn // LANES, tm, LANES), lambda j, i: (j, i, 0))
        out_shape = jax.ShapeDtypeStruct((n // LANES, m, LANES), out_dtype)
    else:
        out_spec = pl.BlockSpec((tm, tn), lambda j, i: (i, j))
        out_shape = jax.ShapeDtypeStruct((m, n), out_dtype)
    return pl.pallas_call(
        functools.partial(_mm_kernel, act=act, w_transposed=w_transposed),
        grid=(n // tn, m // tm),
        in_specs=[pl.BlockSpec((tm, k), lambda j, i: (i, 0)), w_spec],
        out_specs=out_spec,
        out_shape=out_shape,
        scratch_shapes=[w_scratch],
        compiler_params=_cparams(("arbitrary", "arbitrary")),
        name=name,
    )(x, w_all)


def _mm_bf16_kernel(x_ref, w_ref, r_ref, g_ref, o_ref):
    acc = jnp.dot(x_ref[...], w_ref[...], preferred_element_type=F32)
    o_ref[...] = r_ref[...] + g_ref[...] * acc


def _mm_resid_call(x, w, resid, mod3, layer, k_gate, *, tm, tn, name):
    m, k = x.shape
    n = w.shape[2]
    per_tile = D_MODEL // tn
    return pl.pallas_call(
        _mm_bf16_kernel,
        grid=(m // tm, n // tn),
        in_specs=[pl.BlockSpec((tm, k), lambda i, j: (i, 0)),
                  pl.BlockSpec((None, k, tn), lambda i, j: (layer, 0, j)),
                  pl.BlockSpec((tm, tn), lambda i, j: (i, j)),
                  pl.BlockSpec((None, 1, tn),
                               lambda i, j: (layer * MOD_ROWS + _mod_row_of_tile(i, tm), 0,
                                             k_gate * per_tile + j))],
        out_specs=pl.BlockSpec((tm, tn), lambda i, j: (i, j)),
        out_shape=jax.ShapeDtypeStruct((m, n), F32),
        compiler_params=_cparams(("arbitrary", "arbitrary")),
        name=name,
    )(x, w, resid, mod3)


def _outproj_norm_kernel(*refs, n_x):
    m_ref, w_ref = refs[:2]
    x_refs = refs[2:2 + n_x]
    gate_ref, g_ref, sh_ref, sc_ref, xo_ref, h_ref, wbf_ref = refs[2 + n_x:]

    @pl.when(pl.program_id(0) == 0)
    def _cast_weights():
        wbf_ref[...] = w_ref[...].astype(BF16)

    acc = jnp.dot(m_ref[...], wbf_ref[...], preferred_element_type=F32)
    xn = _residual_tile(x_refs, pl.program_id(0), xo_ref.shape[0]) + gate_ref[...] * acc
    xo_ref[...] = xn
    h_ref[...] = _adaln(xn, g_ref[...], sh_ref[...], sc_ref[...]).astype(h_ref.dtype)


def _outproj_norm_call(merged, w_out, x, norm_g, mod3, layer):
    tm = 256 if isinstance(x, tuple) else 512
    def mod_spec(k):
        return pl.BlockSpec((None, 1, D_MODEL),
                            lambda i: (layer * MOD_ROWS + _mod_row_of_tile(i, tm), 0, k))
    row = pl.BlockSpec((tm, D_MODEL), lambda i: (i, 0))
    x_specs, x_args = _residual_specs(x, tm, lambda i: i)
    return pl.pallas_call(
        functools.partial(_outproj_norm_kernel, n_x=len(x_args)),
        grid=(N_TOK // tm,),
        in_specs=[row,
                  pl.BlockSpec((None, D_MODEL, D_MODEL), lambda i: (layer, 0, 0), pipeline_mode=pl.Buffered(1))]
                 + x_specs + [mod_spec(2), pl.BlockSpec((1, D_MODEL), lambda i: (0, 0)), mod_spec(3), mod_spec(4)],
        out_specs=[row, row],
        out_shape=[jax.ShapeDtypeStruct((N_TOK, D_MODEL), F32),
                   jax.ShapeDtypeStruct((N_TOK, D_MODEL), BF16)],
        scratch_shapes=[pltpu.VMEM((D_MODEL, D_MODEL), BF16)],
        compiler_params=_cparams(("arbitrary",)),
        name="out_proj_norm",
    )(merged, w_out, *x_args, mod3, norm_g.reshape(1, D_MODEL), mod3, mod3)


def _ffn_down_norm_kernel(x_ref, w_ref, r_ref, gate_ref, g_ref, sh_ref, sc_ref, o_ref, h_ref, row_ref):
    j = pl.program_id(1)
    tn = o_ref.shape[1]
    acc = jnp.dot(x_ref[...], w_ref[...], preferred_element_type=F32)
    xn = r_ref[...] + gate_ref[...] * acc
    o_ref[...] = xn
    row_ref[:, pl.ds(pl.multiple_of(j * tn, tn), tn)] = xn

    @pl.when(j == pl.num_programs(1) - 1)
    def _next_layer_norm():
        h_ref[...] = _adaln(row_ref[...], g_ref[...], sh_ref[...], sc_ref[...]).astype(h_ref.dtype)


def _ffn_down_norm_call(up, w_bf, resid, mod3, layer, next_norm_g):
    tm, tn = 512, 512
    m, k = up.shape
    per_tile = D_MODEL // tn
    def mod_row(i):
        return _mod_row_of_tile(i, tm)
    def next_mod(kk):
        return pl.BlockSpec((None, 1, D_MODEL), lambda i, j: ((layer + 1) * MOD_ROWS + mod_row(i), 0, kk))
    return pl.pallas_call(
        _ffn_down_norm_kernel,
        grid=(m // tm, D_MODEL // tn),
        in_specs=[pl.BlockSpec((tm, k), lambda i, j: (i, 0)),
                  pl.BlockSpec((None, k, tn), lambda i, j: (layer, 0, j)),
                  pl.BlockSpec((tm, tn), lambda i, j: (i, j)),
                  pl.BlockSpec((None, 1, tn), lambda i, j: (layer * MOD_ROWS + mod_row(i), 0, 5 * per_tile + j)),
                  pl.BlockSpec((1, D_MODEL), lambda i, j: (0, 0)), next_mod(0), next_mod(1)],
        out_specs=[pl.BlockSpec((tm, tn), lambda i, j: (i, j)),
                   pl.BlockSpec((tm, D_MODEL), lambda i, j: (i, 0))],
        out_shape=[jax.ShapeDtypeStruct((m, D_MODEL), F32),
                   jax.ShapeDtypeStruct((m, D_MODEL), BF16)],
        scratch_shapes=[pltpu.VMEM((tm, D_MODEL), F32)],
        compiler_params=_cparams(("arbitrary", "arbitrary")),
        name="ffn_down_norm",
    )(up, w_bf, resid, mod3, next_norm_g.reshape(1, D_MODEL), mod3, mod3)


def _gated_merge_kernel(h_ref, wga_ref, wgb_ref, wgc_ref, oa_ref, ob_ref, oc_ref, wa_ref, wb_ref, wc_ref,
                        o_ref, wg_bf, wbr_bf):
    @pl.when(pl.program_id(1) == 0)
    def _cast_weights():
        for t, ref in enumerate((wga_ref, wgb_ref, wgc_ref)):
            wg_bf[t] = ref[...].reshape(wg_bf.shape[1:]).astype(BF16)
        for t, ref in enumerate((wa_ref, wb_ref, wc_ref)):
            wbr_bf[t] = ref[...].astype(BF16)

    def rows_of(ref):
        return jnp.concatenate([ref[t] for t in range(ref.shape[0])], axis=1)

    h = h_ref[...]
    acc = None
    for t, br_ref in enumerate((oa_ref, ob_ref, oc_ref)):
        gate = _sigmoid(lax.dot_general(h, wg_bf[t], _NT, preferred_element_type=F32))
        term = gate * jnp.dot(rows_of(br_ref), wbr_bf[t], preferred_element_type=F32)
        acc = term if acc is None else acc + term
    o_ref[...] = acc.astype(o_ref.dtype)


def _gated_merge_call(h, w_in_t, o_a, o_b, o_c, w_a, w_b, w_c, layer):
    tm, tn = 1024, 256
    def gate_w(b):
        return pl.BlockSpec((pl.Element(1), pl.Element(tn), pl.Element(D_MODEL)),
                            lambda j, i: (layer, pl.multiple_of(OFF_GATES + b * D_MODEL + j * tn, 8), 0))
    br = pl.BlockSpec((HEADS, tm, HEAD_DIM), lambda j, i: (0, i, 0))
    wt = pl.BlockSpec((None, HG_F, tn), lambda j, i: (layer, 0, j))
    return pl.pallas_call(
        _gated_merge_kernel,
        grid=(D_MODEL // tn, N_TOK // tm),
        in_specs=[pl.BlockSpec((tm, D_MODEL), lambda j, i: (i, 0)), gate_w(0), gate_w(1), gate_w(2),
                  br, br, br, wt, wt, wt],
        out_specs=pl.BlockSpec((tm, tn), lambda j, i: (i, j)),
        out_shape=jax.ShapeDtypeStruct((N_TOK, D_MODEL), BF16),
        scratch_shapes=[pltpu.VMEM((3, tn, D_MODEL), BF16), pltpu.VMEM((3, HG_F, tn), BF16)],
        compiler_params=_cparams(("arbitrary", "arbitrary")),
        name="gated_merge",
    )(h, w_in_t, w_in_t, w_in_t, o_a, o_b, o_c, w_a, w_b, w_c)


def _head_rmsnorm_gate(o, g, og):
    y = o * lax.rsqrt(jnp.mean(o * o, axis=-1, keepdims=True) + EPS) * g
    return y * _silu(og)


def _hgrn_kernel(q_ref, i_ref, og_ref, ff_ref, fb_ref, lb_ref, g_ref, s0_ref, o_ref, s_ref,
                 acc_ref, b_ref, stc_ref, km_ref):
    is_ctx = _is_ctx_step(pl.program_id(1))
    nblk = UNIT // HG_BLOCK
    nsub = HG_BLOCK // HG_SUB
    r = lax.broadcasted_iota(jnp.int32, (HG_BLOCK, HG_BLOCK), 0)
    c = lax.broadcasted_iota(jnp.int32, (HG_BLOCK, HG_BLOCK), 1)
    same = (r // HG_SUB) == (c // HG_SUB)
    row_sub = lax.broadcasted_iota(jnp.int32, (HG_BLOCK, HEAD_DIM), 0) // HG_SUB
    tri = (jnp.logical_and(same, c <= r), jnp.logical_and(same, c >= r))
    tri_bf = [jnp.where(t, 1.0, 0.0).astype(BF16) for t in tri]
    z_refs = (ff_ref, fb_ref)
    inst = [(blk, d) for blk in range(nblk) for d in range(2)]

    def expand(x):
        return jnp.broadcast_to(x[:, None, :], (nsub, HG_SUB, HEAD_DIM)).reshape(HG_BLOCK, HEAD_DIM)

    def blk_rows(blk):
        return slice(blk * HG_BLOCK, (blk + 1) * HG_BLOCK)

    qs = [_silu(q_ref[blk_rows(blk), :]) for blk in range(nblk)]
    vs = [i_ref[blk_rows(blk), :] for blk in range(nblk)]
    ks, lfs = [], []
    for blk, d in inst:
        lb = lb_ref[d:d + 1, :]
        f = lb + (1.0 - lb) * _sigmoid(z_refs[d][blk_rows(blk), :])
        lfs.append(jnp.log(f))
        ks.append(1.0 - f)
    bs = [_dot01_2(tri_bf[d], lf) for (blk, d), lf in zip(inst, lfs)]
    tots, qts, kts, qds = [], [], [], []
    for n, (blk, d) in enumerate(inst):
        b = bs[n]
        b_ref[n] = b
        tot = b_ref[n, pl.ds((HG_SUB - 1) if d == 0 else 0, nsub, stride=HG_SUB), :]
        mid_f = expand(b_ref[n, pl.ds(HG_SUB // 2, nsub, stride=HG_SUB), :])
        tots.append(tot)
        qts.append(qs[blk] * jnp.exp(b - mid_f))
        kts.append(ks[n] * jnp.exp(mid_f - b))
        qds.append((qs[blk] * jnp.exp(b)).astype(BF16))
        kd = ks[n] * jnp.exp(expand(tot) - b)
        for s in range(nsub):
            km_ref[n, :, s * HEAD_DIM:(s + 1) * HEAD_DIM] = jnp.where(row_sub == s, kd, 0.0).astype(BF16)
    scs = [jnp.where(tri[d], _dot1(qt, kt, _NT), 0.0) for (blk, d), qt, kt in zip(inst, qts, kts)]
    uts = [_dg(vs[blk].astype(BF16), km_ref[n], _TN) for n, (blk, d) in enumerate(inst)]
    outs = [_dot1(sc, vs[blk]) for (blk, d), sc in zip(inst, scs)]
    for d in range(2):
        st = s0_ref[d].T
        for blk in (range(nblk) if d == 0 else range(nblk - 1, -1, -1)):
            n = inst.index((blk, d))
            st = jnp.where(is_ctx, 0.0, st)
            for s in (range(nsub) if d == 0 else range(nsub - 1, -1, -1)):
                stc_ref[n, s] = st.T.astype(BF16)
                st = st * jnp.exp(tots[n][s:s + 1, :]) + uts[n][:, s * HEAD_DIM:(s + 1) * HEAD_DIM]
            s_ref[blk, d] = st.T
    for n, (blk, d) in enumerate(inst):
        o_int = [_dg(qds[n][s * HG_SUB:(s + 1) * HG_SUB], stc_ref[n, s], _NN) for s in range(nsub)]
        acc_ref[d, blk_rows(blk), :] = outs[n] + jnp.concatenate(o_int, axis=0)
    o_ref[...] = _head_rmsnorm_gate(acc_ref[0] + acc_ref[1], g_ref[...], og_ref[...]).astype(o_ref.dtype)


def _hgrn_call(z, lb, onorm_g, state, layer, new_states):
    def col(off):
        return pl.BlockSpec((None, UNIT, HEAD_DIM), lambda h, g: (off // HEAD_DIM + h, _scan_unit(g), 0))
    in_specs = [col(OFF_HQ), col(OFF_HI), col(OFF_HG), col(OFF_HFF), col(OFF_HFB),
                pl.BlockSpec((2, HEAD_DIM), lambda h, g: (0, h)),
                pl.BlockSpec((1, HEAD_DIM), lambda h, g: (0, 0)),
                _state_in_spec(layer)]
    args = [z, z, z, z, z, lb, onorm_g.reshape(1, HEAD_DIM), state]
    kernel, aliases = _hgrn_kernel, {}
    if new_states is not None:
        kernel, aliases = _without_arg(_hgrn_kernel, len(args)), {len(args): 1}
        in_specs.append(pl.BlockSpec(memory_space=pl.ANY))
        args.append(new_states)
    return pl.pallas_call(
        kernel,
        grid=(HEADS, N_UNITS),
        in_specs=in_specs,
        out_specs=[pl.BlockSpec((None, UNIT, HEAD_DIM), lambda h, g: (h, _scan_unit(g), 0)), _state_out_spec(layer)],
        out_shape=[jax.ShapeDtypeStruct((HEADS, N_TOK, HEAD_DIM), BF16), _STATE_SHAPE],
        input_output_aliases=aliases,
        scratch_shapes=[pltpu.VMEM((2, UNIT, HEAD_DIM), F32),
                        pltpu.VMEM((HG_INST, HG_BLOCK, HEAD_DIM), F32),
                        pltpu.VMEM((HG_INST, HG_BLOCK // HG_SUB, HEAD_DIM, HEAD_DIM), BF16),
                        pltpu.VMEM((HG_INST, HG_BLOCK, (HG_BLOCK // HG_SUB) * HEAD_DIM), BF16)],
        compiler_params=_cparams(("arbitrary", "arbitrary")),
        name="hgrn2_scan",
    )(*args)


def _gmlp_kernel(u_ref, v_ref, vn_ref, ws_ref, bs_ref, o_ref):
    tm = u_ref.shape[1]
    for g in range(CM_GROUPS):
        cols = slice(g * HEAD_DIM, (g + 1) * HEAD_DIM)
        vg = _gelu(v_ref[g])
        vg = vg * lax.rsqrt(jnp.mean(vg * vg, axis=-1, keepdims=True) + EPS) * vn_ref[:, cols]
        ug = _gelu(u_ref[g])
        w = ws_ref[g]
        bias = bs_ref[:, g:g + 1]
        for ch in range(tm // CM_CHUNK):
            rows = slice(ch * CM_CHUNK, (ch + 1) * CM_CHUNK)
            s = _dot3(w, vg[rows]) + bias
            o_ref[g, rows, :] = (ug[rows] * s).astype(o_ref.dtype)


def _gmlp_call(z, vnorm_g, ws, bs_t):
    tm = 512
    return pl.pallas_call(
        _gmlp_kernel,
        grid=(N_TOK // tm,),
        in_specs=[pl.BlockSpec((CM_GROUPS, tm, HEAD_DIM), lambda i: (OFF_CU // CM_W, i, 0)),
                  pl.BlockSpec((CM_GROUPS, tm, HEAD_DIM), lambda i: (OFF_CV // CM_W, i, 0)),
                  pl.BlockSpec((1, CM_W), lambda i: (0, 0)),
                  pl.BlockSpec((CM_GROUPS, CM_CHUNK, CM_CHUNK), lambda i: (0, 0, 0)),
                  pl.BlockSpec((CM_CHUNK, CM_GROUPS), lambda i: (0, 0))],
        out_specs=pl.BlockSpec((CM_GROUPS, tm, HEAD_DIM), lambda i: (0, i, 0)),
        out_shape=jax.ShapeDtypeStruct((CM_GROUPS, N_TOK, HEAD_DIM), BF16),
        compiler_params=_cparams(("arbitrary",)),
        name="chunk_gmlp",
    )(z, z, vnorm_g.reshape(1, CM_W), ws, bs_t)


GP_BLOCK = 256


def _gdn_gates_kernel(ab_ref, alog_ref, dt_ref, col_ref, rowt_ref):
    ab = ab_ref[...]
    lane = lax.broadcasted_iota(jnp.int32, ab.shape, 1)
    g = jnp.where(lane < 2 * HEADS, -jnp.exp(alog_ref[...]) * _softplus(ab + dt_ref[...]), 0.0)
    r = lax.broadcasted_iota(jnp.int32, (GP_BLOCK, GP_BLOCK), 0)
    c = lax.broadcasted_iota(jnp.int32, (GP_BLOCK, GP_BLOCK), 1)
    same = (r // GDN_CHUNK) == (c // GDN_CHUNK)
    tri_f = jnp.where(jnp.logical_and(same, c <= r), 1.0, 0.0).astype(BF16)
    tri_b = jnp.where(jnp.logical_and(same, c >= r), 1.0, 0.0).astype(BF16)
    cf = _dot01(tri_f, g)
    cb = _dot01(tri_b, g)
    col = jnp.where(lane < HEADS, cf, jnp.where(lane < 2 * HEADS, cb, _sigmoid(ab)))
    col_ref[...] = col
    rowt_ref[...] = col.T


def _gdn_gates_call(ab, a_log, dt_bias):
    pad = lambda t: jnp.pad(t.reshape(1, 2 * HEADS), ((0, 0), (0, LANES - 2 * HEADS)))
    return pl.pallas_call(
        _gdn_gates_kernel,
        grid=(N_TOK // GP_BLOCK,),
        in_specs=[pl.BlockSpec((GP_BLOCK, LANES), lambda i: (i, 0)),
                  pl.BlockSpec((1, LANES), lambda i: (0, 0)),
                  pl.BlockSpec((1, LANES), lambda i: (0, 0))],
        out_specs=[pl.BlockSpec((GP_BLOCK, LANES), lambda i: (i, 0)),
                   pl.BlockSpec((LANES, GP_BLOCK), lambda i: (0, i))],
        out_shape=[jax.ShapeDtypeStruct((N_TOK, LANES), F32),
                   jax.ShapeDtypeStruct((LANES, N_TOK), F32)],
        compiler_params=_cparams(("arbitrary",)),
        name="gdn_gates",
    )(ab, pad(a_log), pad(dt_bias))


CONV_PAD = 72
GDN_AQ = HEAD_DIM + GDN_CHUNK
GDN_STEP = 16 * GDN_CHUNK


def _gdn_kernel(qr_ref, kr_ref, vr_ref, og_ref, col_ref, rowt_ref, cwq_ref, cwk_ref, cwv_ref,
                g_ref, s0_ref, o_ref, s_ref,
                xp_ref, q_s, k_s, v_s, o_s, b_s, aq_s):
    is_ctx = _is_ctx_step(pl.program_id(1))
    head = pl.program_id(0)
    n_chunks = UNIT // GDN_CHUNK
    chunks_per_seq = SEQ // GDN_CHUNK

    t = lax.broadcasted_iota(jnp.int32, (UNIT, 1), 0)
    period = jnp.where(is_ctx, SEQ, GRID_W)
    pos = jnp.bitwise_and(t, period - 1)
    ok_left = pos != 0
    ok_right = pos != period - 1
    zeros_pad = jnp.zeros((CONV_PAD, HEAD_DIM), F32)
    xp_ref[0:CONV_PAD, :] = zeros_pad
    xp_ref[CONV_PAD + UNIT:CONV_PAD + UNIT + CONV_PAD, :] = zeros_pad

    def conv_silu(x_ref, w_ref):
        xp_ref[CONV_PAD:CONV_PAD + UNIT, :] = x_ref[...]
        acc = jnp.zeros((UNIT, HEAD_DIM), F32)
        for i in range(CONV_K):
            for j in range(CONV_K):
                w = w_ref[CONV_K * i + j:CONV_K * i + j + 1, :]
                if i != CONV_K // 2:
                    w = jnp.where(is_ctx, 0.0, w)
                start = CONV_PAD + (i - 1) * GRID_W + (j - 1)
                xs = xp_ref[start:start + UNIT, :]
                if j == 0:
                    xs = jnp.where(ok_left, xs, 0.0)
                elif j == CONV_K - 1:
                    xs = jnp.where(ok_right, xs, 0.0)
                acc = acc + xs * w
        return _silu(acc)

    def l2norm(x):
        return x * lax.rsqrt(jnp.sum(x * x, axis=-1, keepdims=True) + EPS)

    q_s[...] = l2norm(conv_silu(qr_ref, cwq_ref)) * (HEAD_DIM ** -0.5)
    k_s[...] = l2norm(conv_silu(kr_ref, cwk_ref))
    v_s[...] = conv_silu(vr_ref, cwv_ref)

    rr = lax.broadcasted_iota(jnp.int32, (GDN_CHUNK, GDN_CHUNK), 0)
    cc = lax.broadcasted_iota(jnp.int32, (GDN_CHUNK, GDN_CHUNK), 1)
    eye = jnp.where(rr == cc, 1.0, 0.0)
    same_blk = [(rr // b) == (cc // b) for b in (8, 16, 32, 64)]
    lane = lax.broadcasted_iota(jnp.int32, (GDN_STEP, LANES), 1)
    sub8 = lax.broadcasted_iota(jnp.int32, (HEADS, GDN_STEP), 0)

    def pick(x, j):
        return jnp.sum(jnp.where(lane[:x.shape[0]] == j, x, 0.0), axis=-1, keepdims=True)

    incl = (cc <= rr, cc >= rr)
    strict = (cc < rr, cc > rr)
    off_masks = [jnp.logical_and(same_blk[lvl], jnp.logical_not(same_blk[lvl - 1]))
                 for lvl in range(1, len(same_blk))]

    def phase1(p, carry):
        rows = pl.ds(pl.multiple_of(p * GDN_STEP, GDN_STEP), GDN_STEP)
        q2, k2, v2 = q_s[rows, :], k_s[rows, :], v_s[rows, :]
        col = col_ref[rows, :]
        gcols = [pick(col, d * HEADS + head) for d in range(2)]
        betas = [pick(col, (2 + d) * HEADS + head) for d in range(2)]
        grows = [jnp.sum(jnp.where(sub8 == head, rowt_ref[d * HEADS:(d + 1) * HEADS, rows], 0.0),
                         axis=0, keepdims=True) for d in range(2)]
        inst = []
        for half in range(GDN_STEP // GDN_CHUNK):
            sl = slice(half * GDN_CHUNK, (half + 1) * GDN_CHUNK)
            qc, kc, vc = q2[sl], k2[sl], v2[sl]
            kk = _dot1(kc, kc, _NT)
            qk = _dot1(qc, kc, _NT)
            for d in range(2):
                gcol, beta = gcols[d][sl], betas[d][sl]
                decay = jnp.where(incl[d], jnp.exp(gcol - grows[d][:, sl]), 0.0)
                m = jnp.where(strict[d], beta * kk * decay, 0.0)
                inst.append((half, d, qc, kc, vc, qk, gcol, beta, decay, m))
        ms = [t[-1] for t in inst]
        pws = [jnp.where(same_blk[0], -m, 0.0) for m in ms]
        tinvs = [eye + pw for pw in pws]
        for _ in range(2):
            pws = [_dot_inv(pw, pw) for pw in pws]
            tinvs = [t + _dot_inv(t, pw) for t, pw in zip(tinvs, pws)]
        for mask in off_masks:
            tmp = [_dot_inv(jnp.where(mask, m, 0.0), t) for m, t in zip(ms, tinvs)]
            tinvs = [t - _dot_inv(t, x) for t, x in zip(tinvs, tmp)]
        uws = [_dot1(tinv, jnp.concatenate([vc * beta, kc * (beta * jnp.exp(gcol))], axis=1)).astype(BF16)
               for (half, d, qc, kc, vc, qk, gcol, beta, decay, m), tinv in zip(inst, tinvs)]
        bas, ows = [], []
        for (half, d, qc, kc, vc, qk, gcol, beta, decay, m), uw in zip(inst, uws):
            glast = gcol[GDN_CHUNK - 1:GDN_CHUNK] if d == 0 else gcol[0:1]
            bas.append(_dg((kc * jnp.exp(glast - gcol)).astype(BF16), uw, _TN))
            ows.append(_dg((qk * decay).astype(BF16), uw, _NN))
        for (half, d, qc, kc, vc, qk, gcol, beta, decay, m), ba, ow in zip(inst, bas, ows):
            chunk0 = p * GDN_STEP + half * GDN_CHUNK
            o_s[d, pl.ds(pl.multiple_of(chunk0, GDN_CHUNK), GDN_CHUNK), :] = ow[:, :HEAD_DIM]
            b_s[d, pl.ds(pl.multiple_of(2 * chunk0, HEAD_DIM), HEAD_DIM), :] = ba[:, :HEAD_DIM]
            aq_s[d, pl.ds(pl.multiple_of(3 * chunk0, GDN_AQ), GDN_AQ), :] = jnp.concatenate(
                [ba[:, HEAD_DIM:], qc * jnp.exp(gcol) - ow[:, HEAD_DIM:]], axis=0).astype(BF16)
        return carry

    lax.fori_loop(0, UNIT // GDN_STEP, phase1, 0)

    lane1 = lax.broadcasted_iota(jnp.int32, (1, LANES), 1)

    def advance(chains, states):
        rs = [_dg(aq_s[d, pl.ds(pl.multiple_of(ch * GDN_AQ, GDN_AQ), GDN_AQ), :], s.astype(BF16), _NN)
              for (d, ch), s in zip(chains, states)]
        new = []
        for (d, ch), s, r in zip(chains, states, rs):
            tile0 = pl.multiple_of(ch * GDN_CHUNK + (GDN_CHUNK - 8 if d == 0 else 0), 8)
            last = col_ref[pl.ds(tile0, 8), :]
            last = last[7:8] if d == 0 else last[0:1]
            glast = jnp.sum(jnp.where(lane1 == d * HEADS + head, last, 0.0), axis=-1, keepdims=True)
            b = b_s[d, pl.ds(pl.multiple_of(ch * HEAD_DIM, HEAD_DIM), HEAD_DIM), :]
            new.append(s * jnp.exp(glast) + (b - r[:HEAD_DIM]))
            rows = pl.ds(pl.multiple_of(ch * GDN_CHUNK, GDN_CHUNK), GDN_CHUNK)
            o_s[d, rows, :] = o_s[d, rows, :] + r[HEAD_DIM:]
        return tuple(new)

    @pl.when(is_ctx)
    def _context_unit():
        def body(i, states):
            chains = [(d, sq * chunks_per_seq + (i if d == 0 else chunks_per_seq - 1 - i))
                      for sq in range(SEQ_PER_UNIT) for d in range(2)]
            return advance(chains, states)
        zero = jnp.zeros((HEAD_DIM, HEAD_DIM), F32)
        final = lax.fori_loop(0, chunks_per_seq, body, (zero,) * (2 * SEQ_PER_UNIT))
        for sq in range(SEQ_PER_UNIT):
            for d in range(2):
                s_ref[sq, d] = final[2 * sq + d]

    @pl.when(jnp.logical_not(is_ctx))
    def _latent_unit():
        def body(i, states):
            return advance([(0, i), (1, n_chunks - 1 - i)], states)
        final = lax.fori_loop(0, n_chunks, body, (s0_ref[0], s0_ref[1]))
        for sq in range(SEQ_PER_UNIT):
            for d in range(2):
                s_ref[sq, d] = final[d]

    o_ref[...] = _head_rmsnorm_gate(o_s[0] + o_s[1], g_ref[...], og_ref[...]).astype(o_ref.dtype)


def _gdn_call(z, col, rowt, conv_w, onorm_g, state, layer, new_states):
    def zcol(off):
        return pl.BlockSpec((None, UNIT, HEAD_DIM), lambda h, g: (off // HEAD_DIM + h, _scan_unit(g), 0))
    def wcol(part):
        return pl.BlockSpec((CONV_K * CONV_K, HEAD_DIM), lambda h, g: (0, part * HEADS + h))
    scr = lambda *shape: pltpu.VMEM(shape, F32)
    in_specs = [zcol(OFF_GQ), zcol(OFF_GK), zcol(OFF_GV), zcol(OFF_GG),
                pl.BlockSpec((UNIT, LANES), lambda h, g: (_scan_unit(g), 0)),
                pl.BlockSpec((LANES, UNIT), lambda h, g: (0, _scan_unit(g))),
                wcol(0), wcol(1), wcol(2),
                pl.BlockSpec((1, HEAD_DIM), lambda h, g: (0, 0)),
                _state_in_spec(layer)]
    args = [z, z, z, z, col, rowt, conv_w, conv_w, conv_w, onorm_g.reshape(1, HEAD_DIM), state]
    kernel, aliases = _gdn_kernel, {}
    if new_states is not None:
        kernel, aliases = _without_arg(_gdn_kernel, len(args)), {len(args): 1}
        in_specs.append(pl.BlockSpec(memory_space=pl.ANY))
        args.append(new_states)
    return pl.pallas_call(
        kernel,
        grid=(HEADS, N_UNITS),
        in_specs=in_specs,
        out_specs=[pl.BlockSpec((None, UNIT, HEAD_DIM), lambda h, g: (h, _scan_unit(g), 0)), _state_out_spec(layer)],
        out_shape=[jax.ShapeDtypeStruct((HEADS, N_TOK, HEAD_DIM), BF16), _STATE_SHAPE],
        input_output_aliases=aliases,
        scratch_shapes=[scr(UNIT + 2 * CONV_PAD, HEAD_DIM),
                        scr(UNIT, HEAD_DIM), scr(UNIT, HEAD_DIM), scr(UNIT, HEAD_DIM),
                        scr(2, UNIT, HEAD_DIM),
                        scr(2, (UNIT // GDN_CHUNK) * HEAD_DIM, HEAD_DIM),
                        pltpu.VMEM((2, (UNIT // GDN_CHUNK) * GDN_AQ, HEAD_DIM), BF16)],
        compiler_params=_cparams(("arbitrary", "arbitrary")),
        name="gdn_scan",
    )(*args)


def kernel(x_prompt, x_sample, c, state_hgrn, state_gdn, c_ctx, norm1_g, norm2_g, w_mod, b_mod, w_in, hg_lb, hg_onorm_g, cm_vnorm_g, cm_ws, cm_bs, gdn_conv, gdn_A_log, gdn_dt_bias, gdn_onorm_g, w_br_hg, w_br_cm, w_br_gdn, w_out, w_ff1, w_ff2, final_g):
    x = (x_prompt.reshape(N_CTX_TOK, D_MODEL), x_sample.reshape(N_LAT_TOK, D_MODEL))
    cvec = jnp.concatenate([c_ctx[None, :], c, jnp.zeros((MOD_ROWS - 1 - DEC_BATCH, D_MODEL), F32)], axis=0)
    mod3 = _mod_call(cvec, w_mod, b_mod).reshape(DEPTH * MOD_ROWS, 1, 6 * D_MODEL)

    lb_all = jnp.cumsum(jax.nn.softmax(hg_lb.astype(F32), axis=0), axis=0)
    lb_all = lb_all - lb_all[:1]

    w_in_t = jnp.swapaxes(w_in, 1, 2)
    w_ff2_bf = w_ff2.astype(BF16)

    new_hg = new_gdn = None
    h = _normmod_call(x, norm1_g[0], mod3, 0, 0, 1)
    for l in range(DEPTH):
        z = _mm_call(h, w_in_t, l, col0=0, n=N_MAIN, act=None, out_dtype=F32, tm=1024, tn=1024,
                     w_transposed=True, lane_tile_major=True, name="in_proj")
        ab = _mm_call(h, w_in_t, l, col0=OFF_AB, n=LANES, act=None, out_dtype=F32, tm=1024, tn=LANES,
                      w_transposed=True, name="in_proj_ab")
        o_a, new_hg = _hgrn_call(z, lb_all[l], hg_onorm_g[l], state_hgrn, l, new_hg)
        o_b = _gmlp_call(z, cm_vnorm_g[l], cm_ws[l], cm_bs[l].T)
        col, rowt = _gdn_gates_call(ab, gdn_A_log[l], gdn_dt_bias[l])
        o_c, new_gdn = _gdn_call(z, col, rowt, gdn_conv[l].reshape(CONV_K * CONV_K, 3 * HG_F),
                                 gdn_onorm_g[l], state_gdn, l, new_gdn)

        merged = _gated_merge_call(h, w_in_t, o_a, o_b, o_c, w_br_hg, w_br_cm, w_br_gdn, l)
        x, h2 = _outproj_norm_call(merged, w_out, x, norm2_g[l], mod3, l)
        up = _mm_call(h2, w_ff1, l, col0=0, n=D_FF, act="relu2", out_dtype=BF16, tm=2048, tn=1024, name="ffn_up")
        if l + 1 < DEPTH:
            x, h = _ffn_down_norm_call(up, w_ff2_bf, x, mod3, l, norm1_g[l + 1])
        else:
            x = _mm_resid_call(up, w_ff2_bf, x, mod3, l, 5, tm=512, tn=512, name="ffn_down")

    y_prompt = _final_norm_call(x, final_g, 0, N_CTX_TOK).reshape(BATCH, SEQ, D_MODEL)
    y_sample = _final_norm_call(x, final_g, N_CTX_TOK, N_LAT_TOK).reshape(DEC_BATCH, DEC_SEQ, D_MODEL)
    return (y_prompt, y_sample, new_hg, new_gdn)
```

```python
import functools

import jax
import jax.numpy as jnp
from jax import lax
from jax.experimental import pallas as pl
from jax.experimental.pallas import tpu as pltpu

F32 = jnp.float32
BF16 = jnp.bfloat16

D_MODEL = 2048
BATCH = 16
SEQ = 256
DEPTH = 2
DEC_BATCH = 4
DEC_SEQ = 1024
GRID_W = 64
EPS = 1e-6
D_FF = 4 * D_MODEL
HEADS = 8
HEAD_DIM = 128
HG_F = HEADS * HEAD_DIM
CM_GROUPS = 8
CM_W = CM_GROUPS * HEAD_DIM
CM_CHUNK = 128
GDN_CHUNK = 64
CONV_K = 3

OFF_HQ, OFF_HI, OFF_HG, OFF_HFF, OFF_HFB = 0, 1024, 2048, 3072, 4096
OFF_CU, OFF_CV = 5120, 6144
OFF_GQ, OFF_GK, OFF_GV, OFF_GG = 7168, 8192, 9216, 10240
OFF_AB = 11264
OFF_GATES = 11296
IN_DIM = 17440
N_MAIN = OFF_AB

N_CTX_TOK = BATCH * SEQ
N_LAT_TOK = DEC_BATCH * DEC_SEQ
N_TOK = N_CTX_TOK + N_LAT_TOK
UNIT = DEC_SEQ
N_CTX_UNITS = N_CTX_TOK // UNIT
N_UNITS = N_TOK // UNIT
N_LAT_UNITS = N_UNITS - N_CTX_UNITS
SEQ_PER_UNIT = UNIT // SEQ

LANES = 128
MOD_ROWS = 8
HG_BLOCK = 256
HG_SUB = 32
HG_INST = 2 * (UNIT // HG_BLOCK)
VMEM_LIMIT = 56 * 1024 * 1024

_NT = (((1,), (1,)), ((), ()))
_TN = (((0,), (0,)), ((), ()))
_NN = (((1,), (0,)), ((), ()))


def _dg(a, b, dims):
    return lax.dot_general(a, b, dims, preferred_element_type=F32)


def _split2(x):
    hi = x.astype(BF16)
    lo = (x - hi.astype(F32)).astype(BF16)
    return hi, lo


def _dot1(a, b, dims=_NN):
    return _dg(a.astype(BF16), b.astype(BF16), dims)


def _dot3(a, b, dims=_NN):
    ah, al = _split2(a)
    bh, bl = _split2(b)
    return _dg(ah, bh, dims) + (_dg(ah, bl, dims) + _dg(al, bh, dims))


_dot_inv = _dot1


def _dot01(m, x):
    hi = x.astype(BF16)
    r = x - hi.astype(F32)
    mid = r.astype(BF16)
    lo = (r - mid.astype(F32)).astype(BF16)
    return _dg(m, hi, _NN) + (_dg(m, mid, _NN) + _dg(m, lo, _NN))


def _dot01_2(m, x):
    hi, lo = _split2(x)
    return _dg(m, hi, _NN) + _dg(m, lo, _NN)


def _sigmoid(x):
    return 1.0 / (1.0 + jnp.exp(-x))


def _silu(x):
    return x * _sigmoid(x)


def _gelu(x):
    return 0.5 * x * (1.0 + lax.erf(x * (2.0 ** -0.5)))


def _softplus(x):
    return jnp.maximum(x, 0.0) + jnp.log1p(jnp.exp(-jnp.abs(x)))


def _mod_row_of_tile(i, tm):
    return jnp.maximum(0, (i * tm - N_CTX_TOK) // DEC_SEQ + 1)


assert N_LAT_UNITS == N_CTX_UNITS


def _scan_unit(g):
    return jnp.where(g % 2 == 0, N_CTX_UNITS + g // 2, g // 2)


def _is_ctx_step(g):
    return g % 2 == 1


def _state_out_spec(layer, first):
    if first:
        return pl.BlockSpec((SEQ_PER_UNIT, DEPTH, 2, None, HEAD_DIM, HEAD_DIM), lambda h, g: (g // 2, 0, 0, h, 0, 0))
    return pl.BlockSpec((SEQ_PER_UNIT, 1, 2, None, HEAD_DIM, HEAD_DIM), lambda h, g: (g // 2, layer, 0, h, 0, 0))


def _zero_other_slots(s_ref, slot):
    for other in range(s_ref.shape[1]):
        if other != slot:
            s_ref[:, other] = jnp.zeros((s_ref.shape[0],) + tuple(s_ref.shape[2:]), F32)


def _state_in_spec(layer):
    return pl.BlockSpec((None, None, 2, None, HEAD_DIM, HEAD_DIM),
                        lambda h, g: (g // 2, layer, 0, h, 0, 0))


_STATE_SHAPE = jax.ShapeDtypeStruct((BATCH, DEPTH, 2, HEADS, HEAD_DIM, HEAD_DIM), F32)


def _without_arg(kernel, pos):
    def wrapped(*refs):
        return kernel(*refs[:pos], *refs[pos + 1:])
    return wrapped


def _cparams(sem):
    return pltpu.CompilerParams(dimension_semantics=sem, vmem_limit_bytes=VMEM_LIMIT)


def _mod_kernel(c_ref, w_ref, b_ref, o_ref):
    s = _silu(c_ref[...])
    hi, lo = _split2(s)
    w = w_ref[...].astype(BF16)
    o_ref[...] = _dg(hi, w, _NN) + _dg(lo, w, _NN) + b_ref[...]


def _mod_call(cvec, w_mod, b_mod):
    tn = 1024
    n = 6 * D_MODEL
    return pl.pallas_call(
        _mod_kernel,
        grid=(DEPTH, n // tn),
        in_specs=[pl.BlockSpec((MOD_ROWS, D_MODEL), lambda l, j: (0, 0)),
                  pl.BlockSpec((None, D_MODEL, tn), lambda l, j: (l, 0, j)),
                  pl.BlockSpec((None, 1, tn), lambda l, j: (l, 0, j))],
        out_specs=pl.BlockSpec((None, MOD_ROWS, tn), lambda l, j: (l, 0, j)),
        out_shape=jax.ShapeDtypeStruct((DEPTH, MOD_ROWS, n), F32),
        compiler_params=_cparams(("arbitrary", "arbitrary")),
        name="modulation",
    )(cvec, w_mod, b_mod.reshape(DEPTH, 1, n))


def _adaln(x, g, shift, scale):
    y = x * lax.rsqrt(jnp.mean(x * x, axis=-1, keepdims=True) + EPS) * g
    return y * (1.0 + scale) + shift


def _residual_specs(x, tm, idx):
    if not isinstance(x, tuple):
        return [pl.BlockSpec((tm, D_MODEL), lambda *g: (idx(*g), 0))], [x]
    n_ctx = N_CTX_TOK // tm
    return ([pl.BlockSpec((tm, D_MODEL), lambda *g: (jnp.minimum(idx(*g), n_ctx - 1), 0)),
             pl.BlockSpec((tm, D_MODEL), lambda *g: (jnp.maximum(idx(*g) - n_ctx, 0), 0))], list(x))


def _residual_tile(refs, tile, tm):
    if len(refs) == 1:
        return refs[0][...]
    return jnp.where(tile < N_CTX_TOK // tm, refs[0][...], refs[1][...])


def _normmod_kernel(*refs, n_x):
    x_refs, (g_ref, sh_ref, sc_ref, o_ref) = refs[:n_x], refs[n_x:]
    x = _residual_tile(x_refs, pl.program_id(0), o_ref.shape[0])
    o_ref[...] = _adaln(x, g_ref[...], sh_ref[...], sc_ref[...]).astype(o_ref.dtype)


def _normmod_call(x, g, mod3, layer, k_shift, k_scale):
    tm = 512
    def mod_spec(k):
        return pl.BlockSpec((None, 1, D_MODEL),
                            lambda i: (layer * MOD_ROWS + _mod_row_of_tile(i, tm), 0, k))
    x_specs, x_args = _residual_specs(x, tm, lambda i: i)
    return pl.pallas_call(
        functools.partial(_normmod_kernel, n_x=len(x_args)),
        grid=(N_TOK // tm,),
        in_specs=x_specs + [pl.BlockSpec((1, D_MODEL), lambda i: (0, 0)), mod_spec(k_shift), mod_spec(k_scale)],
        out_specs=pl.BlockSpec((tm, D_MODEL), lambda i: (i, 0)),
        out_shape=jax.ShapeDtypeStruct((N_TOK, D_MODEL), BF16),
        compiler_params=_cparams(("arbitrary",)),
        name="norm_mod",
    )(*x_args, g.reshape(1, D_MODEL), mod3, mod3)


def _final_norm_kernel(x_ref, g_ref, o_ref):
    x = x_ref[...]
    o_ref[...] = x * lax.rsqrt(jnp.mean(x * x, axis=-1, keepdims=True) + EPS) * g_ref[...]


def _final_norm_call(x, g, row0, n_rows):
    tm = 512
    return pl.pallas_call(
        _final_norm_kernel,
        grid=(n_rows // tm,),
        in_specs=[pl.BlockSpec((tm, D_MODEL), lambda i: (row0 // tm + i, 0)),
                  pl.BlockSpec((1, D_MODEL), lambda i: (0, 0))],
        out_specs=pl.BlockSpec((tm, D_MODEL), lambda i: (i, 0)),
        out_shape=jax.ShapeDtypeStruct((n_rows, D_MODEL), F32),
        compiler_params=_cparams(("arbitrary",)),
        name="final_norm",
    )(x, g.reshape(1, D_MODEL))


def _mm_kernel(x_ref, w_ref, o_ref, wbf_ref, *, act, w_transposed):
    @pl.when(pl.program_id(1) == 0)
    def _cast_weights():
        wbf_ref[...] = w_ref[...].reshape(wbf_ref.shape).astype(BF16)

    acc = lax.dot_general(x_ref[...], wbf_ref[...], _NT if w_transposed else _NN, preferred_element_type=F32)
    if act == "sigmoid":
        acc = _sigmoid(acc)
    elif act == "relu2":
        acc = jnp.square(jnp.maximum(acc, 0.0))
    if len(o_ref.shape) == 3:
        for c in range(o_ref.shape[0]):
            o_ref[c] = acc[:, c * LANES:(c + 1) * LANES].astype(o_ref.dtype)
    else:
        o_ref[...] = acc.astype(o_ref.dtype)


def _mm_call(x, w_all, layer, *, col0, n, act, out_dtype, tm, tn, name, w_transposed=False,
             lane_tile_major=False):
    m, k = x.shape
    assert n % tn == 0 and m % tm == 0
    if w_transposed:
        assert col0 % 8 == 0
        w_spec = pl.BlockSpec((pl.Element(1), pl.Element(tn), pl.Element(k)),
                              lambda j, i: (layer, pl.multiple_of(col0 + j * tn, 8), 0))
        w_scratch = pltpu.VMEM((tn, k), BF16)
    else:
        assert col0 % tn == 0
        w_spec = pl.BlockSpec((None, k, tn), lambda j, i: (layer, 0, col0 // tn + j))
        w_scratch = pltpu.VMEM((k, tn), BF16)
    if lane_tile_major:
        out_spec = pl.BlockSpec((tn // LANES, tm, LANES), lambda j, i: (j, i, 0))
        out_shape = jax.ShapeDtypeStruct((n // LANES, m, LANES), out_dtype)
    else:
        out_spec = pl.BlockSpec((tm, tn), lambda j, i: (i, j))
        out_shape = jax.ShapeDtypeStruct((m, n), out_dtype)
    return pl.pallas_call(
        functools.partial(_mm_kernel, act=act, w_transposed=w_transposed),
        grid=(n // tn, m // tm),
        in_specs=[pl.BlockSpec((tm, k), lambda j, i: (i, 0)), w_spec],
        out_specs=out_spec,
        out_shape=out_shape,
        scratch_shapes=[w_scratch],
        compiler_params=_cparams(("arbitrary", "arbitrary")),
        name=name,
    )(x, w_all)


def _mm_bf16_kernel(x_ref, w_ref, r_ref, g_ref, o_ref):
    acc = jnp.dot(x_ref[...], w_ref[...], preferred_element_type=F32)
    o_ref[...] = r_ref[...] + g_ref[...] * acc


def _mm_resid_call(x, w, resid, mod3, layer, k_gate, *, tm, tn, name):
    m, k = x.shape
    n = w.shape[2]
    per_tile = D_MODEL // tn
    return pl.pallas_call(
        _mm_bf16_kernel,
        grid=(m // tm, n // tn),
        in_specs=[pl.BlockSpec((tm, k), lambda i, j: (i, 0)),
                  pl.BlockSpec((None, k, tn), lambda i, j: (layer, 0, j)),
                  pl.BlockSpec((tm, tn), lambda i, j: (i, j)),
                  pl.BlockSpec((None, 1, tn),
                               lambda i, j: (layer * MOD_ROWS + _mod_row_of_tile(i, tm), 0,
                                             k_gate * per_tile + j))],
        out_specs=pl.BlockSpec((tm, tn), lambda i, j: (i, j)),
        out_shape=jax.ShapeDtypeStruct((m, n), F32),
        compiler_params=_cparams(("arbitrary", "arbitrary")),
        name=name,
    )(x, w, resid, mod3)


def _outproj_norm_kernel(*refs, n_x):
    m_ref, w_ref = refs[:2]
    x_refs = refs[2:2 + n_x]
    gate_ref, g_ref, sh_ref, sc_ref, xo_ref, h_ref, wbf_ref = refs[2 + n_x:]

    @pl.when(pl.program_id(0) == 0)
    def _cast_weights():
        wbf_ref[...] = w_ref[...].astype(BF16)

    acc = jnp.dot(m_ref[...], wbf_ref[...], preferred_element_type=F32)
    xn = _residual_tile(x_refs, pl.program_id(0), xo_ref.shape[0]) + gate_ref[...] * acc
    xo_ref[...] = xn
    h_ref[...] = _adaln(xn, g_ref[...], sh_ref[...], sc_ref[...]).astype(h_ref.dtype)


def _outproj_norm_call(merged, w_out, x, norm_g, mod3, layer):
    tm = 256 if isinstance(x, tuple) else 512
    def mod_spec(k):
        return pl.BlockSpec((None, 1, D_MODEL),
                            lambda i: (layer * MOD_ROWS + _mod_row_of_tile(i, tm), 0, k))
    row = pl.BlockSpec((tm, D_MODEL), lambda i: (i, 0))
    x_specs, x_args = _residual_specs(x, tm, lambda i: i)
    return pl.pallas_call(
        functools.partial(_outproj_norm_kernel, n_x=len(x_args)),
        grid=(N_TOK // tm,),
        in_specs=[row,
                  pl.BlockSpec((None, D_MODEL, D_MODEL), lambda i: (layer, 0, 0), pipeline_mode=pl.Buffered(1))]
                 + x_specs + [mod_spec(2), pl.BlockSpec((1, D_MODEL), lambda i: (0, 0)), mod_spec(3), mod_spec(4)],
        out_specs=[row, row],
        out_shape=[jax.ShapeDtypeStruct((N_TOK, D_MODEL), F32),
                   jax.ShapeDtypeStruct((N_TOK, D_MODEL), BF16)],
        scratch_shapes=[pltpu.VMEM((D_MODEL, D_MODEL), BF16)],
        compiler_params=_cparams(("arbitrary",)),
        name="out_proj_norm",
    )(merged, w_out, *x_args, mod3, norm_g.reshape(1, D_MODEL), mod3, mod3)


def _ffn_down_norm_kernel(x_ref, w_ref, r_ref, gate_ref, g_ref, sh_ref, sc_ref, o_ref, h_ref, row_ref):
    j = pl.program_id(1)
    tn = o_ref.shape[1]
    acc = jnp.dot(x_ref[...], w_ref[...], preferred_element_type=F32)
    xn = r_ref[...] + gate_ref[...] * acc
    o_ref[...] = xn
    row_ref[:, pl.ds(pl.multiple_of(j * tn, tn), tn)] = xn

    @pl.when(j == pl.num_programs(1) - 1)
    def _next_layer_norm():
        h_ref[...] = _adaln(row_ref[...], g_ref[...], sh_ref[...], sc_ref[...]).astype(h_ref.dtype)


def _ffn_down_norm_call(up, w_bf, resid, mod3, layer, next_norm_g):
    tm, tn = 512, 512
    m, k = up.shape
    per_tile = D_MODEL // tn
    def mod_row(i):
        return _mod_row_of_tile(i, tm)
    def next_mod(kk):
        return pl.BlockSpec((None, 1, D_MODEL), lambda i, j: ((layer + 1) * MOD_ROWS + mod_row(i), 0, kk))
    return pl.pallas_call(
        _ffn_down_norm_kernel,
        grid=(m // tm, D_MODEL // tn),
        in_specs=[pl.BlockSpec((tm, k), lambda i, j: (i, 0)),
                  pl.BlockSpec((None, k, tn), lambda i, j: (layer, 0, j)),
                  pl.BlockSpec((tm, tn), lambda i, j: (i, j)),
                  pl.BlockSpec((None, 1, tn), lambda i, j: (layer * MOD_ROWS + mod_row(i), 0, 5 * per_tile + j)),
                  pl.BlockSpec((1, D_MODEL), lambda i, j: (0, 0)), next_mod(0), next_mod(1)],
        out_specs=[pl.BlockSpec((tm, tn), lambda i, j: (i, j)),
                   pl.BlockSpec((tm, D_MODEL), lambda i, j: (i, 0))],
        out_shape=[jax.ShapeDtypeStruct((m, D_MODEL), F32),
                   jax.ShapeDtypeStruct((m, D_MODEL), BF16)],
        scratch_shapes=[pltpu.VMEM((tm, D_MODEL), F32)],
        compiler_params=_cparams(("arbitrary", "arbitrary")),
        name="ffn_down_norm",
    )(up, w_bf, resid, mod3, next_norm_g.reshape(1, D_MODEL), mod3, mod3)


def _gated_merge_kernel(h_ref, wga_ref, wgb_ref, wgc_ref, oa_ref, ob_ref, oc_ref, wa_ref, wb_ref, wc_ref,
                        o_ref, wg_bf, wbr_bf):
    @pl.when(pl.program_id(1) == 0)
    def _cast_weights():
        for t, ref in enumerate((wga_ref, wgb_ref, wgc_ref)):
            wg_bf[t] = ref[...].reshape(wg_bf.shape[1:]).astype(BF16)
        for t, ref in enumerate((wa_ref, wb_ref, wc_ref)):
            wbr_bf[t] = ref[...].astype(BF16)

    def rows_of(ref):
        return jnp.concatenate([ref[t] for t in range(ref.shape[0])], axis=1)

    h = h_ref[...]
    acc = None
    for t, br_ref in enumerate((oa_ref, ob_ref, oc_ref)):
        gate = _sigmoid(lax.dot_general(h, wg_bf[t], _NT, preferred_element_type=F32))
        term = gate * jnp.dot(rows_of(br_ref), wbr_bf[t], preferred_element_type=F32)
        acc = term if acc is None else acc + term
    o_ref[...] = acc.astype(o_ref.dtype)


def _gated_merge_call(h, w_in_t, o_a, o_b, o_c, w_a, w_b, w_c, layer):
    tm, tn = 1024, 256
    def gate_w(b):
        return pl.BlockSpec((pl.Element(1), pl.Element(tn), pl.Element(D_MODEL)),
                            lambda j, i: (layer, pl.multiple_of(OFF_GATES + b * D_MODEL + j * tn, 8), 0))
    br = pl.BlockSpec((HEADS, tm, HEAD_DIM), lambda j, i: (0, i, 0))
    wt = pl.BlockSpec((None, HG_F, tn), lambda j, i: (layer, 0, j))
    return pl.pallas_call(
        _gated_merge_kernel,
        grid=(D_MODEL // tn, N_TOK // tm),
        in_specs=[pl.BlockSpec((tm, D_MODEL), lambda j, i: (i, 0)), gate_w(0), gate_w(1), gate_w(2),
                  br, br, br, wt, wt, wt],
        out_specs=pl.BlockSpec((tm, tn), lambda j, i: (i, j)),
        out_shape=jax.ShapeDtypeStruct((N_TOK, D_MODEL), BF16),
        scratch_shapes=[pltpu.VMEM((3, tn, D_MODEL), BF16), pltpu.VMEM((3, HG_F, tn), BF16)],
        compiler_params=_cparams(("arbitrary", "arbitrary")),
        name="gated_merge",
    )(h, w_in_t, w_in_t, w_in_t, o_a, o_b, o_c, w_a, w_b, w_c)


def _head_rmsnorm_gate(o, g, og):
    y = o * lax.rsqrt(jnp.mean(o * o, axis=-1, keepdims=True) + EPS) * g
    return y * _silu(og)


def _hgrn_kernel(q_ref, i_ref, og_ref, ff_ref, fb_ref, lb_ref, g_ref, s0_ref, o_ref, s_ref,
                 acc_ref, b_ref, stc_ref, km_ref, *, state_slot):
    is_ctx = _is_ctx_step(pl.program_id(1))
    _zero_other_slots(s_ref, state_slot)
    nblk = UNIT // HG_BLOCK
    nsub = HG_BLOCK // HG_SUB
    r = lax.broadcasted_iota(jnp.int32, (HG_BLOCK, HG_BLOCK), 0)
    c = lax.broadcasted_iota(jnp.int32, (HG_BLOCK, HG_BLOCK), 1)
    same = (r // HG_SUB) == (c // HG_SUB)
    row_sub = lax.broadcasted_iota(jnp.int32, (HG_BLOCK, HEAD_DIM), 0) // HG_SUB
    tri = (jnp.logical_and(same, c <= r), jnp.logical_and(same, c >= r))
    tri_bf = [jnp.where(t, 1.0, 0.0).astype(BF16) for t in tri]
    z_refs = (ff_ref, fb_ref)
    inst = [(blk, d) for blk in range(nblk) for d in range(2)]

    def expand(x):
        return jnp.broadcast_to(x[:, None, :], (nsub, HG_SUB, HEAD_DIM)).reshape(HG_BLOCK, HEAD_DIM)

    def blk_rows(blk):
        return slice(blk * HG_BLOCK, (blk + 1) * HG_BLOCK)

    qs = [_silu(q_ref[blk_rows(blk), :]) for blk in range(nblk)]
    vs = [i_ref[blk_rows(blk), :] for blk in range(nblk)]
    ks, lfs = [], []
    for blk, d in inst:
        lb = lb_ref[d:d + 1, :]
        f = lb + (1.0 - lb) * _sigmoid(z_refs[d][blk_rows(blk), :])
        lfs.append(jnp.log(f))
        ks.append(1.0 - f)
    bs = [_dot01_2(tri_bf[d], lf) for (blk, d), lf in zip(inst, lfs)]
    tots, qts, kts, qds = [], [], [], []
    for n, (blk, d) in enumerate(inst):
        b = bs[n]
        b_ref[n] = b
        tot = b_ref[n, pl.ds((HG_SUB - 1) if d == 0 else 0, nsub, stride=HG_SUB), :]
        mid_f = expand(b_ref[n, pl.ds(HG_SUB // 2, nsub, stride=HG_SUB), :])
        tots.append(tot)
        qts.append(qs[blk] * jnp.exp(b - mid_f))
        kts.append(ks[n] * jnp.exp(mid_f - b))
        qds.append((qs[blk] * jnp.exp(b)).astype(BF16))
        kd = ks[n] * jnp.exp(expand(tot) - b)
        for s in range(nsub):
            km_ref[n, :, s * HEAD_DIM:(s + 1) * HEAD_DIM] = jnp.where(row_sub == s, kd, 0.0).astype(BF16)
    scs = [jnp.where(tri[d], _dot1(qt, kt, _NT), 0.0) for (blk, d), qt, kt in zip(inst, qts, kts)]
    uts = [_dg(vs[blk].astype(BF16), km_ref[n], _TN) for n, (blk, d) in enumerate(inst)]
    outs = [_dot1(sc, vs[blk]) for (blk, d), sc in zip(inst, scs)]
    for d in range(2):
        st = s0_ref[d].T
        for blk in (range(nblk) if d == 0 else range(nblk - 1, -1, -1)):
            n = inst.index((blk, d))
            st = jnp.where(is_ctx, 0.0, st)
            for s in (range(nsub) if d == 0 else range(nsub - 1, -1, -1)):
                stc_ref[n, s] = st.T.astype(BF16)
                st = st * jnp.exp(tots[n][s:s + 1, :]) + uts[n][:, s * HEAD_DIM:(s + 1) * HEAD_DIM]
            s_ref[blk, state_slot, d] = st.T
    for n, (blk, d) in enumerate(inst):
        o_int = [_dg(qds[n][s * HG_SUB:(s + 1) * HG_SUB], stc_ref[n, s], _NN) for s in range(nsub)]
        acc_ref[d, blk_rows(blk), :] = outs[n] + jnp.concatenate(o_int, axis=0)
    o_ref[...] = _head_rmsnorm_gate(acc_ref[0] + acc_ref[1], g_ref[...], og_ref[...]).astype(o_ref.dtype)


def _hgrn_call(z, lb, onorm_g, state, layer, new_states):
    def col(off):
        return pl.BlockSpec((None, UNIT, HEAD_DIM), lambda h, g: (off // HEAD_DIM + h, _scan_unit(g), 0))
    in_specs = [col(OFF_HQ), col(OFF_HI), col(OFF_HG), col(OFF_HFF), col(OFF_HFB),
                pl.BlockSpec((2, HEAD_DIM), lambda h, g: (0, h)),
                pl.BlockSpec((1, HEAD_DIM), lambda h, g: (0, 0)),
                _state_in_spec(layer)]
    args = [z, z, z, z, z, lb, onorm_g.reshape(1, HEAD_DIM), state]
    first = new_states is None
    kernel, aliases = functools.partial(_hgrn_kernel, state_slot=layer if first else 0), {}
    if not first:
        kernel, aliases = _without_arg(kernel, len(args)), {len(args): 1}
        in_specs.append(pl.BlockSpec(memory_space=pl.ANY))
        args.append(new_states)
    return pl.pallas_call(
        kernel,
        grid=(HEADS, N_UNITS),
        in_specs=in_specs,
        out_specs=[pl.BlockSpec((None, UNIT, HEAD_DIM), lambda h, g: (h, _scan_unit(g), 0)), _state_out_spec(layer, first)],
        out_shape=[jax.ShapeDtypeStruct((HEADS, N_TOK, HEAD_DIM), BF16), _STATE_SHAPE],
        input_output_aliases=aliases,
        scratch_shapes=[pltpu.VMEM((2, UNIT, HEAD_DIM), F32),
                        pltpu.VMEM((HG_INST, HG_BLOCK, HEAD_DIM), F32),
                        pltpu.VMEM((HG_INST, HG_BLOCK // HG_SUB, HEAD_DIM, HEAD_DIM), BF16),
                        pltpu.VMEM((HG_INST, HG_BLOCK, (HG_BLOCK // HG_SUB) * HEAD_DIM), BF16)],
        compiler_params=_cparams(("arbitrary", "arbitrary")),
        name="hgrn2_scan",
    )(*args)


def _gmlp_kernel(u_ref, v_ref, vn_ref, ws_ref, bs_ref, o_ref):
    tm = u_ref.shape[1]
    for g in range(CM_GROUPS):
        cols = slice(g * HEAD_DIM, (g + 1) * HEAD_DIM)
        vg = _gelu(v_ref[g])
        vg = vg * lax.rsqrt(jnp.mean(vg * vg, axis=-1, keepdims=True) + EPS) * vn_ref[:, cols]
        ug = _gelu(u_ref[g])
        w = ws_ref[g]
        bias = bs_ref[:, g:g + 1]
        for ch in range(tm // CM_CHUNK):
            rows = slice(ch * CM_CHUNK, (ch + 1) * CM_CHUNK)
            s = _dot3(w, vg[rows]) + bias
            o_ref[g, rows, :] = (ug[rows] * s).astype(o_ref.dtype)


def _gmlp_call(z, vnorm_g, ws, bs_t):
    tm = 512
    return pl.pallas_call(
        _gmlp_kernel,
        grid=(N_TOK // tm,),
        in_specs=[pl.BlockSpec((CM_GROUPS, tm, HEAD_DIM), lambda i: (OFF_CU // CM_W, i, 0)),
                  pl.BlockSpec((CM_GROUPS, tm, HEAD_DIM), lambda i: (OFF_CV // CM_W, i, 0)),
                  pl.BlockSpec((1, CM_W), lambda i: (0, 0)),
                  pl.BlockSpec((CM_GROUPS, CM_CHUNK, CM_CHUNK), lambda i: (0, 0, 0)),
                  pl.BlockSpec((CM_CHUNK, CM_GROUPS), lambda i: (0, 0))],
        out_specs=pl.BlockSpec((CM_GROUPS, tm, HEAD_DIM), lambda i: (0, i, 0)),
        out_shape=jax.ShapeDtypeStruct((CM_GROUPS, N_TOK, HEAD_DIM), BF16),
        compiler_params=_cparams(("arbitrary",)),
        name="chunk_gmlp",
    )(z, z, vnorm_g.reshape(1, CM_W), ws, bs_t)


GP_BLOCK = 256


def _gdn_gates_kernel(ab_ref, alog_ref, dt_ref, col_ref, rowt_ref):
    ab = ab_ref[...]
    lane = lax.broadcasted_iota(jnp.int32, ab.shape, 1)
    g = jnp.where(lane < 2 * HEADS, -jnp.exp(alog_ref[...]) * _softplus(ab + dt_ref[...]), 0.0)
    r = lax.broadcasted_iota(jnp.int32, (GP_BLOCK, GP_BLOCK), 0)
    c = lax.broadcasted_iota(jnp.int32, (GP_BLOCK, GP_BLOCK), 1)
    same = (r // GDN_CHUNK) == (c // GDN_CHUNK)
    tri_f = jnp.where(jnp.logical_and(same, c <= r), 1.0, 0.0).astype(BF16)
    tri_b = jnp.where(jnp.logical_and(same, c >= r), 1.0, 0.0).astype(BF16)
    cf = _dot01(tri_f, g)
    cb = _dot01(tri_b, g)
    col = jnp.where(lane < HEADS, cf, jnp.where(lane < 2 * HEADS, cb, _sigmoid(ab)))
    col_ref[...] = col
    rowt_ref[...] = col.T


def _gdn_gates_call(ab, a_log, dt_bias):
    pad = lambda t: jnp.pad(t.reshape(1, 2 * HEADS), ((0, 0), (0, LANES - 2 * HEADS)))
    return pl.pallas_call(
        _gdn_gates_kernel,
        grid=(N_TOK // GP_BLOCK,),
        in_specs=[pl.BlockSpec((GP_BLOCK, LANES), lambda i: (i, 0)),
                  pl.BlockSpec((1, LANES), lambda i: (0, 0)),
                  pl.BlockSpec((1, LANES), lambda i: (0, 0))],
        out_specs=[pl.BlockSpec((GP_BLOCK, LANES), lambda i: (i, 0)),
                   pl.BlockSpec((LANES, GP_BLOCK), lambda i: (0, i))],
        out_shape=[jax.ShapeDtypeStruct((N_TOK, LANES), F32),
                   jax.ShapeDtypeStruct((LANES, N_TOK), F32)],
        compiler_params=_cparams(("arbitrary",)),
        name="gdn_gates",
    )(ab, pad(a_log), pad(dt_bias))


CONV_PAD = 72
GDN_AQ = HEAD_DIM + GDN_CHUNK
GDN_STEP = 16 * GDN_CHUNK


def _gdn_kernel(qr_ref, kr_ref, vr_ref, og_ref, col_ref, rowt_ref, cwq_ref, cwk_ref, cwv_ref,
                g_ref, s0_ref, o_ref, s_ref,
                xp_ref, q_s, k_s, v_s, o_s, b_s, aq_s, *, state_slot):
    is_ctx = _is_ctx_step(pl.program_id(1))
    _zero_other_slots(s_ref, state_slot)
    head = pl.program_id(0)
    n_chunks = UNIT // GDN_CHUNK
    chunks_per_seq = SEQ // GDN_CHUNK

    t = lax.broadcasted_iota(jnp.int32, (UNIT, 1), 0)
    period = jnp.where(is_ctx, SEQ, GRID_W)
    pos = jnp.bitwise_and(t, period - 1)
    ok_left = pos != 0
    ok_right = pos != period - 1
    zeros_pad = jnp.zeros((CONV_PAD, HEAD_DIM), F32)
    xp_ref[0:CONV_PAD, :] = zeros_pad
    xp_ref[CONV_PAD + UNIT:CONV_PAD + UNIT + CONV_PAD, :] = zeros_pad

    def conv_silu(x_ref, w_ref):
        xp_ref[CONV_PAD:CONV_PAD + UNIT, :] = x_ref[...]
        acc = jnp.zeros((UNIT, HEAD_DIM), F32)
        for i in range(CONV_K):
            for j in range(CONV_K):
                w = w_ref[CONV_K * i + j:CONV_K * i + j + 1, :]
                if i != CONV_K // 2:
                    w = jnp.where(is_ctx, 0.0, w)
                start = CONV_PAD + (i - 1) * GRID_W + (j - 1)
                xs = xp_ref[start:start + UNIT, :]
                if j == 0:
                    xs = jnp.where(ok_left, xs, 0.0)
                elif j == CONV_K - 1:
                    xs = jnp.where(ok_right, xs, 0.0)
                acc = acc + xs * w
        return _silu(acc)

    def l2norm(x):
        return x * lax.rsqrt(jnp.sum(x * x, axis=-1, keepdims=True) + EPS)

    q_s[...] = l2norm(conv_silu(qr_ref, cwq_ref)) * (HEAD_DIM ** -0.5)
    k_s[...] = l2norm(conv_silu(kr_ref, cwk_ref))
    v_s[...] = conv_silu(vr_ref, cwv_ref)

    rr = lax.broadcasted_iota(jnp.int32, (GDN_CHUNK, GDN_CHUNK), 0)
    cc = lax.broadcasted_iota(jnp.int32, (GDN_CHUNK, GDN_CHUNK), 1)
    eye = jnp.where(rr == cc, 1.0, 0.0)
    same_blk = [(rr // b) == (cc // b) for b in (8, 16, 32, 64)]
    lane = lax.broadcasted_iota(jnp.int32, (GDN_STEP, LANES), 1)
    sub8 = lax.broadcasted_iota(jnp.int32, (HEADS, GDN_STEP), 0)

    def pick(x, j):
        return jnp.sum(jnp.where(lane[:x.shape[0]] == j, x, 0.0), axis=-1, keepdims=True)

    incl = (cc <= rr, cc >= rr)
    strict = (cc < rr, cc > rr)
    off_masks = [jnp.logical_and(same_blk[lvl], jnp.logical_not(same_blk[lvl - 1]))
                 for lvl in range(1, len(same_blk))]

    def phase1(p, carry):
        rows = pl.ds(pl.multiple_of(p * GDN_STEP, GDN_STEP), GDN_STEP)
        q2, k2, v2 = q_s[rows, :], k_s[rows, :], v_s[rows, :]
        col = col_ref[rows, :]
        gcols = [pick(col, d * HEADS + head) for d in range(2)]
        betas = [pick(col, (2 + d) * HEADS + head) for d in range(2)]
        grows = [jnp.sum(jnp.where(sub8 == head, rowt_ref[d * HEADS:(d + 1) * HEADS, rows], 0.0),
                         axis=0, keepdims=True) for d in range(2)]
        inst = []
        for half in range(GDN_STEP // GDN_CHUNK):
            sl = slice(half * GDN_CHUNK, (half + 1) * GDN_CHUNK)
            qc, kc, vc = q2[sl], k2[sl], v2[sl]
            kk = _dot1(kc, kc, _NT)
            qk = _dot1(qc, kc, _NT)
            for d in range(2):
                gcol, beta = gcols[d][sl], betas[d][sl]
                decay = jnp.where(incl[d], jnp.exp(gcol - grows[d][:, sl]), 0.0)
                m = jnp.where(strict[d], beta * kk * decay, 0.0)
                inst.append((half, d, qc, kc, vc, qk, gcol, beta, decay, m))
        ms = [t[-1] for t in inst]
        pws = [jnp.where(same_blk[0], -m, 0.0) for m in ms]
        tinvs = [eye + pw for pw in pws]
        for _ in range(2):
            pws = [_dot_inv(pw, pw) for pw in pws]
            tinvs = [t + _dot_inv(t, pw) for t, pw in zip(tinvs, pws)]
        for mask in off_masks:
            tmp = [_dot_inv(jnp.where(mask, m, 0.0), t) for m, t in zip(ms, tinvs)]
            tinvs = [t - _dot_inv(t, x) for t, x in zip(tinvs, tmp)]
        uws = [_dot1(tinv, jnp.concatenate([vc * beta, kc * (beta * jnp.exp(gcol))], axis=1)).astype(BF16)
               for (half, d, qc, kc, vc, qk, gcol, beta, decay, m), tinv in zip(inst, tinvs)]
        bas, ows = [], []
        for (half, d, qc, kc, vc, qk, gcol, beta, decay, m), uw in zip(inst, uws):
            glast = gcol[GDN_CHUNK - 1:GDN_CHUNK] if d == 0 else gcol[0:1]
            bas.append(_dg((kc * jnp.exp(glast - gcol)).astype(BF16), uw, _TN))
            ows.append(_dg((qk * decay).astype(BF16), uw, _NN))
        for (half, d, qc, kc, vc, qk, gcol, beta, decay, m), ba, ow in zip(inst, bas, ows):
            chunk0 = p * GDN_STEP + half * GDN_CHUNK
            o_s[d, pl.ds(pl.multiple_of(chunk0, GDN_CHUNK), GDN_CHUNK), :] = ow[:, :HEAD_DIM]
            b_s[d, pl.ds(pl.multiple_of(2 * chunk0, HEAD_DIM), HEAD_DIM), :] = ba[:, :HEAD_DIM]
            aq_s[d, pl.ds(pl.multiple_of(3 * chunk0, GDN_AQ), GDN_AQ), :] = jnp.concatenate(
                [ba[:, HEAD_DIM:], qc * jnp.exp(gcol) - ow[:, HEAD_DIM:]], axis=0).astype(BF16)
        return carry

    lax.fori_loop(0, UNIT // GDN_STEP, phase1, 0)

    lane1 = lax.broadcasted_iota(jnp.int32, (1, LANES), 1)

    def advance(chains, states):
        rs = [_dg(aq_s[d, pl.ds(pl.multiple_of(ch * GDN_AQ, GDN_AQ), GDN_AQ), :], s.astype(BF16), _NN)
              for (d, ch), s in zip(chains, states)]
        new = []
        for (d, ch), s, r in zip(chains, states, rs):
            tile0 = pl.multiple_of(ch * GDN_CHUNK + (GDN_CHUNK - 8 if d == 0 else 0), 8)
            last = col_ref[pl.ds(tile0, 8), :]
            last = last[7:8] if d == 0 else last[0:1]
            glast = jnp.sum(jnp.where(lane1 == d * HEADS + head, last, 0.0), axis=-1, keepdims=True)
            b = b_s[d, pl.ds(pl.multiple_of(ch * HEAD_DIM, HEAD_DIM), HEAD_DIM), :]
            new.append(s * jnp.exp(glast) + (b - r[:HEAD_DIM]))
            rows = pl.ds(pl.multiple_of(ch * GDN_CHUNK, GDN_CHUNK), GDN_CHUNK)
            o_s[d, rows, :] = o_s[d, rows, :] + r[HEAD_DIM:]
        return tuple(new)

    @pl.when(is_ctx)
    def _context_unit():
        def body(i, states):
            chains = [(d, sq * chunks_per_seq + (i if d == 0 else chunks_per_seq - 1 - i))
                      for sq in range(SEQ_PER_UNIT) for d in range(2)]
            return advance(chains, states)
        zero = jnp.zeros((HEAD_DIM, HEAD_DIM), F32)
        final = lax.fori_loop(0, chunks_per_seq, body, (zero,) * (2 * SEQ_PER_UNIT))
        for sq in range(SEQ_PER_UNIT):
            for d in range(2):
                s_ref[sq, state_slot, d] = final[2 * sq + d]

    @pl.when(jnp.logical_not(is_ctx))
    def _latent_unit():
        def body(i, states):
            return advance([(0, i), (1, n_chunks - 1 - i)], states)
        final = lax.fori_loop(0, n_chunks, body, (s0_ref[0], s0_ref[1]))
        for sq in range(SEQ_PER_UNIT):
            for d in range(2):
                s_ref[sq, state_slot, d] = final[d]

    o_ref[...] = _head_rmsnorm_gate(o_s[0] + o_s[1], g_ref[...], og_ref[...]).astype(o_ref.dtype)


def _gdn_call(z, col, rowt, conv_w, onorm_g, state, layer, new_states):
    def zcol(off):
        return pl.BlockSpec((None, UNIT, HEAD_DIM), lambda h, g: (off // HEAD_DIM + h, _scan_unit(g), 0))
    def wcol(part):
        return pl.BlockSpec((CONV_K * CONV_K, HEAD_DIM), lambda h, g: (0, part * HEADS + h))
    scr = lambda *shape: pltpu.VMEM(shape, F32)
    in_specs = [zcol(OFF_GQ), zcol(OFF_GK), zcol(OFF_GV), zcol(OFF_GG),
                pl.BlockSpec((UNIT, LANES), lambda h, g: (_scan_unit(g), 0)),
                pl.BlockSpec((LANES, UNIT), lambda h, g: (0, _scan_unit(g))),
                wcol(0), wcol(1), wcol(2),
                pl.BlockSpec((1, HEAD_DIM), lambda h, g: (0, 0)),
                _state_in_spec(layer)]
    args = [z, z, z, z, col, rowt, conv_w, conv_w, conv_w, onorm_g.reshape(1, HEAD_DIM), state]
    first = new_states is None
    kernel, aliases = functools.partial(_gdn_kernel, state_slot=layer if first else 0), {}
    if not first:
        kernel, aliases = _without_arg(kernel, len(args)), {len(args): 1}
        in_specs.append(pl.BlockSpec(memory_space=pl.ANY))
        args.append(new_states)
    return pl.pallas_call(
        kernel,
        grid=(HEADS, N_UNITS),
        in_specs=in_specs,
        out_specs=[pl.BlockSpec((None, UNIT, HEAD_DIM), lambda h, g: (h, _scan_unit(g), 0)), _state_out_spec(layer, first)],
        out_shape=[jax.ShapeDtypeStruct((HEADS, N_TOK, HEAD_DIM), BF16), _STATE_SHAPE],
        input_output_aliases=aliases,
        scratch_shapes=[scr(UNIT + 2 * CONV_PAD, HEAD_DIM),
                        scr(UNIT, HEAD_DIM), scr(UNIT, HEAD_DIM), scr(UNIT, HEAD_DIM),
                        scr(2, UNIT, HEAD_DIM),
                        scr(2, (UNIT // GDN_CHUNK) * HEAD_DIM, HEAD_DIM),
                        pltpu.VMEM((2, (UNIT // GDN_CHUNK) * GDN_AQ, HEAD_DIM), BF16)],
        compiler_params=_cparams(("arbitrary", "arbitrary")),
        name="gdn_scan",
    )(*args)


def kernel(x_prompt, x_sample, c, state_hgrn, state_gdn, c_ctx, norm1_g, norm2_g, w_mod, b_mod, w_in, hg_lb, hg_onorm_g, cm_vnorm_g, cm_ws, cm_bs, gdn_conv, gdn_A_log, gdn_dt_bias, gdn_onorm_g, w_br_hg, w_br_cm, w_br_gdn, w_out, w_ff1, w_ff2, final_g):
    x = (x_prompt.reshape(N_CTX_TOK, D_MODEL), x_sample.reshape(N_LAT_TOK, D_MODEL))
    cvec = jnp.concatenate([c_ctx[None, :], c, jnp.zeros((MOD_ROWS - 1 - DEC_BATCH, D_MODEL), F32)], axis=0)
    mod3 = _mod_call(cvec, w_mod, b_mod).reshape(DEPTH * MOD_ROWS, 1, 6 * D_MODEL)

    lb_all = jnp.cumsum(jax.nn.softmax(hg_lb.astype(F32), axis=0), axis=0)
    lb_all = lb_all - lb_all[:1]

    w_in_t = jnp.swapaxes(w_in, 1, 2)
    w_ff2_bf = w_ff2.astype(BF16)

    new_hg = new_gdn = None
    h = _normmod_call(x, norm1_g[0], mod3, 0, 0, 1)
    for l in range(DEPTH):
        z = _mm_call(h, w_in_t, l, col0=0, n=N_MAIN, act=None, out_dtype=F32, tm=1024, tn=1024,
                     w_transposed=True, lane_tile_major=True, name="in_proj")
        ab = _mm_call(h, w_in_t, l, col0=OFF_AB, n=LANES, act=None, out_dtype=F32, tm=1024, tn=LANES,
                      w_transposed=True, name="in_proj_ab")
        o_a, new_hg = _hgrn_call(z, lb_all[l], hg_onorm_g[l], state_hgrn, l, new_hg)
        o_b = _gmlp_call(z, cm_vnorm_g[l], cm_ws[l], cm_bs[l].T)
        col, rowt = _gdn_gates_call(ab, gdn_A_log[l], gdn_dt_bias[l])
        o_c, new_gdn = _gdn_call(z, col, rowt, gdn_conv[l].reshape(CONV_K * CONV_K, 3 * HG_F),
                                 gdn_onorm_g[l], state_gdn, l, new_gdn)

        merged = _gated_merge_call(h, w_in_t, o_a, o_b, o_c, w_br_hg, w_br_cm, w_br_gdn, l)
        x, h2 = _outproj_norm_call(merged, w_out, x, norm2_g[l], mod3, l)
        up = _mm_call(h2, w_ff1, l, col0=0, n=D_FF, act="relu2", out_dtype=BF16, tm=2048, tn=1024, name="ffn_up")
        if l + 1 < DEPTH:
            x, h = _ffn_down_norm_call(up, w_ff2_bf, x, mod3, l, norm1_g[l + 1])
        else:
            x = _mm_resid_call(up, w_ff2_bf, x, mod3, l, 5, tm=512, tn=512, name="ffn_down")

    y_prompt = _final_norm_call(x, final_g, 0, N_CTX_TOK).reshape(BATCH, SEQ, D_MODEL)
    y_sample = _final_norm_call(x, final_g, N_CTX_TOK, N_LAT_TOK).reshape(DEC_BATCH, DEC_SEQ, D_MODEL)
    return (y_prompt, y_sample, new_hg, new_gdn)
```

```python
import functools

import jax
import jax.numpy as jnp
from jax import lax
from jax.experimental import pallas as pl
from jax.experimental.pallas import tpu as pltpu

F32 = jnp.float32
BF16 = jnp.bfloat16

D_MODEL = 2048
BATCH = 16
SEQ = 256
DEPTH = 2
DEC_BATCH = 4
DEC_SEQ = 1024
GRID_W = 64
EPS = 1e-6
D_FF = 4 * D_MODEL
HEADS = 8
HEAD_DIM = 128
HG_F = HEADS * HEAD_DIM
CM_GROUPS = 8
CM_W = CM_GROUPS * HEAD_DIM
CM_CHUNK = 128
GDN_CHUNK = 64
CONV_K = 3

OFF_HQ, OFF_HI, OFF_HG, OFF_HFF, OFF_HFB = 0, 1024, 2048, 3072, 4096
OFF_CU, OFF_CV = 5120, 6144
OFF_GQ, OFF_GK, OFF_GV, OFF_GG = 7168, 8192, 9216, 10240
OFF_AB = 11264
OFF_GATES = 11296
IN_DIM = 17440
N_MAIN = OFF_AB

N_CTX_TOK = BATCH * SEQ
N_LAT_TOK = DEC_BATCH * DEC_SEQ
N_TOK = N_CTX_TOK + N_LAT_TOK
UNIT = DEC_SEQ
N_CTX_UNITS = N_CTX_TOK // UNIT
N_UNITS = N_TOK // UNIT
N_LAT_UNITS = N_UNITS - N_CTX_UNITS
SEQ_PER_UNIT = UNIT // SEQ

LANES = 128
MOD_ROWS = 8
HG_BLOCK = 256
HG_SUB = 32
HG_INST = 2 * (UNIT // HG_BLOCK)
VMEM_LIMIT = 56 * 1024 * 1024

_NT = (((1,), (1,)), ((), ()))
_TN = (((0,), (0,)), ((), ()))
_NN = (((1,), (0,)), ((), ()))


def _dg(a, b, dims):
    return lax.dot_general(a, b, dims, preferred_element_type=F32)


def _split2(x):
    hi = x.astype(BF16)
    lo = (x - hi.astype(F32)).astype(BF16)
    return hi, lo


def _dot1(a, b, dims=_NN):
    return _dg(a.astype(BF16), b.astype(BF16), dims)


def _dot3(a, b, dims=_NN):
    ah, al = _split2(a)
    bh, bl = _split2(b)
    return _dg(ah, bh, dims) + (_dg(ah, bl, dims) + _dg(al, bh, dims))


_dot_inv = _dot1


def _dot01(m, x):
    hi = x.astype(BF16)
    r = x - hi.astype(F32)
    mid = r.astype(BF16)
    lo = (r - mid.astype(F32)).astype(BF16)
    return _dg(m, hi, _NN) + (_dg(m, mid, _NN) + _dg(m, lo, _NN))


def _dot01_2(m, x):
    hi, lo = _split2(x)
    return _dg(m, hi, _NN) + _dg(m, lo, _NN)


def _sigmoid(x):
    return 1.0 / (1.0 + jnp.exp(-x))


def _silu(x):
    return x * _sigmoid(x)


def _gelu(x):
    return 0.5 * x * (1.0 + lax.erf(x * (2.0 ** -0.5)))


def _softplus(x):
    return jnp.maximum(x, 0.0) + jnp.log1p(jnp.exp(-jnp.abs(x)))


def _mod_row_of_tile(i, tm):
    return jnp.maximum(0, (i * tm - N_CTX_TOK) // DEC_SEQ + 1)


assert N_LAT_UNITS == N_CTX_UNITS


def _scan_unit(g):
    return jnp.where(g % 2 == 0, N_CTX_UNITS + g // 2, g // 2)


def _is_ctx_step(g):
    return g % 2 == 1


def _state_out_spec(layer, first):
    if first:
        return pl.BlockSpec((SEQ_PER_UNIT, DEPTH, 2, None, HEAD_DIM, HEAD_DIM), lambda h, g: (g // 2, 0, 0, h, 0, 0))
    return pl.BlockSpec((SEQ_PER_UNIT, 1, 2, None, HEAD_DIM, HEAD_DIM), lambda h, g: (g // 2, layer, 0, h, 0, 0))


def _zero_other_slots(s_ref, slot):
    for other in range(s_ref.shape[1]):
        if other != slot:
            s_ref[:, other] = jnp.zeros((s_ref.shape[0],) + tuple(s_ref.shape[2:]), F32)


def _state_in_spec(layer):
    return pl.BlockSpec((None, None, 2, None, HEAD_DIM, HEAD_DIM),
                        lambda h, g: (g // 2, layer, 0, h, 0, 0))


_STATE_SHAPE = jax.ShapeDtypeStruct((BATCH, DEPTH, 2, HEADS, HEAD_DIM, HEAD_DIM), F32)


def _without_arg(kernel, pos):
    def wrapped(*refs):
        return kernel(*refs[:pos], *refs[pos + 1:])
    return wrapped


def _cparams(sem):
    return pltpu.CompilerParams(dimension_semantics=sem, vmem_limit_bytes=VMEM_LIMIT)


def _mod_kernel(c_ref, w_ref, b_ref, o_ref):
    s = _silu(c_ref[...])
    hi, lo = _split2(s)
    w = w_ref[...].astype(BF16)
    o_ref[...] = _dg(hi, w, _NN) + _dg(lo, w, _NN) + b_ref[...]


def _mod_call(cvec, w_mod, b_mod):
    tn = 1024
    n = 6 * D_MODEL
    return pl.pallas_call(
        _mod_kernel,
        grid=(DEPTH, n // tn),
        in_specs=[pl.BlockSpec((MOD_ROWS, D_MODEL), lambda l, j: (0, 0)),
                  pl.BlockSpec((None, D_MODEL, tn), lambda l, j: (l, 0, j)),
                  pl.BlockSpec((None, 1, tn), lambda l, j: (l, 0, j))],
        out_specs=pl.BlockSpec((None, MOD_ROWS, tn), lambda l, j: (l, 0, j)),
        out_shape=jax.ShapeDtypeStruct((DEPTH, MOD_ROWS, n), F32),
        compiler_params=_cparams(("arbitrary", "arbitrary")),
        name="modulation",
    )(cvec, w_mod, b_mod.reshape(DEPTH, 1, n))


def _adaln(x, g, shift, scale):
    y = x * lax.rsqrt(jnp.mean(x * x, axis=-1, keepdims=True) + EPS) * g
    return y * (1.0 + scale) + shift


def _residual_specs(x, tm, idx):
    if not isinstance(x, tuple):
        return [pl.BlockSpec((tm, D_MODEL), lambda *g: (idx(*g), 0))], [x]
    n_ctx = N_CTX_TOK // tm
    return ([pl.BlockSpec((tm, D_MODEL), lambda *g: (jnp.minimum(idx(*g), n_ctx - 1), 0)),
             pl.BlockSpec((tm, D_MODEL), lambda *g: (jnp.maximum(idx(*g) - n_ctx, 0), 0))], list(x))


def _residual_tile(refs, tile, tm):
    if len(refs) == 1:
        return refs[0][...]
    return jnp.where(tile < N_CTX_TOK // tm, refs[0][...], refs[1][...])


def _normmod_kernel(*refs, n_x):
    x_refs, (g_ref, sh_ref, sc_ref, o_ref) = refs[:n_x], refs[n_x:]
    x = _residual_tile(x_refs, pl.program_id(0), o_ref.shape[0])
    o_ref[...] = _adaln(x, g_ref[...], sh_ref[...], sc_ref[...]).astype(o_ref.dtype)


def _normmod_call(x, g, mod3, layer, k_shift, k_scale):
    tm = 512
    def mod_spec(k):
        return pl.BlockSpec((None, 1, D_MODEL),
                            lambda i: (layer * MOD_ROWS + _mod_row_of_tile(i, tm), 0, k))
    x_specs, x_args = _residual_specs(x, tm, lambda i: i)
    return pl.pallas_call(
        functools.partial(_normmod_kernel, n_x=len(x_args)),
        grid=(N_TOK // tm,),
        in_specs=x_specs + [pl.BlockSpec((1, D_MODEL), lambda i: (0, 0)), mod_spec(k_shift), mod_spec(k_scale)],
        out_specs=pl.BlockSpec((tm, D_MODEL), lambda i: (i, 0)),
        out_shape=jax.ShapeDtypeStruct((N_TOK, D_MODEL), BF16),
        compiler_params=_cparams(("arbitrary",)),
        name="norm_mod",
    )(*x_args, g.reshape(1, D_MODEL), mod3, mod3)


def _final_norm_kernel(x_ref, g_ref, o_ref):
    x = x_ref[...]
    o_ref[...] = x * lax.rsqrt(jnp.mean(x * x, axis=-1, keepdims=True) + EPS) * g_ref[...]


def _final_norm_call(x, g, row0, n_rows):
    tm = 512
    return pl.pallas_call(
        _final_norm_kernel,
        grid=(n_rows // tm,),
        in_specs=[pl.BlockSpec((tm, D_MODEL), lambda i: (row0 // tm + i, 0)),
                  pl.BlockSpec((1, D_MODEL), lambda i: (0, 0))],
        out_specs=pl.BlockSpec((tm, D_MODEL), lambda i: (i, 0)),
        out_shape=jax.ShapeDtypeStruct((n_rows, D_MODEL), F32),
        compiler_params=_cparams(("arbitrary",)),
        name="final_norm",
    )(x, g.reshape(1, D_MODEL))


def _mm_kernel(x_ref, w_ref, o_ref, wbf_ref, *, act, w_transposed):
    @pl.when(pl.program_id(1) == 0)
    def _cast_weights():
        wbf_ref[...] = w_ref[...].reshape(wbf_ref.shape).astype(BF16)

    acc = lax.dot_general(x_ref[...], wbf_ref[...], _NT if w_transposed else _NN, preferred_element_type=F32)
    if act == "relu2":
        acc = jnp.square(jnp.maximum(acc, 0.0))
    if len(o_ref.shape) == 3:
        for c in range(o_ref.shape[0]):
            o_ref[c] = acc[:, c * LANES:(c + 1) * LANES].astype(o_ref.dtype)
    else:
        o_ref[...] = acc.astype(o_ref.dtype)


def _mm_call(x, w_all, layer, *, col0, n, act, out_dtype, tm, tn, name, w_transposed=False,
             lane_tile_major=False):
    m, k = x.shape
    assert n % tn == 0 and m % tm == 0
    if w_transposed:
        assert col0 % 8 == 0
        w_spec = pl.BlockSpec((pl.Element(1), pl.Element(tn), pl.Element(k)),
                              lambda j, i: (layer, pl.multiple_of(col0 + j * tn, 8), 0))
        w_scratch = pltpu.VMEM((tn, k), BF16)
    else:
        assert col0 % tn == 0
        w_spec = pl.BlockSpec((None, k, tn), lambda j, i: (layer, 0, col0 // tn + j))
        w_scratch = pltpu.VMEM((k, tn), BF16)
    if lane_tile_major:
        out_spec = pl.BlockSpec((tn // LANES, tm, LANES), lambda j, i: (j, i, 0))
        out_shape = jax.ShapeDtypeStruct((n // LANES, m, LANES), out_dtype)
    else:
        out_spec = pl.BlockSpec((tm, tn), lambda j, i: (i, j))
        out_shape = jax.ShapeDtypeStruct((m, n), out_dtype)
    return pl.pallas_call(
        functools.partial(_mm_kernel, act=act, w_transposed=w_transposed),
        grid=(n // tn, m // tm),
        in_specs=[pl.BlockSpec((tm, k), lambda j, i: (i, 0)), w_spec],
        out_specs=out_spec,
        out_shape=out_shape,
        scratch_shapes=[w_scratch],
        compiler_params=_cparams(("arbitrary", "arbitrary")),
        name=name,
    )(x, w_all)


def _outproj_norm_kernel(*refs, n_x):
    m_ref, w_ref = refs[:2]
    x_refs = refs[2:2 + n_x]
    gate_ref, g_ref, sh_ref, sc_ref, xo_ref, h_ref, wbf_ref = refs[2 + n_x:]

    @pl.when(pl.program_id(0) == 0)
    def _cast_weights():
        wbf_ref[...] = w_ref[...].astype(BF16)

    acc = jnp.dot(m_ref[...], wbf_ref[...], preferred_element_type=F32)
    xn = _residual_tile(x_refs, pl.program_id(0), xo_ref.shape[0]) + gate_ref[...] * acc
    xo_ref[...] = xn
    h_ref[...] = _adaln(xn, g_ref[...], sh_ref[...], sc_ref[...]).astype(h_ref.dtype)


def _outproj_norm_call(merged, w_out, x, norm_g, mod3, layer):
    tm = 256 if isinstance(x, tuple) else 512
    def mod_spec(k):
        return pl.BlockSpec((None, 1, D_MODEL),
                            lambda i: (layer * MOD_ROWS + _mod_row_of_tile(i, tm), 0, k))
    row = pl.BlockSpec((tm, D_MODEL), lambda i: (i, 0))
    x_specs, x_args = _residual_specs(x, tm, lambda i: i)
    return pl.pallas_call(
        functools.partial(_outproj_norm_kernel, n_x=len(x_args)),
        grid=(N_TOK // tm,),
        in_specs=[row,
                  pl.BlockSpec((None, D_MODEL, D_MODEL), lambda i: (layer, 0, 0), pipeline_mode=pl.Buffered(1))]
                 + x_specs + [mod_spec(2), pl.BlockSpec((1, D_MODEL), lambda i: (0, 0)), mod_spec(3), mod_spec(4)],
        out_specs=[row, row],
        out_shape=[jax.ShapeDtypeStruct((N_TOK, D_MODEL), F32),
                   jax.ShapeDtypeStruct((N_TOK, D_MODEL), BF16)],
        scratch_shapes=[pltpu.VMEM((D_MODEL, D_MODEL), BF16)],
        compiler_params=_cparams(("arbitrary",)),
        name="out_proj_norm",
    )(merged, w_out, *x_args, mod3, norm_g.reshape(1, D_MODEL), mod3, mod3)


def _ffn_down_kernel(*refs, with_norm):
    x_ref, w_ref, r_ref, gate_ref = refs[:4]
    if with_norm:
        g_ref, sh_ref, sc_ref, o_ref, h_ref = refs[4:]
    else:
        (o_ref,) = refs[4:]
    k = pl.program_id(1)

    @pl.when(k == 0)
    def _first_slab():
        o_ref[...] = jnp.zeros(o_ref.shape, F32)

    o_ref[...] += jnp.dot(x_ref[...], w_ref[...], preferred_element_type=F32)

    @pl.when(k == pl.num_programs(1) - 1)
    def _finish_rows():
        o_ref[...] = r_ref[...] + gate_ref[...] * o_ref[...]
        if with_norm:
            h_ref[...] = _adaln(o_ref[...], g_ref[...], sh_ref[...], sc_ref[...]).astype(h_ref.dtype)


def _ffn_down_call(up, w_bf, resid, mod3, layer, next_norm_g):
    tm, tk = 1024, 1024
    m, kdim = up.shape
    with_norm = next_norm_g is not None
    def mod_spec(lyr, kk):
        return pl.BlockSpec((None, 1, D_MODEL), lambda i, k: (lyr * MOD_ROWS + _mod_row_of_tile(i, tm), 0, kk))
    row = pl.BlockSpec((tm, D_MODEL), lambda i, k: (i, 0))
    in_specs = [pl.BlockSpec((tm, tk), lambda i, k: (i, k)),
                pl.BlockSpec((None, tk, D_MODEL), lambda i, k: (layer, k, 0)),
                pl.BlockSpec((tm, D_MODEL), lambda i, k: (i, 0), pipeline_mode=pl.Buffered(1)),
                mod_spec(layer, 5)]
    args = [up, w_bf, resid, mod3]
    out_specs, out_shape = [row], [jax.ShapeDtypeStruct((m, D_MODEL), F32)]
    if with_norm:
        in_specs += [pl.BlockSpec((1, D_MODEL), lambda i, k: (0, 0)), mod_spec(layer + 1, 0), mod_spec(layer + 1, 1)]
        args += [next_norm_g.reshape(1, D_MODEL), mod3, mod3]
        out_specs.append(row)
        out_shape.append(jax.ShapeDtypeStruct((m, D_MODEL), BF16))
    return pl.pallas_call(
        functools.partial(_ffn_down_kernel, with_norm=with_norm),
        grid=(m // tm, kdim // tk),
        in_specs=in_specs,
        out_specs=out_specs,
        out_shape=out_shape,
        compiler_params=_cparams(("arbitrary", "arbitrary")),
        name="ffn_down",
    )(*args)


def _gated_merge_kernel(h_ref, wga_ref, wgb_ref, wgc_ref, oa_ref, ob_ref, oc_ref, wa_ref, wb_ref, wc_ref,
                        o_ref, wg_bf, wbr_bf):
    @pl.when(pl.program_id(1) == 0)
    def _cast_weights():
        for t, ref in enumerate((wga_ref, wgb_ref, wgc_ref)):
            wg_bf[t] = ref[...].reshape(wg_bf.shape[1:]).astype(BF16)
        for t, ref in enumerate((wa_ref, wb_ref, wc_ref)):
            wbr_bf[t] = ref[...].astype(BF16)

    def rows_of(ref):
        return jnp.concatenate([ref[t] for t in range(ref.shape[0])], axis=1)

    h = h_ref[...]
    acc = None
    for t, br_ref in enumerate((oa_ref, ob_ref, oc_ref)):
        gate = _sigmoid(lax.dot_general(h, wg_bf[t], _NT, preferred_element_type=F32))
        term = gate * jnp.dot(rows_of(br_ref), wbr_bf[t], preferred_element_type=F32)
        acc = term if acc is None else acc + term
    o_ref[...] = acc.astype(o_ref.dtype)


def _gated_merge_call(h, w_in_t, o_a, o_b, o_c, w_a, w_b, w_c, layer):
    tm, tn = 1024, 256
    def gate_w(b):
        return pl.BlockSpec((pl.Element(1), pl.Element(tn), pl.Element(D_MODEL)),
                            lambda j, i: (layer, pl.multiple_of(OFF_GATES + b * D_MODEL + j * tn, 8), 0))
    br = pl.BlockSpec((HEADS, tm, HEAD_DIM), lambda j, i: (0, i, 0))
    wt = pl.BlockSpec((None, HG_F, tn), lambda j, i: (layer, 0, j))
    return pl.pallas_call(
        _gated_merge_kernel,
        grid=(D_MODEL // tn, N_TOK // tm),
        in_specs=[pl.BlockSpec((tm, D_MODEL), lambda j, i: (i, 0)), gate_w(0), gate_w(1), gate_w(2),
                  br, br, br, wt, wt, wt],
        out_specs=pl.BlockSpec((tm, tn), lambda j, i: (i, j)),
        out_shape=jax.ShapeDtypeStruct((N_TOK, D_MODEL), BF16),
        scratch_shapes=[pltpu.VMEM((3, tn, D_MODEL), BF16), pltpu.VMEM((3, HG_F, tn), BF16)],
        compiler_params=_cparams(("arbitrary", "arbitrary")),
        name="gated_merge",
    )(h, w_in_t, w_in_t, w_in_t, o_a, o_b, o_c, w_a, w_b, w_c)


def _head_rmsnorm_gate(o, g, og):
    y = o * lax.rsqrt(jnp.mean(o * o, axis=-1, keepdims=True) + EPS) * g
    return y * _silu(og)


def _hgrn_kernel(q_ref, i_ref, og_ref, ff_ref, fb_ref, lb_ref, g_ref, s0_ref, o_ref, s_ref,
                 acc_ref, b_ref, stc_ref, km_ref, *, state_slot):
    is_ctx = _is_ctx_step(pl.program_id(1))
    _zero_other_slots(s_ref, state_slot)
    nblk = UNIT // HG_BLOCK
    nsub = HG_BLOCK // HG_SUB
    r = lax.broadcasted_iota(jnp.int32, (HG_BLOCK, HG_BLOCK), 0)
    c = lax.broadcasted_iota(jnp.int32, (HG_BLOCK, HG_BLOCK), 1)
    same = (r // HG_SUB) == (c // HG_SUB)
    row_sub = lax.broadcasted_iota(jnp.int32, (HG_BLOCK, HEAD_DIM), 0) // HG_SUB
    tri = (jnp.logical_and(same, c <= r), jnp.logical_and(same, c >= r))
    tri_bf = [jnp.where(t, 1.0, 0.0).astype(BF16) for t in tri]
    z_refs = (ff_ref, fb_ref)
    inst = [(blk, d) for blk in range(nblk) for d in range(2)]

    def expand(x):
        return jnp.broadcast_to(x[:, None, :], (nsub, HG_SUB, HEAD_DIM)).reshape(HG_BLOCK, HEAD_DIM)

    def blk_rows(blk):
        return slice(blk * HG_BLOCK, (blk + 1) * HG_BLOCK)

    qs = [_silu(q_ref[blk_rows(blk), :]) for blk in range(nblk)]
    vs = [i_ref[blk_rows(blk), :] for blk in range(nblk)]
    ks, lfs = [], []
    for blk, d in inst:
        lb = lb_ref[d:d + 1, :]
        f = lb + (1.0 - lb) * _sigmoid(z_refs[d][blk_rows(blk), :])
        lfs.append(jnp.log(f))
        ks.append(1.0 - f)
    bs = [_dot01_2(tri_bf[d], lf) for (blk, d), lf in zip(inst, lfs)]
    tots, qts, kts, qds = [], [], [], []
    for n, (blk, d) in enumerate(inst):
        b = bs[n]
        b_ref[n] = b
        tot = b_ref[n, pl.ds((HG_SUB - 1) if d == 0 else 0, nsub, stride=HG_SUB), :]
        mid_f = expand(b_ref[n, pl.ds(HG_SUB // 2, nsub, stride=HG_SUB), :])
        tots.append(tot)
        qts.append(qs[blk] * jnp.exp(b - mid_f))
        kts.append(ks[n] * jnp.exp(mid_f - b))
        qds.append((qs[blk] * jnp.exp(b)).astype(BF16))
        kd = ks[n] * jnp.exp(expand(tot) - b)
        for s in range(nsub):
            km_ref[n, :, s * HEAD_DIM:(s + 1) * HEAD_DIM] = jnp.where(row_sub == s, kd, 0.0).astype(BF16)
    scs = [jnp.where(tri[d], _dot1(qt, kt, _NT), 0.0) for (blk, d), qt, kt in zip(inst, qts, kts)]
    uts = [_dg(vs[blk].astype(BF16), km_ref[n], _TN) for n, (blk, d) in enumerate(inst)]
    outs = [_dot1(sc, vs[blk]) for (blk, d), sc in zip(inst, scs)]
    for d in range(2):
        st = s0_ref[d].T
        for blk in (range(nblk) if d == 0 else range(nblk - 1, -1, -1)):
            n = inst.index((blk, d))
            st = jnp.where(is_ctx, 0.0, st)
            for s in (range(nsub) if d == 0 else range(nsub - 1, -1, -1)):
                stc_ref[n, s] = st.T.astype(BF16)
                st = st * jnp.exp(tots[n][s:s + 1, :]) + uts[n][:, s * HEAD_DIM:(s + 1) * HEAD_DIM]
            s_ref[blk, state_slot, d] = st.T
    for n, (blk, d) in enumerate(inst):
        o_int = [_dg(qds[n][s * HG_SUB:(s + 1) * HG_SUB], stc_ref[n, s], _NN) for s in range(nsub)]
        acc_ref[d, blk_rows(blk), :] = outs[n] + jnp.concatenate(o_int, axis=0)
    o_ref[...] = _head_rmsnorm_gate(acc_ref[0] + acc_ref[1], g_ref[...], og_ref[...]).astype(o_ref.dtype)


def _hgrn_call(z, lb, onorm_g, state, layer, new_states):
    def col(off):
        return pl.BlockSpec((None, UNIT, HEAD_DIM), lambda h, g: (off // HEAD_DIM + h, _scan_unit(g), 0))
    in_specs = [col(OFF_HQ), col(OFF_HI), col(OFF_HG), col(OFF_HFF), col(OFF_HFB),
                pl.BlockSpec((2, HEAD_DIM), lambda h, g: (0, h)),
                pl.BlockSpec((1, HEAD_DIM), lambda h, g: (0, 0)),
                _state_in_spec(layer)]
    args = [z, z, z, z, z, lb, onorm_g.reshape(1, HEAD_DIM), state]
    first = new_states is None
    kernel, aliases = functools.partial(_hgrn_kernel, state_slot=layer if first else 0), {}
    if not first:
        kernel, aliases = _without_arg(kernel, len(args)), {len(args): 1}
        in_specs.append(pl.BlockSpec(memory_space=pl.ANY))
        args.append(new_states)
    return pl.pallas_call(
        kernel,
        grid=(HEADS, N_UNITS),
        in_specs=in_specs,
        out_specs=[pl.BlockSpec((None, UNIT, HEAD_DIM), lambda h, g: (h, _scan_unit(g), 0)), _state_out_spec(layer, first)],
        out_shape=[jax.ShapeDtypeStruct((HEADS, N_TOK, HEAD_DIM), BF16), _STATE_SHAPE],
        input_output_aliases=aliases,
        scratch_shapes=[pltpu.VMEM((2, UNIT, HEAD_DIM), F32),
                        pltpu.VMEM((HG_INST, HG_BLOCK, HEAD_DIM), F32),
                        pltpu.VMEM((HG_INST, HG_BLOCK // HG_SUB, HEAD_DIM, HEAD_DIM), BF16),
                        pltpu.VMEM((HG_INST, HG_BLOCK, (HG_BLOCK // HG_SUB) * HEAD_DIM), BF16)],
        compiler_params=_cparams(("arbitrary", "arbitrary")),
        name="hgrn2_scan",
    )(*args)


def _gmlp_kernel(u_ref, v_ref, vn_ref, ws_ref, bs_ref, o_ref):
    tm = u_ref.shape[1]
    for g in range(CM_GROUPS):
        cols = slice(g * HEAD_DIM, (g + 1) * HEAD_DIM)
        vg = _gelu(v_ref[g])
        vg = vg * lax.rsqrt(jnp.mean(vg * vg, axis=-1, keepdims=True) + EPS) * vn_ref[:, cols]
        ug = _gelu(u_ref[g])
        w = ws_ref[g]
        bias = bs_ref[:, g:g + 1]
        for ch in range(tm // CM_CHUNK):
            rows = slice(ch * CM_CHUNK, (ch + 1) * CM_CHUNK)
            s = _dot3(w, vg[rows]) + bias
            o_ref[g, rows, :] = (ug[rows] * s).astype(o_ref.dtype)


def _gmlp_call(z, vnorm_g, ws, bs_t):
    tm = 512
    return pl.pallas_call(
        _gmlp_kernel,
        grid=(N_TOK // tm,),
        in_specs=[pl.BlockSpec((CM_GROUPS, tm, HEAD_DIM), lambda i: (OFF_CU // CM_W, i, 0)),
                  pl.BlockSpec((CM_GROUPS, tm, HEAD_DIM), lambda i: (OFF_CV // CM_W, i, 0)),
                  pl.BlockSpec((1, CM_W), lambda i: (0, 0)),
                  pl.BlockSpec((CM_GROUPS, CM_CHUNK, CM_CHUNK), lambda i: (0, 0, 0)),
                  pl.BlockSpec((CM_CHUNK, CM_GROUPS), lambda i: (0, 0))],
        out_specs=pl.BlockSpec((CM_GROUPS, tm, HEAD_DIM), lambda i: (0, i, 0)),
        out_shape=jax.ShapeDtypeStruct((CM_GROUPS, N_TOK, HEAD_DIM), BF16),
        compiler_params=_cparams(("arbitrary",)),
        name="chunk_gmlp",
    )(z, z, vnorm_g.reshape(1, CM_W), ws, bs_t)


GP_BLOCK = 256


def _gdn_gates_kernel(ab_ref, alog_ref, dt_ref, col_ref, rowt_ref):
    ab = ab_ref[...]
    lane = lax.broadcasted_iota(jnp.int32, ab.shape, 1)
    g = jnp.where(lane < 2 * HEADS, -jnp.exp(alog_ref[...]) * _softplus(ab + dt_ref[...]), 0.0)
    r = lax.broadcasted_iota(jnp.int32, (GP_BLOCK, GP_BLOCK), 0)
    c = lax.broadcasted_iota(jnp.int32, (GP_BLOCK, GP_BLOCK), 1)
    same = (r // GDN_CHUNK) == (c // GDN_CHUNK)
    tri_f = jnp.where(jnp.logical_and(same, c <= r), 1.0, 0.0).astype(BF16)
    tri_b = jnp.where(jnp.logical_and(same, c >= r), 1.0, 0.0).astype(BF16)
    cf = _dot01(tri_f, g)
    cb = _dot01(tri_b, g)
    col = jnp.where(lane < HEADS, cf, jnp.where(lane < 2 * HEADS, cb, _sigmoid(ab)))
    col_ref[...] = col
    rowt_ref[...] = col.T


def _gdn_gates_call(ab, a_log, dt_bias):
    pad = lambda t: jnp.pad(t.reshape(1, 2 * HEADS), ((0, 0), (0, LANES - 2 * HEADS)))
    return pl.pallas_call(
        _gdn_gates_kernel,
        grid=(N_TOK // GP_BLOCK,),
        in_specs=[pl.BlockSpec((GP_BLOCK, LANES), lambda i: (i, 0)),
                  pl.BlockSpec((1, LANES), lambda i: (0, 0)),
                  pl.BlockSpec((1, LANES), lambda i: (0, 0))],
        out_specs=[pl.BlockSpec((GP_BLOCK, LANES), lambda i: (i, 0)),
                   pl.BlockSpec((LANES, GP_BLOCK), lambda i: (0, i))],
        out_shape=[jax.ShapeDtypeStruct((N_TOK, LANES), F32),
                   jax.ShapeDtypeStruct((LANES, N_TOK), F32)],
        compiler_params=_cparams(("arbitrary",)),
        name="gdn_gates",
    )(ab, pad(a_log), pad(dt_bias))


CONV_PAD = 72
GDN_AQ = HEAD_DIM + GDN_CHUNK
GDN_STEP = 16 * GDN_CHUNK


def _gdn_kernel(qr_ref, kr_ref, vr_ref, og_ref, col_ref, rowt_ref, cwq_ref, cwk_ref, cwv_ref,
                g_ref, s0_ref, o_ref, s_ref,
                xp_ref, q_s, k_s, v_s, o_s, b_s, aq_s, *, state_slot):
    is_ctx = _is_ctx_step(pl.program_id(1))
    _zero_other_slots(s_ref, state_slot)
    head = pl.program_id(0)
    n_chunks = UNIT // GDN_CHUNK
    chunks_per_seq = SEQ // GDN_CHUNK

    t = lax.broadcasted_iota(jnp.int32, (UNIT, 1), 0)
    period = jnp.where(is_ctx, SEQ, GRID_W)
    pos = jnp.bitwise_and(t, period - 1)
    ok_left = pos != 0
    ok_right = pos != period - 1
    zeros_pad = jnp.zeros((CONV_PAD, HEAD_DIM), F32)
    xp_ref[0:CONV_PAD, :] = zeros_pad
    xp_ref[CONV_PAD + UNIT:CONV_PAD + UNIT + CONV_PAD, :] = zeros_pad

    def conv_silu(x_ref, w_ref):
        xp_ref[CONV_PAD:CONV_PAD + UNIT, :] = x_ref[...]
        acc = jnp.zeros((UNIT, HEAD_DIM), F32)
        for i in range(CONV_K):
            for j in range(CONV_K):
                w = w_ref[CONV_K * i + j:CONV_K * i + j + 1, :]
                if i != CONV_K // 2:
                    w = jnp.where(is_ctx, 0.0, w)
                start = CONV_PAD + (i - 1) * GRID_W + (j - 1)
                xs = xp_ref[start:start + UNIT, :]
                if j == 0:
                    xs = jnp.where(ok_left, xs, 0.0)
                elif j == CONV_K - 1:
                    xs = jnp.where(ok_right, xs, 0.0)
                acc = acc + xs * w
        return _silu(acc)

    def l2norm(x):
        return x * lax.rsqrt(jnp.sum(x * x, axis=-1, keepdims=True) + EPS)

    q_s[...] = l2norm(conv_silu(qr_ref, cwq_ref)) * (HEAD_DIM ** -0.5)
    k_s[...] = l2norm(conv_silu(kr_ref, cwk_ref))
    v_s[...] = conv_silu(vr_ref, cwv_ref)

    rr = lax.broadcasted_iota(jnp.int32, (GDN_CHUNK, GDN_CHUNK), 0)
    cc = lax.broadcasted_iota(jnp.int32, (GDN_CHUNK, GDN_CHUNK), 1)
    eye = jnp.where(rr == cc, 1.0, 0.0)
    same_blk = [(rr // b) == (cc // b) for b in (8, 16, 32, 64)]
    lane = lax.broadcasted_iota(jnp.int32, (GDN_STEP, LANES), 1)
    sub8 = lax.broadcasted_iota(jnp.int32, (HEADS, GDN_STEP), 0)

    def pick(x, j):
        return jnp.sum(jnp.where(lane[:x.shape[0]] == j, x, 0.0), axis=-1, keepdims=True)

    incl = (cc <= rr, cc >= rr)
    strict = (cc < rr, cc > rr)
    off_masks = [jnp.logical_and(same_blk[lvl], jnp.logical_not(same_blk[lvl - 1]))
                 for lvl in range(1, len(same_blk))]

    def phase1(p, carry):
        rows = pl.ds(pl.multiple_of(p * GDN_STEP, GDN_STEP), GDN_STEP)
        q2, k2, v2 = q_s[rows, :], k_s[rows, :], v_s[rows, :]
        col = col_ref[rows, :]
        gcols = [pick(col, d * HEADS + head) for d in range(2)]
        betas = [pick(col, (2 + d) * HEADS + head) for d in range(2)]
        grows = [jnp.sum(jnp.where(sub8 == head, rowt_ref[d * HEADS:(d + 1) * HEADS, rows], 0.0),
                         axis=0, keepdims=True) for d in range(2)]
        inst = []
        for half in range(GDN_STEP // GDN_CHUNK):
            sl = slice(half * GDN_CHUNK, (half + 1) * GDN_CHUNK)
            qc, kc, vc = q2[sl], k2[sl], v2[sl]
            kk = _dot1(kc, kc, _NT)
            qk = _dot1(qc, kc, _NT)
            for d in range(2):
                gcol, beta = gcols[d][sl], betas[d][sl]
                decay = jnp.where(incl[d], jnp.exp(gcol - grows[d][:, sl]), 0.0)
                m = jnp.where(strict[d], beta * kk * decay, 0.0)
                inst.append((half, d, qc, kc, vc, qk, gcol, beta, decay, m))
        ms = [t[-1] for t in inst]
        pws = [jnp.where(same_blk[0], -m, 0.0) for m in ms]
        tinvs = [eye + pw for pw in pws]
        for _ in range(2):
            pws = [_dot_inv(pw, pw) for pw in pws]
            tinvs = [t + _dot_inv(t, pw) for t, pw in zip(tinvs, pws)]
        for mask in off_masks:
            tmp = [_dot_inv(jnp.where(mask, m, 0.0), t) for m, t in zip(ms, tinvs)]
            tinvs = [t - _dot_inv(t, x) for t, x in zip(tinvs, tmp)]
        uws = [_dot1(tinv, jnp.concatenate([vc * beta, kc * (beta * jnp.exp(gcol))], axis=1)).astype(BF16)
               for (half, d, qc, kc, vc, qk, gcol, beta, decay, m), tinv in zip(inst, tinvs)]
        bas, ows = [], []
        for (half, d, qc, kc, vc, qk, gcol, beta, decay, m), uw in zip(inst, uws):
            glast = gcol[GDN_CHUNK - 1:GDN_CHUNK] if d == 0 else gcol[0:1]
            bas.append(_dg((kc * jnp.exp(glast - gcol)).astype(BF16), uw, _TN))
            ows.append(_dg((qk * decay).astype(BF16), uw, _NN))
        for (half, d, qc, kc, vc, qk, gcol, beta, decay, m), ba, ow in zip(inst, bas, ows):
            chunk0 = p * GDN_STEP + half * GDN_CHUNK
            o_s[d, pl.ds(pl.multiple_of(chunk0, GDN_CHUNK), GDN_CHUNK), :] = ow[:, :HEAD_DIM]
            b_s[d, pl.ds(pl.multiple_of(2 * chunk0, HEAD_DIM), HEAD_DIM), :] = ba[:, :HEAD_DIM]
            aq_s[d, pl.ds(pl.multiple_of(3 * chunk0, GDN_AQ), GDN_AQ), :] = jnp.concatenate(
                [ba[:, HEAD_DIM:], qc * jnp.exp(gcol) - ow[:, HEAD_DIM:]], axis=0).astype(BF16)
        return carry

    lax.fori_loop(0, UNIT // GDN_STEP, phase1, 0)

    lane1 = lax.broadcasted_iota(jnp.int32, (1, LANES), 1)

    def advance(chains, states):
        rs = [_dg(aq_s[d, pl.ds(pl.multiple_of(ch * GDN_AQ, GDN_AQ), GDN_AQ), :], s.astype(BF16), _NN)
              for (d, ch), s in zip(chains, states)]
        new = []
        for (d, ch), s, r in zip(chains, states, rs):
            tile0 = pl.multiple_of(ch * GDN_CHUNK + (GDN_CHUNK - 8 if d == 0 else 0), 8)
            last = col_ref[pl.ds(tile0, 8), :]
            last = last[7:8] if d == 0 else last[0:1]
            glast = jnp.sum(jnp.where(lane1 == d * HEADS + head, last, 0.0), axis=-1, keepdims=True)
            b = b_s[d, pl.ds(pl.multiple_of(ch * HEAD_DIM, HEAD_DIM), HEAD_DIM), :]
            new.append(s * jnp.exp(glast) + (b - r[:HEAD_DIM]))
            rows = pl.ds(pl.multiple_of(ch * GDN_CHUNK, GDN_CHUNK), GDN_CHUNK)
            o_s[d, rows, :] = o_s[d, rows, :] + r[HEAD_DIM:]
        return tuple(new)

    @pl.when(is_ctx)
    def _context_unit():
        def body(i, states):
            chains = [(d, sq * chunks_per_seq + (i if d == 0 else chunks_per_seq - 1 - i))
                      for sq in range(SEQ_PER_UNIT) for d in range(2)]
            return advance(chains, states)
        zero = jnp.zeros((HEAD_DIM, HEAD_DIM), F32)
        final = lax.fori_loop(0, chunks_per_seq, body, (zero,) * (2 * SEQ_PER_UNIT))
        for sq in range(SEQ_PER_UNIT):
            for d in range(2):
                s_ref[sq, state_slot, d] = final[2 * sq + d]

    @pl.when(jnp.logical_not(is_ctx))
    def _latent_unit():
        def body(i, states):
            return advance([(0, i), (1, n_chunks - 1 - i)], states)
        final = lax.fori_loop(0, n_chunks, body, (s0_ref[0], s0_ref[1]))
        for sq in range(SEQ_PER_UNIT):
            for d in range(2):
                s_ref[sq, state_slot, d] = final[d]

    o_ref[...] = _head_rmsnorm_gate(o_s[0] + o_s[1], g_ref[...], og_ref[...]).astype(o_ref.dtype)


def _gdn_call(z, col, rowt, conv_w, onorm_g, state, layer, new_states):
    def zcol(off):
        return pl.BlockSpec((None, UNIT, HEAD_DIM), lambda h, g: (off // HEAD_DIM + h, _scan_unit(g), 0))
    def wcol(part):
        return pl.BlockSpec((CONV_K * CONV_K, HEAD_DIM), lambda h, g: (0, part * HEADS + h))
    scr = lambda *shape: pltpu.VMEM(shape, F32)
    in_specs = [zcol(OFF_GQ), zcol(OFF_GK), zcol(OFF_GV), zcol(OFF_GG),
                pl.BlockSpec((UNIT, LANES), lambda h, g: (_scan_unit(g), 0)),
                pl.BlockSpec((LANES, UNIT), lambda h, g: (0, _scan_unit(g))),
                wcol(0), wcol(1), wcol(2),
                pl.BlockSpec((1, HEAD_DIM), lambda h, g: (0, 0)),
                _state_in_spec(layer)]
    args = [z, z, z, z, col, rowt, conv_w, conv_w, conv_w, onorm_g.reshape(1, HEAD_DIM), state]
    first = new_states is None
    kernel, aliases = functools.partial(_gdn_kernel, state_slot=layer if first else 0), {}
    if not first:
        kernel, aliases = _without_arg(kernel, len(args)), {len(args): 1}
        in_specs.append(pl.BlockSpec(memory_space=pl.ANY))
        args.append(new_states)
    return pl.pallas_call(
        kernel,
        grid=(HEADS, N_UNITS),
        in_specs=in_specs,
        out_specs=[pl.BlockSpec((None, UNIT, HEAD_DIM), lambda h, g: (h, _scan_unit(g), 0)), _state_out_spec(layer, first)],
        out_shape=[jax.ShapeDtypeStruct((HEADS, N_TOK, HEAD_DIM), BF16), _STATE_SHAPE],
        input_output_aliases=aliases,
        scratch_shapes=[scr(UNIT + 2 * CONV_PAD, HEAD_DIM),
                        scr(UNIT, HEAD_DIM), scr(UNIT, HEAD_DIM), scr(UNIT, HEAD_DIM),
                        scr(2, UNIT, HEAD_DIM),
                        scr(2, (UNIT // GDN_CHUNK) * HEAD_DIM, HEAD_DIM),
                        pltpu.VMEM((2, (UNIT // GDN_CHUNK) * GDN_AQ, HEAD_DIM), BF16)],
        compiler_params=_cparams(("arbitrary", "arbitrary")),
        name="gdn_scan",
    )(*args)


def kernel(x_prompt, x_sample, c, state_hgrn, state_gdn, c_ctx, norm1_g, norm2_g, w_mod, b_mod, w_in, hg_lb, hg_onorm_g, cm_vnorm_g, cm_ws, cm_bs, gdn_conv, gdn_A_log, gdn_dt_bias, gdn_onorm_g, w_br_hg, w_br_cm, w_br_gdn, w_out, w_ff1, w_ff2, final_g):
    x = (x_prompt.reshape(N_CTX_TOK, D_MODEL), x_sample.reshape(N_LAT_TOK, D_MODEL))
    cvec = jnp.concatenate([c_ctx[None, :], c, jnp.zeros((MOD_ROWS - 1 - DEC_BATCH, D_MODEL), F32)], axis=0)
    mod3 = _mod_call(cvec, w_mod, b_mod).reshape(DEPTH * MOD_ROWS, 1, 6 * D_MODEL)

    lb_all = jnp.cumsum(jax.nn.softmax(hg_lb.astype(F32), axis=0), axis=0)
    lb_all = lb_all - lb_all[:1]

    w_in_t = jnp.swapaxes(w_in, 1, 2)
    w_ff2_bf = w_ff2.astype(BF16)

    new_hg = new_gdn = None
    h = _normmod_call(x, norm1_g[0], mod3, 0, 0, 1)
    for l in range(DEPTH):
        z = _mm_call(h, w_in_t, l, col0=0, n=N_MAIN, act=None, out_dtype=F32, tm=1024, tn=1024,
                     w_transposed=True, lane_tile_major=True, name="in_proj")
        ab = _mm_call(h, w_in_t, l, col0=OFF_AB, n=LANES, act=None, out_dtype=F32, tm=1024, tn=LANES,
                      w_transposed=True, name="in_proj_ab")
        o_a, new_hg = _hgrn_call(z, lb_all[l], hg_onorm_g[l], state_hgrn, l, new_hg)
        o_b = _gmlp_call(z, cm_vnorm_g[l], cm_ws[l], cm_bs[l].T)
        col, rowt = _gdn_gates_call(ab, gdn_A_log[l], gdn_dt_bias[l])
        o_c, new_gdn = _gdn_call(z, col, rowt, gdn_conv[l].reshape(CONV_K * CONV_K, 3 * HG_F),
                                 gdn_onorm_g[l], state_gdn, l, new_gdn)

        merged = _gated_merge_call(h, w_in_t, o_a, o_b, o_c, w_br_hg, w_br_cm, w_br_gdn, l)
        x, h2 = _outproj_norm_call(merged, w_out, x, norm2_g[l], mod3, l)
        up = _mm_call(h2, w_ff1, l, col0=0, n=D_FF, act="relu2", out_dtype=BF16, tm=2048, tn=1024, name="ffn_up")
        if l + 1 < DEPTH:
            x, h = _ffn_down_call(up, w_ff2_bf, x, mod3, l, norm1_g[l + 1])
        else:
            (x,) = _ffn_down_call(up, w_ff2_bf, x, mod3, l, None)

    y_prompt = _final_norm_call(x, final_g, 0, N_CTX_TOK).reshape(BATCH, SEQ, D_MODEL)
    y_sample = _final_norm_call(x, final_g, N_CTX_TOK, N_LAT_TOK).reshape(DEC_BATCH, DEC_SEQ, D_MODEL)
    return (y_prompt, y_sample, new_hg, new_gdn)
```

```python
import functools

import jax
import jax.numpy as jnp
from jax import lax
from jax.experimental import pallas as pl
from jax.experimental.pallas import tpu as pltpu

F32 = jnp.float32
BF16 = jnp.bfloat16

D_MODEL = 2048
BATCH = 16
SEQ = 256
DEPTH = 2
DEC_BATCH = 4
DEC_SEQ = 1024
GRID_W = 64
EPS = 1e-6
D_FF = 4 * D_MODEL
HEADS = 8
HEAD_DIM = 128
HG_F = HEADS * HEAD_DIM
CM_GROUPS = 8
CM_W = CM_GROUPS * HEAD_DIM
CM_CHUNK = 128
GDN_CHUNK = 64
CONV_K = 3

OFF_HQ, OFF_HI, OFF_HG, OFF_HFF, OFF_HFB = 0, 1024, 2048, 3072, 4096
OFF_CU, OFF_CV = 5120, 6144
OFF_GQ, OFF_GK, OFF_GV, OFF_GG = 7168, 8192, 9216, 10240
OFF_AB = 11264
OFF_GATES = 11296
IN_DIM = 17440
N_MAIN = OFF_AB

N_CTX_TOK = BATCH * SEQ
N_LAT_TOK = DEC_BATCH * DEC_SEQ
N_TOK = N_CTX_TOK + N_LAT_TOK
UNIT = DEC_SEQ
N_CTX_UNITS = N_CTX_TOK // UNIT
N_UNITS = N_TOK // UNIT
N_LAT_UNITS = N_UNITS - N_CTX_UNITS
SEQ_PER_UNIT = UNIT // SEQ

LANES = 128
MOD_ROWS = 8
HG_BLOCK = 256
HG_SUB = 32
HG_INST = 2 * (UNIT // HG_BLOCK)
VMEM_LIMIT = 56 * 1024 * 1024

_NT = (((1,), (1,)), ((), ()))
_TN = (((0,), (0,)), ((), ()))
_NN = (((1,), (0,)), ((), ()))


def _dg(a, b, dims):
    return lax.dot_general(a, b, dims, preferred_element_type=F32)


def _split2(x):
    hi = x.astype(BF16)
    lo = (x - hi.astype(F32)).astype(BF16)
    return hi, lo


def _dot1(a, b, dims=_NN):
    return _dg(a.astype(BF16), b.astype(BF16), dims)


def _dot3(a, b, dims=_NN):
    ah, al = _split2(a)
    bh, bl = _split2(b)
    return _dg(ah, bh, dims) + (_dg(ah, bl, dims) + _dg(al, bh, dims))


_dot_inv = _dot1


def _dot01(m, x):
    hi = x.astype(BF16)
    r = x - hi.astype(F32)
    mid = r.astype(BF16)
    lo = (r - mid.astype(F32)).astype(BF16)
    return _dg(m, hi, _NN) + (_dg(m, mid, _NN) + _dg(m, lo, _NN))


def _dot01_2(m, x):
    hi, lo = _split2(x)
    return _dg(m, hi, _NN) + _dg(m, lo, _NN)


def _sigmoid(x):
    return 1.0 / (1.0 + jnp.exp(-x))


def _silu(x):
    return x * _sigmoid(x)


def _gelu(x):
    return 0.5 * x * (1.0 + lax.erf(x * (2.0 ** -0.5)))


def _softplus(x):
    return jnp.maximum(x, 0.0) + jnp.log1p(jnp.exp(-jnp.abs(x)))


def _mod_row_of_tile(i, tm):
    return jnp.maximum(0, (i * tm - N_CTX_TOK) // DEC_SEQ + 1)


assert N_LAT_UNITS == N_CTX_UNITS


def _scan_unit(g):
    return jnp.where(g % 2 == 0, N_CTX_UNITS + g // 2, g // 2)


def _is_ctx_step(g):
    return g % 2 == 1


def _state_out_spec(layer, first):
    if first:
        return pl.BlockSpec((SEQ_PER_UNIT, DEPTH, 2, None, HEAD_DIM, HEAD_DIM), lambda h, g: (g // 2, 0, 0, h, 0, 0))
    return pl.BlockSpec((SEQ_PER_UNIT, 1, 2, None, HEAD_DIM, HEAD_DIM), lambda h, g: (g // 2, layer, 0, h, 0, 0))


def _zero_other_slots(s_ref, slot):
    for other in range(s_ref.shape[1]):
        if other != slot:
            s_ref[:, other] = jnp.zeros((s_ref.shape[0],) + tuple(s_ref.shape[2:]), F32)


def _state_in_spec(layer):
    return pl.BlockSpec((None, None, 2, None, HEAD_DIM, HEAD_DIM),
                        lambda h, g: (g // 2, layer, 0, h, 0, 0))


_STATE_SHAPE = jax.ShapeDtypeStruct((BATCH, DEPTH, 2, HEADS, HEAD_DIM, HEAD_DIM), F32)


def _without_arg(kernel, pos):
    def wrapped(*refs):
        return kernel(*refs[:pos], *refs[pos + 1:])
    return wrapped


def _cparams(sem):
    return pltpu.CompilerParams(dimension_semantics=sem, vmem_limit_bytes=VMEM_LIMIT)


def _mod_kernel(c_ref, w_ref, b_ref, o_ref):
    s = _silu(c_ref[...])
    hi, lo = _split2(s)
    w = w_ref[...].astype(BF16)
    o_ref[...] = _dg(hi, w, _NN) + _dg(lo, w, _NN) + b_ref[...]


def _mod_call(cvec, w_mod, b_mod):
    tn = 1024
    n = 6 * D_MODEL
    return pl.pallas_call(
        _mod_kernel,
        grid=(DEPTH, n // tn),
        in_specs=[pl.BlockSpec((MOD_ROWS, D_MODEL), lambda l, j: (0, 0)),
                  pl.BlockSpec((None, D_MODEL, tn), lambda l, j: (l, 0, j)),
                  pl.BlockSpec((None, 1, tn), lambda l, j: (l, 0, j))],
        out_specs=pl.BlockSpec((None, MOD_ROWS, tn), lambda l, j: (l, 0, j)),
        out_shape=jax.ShapeDtypeStruct((DEPTH, MOD_ROWS, n), F32),
        compiler_params=_cparams(("arbitrary", "arbitrary")),
        name="modulation",
    )(cvec, w_mod, b_mod.reshape(DEPTH, 1, n))


def _adaln(x, g, shift, scale):
    y = x * lax.rsqrt(jnp.mean(x * x, axis=-1, keepdims=True) + EPS) * g
    return y * (1.0 + scale) + shift


def _residual_specs(x, tm, idx):
    if not isinstance(x, tuple):
        return [pl.BlockSpec((tm, D_MODEL), lambda *g: (idx(*g), 0))], [x]
    n_ctx = N_CTX_TOK // tm
    return ([pl.BlockSpec((tm, D_MODEL), lambda *g: (jnp.minimum(idx(*g), n_ctx - 1), 0)),
             pl.BlockSpec((tm, D_MODEL), lambda *g: (jnp.maximum(idx(*g) - n_ctx, 0), 0))], list(x))


def _residual_tile(refs, tile, tm):
    if len(refs) == 1:
        return refs[0][...]
    return jnp.where(tile < N_CTX_TOK // tm, refs[0][...], refs[1][...])


def _normmod_kernel(*refs, n_x):
    x_refs, (g_ref, sh_ref, sc_ref, o_ref) = refs[:n_x], refs[n_x:]
    x = _residual_tile(x_refs, pl.program_id(0), o_ref.shape[0])
    o_ref[...] = _adaln(x, g_ref[...], sh_ref[...], sc_ref[...]).astype(o_ref.dtype)


def _normmod_call(x, g, mod3, layer, k_shift, k_scale):
    tm = 512
    def mod_spec(k):
        return pl.BlockSpec((None, 1, D_MODEL),
                            lambda i: (layer * MOD_ROWS + _mod_row_of_tile(i, tm), 0, k))
    x_specs, x_args = _residual_specs(x, tm, lambda i: i)
    return pl.pallas_call(
        functools.partial(_normmod_kernel, n_x=len(x_args)),
        grid=(N_TOK // tm,),
        in_specs=x_specs + [pl.BlockSpec((1, D_MODEL), lambda i: (0, 0)), mod_spec(k_shift), mod_spec(k_scale)],
        out_specs=pl.BlockSpec((tm, D_MODEL), lambda i: (i, 0)),
        out_shape=jax.ShapeDtypeStruct((N_TOK, D_MODEL), BF16),
        compiler_params=_cparams(("arbitrary",)),
        name="norm_mod",
    )(*x_args, g.reshape(1, D_MODEL), mod3, mod3)


def _final_norm_kernel(x_ref, g_ref, o_ref):
    x = x_ref[...]
    o_ref[...] = x * lax.rsqrt(jnp.mean(x * x, axis=-1, keepdims=True) + EPS) * g_ref[...]


def _final_norm_call(x, g, row0, n_rows):
    tm = 512
    return pl.pallas_call(
        _final_norm_kernel,
        grid=(n_rows // tm,),
        in_specs=[pl.BlockSpec((tm, D_MODEL), lambda i: (row0 // tm + i, 0)),
                  pl.BlockSpec((1, D_MODEL), lambda i: (0, 0))],
        out_specs=pl.BlockSpec((tm, D_MODEL), lambda i: (i, 0)),
        out_shape=jax.ShapeDtypeStruct((n_rows, D_MODEL), F32),
        compiler_params=_cparams(("arbitrary",)),
        name="final_norm",
    )(x, g.reshape(1, D_MODEL))


def _mm_kernel(x_ref, w_ref, o_ref, wbf_ref, *, act, w_transposed):
    @pl.when(pl.program_id(1) == 0)
    def _cast_weights():
        wbf_ref[...] = w_ref[...].reshape(wbf_ref.shape).astype(BF16)

    acc = lax.dot_general(x_ref[...], wbf_ref[...], _NT if w_transposed else _NN, preferred_element_type=F32)
    if act == "relu2":
        acc = jnp.square(jnp.maximum(acc, 0.0))
    if len(o_ref.shape) == 3:
        for c in range(o_ref.shape[0]):
            o_ref[c] = acc[:, c * LANES:(c + 1) * LANES].astype(o_ref.dtype)
    else:
        o_ref[...] = acc.astype(o_ref.dtype)


def _mm_call(x, w_all, layer, *, col0, n, act, out_dtype, tm, tn, name, w_transposed=False,
             lane_tile_major=False):
    m, k = x.shape
    assert n % tn == 0 and m % tm == 0
    if w_transposed:
        assert col0 % 8 == 0
        w_spec = pl.BlockSpec((pl.Element(1), pl.Element(tn), pl.Element(k)),
                              lambda j, i: (layer, pl.multiple_of(col0 + j * tn, 8), 0))
        w_scratch = pltpu.VMEM((tn, k), BF16)
    else:
        assert col0 % tn == 0
        w_spec = pl.BlockSpec((None, k, tn), lambda j, i: (layer, 0, col0 // tn + j))
        w_scratch = pltpu.VMEM((k, tn), BF16)
    if lane_tile_major:
        out_spec = pl.BlockSpec((tn // LANES, tm, LANES), lambda j, i: (j, i, 0))
        out_shape = jax.ShapeDtypeStruct((n // LANES, m, LANES), out_dtype)
    else:
        out_spec = pl.BlockSpec((tm, tn), lambda j, i: (i, j))
        out_shape = jax.ShapeDtypeStruct((m, n), out_dtype)
    return pl.pallas_call(
        functools.partial(_mm_kernel, act=act, w_transposed=w_transposed),
        grid=(n // tn, m // tm),
        in_specs=[pl.BlockSpec((tm, k), lambda j, i: (i, 0)), w_spec],
        out_specs=out_spec,
        out_shape=out_shape,
        scratch_shapes=[w_scratch],
        compiler_params=_cparams(("arbitrary", "arbitrary")),
        name=name,
    )(x, w_all)


def _mm_bf16_kernel(x_ref, w_ref, r_ref, g_ref, o_ref):
    acc = jnp.dot(x_ref[...], w_ref[...], preferred_element_type=F32)
    o_ref[...] = r_ref[...] + g_ref[...] * acc


def _mm_resid_call(x, w, resid, mod3, layer, k_gate, *, tm, tn, name):
    m, k = x.shape
    n = w.shape[2]
    per_tile = D_MODEL // tn
    return pl.pallas_call(
        _mm_bf16_kernel,
        grid=(m // tm, n // tn),
        in_specs=[pl.BlockSpec((tm, k), lambda i, j: (i, 0)),
                  pl.BlockSpec((None, k, tn), lambda i, j: (layer, 0, j)),
                  pl.BlockSpec((tm, tn), lambda i, j: (i, j)),
                  pl.BlockSpec((None, 1, tn),
                               lambda i, j: (layer * MOD_ROWS + _mod_row_of_tile(i, tm), 0,
                                             k_gate * per_tile + j))],
        out_specs=pl.BlockSpec((tm, tn), lambda i, j: (i, j)),
        out_shape=jax.ShapeDtypeStruct((m, n), F32),
        compiler_params=_cparams(("arbitrary", "arbitrary")),
        name=name,
    )(x, w, resid, mod3)


def _outproj_norm_kernel(*refs, n_x):
    m_ref, w_ref = refs[:2]
    x_refs = refs[2:2 + n_x]
    gate_ref, g_ref, sh_ref, sc_ref, xo_ref, h_ref, wbf_ref = refs[2 + n_x:]

    @pl.when(pl.program_id(0) == 0)
    def _cast_weights():
        wbf_ref[...] = w_ref[...].astype(BF16)

    acc = jnp.dot(m_ref[...], wbf_ref[...], preferred_element_type=F32)
    xn = _residual_tile(x_refs, pl.program_id(0), xo_ref.shape[0]) + gate_ref[...] * acc
    xo_ref[...] = xn
    h_ref[...] = _adaln(xn, g_ref[...], sh_ref[...], sc_ref[...]).astype(h_ref.dtype)


def _outproj_norm_call(merged, w_out, x, norm_g, mod3, layer):
    tm = 256 if isinstance(x, tuple) else 512
    def mod_spec(k):
        return pl.BlockSpec((None, 1, D_MODEL),
                            lambda i: (layer * MOD_ROWS + _mod_row_of_tile(i, tm), 0, k))
    row = pl.BlockSpec((tm, D_MODEL), lambda i: (i, 0))
    x_specs, x_args = _residual_specs(x, tm, lambda i: i)
    return pl.pallas_call(
        functools.partial(_outproj_norm_kernel, n_x=len(x_args)),
        grid=(N_TOK // tm,),
        in_specs=[row,
                  pl.BlockSpec((None, D_MODEL, D_MODEL), lambda i: (layer, 0, 0), pipeline_mode=pl.Buffered(1))]
                 + x_specs + [mod_spec(2), pl.BlockSpec((1, D_MODEL), lambda i: (0, 0)), mod_spec(3), mod_spec(4)],
        out_specs=[row, row],
        out_shape=[jax.ShapeDtypeStruct((N_TOK, D_MODEL), F32),
                   jax.ShapeDtypeStruct((N_TOK, D_MODEL), BF16)],
        scratch_shapes=[pltpu.VMEM((D_MODEL, D_MODEL), BF16)],
        compiler_params=_cparams(("arbitrary",)),
        name="out_proj_norm",
    )(merged, w_out, *x_args, mod3, norm_g.reshape(1, D_MODEL), mod3, mod3)


def _ffn_down_norm_kernel(x_ref, w_ref, r_ref, gate_ref, g_ref, sh_ref, sc_ref, o_ref, h_ref, row_ref):
    j = pl.program_id(1)
    tn = o_ref.shape[1]
    acc = jnp.dot(x_ref[...], w_ref[...], preferred_element_type=F32)
    xn = r_ref[...] + gate_ref[...] * acc
    o_ref[...] = xn
    row_ref[:, pl.ds(pl.multiple_of(j * tn, tn), tn)] = xn

    @pl.when(j == pl.num_programs(1) - 1)
    def _next_layer_norm():
        h_ref[...] = _adaln(row_ref[...], g_ref[...], sh_ref[...], sc_ref[...]).astype(h_ref.dtype)


def _ffn_down_norm_call(up, w_bf, resid, mod3, layer, next_norm_g):
    tm, tn = 512, 512
    m, k = up.shape
    per_tile = D_MODEL // tn
    def mod_row(i):
        return _mod_row_of_tile(i, tm)
    def next_mod(kk):
        return pl.BlockSpec((None, 1, D_MODEL), lambda i, j: ((layer + 1) * MOD_ROWS + mod_row(i), 0, kk))
    return pl.pallas_call(
        _ffn_down_norm_kernel,
        grid=(m // tm, D_MODEL // tn),
        in_specs=[pl.BlockSpec((tm, k), lambda i, j: (i, 0)),
                  pl.BlockSpec((None, k, tn), lambda i, j: (layer, 0, j)),
                  pl.BlockSpec((tm, tn), lambda i, j: (i, j)),
                  pl.BlockSpec((None, 1, tn), lambda i, j: (layer * MOD_ROWS + mod_row(i), 0, 5 * per_tile + j)),
                  pl.BlockSpec((1, D_MODEL), lambda i, j: (0, 0)), next_mod(0), next_mod(1)],
        out_specs=[pl.BlockSpec((tm, tn), lambda i, j: (i, j)),
                   pl.BlockSpec((tm, D_MODEL), lambda i, j: (i, 0))],
        out_shape=[jax.ShapeDtypeStruct((m, D_MODEL), F32),
                   jax.ShapeDtypeStruct((m, D_MODEL), BF16)],
        scratch_shapes=[pltpu.VMEM((tm, D_MODEL), F32)],
        compiler_params=_cparams(("arbitrary", "arbitrary")),
        name="ffn_down_norm",
    )(up, w_bf, resid, mod3, next_norm_g.reshape(1, D_MODEL), mod3, mod3)


def _gated_merge_kernel(h_ref, wga_ref, wgb_ref, wgc_ref, oa_ref, ob_ref, oc_ref, wa_ref, wb_ref, wc_ref,
                        o_ref, wg_bf, wbr_bf):
    @pl.when(pl.program_id(1) == 0)
    def _cast_weights():
        for t, ref in enumerate((wga_ref, wgb_ref, wgc_ref)):
            wg_bf[t] = ref[...].reshape(wg_bf.shape[1:]).astype(BF16)
        for t, ref in enumerate((wa_ref, wb_ref, wc_ref)):
            wbr_bf[t] = ref[...].astype(BF16)

    def rows_of(ref):
        return jnp.concatenate([ref[t] for t in range(ref.shape[0])], axis=1)

    h = h_ref[...]
    acc = None
    for t, br_ref in enumerate((oa_ref, ob_ref, oc_ref)):
        gate = _sigmoid(lax.dot_general(h, wg_bf[t], _NT, preferred_element_type=F32))
        term = gate * jnp.dot(rows_of(br_ref), wbr_bf[t], preferred_element_type=F32)
        acc = term if acc is None else acc + term
    o_ref[...] = acc.astype(o_ref.dtype)


def _gated_merge_call(h, w_in_t, o_a, o_b, o_c, w_a, w_b, w_c, layer):
    tm, tn = 1024, 256
    def gate_w(b):
        return pl.BlockSpec((pl.Element(1), pl.Element(tn), pl.Element(D_MODEL)),
                            lambda j, i: (layer, pl.multiple_of(OFF_GATES + b * D_MODEL + j * tn, 8), 0))
    br = pl.BlockSpec((HEADS, tm, HEAD_DIM), lambda j, i: (0, i, 0))
    wt = pl.BlockSpec((None, HG_F, tn), lambda j, i: (layer, 0, j))
    return pl.pallas_call(
        _gated_merge_kernel,
        grid=(D_MODEL // tn, N_TOK // tm),
        in_specs=[pl.BlockSpec((tm, D_MODEL), lambda j, i: (i, 0)), gate_w(0), gate_w(1), gate_w(2),
                  br, br, br, wt, wt, wt],
        out_specs=pl.BlockSpec((tm, tn), lambda j, i: (i, j)),
        out_shape=jax.ShapeDtypeStruct((N_TOK, D_MODEL), BF16),
        scratch_shapes=[pltpu.VMEM((3, tn, D_MODEL), BF16), pltpu.VMEM((3, HG_F, tn), BF16)],
        compiler_params=_cparams(("arbitrary", "arbitrary")),
        name="gated_merge",
    )(h, w_in_t, w_in_t, w_in_t, o_a, o_b, o_c, w_a, w_b, w_c)


def _head_rmsnorm_gate(o, g, og):
    ones = jnp.ones((HEAD_DIM, HEAD_DIM), BF16)
    mean_sq = _dg((o * o).astype(BF16), ones, _NN) * (1.0 / HEAD_DIM)
    y = o * lax.rsqrt(mean_sq + EPS) * g
    return y * _silu(og)


def _hgrn_kernel(q_ref, i_ref, og_ref, ff_ref, fb_ref, lb_ref, g_ref, s0_ref, o_ref, s_ref,
                 acc_ref, b_ref, stc_ref, km_ref, *, state_slot):
    is_ctx = _is_ctx_step(pl.program_id(1))
    _zero_other_slots(s_ref, state_slot)
    nblk = UNIT // HG_BLOCK
    nsub = HG_BLOCK // HG_SUB
    r = lax.broadcasted_iota(jnp.int32, (HG_BLOCK, HG_BLOCK), 0)
    c = lax.broadcasted_iota(jnp.int32, (HG_BLOCK, HG_BLOCK), 1)
    same = (r // HG_SUB) == (c // HG_SUB)
    row_sub = lax.broadcasted_iota(jnp.int32, (HG_BLOCK, HEAD_DIM), 0) // HG_SUB
    tri = (jnp.logical_and(same, c <= r), jnp.logical_and(same, c >= r))
    tri_bf = [jnp.where(t, 1.0, 0.0).astype(BF16) for t in tri]
    z_refs = (ff_ref, fb_ref)
    inst = [(blk, d) for blk in range(nblk) for d in range(2)]

    def expand(x):
        return jnp.broadcast_to(x[:, None, :], (nsub, HG_SUB, HEAD_DIM)).reshape(HG_BLOCK, HEAD_DIM)

    def blk_rows(blk):
        return slice(blk * HG_BLOCK, (blk + 1) * HG_BLOCK)

    qs = [_silu(q_ref[blk_rows(blk), :]) for blk in range(nblk)]
    vs = [i_ref[blk_rows(blk), :] for blk in range(nblk)]
    ks, lfs = [], []
    for blk, d in inst:
        lb = lb_ref[d:d + 1, :]
        f = lb + (1.0 - lb) * _sigmoid(z_refs[d][blk_rows(blk), :])
        lfs.append(jnp.log(f))
        ks.append(1.0 - f)
    bs = [_dot01_2(tri_bf[d], lf) for (blk, d), lf in zip(inst, lfs)]
    tots, qts, kts, qds = [], [], [], []
    for n, (blk, d) in enumerate(inst):
        b = bs[n]
        b_ref[n] = b
        tot = b_ref[n, pl.ds((HG_SUB - 1) if d == 0 else 0, nsub, stride=HG_SUB), :]
        mid_f = expand(b_ref[n, pl.ds(HG_SUB // 2, nsub, stride=HG_SUB), :])
        tots.append(tot)
        qts.append(qs[blk] * jnp.exp(b - mid_f))
        kts.append(ks[n] * jnp.exp(mid_f - b))
        qds.append((qs[blk] * jnp.exp(b)).astype(BF16))
        kd = ks[n] * jnp.exp(expand(tot) - b)
        for s in range(nsub):
            km_ref[n, :, s * HEAD_DIM:(s + 1) * HEAD_DIM] = jnp.where(row_sub == s, kd, 0.0).astype(BF16)
    scs = [jnp.where(tri[d], _dot1(qt, kt, _NT), 0.0) for (blk, d), qt, kt in zip(inst, qts, kts)]
    uts = [_dg(vs[blk].astype(BF16), km_ref[n], _TN) for n, (blk, d) in enumerate(inst)]
    outs = [_dot1(sc, vs[blk]) for (blk, d), sc in zip(inst, scs)]
    for d in range(2):
        st = s0_ref[d].T
        for blk in (range(nblk) if d == 0 else range(nblk - 1, -1, -1)):
            n = inst.index((blk, d))
            st = jnp.where(is_ctx, 0.0, st)
            for s in (range(nsub) if d == 0 else range(nsub - 1, -1, -1)):
                stc_ref[n, s] = st.T.astype(BF16)
                st = st * jnp.exp(tots[n][s:s + 1, :]) + uts[n][:, s * HEAD_DIM:(s + 1) * HEAD_DIM]
            s_ref[blk, state_slot, d] = st.T
    for n, (blk, d) in enumerate(inst):
        o_int = [_dg(qds[n][s * HG_SUB:(s + 1) * HG_SUB], stc_ref[n, s], _NN) for s in range(nsub)]
        acc_ref[d, blk_rows(blk), :] = outs[n] + jnp.concatenate(o_int, axis=0)
    o_ref[...] = _head_rmsnorm_gate(acc_ref[0] + acc_ref[1], g_ref[...], og_ref[...]).astype(o_ref.dtype)


def _hgrn_call(z, lb, onorm_g, state, layer, new_states):
    def col(off):
        return pl.BlockSpec((None, UNIT, HEAD_DIM), lambda h, g: (off // HEAD_DIM + h, _scan_unit(g), 0))
    in_specs = [col(OFF_HQ), col(OFF_HI), col(OFF_HG), col(OFF_HFF), col(OFF_HFB),
                pl.BlockSpec((2, HEAD_DIM), lambda h, g: (0, h)),
                pl.BlockSpec((1, HEAD_DIM), lambda h, g: (0, 0)),
                _state_in_spec(layer)]
    args = [z, z, z, z, z, lb, onorm_g.reshape(1, HEAD_DIM), state]
    first = new_states is None
    kernel, aliases = functools.partial(_hgrn_kernel, state_slot=layer if first else 0), {}
    if not first:
        kernel, aliases = _without_arg(kernel, len(args)), {len(args): 1}
        in_specs.append(pl.BlockSpec(memory_space=pl.ANY))
        args.append(new_states)
    return pl.pallas_call(
        kernel,
        grid=(HEADS, N_UNITS),
        in_specs=in_specs,
        out_specs=[pl.BlockSpec((None, UNIT, HEAD_DIM), lambda h, g: (h, _scan_unit(g), 0)), _state_out_spec(layer, first)],
        out_shape=[jax.ShapeDtypeStruct((HEADS, N_TOK, HEAD_DIM), BF16), _STATE_SHAPE],
        input_output_aliases=aliases,
        scratch_shapes=[pltpu.VMEM((2, UNIT, HEAD_DIM), F32),
                        pltpu.VMEM((HG_INST, HG_BLOCK, HEAD_DIM), F32),
                        pltpu.VMEM((HG_INST, HG_BLOCK // HG_SUB, HEAD_DIM, HEAD_DIM), BF16),
                        pltpu.VMEM((HG_INST, HG_BLOCK, (HG_BLOCK // HG_SUB) * HEAD_DIM), BF16)],
        compiler_params=_cparams(("arbitrary", "arbitrary")),
        name="hgrn2_scan",
    )(*args)


def _gmlp_kernel(u_ref, v_ref, vn_ref, ws_ref, bs_ref, o_ref):
    tm = u_ref.shape[1]
    for g in range(CM_GROUPS):
        cols = slice(g * HEAD_DIM, (g + 1) * HEAD_DIM)
        vg = _gelu(v_ref[g])
        vg = vg * lax.rsqrt(jnp.mean(vg * vg, axis=-1, keepdims=True) + EPS) * vn_ref[:, cols]
        ug = _gelu(u_ref[g])
        w = ws_ref[g]
        bias = bs_ref[:, g:g + 1]
        for ch in range(tm // CM_CHUNK):
            rows = slice(ch * CM_CHUNK, (ch + 1) * CM_CHUNK)
            s = _dot3(w, vg[rows]) + bias
            o_ref[g, rows, :] = (ug[rows] * s).astype(o_ref.dtype)


def _gmlp_call(z, vnorm_g, ws, bs_t):
    tm = 512
    return pl.pallas_call(
        _gmlp_kernel,
        grid=(N_TOK // tm,),
        in_specs=[pl.BlockSpec((CM_GROUPS, tm, HEAD_DIM), lambda i: (OFF_CU // CM_W, i, 0)),
                  pl.BlockSpec((CM_GROUPS, tm, HEAD_DIM), lambda i: (OFF_CV // CM_W, i, 0)),
                  pl.BlockSpec((1, CM_W), lambda i: (0, 0)),
                  pl.BlockSpec((CM_GROUPS, CM_CHUNK, CM_CHUNK), lambda i: (0, 0, 0)),
                  pl.BlockSpec((CM_CHUNK, CM_GROUPS), lambda i: (0, 0))],
        out_specs=pl.BlockSpec((CM_GROUPS, tm, HEAD_DIM), lambda i: (0, i, 0)),
        out_shape=jax.ShapeDtypeStruct((CM_GROUPS, N_TOK, HEAD_DIM), BF16),
        compiler_params=_cparams(("arbitrary",)),
        name="chunk_gmlp",
    )(z, z, vnorm_g.reshape(1, CM_W), ws, bs_t)


GP_BLOCK = 256


def _gdn_gates_kernel(ab_ref, alog_ref, dt_ref, col_ref, rowt_ref):
    ab = ab_ref[...]
    lane = lax.broadcasted_iota(jnp.int32, ab.shape, 1)
    g = jnp.where(lane < 2 * HEADS, -jnp.exp(alog_ref[...]) * _softplus(ab + dt_ref[...]), 0.0)
    r = lax.broadcasted_iota(jnp.int32, (GP_BLOCK, GP_BLOCK), 0)
    c = lax.broadcasted_iota(jnp.int32, (GP_BLOCK, GP_BLOCK), 1)
    same = (r // GDN_CHUNK) == (c // GDN_CHUNK)
    tri_f = jnp.where(jnp.logical_and(same, c <= r), 1.0, 0.0).astype(BF16)
    tri_b = jnp.where(jnp.logical_and(same, c >= r), 1.0, 0.0).astype(BF16)
    cf = _dot01(tri_f, g)
    cb = _dot01(tri_b, g)
    col = jnp.where(lane < HEADS, cf, jnp.where(lane < 2 * HEADS, cb, _sigmoid(ab)))
    col_ref[...] = col
    rowt_ref[...] = col.T


def _gdn_gates_call(ab, a_log, dt_bias):
    pad = lambda t: jnp.pad(t.reshape(1, 2 * HEADS), ((0, 0), (0, LANES - 2 * HEADS)))
    return pl.pallas_call(
        _gdn_gates_kernel,
        grid=(N_TOK // GP_BLOCK,),
        in_specs=[pl.BlockSpec((GP_BLOCK, LANES), lambda i: (i, 0)),
                  pl.BlockSpec((1, LANES), lambda i: (0, 0)),
                  pl.BlockSpec((1, LANES), lambda i: (0, 0))],
        out_specs=[pl.BlockSpec((GP_BLOCK, LANES), lambda i: (i, 0)),
                   pl.BlockSpec((LANES, GP_BLOCK), lambda i: (0, i))],
        out_shape=[jax.ShapeDtypeStruct((N_TOK, LANES), F32),
                   jax.ShapeDtypeStruct((LANES, N_TOK), F32)],
        compiler_params=_cparams(("arbitrary",)),
        name="gdn_gates",
    )(ab, pad(a_log), pad(dt_bias))


CONV_PAD = 72
GDN_AQ = HEAD_DIM + GDN_CHUNK
GDN_STEP = 16 * GDN_CHUNK


def _gdn_kernel(qr_ref, kr_ref, vr_ref, og_ref, col_ref, rowt_ref, cwq_ref, cwk_ref, cwv_ref,
                g_ref, s0_ref, o_ref, s_ref,
                xp_ref, q_s, k_s, v_s, o_s, b_s, aq_s, *, state_slot):
    is_ctx = _is_ctx_step(pl.program_id(1))
    _zero_other_slots(s_ref, state_slot)
    head = pl.program_id(0)
    n_chunks = UNIT // GDN_CHUNK
    chunks_per_seq = SEQ // GDN_CHUNK

    t = lax.broadcasted_iota(jnp.int32, (UNIT, 1), 0)
    period = jnp.where(is_ctx, SEQ, GRID_W)
    pos = jnp.bitwise_and(t, period - 1)
    ok_left = pos != 0
    ok_right = pos != period - 1
    zeros_pad = jnp.zeros((CONV_PAD, HEAD_DIM), F32)
    xp_ref[0:CONV_PAD, :] = zeros_pad
    xp_ref[CONV_PAD + UNIT:CONV_PAD + UNIT + CONV_PAD, :] = zeros_pad

    def conv_silu(x_ref, w_ref):
        xp_ref[CONV_PAD:CONV_PAD + UNIT, :] = x_ref[...]
        acc = jnp.zeros((UNIT, HEAD_DIM), F32)
        for i in range(CONV_K):
            for j in range(CONV_K):
                w = w_ref[CONV_K * i + j:CONV_K * i + j + 1, :]
                if i != CONV_K // 2:
                    w = jnp.where(is_ctx, 0.0, w)
                start = CONV_PAD + (i - 1) * GRID_W + (j - 1)
                xs = xp_ref[start:start + UNIT, :]
                if j == 0:
                    xs = jnp.where(ok_left, xs, 0.0)
                elif j == CONV_K - 1:
                    xs = jnp.where(ok_right, xs, 0.0)
                acc = acc + xs * w
        return _silu(acc)

    def l2norm(x):
        return x * lax.rsqrt(jnp.sum(x * x, axis=-1, keepdims=True) + EPS)

    q_s[...] = l2norm(conv_silu(qr_ref, cwq_ref)) * (HEAD_DIM ** -0.5)
    k_s[...] = l2norm(conv_silu(kr_ref, cwk_ref))
    v_s[...] = conv_silu(vr_ref, cwv_ref)

    rr = lax.broadcasted_iota(jnp.int32, (GDN_CHUNK, GDN_CHUNK), 0)
    cc = lax.broadcasted_iota(jnp.int32, (GDN_CHUNK, GDN_CHUNK), 1)
    eye = jnp.where(rr == cc, 1.0, 0.0)
    same_blk = [(rr // b) == (cc // b) for b in (8, 16, 32, 64)]
    lane = lax.broadcasted_iota(jnp.int32, (GDN_STEP, LANES), 1)
    sub8 = lax.broadcasted_iota(jnp.int32, (HEADS, GDN_STEP), 0)

    def pick(x, j):
        return jnp.sum(jnp.where(lane[:x.shape[0]] == j, x, 0.0), axis=-1, keepdims=True)

    incl = (cc <= rr, cc >= rr)
    strict = (cc < rr, cc > rr)
    off_masks = [jnp.logical_and(same_blk[lvl], jnp.logical_not(same_blk[lvl - 1]))
                 for lvl in range(1, len(same_blk))]

    def phase1(p, carry):
        rows = pl.ds(pl.multiple_of(p * GDN_STEP, GDN_STEP), GDN_STEP)
        q2, k2, v2 = q_s[rows, :], k_s[rows, :], v_s[rows, :]
        col = col_ref[rows, :]
        gcols = [pick(col, d * HEADS + head) for d in range(2)]
        betas = [pick(col, (2 + d) * HEADS + head) for d in range(2)]
        grows = [jnp.sum(jnp.where(sub8 == head, rowt_ref[d * HEADS:(d + 1) * HEADS, rows], 0.0),
                         axis=0, keepdims=True) for d in range(2)]
        inst = []
        for half in range(GDN_STEP // GDN_CHUNK):
            sl = slice(half * GDN_CHUNK, (half + 1) * GDN_CHUNK)
            qc, kc, vc = q2[sl], k2[sl], v2[sl]
            kk = _dot1(kc, kc, _NT)
            qk = _dot1(qc, kc, _NT)
            for d in range(2):
                gcol, beta = gcols[d][sl], betas[d][sl]
                decay = jnp.where(incl[d], jnp.exp(gcol - grows[d][:, sl]), 0.0)
                m = jnp.where(strict[d], beta * kk * decay, 0.0)
                inst.append((half, d, qc, kc, vc, qk, gcol, beta, decay, m))
        ms = [t[-1] for t in inst]
        pws = [jnp.where(same_blk[0], -m, 0.0) for m in ms]
        tinvs = [eye + pw for pw in pws]
        for _ in range(2):
            pws = [_dot_inv(pw, pw) for pw in pws]
            tinvs = [t + _dot_inv(t, pw) for t, pw in zip(tinvs, pws)]
        for mask in off_masks:
            tmp = [_dot_inv(jnp.where(mask, m, 0.0), t) for m, t in zip(ms, tinvs)]
            tinvs = [t - _dot_inv(t, x) for t, x in zip(tinvs, tmp)]
        uws = [_dot1(tinv, jnp.concatenate([vc * beta, kc * (beta * jnp.exp(gcol))], axis=1)).astype(BF16)
               for (half, d, qc, kc, vc, qk, gcol, beta, decay, m), tinv in zip(inst, tinvs)]
        bas, ows = [], []
        for (half, d, qc, kc, vc, qk, gcol, beta, decay, m), uw in zip(inst, uws):
            glast = gcol[GDN_CHUNK - 1:GDN_CHUNK] if d == 0 else gcol[0:1]
            bas.append(_dg((kc * jnp.exp(glast - gcol)).astype(BF16), uw, _TN))
            ows.append(_dg((qk * decay).astype(BF16), uw, _NN))
        for (half, d, qc, kc, vc, qk, gcol, beta, decay, m), ba, ow in zip(inst, bas, ows):
            chunk0 = p * GDN_STEP + half * GDN_CHUNK
            o_s[d, pl.ds(pl.multiple_of(chunk0, GDN_CHUNK), GDN_CHUNK), :] = ow[:, :HEAD_DIM]
            b_s[d, pl.ds(pl.multiple_of(2 * chunk0, HEAD_DIM), HEAD_DIM), :] = ba[:, :HEAD_DIM]
            aq_s[d, pl.ds(pl.multiple_of(3 * chunk0, GDN_AQ), GDN_AQ), :] = jnp.concatenate(
                [ba[:, HEAD_DIM:], qc * jnp.exp(gcol) - ow[:, HEAD_DIM:]], axis=0).astype(BF16)
        return carry

    lax.fori_loop(0, UNIT // GDN_STEP, phase1, 0)

    lane1 = lax.broadcasted_iota(jnp.int32, (1, LANES), 1)

    def advance(chains, states):
        rs = [_dg(aq_s[d, pl.ds(pl.multiple_of(ch * GDN_AQ, GDN_AQ), GDN_AQ), :], s.astype(BF16), _NN)
              for (d, ch), s in zip(chains, states)]
        new = []
        for (d, ch), s, r in zip(chains, states, rs):
            tile0 = pl.multiple_of(ch * GDN_CHUNK + (GDN_CHUNK - 8 if d == 0 else 0), 8)
            last = col_ref[pl.ds(tile0, 8), :]
            last = last[7:8] if d == 0 else last[0:1]
            glast = jnp.sum(jnp.where(lane1 == d * HEADS + head, last, 0.0), axis=-1, keepdims=True)
            b = b_s[d, pl.ds(pl.multiple_of(ch * HEAD_DIM, HEAD_DIM), HEAD_DIM), :]
            new.append(s * jnp.exp(glast) + (b - r[:HEAD_DIM]))
            rows = pl.ds(pl.multiple_of(ch * GDN_CHUNK, GDN_CHUNK), GDN_CHUNK)
            o_s[d, rows, :] = o_s[d, rows, :] + r[HEAD_DIM:]
        return tuple(new)

    @pl.when(is_ctx)
    def _context_unit():
        def body(i, states):
            chains = [(d, sq * chunks_per_seq + (i if d == 0 else chunks_per_seq - 1 - i))
                      for sq in range(SEQ_PER_UNIT) for d in range(2)]
            return advance(chains, states)
        zero = jnp.zeros((HEAD_DIM, HEAD_DIM), F32)
        final = lax.fori_loop(0, chunks_per_seq, body, (zero,) * (2 * SEQ_PER_UNIT))
        for sq in range(SEQ_PER_UNIT):
            for d in range(2):
                s_ref[sq, state_slot, d] = final[2 * sq + d]

    @pl.when(jnp.logical_not(is_ctx))
    def _latent_unit():
        def body(i, states):
            return advance([(0, i), (1, n_chunks - 1 - i)], states)
        final = lax.fori_loop(0, n_chunks, body, (s0_ref[0], s0_ref[1]))
        for sq in range(SEQ_PER_UNIT):
            for d in range(2):
                s_ref[sq, state_slot, d] = final[d]

    o_ref[...] = _head_rmsnorm_gate(o_s[0] + o_s[1], g_ref[...], og_ref[...]).astype(o_ref.dtype)


def _gdn_call(z, col, rowt, conv_w, onorm_g, state, layer, new_states):
    def zcol(off):
        return pl.BlockSpec((None, UNIT, HEAD_DIM), lambda h, g: (off // HEAD_DIM + h, _scan_unit(g), 0))
    def wcol(part):
        return pl.BlockSpec((CONV_K * CONV_K, HEAD_DIM), lambda h, g: (0, part * HEADS + h))
    scr = lambda *shape: pltpu.VMEM(shape, F32)
    in_specs = [zcol(OFF_GQ), zcol(OFF_GK), zcol(OFF_GV), zcol(OFF_GG),
                pl.BlockSpec((UNIT, LANES), lambda h, g: (_scan_unit(g), 0)),
                pl.BlockSpec((LANES, UNIT), lambda h, g: (0, _scan_unit(g))),
                wcol(0), wcol(1), wcol(2),
                pl.BlockSpec((1, HEAD_DIM), lambda h, g: (0, 0)),
                _state_in_spec(layer)]
    args = [z, z, z, z, col, rowt, conv_w, conv_w, conv_w, onorm_g.reshape(1, HEAD_DIM), state]
    first = new_states is None
    kernel, aliases = functools.partial(_gdn_kernel, state_slot=layer if first else 0), {}
    if not first:
        kernel, aliases = _without_arg(kernel, len(args)), {len(args): 1}
        in_specs.append(pl.BlockSpec(memory_space=pl.ANY))
        args.append(new_states)
    return pl.pallas_call(
        kernel,
        grid=(HEADS, N_UNITS),
        in_specs=in_specs,
        out_specs=[pl.BlockSpec((None, UNIT, HEAD_DIM), lambda h, g: (h, _scan_unit(g), 0)), _state_out_spec(layer, first)],
        out_shape=[jax.ShapeDtypeStruct((HEADS, N_TOK, HEAD_DIM), BF16), _STATE_SHAPE],
        input_output_aliases=aliases,
        scratch_shapes=[scr(UNIT + 2 * CONV_PAD, HEAD_DIM),
                        scr(UNIT, HEAD_DIM), scr(UNIT, HEAD_DIM), scr(UNIT, HEAD_DIM),
                        scr(2, UNIT, HEAD_DIM),
                        scr(2, (UNIT // GDN_CHUNK) * HEAD_DIM, HEAD_DIM),
                        pltpu.VMEM((2, (UNIT // GDN_CHUNK) * GDN_AQ, HEAD_DIM), BF16)],
        compiler_params=_cparams(("arbitrary", "arbitrary")),
        name="gdn_scan",
    )(*args)


def kernel(x_prompt, x_sample, c, state_hgrn, state_gdn, c_ctx, norm1_g, norm2_g, w_mod, b_mod, w_in, hg_lb, hg_onorm_g, cm_vnorm_g, cm_ws, cm_bs, gdn_conv, gdn_A_log, gdn_dt_bias, gdn_onorm_g, w_br_hg, w_br_cm, w_br_gdn, w_out, w_ff1, w_ff2, final_g):
    x = (x_prompt.reshape(N_CTX_TOK, D_MODEL), x_sample.reshape(N_LAT_TOK, D_MODEL))
    cvec = jnp.concatenate([c_ctx[None, :], c, jnp.zeros((MOD_ROWS - 1 - DEC_BATCH, D_MODEL), F32)], axis=0)
    mod3 = _mod_call(cvec, w_mod, b_mod).reshape(DEPTH * MOD_ROWS, 1, 6 * D_MODEL)

    lb_all = jnp.cumsum(jax.nn.softmax(hg_lb.astype(F32), axis=0), axis=0)
    lb_all = lb_all - lb_all[:1]

    w_in_t = jnp.swapaxes(w_in, 1, 2)
    w_ff2_bf = w_ff2.astype(BF16)

    new_hg = new_gdn = None
    h = _normmod_call(x, norm1_g[0], mod3, 0, 0, 1)
    for l in range(DEPTH):
        z = _mm_call(h, w_in_t, l, col0=0, n=N_MAIN, act=None, out_dtype=F32, tm=1024, tn=1024,
                     w_transposed=True, lane_tile_major=True, name="in_proj")
        ab = _mm_call(h, w_in_t, l, col0=OFF_AB, n=LANES, act=None, out_dtype=F32, tm=1024, tn=LANES,
                      w_transposed=True, name="in_proj_ab")
        o_a, new_hg = _hgrn_call(z, lb_all[l], hg_onorm_g[l], state_hgrn, l, new_hg)
        o_b = _gmlp_call(z, cm_vnorm_g[l], cm_ws[l], cm_bs[l].T)
        col, rowt = _gdn_gates_call(ab, gdn_A_log[l], gdn_dt_bias[l])
        o_c, new_gdn = _gdn_call(z, col, rowt, gdn_conv[l].reshape(CONV_K * CONV_K, 3 * HG_F),
                                 gdn_onorm_g[l], state_gdn, l, new_gdn)

        merged = _gated_merge_call(h, w_in_t, o_a, o_b, o_c, w_br_hg, w_br_cm, w_br_gdn, l)
        x, h2 = _outproj_norm_call(merged, w_out, x, norm2_g[l], mod3, l)
        up = _mm_call(h2, w_ff1, l, col0=0, n=D_FF, act="relu2", out_dtype=BF16, tm=2048, tn=1024, name="ffn_up")
        if l + 1 < DEPTH:
            x, h = _ffn_down_norm_call(up, w_ff2_bf, x, mod3, l, norm1_g[l + 1])
        else:
            x = _mm_resid_call(up, w_ff2_bf, x, mod3, l, 5, tm=512, tn=512, name="ffn_down")

    y_prompt = _final_norm_call(x, final_g, 0, N_CTX_TOK).reshape(BATCH, SEQ, D_MODEL)
    y_sample = _final_norm_call(x, final_g, N_CTX_TOK, N_LAT_TOK).reshape(DEC_BATCH, DEC_SEQ, D_MODEL)
    return (y_prompt, y_sample, new_hg, new_gdn)
```

```python
import functools

import jax
import jax.numpy as jnp
from jax import lax
from jax.experimental import pallas as pl
from jax.experimental.pallas import tpu as pltpu

F32 = jnp.float32
BF16 = jnp.bfloat16

D_MODEL = 2048
BATCH = 16
SEQ = 256
DEPTH = 2
DEC_BATCH = 4
DEC_SEQ = 1024
GRID_W = 64
EPS = 1e-6
D_FF = 4 * D_MODEL
HEADS = 8
HEAD_DIM = 128
HG_F = HEADS * HEAD_DIM
CM_GROUPS = 8
CM_W = CM_GROUPS * HEAD_DIM
CM_CHUNK = 128
GDN_CHUNK = 64
CONV_K = 3

OFF_HQ, OFF_HI, OFF_HG, OFF_HFF, OFF_HFB = 0, 1024, 2048, 3072, 4096
OFF_CU, OFF_CV = 5120, 6144
OFF_GQ, OFF_GK, OFF_GV, OFF_GG = 7168, 8192, 9216, 10240
OFF_AB = 11264
OFF_GATES = 11296
IN_DIM = 17440
N_MAIN = OFF_AB

N_CTX_TOK = BATCH * SEQ
N_LAT_TOK = DEC_BATCH * DEC_SEQ
N_TOK = N_CTX_TOK + N_LAT_TOK
UNIT = DEC_SEQ
N_CTX_UNITS = N_CTX_TOK // UNIT
N_UNITS = N_TOK // UNIT
N_LAT_UNITS = N_UNITS - N_CTX_UNITS
SEQ_PER_UNIT = UNIT // SEQ

LANES = 128
MOD_ROWS = 8
HG_BLOCK = 256
HG_SUB = 32
HG_INST = 2 * (UNIT // HG_BLOCK)
HG_HEADS_PER_STEP = 2
VMEM_LIMIT = 56 * 1024 * 1024

_NT = (((1,), (1,)), ((), ()))
_TN = (((0,), (0,)), ((), ()))
_NN = (((1,), (0,)), ((), ()))


def _dg(a, b, dims):
    return lax.dot_general(a, b, dims, preferred_element_type=F32)


def _split2(x):
    hi = x.astype(BF16)
    lo = (x - hi.astype(F32)).astype(BF16)
    return hi, lo


def _dot1(a, b, dims=_NN):
    return _dg(a.astype(BF16), b.astype(BF16), dims)


def _dot3(a, b, dims=_NN):
    ah, al = _split2(a)
    bh, bl = _split2(b)
    return _dg(ah, bh, dims) + (_dg(ah, bl, dims) + _dg(al, bh, dims))


_dot_inv = _dot1


def _dot01(m, x):
    hi = x.astype(BF16)
    r = x - hi.astype(F32)
    mid = r.astype(BF16)
    lo = (r - mid.astype(F32)).astype(BF16)
    return _dg(m, hi, _NN) + (_dg(m, mid, _NN) + _dg(m, lo, _NN))


def _dot01_2(m, x):
    hi, lo = _split2(x)
    return _dg(m, hi, _NN) + _dg(m, lo, _NN)


def _sigmoid(x):
    return 1.0 / (1.0 + jnp.exp(-x))


def _silu(x):
    return x * _sigmoid(x)


def _gelu(x):
    return 0.5 * x * (1.0 + lax.erf(x * (2.0 ** -0.5)))


def _softplus(x):
    return jnp.maximum(x, 0.0) + jnp.log1p(jnp.exp(-jnp.abs(x)))


def _mod_row_of_tile(i, tm):
    return jnp.maximum(0, (i * tm - N_CTX_TOK) // DEC_SEQ + 1)


assert N_LAT_UNITS == N_CTX_UNITS


def _scan_unit(g):
    return jnp.where(g % 2 == 0, N_CTX_UNITS + g // 2, g // 2)


def _is_ctx_step(g):
    return g % 2 == 1


def _state_out_spec(layer, first, heads_per_step=None):
    if first:
        return pl.BlockSpec((SEQ_PER_UNIT, DEPTH, 2, heads_per_step, HEAD_DIM, HEAD_DIM),
                            lambda h, g: (g // 2, 0, 0, h, 0, 0))
    return pl.BlockSpec((SEQ_PER_UNIT, 1, 2, heads_per_step, HEAD_DIM, HEAD_DIM),
                        lambda h, g: (g // 2, layer, 0, h, 0, 0))


def _zero_other_slots(s_ref, slot):
    for other in range(s_ref.shape[1]):
        if other != slot:
            s_ref[:, other] = jnp.zeros((s_ref.shape[0],) + tuple(s_ref.shape[2:]), F32)


def _state_in_spec(layer, heads_per_step=None):
    return pl.BlockSpec((None, None, 2, heads_per_step, HEAD_DIM, HEAD_DIM),
                        lambda h, g: (g // 2, layer, 0, h, 0, 0))


_STATE_SHAPE = jax.ShapeDtypeStruct((BATCH, DEPTH, 2, HEADS, HEAD_DIM, HEAD_DIM), F32)


def _without_arg(kernel, pos):
    def wrapped(*refs):
        return kernel(*refs[:pos], *refs[pos + 1:])
    return wrapped


def _cparams(sem):
    return pltpu.CompilerParams(dimension_semantics=sem, vmem_limit_bytes=VMEM_LIMIT)


def _mod_kernel(c_ref, w_ref, b_ref, o_ref):
    s = _silu(c_ref[...])
    hi, lo = _split2(s)
    w = w_ref[...].astype(BF16)
    o_ref[...] = _dg(hi, w, _NN) + _dg(lo, w, _NN) + b_ref[...]


def _mod_call(cvec, w_mod, b_mod):
    tn = 2048
    n = 6 * D_MODEL
    return pl.pallas_call(
        _mod_kernel,
        grid=(DEPTH, n // tn),
        in_specs=[pl.BlockSpec((MOD_ROWS, D_MODEL), lambda l, j: (0, 0)),
                  pl.BlockSpec((None, D_MODEL, tn), lambda l, j: (l, 0, j)),
                  pl.BlockSpec((None, 1, tn), lambda l, j: (l, 0, j))],
        out_specs=pl.BlockSpec((None, MOD_ROWS, tn), lambda l, j: (l, 0, j)),
        out_shape=jax.ShapeDtypeStruct((DEPTH, MOD_ROWS, n), F32),
        compiler_params=_cparams(("arbitrary", "arbitrary")),
        name="modulation",
    )(cvec, w_mod, b_mod.reshape(DEPTH, 1, n))


def _adaln(x, g, shift, scale):
    y = x * lax.rsqrt(jnp.mean(x * x, axis=-1, keepdims=True) + EPS) * g
    return y * (1.0 + scale) + shift


def _residual_specs(x, tm, idx):
    if not isinstance(x, tuple):
        return [pl.BlockSpec((tm, D_MODEL), lambda *g: (idx(*g), 0))], [x]
    n_ctx = N_CTX_TOK // tm
    return ([pl.BlockSpec((tm, D_MODEL), lambda *g: (jnp.minimum(idx(*g), n_ctx - 1), 0)),
             pl.BlockSpec((tm, D_MODEL), lambda *g: (jnp.maximum(idx(*g) - n_ctx, 0), 0))], list(x))


def _residual_tile(refs, tile, tm):
    if len(refs) == 1:
        return refs[0][...]
    return jnp.where(tile < N_CTX_TOK // tm, refs[0][...], refs[1][...])


def _normmod_kernel(*refs, n_x):
    x_refs, (g_ref, sh_ref, sc_ref, o_ref) = refs[:n_x], refs[n_x:]
    x = _residual_tile(x_refs, pl.program_id(0), o_ref.shape[0])
    o_ref[...] = _adaln(x, g_ref[...], sh_ref[...], sc_ref[...]).astype(o_ref.dtype)


def _normmod_call(x, g, mod3, layer, k_shift, k_scale):
    tm = 512
    def mod_spec(k):
        return pl.BlockSpec((None, 1, D_MODEL),
                            lambda i: (layer * MOD_ROWS + _mod_row_of_tile(i, tm), 0, k))
    x_specs, x_args = _residual_specs(x, tm, lambda i: i)
    return pl.pallas_call(
        functools.partial(_normmod_kernel, n_x=len(x_args)),
        grid=(N_TOK // tm,),
        in_specs=x_specs + [pl.BlockSpec((1, D_MODEL), lambda i: (0, 0)), mod_spec(k_shift), mod_spec(k_scale)],
        out_specs=pl.BlockSpec((tm, D_MODEL), lambda i: (i, 0)),
        out_shape=jax.ShapeDtypeStruct((N_TOK, D_MODEL), BF16),
        compiler_params=_cparams(("arbitrary",)),
        name="norm_mod",
    )(*x_args, g.reshape(1, D_MODEL), mod3, mod3)


def _final_norm_kernel(x_ref, g_ref, o_ref):
    x = x_ref[...]
    o_ref[...] = x * lax.rsqrt(jnp.mean(x * x, axis=-1, keepdims=True) + EPS) * g_ref[...]


def _final_norm_call(x, g, row0, n_rows):
    tm = 512
    return pl.pallas_call(
        _final_norm_kernel,
        grid=(n_rows // tm,),
        in_specs=[pl.BlockSpec((tm, D_MODEL), lambda i: (row0 // tm + i, 0)),
                  pl.BlockSpec((1, D_MODEL), lambda i: (0, 0))],
        out_specs=pl.BlockSpec((tm, D_MODEL), lambda i: (i, 0)),
        out_shape=jax.ShapeDtypeStruct((n_rows, D_MODEL), F32),
        compiler_params=_cparams(("arbitrary",)),
        name="final_norm",
    )(x, g.reshape(1, D_MODEL))


def _mm_kernel(x_ref, w_ref, o_ref, wbf_ref, *, act, w_transposed):
    @pl.when(pl.program_id(1) == 0)
    def _cast_weights():
        wbf_ref[...] = w_ref[...].reshape(wbf_ref.shape).astype(BF16)

    acc = lax.dot_general(x_ref[...], wbf_ref[...], _NT if w_transposed else _NN, preferred_element_type=F32)
    if act == "relu2":
        acc = jnp.square(jnp.maximum(acc, 0.0))
    if len(o_ref.shape) == 3:
        for c in range(o_ref.shape[0]):
            o_ref[c] = acc[:, c * LANES:(c + 1) * LANES].astype(o_ref.dtype)
    else:
        o_ref[...] = acc.astype(o_ref.dtype)


def _mm_call(x, w_all, layer, *, col0, n, act, out_dtype, tm, tn, name, w_transposed=False,
             lane_tile_major=False):
    m, k = x.shape
    assert n % tn == 0 and m % tm == 0
    if w_transposed:
        assert col0 % 8 == 0
        w_spec = pl.BlockSpec((pl.Element(1), pl.Element(tn), pl.Element(k)),
                              lambda j, i: (layer, pl.multiple_of(col0 + j * tn, 8), 0))
        w_scratch = pltpu.VMEM((tn, k), BF16)
    else:
        assert col0 % tn == 0
        w_spec = pl.BlockSpec((None, k, tn), lambda j, i: (layer, 0, col0 // tn + j))
        w_scratch = pltpu.VMEM((k, tn), BF16)
    if lane_tile_major:
        out_spec = pl.BlockSpec((tn // LANES, tm, LANES), lambda j, i: (j, i, 0))
        out_shape = jax.ShapeDtypeStruct((n // LANES, m, LANES), out_dtype)
    else:
        out_spec = pl.BlockSpec((tm, tn), lambda j, i: (i, j))
        out_shape = jax.ShapeDtypeStruct((m, n), out_dtype)
    return pl.pallas_call(
        functools.partial(_mm_kernel, act=act, w_transposed=w_transposed),
        grid=(n // tn, m // tm),
        in_specs=[pl.BlockSpec((tm, k), lambda j, i: (i, 0)), w_spec],
        out_specs=out_spec,
        out_shape=out_shape,
        scratch_shapes=[w_scratch],
        compiler_params=_cparams(("arbitrary", "arbitrary")),
        name=name,
    )(x, w_all)


def _mm_bf16_kernel(x_ref, w_ref, r_ref, g_ref, o_ref):
    acc = jnp.dot(x_ref[...], w_ref[...], preferred_element_type=F32)
    o_ref[...] = r_ref[...] + g_ref[...] * acc


def _mm_resid_call(x, w, resid, mod3, layer, k_gate, *, tm, tn, name):
    m, k = x.shape
    n = w.shape[2]
    per_tile = D_MODEL // tn
    return pl.pallas_call(
        _mm_bf16_kernel,
        grid=(m // tm, n // tn),
        in_specs=[pl.BlockSpec((tm, k), lambda i, j: (i, 0)),
                  pl.BlockSpec((None, k, tn), lambda i, j: (layer, 0, j)),
                  pl.BlockSpec((tm, tn), lambda i, j: (i, j)),
                  pl.BlockSpec((None, 1, tn),
                               lambda i, j: (layer * MOD_ROWS + _mod_row_of_tile(i, tm), 0,
                                             k_gate * per_tile + j))],
        out_specs=pl.BlockSpec((tm, tn), lambda i, j: (i, j)),
        out_shape=jax.ShapeDtypeStruct((m, n), F32),
        compiler_params=_cparams(("arbitrary", "arbitrary")),
        name=name,
    )(x, w, resid, mod3)


def _outproj_norm_kernel(*refs, n_x):
    m_ref, w_ref = refs[:2]
    x_refs = refs[2:2 + n_x]
    gate_ref, g_ref, sh_ref, sc_ref, xo_ref, h_ref, wbf_ref = refs[2 + n_x:]

    @pl.when(pl.program_id(0) == 0)
    def _cast_weights():
        wbf_ref[...] = w_ref[...].astype(BF16)

    acc = jnp.dot(m_ref[...], wbf_ref[...], preferred_element_type=F32)
    xn = _residual_tile(x_refs, pl.program_id(0), xo_ref.shape[0]) + gate_ref[...] * acc
    xo_ref[...] = xn
    h_ref[...] = _adaln(xn, g_ref[...], sh_ref[...], sc_ref[...]).astype(h_ref.dtype)


def _outproj_norm_call(merged, w_out, x, norm_g, mod3, layer):
    tm = 256 if isinstance(x, tuple) else 512
    def mod_spec(k):
        return pl.BlockSpec((None, 1, D_MODEL),
                            lambda i: (layer * MOD_ROWS + _mod_row_of_tile(i, tm), 0, k))
    row = pl.BlockSpec((tm, D_MODEL), lambda i: (i, 0))
    x_specs, x_args = _residual_specs(x, tm, lambda i: i)
    return pl.pallas_call(
        functools.partial(_outproj_norm_kernel, n_x=len(x_args)),
        grid=(N_TOK // tm,),
        in_specs=[row,
                  pl.BlockSpec((None, D_MODEL, D_MODEL), lambda i: (layer, 0, 0), pipeline_mode=pl.Buffered(1))]
                 + x_specs + [mod_spec(2), pl.BlockSpec((1, D_MODEL), lambda i: (0, 0)), mod_spec(3), mod_spec(4)],
        out_specs=[row, row],
        out_shape=[jax.ShapeDtypeStruct((N_TOK, D_MODEL), F32),
                   jax.ShapeDtypeStruct((N_TOK, D_MODEL), BF16)],
        scratch_shapes=[pltpu.VMEM((D_MODEL, D_MODEL), BF16)],
        compiler_params=_cparams(("arbitrary",)),
        name="out_proj_norm",
    )(merged, w_out, *x_args, mod3, norm_g.reshape(1, D_MODEL), mod3, mod3)


def _ffn_down_norm_kernel(x_ref, w_ref, r_ref, gate_ref, g_ref, sh_ref, sc_ref, o_ref, h_ref, row_ref):
    j = pl.program_id(1)
    tn = o_ref.shape[1]
    acc = jnp.dot(x_ref[...], w_ref[...], preferred_element_type=F32)
    xn = r_ref[...] + gate_ref[...] * acc
    o_ref[...] = xn
    row_ref[:, pl.ds(pl.multiple_of(j * tn, tn), tn)] = xn

    @pl.when(j == pl.num_programs(1) - 1)
    def _next_layer_norm():
        h_ref[...] = _adaln(row_ref[...], g_ref[...], sh_ref[...], sc_ref[...]).astype(h_ref.dtype)


def _ffn_down_norm_call(up, w_bf, resid, mod3, layer, next_norm_g):
    tm, tn = 512, 512
    m, k = up.shape
    per_tile = D_MODEL // tn
    def mod_row(i):
        return _mod_row_of_tile(i, tm)
    def next_mod(kk):
        return pl.BlockSpec((None, 1, D_MODEL), lambda i, j: ((layer + 1) * MOD_ROWS + mod_row(i), 0, kk))
    return pl.pallas_call(
        _ffn_down_norm_kernel,
        grid=(m // tm, D_MODEL // tn),
        in_specs=[pl.BlockSpec((tm, k), lambda i, j: (i, 0)),
                  pl.BlockSpec((None, k, tn), lambda i, j: (layer, 0, j)),
                  pl.BlockSpec((tm, tn), lambda i, j: (i, j)),
                  pl.BlockSpec((None, 1, tn), lambda i, j: (layer * MOD_ROWS + mod_row(i), 0, 5 * per_tile + j)),
                  pl.BlockSpec((1, D_MODEL), lambda i, j: (0, 0)), next_mod(0), next_mod(1)],
        out_specs=[pl.BlockSpec((tm, tn), lambda i, j: (i, j)),
                   pl.BlockSpec((tm, D_MODEL), lambda i, j: (i, 0))],
        out_shape=[jax.ShapeDtypeStruct((m, D_MODEL), F32),
                   jax.ShapeDtypeStruct((m, D_MODEL), BF16)],
        scratch_shapes=[pltpu.VMEM((tm, D_MODEL), F32)],
        compiler_params=_cparams(("arbitrary", "arbitrary")),
        name="ffn_down_norm",
    )(up, w_bf, resid, mod3, next_norm_g.reshape(1, D_MODEL), mod3, mod3)


def _gated_merge_kernel(h_ref, wga_ref, wgb_ref, wgc_ref, oa_ref, ob_ref, oc_ref, wa_ref, wb_ref, wc_ref,
                        o_ref, wg_bf, wbr_bf):
    @pl.when(pl.program_id(1) == 0)
    def _cast_weights():
        for t, ref in enumerate((wga_ref, wgb_ref, wgc_ref)):
            wg_bf[t] = ref[...].reshape(wg_bf.shape[1:]).astype(BF16)
        for t, ref in enumerate((wa_ref, wb_ref, wc_ref)):
            wbr_bf[t] = ref[...].astype(BF16)

    def rows_of(ref):
        return jnp.concatenate([ref[t] for t in range(ref.shape[0])], axis=1)

    h = h_ref[...]
    acc = None
    for t, br_ref in enumerate((oa_ref, ob_ref, oc_ref)):
        gate = _sigmoid(lax.dot_general(h, wg_bf[t], _NT, preferred_element_type=F32))
        term = gate * jnp.dot(rows_of(br_ref), wbr_bf[t], preferred_element_type=F32)
        acc = term if acc is None else acc + term
    o_ref[...] = acc.astype(o_ref.dtype)


def _gated_merge_call(h, w_in_t, o_a, o_b, o_c, w_a, w_b, w_c, layer):
    tm, tn = 1024, 256
    def gate_w(b):
        return pl.BlockSpec((pl.Element(1), pl.Element(tn), pl.Element(D_MODEL)),
                            lambda j, i: (layer, pl.multiple_of(OFF_GATES + b * D_MODEL + j * tn, 8), 0))
    br = pl.BlockSpec((HEADS, tm, HEAD_DIM), lambda j, i: (0, i, 0))
    wt = pl.BlockSpec((None, HG_F, tn), lambda j, i: (layer, 0, j))
    return pl.pallas_call(
        _gated_merge_kernel,
        grid=(D_MODEL // tn, N_TOK // tm),
        in_specs=[pl.BlockSpec((tm, D_MODEL), lambda j, i: (i, 0)), gate_w(0), gate_w(1), gate_w(2),
                  br, br, br, wt, wt, wt],
        out_specs=pl.BlockSpec((tm, tn), lambda j, i: (i, j)),
        out_shape=jax.ShapeDtypeStruct((N_TOK, D_MODEL), BF16),
        scratch_shapes=[pltpu.VMEM((3, tn, D_MODEL), BF16), pltpu.VMEM((3, HG_F, tn), BF16)],
        compiler_params=_cparams(("arbitrary", "arbitrary")),
        name="gated_merge",
    )(h, w_in_t, w_in_t, w_in_t, o_a, o_b, o_c, w_a, w_b, w_c)


def _head_rmsnorm_gate(o, g, og):
    y = o * lax.rsqrt(jnp.mean(o * o, axis=-1, keepdims=True) + EPS) * g
    return y * _silu(og)


def _hgrn_kernel(q_ref, i_ref, og_ref, ff_ref, fb_ref, lb_ref, g_ref, s0_ref, o_ref, s_ref,
                 acc_ref, b_ref, stc_ref, km_ref, *, state_slot):
    is_ctx = _is_ctx_step(pl.program_id(1))
    _zero_other_slots(s_ref, state_slot)
    nblk = UNIT // HG_BLOCK
    nsub = HG_BLOCK // HG_SUB
    r = lax.broadcasted_iota(jnp.int32, (HG_BLOCK, HG_BLOCK), 0)
    c = lax.broadcasted_iota(jnp.int32, (HG_BLOCK, HG_BLOCK), 1)
    same = (r // HG_SUB) == (c // HG_SUB)
    row_sub = lax.broadcasted_iota(jnp.int32, (HG_BLOCK, HEAD_DIM), 0) // HG_SUB
    tri = (jnp.logical_and(same, c <= r), jnp.logical_and(same, c >= r))
    tri_bf = [jnp.where(t, 1.0, 0.0).astype(BF16) for t in tri]
    z_refs = (ff_ref, fb_ref)
    inst = [(hh, blk, d) for hh in range(HG_HEADS_PER_STEP) for blk in range(nblk) for d in range(2)]

    def expand(x):
        return jnp.broadcast_to(x[:, None, :], (nsub, HG_SUB, HEAD_DIM)).reshape(HG_BLOCK, HEAD_DIM)

    def blk_rows(blk):
        return slice(blk * HG_BLOCK, (blk + 1) * HG_BLOCK)

    def head_cols(hh):
        return slice(hh * HEAD_DIM, (hh + 1) * HEAD_DIM)

    qs = {(hh, blk): _silu(q_ref[hh, blk_rows(blk), :]) for hh in range(HG_HEADS_PER_STEP) for blk in range(nblk)}
    vs = {(hh, blk): i_ref[hh, blk_rows(blk), :] for hh in range(HG_HEADS_PER_STEP) for blk in range(nblk)}
    ks, lfs = [], []
    for hh, blk, d in inst:
        lb = lb_ref[d:d + 1, head_cols(hh)]
        f = lb + (1.0 - lb) * _sigmoid(z_refs[d][hh, blk_rows(blk), :])
        lfs.append(jnp.log(f))
        ks.append(1.0 - f)
    bs = [_dot01_2(tri_bf[d], lf) for (hh, blk, d), lf in zip(inst, lfs)]
    tots, qts, kts, qds = [], [], [], []
    for n, (hh, blk, d) in enumerate(inst):
        b = bs[n]
        b_ref[n] = b
        tot = b_ref[n, pl.ds((HG_SUB - 1) if d == 0 else 0, nsub, stride=HG_SUB), :]
        mid_f = expand(b_ref[n, pl.ds(HG_SUB // 2, nsub, stride=HG_SUB), :])
        tots.append(tot)
        qts.append(qs[hh, blk] * jnp.exp(b - mid_f))
        kts.append(ks[n] * jnp.exp(mid_f - b))
        qds.append((qs[hh, blk] * jnp.exp(b)).astype(BF16))
        kd = ks[n] * jnp.exp(expand(tot) - b)
        for s in range(nsub):
            km_ref[n, :, s * HEAD_DIM:(s + 1) * HEAD_DIM] = jnp.where(row_sub == s, kd, 0.0).astype(BF16)
    scs = [jnp.where(tri[d], _dot1(qt, kt, _NT), 0.0) for (hh, blk, d), qt, kt in zip(inst, qts, kts)]
    uts = [_dg(vs[hh, blk].astype(BF16), km_ref[n], _TN) for n, (hh, blk, d) in enumerate(inst)]
    outs = [_dot1(sc, vs[hh, blk]) for (hh, blk, d), sc in zip(inst, scs)]
    for hh in range(HG_HEADS_PER_STEP):
        for d in range(2):
            st = s0_ref[d, hh].T
            for blk in (range(nblk) if d == 0 else range(nblk - 1, -1, -1)):
                n = inst.index((hh, blk, d))
                st = jnp.where(is_ctx, 0.0, st)
                for s in (range(nsub) if d == 0 else range(nsub - 1, -1, -1)):
                    stc_ref[n, s] = st.T.astype(BF16)
                    st = st * jnp.exp(tots[n][s:s + 1, :]) + uts[n][:, s * HEAD_DIM:(s + 1) * HEAD_DIM]
                s_ref[blk, state_slot, d, hh] = st.T
    for n, (hh, blk, d) in enumerate(inst):
        o_int = [_dg(qds[n][s * HG_SUB:(s + 1) * HG_SUB], stc_ref[n, s], _NN) for s in range(nsub)]
        acc_ref[d, hh, blk_rows(blk), :] = outs[n] + jnp.concatenate(o_int, axis=0)
    for hh in range(HG_HEADS_PER_STEP):
        o_ref[hh] = _head_rmsnorm_gate(acc_ref[0, hh] + acc_ref[1, hh], g_ref[...], og_ref[hh]).astype(o_ref.dtype)


def _hgrn_call(z, lb, onorm_g, state, layer, new_states):
    nh = HG_HEADS_PER_STEP
    def col(off):
        return pl.BlockSpec((nh, UNIT, HEAD_DIM), lambda h, g: (off // HEAD_DIM // nh + h, _scan_unit(g), 0))
    in_specs = [col(OFF_HQ), col(OFF_HI), col(OFF_HG), col(OFF_HFF), col(OFF_HFB),
                pl.BlockSpec((2, nh * HEAD_DIM), lambda h, g: (0, h)),
                pl.BlockSpec((1, HEAD_DIM), lambda h, g: (0, 0)),
                _state_in_spec(layer, nh)]
    args = [z, z, z, z, z, lb, onorm_g.reshape(1, HEAD_DIM), state]
    first = new_states is None
    kernel, aliases = functools.partial(_hgrn_kernel, state_slot=layer if first else 0), {}
    if not first:
        kernel, aliases = _without_arg(kernel, len(args)), {len(args): 1}
        in_specs.append(pl.BlockSpec(memory_space=pl.ANY))
        args.append(new_states)
    n_inst = nh * HG_INST
    return pl.pallas_call(
        kernel,
        grid=(HEADS // nh, N_UNITS),
        in_specs=in_specs,
        out_specs=[pl.BlockSpec((nh, UNIT, HEAD_DIM), lambda h, g: (h, _scan_unit(g), 0)),
                   _state_out_spec(layer, first, nh)],
        out_shape=[jax.ShapeDtypeStruct((HEADS, N_TOK, HEAD_DIM), BF16), _STATE_SHAPE],
        input_output_aliases=aliases,
        scratch_shapes=[pltpu.VMEM((2, nh, UNIT, HEAD_DIM), F32),
                        pltpu.VMEM((n_inst, HG_BLOCK, HEAD_DIM), F32),
                        pltpu.VMEM((n_inst, HG_BLOCK // HG_SUB, HEAD_DIM, HEAD_DIM), BF16),
                        pltpu.VMEM((n_inst, HG_BLOCK, (HG_BLOCK // HG_SUB) * HEAD_DIM), BF16)],
        compiler_params=_cparams(("arbitrary", "arbitrary")),
        name="hgrn2_scan",
    )(*args)


def _gmlp_kernel(u_ref, v_ref, vn_ref, ws_ref, bs_ref, o_ref):
    tm = u_ref.shape[1]
    for g in range(CM_GROUPS):
        cols = slice(g * HEAD_DIM, (g + 1) * HEAD_DIM)
        vg = _gelu(v_ref[g])
        vg = vg * lax.rsqrt(jnp.mean(vg * vg, axis=-1, keepdims=True) + EPS) * vn_ref[:, cols]
        ug = _gelu(u_ref[g])
        w = ws_ref[g]
        bias = bs_ref[:, g:g + 1]
        for ch in range(tm // CM_CHUNK):
            rows = slice(ch * CM_CHUNK, (ch + 1) * CM_CHUNK)
            s = _dot3(w, vg[rows]) + bias
            o_ref[g, rows, :] = (ug[rows] * s).astype(o_ref.dtype)


def _gmlp_call(z, vnorm_g, ws, bs_t):
    tm = 512
    return pl.pallas_call(
        _gmlp_kernel,
        grid=(N_TOK // tm,),
        in_specs=[pl.BlockSpec((CM_GROUPS, tm, HEAD_DIM), lambda i: (OFF_CU // CM_W, i, 0)),
                  pl.BlockSpec((CM_GROUPS, tm, HEAD_DIM), lambda i: (OFF_CV // CM_W, i, 0)),
                  pl.BlockSpec((1, CM_W), lambda i: (0, 0)),
                  pl.BlockSpec((CM_GROUPS, CM_CHUNK, CM_CHUNK), lambda i: (0, 0, 0)),
                  pl.BlockSpec((CM_CHUNK, CM_GROUPS), lambda i: (0, 0))],
        out_specs=pl.BlockSpec((CM_GROUPS, tm, HEAD_DIM), lambda i: (0, i, 0)),
        out_shape=jax.ShapeDtypeStruct((CM_GROUPS, N_TOK, HEAD_DIM), BF16),
        compiler_params=_cparams(("arbitrary",)),
        name="chunk_gmlp",
    )(z, z, vnorm_g.reshape(1, CM_W), ws, bs_t)


GP_BLOCK = 256


def _gdn_gates_kernel(ab_ref, alog_ref, dt_ref, col_ref, rowt_ref):
    ab = ab_ref[...]
    lane = lax.broadcasted_iota(jnp.int32, ab.shape, 1)
    g = jnp.where(lane < 2 * HEADS, -jnp.exp(alog_ref[...]) * _softplus(ab + dt_ref[...]), 0.0)
    r = lax.broadcasted_iota(jnp.int32, (GP_BLOCK, GP_BLOCK), 0)
    c = lax.broadcasted_iota(jnp.int32, (GP_BLOCK, GP_BLOCK), 1)
    same = (r // GDN_CHUNK) == (c // GDN_CHUNK)
    tri_f = jnp.where(jnp.logical_and(same, c <= r), 1.0, 0.0).astype(BF16)
    tri_b = jnp.where(jnp.logical_and(same, c >= r), 1.0, 0.0).astype(BF16)
    cf = _dot01(tri_f, g)
    cb = _dot01(tri_b, g)
    col = jnp.where(lane < HEADS, cf, jnp.where(lane < 2 * HEADS, cb, _sigmoid(ab)))
    col_ref[...] = col
    rowt_ref[...] = col.T


def _gdn_gates_call(ab, a_log, dt_bias):
    pad = lambda t: jnp.pad(t.reshape(1, 2 * HEADS), ((0, 0), (0, LANES - 2 * HEADS)))
    return pl.pallas_call(
        _gdn_gates_kernel,
        grid=(N_TOK // GP_BLOCK,),
        in_specs=[pl.BlockSpec((GP_BLOCK, LANES), lambda i: (i, 0)),
                  pl.BlockSpec((1, LANES), lambda i: (0, 0)),
                  pl.BlockSpec((1, LANES), lambda i: (0, 0))],
        out_specs=[pl.BlockSpec((GP_BLOCK, LANES), lambda i: (i, 0)),
                   pl.BlockSpec((LANES, GP_BLOCK), lambda i: (0, i))],
        out_shape=[jax.ShapeDtypeStruct((N_TOK, LANES), F32),
                   jax.ShapeDtypeStruct((LANES, N_TOK), F32)],
        compiler_params=_cparams(("arbitrary",)),
        name="gdn_gates",
    )(ab, pad(a_log), pad(dt_bias))


CONV_PAD = 72
GDN_AQ = HEAD_DIM + GDN_CHUNK
GDN_STEP = 16 * GDN_CHUNK


def _gdn_kernel(qr_ref, kr_ref, vr_ref, og_ref, col_ref, rowt_ref, cwq_ref, cwk_ref, cwv_ref,
                g_ref, s0_ref, o_ref, s_ref,
                xp_ref, q_s, k_s, v_s, o_s, b_s, aq_s, *, state_slot):
    is_ctx = _is_ctx_step(pl.program_id(1))
    _zero_other_slots(s_ref, state_slot)
    head = pl.program_id(0)
    n_chunks = UNIT // GDN_CHUNK
    chunks_per_seq = SEQ // GDN_CHUNK

    t = lax.broadcasted_iota(jnp.int32, (UNIT, 1), 0)
    period = jnp.where(is_ctx, SEQ, GRID_W)
    pos = jnp.bitwise_and(t, period - 1)
    ok_left = pos != 0
    ok_right = pos != period - 1
    zeros_pad = jnp.zeros((CONV_PAD, HEAD_DIM), F32)
    xp_ref[0:CONV_PAD, :] = zeros_pad
    xp_ref[CONV_PAD + UNIT:CONV_PAD + UNIT + CONV_PAD, :] = zeros_pad

    def conv_silu(x_ref, w_ref):
        xp_ref[CONV_PAD:CONV_PAD + UNIT, :] = x_ref[...]
        acc = jnp.zeros((UNIT, HEAD_DIM), F32)
        for i in range(CONV_K):
            for j in range(CONV_K):
                w = w_ref[CONV_K * i + j:CONV_K * i + j + 1, :]
                if i != CONV_K // 2:
                    w = jnp.where(is_ctx, 0.0, w)
                start = CONV_PAD + (i - 1) * GRID_W + (j - 1)
                xs = xp_ref[start:start + UNIT, :]
                if j == 0:
                    xs = jnp.where(ok_left, xs, 0.0)
                elif j == CONV_K - 1:
                    xs = jnp.where(ok_right, xs, 0.0)
                acc = acc + xs * w
        return _silu(acc)

    def l2norm(x):
        return x * lax.rsqrt(jnp.sum(x * x, axis=-1, keepdims=True) + EPS)

    q_s[...] = l2norm(conv_silu(qr_ref, cwq_ref)) * (HEAD_DIM ** -0.5)
    k_s[...] = l2norm(conv_silu(kr_ref, cwk_ref))
    v_s[...] = conv_silu(vr_ref, cwv_ref)

    rr = lax.broadcasted_iota(jnp.int32, (GDN_CHUNK, GDN_CHUNK), 0)
    cc = lax.broadcasted_iota(jnp.int32, (GDN_CHUNK, GDN_CHUNK), 1)
    eye = jnp.where(rr == cc, 1.0, 0.0)
    same_blk = [(rr // b) == (cc // b) for b in (8, 16, 32, 64)]
    lane = lax.broadcasted_iota(jnp.int32, (GDN_STEP, LANES), 1)
    sub8 = lax.broadcasted_iota(jnp.int32, (HEADS, GDN_STEP), 0)

    def pick(x, j):
        return jnp.sum(jnp.where(lane[:x.shape[0]] == j, x, 0.0), axis=-1, keepdims=True)

    incl = (cc <= rr, cc >= rr)
    strict = (cc < rr, cc > rr)
    off_masks = [jnp.logical_and(same_blk[lvl], jnp.logical_not(same_blk[lvl - 1]))
                 for lvl in range(1, len(same_blk))]

    def phase1(p, carry):
        rows = pl.ds(pl.multiple_of(p * GDN_STEP, GDN_STEP), GDN_STEP)
        q2, k2, v2 = q_s[rows, :], k_s[rows, :], v_s[rows, :]
        col = col_ref[rows, :]
        gcols = [pick(col, d * HEADS + head) for d in range(2)]
        betas = [pick(col, (2 + d) * HEADS + head) for d in range(2)]
        grows = [jnp.sum(jnp.where(sub8 == head, rowt_ref[d * HEADS:(d + 1) * HEADS, rows], 0.0),
                         axis=0, keepdims=True) for d in range(2)]
        inst = []
        for half in range(GDN_STEP // GDN_CHUNK):
            sl = slice(half * GDN_CHUNK, (half + 1) * GDN_CHUNK)
            qc, kc, vc = q2[sl], k2[sl], v2[sl]
            kk = _dot1(kc, kc, _NT)
            qk = _dot1(qc, kc, _NT)
            for d in range(2):
                gcol, beta = gcols[d][sl], betas[d][sl]
                decay = jnp.where(incl[d], jnp.exp(gcol - grows[d][:, sl]), 0.0)
                m = jnp.where(strict[d], beta * kk * decay, 0.0)
                inst.append((half, d, qc, kc, vc, qk, gcol, beta, decay, m))
        ms = [t[-1] for t in inst]
        pws = [jnp.where(same_blk[0], -m, 0.0) for m in ms]
        tinvs = [eye + pw for pw in pws]
        for _ in range(2):
            pws = [_dot_inv(pw, pw) for pw in pws]
            tinvs = [t + _dot_inv(t, pw) for t, pw in zip(tinvs, pws)]
        for mask in off_masks:
            tmp = [_dot_inv(jnp.where(mask, m, 0.0), t) for m, t in zip(ms, tinvs)]
            tinvs = [t - _dot_inv(t, x) for t, x in zip(tinvs, tmp)]
        uws = [_dot1(tinv, jnp.concatenate([vc * beta, kc * (beta * jnp.exp(gcol))], axis=1)).astype(BF16)
               for (half, d, qc, kc, vc, qk, gcol, beta, decay, m), tinv in zip(inst, tinvs)]
        bas, ows = [], []
        for (half, d, qc, kc, vc, qk, gcol, beta, decay, m), uw in zip(inst, uws):
            glast = gcol[GDN_CHUNK - 1:GDN_CHUNK] if d == 0 else gcol[0:1]
            bas.append(_dg((kc * jnp.exp(glast - gcol)).astype(BF16), uw, _TN))
            ows.append(_dg((qk * decay).astype(BF16), uw, _NN))
        for (half, d, qc, kc, vc, qk, gcol, beta, decay, m), ba, ow in zip(inst, bas, ows):
            chunk0 = p * GDN_STEP + half * GDN_CHUNK
            o_s[d, pl.ds(pl.multiple_of(chunk0, GDN_CHUNK), GDN_CHUNK), :] = ow[:, :HEAD_DIM]
            b_s[d, pl.ds(pl.multiple_of(2 * chunk0, HEAD_DIM), HEAD_DIM), :] = ba[:, :HEAD_DIM]
            aq_s[d, pl.ds(pl.multiple_of(3 * chunk0, GDN_AQ), GDN_AQ), :] = jnp.concatenate(
                [ba[:, HEAD_DIM:], qc * jnp.exp(gcol) - ow[:, HEAD_DIM:]], axis=0).astype(BF16)
        return carry

    lax.fori_loop(0, UNIT // GDN_STEP, phase1, 0)

    lane1 = lax.broadcasted_iota(jnp.int32, (1, LANES), 1)

    def advance(chains, states):
        rs = [_dg(aq_s[d, pl.ds(pl.multiple_of(ch * GDN_AQ, GDN_AQ), GDN_AQ), :], s.astype(BF16), _NN)
              for (d, ch), s in zip(chains, states)]
        new = []
        for (d, ch), s, r in zip(chains, states, rs):
            tile0 = pl.multiple_of(ch * GDN_CHUNK + (GDN_CHUNK - 8 if d == 0 else 0), 8)
            last = col_ref[pl.ds(tile0, 8), :]
            last = last[7:8] if d == 0 else last[0:1]
            glast = jnp.sum(jnp.where(lane1 == d * HEADS + head, last, 0.0), axis=-1, keepdims=True)
            b = b_s[d, pl.ds(pl.multiple_of(ch * HEAD_DIM, HEAD_DIM), HEAD_DIM), :]
            new.append(s * jnp.exp(glast) + (b - r[:HEAD_DIM]))
            rows = pl.ds(pl.multiple_of(ch * GDN_CHUNK, GDN_CHUNK), GDN_CHUNK)
            o_s[d, rows, :] = o_s[d, rows, :] + r[HEAD_DIM:]
        return tuple(new)

    @pl.when(is_ctx)
    def _context_unit():
        def body(i, states):
            chains = [(d, sq * chunks_per_seq + (i if d == 0 else chunks_per_seq - 1 - i))
                      for sq in range(SEQ_PER_UNIT) for d in range(2)]
            return advance(chains, states)
        zero = jnp.zeros((HEAD_DIM, HEAD_DIM), F32)
        final = lax.fori_loop(0, chunks_per_seq, body, (zero,) * (2 * SEQ_PER_UNIT))
        for sq in range(SEQ_PER_UNIT):
            for d in range(2):
                s_ref[sq, state_slot, d] = final[2 * sq + d]

    @pl.when(jnp.logical_not(is_ctx))
    def _latent_unit():
        def body(i, states):
            return advance([(0, i), (1, n_chunks - 1 - i)], states)
        final = lax.fori_loop(0, n_chunks, body, (s0_ref[0], s0_ref[1]))
        for sq in range(SEQ_PER_UNIT):
            for d in range(2):
                s_ref[sq, state_slot, d] = final[d]

    o_ref[...] = _head_rmsnorm_gate(o_s[0] + o_s[1], g_ref[...], og_ref[...]).astype(o_ref.dtype)


def _gdn_call(z, col, rowt, conv_w, onorm_g, state, layer, new_states):
    def zcol(off):
        return pl.BlockSpec((None, UNIT, HEAD_DIM), lambda h, g: (off // HEAD_DIM + h, _scan_unit(g), 0))
    def wcol(part):
        return pl.BlockSpec((CONV_K * CONV_K, HEAD_DIM), lambda h, g: (0, part * HEADS + h))
    scr = lambda *shape: pltpu.VMEM(shape, F32)
    in_specs = [zcol(OFF_GQ), zcol(OFF_GK), zcol(OFF_GV), zcol(OFF_GG),
                pl.BlockSpec((UNIT, LANES), lambda h, g: (_scan_unit(g), 0)),
                pl.BlockSpec((LANES, UNIT), lambda h, g: (0, _scan_unit(g))),
                wcol(0), wcol(1), wcol(2),
                pl.BlockSpec((1, HEAD_DIM), lambda h, g: (0, 0)),
                _state_in_spec(layer)]
    args = [z, z, z, z, col, rowt, conv_w, conv_w, conv_w, onorm_g.reshape(1, HEAD_DIM), state]
    first = new_states is None
    kernel, aliases = functools.partial(_gdn_kernel, state_slot=layer if first else 0), {}
    if not first:
        kernel, aliases = _without_arg(kernel, len(args)), {len(args): 1}
        in_specs.append(pl.BlockSpec(memory_space=pl.ANY))
        args.append(new_states)
    return pl.pallas_call(
        kernel,
        grid=(HEADS, N_UNITS),
        in_specs=in_specs,
        out_specs=[pl.BlockSpec((None, UNIT, HEAD_DIM), lambda h, g: (h, _scan_unit(g), 0)), _state_out_spec(layer, first)],
        out_shape=[jax.ShapeDtypeStruct((HEADS, N_TOK, HEAD_DIM), BF16), _STATE_SHAPE],
        input_output_aliases=aliases,
        scratch_shapes=[scr(UNIT + 2 * CONV_PAD, HEAD_DIM),
                        scr(UNIT, HEAD_DIM), scr(UNIT, HEAD_DIM), scr(UNIT, HEAD_DIM),
                        scr(2, UNIT, HEAD_DIM),
                        scr(2, (UNIT // GDN_CHUNK) * HEAD_DIM, HEAD_DIM),
                        pltpu.VMEM((2, (UNIT // GDN_CHUNK) * GDN_AQ, HEAD_DIM), BF16)],
        compiler_params=_cparams(("arbitrary", "arbitrary")),
        name="gdn_scan",
    )(*args)


def kernel(x_prompt, x_sample, c, state_hgrn, state_gdn, c_ctx, norm1_g, norm2_g, w_mod, b_mod, w_in, hg_lb, hg_onorm_g, cm_vnorm_g, cm_ws, cm_bs, gdn_conv, gdn_A_log, gdn_dt_bias, gdn_onorm_g, w_br_hg, w_br_cm, w_br_gdn, w_out, w_ff1, w_ff2, final_g):
    x = (x_prompt.reshape(N_CTX_TOK, D_MODEL), x_sample.reshape(N_LAT_TOK, D_MODEL))
    cvec = jnp.concatenate([c_ctx[None, :], c, jnp.zeros((MOD_ROWS - 1 - DEC_BATCH, D_MODEL), F32)], axis=0)
    mod3 = _mod_call(cvec, w_mod, b_mod).reshape(DEPTH * MOD_ROWS, 1, 6 * D_MODEL)

    lb_all = jnp.cumsum(jax.nn.softmax(hg_lb.astype(F32), axis=0), axis=0)
    lb_all = lb_all - lb_all[:1]

    w_in_t = jnp.swapaxes(w_in, 1, 2)
    w_ff2_bf = w_ff2.astype(BF16)

    new_hg = new_gdn = None
    h = _normmod_call(x, norm1_g[0], mod3, 0, 0, 1)
    for l in range(DEPTH):
        z = _mm_call(h, w_in_t, l, col0=0, n=N_MAIN, act=None, out_dtype=F32, tm=1024, tn=1024,
                     w_transposed=True, lane_tile_major=True, name="in_proj")
        ab = _mm_call(h, w_in_t, l, col0=OFF_AB, n=LANES, act=None, out_dtype=F32, tm=1024, tn=LANES,
                      w_transposed=True, name="in_proj_ab")
        o_a, new_hg = _hgrn_call(z, lb_all[l], hg_onorm_g[l], state_hgrn, l, new_hg)
        o_b = _gmlp_call(z, cm_vnorm_g[l], cm_ws[l], cm_bs[l].T)
        col, rowt = _gdn_gates_call(ab, gdn_A_log[l], gdn_dt_bias[l])
        o_c, new_gdn = _gdn_call(z, col, rowt, gdn_conv[l].reshape(CONV_K * CONV_K, 3 * HG_F),
                                 gdn_onorm_g[l], state_gdn, l, new_gdn)

        merged = _gated_merge_call(h, w_in_t, o_a, o_b, o_c, w_br_hg, w_br_cm, w_br_gdn, l)
        x, h2 = _outproj_norm_call(merged, w_out, x, norm2_g[l], mod3, l)
        up = _mm_call(h2, w_ff1, l, col0=0, n=D_FF, act="relu2", out_dtype=BF16, tm=2048, tn=1024, name="ffn_up")
        if l + 1 < DEPTH:
            x, h = _ffn_down_norm_call(up, w_ff2_bf, x, mod3, l, norm1_g[l + 1])
        else:
            x = _mm_resid_call(up, w_ff2_bf, x, mod3, l, 5, tm=512, tn=512, name="ffn_down")

    y_prompt = _final_norm_call(x, final_g, 0, N_CTX_TOK).reshape(BATCH, SEQ, D_MODEL)
    y_sample = _final_norm_call(x, final_g, N_CTX_TOK, N_LAT_TOK).reshape(DEC_BATCH, DEC_SEQ, D_MODEL)
    return (y_prompt, y_sample, new_hg, new_gdn)
```

```python
import functools

import jax
import jax.numpy as jnp
from jax import lax
from jax.experimental import pallas as pl
from jax.experimental.pallas import tpu as pltpu

F32 = jnp.float32
BF16 = jnp.bfloat16

D_MODEL = 2048
BATCH = 16
SEQ = 256
DEPTH = 2
DEC_BATCH = 4
DEC_SEQ = 1024
GRID_W = 64
EPS = 1e-6
D_FF = 4 * D_MODEL
HEADS = 8
HEAD_DIM = 128
HG_F = HEADS * HEAD_DIM
CM_GROUPS = 8
CM_W = CM_GROUPS * HEAD_DIM
CM_CHUNK = 128
GDN_CHUNK = 64
CONV_K = 3

OFF_HQ, OFF_HI, OFF_HG, OFF_HFF, OFF_HFB = 0, 1024, 2048, 3072, 4096
OFF_CU, OFF_CV = 5120, 6144
OFF_GQ, OFF_GK, OFF_GV, OFF_GG = 7168, 8192, 9216, 10240
OFF_AB = 11264
OFF_GATES = 11296
IN_DIM = 17440
N_MAIN = OFF_AB

N_CTX_TOK = BATCH * SEQ
N_LAT_TOK = DEC_BATCH * DEC_SEQ
N_TOK = N_CTX_TOK + N_LAT_TOK
UNIT = DEC_SEQ
N_CTX_UNITS = N_CTX_TOK // UNIT
N_UNITS = N_TOK // UNIT
N_LAT_UNITS = N_UNITS - N_CTX_UNITS
SEQ_PER_UNIT = UNIT // SEQ

LANES = 128
MOD_ROWS = 8
HG_BLOCK = 256
HG_SUB = 32
HG_INST = 2 * (UNIT // HG_BLOCK)
HG_HEADS_PER_STEP = 2
VMEM_LIMIT = 56 * 1024 * 1024

_NT = (((1,), (1,)), ((), ()))
_TN = (((0,), (0,)), ((), ()))
_NN = (((1,), (0,)), ((), ()))


def _dg(a, b, dims):
    return lax.dot_general(a, b, dims, preferred_element_type=F32)


def _split2(x):
    hi = x.astype(BF16)
    lo = (x - hi.astype(F32)).astype(BF16)
    return hi, lo


def _dot1(a, b, dims=_NN):
    return _dg(a.astype(BF16), b.astype(BF16), dims)


def _dot3(a, b, dims=_NN):
    ah, al = _split2(a)
    bh, bl = _split2(b)
    return _dg(ah, bh, dims) + (_dg(ah, bl, dims) + _dg(al, bh, dims))


_dot_inv = _dot1


def _dot01(m, x):
    hi = x.astype(BF16)
    r = x - hi.astype(F32)
    mid = r.astype(BF16)
    lo = (r - mid.astype(F32)).astype(BF16)
    return _dg(m, hi, _NN) + (_dg(m, mid, _NN) + _dg(m, lo, _NN))


def _dot01_2(m, x):
    hi, lo = _split2(x)
    return _dg(m, hi, _NN) + _dg(m, lo, _NN)


def _sigmoid(x):
    return 1.0 / (1.0 + jnp.exp(-x))


def _silu(x):
    return x * _sigmoid(x)


def _gelu(x):
    return 0.5 * x * (1.0 + lax.erf(x * (2.0 ** -0.5)))


def _softplus(x):
    return jnp.maximum(x, 0.0) + jnp.log1p(jnp.exp(-jnp.abs(x)))


def _mod_row_of_tile(i, tm):
    return jnp.maximum(0, (i * tm - N_CTX_TOK) // DEC_SEQ + 1)


assert N_LAT_UNITS == N_CTX_UNITS


def _scan_unit(g):
    return jnp.where(g % 2 == 0, N_CTX_UNITS + g // 2, g // 2)


def _is_ctx_step(g):
    return g % 2 == 1


def _state_out_spec(layer, first, heads_per_step=None):
    if first:
        return pl.BlockSpec((SEQ_PER_UNIT, DEPTH, 2, heads_per_step, HEAD_DIM, HEAD_DIM),
                            lambda h, g: (g // 2, 0, 0, h, 0, 0))
    return pl.BlockSpec((SEQ_PER_UNIT, 1, 2, heads_per_step, HEAD_DIM, HEAD_DIM),
                        lambda h, g: (g // 2, layer, 0, h, 0, 0))


def _zero_other_slots(s_ref, slot):
    for other in range(s_ref.shape[1]):
        if other != slot:
            s_ref[:, other] = jnp.zeros((s_ref.shape[0],) + tuple(s_ref.shape[2:]), F32)


def _state_in_spec(layer, heads_per_step=None):
    return pl.BlockSpec((None, None, 2, heads_per_step, HEAD_DIM, HEAD_DIM),
                        lambda h, g: (g // 2, layer, 0, h, 0, 0))


_STATE_SHAPE = jax.ShapeDtypeStruct((BATCH, DEPTH, 2, HEADS, HEAD_DIM, HEAD_DIM), F32)


def _without_arg(kernel, pos):
    def wrapped(*refs):
        return kernel(*refs[:pos], *refs[pos + 1:])
    return wrapped


def _cparams(sem):
    return pltpu.CompilerParams(dimension_semantics=sem, vmem_limit_bytes=VMEM_LIMIT)


def _mod_kernel(c_ref, w_ref, b_ref, o_ref):
    s = _silu(c_ref[...])
    hi, lo = _split2(s)
    w = w_ref[...].astype(BF16)
    o_ref[...] = _dg(hi, w, _NN) + _dg(lo, w, _NN) + b_ref[...]


def _mod_call(cvec, w_mod, b_mod):
    tn = 2048
    n = 6 * D_MODEL
    return pl.pallas_call(
        _mod_kernel,
        grid=(DEPTH, n // tn),
        in_specs=[pl.BlockSpec((MOD_ROWS, D_MODEL), lambda l, j: (0, 0)),
                  pl.BlockSpec((None, D_MODEL, tn), lambda l, j: (l, 0, j)),
                  pl.BlockSpec((None, 1, tn), lambda l, j: (l, 0, j))],
        out_specs=pl.BlockSpec((None, MOD_ROWS, tn), lambda l, j: (l, 0, j)),
        out_shape=jax.ShapeDtypeStruct((DEPTH, MOD_ROWS, n), F32),
        compiler_params=_cparams(("arbitrary", "arbitrary")),
        name="modulation",
    )(cvec, w_mod, b_mod.reshape(DEPTH, 1, n))


def _adaln(x, g, shift, scale):
    y = x * lax.rsqrt(jnp.mean(x * x, axis=-1, keepdims=True) + EPS) * g
    return y * (1.0 + scale) + shift


def _residual_specs(x, tm, idx):
    if not isinstance(x, tuple):
        return [pl.BlockSpec((tm, D_MODEL), lambda *g: (idx(*g), 0))], [x]
    n_ctx = N_CTX_TOK // tm
    return ([pl.BlockSpec((tm, D_MODEL), lambda *g: (jnp.minimum(idx(*g), n_ctx - 1), 0)),
             pl.BlockSpec((tm, D_MODEL), lambda *g: (jnp.maximum(idx(*g) - n_ctx, 0), 0))], list(x))


def _residual_tile(refs, tile, tm):
    if len(refs) == 1:
        return refs[0][...]
    return jnp.where(tile < N_CTX_TOK // tm, refs[0][...], refs[1][...])


def _normmod_kernel(*refs, n_x):
    x_refs, (g_ref, sh_ref, sc_ref, o_ref) = refs[:n_x], refs[n_x:]
    x = _residual_tile(x_refs, pl.program_id(0), o_ref.shape[0])
    o_ref[...] = _adaln(x, g_ref[...], sh_ref[...], sc_ref[...]).astype(o_ref.dtype)


def _normmod_call(x, g, mod3, layer, k_shift, k_scale):
    tm = 512
    def mod_spec(k):
        return pl.BlockSpec((None, 1, D_MODEL),
                            lambda i: (layer * MOD_ROWS + _mod_row_of_tile(i, tm), 0, k))
    x_specs, x_args = _residual_specs(x, tm, lambda i: i)
    return pl.pallas_call(
        functools.partial(_normmod_kernel, n_x=len(x_args)),
        grid=(N_TOK // tm,),
        in_specs=x_specs + [pl.BlockSpec((1, D_MODEL), lambda i: (0, 0)), mod_spec(k_shift), mod_spec(k_scale)],
        out_specs=pl.BlockSpec((tm, D_MODEL), lambda i: (i, 0)),
        out_shape=jax.ShapeDtypeStruct((N_TOK, D_MODEL), BF16),
        compiler_params=_cparams(("arbitrary",)),
        name="norm_mod",
    )(*x_args, g.reshape(1, D_MODEL), mod3, mod3)


def _final_norm_kernel(x_ref, g_ref, o_ref):
    x = x_ref[...]
    o_ref[...] = x * lax.rsqrt(jnp.mean(x * x, axis=-1, keepdims=True) + EPS) * g_ref[...]


def _final_norm_call(x, g, row0, n_rows):
    tm = 512
    return pl.pallas_call(
        _final_norm_kernel,
        grid=(n_rows // tm,),
        in_specs=[pl.BlockSpec((tm, D_MODEL), lambda i: (row0 // tm + i, 0)),
                  pl.BlockSpec((1, D_MODEL), lambda i: (0, 0))],
        out_specs=pl.BlockSpec((tm, D_MODEL), lambda i: (i, 0)),
        out_shape=jax.ShapeDtypeStruct((n_rows, D_MODEL), F32),
        compiler_params=_cparams(("arbitrary",)),
        name="final_norm",
    )(x, g.reshape(1, D_MODEL))


def _mm_kernel(x_ref, w_ref, o_ref, wbf_ref, *, act, w_transposed):
    @pl.when(pl.program_id(1) == 0)
    def _cast_weights():
        wbf_ref[...] = w_ref[...].reshape(wbf_ref.shape).astype(BF16)

    acc = lax.dot_general(x_ref[...], wbf_ref[...], _NT if w_transposed else _NN, preferred_element_type=F32)
    if act == "relu2":
        acc = jnp.square(jnp.maximum(acc, 0.0))
    if len(o_ref.shape) == 3:
        for c in range(o_ref.shape[0]):
            o_ref[c] = acc[:, c * LANES:(c + 1) * LANES].astype(o_ref.dtype)
    else:
        o_ref[...] = acc.astype(o_ref.dtype)


def _mm_call(x, w_all, layer, *, col0, n, act, out_dtype, tm, tn, name, w_transposed=False,
             lane_tile_major=False):
    m, k = x.shape
    assert n % tn == 0 and m % tm == 0
    if w_transposed:
        assert col0 % 8 == 0
        w_spec = pl.BlockSpec((pl.Element(1), pl.Element(tn), pl.Element(k)),
                              lambda j, i: (layer, pl.multiple_of(col0 + j * tn, 8), 0))
        w_scratch = pltpu.VMEM((tn, k), BF16)
    else:
        assert col0 % tn == 0
        w_spec = pl.BlockSpec((None, k, tn), lambda j, i: (layer, 0, col0 // tn + j))
        w_scratch = pltpu.VMEM((k, tn), BF16)
    if lane_tile_major:
        out_spec = pl.BlockSpec((tn // LANES, tm, LANES), lambda j, i: (j, i, 0))
        out_shape = jax.ShapeDtypeStruct((n // LANES, m, LANES), out_dtype)
    else:
        out_spec = pl.BlockSpec((tm, tn), lambda j, i: (i, j))
        out_shape = jax.ShapeDtypeStruct((m, n), out_dtype)
    return pl.pallas_call(
        functools.partial(_mm_kernel, act=act, w_transposed=w_transposed),
        grid=(n // tn, m // tm),
        in_specs=[pl.BlockSpec((tm, k), lambda j, i: (i, 0)), w_spec],
        out_specs=out_spec,
        out_shape=out_shape,
        scratch_shapes=[w_scratch],
        compiler_params=_cparams(("arbitrary", "arbitrary")),
        name=name,
    )(x, w_all)


def _mm_bf16_kernel(x_ref, w_ref, r_ref, g_ref, o_ref):
    acc = jnp.dot(x_ref[...], w_ref[...], preferred_element_type=F32)
    o_ref[...] = r_ref[...] + g_ref[...] * acc


def _mm_resid_call(x, w, resid, mod3, layer, k_gate, *, tm, tn, name):
    m, k = x.shape
    n = w.shape[2]
    per_tile = D_MODEL // tn
    return pl.pallas_call(
        _mm_bf16_kernel,
        grid=(m // tm, n // tn),
        in_specs=[pl.BlockSpec((tm, k), lambda i, j: (i, 0)),
                  pl.BlockSpec((None, k, tn), lambda i, j: (layer, 0, j)),
                  pl.BlockSpec((tm, tn), lambda i, j: (i, j)),
                  pl.BlockSpec((None, 1, tn),
                               lambda i, j: (layer * MOD_ROWS + _mod_row_of_tile(i, tm), 0,
                                             k_gate * per_tile + j))],
        out_specs=pl.BlockSpec((tm, tn), lambda i, j: (i, j)),
        out_shape=jax.ShapeDtypeStruct((m, n), F32),
        compiler_params=_cparams(("arbitrary", "arbitrary")),
        name=name,
    )(x, w, resid, mod3)


def _outproj_norm_kernel(*refs, n_x):
    m_ref, w_ref = refs[:2]
    x_refs = refs[2:2 + n_x]
    gate_ref, g_ref, sh_ref, sc_ref, xo_ref, h_ref, wbf_ref = refs[2 + n_x:]

    @pl.when(pl.program_id(0) == 0)
    def _cast_weights():
        wbf_ref[...] = w_ref[...].astype(BF16)

    acc = jnp.dot(m_ref[...], wbf_ref[...], preferred_element_type=F32)
    xn = _residual_tile(x_refs, pl.program_id(0), xo_ref.shape[0]) + gate_ref[...] * acc
    xo_ref[...] = xn
    h_ref[...] = _adaln(xn, g_ref[...], sh_ref[...], sc_ref[...]).astype(h_ref.dtype)


def _outproj_norm_call(merged, w_out, x, norm_g, mod3, layer):
    tm = 256 if isinstance(x, tuple) else 512
    def mod_spec(k):
        return pl.BlockSpec((None, 1, D_MODEL),
                            lambda i: (layer * MOD_ROWS + _mod_row_of_tile(i, tm), 0, k))
    row = pl.BlockSpec((tm, D_MODEL), lambda i: (i, 0))
    x_specs, x_args = _residual_specs(x, tm, lambda i: i)
    return pl.pallas_call(
        functools.partial(_outproj_norm_kernel, n_x=len(x_args)),
        grid=(N_TOK // tm,),
        in_specs=[row,
                  pl.BlockSpec((None, D_MODEL, D_MODEL), lambda i: (layer, 0, 0), pipeline_mode=pl.Buffered(1))]
                 + x_specs + [mod_spec(2), pl.BlockSpec((1, D_MODEL), lambda i: (0, 0)), mod_spec(3), mod_spec(4)],
        out_specs=[row, row],
        out_shape=[jax.ShapeDtypeStruct((N_TOK, D_MODEL), F32),
                   jax.ShapeDtypeStruct((N_TOK, D_MODEL), BF16)],
        scratch_shapes=[pltpu.VMEM((D_MODEL, D_MODEL), BF16)],
        compiler_params=_cparams(("arbitrary",)),
        name="out_proj_norm",
    )(merged, w_out, *x_args, mod3, norm_g.reshape(1, D_MODEL), mod3, mod3)


def _ffn_down_norm_kernel(x_ref, w_ref, r_ref, gate_ref, g_ref, sh_ref, sc_ref, o_ref, h_ref, row_ref):
    j = pl.program_id(1)
    tn = o_ref.shape[1]
    acc = jnp.dot(x_ref[...], w_ref[...], preferred_element_type=F32)
    xn = r_ref[...] + gate_ref[...] * acc
    o_ref[...] = xn
    row_ref[:, pl.ds(pl.multiple_of(j * tn, tn), tn)] = xn

    @pl.when(j == pl.num_programs(1) - 1)
    def _next_layer_norm():
        h_ref[...] = _adaln(row_ref[...], g_ref[...], sh_ref[...], sc_ref[...]).astype(h_ref.dtype)


def _ffn_down_norm_call(up, w_bf, resid, mod3, layer, next_norm_g):
    tm, tn = 512, 512
    m, k = up.shape
    per_tile = D_MODEL // tn
    def mod_row(i):
        return _mod_row_of_tile(i, tm)
    def next_mod(kk):
        return pl.BlockSpec((None, 1, D_MODEL), lambda i, j: ((layer + 1) * MOD_ROWS + mod_row(i), 0, kk))
    return pl.pallas_call(
        _ffn_down_norm_kernel,
        grid=(m // tm, D_MODEL // tn),
        in_specs=[pl.BlockSpec((tm, k), lambda i, j: (i, 0)),
                  pl.BlockSpec((None, k, tn), lambda i, j: (layer, 0, j)),
                  pl.BlockSpec((tm, tn), lambda i, j: (i, j)),
                  pl.BlockSpec((None, 1, tn), lambda i, j: (layer * MOD_ROWS + mod_row(i), 0, 5 * per_tile + j)),
                  pl.BlockSpec((1, D_MODEL), lambda i, j: (0, 0)), next_mod(0), next_mod(1)],
        out_specs=[pl.BlockSpec((tm, tn), lambda i, j: (i, j)),
                   pl.BlockSpec((tm, D_MODEL), lambda i, j: (i, 0))],
        out_shape=[jax.ShapeDtypeStruct((m, D_MODEL), F32),
                   jax.ShapeDtypeStruct((m, D_MODEL), BF16)],
        scratch_shapes=[pltpu.VMEM((tm, D_MODEL), F32)],
        compiler_params=_cparams(("arbitrary", "arbitrary")),
        name="ffn_down_norm",
    )(up, w_bf, resid, mod3, next_norm_g.reshape(1, D_MODEL), mod3, mod3)


def _gated_merge_kernel(h_ref, wga_ref, wgb_ref, wgc_ref, oa_ref, ob_ref, oc_ref, wa_ref, wb_ref, wc_ref,
                        o_ref, wg_bf, wbr_bf):
    @pl.when(pl.program_id(1) == 0)
    def _cast_weights():
        for t, ref in enumerate((wga_ref, wgb_ref, wgc_ref)):
            wg_bf[t] = ref[...].reshape(wg_bf.shape[1:]).astype(BF16)
        for t, ref in enumerate((wa_ref, wb_ref, wc_ref)):
            wbr_bf[t] = ref[...].astype(BF16)

    def rows_of(ref):
        return jnp.concatenate([ref[t] for t in range(ref.shape[0])], axis=1)

    h = h_ref[...]
    acc = None
    for t, br_ref in enumerate((oa_ref, ob_ref, oc_ref)):
        gate = _sigmoid(lax.dot_general(h, wg_bf[t], _NT, preferred_element_type=F32))
        term = gate * jnp.dot(rows_of(br_ref), wbr_bf[t], preferred_element_type=F32)
        acc = term if acc is None else acc + term
    o_ref[...] = acc.astype(o_ref.dtype)


def _gated_merge_call(h, w_in_t, o_a, o_b, o_c, w_a, w_b, w_c, layer):
    tm, tn = 1024, 256
    def gate_w(b):
        return pl.BlockSpec((pl.Element(1), pl.Element(tn), pl.Element(D_MODEL)),
                            lambda j, i: (layer, pl.multiple_of(OFF_GATES + b * D_MODEL + j * tn, 8), 0))
    br = pl.BlockSpec((HEADS, tm, HEAD_DIM), lambda j, i: (0, i, 0))
    wt = pl.BlockSpec((None, HG_F, tn), lambda j, i: (layer, 0, j))
    return pl.pallas_call(
        _gated_merge_kernel,
        grid=(D_MODEL // tn, N_TOK // tm),
        in_specs=[pl.BlockSpec((tm, D_MODEL), lambda j, i: (i, 0)), gate_w(0), gate_w(1), gate_w(2),
                  br, br, br, wt, wt, wt],
        out_specs=pl.BlockSpec((tm, tn), lambda j, i: (i, j)),
        out_shape=jax.ShapeDtypeStruct((N_TOK, D_MODEL), BF16),
        scratch_shapes=[pltpu.VMEM((3, tn, D_MODEL), BF16), pltpu.VMEM((3, HG_F, tn), BF16)],
        compiler_params=_cparams(("arbitrary", "arbitrary")),
        name="gated_merge",
    )(h, w_in_t, w_in_t, w_in_t, o_a, o_b, o_c, w_a, w_b, w_c)


def _head_rmsnorm_gate(o, g, og):
    y = o * lax.rsqrt(jnp.mean(o * o, axis=-1, keepdims=True) + EPS) * g
    return y * _silu(og)


def _hgrn_kernel(q_ref, i_ref, og_ref, ff_ref, fb_ref, lb_ref, g_ref, s0_ref, o_ref, s_ref,
                 acc_ref, b_ref, stc_ref, km_ref, *, state_slot):
    is_ctx = _is_ctx_step(pl.program_id(1))
    _zero_other_slots(s_ref, state_slot)
    nblk = UNIT // HG_BLOCK
    nsub = HG_BLOCK // HG_SUB
    r = lax.broadcasted_iota(jnp.int32, (HG_BLOCK, HG_BLOCK), 0)
    c = lax.broadcasted_iota(jnp.int32, (HG_BLOCK, HG_BLOCK), 1)
    same = (r // HG_SUB) == (c // HG_SUB)
    row_sub = lax.broadcasted_iota(jnp.int32, (HG_BLOCK, HEAD_DIM), 0) // HG_SUB
    tri = (jnp.logical_and(same, c <= r), jnp.logical_and(same, c >= r))
    tri_bf = [jnp.where(t, 1.0, 0.0).astype(BF16) for t in tri]
    z_refs = (ff_ref, fb_ref)
    inst = [(hh, blk, d) for hh in range(HG_HEADS_PER_STEP) for blk in range(nblk) for d in range(2)]

    def expand(x):
        return jnp.broadcast_to(x[:, None, :], (nsub, HG_SUB, HEAD_DIM)).reshape(HG_BLOCK, HEAD_DIM)

    def blk_rows(blk):
        return slice(blk * HG_BLOCK, (blk + 1) * HG_BLOCK)

    def head_cols(hh):
        return slice(hh * HEAD_DIM, (hh + 1) * HEAD_DIM)

    qs = {(hh, blk): _silu(q_ref[hh, blk_rows(blk), :]) for hh in range(HG_HEADS_PER_STEP) for blk in range(nblk)}
    vs = {(hh, blk): i_ref[hh, blk_rows(blk), :] for hh in range(HG_HEADS_PER_STEP) for blk in range(nblk)}
    ks, lfs = [], []
    for hh, blk, d in inst:
        lb = lb_ref[d:d + 1, head_cols(hh)]
        f = lb + (1.0 - lb) * _sigmoid(z_refs[d][hh, blk_rows(blk), :])
        lfs.append(jnp.log(f))
        ks.append(1.0 - f)
    bs = [_dot01_2(tri_bf[d], lf) for (hh, blk, d), lf in zip(inst, lfs)]
    tots, qts, kts, qds = [], [], [], []
    for n, (hh, blk, d) in enumerate(inst):
        b = bs[n]
        b_ref[n] = b
        tot = b_ref[n, pl.ds((HG_SUB - 1) if d == 0 else 0, nsub, stride=HG_SUB), :]
        mid_f = expand(b_ref[n, pl.ds(HG_SUB // 2, nsub, stride=HG_SUB), :])
        tots.append(tot)
        qts.append(qs[hh, blk] * jnp.exp(b - mid_f))
        kts.append(ks[n] * jnp.exp(mid_f - b))
        qds.append((qs[hh, blk] * jnp.exp(b)).astype(BF16))
        kd = ks[n] * jnp.exp(expand(tot) - b)
        for s in range(nsub):
            km_ref[n, :, s * HEAD_DIM:(s + 1) * HEAD_DIM] = jnp.where(row_sub == s, kd, 0.0).astype(BF16)
    scs = [jnp.where(tri[d], _dot1(qt, kt, _NT), 0.0) for (hh, blk, d), qt, kt in zip(inst, qts, kts)]
    uts = [_dg(vs[hh, blk].astype(BF16), km_ref[n], _TN) for n, (hh, blk, d) in enumerate(inst)]
    outs = [_dot1(sc, vs[hh, blk]) for (hh, blk, d), sc in zip(inst, scs)]
    for hh in range(HG_HEADS_PER_STEP):
        for d in range(2):
            st = s0_ref[d, hh].T
            for blk in (range(nblk) if d == 0 else range(nblk - 1, -1, -1)):
                n = inst.index((hh, blk, d))
                st = jnp.where(is_ctx, 0.0, st)
                for s in (range(nsub) if d == 0 else range(nsub - 1, -1, -1)):
                    stc_ref[n, s] = st.T.astype(BF16)
                    st = st * jnp.exp(tots[n][s:s + 1, :]) + uts[n][:, s * HEAD_DIM:(s + 1) * HEAD_DIM]
                s_ref[blk, state_slot, d, hh] = st.T
    for n, (hh, blk, d) in enumerate(inst):
        o_int = [_dg(qds[n][s * HG_SUB:(s + 1) * HG_SUB], stc_ref[n, s], _NN) for s in range(nsub)]
        acc_ref[d, hh, blk_rows(blk), :] = outs[n] + jnp.concatenate(o_int, axis=0)
    for hh in range(HG_HEADS_PER_STEP):
        o_ref[hh] = _head_rmsnorm_gate(acc_ref[0, hh] + acc_ref[1, hh], g_ref[...], og_ref[hh]).astype(o_ref.dtype)


def _hgrn_call(z, lb, onorm_g, state, layer, new_states):
    nh = HG_HEADS_PER_STEP
    def col(off):
        return pl.BlockSpec((nh, UNIT, HEAD_DIM), lambda h, g: (off // HEAD_DIM // nh + h, _scan_unit(g), 0))
    in_specs = [col(OFF_HQ), col(OFF_HI), col(OFF_HG), col(OFF_HFF), col(OFF_HFB),
                pl.BlockSpec((2, nh * HEAD_DIM), lambda h, g: (0, h)),
                pl.BlockSpec((1, HEAD_DIM), lambda h, g: (0, 0)),
                _state_in_spec(layer, nh)]
    args = [z, z, z, z, z, lb, onorm_g.reshape(1, HEAD_DIM), state]
    first = new_states is None
    kernel, aliases = functools.partial(_hgrn_kernel, state_slot=layer if first else 0), {}
    if not first:
        kernel, aliases = _without_arg(kernel, len(args)), {len(args): 1}
        in_specs.append(pl.BlockSpec(memory_space=pl.ANY))
        args.append(new_states)
    n_inst = nh * HG_INST
    return pl.pallas_call(
        kernel,
        grid=(HEADS // nh, N_UNITS),
        in_specs=in_specs,
        out_specs=[pl.BlockSpec((nh, UNIT, HEAD_DIM), lambda h, g: (h, _scan_unit(g), 0)),
                   _state_out_spec(layer, first, nh)],
        out_shape=[jax.ShapeDtypeStruct((HEADS, N_TOK, HEAD_DIM), BF16), _STATE_SHAPE],
        input_output_aliases=aliases,
        scratch_shapes=[pltpu.VMEM((2, nh, UNIT, HEAD_DIM), F32),
                        pltpu.VMEM((n_inst, HG_BLOCK, HEAD_DIM), F32),
                        pltpu.VMEM((n_inst, HG_BLOCK // HG_SUB, HEAD_DIM, HEAD_DIM), BF16),
                        pltpu.VMEM((n_inst, HG_BLOCK, (HG_BLOCK // HG_SUB) * HEAD_DIM), BF16)],
        compiler_params=_cparams(("arbitrary", "arbitrary")),
        name="hgrn2_scan",
    )(*args)


def _gmlp_kernel(u_ref, v_ref, vn_ref, ws_ref, bs_ref, o_ref):
    tm = u_ref.shape[1]
    for g in range(CM_GROUPS):
        cols = slice(g * HEAD_DIM, (g + 1) * HEAD_DIM)
        vg = _gelu(v_ref[g])
        vg = vg * lax.rsqrt(jnp.mean(vg * vg, axis=-1, keepdims=True) + EPS) * vn_ref[:, cols]
        ug = _gelu(u_ref[g])
        w = ws_ref[g]
        bias = bs_ref[:, g:g + 1]
        for ch in range(tm // CM_CHUNK):
            rows = slice(ch * CM_CHUNK, (ch + 1) * CM_CHUNK)
            s = _dot3(w, vg[rows]) + bias
            o_ref[g, rows, :] = (ug[rows] * s).astype(o_ref.dtype)


def _gmlp_call(z, vnorm_g, ws, bs_t):
    tm = 512
    return pl.pallas_call(
        _gmlp_kernel,
        grid=(N_TOK // tm,),
        in_specs=[pl.BlockSpec((CM_GROUPS, tm, HEAD_DIM), lambda i: (OFF_CU // CM_W, i, 0)),
                  pl.BlockSpec((CM_GROUPS, tm, HEAD_DIM), lambda i: (OFF_CV // CM_W, i, 0)),
                  pl.BlockSpec((1, CM_W), lambda i: (0, 0)),
                  pl.BlockSpec((CM_GROUPS, CM_CHUNK, CM_CHUNK), lambda i: (0, 0, 0)),
                  pl.BlockSpec((CM_CHUNK, CM_GROUPS), lambda i: (0, 0))],
        out_specs=pl.BlockSpec((CM_GROUPS, tm, HEAD_DIM), lambda i: (0, i, 0)),
        out_shape=jax.ShapeDtypeStruct((CM_GROUPS, N_TOK, HEAD_DIM), BF16),
        compiler_params=_cparams(("arbitrary",)),
        name="chunk_gmlp",
    )(z, z, vnorm_g.reshape(1, CM_W), ws, bs_t)


GP_BLOCK = 256


def _gdn_gates_kernel(ab_ref, alog_ref, dt_ref, col_ref, rowt_ref):
    ab = ab_ref[...]
    lane = lax.broadcasted_iota(jnp.int32, ab.shape, 1)
    g = jnp.where(lane < 2 * HEADS, -jnp.exp(alog_ref[...]) * _softplus(ab + dt_ref[...]), 0.0)
    r = lax.broadcasted_iota(jnp.int32, (GP_BLOCK, GP_BLOCK), 0)
    c = lax.broadcasted_iota(jnp.int32, (GP_BLOCK, GP_BLOCK), 1)
    same = (r // GDN_CHUNK) == (c // GDN_CHUNK)
    tri_f = jnp.where(jnp.logical_and(same, c <= r), 1.0, 0.0).astype(BF16)
    tri_b = jnp.where(jnp.logical_and(same, c >= r), 1.0, 0.0).astype(BF16)
    cf = _dot01(tri_f, g)
    cb = _dot01(tri_b, g)
    col = jnp.where(lane < HEADS, cf, jnp.where(lane < 2 * HEADS, cb, _sigmoid(ab)))
    col_ref[...] = col
    rowt_ref[...] = col.T


def _gdn_gates_call(ab, a_log, dt_bias):
    pad = lambda t: jnp.pad(t.reshape(1, 2 * HEADS), ((0, 0), (0, LANES - 2 * HEADS)))
    return pl.pallas_call(
        _gdn_gates_kernel,
        grid=(N_TOK // GP_BLOCK,),
        in_specs=[pl.BlockSpec((GP_BLOCK, LANES), lambda i: (i, 0)),
                  pl.BlockSpec((1, LANES), lambda i: (0, 0)),
                  pl.BlockSpec((1, LANES), lambda i: (0, 0))],
        out_specs=[pl.BlockSpec((GP_BLOCK, LANES), lambda i: (i, 0)),
                   pl.BlockSpec((LANES, GP_BLOCK), lambda i: (0, i))],
        out_shape=[jax.ShapeDtypeStruct((N_TOK, LANES), F32),
                   jax.ShapeDtypeStruct((LANES, N_TOK), F32)],
        compiler_params=_cparams(("arbitrary",)),
        name="gdn_gates",
    )(ab, pad(a_log), pad(dt_bias))


CONV_PAD = 72
GDN_AQ = HEAD_DIM + GDN_CHUNK
GDN_HEADS_PER_STEP = 2
GDN_STEP = 16 * GDN_CHUNK


def _gdn_kernel(qr_ref, kr_ref, vr_ref, og_ref, col_ref, rowt_ref, cwq_ref, cwk_ref, cwv_ref,
                g_ref, s0_ref, o_ref, s_ref,
                xp_ref, q_s, k_s, v_s, o_s, b_s, aq_s, *, state_slot):
    is_ctx = _is_ctx_step(pl.program_id(1))
    _zero_other_slots(s_ref, state_slot)
    heads = [GDN_HEADS_PER_STEP * pl.program_id(0) + hh for hh in range(GDN_HEADS_PER_STEP)]
    n_chunks = UNIT // GDN_CHUNK
    chunks_per_seq = SEQ // GDN_CHUNK

    t = lax.broadcasted_iota(jnp.int32, (UNIT, 1), 0)
    period = jnp.where(is_ctx, SEQ, GRID_W)
    pos = jnp.bitwise_and(t, period - 1)
    ok_left = pos != 0
    ok_right = pos != period - 1
    zeros_pad = jnp.zeros((CONV_PAD, HEAD_DIM), F32)
    xp_ref[0:CONV_PAD, :] = zeros_pad
    xp_ref[CONV_PAD + UNIT:CONV_PAD + UNIT + CONV_PAD, :] = zeros_pad

    def conv_silu(x_ref, w_ref, hh):
        xp_ref[CONV_PAD:CONV_PAD + UNIT, :] = x_ref[hh]
        acc = jnp.zeros((UNIT, HEAD_DIM), F32)
        for i in range(CONV_K):
            for j in range(CONV_K):
                w = w_ref[CONV_K * i + j:CONV_K * i + j + 1, hh * HEAD_DIM:(hh + 1) * HEAD_DIM]
                if i != CONV_K // 2:
                    w = jnp.where(is_ctx, 0.0, w)
                start = CONV_PAD + (i - 1) * GRID_W + (j - 1)
                xs = xp_ref[start:start + UNIT, :]
                if j == 0:
                    xs = jnp.where(ok_left, xs, 0.0)
                elif j == CONV_K - 1:
                    xs = jnp.where(ok_right, xs, 0.0)
                acc = acc + xs * w
        return _silu(acc)

    def l2norm(x):
        return x * lax.rsqrt(jnp.sum(x * x, axis=-1, keepdims=True) + EPS)

    rr = lax.broadcasted_iota(jnp.int32, (GDN_CHUNK, GDN_CHUNK), 0)
    cc = lax.broadcasted_iota(jnp.int32, (GDN_CHUNK, GDN_CHUNK), 1)
    eye = jnp.where(rr == cc, 1.0, 0.0)
    same_blk = [(rr // b) == (cc // b) for b in (8, 16, 32, 64)]
    lane = lax.broadcasted_iota(jnp.int32, (GDN_STEP, LANES), 1)
    sub8 = lax.broadcasted_iota(jnp.int32, (HEADS, GDN_STEP), 0)

    def pick(x, j):
        return jnp.sum(jnp.where(lane[:x.shape[0]] == j, x, 0.0), axis=-1, keepdims=True)

    incl = (cc <= rr, cc >= rr)
    strict = (cc < rr, cc > rr)
    off_masks = [jnp.logical_and(same_blk[lvl], jnp.logical_not(same_blk[lvl - 1]))
                 for lvl in range(1, len(same_blk))]

    def phase1(p, hh):
        head = heads[hh]
        rows = pl.ds(pl.multiple_of(p * GDN_STEP, GDN_STEP), GDN_STEP)
        q2, k2, v2 = q_s[rows, :], k_s[rows, :], v_s[rows, :]
        col = col_ref[rows, :]
        gcols = [pick(col, d * HEADS + head) for d in range(2)]
        betas = [pick(col, (2 + d) * HEADS + head) for d in range(2)]
        grows = [jnp.sum(jnp.where(sub8 == head, rowt_ref[d * HEADS:(d + 1) * HEADS, rows], 0.0),
                         axis=0, keepdims=True) for d in range(2)]
        inst = []
        for half in range(GDN_STEP // GDN_CHUNK):
            sl = slice(half * GDN_CHUNK, (half + 1) * GDN_CHUNK)
            qc, kc, vc = q2[sl], k2[sl], v2[sl]
            kk = _dot1(kc, kc, _NT)
            qk = _dot1(qc, kc, _NT)
            for d in range(2):
                gcol, beta = gcols[d][sl], betas[d][sl]
                decay = jnp.where(incl[d], jnp.exp(gcol - grows[d][:, sl]), 0.0)
                m = jnp.where(strict[d], beta * kk * decay, 0.0)
                inst.append((half, d, qc, kc, vc, qk, gcol, beta, decay, m))
        ms = [t[-1] for t in inst]
        pws = [jnp.where(same_blk[0], -m, 0.0) for m in ms]
        tinvs = [eye + pw for pw in pws]
        for _ in range(2):
            pws = [_dot_inv(pw, pw) for pw in pws]
            tinvs = [t + _dot_inv(t, pw) for t, pw in zip(tinvs, pws)]
        for mask in off_masks:
            tmp = [_dot_inv(jnp.where(mask, m, 0.0), t) for m, t in zip(ms, tinvs)]
            tinvs = [t - _dot_inv(t, x) for t, x in zip(tinvs, tmp)]
        uws = [_dot1(tinv, jnp.concatenate([vc * beta, kc * (beta * jnp.exp(gcol))], axis=1)).astype(BF16)
               for (half, d, qc, kc, vc, qk, gcol, beta, decay, m), tinv in zip(inst, tinvs)]
        bas, ows = [], []
        for (half, d, qc, kc, vc, qk, gcol, beta, decay, m), uw in zip(inst, uws):
            glast = gcol[GDN_CHUNK - 1:GDN_CHUNK] if d == 0 else gcol[0:1]
            bas.append(_dg((kc * jnp.exp(glast - gcol)).astype(BF16), uw, _TN))
            ows.append(_dg((qk * decay).astype(BF16), uw, _NN))
        for (half, d, qc, kc, vc, qk, gcol, beta, decay, m), ba, ow in zip(inst, bas, ows):
            chunk0 = p * GDN_STEP + half * GDN_CHUNK
            o_s[hh, d, pl.ds(pl.multiple_of(chunk0, GDN_CHUNK), GDN_CHUNK), :] = ow[:, :HEAD_DIM]
            b_s[hh, d, pl.ds(pl.multiple_of(2 * chunk0, HEAD_DIM), HEAD_DIM), :] = ba[:, :HEAD_DIM]
            aq_s[hh, d, pl.ds(pl.multiple_of(3 * chunk0, GDN_AQ), GDN_AQ), :] = jnp.concatenate(
                [ba[:, HEAD_DIM:], qc * jnp.exp(gcol) - ow[:, HEAD_DIM:]], axis=0).astype(BF16)

    for hh in range(GDN_HEADS_PER_STEP):
        q_s[...] = l2norm(conv_silu(qr_ref, cwq_ref, hh)) * (HEAD_DIM ** -0.5)
        k_s[...] = l2norm(conv_silu(kr_ref, cwk_ref, hh))
        v_s[...] = conv_silu(vr_ref, cwv_ref, hh)
        for p in range(UNIT // GDN_STEP):
            phase1(p, hh)

    lane1 = lax.broadcasted_iota(jnp.int32, (1, LANES), 1)

    def advance(chains, states):
        rs = [_dg(aq_s[hh, d, pl.ds(pl.multiple_of(ch * GDN_AQ, GDN_AQ), GDN_AQ), :], s.astype(BF16), _NN)
              for (hh, d, ch), s in zip(chains, states)]
        new = []
        for (hh, d, ch), s, r in zip(chains, states, rs):
            tile0 = pl.multiple_of(ch * GDN_CHUNK + (GDN_CHUNK - 8 if d == 0 else 0), 8)
            last = col_ref[pl.ds(tile0, 8), :]
            last = last[7:8] if d == 0 else last[0:1]
            glast = jnp.sum(jnp.where(lane1 == d * HEADS + heads[hh], last, 0.0), axis=-1, keepdims=True)
            b = b_s[hh, d, pl.ds(pl.multiple_of(ch * HEAD_DIM, HEAD_DIM), HEAD_DIM), :]
            new.append(s * jnp.exp(glast) + (b - r[:HEAD_DIM]))
            rows = pl.ds(pl.multiple_of(ch * GDN_CHUNK, GDN_CHUNK), GDN_CHUNK)
            o_s[hh, d, rows, :] = o_s[hh, d, rows, :] + r[HEAD_DIM:]
        return tuple(new)

    @pl.when(is_ctx)
    def _context_unit():
        def body(i, states):
            chains = [(hh, d, sq * chunks_per_seq + (i if d == 0 else chunks_per_seq - 1 - i))
                      for hh in range(GDN_HEADS_PER_STEP) for sq in range(SEQ_PER_UNIT) for d in range(2)]
            return advance(chains, states)
        zero = jnp.zeros((HEAD_DIM, HEAD_DIM), F32)
        final = lax.fori_loop(0, chunks_per_seq, body, (zero,) * (2 * SEQ_PER_UNIT * GDN_HEADS_PER_STEP))
        for hh in range(GDN_HEADS_PER_STEP):
            for sq in range(SEQ_PER_UNIT):
                for d in range(2):
                    s_ref[sq, state_slot, d, hh] = final[(hh * SEQ_PER_UNIT + sq) * 2 + d]

    @pl.when(jnp.logical_not(is_ctx))
    def _latent_unit():
        def body(i, states):
            chains = [(hh, d, i if d == 0 else n_chunks - 1 - i) for hh in range(GDN_HEADS_PER_STEP) for d in range(2)]
            return advance(chains, states)
        init = tuple(s0_ref[d, hh] for hh in range(GDN_HEADS_PER_STEP) for d in range(2))
        final = lax.fori_loop(0, n_chunks, body, init)
        for hh in range(GDN_HEADS_PER_STEP):
            for sq in range(SEQ_PER_UNIT):
                for d in range(2):
                    s_ref[sq, state_slot, d, hh] = final[2 * hh + d]

    for hh in range(GDN_HEADS_PER_STEP):
        o_ref[hh] = _head_rmsnorm_gate(o_s[hh, 0] + o_s[hh, 1], g_ref[...], og_ref[hh]).astype(o_ref.dtype)


def _gdn_call(z, col, rowt, conv_w, onorm_g, state, layer, new_states):
    nh = GDN_HEADS_PER_STEP
    def zcol(off):
        return pl.BlockSpec((nh, UNIT, HEAD_DIM), lambda h, g: (off // HEAD_DIM // nh + h, _scan_unit(g), 0))
    def wcol(part):
        return pl.BlockSpec((CONV_K * CONV_K, nh * HEAD_DIM), lambda h, g: (0, part * HEADS // nh + h))
    scr = lambda *shape: pltpu.VMEM(shape, F32)
    in_specs = [zcol(OFF_GQ), zcol(OFF_GK), zcol(OFF_GV), zcol(OFF_GG),
                pl.BlockSpec((UNIT, LANES), lambda h, g: (_scan_unit(g), 0)),
                pl.BlockSpec((LANES, UNIT), lambda h, g: (0, _scan_unit(g))),
                wcol(0), wcol(1), wcol(2),
                pl.BlockSpec((1, HEAD_DIM), lambda h, g: (0, 0)),
                _state_in_spec(layer, nh)]
    args = [z, z, z, z, col, rowt, conv_w, conv_w, conv_w, onorm_g.reshape(1, HEAD_DIM), state]
    first = new_states is None
    kernel, aliases = functools.partial(_gdn_kernel, state_slot=layer if first else 0), {}
    if not first:
        kernel, aliases = _without_arg(kernel, len(args)), {len(args): 1}
        in_specs.append(pl.BlockSpec(memory_space=pl.ANY))
        args.append(new_states)
    return pl.pallas_call(
        kernel,
        grid=(HEADS // nh, N_UNITS),
        in_specs=in_specs,
        out_specs=[pl.BlockSpec((nh, UNIT, HEAD_DIM), lambda h, g: (h, _scan_unit(g), 0)),
                   _state_out_spec(layer, first, nh)],
        out_shape=[jax.ShapeDtypeStruct((HEADS, N_TOK, HEAD_DIM), BF16), _STATE_SHAPE],
        input_output_aliases=aliases,
        scratch_shapes=[scr(UNIT + 2 * CONV_PAD, HEAD_DIM),
                        scr(UNIT, HEAD_DIM), scr(UNIT, HEAD_DIM), scr(UNIT, HEAD_DIM),
                        scr(nh, 2, UNIT, HEAD_DIM),
                        scr(nh, 2, (UNIT // GDN_CHUNK) * HEAD_DIM, HEAD_DIM),
                        pltpu.VMEM((nh, 2, (UNIT // GDN_CHUNK) * GDN_AQ, HEAD_DIM), BF16)],
        compiler_params=_cparams(("arbitrary", "arbitrary")),
        name="gdn_scan",
    )(*args)


def kernel(x_prompt, x_sample, c, state_hgrn, state_gdn, c_ctx, norm1_g, norm2_g, w_mod, b_mod, w_in, hg_lb, hg_onorm_g, cm_vnorm_g, cm_ws, cm_bs, gdn_conv, gdn_A_log, gdn_dt_bias, gdn_onorm_g, w_br_hg, w_br_cm, w_br_gdn, w_out, w_ff1, w_ff2, final_g):
    x = (x_prompt.reshape(N_CTX_TOK, D_MODEL), x_sample.reshape(N_LAT_TOK, D_MODEL))
    cvec = jnp.concatenate([c_ctx[None, :], c, jnp.zeros((MOD_ROWS - 1 - DEC_BATCH, D_MODEL), F32)], axis=0)
    mod3 = _mod_call(cvec, w_mod, b_mod).reshape(DEPTH * MOD_ROWS, 1, 6 * D_MODEL)

    lb_all = jnp.cumsum(jax.nn.softmax(hg_lb.astype(F32), axis=0), axis=0)
    lb_all = lb_all - lb_all[:1]

    w_in_t = jnp.swapaxes(w_in, 1, 2)
    w_ff2_bf = w_ff2.astype(BF16)

    new_hg = new_gdn = None
    h = _normmod_call(x, norm1_g[0], mod3, 0, 0, 1)
    for l in range(DEPTH):
        z = _mm_call(h, w_in_t, l, col0=0, n=N_MAIN, act=None, out_dtype=F32, tm=1024, tn=1024,
                     w_transposed=True, lane_tile_major=True, name="in_proj")
        ab = _mm_call(h, w_in_t, l, col0=OFF_AB, n=LANES, act=None, out_dtype=F32, tm=1024, tn=LANES,
                      w_transposed=True, name="in_proj_ab")
        o_a, new_hg = _hgrn_call(z, lb_all[l], hg_onorm_g[l], state_hgrn, l, new_hg)
        o_b = _gmlp_call(z, cm_vnorm_g[l], cm_ws[l], cm_bs[l].T)
        col, rowt = _gdn_gates_call(ab, gdn_A_log[l], gdn_dt_bias[l])
        o_c, new_gdn = _gdn_call(z, col, rowt, gdn_conv[l].reshape(CONV_K * CONV_K, 3 * HG_F),
                                 gdn_onorm_g[l], state_gdn, l, new_gdn)

        merged = _gated_merge_call(h, w_in_t, o_a, o_b, o_c, w_br_hg, w_br_cm, w_br_gdn, l)
        x, h2 = _outproj_norm_call(merged, w_out, x, norm2_g[l], mod3, l)
        up = _mm_call(h2, w_ff1, l, col0=0, n=D_FF, act="relu2", out_dtype=BF16, tm=2048, tn=1024, name="ffn_up")
        if l + 1 < DEPTH:
            x, h = _ffn_down_norm_call(up, w_ff2_bf, x, mod3, l, norm1_g[l + 1])
        else:
            x = _mm_resid_call(up, w_ff2_bf, x, mod3, l, 5, tm=512, tn=512, name="ffn_down")

    y_prompt = _final_norm_call(x, final_g, 0, N_CTX_TOK).reshape(BATCH, SEQ, D_MODEL)
    y_sample = _final_norm_call(x, final_g, N_CTX_TOK, N_LAT_TOK).reshape(DEC_BATCH, DEC_SEQ, D_MODEL)
    return (y_prompt, y_sample, new_hg, new_gdn)
```

```python
import functools

import jax
import jax.numpy as jnp
from jax import lax
from jax.experimental import pallas as pl
from jax.experimental.pallas import tpu as pltpu

F32 = jnp.float32
BF16 = jnp.bfloat16

D_MODEL = 2048
BATCH = 16
SEQ = 256
DEPTH = 2
DEC_BATCH = 4
DEC_SEQ = 1024
GRID_W = 64
EPS = 1e-6
D_FF = 4 * D_MODEL
HEADS = 8
HEAD_DIM = 128
HG_F = HEADS * HEAD_DIM
CM_GROUPS = 8
CM_W = CM_GROUPS * HEAD_DIM
CM_CHUNK = 128
GDN_CHUNK = 64
CONV_K = 3

OFF_HQ, OFF_HI, OFF_HG, OFF_HFF, OFF_HFB = 0, 1024, 2048, 3072, 4096
OFF_CU, OFF_CV = 5120, 6144
OFF_GQ, OFF_GK, OFF_GV, OFF_GG = 7168, 8192, 9216, 10240
OFF_AB = 11264
OFF_GATES = 11296
IN_DIM = 17440
N_MAIN = OFF_AB

N_CTX_TOK = BATCH * SEQ
N_LAT_TOK = DEC_BATCH * DEC_SEQ
N_TOK = N_CTX_TOK + N_LAT_TOK
UNIT = DEC_SEQ
N_CTX_UNITS = N_CTX_TOK // UNIT
N_UNITS = N_TOK // UNIT
N_LAT_UNITS = N_UNITS - N_CTX_UNITS
SEQ_PER_UNIT = UNIT // SEQ

LANES = 128
MOD_ROWS = 8
HG_BLOCK = 256
HG_SUB = 32
HG_INST = 2 * (UNIT // HG_BLOCK)
HG_HEADS_PER_STEP = 2
VMEM_LIMIT = 56 * 1024 * 1024

_NT = (((1,), (1,)), ((), ()))
_TN = (((0,), (0,)), ((), ()))
_NN = (((1,), (0,)), ((), ()))


def _dg(a, b, dims):
    return lax.dot_general(a, b, dims, preferred_element_type=F32)


def _split2(x):
    hi = x.astype(BF16)
    lo = (x - hi.astype(F32)).astype(BF16)
    return hi, lo


def _dot1(a, b, dims=_NN):
    return _dg(a.astype(BF16), b.astype(BF16), dims)


def _dot3(a, b, dims=_NN):
    ah, al = _split2(a)
    bh, bl = _split2(b)
    return _dg(ah, bh, dims) + (_dg(ah, bl, dims) + _dg(al, bh, dims))


_dot_inv = _dot1


def _dot01(m, x):
    hi = x.astype(BF16)
    r = x - hi.astype(F32)
    mid = r.astype(BF16)
    lo = (r - mid.astype(F32)).astype(BF16)
    return _dg(m, hi, _NN) + (_dg(m, mid, _NN) + _dg(m, lo, _NN))


def _dot01_2(m, x):
    hi, lo = _split2(x)
    return _dg(m, hi, _NN) + _dg(m, lo, _NN)


def _sigmoid(x):
    return 1.0 / (1.0 + jnp.exp(-x))


def _silu(x):
    return x * _sigmoid(x)


def _gelu(x):
    return 0.5 * x * (1.0 + lax.erf(x * (2.0 ** -0.5)))


def _softplus(x):
    return jnp.maximum(x, 0.0) + jnp.log1p(jnp.exp(-jnp.abs(x)))


def _mod_row_of_tile(i, tm):
    return jnp.maximum(0, (i * tm - N_CTX_TOK) // DEC_SEQ + 1)


assert N_LAT_UNITS == N_CTX_UNITS


def _scan_unit(g):
    return jnp.where(g % 2 == 0, N_CTX_UNITS + g // 2, g // 2)


def _is_ctx_step(g):
    return g % 2 == 1


def _state_out_spec(layer, first, heads_per_step=None):
    if first:
        return pl.BlockSpec((SEQ_PER_UNIT, DEPTH, 2, heads_per_step, HEAD_DIM, HEAD_DIM),
                            lambda h, g: (g // 2, 0, 0, h, 0, 0))
    return pl.BlockSpec((SEQ_PER_UNIT, 1, 2, heads_per_step, HEAD_DIM, HEAD_DIM),
                        lambda h, g: (g // 2, layer, 0, h, 0, 0))


def _zero_other_slots(s_ref, slot):
    for other in range(s_ref.shape[1]):
        if other != slot:
            s_ref[:, other] = jnp.zeros((s_ref.shape[0],) + tuple(s_ref.shape[2:]), F32)


def _state_in_spec(layer, heads_per_step=None):
    return pl.BlockSpec((None, None, 2, heads_per_step, HEAD_DIM, HEAD_DIM),
                        lambda h, g: (g // 2, layer, 0, h, 0, 0))


_STATE_SHAPE = jax.ShapeDtypeStruct((BATCH, DEPTH, 2, HEADS, HEAD_DIM, HEAD_DIM), F32)


def _without_arg(kernel, pos):
    def wrapped(*refs):
        return kernel(*refs[:pos], *refs[pos + 1:])
    return wrapped


def _cparams(sem):
    return pltpu.CompilerParams(dimension_semantics=sem, vmem_limit_bytes=VMEM_LIMIT)


def _mod_kernel(c_ref, w_ref, b_ref, o_ref):
    s = _silu(c_ref[...])
    hi, lo = _split2(s)
    w = w_ref[...].astype(BF16)
    o_ref[...] = _dg(hi, w, _NN) + _dg(lo, w, _NN) + b_ref[...]


def _mod_call(cvec, w_mod, b_mod):
    tn = 2048
    n = 6 * D_MODEL
    return pl.pallas_call(
        _mod_kernel,
        grid=(DEPTH, n // tn),
        in_specs=[pl.BlockSpec((MOD_ROWS, D_MODEL), lambda l, j: (0, 0)),
                  pl.BlockSpec((None, D_MODEL, tn), lambda l, j: (l, 0, j)),
                  pl.BlockSpec((None, 1, tn), lambda l, j: (l, 0, j))],
        out_specs=pl.BlockSpec((None, MOD_ROWS, tn), lambda l, j: (l, 0, j)),
        out_shape=jax.ShapeDtypeStruct((DEPTH, MOD_ROWS, n), F32),
        compiler_params=_cparams(("arbitrary", "arbitrary")),
        name="modulation",
    )(cvec, w_mod, b_mod.reshape(DEPTH, 1, n))


def _adaln(x, g, shift, scale):
    y = x * lax.rsqrt(jnp.mean(x * x, axis=-1, keepdims=True) + EPS) * g
    return y * (1.0 + scale) + shift


def _residual_specs(x, tm, idx):
    if not isinstance(x, tuple):
        return [pl.BlockSpec((tm, D_MODEL), lambda *g: (idx(*g), 0))], [x]
    n_ctx = N_CTX_TOK // tm
    return ([pl.BlockSpec((tm, D_MODEL), lambda *g: (jnp.minimum(idx(*g), n_ctx - 1), 0)),
             pl.BlockSpec((tm, D_MODEL), lambda *g: (jnp.maximum(idx(*g) - n_ctx, 0), 0))], list(x))


def _residual_tile(refs, tile, tm):
    if len(refs) == 1:
        return refs[0][...]
    return jnp.where(tile < N_CTX_TOK // tm, refs[0][...], refs[1][...])


def _normmod_kernel(*refs, n_x):
    x_refs, (g_ref, sh_ref, sc_ref, o_ref) = refs[:n_x], refs[n_x:]
    x = _residual_tile(x_refs, pl.program_id(0), o_ref.shape[0])
    o_ref[...] = _adaln(x, g_ref[...], sh_ref[...], sc_ref[...]).astype(o_ref.dtype)


def _normmod_call(x, g, mod3, layer, k_shift, k_scale):
    tm = 512
    def mod_spec(k):
        return pl.BlockSpec((None, 1, D_MODEL),
                            lambda i: (layer * MOD_ROWS + _mod_row_of_tile(i, tm), 0, k))
    x_specs, x_args = _residual_specs(x, tm, lambda i: i)
    return pl.pallas_call(
        functools.partial(_normmod_kernel, n_x=len(x_args)),
        grid=(N_TOK // tm,),
        in_specs=x_specs + [pl.BlockSpec((1, D_MODEL), lambda i: (0, 0)), mod_spec(k_shift), mod_spec(k_scale)],
        out_specs=pl.BlockSpec((tm, D_MODEL), lambda i: (i, 0)),
        out_shape=jax.ShapeDtypeStruct((N_TOK, D_MODEL), BF16),
        compiler_params=_cparams(("arbitrary",)),
        name="norm_mod",
    )(*x_args, g.reshape(1, D_MODEL), mod3, mod3)


def _final_norm_kernel(x_ref, g_ref, o_ref):
    x = x_ref[...]
    o_ref[...] = x * lax.rsqrt(jnp.mean(x * x, axis=-1, keepdims=True) + EPS) * g_ref[...]


def _final_norm_call(x, g, row0, n_rows):
    tm = 512
    return pl.pallas_call(
        _final_norm_kernel,
        grid=(n_rows // tm,),
        in_specs=[pl.BlockSpec((tm, D_MODEL), lambda i: (row0 // tm + i, 0)),
                  pl.BlockSpec((1, D_MODEL), lambda i: (0, 0))],
        out_specs=pl.BlockSpec((tm, D_MODEL), lambda i: (i, 0)),
        out_shape=jax.ShapeDtypeStruct((n_rows, D_MODEL), F32),
        compiler_params=_cparams(("arbitrary",)),
        name="final_norm",
    )(x, g.reshape(1, D_MODEL))


def _mm_kernel(x_ref, w_ref, o_ref, wbf_ref, *, act, w_transposed):
    @pl.when(pl.program_id(1) == 0)
    def _cast_weights():
        wbf_ref[...] = w_ref[...].reshape(wbf_ref.shape).astype(BF16)

    acc = lax.dot_general(x_ref[...], wbf_ref[...], _NT if w_transposed else _NN, preferred_element_type=F32)
    if act == "relu2":
        acc = jnp.square(jnp.maximum(acc, 0.0))
    if len(o_ref.shape) == 3:
        for c in range(o_ref.shape[0]):
            o_ref[c] = acc[:, c * LANES:(c + 1) * LANES].astype(o_ref.dtype)
    else:
        o_ref[...] = acc.astype(o_ref.dtype)


def _mm_call(x, w_all, layer, *, col0, n, act, out_dtype, tm, tn, name, w_transposed=False,
             lane_tile_major=False):
    m, k = x.shape
    assert n % tn == 0 and m % tm == 0
    if w_transposed:
        assert col0 % 8 == 0
        w_spec = pl.BlockSpec((pl.Element(1), pl.Element(tn), pl.Element(k)),
                              lambda j, i: (layer, pl.multiple_of(col0 + j * tn, 8), 0))
        w_scratch = pltpu.VMEM((tn, k), BF16)
    else:
        assert col0 % tn == 0
        w_spec = pl.BlockSpec((None, k, tn), lambda j, i: (layer, 0, col0 // tn + j))
        w_scratch = pltpu.VMEM((k, tn), BF16)
    if lane_tile_major:
        out_spec = pl.BlockSpec((tn // LANES, tm, LANES), lambda j, i: (j, i, 0))
        out_shape = jax.ShapeDtypeStruct((n // LANES, m, LANES), out_dtype)
    else:
        out_spec = pl.BlockSpec((tm, tn), lambda j, i: (i, j))
        out_shape = jax.ShapeDtypeStruct((m, n), out_dtype)
    return pl.pallas_call(
        functools.partial(_mm_kernel, act=act, w_transposed=w_transposed),
        grid=(n // tn, m // tm),
        in_specs=[pl.BlockSpec((tm, k), lambda j, i: (i, 0)), w_spec],
        out_specs=out_spec,
        out_shape=out_shape,
        scratch_shapes=[w_scratch],
        compiler_params=_cparams(("arbitrary", "arbitrary")),
        name=name,
    )(x, w_all)


def _ffn_up_kernel(x_ref, w_ref, w2_ref, o_ref, w2bf_ref, wbf_ref):
    @pl.when(pl.program_id(1) == 0)
    def _cast_weights():
        wbf_ref[...] = w_ref[...].astype(BF16)
        w2bf_ref[...] = w2_ref[...].astype(BF16)

    acc = jnp.dot(x_ref[...], wbf_ref[...], preferred_element_type=F32)
    o_ref[...] = jnp.square(jnp.maximum(acc, 0.0)).astype(o_ref.dtype)


def _ffn_up_call(h2, w_ff1, w_ff2, layer):
    tm, tn = 2048, 512
    n_tiles = D_FF // tn
    slab = D_FF // n_tiles
    return pl.pallas_call(
        _ffn_up_kernel,
        grid=(n_tiles, N_TOK // tm),
        in_specs=[pl.BlockSpec((tm, D_MODEL), lambda j, i: (i, 0)),
                  pl.BlockSpec((None, D_MODEL, tn), lambda j, i: (layer, 0, j)),
                  pl.BlockSpec((None, slab, D_MODEL), lambda j, i: (layer, j, 0))],
        out_specs=[pl.BlockSpec((tm, tn), lambda j, i: (i, j)),
                   pl.BlockSpec((slab, D_MODEL), lambda j, i: (j, 0))],
        out_shape=[jax.ShapeDtypeStruct((N_TOK, D_FF), BF16),
                   jax.ShapeDtypeStruct((D_FF, D_MODEL), BF16)],
        scratch_shapes=[pltpu.VMEM((D_MODEL, tn), BF16)],
        compiler_params=_cparams(("arbitrary", "arbitrary")),
        name="ffn_up",
    )(h2, w_ff1, w_ff2)


def _mm_bf16_kernel(x_ref, w_ref, r_ref, g_ref, o_ref):
    acc = jnp.dot(x_ref[...], w_ref[...], preferred_element_type=F32)
    o_ref[...] = r_ref[...] + g_ref[...] * acc


def _mm_resid_call(x, w, resid, mod3, layer, k_gate, *, tm, tn, name):
    m, k = x.shape
    n = w.shape[1]
    per_tile = D_MODEL // tn
    return pl.pallas_call(
        _mm_bf16_kernel,
        grid=(m // tm, n // tn),
        in_specs=[pl.BlockSpec((tm, k), lambda i, j: (i, 0)),
                  pl.BlockSpec((k, tn), lambda i, j: (0, j)),
                  pl.BlockSpec((tm, tn), lambda i, j: (i, j)),
                  pl.BlockSpec((None, 1, tn),
                               lambda i, j: (layer * MOD_ROWS + _mod_row_of_tile(i, tm), 0,
                                             k_gate * per_tile + j))],
        out_specs=pl.BlockSpec((tm, tn), lambda i, j: (i, j)),
        out_shape=jax.ShapeDtypeStruct((m, n), F32),
        compiler_params=_cparams(("arbitrary", "arbitrary")),
        name=name,
    )(x, w, resid, mod3)


def _outproj_norm_kernel(*refs, n_x):
    m_ref, w_ref = refs[:2]
    x_refs = refs[2:2 + n_x]
    gate_ref, g_ref, sh_ref, sc_ref, xo_ref, h_ref, wbf_ref = refs[2 + n_x:]

    @pl.when(pl.program_id(0) == 0)
    def _cast_weights():
        wbf_ref[...] = w_ref[...].astype(BF16)

    acc = jnp.dot(m_ref[...], wbf_ref[...], preferred_element_type=F32)
    xn = _residual_tile(x_refs, pl.program_id(0), xo_ref.shape[0]) + gate_ref[...] * acc
    xo_ref[...] = xn
    h_ref[...] = _adaln(xn, g_ref[...], sh_ref[...], sc_ref[...]).astype(h_ref.dtype)


def _outproj_norm_call(merged, w_out, x, norm_g, mod3, layer):
    tm = 256 if isinstance(x, tuple) else 512
    def mod_spec(k):
        return pl.BlockSpec((None, 1, D_MODEL),
                            lambda i: (layer * MOD_ROWS + _mod_row_of_tile(i, tm), 0, k))
    row = pl.BlockSpec((tm, D_MODEL), lambda i: (i, 0))
    x_specs, x_args = _residual_specs(x, tm, lambda i: i)
    return pl.pallas_call(
        functools.partial(_outproj_norm_kernel, n_x=len(x_args)),
        grid=(N_TOK // tm,),
        in_specs=[row,
                  pl.BlockSpec((None, D_MODEL, D_MODEL), lambda i: (layer, 0, 0), pipeline_mode=pl.Buffered(1))]
                 + x_specs + [mod_spec(2), pl.BlockSpec((1, D_MODEL), lambda i: (0, 0)), mod_spec(3), mod_spec(4)],
        out_specs=[row, row],
        out_shape=[jax.ShapeDtypeStruct((N_TOK, D_MODEL), F32),
                   jax.ShapeDtypeStruct((N_TOK, D_MODEL), BF16)],
        scratch_shapes=[pltpu.VMEM((D_MODEL, D_MODEL), BF16)],
        compiler_params=_cparams(("arbitrary",)),
        name="out_proj_norm",
    )(merged, w_out, *x_args, mod3, norm_g.reshape(1, D_MODEL), mod3, mod3)


def _ffn_down_norm_kernel(x_ref, w_ref, r_ref, gate_ref, g_ref, sh_ref, sc_ref, o_ref, h_ref, row_ref):
    j = pl.program_id(1)
    tn = o_ref.shape[1]
    acc = jnp.dot(x_ref[...], w_ref[...], preferred_element_type=F32)
    xn = r_ref[...] + gate_ref[...] * acc
    o_ref[...] = xn
    row_ref[:, pl.ds(pl.multiple_of(j * tn, tn), tn)] = xn

    @pl.when(j == pl.num_programs(1) - 1)
    def _next_layer_norm():
        h_ref[...] = _adaln(row_ref[...], g_ref[...], sh_ref[...], sc_ref[...]).astype(h_ref.dtype)


def _ffn_down_norm_call(up, w_bf, resid, mod3, layer, next_norm_g):
    tm, tn = 512, 512
    m, k = up.shape
    per_tile = D_MODEL // tn
    def mod_row(i):
        return _mod_row_of_tile(i, tm)
    def next_mod(kk):
        return pl.BlockSpec((None, 1, D_MODEL), lambda i, j: ((layer + 1) * MOD_ROWS + mod_row(i), 0, kk))
    return pl.pallas_call(
        _ffn_down_norm_kernel,
        grid=(m // tm, D_MODEL // tn),
        in_specs=[pl.BlockSpec((tm, k), lambda i, j: (i, 0)),
                  pl.BlockSpec((k, tn), lambda i, j: (0, j)),
                  pl.BlockSpec((tm, tn), lambda i, j: (i, j)),
                  pl.BlockSpec((None, 1, tn), lambda i, j: (layer * MOD_ROWS + mod_row(i), 0, 5 * per_tile + j)),
                  pl.BlockSpec((1, D_MODEL), lambda i, j: (0, 0)), next_mod(0), next_mod(1)],
        out_specs=[pl.BlockSpec((tm, tn), lambda i, j: (i, j)),
                   pl.BlockSpec((tm, D_MODEL), lambda i, j: (i, 0))],
        out_shape=[jax.ShapeDtypeStruct((m, D_MODEL), F32),
                   jax.ShapeDtypeStruct((m, D_MODEL), BF16)],
        scratch_shapes=[pltpu.VMEM((tm, D_MODEL), F32)],
        compiler_params=_cparams(("arbitrary", "arbitrary")),
        name="ffn_down_norm",
    )(up, w_bf, resid, mod3, next_norm_g.reshape(1, D_MODEL), mod3, mod3)


def _gated_merge_kernel(h_ref, wga_ref, wgb_ref, wgc_ref, oa_ref, ob_ref, oc_ref, wa_ref, wb_ref, wc_ref,
                        o_ref, wg_bf, wbr_bf):
    @pl.when(pl.program_id(1) == 0)
    def _cast_weights():
        for t, ref in enumerate((wga_ref, wgb_ref, wgc_ref)):
            wg_bf[t] = ref[...].reshape(wg_bf.shape[1:]).astype(BF16)
        for t, ref in enumerate((wa_ref, wb_ref, wc_ref)):
            wbr_bf[t] = ref[...].astype(BF16)

    def rows_of(ref):
        return jnp.concatenate([ref[t] for t in range(ref.shape[0])], axis=1)

    h = h_ref[...]
    acc = None
    for t, br_ref in enumerate((oa_ref, ob_ref, oc_ref)):
        gate = _sigmoid(lax.dot_general(h, wg_bf[t], _NT, preferred_element_type=F32))
        term = gate * jnp.dot(rows_of(br_ref), wbr_bf[t], preferred_element_type=F32)
        acc = term if acc is None else acc + term
    o_ref[...] = acc.astype(o_ref.dtype)


def _gated_merge_call(h, w_in_t, o_a, o_b, o_c, w_a, w_b, w_c, layer):
    tm, tn = 1024, 256
    def gate_w(b):
        return pl.BlockSpec((pl.Element(1), pl.Element(tn), pl.Element(D_MODEL)),
                            lambda j, i: (layer, pl.multiple_of(OFF_GATES + b * D_MODEL + j * tn, 8), 0))
    br = pl.BlockSpec((HEADS, tm, HEAD_DIM), lambda j, i: (0, i, 0))
    wt = pl.BlockSpec((None, HG_F, tn), lambda j, i: (layer, 0, j))
    return pl.pallas_call(
        _gated_merge_kernel,
        grid=(D_MODEL // tn, N_TOK // tm),
        in_specs=[pl.BlockSpec((tm, D_MODEL), lambda j, i: (i, 0)), gate_w(0), gate_w(1), gate_w(2),
                  br, br, br, wt, wt, wt],
        out_specs=pl.BlockSpec((tm, tn), lambda j, i: (i, j)),
        out_shape=jax.ShapeDtypeStruct((N_TOK, D_MODEL), BF16),
        scratch_shapes=[pltpu.VMEM((3, tn, D_MODEL), BF16), pltpu.VMEM((3, HG_F, tn), BF16)],
        compiler_params=_cparams(("arbitrary", "arbitrary")),
        name="gated_merge",
    )(h, w_in_t, w_in_t, w_in_t, o_a, o_b, o_c, w_a, w_b, w_c)


def _head_rmsnorm_gate(o, g, og):
    y = o * lax.rsqrt(jnp.mean(o * o, axis=-1, keepdims=True) + EPS) * g
    return y * _silu(og)


def _hgrn_kernel(q_ref, i_ref, og_ref, ff_ref, fb_ref, lb_ref, g_ref, s0_ref, o_ref, s_ref,
                 acc_ref, b_ref, stc_ref, km_ref, *, state_slot):
    is_ctx = _is_ctx_step(pl.program_id(1))
    _zero_other_slots(s_ref, state_slot)
    nblk = UNIT // HG_BLOCK
    nsub = HG_BLOCK // HG_SUB
    r = lax.broadcasted_iota(jnp.int32, (HG_BLOCK, HG_BLOCK), 0)
    c = lax.broadcasted_iota(jnp.int32, (HG_BLOCK, HG_BLOCK), 1)
    same = (r // HG_SUB) == (c // HG_SUB)
    row_sub = lax.broadcasted_iota(jnp.int32, (HG_BLOCK, HEAD_DIM), 0) // HG_SUB
    tri = (jnp.logical_and(same, c <= r), jnp.logical_and(same, c >= r))
    tri_bf = [jnp.where(t, 1.0, 0.0).astype(BF16) for t in tri]
    z_refs = (ff_ref, fb_ref)
    inst = [(hh, blk, d) for hh in range(HG_HEADS_PER_STEP) for blk in range(nblk) for d in range(2)]

    def expand(x):
        return jnp.broadcast_to(x[:, None, :], (nsub, HG_SUB, HEAD_DIM)).reshape(HG_BLOCK, HEAD_DIM)

    def blk_rows(blk):
        return slice(blk * HG_BLOCK, (blk + 1) * HG_BLOCK)

    def head_cols(hh):
        return slice(hh * HEAD_DIM, (hh + 1) * HEAD_DIM)

    qs = {(hh, blk): _silu(q_ref[hh, blk_rows(blk), :]) for hh in range(HG_HEADS_PER_STEP) for blk in range(nblk)}
    vs = {(hh, blk): i_ref[hh, blk_rows(blk), :] for hh in range(HG_HEADS_PER_STEP) for blk in range(nblk)}
    ks, lfs = [], []
    for hh, blk, d in inst:
        lb = lb_ref[d:d + 1, head_cols(hh)]
        f = lb + (1.0 - lb) * _sigmoid(z_refs[d][hh, blk_rows(blk), :])
        lfs.append(jnp.log(f))
        ks.append(1.0 - f)
    bs = [_dot01_2(tri_bf[d], lf) for (hh, blk, d), lf in zip(inst, lfs)]
    tots, qts, kts, qds = [], [], [], []
    for n, (hh, blk, d) in enumerate(inst):
        b = bs[n]
        b_ref[n] = b
        tot = b_ref[n, pl.ds((HG_SUB - 1) if d == 0 else 0, nsub, stride=HG_SUB), :]
        mid_f = expand(b_ref[n, pl.ds(HG_SUB // 2, nsub, stride=HG_SUB), :])
        tots.append(tot)
        qts.append(qs[hh, blk] * jnp.exp(b - mid_f))
        kts.append(ks[n] * jnp.exp(mid_f - b))
        qds.append((qs[hh, blk] * jnp.exp(b)).astype(BF16))
        kd = ks[n] * jnp.exp(expand(tot) - b)
        for s in range(nsub):
            km_ref[n, :, s * HEAD_DIM:(s + 1) * HEAD_DIM] = jnp.where(row_sub == s, kd, 0.0).astype(BF16)
    scs = [jnp.where(tri[d], _dot1(qt, kt, _NT), 0.0) for (hh, blk, d), qt, kt in zip(inst, qts, kts)]
    uts = [_dg(vs[hh, blk].astype(BF16), km_ref[n], _TN) for n, (hh, blk, d) in enumerate(inst)]
    outs = [_dot1(sc, vs[hh, blk]) for (hh, blk, d), sc in zip(inst, scs)]
    for hh in range(HG_HEADS_PER_STEP):
        for d in range(2):
            st = s0_ref[d, hh].T
            for blk in (range(nblk) if d == 0 else range(nblk - 1, -1, -1)):
                n = inst.index((hh, blk, d))
                st = jnp.where(is_ctx, 0.0, st)
                for s in (range(nsub) if d == 0 else range(nsub - 1, -1, -1)):
                    stc_ref[n, s] = st.T.astype(BF16)
                    st = st * jnp.exp(tots[n][s:s + 1, :]) + uts[n][:, s * HEAD_DIM:(s + 1) * HEAD_DIM]
                s_ref[blk, state_slot, d, hh] = st.T
    for n, (hh, blk, d) in enumerate(inst):
        o_int = [_dg(qds[n][s * HG_SUB:(s + 1) * HG_SUB], stc_ref[n, s], _NN) for s in range(nsub)]
        acc_ref[d, hh, blk_rows(blk), :] = outs[n] + jnp.concatenate(o_int, axis=0)
    for hh in range(HG_HEADS_PER_STEP):
        o_ref[hh] = _head_rmsnorm_gate(acc_ref[0, hh] + acc_ref[1, hh], g_ref[...], og_ref[hh]).astype(o_ref.dtype)


def _hgrn_call(z, lb, onorm_g, state, layer, new_states):
    nh = HG_HEADS_PER_STEP
    def col(off):
        return pl.BlockSpec((nh, UNIT, HEAD_DIM), lambda h, g: (off // HEAD_DIM // nh + h, _scan_unit(g), 0))
    in_specs = [col(OFF_HQ), col(OFF_HI), col(OFF_HG), col(OFF_HFF), col(OFF_HFB),
                pl.BlockSpec((2, nh * HEAD_DIM), lambda h, g: (0, h)),
                pl.BlockSpec((1, HEAD_DIM), lambda h, g: (0, 0)),
                _state_in_spec(layer, nh)]
    args = [z, z, z, z, z, lb, onorm_g.reshape(1, HEAD_DIM), state]
    first = new_states is None
    kernel, aliases = functools.partial(_hgrn_kernel, state_slot=layer if first else 0), {}
    if not first:
        kernel, aliases = _without_arg(kernel, len(args)), {len(args): 1}
        in_specs.append(pl.BlockSpec(memory_space=pl.ANY))
        args.append(new_states)
    n_inst = nh * HG_INST
    return pl.pallas_call(
        kernel,
        grid=(HEADS // nh, N_UNITS),
        in_specs=in_specs,
        out_specs=[pl.BlockSpec((nh, UNIT, HEAD_DIM), lambda h, g: (h, _scan_unit(g), 0)),
                   _state_out_spec(layer, first, nh)],
        out_shape=[jax.ShapeDtypeStruct((HEADS, N_TOK, HEAD_DIM), BF16), _STATE_SHAPE],
        input_output_aliases=aliases,
        scratch_shapes=[pltpu.VMEM((2, nh, UNIT, HEAD_DIM), F32),
                        pltpu.VMEM((n_inst, HG_BLOCK, HEAD_DIM), F32),
                        pltpu.VMEM((n_inst, HG_BLOCK // HG_SUB, HEAD_DIM, HEAD_DIM), BF16),
                        pltpu.VMEM((n_inst, HG_BLOCK, (HG_BLOCK // HG_SUB) * HEAD_DIM), BF16)],
        compiler_params=_cparams(("arbitrary", "arbitrary")),
        name="hgrn2_scan",
    )(*args)


def _gmlp_kernel(u_ref, v_ref, vn_ref, ws_ref, bs_ref, o_ref):
    tm = u_ref.shape[1]
    for g in range(CM_GROUPS):
        cols = slice(g * HEAD_DIM, (g + 1) * HEAD_DIM)
        vg = _gelu(v_ref[g])
        vg = vg * lax.rsqrt(jnp.mean(vg * vg, axis=-1, keepdims=True) + EPS) * vn_ref[:, cols]
        ug = _gelu(u_ref[g])
        w = ws_ref[g]
        bias = bs_ref[:, g:g + 1]
        for ch in range(tm // CM_CHUNK):
            rows = slice(ch * CM_CHUNK, (ch + 1) * CM_CHUNK)
            s = _dot3(w, vg[rows]) + bias
            o_ref[g, rows, :] = (ug[rows] * s).astype(o_ref.dtype)


def _gmlp_call(z, vnorm_g, ws, bs_t):
    tm = 512
    return pl.pallas_call(
        _gmlp_kernel,
        grid=(N_TOK // tm,),
        in_specs=[pl.BlockSpec((CM_GROUPS, tm, HEAD_DIM), lambda i: (OFF_CU // CM_W, i, 0)),
                  pl.BlockSpec((CM_GROUPS, tm, HEAD_DIM), lambda i: (OFF_CV // CM_W, i, 0)),
                  pl.BlockSpec((1, CM_W), lambda i: (0, 0)),
                  pl.BlockSpec((CM_GROUPS, CM_CHUNK, CM_CHUNK), lambda i: (0, 0, 0)),
                  pl.BlockSpec((CM_CHUNK, CM_GROUPS), lambda i: (0, 0))],
        out_specs=pl.BlockSpec((CM_GROUPS, tm, HEAD_DIM), lambda i: (0, i, 0)),
        out_shape=jax.ShapeDtypeStruct((CM_GROUPS, N_TOK, HEAD_DIM), BF16),
        compiler_params=_cparams(("arbitrary",)),
        name="chunk_gmlp",
    )(z, z, vnorm_g.reshape(1, CM_W), ws, bs_t)


GP_BLOCK = 256


def _gdn_gates_kernel(ab_ref, alog_ref, dt_ref, col_ref, rowt_ref):
    ab = ab_ref[...]
    lane = lax.broadcasted_iota(jnp.int32, ab.shape, 1)
    g = jnp.where(lane < 2 * HEADS, -jnp.exp(alog_ref[...]) * _softplus(ab + dt_ref[...]), 0.0)
    r = lax.broadcasted_iota(jnp.int32, (GP_BLOCK, GP_BLOCK), 0)
    c = lax.broadcasted_iota(jnp.int32, (GP_BLOCK, GP_BLOCK), 1)
    same = (r // GDN_CHUNK) == (c // GDN_CHUNK)
    tri_f = jnp.where(jnp.logical_and(same, c <= r), 1.0, 0.0).astype(BF16)
    tri_b = jnp.where(jnp.logical_and(same, c >= r), 1.0, 0.0).astype(BF16)
    cf = _dot01(tri_f, g)
    cb = _dot01(tri_b, g)
    col = jnp.where(lane < HEADS, cf, jnp.where(lane < 2 * HEADS, cb, _sigmoid(ab)))
    col_ref[...] = col
    rowt_ref[...] = col.T


def _gdn_gates_call(ab, a_log, dt_bias):
    pad = lambda t: jnp.pad(t.reshape(1, 2 * HEADS), ((0, 0), (0, LANES - 2 * HEADS)))
    return pl.pallas_call(
        _gdn_gates_kernel,
        grid=(N_TOK // GP_BLOCK,),
        in_specs=[pl.BlockSpec((GP_BLOCK, LANES), lambda i: (i, 0)),
                  pl.BlockSpec((1, LANES), lambda i: (0, 0)),
                  pl.BlockSpec((1, LANES), lambda i: (0, 0))],
        out_specs=[pl.BlockSpec((GP_BLOCK, LANES), lambda i: (i, 0)),
                   pl.BlockSpec((LANES, GP_BLOCK), lambda i: (0, i))],
        out_shape=[jax.ShapeDtypeStruct((N_TOK, LANES), F32),
                   jax.ShapeDtypeStruct((LANES, N_TOK), F32)],
        compiler_params=_cparams(("arbitrary",)),
        name="gdn_gates",
    )(ab, pad(a_log), pad(dt_bias))


CONV_PAD = 72
GDN_AQ = HEAD_DIM + GDN_CHUNK
GDN_HEADS_PER_STEP = 2
GDN_STEP = 16 * GDN_CHUNK


def _gdn_kernel(qr_ref, kr_ref, vr_ref, og_ref, col_ref, rowt_ref, cwq_ref, cwk_ref, cwv_ref,
                g_ref, s0_ref, o_ref, s_ref,
                xp_ref, q_s, k_s, v_s, o_s, b_s, aq_s, *, state_slot):
    is_ctx = _is_ctx_step(pl.program_id(1))
    _zero_other_slots(s_ref, state_slot)
    heads = [GDN_HEADS_PER_STEP * pl.program_id(0) + hh for hh in range(GDN_HEADS_PER_STEP)]
    n_chunks = UNIT // GDN_CHUNK
    chunks_per_seq = SEQ // GDN_CHUNK

    t = lax.broadcasted_iota(jnp.int32, (UNIT, 1), 0)
    period = jnp.where(is_ctx, SEQ, GRID_W)
    pos = jnp.bitwise_and(t, period - 1)
    ok_left = pos != 0
    ok_right = pos != period - 1
    zeros_pad = jnp.zeros((CONV_PAD, HEAD_DIM), F32)
    xp_ref[0:CONV_PAD, :] = zeros_pad
    xp_ref[CONV_PAD + UNIT:CONV_PAD + UNIT + CONV_PAD, :] = zeros_pad

    def conv_silu(x_ref, w_ref, hh):
        xp_ref[CONV_PAD:CONV_PAD + UNIT, :] = x_ref[hh]
        acc = jnp.zeros((UNIT, HEAD_DIM), F32)
        for i in range(CONV_K):
            for j in range(CONV_K):
                w = w_ref[CONV_K * i + j:CONV_K * i + j + 1, hh * HEAD_DIM:(hh + 1) * HEAD_DIM]
                if i != CONV_K // 2:
                    w = jnp.where(is_ctx, 0.0, w)
                start = CONV_PAD + (i - 1) * GRID_W + (j - 1)
                xs = xp_ref[start:start + UNIT, :]
                if j == 0:
                    xs = jnp.where(ok_left, xs, 0.0)
                elif j == CONV_K - 1:
                    xs = jnp.where(ok_right, xs, 0.0)
                acc = acc + xs * w
        return _silu(acc)

    def l2norm(x):
        return x * lax.rsqrt(jnp.sum(x * x, axis=-1, keepdims=True) + EPS)

    rr = lax.broadcasted_iota(jnp.int32, (GDN_CHUNK, GDN_CHUNK), 0)
    cc = lax.broadcasted_iota(jnp.int32, (GDN_CHUNK, GDN_CHUNK), 1)
    eye = jnp.where(rr == cc, 1.0, 0.0)
    same_blk = [(rr // b) == (cc // b) for b in (8, 16, 32, 64)]
    lane = lax.broadcasted_iota(jnp.int32, (GDN_STEP, LANES), 1)
    sub8 = lax.broadcasted_iota(jnp.int32, (HEADS, GDN_STEP), 0)

    def pick(x, j):
        return jnp.sum(jnp.where(lane[:x.shape[0]] == j, x, 0.0), axis=-1, keepdims=True)

    incl = (cc <= rr, cc >= rr)
    strict = (cc < rr, cc > rr)
    off_masks = [jnp.logical_and(same_blk[lvl], jnp.logical_not(same_blk[lvl - 1]))
                 for lvl in range(1, len(same_blk))]

    def phase1(p, hh):
        head = heads[hh]
        rows = pl.ds(pl.multiple_of(p * GDN_STEP, GDN_STEP), GDN_STEP)
        q2, k2, v2 = q_s[rows, :], k_s[rows, :], v_s[rows, :]
        col = col_ref[rows, :]
        gcols = [pick(col, d * HEADS + head) for d in range(2)]
        betas = [pick(col, (2 + d) * HEADS + head) for d in range(2)]
        grows = [jnp.sum(jnp.where(sub8 == head, rowt_ref[d * HEADS:(d + 1) * HEADS, rows], 0.0),
                         axis=0, keepdims=True) for d in range(2)]
        inst = []
        for half in range(GDN_STEP // GDN_CHUNK):
            sl = slice(half * GDN_CHUNK, (half + 1) * GDN_CHUNK)
            qc, kc, vc = q2[sl], k2[sl], v2[sl]
            kk = _dot1(kc, kc, _NT)
            qk = _dot1(qc, kc, _NT)
            for d in range(2):
                gcol, beta = gcols[d][sl], betas[d][sl]
                decay = jnp.where(incl[d], jnp.exp(gcol - grows[d][:, sl]), 0.0)
                m = jnp.where(strict[d], beta * kk * decay, 0.0)
                inst.append((half, d, qc, kc, vc, qk, gcol, beta, decay, m))
        ms = [t[-1] for t in inst]
        pws = [jnp.where(same_blk[0], -m, 0.0) for m in ms]
        tinvs = [eye + pw for pw in pws]
        for _ in range(2):
            pws = [_dot_inv(pw, pw) for pw in pws]
            tinvs = [t + _dot_inv(t, pw) for t, pw in zip(tinvs, pws)]
        for mask in off_masks:
            tmp = [_dot_inv(jnp.where(mask, m, 0.0), t) for m, t in zip(ms, tinvs)]
            tinvs = [t - _dot_inv(t, x) for t, x in zip(tinvs, tmp)]
        uws = [_dot1(tinv, jnp.concatenate([vc * beta, kc * (beta * jnp.exp(gcol))], axis=1)).astype(BF16)
               for (half, d, qc, kc, vc, qk, gcol, beta, decay, m), tinv in zip(inst, tinvs)]
        bas, ows = [], []
        for (half, d, qc, kc, vc, qk, gcol, beta, decay, m), uw in zip(inst, uws):
            glast = gcol[GDN_CHUNK - 1:GDN_CHUNK] if d == 0 else gcol[0:1]
            bas.append(_dg((kc * jnp.exp(glast - gcol)).astype(BF16), uw, _TN))
            ows.append(_dg((qk * decay).astype(BF16), uw, _NN))
        for (half, d, qc, kc, vc, qk, gcol, beta, decay, m), ba, ow in zip(inst, bas, ows):
            chunk0 = p * GDN_STEP + half * GDN_CHUNK
            o_s[hh, d, pl.ds(pl.multiple_of(chunk0, GDN_CHUNK), GDN_CHUNK), :] = ow[:, :HEAD_DIM]
            b_s[hh, d, pl.ds(pl.multiple_of(2 * chunk0, HEAD_DIM), HEAD_DIM), :] = ba[:, :HEAD_DIM]
            aq_s[hh, d, pl.ds(pl.multiple_of(3 * chunk0, GDN_AQ), GDN_AQ), :] = jnp.concatenate(
                [ba[:, HEAD_DIM:], qc * jnp.exp(gcol) - ow[:, HEAD_DIM:]], axis=0).astype(BF16)

    for hh in range(GDN_HEADS_PER_STEP):
        q_s[...] = l2norm(conv_silu(qr_ref, cwq_ref, hh)) * (HEAD_DIM ** -0.5)
        k_s[...] = l2norm(conv_silu(kr_ref, cwk_ref, hh))
        v_s[...] = conv_silu(vr_ref, cwv_ref, hh)
        for p in range(UNIT // GDN_STEP):
            phase1(p, hh)

    lane1 = lax.broadcasted_iota(jnp.int32, (1, LANES), 1)

    def advance(chains, states):
        rs = [_dg(aq_s[hh, d, pl.ds(pl.multiple_of(ch * GDN_AQ, GDN_AQ), GDN_AQ), :], s.astype(BF16), _NN)
              for (hh, d, ch), s in zip(chains, states)]
        new = []
        for (hh, d, ch), s, r in zip(chains, states, rs):
            tile0 = pl.multiple_of(ch * GDN_CHUNK + (GDN_CHUNK - 8 if d == 0 else 0), 8)
            last = col_ref[pl.ds(tile0, 8), :]
            last = last[7:8] if d == 0 else last[0:1]
            glast = jnp.sum(jnp.where(lane1 == d * HEADS + heads[hh], last, 0.0), axis=-1, keepdims=True)
            b = b_s[hh, d, pl.ds(pl.multiple_of(ch * HEAD_DIM, HEAD_DIM), HEAD_DIM), :]
            new.append(s * jnp.exp(glast) + (b - r[:HEAD_DIM]))
            rows = pl.ds(pl.multiple_of(ch * GDN_CHUNK, GDN_CHUNK), GDN_CHUNK)
            o_s[hh, d, rows, :] = o_s[hh, d, rows, :] + r[HEAD_DIM:]
        return tuple(new)

    @pl.when(is_ctx)
    def _context_unit():
        def body(i, states):
            chains = [(hh, d, sq * chunks_per_seq + (i if d == 0 else chunks_per_seq - 1 - i))
                      for hh in range(GDN_HEADS_PER_STEP) for sq in range(SEQ_PER_UNIT) for d in range(2)]
            return advance(chains, states)
        zero = jnp.zeros((HEAD_DIM, HEAD_DIM), F32)
        final = lax.fori_loop(0, chunks_per_seq, body, (zero,) * (2 * SEQ_PER_UNIT * GDN_HEADS_PER_STEP))
        for hh in range(GDN_HEADS_PER_STEP):
            for sq in range(SEQ_PER_UNIT):
                for d in range(2):
                    s_ref[sq, state_slot, d, hh] = final[(hh * SEQ_PER_UNIT + sq) * 2 + d]

    @pl.when(jnp.logical_not(is_ctx))
    def _latent_unit():
        def body(i, states):
            chains = [(hh, d, i if d == 0 else n_chunks - 1 - i) for hh in range(GDN_HEADS_PER_STEP) for d in range(2)]
            return advance(chains, states)
        init = tuple(s0_ref[d, hh] for hh in range(GDN_HEADS_PER_STEP) for d in range(2))
        final = lax.fori_loop(0, n_chunks, body, init)
        for hh in range(GDN_HEADS_PER_STEP):
            for sq in range(SEQ_PER_UNIT):
                for d in range(2):
                    s_ref[sq, state_slot, d, hh] = final[2 * hh + d]

    for hh in range(GDN_HEADS_PER_STEP):
        o_ref[hh] = _head_rmsnorm_gate(o_s[hh, 0] + o_s[hh, 1], g_ref[...], og_ref[hh]).astype(o_ref.dtype)


def _gdn_call(z, col, rowt, conv_w, onorm_g, state, layer, new_states):
    nh = GDN_HEADS_PER_STEP
    def zcol(off):
        return pl.BlockSpec((nh, UNIT, HEAD_DIM), lambda h, g: (off // HEAD_DIM // nh + h, _scan_unit(g), 0))
    def wcol(part):
        return pl.BlockSpec((CONV_K * CONV_K, nh * HEAD_DIM), lambda h, g: (0, part * HEADS // nh + h))
    scr = lambda *shape: pltpu.VMEM(shape, F32)
    in_specs = [zcol(OFF_GQ), zcol(OFF_GK), zcol(OFF_GV), zcol(OFF_GG),
                pl.BlockSpec((UNIT, LANES), lambda h, g: (_scan_unit(g), 0)),
                pl.BlockSpec((LANES, UNIT), lambda h, g: (0, _scan_unit(g))),
                wcol(0), wcol(1), wcol(2),
                pl.BlockSpec((1, HEAD_DIM), lambda h, g: (0, 0)),
                _state_in_spec(layer, nh)]
    args = [z, z, z, z, col, rowt, conv_w, conv_w, conv_w, onorm_g.reshape(1, HEAD_DIM), state]
    first = new_states is None
    kernel, aliases = functools.partial(_gdn_kernel, state_slot=layer if first else 0), {}
    if not first:
        kernel, aliases = _without_arg(kernel, len(args)), {len(args): 1}
        in_specs.append(pl.BlockSpec(memory_space=pl.ANY))
        args.append(new_states)
    return pl.pallas_call(
        kernel,
        grid=(HEADS // nh, N_UNITS),
        in_specs=in_specs,
        out_specs=[pl.BlockSpec((nh, UNIT, HEAD_DIM), lambda h, g: (h, _scan_unit(g), 0)),
                   _state_out_spec(layer, first, nh)],
        out_shape=[jax.ShapeDtypeStruct((HEADS, N_TOK, HEAD_DIM), BF16), _STATE_SHAPE],
        input_output_aliases=aliases,
        scratch_shapes=[scr(UNIT + 2 * CONV_PAD, HEAD_DIM),
                        scr(UNIT, HEAD_DIM), scr(UNIT, HEAD_DIM), scr(UNIT, HEAD_DIM),
                        scr(nh, 2, UNIT, HEAD_DIM),
                        scr(nh, 2, (UNIT // GDN_CHUNK) * HEAD_DIM, HEAD_DIM),
                        pltpu.VMEM((nh, 2, (UNIT // GDN_CHUNK) * GDN_AQ, HEAD_DIM), BF16)],
        compiler_params=_cparams(("arbitrary", "arbitrary")),
        name="gdn_scan",
    )(*args)


def kernel(x_prompt, x_sample, c, state_hgrn, state_gdn, c_ctx, norm1_g, norm2_g, w_mod, b_mod, w_in, hg_lb, hg_onorm_g, cm_vnorm_g, cm_ws, cm_bs, gdn_conv, gdn_A_log, gdn_dt_bias, gdn_onorm_g, w_br_hg, w_br_cm, w_br_gdn, w_out, w_ff1, w_ff2, final_g):
    x = (x_prompt.reshape(N_CTX_TOK, D_MODEL), x_sample.reshape(N_LAT_TOK, D_MODEL))
    cvec = jnp.concatenate([c_ctx[None, :], c, jnp.zeros((MOD_ROWS - 1 - DEC_BATCH, D_MODEL), F32)], axis=0)
    mod3 = _mod_call(cvec, w_mod, b_mod).reshape(DEPTH * MOD_ROWS, 1, 6 * D_MODEL)

    lb_all = jnp.cumsum(jax.nn.softmax(hg_lb.astype(F32), axis=0), axis=0)
    lb_all = lb_all - lb_all[:1]

    w_in_t = jnp.swapaxes(w_in, 1, 2)

    new_hg = new_gdn = None
    h = _normmod_call(x, norm1_g[0], mod3, 0, 0, 1)
    for l in range(DEPTH):
        z = _mm_call(h, w_in_t, l, col0=0, n=N_MAIN, act=None, out_dtype=F32, tm=1024, tn=1024,
                     w_transposed=True, lane_tile_major=True, name="in_proj")
        ab = _mm_call(h, w_in_t, l, col0=OFF_AB, n=LANES, act=None, out_dtype=F32, tm=1024, tn=LANES,
                      w_transposed=True, name="in_proj_ab")
        o_a, new_hg = _hgrn_call(z, lb_all[l], hg_onorm_g[l], state_hgrn, l, new_hg)
        o_b = _gmlp_call(z, cm_vnorm_g[l], cm_ws[l], cm_bs[l].T)
        col, rowt = _gdn_gates_call(ab, gdn_A_log[l], gdn_dt_bias[l])
        o_c, new_gdn = _gdn_call(z, col, rowt, gdn_conv[l].reshape(CONV_K * CONV_K, 3 * HG_F),
                                 gdn_onorm_g[l], state_gdn, l, new_gdn)

        merged = _gated_merge_call(h, w_in_t, o_a, o_b, o_c, w_br_hg, w_br_cm, w_br_gdn, l)
        x, h2 = _outproj_norm_call(merged, w_out, x, norm2_g[l], mod3, l)
        up, w_ff2_bf = _ffn_up_call(h2, w_ff1, w_ff2, l)
        if l + 1 < DEPTH:
            x, h = _ffn_down_norm_call(up, w_ff2_bf, x, mod3, l, norm1_g[l + 1])
        else:
            x = _mm_resid_call(up, w_ff2_bf, x, mod3, l, 5, tm=512, tn=512, name="ffn_down")

    y_prompt = _final_norm_call(x, final_g, 0, N_CTX_TOK).reshape(BATCH, SEQ, D_MODEL)
    y_sample = _final_norm_call(x, final_g, N_CTX_TOK, N_LAT_TOK).reshape(DEC_BATCH, DEC_SEQ, D_MODEL)
    return (y_prompt, y_sample, new_hg, new_gdn)
```

```python
import functools

import jax
import jax.numpy as jnp
from jax import lax
from jax.experimental import pallas as pl
from jax.experimental.pallas import tpu as pltpu

F32 = jnp.float32
BF16 = jnp.bfloat16

D_MODEL = 2048
BATCH = 16
SEQ = 256
DEPTH = 2
DEC_BATCH = 4
DEC_SEQ = 1024
GRID_W = 64
EPS = 1e-6
D_FF = 4 * D_MODEL
HEADS = 8
HEAD_DIM = 128
HG_F = HEADS * HEAD_DIM
CM_GROUPS = 8
CM_W = CM_GROUPS * HEAD_DIM
CM_CHUNK = 128
GDN_CHUNK = 64
CONV_K = 3

OFF_HQ, OFF_HI, OFF_HG, OFF_HFF, OFF_HFB = 0, 1024, 2048, 3072, 4096
OFF_CU, OFF_CV = 5120, 6144
OFF_GQ, OFF_GK, OFF_GV, OFF_GG = 7168, 8192, 9216, 10240
OFF_AB = 11264
OFF_GATES = 11296
IN_DIM = 17440
N_MAIN = OFF_AB

N_CTX_TOK = BATCH * SEQ
N_LAT_TOK = DEC_BATCH * DEC_SEQ
N_TOK = N_CTX_TOK + N_LAT_TOK
UNIT = DEC_SEQ
N_CTX_UNITS = N_CTX_TOK // UNIT
N_UNITS = N_TOK // UNIT
N_LAT_UNITS = N_UNITS - N_CTX_UNITS
SEQ_PER_UNIT = UNIT // SEQ

LANES = 128
MOD_ROWS = 8
HG_BLOCK = 256
HG_SUB = 32
HG_INST = 2 * (UNIT // HG_BLOCK)
HG_HEADS_PER_STEP = 2
VMEM_LIMIT = 56 * 1024 * 1024

_NT = (((1,), (1,)), ((), ()))
_TN = (((0,), (0,)), ((), ()))
_NN = (((1,), (0,)), ((), ()))


def _dg(a, b, dims):
    return lax.dot_general(a, b, dims, preferred_element_type=F32)


def _split2(x):
    hi = x.astype(BF16)
    lo = (x - hi.astype(F32)).astype(BF16)
    return hi, lo


def _dot1(a, b, dims=_NN):
    return _dg(a.astype(BF16), b.astype(BF16), dims)


def _dot3(a, b, dims=_NN):
    ah, al = _split2(a)
    bh, bl = _split2(b)
    return _dg(ah, bh, dims) + (_dg(ah, bl, dims) + _dg(al, bh, dims))


_dot_inv = _dot1


def _dot01(m, x):
    hi = x.astype(BF16)
    r = x - hi.astype(F32)
    mid = r.astype(BF16)
    lo = (r - mid.astype(F32)).astype(BF16)
    return _dg(m, hi, _NN) + (_dg(m, mid, _NN) + _dg(m, lo, _NN))


def _dot01_2(m, x):
    hi, lo = _split2(x)
    return _dg(m, hi, _NN) + _dg(m, lo, _NN)


def _sigmoid(x):
    return 1.0 / (1.0 + jnp.exp(-x))


def _silu(x):
    return x * _sigmoid(x)


def _gelu(x):
    return 0.5 * x * (1.0 + lax.erf(x * (2.0 ** -0.5)))


def _softplus(x):
    return jnp.maximum(x, 0.0) + jnp.log1p(jnp.exp(-jnp.abs(x)))


def _mod_row_of_tile(i, tm):
    return jnp.maximum(0, (i * tm - N_CTX_TOK) // DEC_SEQ + 1)


assert N_LAT_UNITS == N_CTX_UNITS


def _scan_unit(g):
    return jnp.where(g % 2 == 0, N_CTX_UNITS + g // 2, g // 2)


def _is_ctx_step(g):
    return g % 2 == 1


def _state_out_spec(layer, first, heads_per_step=None):
    if first:
        return pl.BlockSpec((SEQ_PER_UNIT, DEPTH, 2, heads_per_step, HEAD_DIM, HEAD_DIM),
                            lambda h, g: (g // 2, 0, 0, h, 0, 0))
    return pl.BlockSpec((SEQ_PER_UNIT, 1, 2, heads_per_step, HEAD_DIM, HEAD_DIM),
                        lambda h, g: (g // 2, layer, 0, h, 0, 0))


def _zero_other_slots(s_ref, slot):
    for other in range(s_ref.shape[1]):
        if other != slot:
            s_ref[:, other] = jnp.zeros((s_ref.shape[0],) + tuple(s_ref.shape[2:]), F32)


def _state_in_spec(layer, heads_per_step=None):
    return pl.BlockSpec((None, None, 2, heads_per_step, HEAD_DIM, HEAD_DIM),
                        lambda h, g: (g // 2, layer, 0, h, 0, 0))


_STATE_SHAPE = jax.ShapeDtypeStruct((BATCH, DEPTH, 2, HEADS, HEAD_DIM, HEAD_DIM), F32)


def _without_arg(kernel, pos):
    def wrapped(*refs):
        return kernel(*refs[:pos], *refs[pos + 1:])
    return wrapped


def _cparams(sem):
    return pltpu.CompilerParams(dimension_semantics=sem, vmem_limit_bytes=VMEM_LIMIT)


def _mod_kernel(c_ref, w_ref, b_ref, o_ref):
    s = _silu(c_ref[...])
    hi, lo = _split2(s)
    w = w_ref[...].astype(BF16)
    o_ref[...] = _dg(hi, w, _NN) + _dg(lo, w, _NN) + b_ref[...]


def _mod_call(cvec, w_mod, b_mod):
    tn = 2048
    n = 6 * D_MODEL
    return pl.pallas_call(
        _mod_kernel,
        grid=(DEPTH, n // tn),
        in_specs=[pl.BlockSpec((MOD_ROWS, D_MODEL), lambda l, j: (0, 0)),
                  pl.BlockSpec((None, D_MODEL, tn), lambda l, j: (l, 0, j)),
                  pl.BlockSpec((None, 1, tn), lambda l, j: (l, 0, j))],
        out_specs=pl.BlockSpec((None, MOD_ROWS, tn), lambda l, j: (l, 0, j)),
        out_shape=jax.ShapeDtypeStruct((DEPTH, MOD_ROWS, n), F32),
        compiler_params=_cparams(("arbitrary", "arbitrary")),
        name="modulation",
    )(cvec, w_mod, b_mod.reshape(DEPTH, 1, n))


def _adaln(x, g, shift, scale):
    y = x * lax.rsqrt(jnp.mean(x * x, axis=-1, keepdims=True) + EPS) * g
    return y * (1.0 + scale) + shift


def _residual_specs(x, tm, idx):
    if not isinstance(x, tuple):
        return [pl.BlockSpec((tm, D_MODEL), lambda *g: (idx(*g), 0))], [x]
    n_ctx = N_CTX_TOK // tm
    return ([pl.BlockSpec((tm, D_MODEL), lambda *g: (jnp.minimum(idx(*g), n_ctx - 1), 0)),
             pl.BlockSpec((tm, D_MODEL), lambda *g: (jnp.maximum(idx(*g) - n_ctx, 0), 0))], list(x))


def _residual_tile(refs, tile, tm):
    if len(refs) == 1:
        return refs[0][...]
    return jnp.where(tile < N_CTX_TOK // tm, refs[0][...], refs[1][...])


def _normmod_kernel(*refs, n_x):
    x_refs, (g_ref, sh_ref, sc_ref, o_ref) = refs[:n_x], refs[n_x:]
    x = _residual_tile(x_refs, pl.program_id(0), o_ref.shape[0])
    o_ref[...] = _adaln(x, g_ref[...], sh_ref[...], sc_ref[...]).astype(o_ref.dtype)


def _normmod_call(x, g, mod3, layer, k_shift, k_scale):
    tm = 512
    def mod_spec(k):
        return pl.BlockSpec((None, 1, D_MODEL),
                            lambda i: (layer * MOD_ROWS + _mod_row_of_tile(i, tm), 0, k))
    x_specs, x_args = _residual_specs(x, tm, lambda i: i)
    return pl.pallas_call(
        functools.partial(_normmod_kernel, n_x=len(x_args)),
        grid=(N_TOK // tm,),
        in_specs=x_specs + [pl.BlockSpec((1, D_MODEL), lambda i: (0, 0)), mod_spec(k_shift), mod_spec(k_scale)],
        out_specs=pl.BlockSpec((tm, D_MODEL), lambda i: (i, 0)),
        out_shape=jax.ShapeDtypeStruct((N_TOK, D_MODEL), BF16),
        compiler_params=_cparams(("arbitrary",)),
        name="norm_mod",
    )(*x_args, g.reshape(1, D_MODEL), mod3, mod3)


def _final_norm_kernel(x_ref, g_ref, o_ref):
    x = x_ref[...]
    o_ref[...] = x * lax.rsqrt(jnp.mean(x * x, axis=-1, keepdims=True) + EPS) * g_ref[...]


def _final_norm_call(x, g, row0, n_rows):
    tm = 512
    return pl.pallas_call(
        _final_norm_kernel,
        grid=(n_rows // tm,),
        in_specs=[pl.BlockSpec((tm, D_MODEL), lambda i: (row0 // tm + i, 0)),
                  pl.BlockSpec((1, D_MODEL), lambda i: (0, 0))],
        out_specs=pl.BlockSpec((tm, D_MODEL), lambda i: (i, 0)),
        out_shape=jax.ShapeDtypeStruct((n_rows, D_MODEL), F32),
        compiler_params=_cparams(("arbitrary",)),
        name="final_norm",
    )(x, g.reshape(1, D_MODEL))


def _mm_kernel(x_ref, w_ref, o_ref, wbf_ref, *, act, w_transposed):
    @pl.when(pl.program_id(1) == 0)
    def _cast_weights():
        wbf_ref[...] = w_ref[...].reshape(wbf_ref.shape).astype(BF16)

    acc = lax.dot_general(x_ref[...], wbf_ref[...], _NT if w_transposed else _NN, preferred_element_type=F32)
    if act == "relu2":
        acc = jnp.square(jnp.maximum(acc, 0.0))
    if len(o_ref.shape) == 3:
        for c in range(o_ref.shape[0]):
            o_ref[c] = acc[:, c * LANES:(c + 1) * LANES].astype(o_ref.dtype)
    else:
        o_ref[...] = acc.astype(o_ref.dtype)


def _mm_call(x, w_all, layer, *, col0, n, act, out_dtype, tm, tn, name, w_transposed=False,
             lane_tile_major=False):
    m, k = x.shape
    assert n % tn == 0 and m % tm == 0
    if w_transposed:
        assert col0 % 8 == 0
        w_spec = pl.BlockSpec((pl.Element(1), pl.Element(tn), pl.Element(k)),
                              lambda j, i: (layer, pl.multiple_of(col0 + j * tn, 8), 0))
        w_scratch = pltpu.VMEM((tn, k), BF16)
    else:
        assert col0 % tn == 0
        w_spec = pl.BlockSpec((None, k, tn), lambda j, i: (layer, 0, col0 // tn + j))
        w_scratch = pltpu.VMEM((k, tn), BF16)
    if lane_tile_major:
        out_spec = pl.BlockSpec((tn // LANES, tm, LANES), lambda j, i: (j, i, 0))
        out_shape = jax.ShapeDtypeStruct((n // LANES, m, LANES), out_dtype)
    else:
        out_spec = pl.BlockSpec((tm, tn), lambda j, i: (i, j))
        out_shape = jax.ShapeDtypeStruct((m, n), out_dtype)
    return pl.pallas_call(
        functools.partial(_mm_kernel, act=act, w_transposed=w_transposed),
        grid=(n // tn, m // tm),
        in_specs=[pl.BlockSpec((tm, k), lambda j, i: (i, 0)), w_spec],
        out_specs=out_spec,
        out_shape=out_shape,
        scratch_shapes=[w_scratch],
        compiler_params=_cparams(("arbitrary", "arbitrary")),
        name=name,
    )(x, w_all)


def _mm_bf16_kernel(x_ref, w_ref, r_ref, g_ref, o_ref):
    acc = jnp.dot(x_ref[...], w_ref[...], preferred_element_type=F32)
    o_ref[...] = r_ref[...] + g_ref[...] * acc


def _mm_resid_call(x, w, resid, mod3, layer, k_gate, *, tm, tn, name):
    m, k = x.shape
    n = w.shape[2]
    per_tile = D_MODEL // tn
    return pl.pallas_call(
        _mm_bf16_kernel,
        grid=(m // tm, n // tn),
        in_specs=[pl.BlockSpec((tm, k), lambda i, j: (i, 0)),
                  pl.BlockSpec((None, k, tn), lambda i, j: (layer, 0, j)),
                  pl.BlockSpec((tm, tn), lambda i, j: (i, j)),
                  pl.BlockSpec((None, 1, tn),
                               lambda i, j: (layer * MOD_ROWS + _mod_row_of_tile(i, tm), 0,
                                             k_gate * per_tile + j))],
        out_specs=pl.BlockSpec((tm, tn), lambda i, j: (i, j)),
        out_shape=jax.ShapeDtypeStruct((m, n), F32),
        compiler_params=_cparams(("arbitrary", "arbitrary")),
        name=name,
    )(x, w, resid, mod3)


def _outproj_norm_kernel(*refs, n_x):
    m_ref, w_ref = refs[:2]
    x_refs = refs[2:2 + n_x]
    gate_ref, g_ref, sh_ref, sc_ref, xo_ref, h_ref, wbf_ref = refs[2 + n_x:]

    @pl.when(pl.program_id(0) == 0)
    def _cast_weights():
        wbf_ref[...] = w_ref[...].astype(BF16)

    acc = jnp.dot(m_ref[...], wbf_ref[...], preferred_element_type=F32)
    xn = _residual_tile(x_refs, pl.program_id(0), xo_ref.shape[0]) + gate_ref[...] * acc
    xo_ref[...] = xn
    h_ref[...] = _adaln(xn, g_ref[...], sh_ref[...], sc_ref[...]).astype(h_ref.dtype)


def _outproj_norm_call(merged, w_out, x, norm_g, mod3, layer):
    tm = 256 if isinstance(x, tuple) else 512
    def mod_spec(k):
        return pl.BlockSpec((None, 1, D_MODEL),
                            lambda i: (layer * MOD_ROWS + _mod_row_of_tile(i, tm), 0, k))
    row = pl.BlockSpec((tm, D_MODEL), lambda i: (i, 0))
    x_specs, x_args = _residual_specs(x, tm, lambda i: i)
    return pl.pallas_call(
        functools.partial(_outproj_norm_kernel, n_x=len(x_args)),
        grid=(N_TOK // tm,),
        in_specs=[row,
                  pl.BlockSpec((None, D_MODEL, D_MODEL), lambda i: (layer, 0, 0), pipeline_mode=pl.Buffered(1))]
                 + x_specs + [mod_spec(2), pl.BlockSpec((1, D_MODEL), lambda i: (0, 0)), mod_spec(3), mod_spec(4)],
        out_specs=[row, row],
        out_shape=[jax.ShapeDtypeStruct((N_TOK, D_MODEL), F32),
                   jax.ShapeDtypeStruct((N_TOK, D_MODEL), BF16)],
        scratch_shapes=[pltpu.VMEM((D_MODEL, D_MODEL), BF16)],
        compiler_params=_cparams(("arbitrary",)),
        name="out_proj_norm",
    )(merged, w_out, *x_args, mod3, norm_g.reshape(1, D_MODEL), mod3, mod3)


def _ffn_down_norm_kernel(x_ref, w_ref, r_ref, gate_ref, g_ref, sh_ref, sc_ref, o_ref, h_ref, row_ref):
    j = pl.program_id(1)
    tn = o_ref.shape[1]
    acc = jnp.dot(x_ref[...], w_ref[...], preferred_element_type=F32)
    xn = r_ref[...] + gate_ref[...] * acc
    o_ref[...] = xn
    row_ref[:, pl.ds(pl.multiple_of(j * tn, tn), tn)] = xn

    @pl.when(j == pl.num_programs(1) - 1)
    def _next_layer_norm():
        h_ref[...] = _adaln(row_ref[...], g_ref[...], sh_ref[...], sc_ref[...]).astype(h_ref.dtype)


def _ffn_down_norm_call(up, w_bf, resid, mod3, layer, next_norm_g):
    tm, tn = 512, 512
    m, k = up.shape
    per_tile = D_MODEL // tn
    def mod_row(i):
        return _mod_row_of_tile(i, tm)
    def next_mod(kk):
        return pl.BlockSpec((None, 1, D_MODEL), lambda i, j: ((layer + 1) * MOD_ROWS + mod_row(i), 0, kk))
    return pl.pallas_call(
        _ffn_down_norm_kernel,
        grid=(m // tm, D_MODEL // tn),
        in_specs=[pl.BlockSpec((tm, k), lambda i, j: (i, 0)),
                  pl.BlockSpec((None, k, tn), lambda i, j: (layer, 0, j)),
                  pl.BlockSpec((tm, tn), lambda i, j: (i, j)),
                  pl.BlockSpec((None, 1, tn), lambda i, j: (layer * MOD_ROWS + mod_row(i), 0, 5 * per_tile + j)),
                  pl.BlockSpec((1, D_MODEL), lambda i, j: (0, 0)), next_mod(0), next_mod(1)],
        out_specs=[pl.BlockSpec((tm, tn), lambda i, j: (i, j)),
                   pl.BlockSpec((tm, D_MODEL), lambda i, j: (i, 0))],
        out_shape=[jax.ShapeDtypeStruct((m, D_MODEL), F32),
                   jax.ShapeDtypeStruct((m, D_MODEL), BF16)],
        scratch_shapes=[pltpu.VMEM((tm, D_MODEL), F32)],
        compiler_params=_cparams(("arbitrary", "arbitrary")),
        name="ffn_down_norm",
    )(up, w_bf, resid, mod3, next_norm_g.reshape(1, D_MODEL), mod3, mod3)


def _gated_merge_kernel(h_ref, wga_ref, wgb_ref, wgc_ref, oa_ref, ob_ref, oc_ref, wa_ref, wb_ref, wc_ref,
                        o_ref, wg_bf, wbr_bf):
    @pl.when(pl.program_id(1) == 0)
    def _cast_weights():
        for t, ref in enumerate((wga_ref, wgb_ref, wgc_ref)):
            wg_bf[t] = ref[...].reshape(wg_bf.shape[1:]).astype(BF16)
        for t, ref in enumerate((wa_ref, wb_ref, wc_ref)):
            wbr_bf[t] = ref[...].astype(BF16)

    def rows_of(ref):
        return jnp.concatenate([ref[t] for t in range(ref.shape[0])], axis=1)

    h = h_ref[...]
    acc = None
    for t, br_ref in enumerate((oa_ref, ob_ref, oc_ref)):
        gate = _sigmoid(lax.dot_general(h, wg_bf[t], _NT, preferred_element_type=F32))
        term = gate * jnp.dot(rows_of(br_ref), wbr_bf[t], preferred_element_type=F32)
        acc = term if acc is None else acc + term
    o_ref[...] = acc.astype(o_ref.dtype)


def _gated_merge_call(h, w_in_t, o_a, o_b, o_c, w_a, w_b, w_c, layer):
    tm, tn = 1024, 256
    def gate_w(b):
        return pl.BlockSpec((pl.Element(1), pl.Element(tn), pl.Element(D_MODEL)),
                            lambda j, i: (layer, pl.multiple_of(OFF_GATES + b * D_MODEL + j * tn, 8), 0))
    br = pl.BlockSpec((HEADS, tm, HEAD_DIM), lambda j, i: (0, i, 0))
    wt = pl.BlockSpec((None, HG_F, tn), lambda j, i: (layer, 0, j))
    return pl.pallas_call(
        _gated_merge_kernel,
        grid=(D_MODEL // tn, N_TOK // tm),
        in_specs=[pl.BlockSpec((tm, D_MODEL), lambda j, i: (i, 0)), gate_w(0), gate_w(1), gate_w(2),
                  br, br, br, wt, wt, wt],
        out_specs=pl.BlockSpec((tm, tn), lambda j, i: (i, j)),
        out_shape=jax.ShapeDtypeStruct((N_TOK, D_MODEL), BF16),
        scratch_shapes=[pltpu.VMEM((3, tn, D_MODEL), BF16), pltpu.VMEM((3, HG_F, tn), BF16)],
        compiler_params=_cparams(("arbitrary", "arbitrary")),
        name="gated_merge",
    )(h, w_in_t, w_in_t, w_in_t, o_a, o_b, o_c, w_a, w_b, w_c)


def _head_rmsnorm_gate(o, g, og):
    y = o * lax.rsqrt(jnp.mean(o * o, axis=-1, keepdims=True) + EPS) * g
    return y * _silu(og)


def _hgrn_kernel(q_ref, i_ref, og_ref, ff_ref, fb_ref, lb_ref, g_ref, s0_ref, o_ref, s_ref,
                 acc_ref, b_ref, stc_ref, km_ref, *, state_slot):
    is_ctx = _is_ctx_step(pl.program_id(1))
    _zero_other_slots(s_ref, state_slot)
    nblk = UNIT // HG_BLOCK
    nsub = HG_BLOCK // HG_SUB
    r = lax.broadcasted_iota(jnp.int32, (HG_BLOCK, HG_BLOCK), 0)
    c = lax.broadcasted_iota(jnp.int32, (HG_BLOCK, HG_BLOCK), 1)
    same = (r // HG_SUB) == (c // HG_SUB)
    row_sub = lax.broadcasted_iota(jnp.int32, (HG_BLOCK, HEAD_DIM), 0) // HG_SUB
    tri = (jnp.logical_and(same, c <= r), jnp.logical_and(same, c >= r))
    tri_bf = [jnp.where(t, 1.0, 0.0).astype(BF16) for t in tri]
    z_refs = (ff_ref, fb_ref)
    inst = [(hh, blk, d) for hh in range(HG_HEADS_PER_STEP) for blk in range(nblk) for d in range(2)]

    def expand(x):
        return jnp.broadcast_to(x[:, None, :], (nsub, HG_SUB, HEAD_DIM)).reshape(HG_BLOCK, HEAD_DIM)

    def blk_rows(blk):
        return slice(blk * HG_BLOCK, (blk + 1) * HG_BLOCK)

    def head_cols(hh):
        return slice(hh * HEAD_DIM, (hh + 1) * HEAD_DIM)

    qs = {(hh, blk): _silu(q_ref[hh, blk_rows(blk), :]) for hh in range(HG_HEADS_PER_STEP) for blk in range(nblk)}
    vs = {(hh, blk): i_ref[hh, blk_rows(blk), :] for hh in range(HG_HEADS_PER_STEP) for blk in range(nblk)}
    ks, lfs = [], []
    for hh, blk, d in inst:
        lb = lb_ref[d:d + 1, head_cols(hh)]
        f = lb + (1.0 - lb) * _sigmoid(z_refs[d][hh, blk_rows(blk), :])
        lfs.append(jnp.log(f))
        ks.append(1.0 - f)
    bs = [_dot01_2(tri_bf[d], lf) for (hh, blk, d), lf in zip(inst, lfs)]
    tots, qts, kts, qds = [], [], [], []
    for n, (hh, blk, d) in enumerate(inst):
        b = bs[n]
        b_ref[n] = b
        tot = b_ref[n, pl.ds((HG_SUB - 1) if d == 0 else 0, nsub, stride=HG_SUB), :]
        mid_f = expand(b_ref[n, pl.ds(HG_SUB // 2, nsub, stride=HG_SUB), :])
        tots.append(tot)
        qts.append(qs[hh, blk] * jnp.exp(b - mid_f))
        kts.append(ks[n] * jnp.exp(mid_f - b))
        qds.append((qs[hh, blk] * jnp.exp(b)).astype(BF16))
        kd = ks[n] * jnp.exp(expand(tot) - b)
        for s in range(nsub):
            km_ref[n, :, s * HEAD_DIM:(s + 1) * HEAD_DIM] = jnp.where(row_sub == s, kd, 0.0).astype(BF16)
    scs = [jnp.where(tri[d], _dot1(qt, kt, _NT), 0.0) for (hh, blk, d), qt, kt in zip(inst, qts, kts)]
    uts = [_dg(vs[hh, blk].astype(BF16), km_ref[n], _TN) for n, (hh, blk, d) in enumerate(inst)]
    outs = [_dot1(sc, vs[hh, blk]) for (hh, blk, d), sc in zip(inst, scs)]
    for hh in range(HG_HEADS_PER_STEP):
        for d in range(2):
            st = s0_ref[d, hh].T
            for blk in (range(nblk) if d == 0 else range(nblk - 1, -1, -1)):
                n = inst.index((hh, blk, d))
                st = jnp.where(is_ctx, 0.0, st)
                for s in (range(nsub) if d == 0 else range(nsub - 1, -1, -1)):
                    stc_ref[n, s] = st.T.astype(BF16)
                    st = st * jnp.exp(tots[n][s:s + 1, :]) + uts[n][:, s * HEAD_DIM:(s + 1) * HEAD_DIM]
                s_ref[blk, state_slot, d, hh] = st.T
    for n, (hh, blk, d) in enumerate(inst):
        o_int = [_dg(qds[n][s * HG_SUB:(s + 1) * HG_SUB], stc_ref[n, s], _NN) for s in range(nsub)]
        acc_ref[d, hh, blk_rows(blk), :] = outs[n] + jnp.concatenate(o_int, axis=0)
    for hh in range(HG_HEADS_PER_STEP):
        o_ref[hh] = _head_rmsnorm_gate(acc_ref[0, hh] + acc_ref[1, hh], g_ref[...], og_ref[hh]).astype(o_ref.dtype)


def _hgrn_call(z, lb, onorm_g, state, layer, new_states):
    nh = HG_HEADS_PER_STEP
    def col(off):
        return pl.BlockSpec((nh, UNIT, HEAD_DIM), lambda h, g: (off // HEAD_DIM // nh + h, _scan_unit(g), 0))
    in_specs = [col(OFF_HQ), col(OFF_HI), col(OFF_HG), col(OFF_HFF), col(OFF_HFB),
                pl.BlockSpec((2, nh * HEAD_DIM), lambda h, g: (0, h)),
                pl.BlockSpec((1, HEAD_DIM), lambda h, g: (0, 0)),
                _state_in_spec(layer, nh)]
    args = [z, z, z, z, z, lb, onorm_g.reshape(1, HEAD_DIM), state]
    first = new_states is None
    kernel, aliases = functools.partial(_hgrn_kernel, state_slot=layer if first else 0), {}
    if not first:
        kernel, aliases = _without_arg(kernel, len(args)), {len(args): 1}
        in_specs.append(pl.BlockSpec(memory_space=pl.ANY))
        args.append(new_states)
    n_inst = nh * HG_INST
    return pl.pallas_call(
        kernel,
        grid=(HEADS // nh, N_UNITS),
        in_specs=in_specs,
        out_specs=[pl.BlockSpec((nh, UNIT, HEAD_DIM), lambda h, g: (h, _scan_unit(g), 0)),
                   _state_out_spec(layer, first, nh)],
        out_shape=[jax.ShapeDtypeStruct((HEADS, N_TOK, HEAD_DIM), BF16), _STATE_SHAPE],
        input_output_aliases=aliases,
        scratch_shapes=[pltpu.VMEM((2, nh, UNIT, HEAD_DIM), F32),
                        pltpu.VMEM((n_inst, HG_BLOCK, HEAD_DIM), F32),
                        pltpu.VMEM((n_inst, HG_BLOCK // HG_SUB, HEAD_DIM, HEAD_DIM), BF16),
                        pltpu.VMEM((n_inst, HG_BLOCK, (HG_BLOCK // HG_SUB) * HEAD_DIM), BF16)],
        compiler_params=_cparams(("arbitrary", "arbitrary")),
        name="hgrn2_scan",
    )(*args)


def _gmlp_kernel(u_ref, v_ref, vn_ref, ws_ref, bs_ref, o_ref):
    tm = u_ref.shape[1]
    for g in range(CM_GROUPS):
        cols = slice(g * HEAD_DIM, (g + 1) * HEAD_DIM)
        vg = _gelu(v_ref[g])
        vg = vg * lax.rsqrt(jnp.mean(vg * vg, axis=-1, keepdims=True) + EPS) * vn_ref[:, cols]
        ug = _gelu(u_ref[g])
        w = ws_ref[g]
        bias = bs_ref[:, g:g + 1]
        for ch in range(tm // CM_CHUNK):
            rows = slice(ch * CM_CHUNK, (ch + 1) * CM_CHUNK)
            s = _dot3(w, vg[rows]) + bias
            o_ref[g, rows, :] = (ug[rows] * s).astype(o_ref.dtype)


def _gmlp_call(z, vnorm_g, ws, bs_t):
    tm = 512
    return pl.pallas_call(
        _gmlp_kernel,
        grid=(N_TOK // tm,),
        in_specs=[pl.BlockSpec((CM_GROUPS, tm, HEAD_DIM), lambda i: (OFF_CU // CM_W, i, 0)),
                  pl.BlockSpec((CM_GROUPS, tm, HEAD_DIM), lambda i: (OFF_CV // CM_W, i, 0)),
                  pl.BlockSpec((1, CM_W), lambda i: (0, 0)),
                  pl.BlockSpec((CM_GROUPS, CM_CHUNK, CM_CHUNK), lambda i: (0, 0, 0)),
                  pl.BlockSpec((CM_CHUNK, CM_GROUPS), lambda i: (0, 0))],
        out_specs=pl.BlockSpec((CM_GROUPS, tm, HEAD_DIM), lambda i: (0, i, 0)),
        out_shape=jax.ShapeDtypeStruct((CM_GROUPS, N_TOK, HEAD_DIM), BF16),
        compiler_params=_cparams(("arbitrary",)),
        name="chunk_gmlp",
    )(z, z, vnorm_g.reshape(1, CM_W), ws, bs_t)


GP_BLOCK = 256


def _gdn_gates_kernel(ab_ref, alog_ref, dt_ref, col_ref, rowt_ref):
    ab = ab_ref[...]
    lane = lax.broadcasted_iota(jnp.int32, ab.shape, 1)
    g = jnp.where(lane < 2 * HEADS, -jnp.exp(alog_ref[...]) * _softplus(ab + dt_ref[...]), 0.0)
    r = lax.broadcasted_iota(jnp.int32, (GP_BLOCK, GP_BLOCK), 0)
    c = lax.broadcasted_iota(jnp.int32, (GP_BLOCK, GP_BLOCK), 1)
    same = (r // GDN_CHUNK) == (c // GDN_CHUNK)
    tri_f = jnp.where(jnp.logical_and(same, c <= r), 1.0, 0.0).astype(BF16)
    tri_b = jnp.where(jnp.logical_and(same, c >= r), 1.0, 0.0).astype(BF16)
    cf = _dot01(tri_f, g)
    cb = _dot01(tri_b, g)
    col = jnp.where(lane < HEADS, cf, jnp.where(lane < 2 * HEADS, cb, _sigmoid(ab)))
    col_ref[...] = col
    rowt_ref[...] = col.T


def _gdn_gates_call(ab, a_log, dt_bias):
    pad = lambda t: jnp.pad(t.reshape(1, 2 * HEADS), ((0, 0), (0, LANES - 2 * HEADS)))
    return pl.pallas_call(
        _gdn_gates_kernel,
        grid=(N_TOK // GP_BLOCK,),
        in_specs=[pl.BlockSpec((GP_BLOCK, LANES), lambda i: (i, 0)),
                  pl.BlockSpec((1, LANES), lambda i: (0, 0)),
                  pl.BlockSpec((1, LANES), lambda i: (0, 0))],
        out_specs=[pl.BlockSpec((GP_BLOCK, LANES), lambda i: (i, 0)),
                   pl.BlockSpec((LANES, GP_BLOCK), lambda i: (0, i))],
        out_shape=[jax.ShapeDtypeStruct((N_TOK, LANES), F32),
                   jax.ShapeDtypeStruct((LANES, N_TOK), F32)],
        compiler_params=_cparams(("arbitrary",)),
        name="gdn_gates",
    )(ab, pad(a_log), pad(dt_bias))


CONV_PAD = 72
GDN_AQ = HEAD_DIM + GDN_CHUNK
GDN_HEADS_PER_STEP = 2
GDN_STEP = 16 * GDN_CHUNK


def _gdn_kernel(qr_ref, kr_ref, vr_ref, og_ref, col_ref, rowt_ref, cwq_ref, cwk_ref, cwv_ref,
                g_ref, s0_ref, o_ref, s_ref,
                xp_ref, q_s, k_s, v_s, o_s, b_s, aq_s, *, state_slot):
    is_ctx = _is_ctx_step(pl.program_id(1))
    _zero_other_slots(s_ref, state_slot)
    heads = [GDN_HEADS_PER_STEP * pl.program_id(0) + hh for hh in range(GDN_HEADS_PER_STEP)]
    n_chunks = UNIT // GDN_CHUNK
    chunks_per_seq = SEQ // GDN_CHUNK

    t = lax.broadcasted_iota(jnp.int32, (UNIT, 1), 0)
    period = jnp.where(is_ctx, SEQ, GRID_W)
    pos = jnp.bitwise_and(t, period - 1)
    ok_left = pos != 0
    ok_right = pos != period - 1
    zeros_pad = jnp.zeros((CONV_K, CONV_PAD, HEAD_DIM), F32)
    xp_ref[:, 0:CONV_PAD, :] = zeros_pad
    xp_ref[:, CONV_PAD + UNIT:CONV_PAD + UNIT + CONV_PAD, :] = zeros_pad

    def conv_silu(x_ref, w_ref, hh):
        x = x_ref[hh]
        xp_ref[0, CONV_PAD:CONV_PAD + UNIT, :] = jnp.where(ok_right, x, 0.0)
        xp_ref[1, CONV_PAD:CONV_PAD + UNIT, :] = x
        xp_ref[2, CONV_PAD:CONV_PAD + UNIT, :] = jnp.where(ok_left, x, 0.0)
        acc = jnp.zeros((UNIT, HEAD_DIM), F32)
        for i in range(CONV_K):
            for j in range(CONV_K):
                w = w_ref[CONV_K * i + j:CONV_K * i + j + 1, hh * HEAD_DIM:(hh + 1) * HEAD_DIM]
                if i != CONV_K // 2:
                    w = jnp.where(is_ctx, 0.0, w)
                start = CONV_PAD + (i - 1) * GRID_W + (j - 1)
                acc = acc + xp_ref[j, start:start + UNIT, :] * w
        return _silu(acc)

    def l2norm(x):
        return x * lax.rsqrt(jnp.sum(x * x, axis=-1, keepdims=True) + EPS)

    rr = lax.broadcasted_iota(jnp.int32, (GDN_CHUNK, GDN_CHUNK), 0)
    cc = lax.broadcasted_iota(jnp.int32, (GDN_CHUNK, GDN_CHUNK), 1)
    eye = jnp.where(rr == cc, 1.0, 0.0)
    same_blk = [(rr // b) == (cc // b) for b in (8, 16, 32, 64)]
    lane = lax.broadcasted_iota(jnp.int32, (GDN_STEP, LANES), 1)
    sub8 = lax.broadcasted_iota(jnp.int32, (HEADS, GDN_STEP), 0)

    def pick(x, j):
        return jnp.sum(jnp.where(lane[:x.shape[0]] == j, x, 0.0), axis=-1, keepdims=True)

    incl = (cc <= rr, cc >= rr)
    strict = (cc < rr, cc > rr)
    off_masks = [jnp.logical_and(same_blk[lvl], jnp.logical_not(same_blk[lvl - 1]))
                 for lvl in range(1, len(same_blk))]

    def phase1(p, hh):
        head = heads[hh]
        rows = pl.ds(pl.multiple_of(p * GDN_STEP, GDN_STEP), GDN_STEP)
        q2, k2, v2 = q_s[rows, :], k_s[rows, :], v_s[rows, :]
        col = col_ref[rows, :]
        gcols = [pick(col, d * HEADS + head) for d in range(2)]
        betas = [pick(col, (2 + d) * HEADS + head) for d in range(2)]
        grows = [jnp.sum(jnp.where(sub8 == head, rowt_ref[d * HEADS:(d + 1) * HEADS, rows], 0.0),
                         axis=0, keepdims=True) for d in range(2)]
        inst = []
        for half in range(GDN_STEP // GDN_CHUNK):
            sl = slice(half * GDN_CHUNK, (half + 1) * GDN_CHUNK)
            qc, kc, vc = q2[sl], k2[sl], v2[sl]
            kk = _dot1(kc, kc, _NT)
            qk = _dot1(qc, kc, _NT)
            for d in range(2):
                gcol, beta = gcols[d][sl], betas[d][sl]
                decay = jnp.where(incl[d], jnp.exp(gcol - grows[d][:, sl]), 0.0)
                m = jnp.where(strict[d], beta * kk * decay, 0.0)
                inst.append((half, d, qc, kc, vc, qk, gcol, beta, decay, m))
        ms = [t[-1] for t in inst]
        pws = [jnp.where(same_blk[0], -m, 0.0) for m in ms]
        tinvs = [eye + pw for pw in pws]
        for _ in range(2):
            pws = [_dot_inv(pw, pw) for pw in pws]
            tinvs = [t + _dot_inv(t, pw) for t, pw in zip(tinvs, pws)]
        for mask in off_masks:
            tmp = [_dot_inv(jnp.where(mask, m, 0.0), t) for m, t in zip(ms, tinvs)]
            tinvs = [t - _dot_inv(t, x) for t, x in zip(tinvs, tmp)]
        uws = [_dot1(tinv, jnp.concatenate([vc * beta, kc * (beta * jnp.exp(gcol))], axis=1)).astype(BF16)
               for (half, d, qc, kc, vc, qk, gcol, beta, decay, m), tinv in zip(inst, tinvs)]
        bas, ows = [], []
        for (half, d, qc, kc, vc, qk, gcol, beta, decay, m), uw in zip(inst, uws):
            glast = gcol[GDN_CHUNK - 1:GDN_CHUNK] if d == 0 else gcol[0:1]
            bas.append(_dg((kc * jnp.exp(glast - gcol)).astype(BF16), uw, _TN))
            ows.append(_dg((qk * decay).astype(BF16), uw, _NN))
        for (half, d, qc, kc, vc, qk, gcol, beta, decay, m), ba, ow in zip(inst, bas, ows):
            chunk0 = p * GDN_STEP + half * GDN_CHUNK
            o_s[hh, d, pl.ds(pl.multiple_of(chunk0, GDN_CHUNK), GDN_CHUNK), :] = ow[:, :HEAD_DIM]
            b_s[hh, d, pl.ds(pl.multiple_of(2 * chunk0, HEAD_DIM), HEAD_DIM), :] = ba[:, :HEAD_DIM]
            aq_s[hh, d, pl.ds(pl.multiple_of(3 * chunk0, GDN_AQ), GDN_AQ), :] = jnp.concatenate(
                [ba[:, HEAD_DIM:], qc * jnp.exp(gcol) - ow[:, HEAD_DIM:]], axis=0).astype(BF16)

    for hh in range(GDN_HEADS_PER_STEP):
        q_s[...] = l2norm(conv_silu(qr_ref, cwq_ref, hh)) * (HEAD_DIM ** -0.5)
        k_s[...] = l2norm(conv_silu(kr_ref, cwk_ref, hh))
        v_s[...] = conv_silu(vr_ref, cwv_ref, hh)
        for p in range(UNIT // GDN_STEP):
            phase1(p, hh)

    lane1 = lax.broadcasted_iota(jnp.int32, (1, LANES), 1)

    def advance(chains, states):
        rs = [_dg(aq_s[hh, d, pl.ds(pl.multiple_of(ch * GDN_AQ, GDN_AQ), GDN_AQ), :], s.astype(BF16), _NN)
              for (hh, d, ch), s in zip(chains, states)]
        new = []
        for (hh, d, ch), s, r in zip(chains, states, rs):
            tile0 = pl.multiple_of(ch * GDN_CHUNK + (GDN_CHUNK - 8 if d == 0 else 0), 8)
            last = col_ref[pl.ds(tile0, 8), :]
            last = last[7:8] if d == 0 else last[0:1]
            glast = jnp.sum(jnp.where(lane1 == d * HEADS + heads[hh], last, 0.0), axis=-1, keepdims=True)
            b = b_s[hh, d, pl.ds(pl.multiple_of(ch * HEAD_DIM, HEAD_DIM), HEAD_DIM), :]
            new.append(s * jnp.exp(glast) + (b - r[:HEAD_DIM]))
            rows = pl.ds(pl.multiple_of(ch * GDN_CHUNK, GDN_CHUNK), GDN_CHUNK)
            o_s[hh, d, rows, :] = o_s[hh, d, rows, :] + r[HEAD_DIM:]
        return tuple(new)

    @pl.when(is_ctx)
    def _context_unit():
        def body(i, states):
            chains = [(hh, d, sq * chunks_per_seq + (i if d == 0 else chunks_per_seq - 1 - i))
                      for hh in range(GDN_HEADS_PER_STEP) for sq in range(SEQ_PER_UNIT) for d in range(2)]
            return advance(chains, states)
        zero = jnp.zeros((HEAD_DIM, HEAD_DIM), F32)
        final = lax.fori_loop(0, chunks_per_seq, body, (zero,) * (2 * SEQ_PER_UNIT * GDN_HEADS_PER_STEP))
        for hh in range(GDN_HEADS_PER_STEP):
            for sq in range(SEQ_PER_UNIT):
                for d in range(2):
                    s_ref[sq, state_slot, d, hh] = final[(hh * SEQ_PER_UNIT + sq) * 2 + d]

    @pl.when(jnp.logical_not(is_ctx))
    def _latent_unit():
        def body(i, states):
            chains = [(hh, d, i if d == 0 else n_chunks - 1 - i) for hh in range(GDN_HEADS_PER_STEP) for d in range(2)]
            return advance(chains, states)
        init = tuple(s0_ref[d, hh] for hh in range(GDN_HEADS_PER_STEP) for d in range(2))
        final = lax.fori_loop(0, n_chunks, body, init)
        for hh in range(GDN_HEADS_PER_STEP):
            for sq in range(SEQ_PER_UNIT):
                for d in range(2):
                    s_ref[sq, state_slot, d, hh] = final[2 * hh + d]

    for hh in range(GDN_HEADS_PER_STEP):
        o_ref[hh] = _head_rmsnorm_gate(o_s[hh, 0] + o_s[hh, 1], g_ref[...], og_ref[hh]).astype(o_ref.dtype)


def _gdn_call(z, col, rowt, conv_w, onorm_g, state, layer, new_states):
    nh = GDN_HEADS_PER_STEP
    def zcol(off):
        return pl.BlockSpec((nh, UNIT, HEAD_DIM), lambda h, g: (off // HEAD_DIM // nh + h, _scan_unit(g), 0))
    def wcol(part):
        return pl.BlockSpec((CONV_K * CONV_K, nh * HEAD_DIM), lambda h, g: (0, part * HEADS // nh + h))
    scr = lambda *shape: pltpu.VMEM(shape, F32)
    in_specs = [zcol(OFF_GQ), zcol(OFF_GK), zcol(OFF_GV), zcol(OFF_GG),
                pl.BlockSpec((UNIT, LANES), lambda h, g: (_scan_unit(g), 0)),
                pl.BlockSpec((LANES, UNIT), lambda h, g: (0, _scan_unit(g))),
                wcol(0), wcol(1), wcol(2),
                pl.BlockSpec((1, HEAD_DIM), lambda h, g: (0, 0)),
                _state_in_spec(layer, nh)]
    args = [z, z, z, z, col, rowt, conv_w, conv_w, conv_w, onorm_g.reshape(1, HEAD_DIM), state]
    first = new_states is None
    kernel, aliases = functools.partial(_gdn_kernel, state_slot=layer if first else 0), {}
    if not first:
        kernel, aliases = _without_arg(kernel, len(args)), {len(args): 1}
        in_specs.append(pl.BlockSpec(memory_space=pl.ANY))
        args.append(new_states)
    return pl.pallas_call(
        kernel,
        grid=(HEADS // nh, N_UNITS),
        in_specs=in_specs,
        out_specs=[pl.BlockSpec((nh, UNIT, HEAD_DIM), lambda h, g: (h, _scan_unit(g), 0)),
                   _state_out_spec(layer, first, nh)],
        out_shape=[jax.ShapeDtypeStruct((HEADS, N_TOK, HEAD_DIM), BF16), _STATE_SHAPE],
        input_output_aliases=aliases,
        scratch_shapes=[scr(CONV_K, UNIT + 2 * CONV_PAD, HEAD_DIM),
                        scr(UNIT, HEAD_DIM), scr(UNIT, HEAD_DIM), scr(UNIT, HEAD_DIM),
                        scr(nh, 2, UNIT, HEAD_DIM),
                        scr(nh, 2, (UNIT // GDN_CHUNK) * HEAD_DIM, HEAD_DIM),
                        pltpu.VMEM((nh, 2, (UNIT // GDN_CHUNK) * GDN_AQ, HEAD_DIM), BF16)],
        compiler_params=_cparams(("arbitrary", "arbitrary")),
        name="gdn_scan",
    )(*args)


def kernel(x_prompt, x_sample, c, state_hgrn, state_gdn, c_ctx, norm1_g, norm2_g, w_mod, b_mod, w_in, hg_lb, hg_onorm_g, cm_vnorm_g, cm_ws, cm_bs, gdn_conv, gdn_A_log, gdn_dt_bias, gdn_onorm_g, w_br_hg, w_br_cm, w_br_gdn, w_out, w_ff1, w_ff2, final_g):
    x = (x_prompt.reshape(N_CTX_TOK, D_MODEL), x_sample.reshape(N_LAT_TOK, D_MODEL))
    cvec = jnp.concatenate([c_ctx[None, :], c, jnp.zeros((MOD_ROWS - 1 - DEC_BATCH, D_MODEL), F32)], axis=0)
    mod3 = _mod_call(cvec, w_mod, b_mod).reshape(DEPTH * MOD_ROWS, 1, 6 * D_MODEL)

    lb_all = jnp.cumsum(jax.nn.softmax(hg_lb.astype(F32), axis=0), axis=0)
    lb_all = lb_all - lb_all[:1]

    w_in_t = jnp.swapaxes(w_in, 1, 2)
    w_ff2_bf = w_ff2.astype(BF16)

    new_hg = new_gdn = None
    h = _normmod_call(x, norm1_g[0], mod3, 0, 0, 1)
    for l in range(DEPTH):
        z = _mm_call(h, w_in_t, l, col0=0, n=N_MAIN, act=None, out_dtype=F32, tm=1024, tn=1024,
                     w_transposed=True, lane_tile_major=True, name="in_proj")
        ab = _mm_call(h, w_in_t, l, col0=OFF_AB, n=LANES, act=None, out_dtype=F32, tm=1024, tn=LANES,
                      w_transposed=True, name="in_proj_ab")
        o_a, new_hg = _hgrn_call(z, lb_all[l], hg_onorm_g[l], state_hgrn, l, new_hg)
        o_b = _gmlp_call(z, cm_vnorm_g[l], cm_ws[l], cm_bs[l].T)
        col, rowt = _gdn_gates_call(ab, gdn_A_log[l], gdn_dt_bias[l])
        o_c, new_gdn = _gdn_call(z, col, rowt, gdn_conv[l].reshape(CONV_K * CONV_K, 3 * HG_F),
                                 gdn_onorm_g[l], state_gdn, l, new_gdn)

        merged = _gated_merge_call(h, w_in_t, o_a, o_b, o_c, w_br_hg, w_br_cm, w_br_gdn, l)
        x, h2 = _outproj_norm_call(merged, w_out, x, norm2_g[l], mod3, l)
        up = _mm_call(h2, w_ff1, l, col0=0, n=D_FF, act="relu2", out_dtype=BF16, tm=2048, tn=1024, name="ffn_up")
        if l + 1 < DEPTH:
            x, h = _ffn_down_norm_call(up, w_ff2_bf, x, mod3, l, norm1_g[l + 1])
        else:
            x = _mm_resid_call(up, w_ff2_bf, x, mod3, l, 5, tm=512, tn=512, name="ffn_down")

    y_prompt = _final_norm_call(x, final_g, 0, N_CTX_TOK).reshape(BATCH, SEQ, D_MODEL)
    y_sample = _final_norm_call(x, final_g, N_CTX_TOK, N_LAT_TOK).reshape(DEC_BATCH, DEC_SEQ, D_MODEL)
    return (y_prompt, y_sample, new_hg, new_gdn)
```

```python
import functools

import jax
import jax.numpy as jnp
from jax import lax
from jax.experimental import pallas as pl
from jax.experimental.pallas import tpu as pltpu

F32 = jnp.float32
BF16 = jnp.bfloat16

D_MODEL = 2048
BATCH = 16
SEQ = 256
DEPTH = 2
DEC_BATCH = 4
DEC_SEQ = 1024
GRID_W = 64
EPS = 1e-6
LOG2E = 1.4426950408889634
D_FF = 4 * D_MODEL
HEADS = 8
HEAD_DIM = 128
HG_F = HEADS * HEAD_DIM
CM_GROUPS = 8
CM_W = CM_GROUPS * HEAD_DIM
CM_CHUNK = 128
GDN_CHUNK = 64
CONV_K = 3

OFF_HQ, OFF_HI, OFF_HG, OFF_HFF, OFF_HFB = 0, 1024, 2048, 3072, 4096
OFF_CU, OFF_CV = 5120, 6144
OFF_GQ, OFF_GK, OFF_GV, OFF_GG = 7168, 8192, 9216, 10240
OFF_AB = 11264
OFF_GATES = 11296
IN_DIM = 17440
N_MAIN = OFF_AB

N_CTX_TOK = BATCH * SEQ
N_LAT_TOK = DEC_BATCH * DEC_SEQ
N_TOK = N_CTX_TOK + N_LAT_TOK
UNIT = DEC_SEQ
N_CTX_UNITS = N_CTX_TOK // UNIT
N_UNITS = N_TOK // UNIT
N_LAT_UNITS = N_UNITS - N_CTX_UNITS
SEQ_PER_UNIT = UNIT // SEQ

LANES = 128
MOD_ROWS = 8
HG_BLOCK = 256
HG_SUB = 32
HG_INST = 2 * (UNIT // HG_BLOCK)
HG_HEADS_PER_STEP = 2
VMEM_LIMIT = 56 * 1024 * 1024

_NT = (((1,), (1,)), ((), ()))
_TN = (((0,), (0,)), ((), ()))
_NN = (((1,), (0,)), ((), ()))


def _dg(a, b, dims):
    return lax.dot_general(a, b, dims, preferred_element_type=F32)


def _split2(x):
    hi = x.astype(BF16)
    lo = (x - hi.astype(F32)).astype(BF16)
    return hi, lo


def _dot1(a, b, dims=_NN):
    return _dg(a.astype(BF16), b.astype(BF16), dims)


def _dot3(a, b, dims=_NN):
    ah, al = _split2(a)
    bh, bl = _split2(b)
    return _dg(ah, bh, dims) + (_dg(ah, bl, dims) + _dg(al, bh, dims))


_dot_inv = _dot1


def _dot01(m, x):
    hi = x.astype(BF16)
    r = x - hi.astype(F32)
    mid = r.astype(BF16)
    lo = (r - mid.astype(F32)).astype(BF16)
    return _dg(m, hi, _NN) + (_dg(m, mid, _NN) + _dg(m, lo, _NN))


def _dot01_2(m, x):
    hi, lo = _split2(x)
    return _dg(m, hi, _NN) + _dg(m, lo, _NN)


def _sigmoid(x):
    return 1.0 / (1.0 + jnp.exp(-x))


def _silu(x):
    return x * _sigmoid(x)


def _gelu(x):
    return 0.5 * x * (1.0 + lax.erf(x * (2.0 ** -0.5)))


def _softplus(x):
    return jnp.maximum(x, 0.0) + jnp.log1p(jnp.exp(-jnp.abs(x)))


def _mod_row_of_tile(i, tm):
    return jnp.maximum(0, (i * tm - N_CTX_TOK) // DEC_SEQ + 1)


assert N_LAT_UNITS == N_CTX_UNITS


def _scan_unit(g):
    return jnp.where(g % 2 == 0, N_CTX_UNITS + g // 2, g // 2)


def _is_ctx_step(g):
    return g % 2 == 1


def _state_out_spec(layer, first, heads_per_step=None):
    if first:
        return pl.BlockSpec((SEQ_PER_UNIT, DEPTH, 2, heads_per_step, HEAD_DIM, HEAD_DIM),
                            lambda h, g: (g // 2, 0, 0, h, 0, 0))
    return pl.BlockSpec((SEQ_PER_UNIT, 1, 2, heads_per_step, HEAD_DIM, HEAD_DIM),
                        lambda h, g: (g // 2, layer, 0, h, 0, 0))


def _zero_other_slots(s_ref, slot):
    for other in range(s_ref.shape[1]):
        if other != slot:
            s_ref[:, other] = jnp.zeros((s_ref.shape[0],) + tuple(s_ref.shape[2:]), F32)


def _state_in_spec(layer, heads_per_step=None):
    return pl.BlockSpec((None, None, 2, heads_per_step, HEAD_DIM, HEAD_DIM),
                        lambda h, g: (g // 2, layer, 0, h, 0, 0))


_STATE_SHAPE = jax.ShapeDtypeStruct((BATCH, DEPTH, 2, HEADS, HEAD_DIM, HEAD_DIM), F32)


def _without_arg(kernel, pos):
    def wrapped(*refs):
        return kernel(*refs[:pos], *refs[pos + 1:])
    return wrapped


def _cparams(sem):
    return pltpu.CompilerParams(dimension_semantics=sem, vmem_limit_bytes=VMEM_LIMIT)


def _mod_kernel(c_ref, w_ref, b_ref, o_ref):
    s = _silu(c_ref[...])
    hi, lo = _split2(s)
    w = w_ref[...].astype(BF16)
    o_ref[...] = _dg(hi, w, _NN) + _dg(lo, w, _NN) + b_ref[...]


def _mod_call(cvec, w_mod, b_mod):
    tn = 2048
    n = 6 * D_MODEL
    return pl.pallas_call(
        _mod_kernel,
        grid=(DEPTH, n // tn),
        in_specs=[pl.BlockSpec((MOD_ROWS, D_MODEL), lambda l, j: (0, 0)),
                  pl.BlockSpec((None, D_MODEL, tn), lambda l, j: (l, 0, j)),
                  pl.BlockSpec((None, 1, tn), lambda l, j: (l, 0, j))],
        out_specs=pl.BlockSpec((None, MOD_ROWS, tn), lambda l, j: (l, 0, j)),
        out_shape=jax.ShapeDtypeStruct((DEPTH, MOD_ROWS, n), F32),
        compiler_params=_cparams(("arbitrary", "arbitrary")),
        name="modulation",
    )(cvec, w_mod, b_mod.reshape(DEPTH, 1, n))


def _adaln(x, g, shift, scale):
    y = x * lax.rsqrt(jnp.mean(x * x, axis=-1, keepdims=True) + EPS) * g
    return y * (1.0 + scale) + shift


def _residual_specs(x, tm, idx):
    if not isinstance(x, tuple):
        return [pl.BlockSpec((tm, D_MODEL), lambda *g: (idx(*g), 0))], [x]
    n_ctx = N_CTX_TOK // tm
    return ([pl.BlockSpec((tm, D_MODEL), lambda *g: (jnp.minimum(idx(*g), n_ctx - 1), 0)),
             pl.BlockSpec((tm, D_MODEL), lambda *g: (jnp.maximum(idx(*g) - n_ctx, 0), 0))], list(x))


def _residual_tile(refs, tile, tm):
    if len(refs) == 1:
        return refs[0][...]
    return jnp.where(tile < N_CTX_TOK // tm, refs[0][...], refs[1][...])


def _normmod_kernel(*refs, n_x):
    x_refs, (g_ref, sh_ref, sc_ref, o_ref) = refs[:n_x], refs[n_x:]
    x = _residual_tile(x_refs, pl.program_id(0), o_ref.shape[0])
    o_ref[...] = _adaln(x, g_ref[...], sh_ref[...], sc_ref[...]).astype(o_ref.dtype)


def _normmod_call(x, g, mod3, layer, k_shift, k_scale):
    tm = 512
    def mod_spec(k):
        return pl.BlockSpec((None, 1, D_MODEL),
                            lambda i: (layer * MOD_ROWS + _mod_row_of_tile(i, tm), 0, k))
    x_specs, x_args = _residual_specs(x, tm, lambda i: i)
    return pl.pallas_call(
        functools.partial(_normmod_kernel, n_x=len(x_args)),
        grid=(N_TOK // tm,),
        in_specs=x_specs + [pl.BlockSpec((1, D_MODEL), lambda i: (0, 0)), mod_spec(k_shift), mod_spec(k_scale)],
        out_specs=pl.BlockSpec((tm, D_MODEL), lambda i: (i, 0)),
        out_shape=jax.ShapeDtypeStruct((N_TOK, D_MODEL), BF16),
        compiler_params=_cparams(("arbitrary",)),
        name="norm_mod",
    )(*x_args, g.reshape(1, D_MODEL), mod3, mod3)


def _final_norm_kernel(x_ref, g_ref, o_ref):
    x = x_ref[...]
    o_ref[...] = x * lax.rsqrt(jnp.mean(x * x, axis=-1, keepdims=True) + EPS) * g_ref[...]


def _final_norm_call(x, g, row0, n_rows):
    tm = 512
    return pl.pallas_call(
        _final_norm_kernel,
        grid=(n_rows // tm,),
        in_specs=[pl.BlockSpec((tm, D_MODEL), lambda i: (row0 // tm + i, 0)),
                  pl.BlockSpec((1, D_MODEL), lambda i: (0, 0))],
        out_specs=pl.BlockSpec((tm, D_MODEL), lambda i: (i, 0)),
        out_shape=jax.ShapeDtypeStruct((n_rows, D_MODEL), F32),
        compiler_params=_cparams(("arbitrary",)),
        name="final_norm",
    )(x, g.reshape(1, D_MODEL))


def _mm_kernel(x_ref, w_ref, o_ref, wbf_ref, *, act, w_transposed):
    @pl.when(pl.program_id(1) == 0)
    def _cast_weights():
        wbf_ref[...] = w_ref[...].reshape(wbf_ref.shape).astype(BF16)

    acc = lax.dot_general(x_ref[...], wbf_ref[...], _NT if w_transposed else _NN, preferred_element_type=F32)
    if act == "relu2":
        acc = jnp.square(jnp.maximum(acc, 0.0))
    if len(o_ref.shape) == 3:
        for c in range(o_ref.shape[0]):
            o_ref[c] = acc[:, c * LANES:(c + 1) * LANES].astype(o_ref.dtype)
    else:
        o_ref[...] = acc.astype(o_ref.dtype)


def _mm_call(x, w_all, layer, *, col0, n, act, out_dtype, tm, tn, name, w_transposed=False,
             lane_tile_major=False):
    m, k = x.shape
    assert n % tn == 0 and m % tm == 0
    if w_transposed:
        assert col0 % 8 == 0
        w_spec = pl.BlockSpec((pl.Element(1), pl.Element(tn), pl.Element(k)),
                              lambda j, i: (layer, pl.multiple_of(col0 + j * tn, 8), 0))
        w_scratch = pltpu.VMEM((tn, k), BF16)
    else:
        assert col0 % tn == 0
        w_spec = pl.BlockSpec((None, k, tn), lambda j, i: (layer, 0, col0 // tn + j))
        w_scratch = pltpu.VMEM((k, tn), BF16)
    if lane_tile_major:
        out_spec = pl.BlockSpec((tn // LANES, tm, LANES), lambda j, i: (j, i, 0))
        out_shape = jax.ShapeDtypeStruct((n // LANES, m, LANES), out_dtype)
    else:
        out_spec = pl.BlockSpec((tm, tn), lambda j, i: (i, j))
        out_shape = jax.ShapeDtypeStruct((m, n), out_dtype)
    return pl.pallas_call(
        functools.partial(_mm_kernel, act=act, w_transposed=w_transposed),
        grid=(n // tn, m // tm),
        in_specs=[pl.BlockSpec((tm, k), lambda j, i: (i, 0)), w_spec],
        out_specs=out_spec,
        out_shape=out_shape,
        scratch_shapes=[w_scratch],
        compiler_params=_cparams(("arbitrary", "arbitrary")),
        name=name,
    )(x, w_all)


def _mm_bf16_kernel(x_ref, w_ref, r_ref, g_ref, o_ref):
    acc = jnp.dot(x_ref[...], w_ref[...], preferred_element_type=F32)
    o_ref[...] = r_ref[...] + g_ref[...] * acc


def _mm_resid_call(x, w, resid, mod3, layer, k_gate, *, tm, tn, name):
    m, k = x.shape
    n = w.shape[2]
    per_tile = D_MODEL // tn
    return pl.pallas_call(
        _mm_bf16_kernel,
        grid=(m // tm, n // tn),
        in_specs=[pl.BlockSpec((tm, k), lambda i, j: (i, 0)),
                  pl.BlockSpec((None, k, tn), lambda i, j: (layer, 0, j)),
                  pl.BlockSpec((tm, tn), lambda i, j: (i, j)),
                  pl.BlockSpec((None, 1, tn),
                               lambda i, j: (layer * MOD_ROWS + _mod_row_of_tile(i, tm), 0,
                                             k_gate * per_tile + j))],
        out_specs=pl.BlockSpec((tm, tn), lambda i, j: (i, j)),
        out_shape=jax.ShapeDtypeStruct((m, n), F32),
        compiler_params=_cparams(("arbitrary", "arbitrary")),
        name=name,
    )(x, w, resid, mod3)


def _outproj_norm_kernel(*refs, n_x):
    m_ref, w_ref = refs[:2]
    x_refs = refs[2:2 + n_x]
    gate_ref, g_ref, sh_ref, sc_ref, xo_ref, h_ref, wbf_ref = refs[2 + n_x:]

    @pl.when(pl.program_id(0) == 0)
    def _cast_weights():
        wbf_ref[...] = w_ref[...].astype(BF16)

    acc = jnp.dot(m_ref[...], wbf_ref[...], preferred_element_type=F32)
    xn = _residual_tile(x_refs, pl.program_id(0), xo_ref.shape[0]) + gate_ref[...] * acc
    xo_ref[...] = xn
    h_ref[...] = _adaln(xn, g_ref[...], sh_ref[...], sc_ref[...]).astype(h_ref.dtype)


def _outproj_norm_call(merged, w_out, x, norm_g, mod3, layer):
    tm = 256 if isinstance(x, tuple) else 512
    def mod_spec(k):
        return pl.BlockSpec((None, 1, D_MODEL),
                            lambda i: (layer * MOD_ROWS + _mod_row_of_tile(i, tm), 0, k))
    row = pl.BlockSpec((tm, D_MODEL), lambda i: (i, 0))
    x_specs, x_args = _residual_specs(x, tm, lambda i: i)
    return pl.pallas_call(
        functools.partial(_outproj_norm_kernel, n_x=len(x_args)),
        grid=(N_TOK // tm,),
        in_specs=[row,
                  pl.BlockSpec((None, D_MODEL, D_MODEL), lambda i: (layer, 0, 0), pipeline_mode=pl.Buffered(1))]
                 + x_specs + [mod_spec(2), pl.BlockSpec((1, D_MODEL), lambda i: (0, 0)), mod_spec(3), mod_spec(4)],
        out_specs=[row, row],
        out_shape=[jax.ShapeDtypeStruct((N_TOK, D_MODEL), F32),
                   jax.ShapeDtypeStruct((N_TOK, D_MODEL), BF16)],
        scratch_shapes=[pltpu.VMEM((D_MODEL, D_MODEL), BF16)],
        compiler_params=_cparams(("arbitrary",)),
        name="out_proj_norm",
    )(merged, w_out, *x_args, mod3, norm_g.reshape(1, D_MODEL), mod3, mod3)


def _ffn_down_norm_kernel(x_ref, w_ref, r_ref, gate_ref, g_ref, sh_ref, sc_ref, o_ref, h_ref, row_ref):
    j = pl.program_id(1)
    tn = o_ref.shape[1]
    acc = jnp.dot(x_ref[...], w_ref[...], preferred_element_type=F32)
    xn = r_ref[...] + gate_ref[...] * acc
    o_ref[...] = xn
    row_ref[:, pl.ds(pl.multiple_of(j * tn, tn), tn)] = xn

    @pl.when(j == pl.num_programs(1) - 1)
    def _next_layer_norm():
        h_ref[...] = _adaln(row_ref[...], g_ref[...], sh_ref[...], sc_ref[...]).astype(h_ref.dtype)


def _ffn_down_norm_call(up, w_bf, resid, mod3, layer, next_norm_g):
    tm, tn = 512, 512
    m, k = up.shape
    per_tile = D_MODEL // tn
    def mod_row(i):
        return _mod_row_of_tile(i, tm)
    def next_mod(kk):
        return pl.BlockSpec((None, 1, D_MODEL), lambda i, j: ((layer + 1) * MOD_ROWS + mod_row(i), 0, kk))
    return pl.pallas_call(
        _ffn_down_norm_kernel,
        grid=(m // tm, D_MODEL // tn),
        in_specs=[pl.BlockSpec((tm, k), lambda i, j: (i, 0)),
                  pl.BlockSpec((None, k, tn), lambda i, j: (layer, 0, j)),
                  pl.BlockSpec((tm, tn), lambda i, j: (i, j)),
                  pl.BlockSpec((None, 1, tn), lambda i, j: (layer * MOD_ROWS + mod_row(i), 0, 5 * per_tile + j)),
                  pl.BlockSpec((1, D_MODEL), lambda i, j: (0, 0)), next_mod(0), next_mod(1)],
        out_specs=[pl.BlockSpec((tm, tn), lambda i, j: (i, j)),
                   pl.BlockSpec((tm, D_MODEL), lambda i, j: (i, 0))],
        out_shape=[jax.ShapeDtypeStruct((m, D_MODEL), F32),
                   jax.ShapeDtypeStruct((m, D_MODEL), BF16)],
        scratch_shapes=[pltpu.VMEM((tm, D_MODEL), F32)],
        compiler_params=_cparams(("arbitrary", "arbitrary")),
        name="ffn_down_norm",
    )(up, w_bf, resid, mod3, next_norm_g.reshape(1, D_MODEL), mod3, mod3)


def _gated_merge_kernel(h_ref, wga_ref, wgb_ref, wgc_ref, oa_ref, ob_ref, oc_ref, wa_ref, wb_ref, wc_ref,
                        o_ref, wg_bf, wbr_bf):
    @pl.when(pl.program_id(1) == 0)
    def _cast_weights():
        for t, ref in enumerate((wga_ref, wgb_ref, wgc_ref)):
            wg_bf[t] = ref[...].reshape(wg_bf.shape[1:]).astype(BF16)
        for t, ref in enumerate((wa_ref, wb_ref, wc_ref)):
            wbr_bf[t] = ref[...].astype(BF16)

    def rows_of(ref):
        return jnp.concatenate([ref[t] for t in range(ref.shape[0])], axis=1)

    h = h_ref[...]
    acc = None
    for t, br_ref in enumerate((oa_ref, ob_ref, oc_ref)):
        gate = _sigmoid(lax.dot_general(h, wg_bf[t], _NT, preferred_element_type=F32))
        term = gate * jnp.dot(rows_of(br_ref), wbr_bf[t], preferred_element_type=F32)
        acc = term if acc is None else acc + term
    o_ref[...] = acc.astype(o_ref.dtype)


def _gated_merge_call(h, w_in_t, o_a, o_b, o_c, w_a, w_b, w_c, layer):
    tm, tn = 1024, 256
    def gate_w(b):
        return pl.BlockSpec((pl.Element(1), pl.Element(tn), pl.Element(D_MODEL)),
                            lambda j, i: (layer, pl.multiple_of(OFF_GATES + b * D_MODEL + j * tn, 8), 0))
    br = pl.BlockSpec((HEADS, tm, HEAD_DIM), lambda j, i: (0, i, 0))
    wt = pl.BlockSpec((None, HG_F, tn), lambda j, i: (layer, 0, j))
    return pl.pallas_call(
        _gated_merge_kernel,
        grid=(D_MODEL // tn, N_TOK // tm),
        in_specs=[pl.BlockSpec((tm, D_MODEL), lambda j, i: (i, 0)), gate_w(0), gate_w(1), gate_w(2),
                  br, br, br, wt, wt, wt],
        out_specs=pl.BlockSpec((tm, tn), lambda j, i: (i, j)),
        out_shape=jax.ShapeDtypeStruct((N_TOK, D_MODEL), BF16),
        scratch_shapes=[pltpu.VMEM((3, tn, D_MODEL), BF16), pltpu.VMEM((3, HG_F, tn), BF16)],
        compiler_params=_cparams(("arbitrary", "arbitrary")),
        name="gated_merge",
    )(h, w_in_t, w_in_t, w_in_t, o_a, o_b, o_c, w_a, w_b, w_c)


def _head_rmsnorm_gate(o, g, og):
    y = o * lax.rsqrt(jnp.mean(o * o, axis=-1, keepdims=True) + EPS) * g
    return y * _silu(og)


def _hgrn_kernel(q_ref, i_ref, og_ref, ff_ref, fb_ref, lb_ref, g_ref, s0_ref, o_ref, s_ref,
                 acc_ref, b_ref, stc_ref, km_ref, *, state_slot):
    is_ctx = _is_ctx_step(pl.program_id(1))
    _zero_other_slots(s_ref, state_slot)
    nblk = UNIT // HG_BLOCK
    nsub = HG_BLOCK // HG_SUB
    r = lax.broadcasted_iota(jnp.int32, (HG_BLOCK, HG_BLOCK), 0)
    c = lax.broadcasted_iota(jnp.int32, (HG_BLOCK, HG_BLOCK), 1)
    same = (r // HG_SUB) == (c // HG_SUB)
    row_sub = lax.broadcasted_iota(jnp.int32, (HG_BLOCK, HEAD_DIM), 0) // HG_SUB
    tri = (jnp.logical_and(same, c <= r), jnp.logical_and(same, c >= r))
    tri_bf = [jnp.where(t, 1.0, 0.0).astype(BF16) for t in tri]
    z_refs = (ff_ref, fb_ref)
    inst = [(hh, blk, d) for hh in range(HG_HEADS_PER_STEP) for blk in range(nblk) for d in range(2)]

    def expand(x):
        return jnp.broadcast_to(x[:, None, :], (nsub, HG_SUB, HEAD_DIM)).reshape(HG_BLOCK, HEAD_DIM)

    def blk_rows(blk):
        return slice(blk * HG_BLOCK, (blk + 1) * HG_BLOCK)

    def head_cols(hh):
        return slice(hh * HEAD_DIM, (hh + 1) * HEAD_DIM)

    qs = {(hh, blk): _silu(q_ref[hh, blk_rows(blk), :]) for hh in range(HG_HEADS_PER_STEP) for blk in range(nblk)}
    vs = {(hh, blk): i_ref[hh, blk_rows(blk), :] for hh in range(HG_HEADS_PER_STEP) for blk in range(nblk)}
    ks, lfs = [], []
    for hh, blk, d in inst:
        lb = lb_ref[d:d + 1, head_cols(hh)]
        f = lb + (1.0 - lb) * _sigmoid(z_refs[d][hh, blk_rows(blk), :])
        lfs.append(jnp.log(f) * LOG2E)
        ks.append(1.0 - f)
    bs = [_dot01_2(tri_bf[d], lf) for (hh, blk, d), lf in zip(inst, lfs)]
    tots, qts, kts, qds = [], [], [], []
    for n, (hh, blk, d) in enumerate(inst):
        b = bs[n]
        b_ref[n] = b
        tot = b_ref[n, pl.ds((HG_SUB - 1) if d == 0 else 0, nsub, stride=HG_SUB), :]
        mid_f = expand(b_ref[n, pl.ds(HG_SUB // 2, nsub, stride=HG_SUB), :])
        tots.append(tot)
        qts.append(qs[hh, blk] * jnp.exp2(b - mid_f))
        kts.append(ks[n] * jnp.exp2(mid_f - b))
        qds.append((qs[hh, blk] * jnp.exp2(b)).astype(BF16))
        kd = ks[n] * jnp.exp2(expand(tot) - b)
        for s in range(nsub):
            km_ref[n, :, s * HEAD_DIM:(s + 1) * HEAD_DIM] = jnp.where(row_sub == s, kd, 0.0).astype(BF16)
    scs = [jnp.where(tri[d], _dot1(qt, kt, _NT), 0.0) for (hh, blk, d), qt, kt in zip(inst, qts, kts)]
    uts = [_dg(vs[hh, blk].astype(BF16), km_ref[n], _TN) for n, (hh, blk, d) in enumerate(inst)]
    outs = [_dot1(sc, vs[hh, blk]) for (hh, blk, d), sc in zip(inst, scs)]
    for hh in range(HG_HEADS_PER_STEP):
        for d in range(2):
            st = s0_ref[d, hh].T
            for blk in (range(nblk) if d == 0 else range(nblk - 1, -1, -1)):
                n = inst.index((hh, blk, d))
                st = jnp.where(is_ctx, 0.0, st)
                for s in (range(nsub) if d == 0 else range(nsub - 1, -1, -1)):
                    stc_ref[n, s] = st.T.astype(BF16)
                    st = st * jnp.exp2(tots[n][s:s + 1, :]) + uts[n][:, s * HEAD_DIM:(s + 1) * HEAD_DIM]
                s_ref[blk, state_slot, d, hh] = st.T
    for n, (hh, blk, d) in enumerate(inst):
        o_int = [_dg(qds[n][s * HG_SUB:(s + 1) * HG_SUB], stc_ref[n, s], _NN) for s in range(nsub)]
        acc_ref[d, hh, blk_rows(blk), :] = outs[n] + jnp.concatenate(o_int, axis=0)
    for hh in range(HG_HEADS_PER_STEP):
        o_ref[hh] = _head_rmsnorm_gate(acc_ref[0, hh] + acc_ref[1, hh], g_ref[...], og_ref[hh]).astype(o_ref.dtype)


def _hgrn_call(z, lb, onorm_g, state, layer, new_states):
    nh = HG_HEADS_PER_STEP
    def col(off):
        return pl.BlockSpec((nh, UNIT, HEAD_DIM), lambda h, g: (off // HEAD_DIM // nh + h, _scan_unit(g), 0))
    in_specs = [col(OFF_HQ), col(OFF_HI), col(OFF_HG), col(OFF_HFF), col(OFF_HFB),
                pl.BlockSpec((2, nh * HEAD_DIM), lambda h, g: (0, h)),
                pl.BlockSpec((1, HEAD_DIM), lambda h, g: (0, 0)),
                _state_in_spec(layer, nh)]
    args = [z, z, z, z, z, lb, onorm_g.reshape(1, HEAD_DIM), state]
    first = new_states is None
    kernel, aliases = functools.partial(_hgrn_kernel, state_slot=layer if first else 0), {}
    if not first:
        kernel, aliases = _without_arg(kernel, len(args)), {len(args): 1}
        in_specs.append(pl.BlockSpec(memory_space=pl.ANY))
        args.append(new_states)
    n_inst = nh * HG_INST
    return pl.pallas_call(
        kernel,
        grid=(HEADS // nh, N_UNITS),
        in_specs=in_specs,
        out_specs=[pl.BlockSpec((nh, UNIT, HEAD_DIM), lambda h, g: (h, _scan_unit(g), 0)),
                   _state_out_spec(layer, first, nh)],
        out_shape=[jax.ShapeDtypeStruct((HEADS, N_TOK, HEAD_DIM), BF16), _STATE_SHAPE],
        input_output_aliases=aliases,
        scratch_shapes=[pltpu.VMEM((2, nh, UNIT, HEAD_DIM), F32),
                        pltpu.VMEM((n_inst, HG_BLOCK, HEAD_DIM), F32),
                        pltpu.VMEM((n_inst, HG_BLOCK // HG_SUB, HEAD_DIM, HEAD_DIM), BF16),
                        pltpu.VMEM((n_inst, HG_BLOCK, (HG_BLOCK // HG_SUB) * HEAD_DIM), BF16)],
        compiler_params=_cparams(("arbitrary", "arbitrary")),
        name="hgrn2_scan",
    )(*args)


def _gmlp_kernel(u_ref, v_ref, vn_ref, ws_ref, bs_ref, o_ref):
    tm = u_ref.shape[1]
    for g in range(CM_GROUPS):
        cols = slice(g * HEAD_DIM, (g + 1) * HEAD_DIM)
        vg = _gelu(v_ref[g])
        vg = vg * lax.rsqrt(jnp.mean(vg * vg, axis=-1, keepdims=True) + EPS) * vn_ref[:, cols]
        ug = _gelu(u_ref[g])
        w = ws_ref[g]
        bias = bs_ref[:, g:g + 1]
        for ch in range(tm // CM_CHUNK):
            rows = slice(ch * CM_CHUNK, (ch + 1) * CM_CHUNK)
            s = _dot3(w, vg[rows]) + bias
            o_ref[g, rows, :] = (ug[rows] * s).astype(o_ref.dtype)


def _gmlp_call(z, vnorm_g, ws, bs_t):
    tm = 512
    return pl.pallas_call(
        _gmlp_kernel,
        grid=(N_TOK // tm,),
        in_specs=[pl.BlockSpec((CM_GROUPS, tm, HEAD_DIM), lambda i: (OFF_CU // CM_W, i, 0)),
                  pl.BlockSpec((CM_GROUPS, tm, HEAD_DIM), lambda i: (OFF_CV // CM_W, i, 0)),
                  pl.BlockSpec((1, CM_W), lambda i: (0, 0)),
                  pl.BlockSpec((CM_GROUPS, CM_CHUNK, CM_CHUNK), lambda i: (0, 0, 0)),
                  pl.BlockSpec((CM_CHUNK, CM_GROUPS), lambda i: (0, 0))],
        out_specs=pl.BlockSpec((CM_GROUPS, tm, HEAD_DIM), lambda i: (0, i, 0)),
        out_shape=jax.ShapeDtypeStruct((CM_GROUPS, N_TOK, HEAD_DIM), BF16),
        compiler_params=_cparams(("arbitrary",)),
        name="chunk_gmlp",
    )(z, z, vnorm_g.reshape(1, CM_W), ws, bs_t)


GP_BLOCK = 256


def _gdn_gates_kernel(ab_ref, alog_ref, dt_ref, col_ref, rowt_ref):
    ab = ab_ref[...]
    lane = lax.broadcasted_iota(jnp.int32, ab.shape, 1)
    g = jnp.where(lane < 2 * HEADS, (-LOG2E) * jnp.exp(alog_ref[...]) * _softplus(ab + dt_ref[...]), 0.0)
    r = lax.broadcasted_iota(jnp.int32, (GP_BLOCK, GP_BLOCK), 0)
    c = lax.broadcasted_iota(jnp.int32, (GP_BLOCK, GP_BLOCK), 1)
    same = (r // GDN_CHUNK) == (c // GDN_CHUNK)
    tri_f = jnp.where(jnp.logical_and(same, c <= r), 1.0, 0.0).astype(BF16)
    tri_b = jnp.where(jnp.logical_and(same, c >= r), 1.0, 0.0).astype(BF16)
    cf = _dot01(tri_f, g)
    cb = _dot01(tri_b, g)
    col = jnp.where(lane < HEADS, cf, jnp.where(lane < 2 * HEADS, cb, _sigmoid(ab)))
    col_ref[...] = col
    rowt_ref[...] = col.T


def _gdn_gates_call(ab, a_log, dt_bias):
    pad = lambda t: jnp.pad(t.reshape(1, 2 * HEADS), ((0, 0), (0, LANES - 2 * HEADS)))
    return pl.pallas_call(
        _gdn_gates_kernel,
        grid=(N_TOK // GP_BLOCK,),
        in_specs=[pl.BlockSpec((GP_BLOCK, LANES), lambda i: (i, 0)),
                  pl.BlockSpec((1, LANES), lambda i: (0, 0)),
                  pl.BlockSpec((1, LANES), lambda i: (0, 0))],
        out_specs=[pl.BlockSpec((GP_BLOCK, LANES), lambda i: (i, 0)),
                   pl.BlockSpec((LANES, GP_BLOCK), lambda i: (0, i))],
        out_shape=[jax.ShapeDtypeStruct((N_TOK, LANES), F32),
                   jax.ShapeDtypeStruct((LANES, N_TOK), F32)],
        compiler_params=_cparams(("arbitrary",)),
        name="gdn_gates",
    )(ab, pad(a_log), pad(dt_bias))


CONV_PAD = 72
GDN_AQ = HEAD_DIM + GDN_CHUNK
GDN_HEADS_PER_STEP = 2
GDN_STEP = 16 * GDN_CHUNK


def _gdn_kernel(qr_ref, kr_ref, vr_ref, og_ref, col_ref, rowt_ref, cwq_ref, cwk_ref, cwv_ref,
                g_ref, s0_ref, o_ref, s_ref,
                xp_ref, q_s, k_s, v_s, o_s, b_s, aq_s, *, state_slot):
    is_ctx = _is_ctx_step(pl.program_id(1))
    _zero_other_slots(s_ref, state_slot)
    heads = [GDN_HEADS_PER_STEP * pl.program_id(0) + hh for hh in range(GDN_HEADS_PER_STEP)]
    n_chunks = UNIT // GDN_CHUNK
    chunks_per_seq = SEQ // GDN_CHUNK

    t = lax.broadcasted_iota(jnp.int32, (UNIT, 1), 0)
    period = jnp.where(is_ctx, SEQ, GRID_W)
    pos = jnp.bitwise_and(t, period - 1)
    ok_left = pos != 0
    ok_right = pos != period - 1
    zeros_pad = jnp.zeros((CONV_K, CONV_PAD, HEAD_DIM), F32)
    xp_ref[:, 0:CONV_PAD, :] = zeros_pad
    xp_ref[:, CONV_PAD + UNIT:CONV_PAD + UNIT + CONV_PAD, :] = zeros_pad

    def conv_silu(x_ref, w_ref, hh):
        x = x_ref[hh]
        xp_ref[0, CONV_PAD:CONV_PAD + UNIT, :] = jnp.where(ok_right, x, 0.0)
        xp_ref[1, CONV_PAD:CONV_PAD + UNIT, :] = x
        xp_ref[2, CONV_PAD:CONV_PAD + UNIT, :] = jnp.where(ok_left, x, 0.0)
        acc = jnp.zeros((UNIT, HEAD_DIM), F32)
        for i in range(CONV_K):
            for j in range(CONV_K):
                w = w_ref[CONV_K * i + j:CONV_K * i + j + 1, hh * HEAD_DIM:(hh + 1) * HEAD_DIM]
                if i != CONV_K // 2:
                    w = jnp.where(is_ctx, 0.0, w)
                start = CONV_PAD + (i - 1) * GRID_W + (j - 1)
                acc = acc + xp_ref[j, start:start + UNIT, :] * w
        return _silu(acc)

    def l2norm(x):
        return x * lax.rsqrt(jnp.sum(x * x, axis=-1, keepdims=True) + EPS)

    rr = lax.broadcasted_iota(jnp.int32, (GDN_CHUNK, GDN_CHUNK), 0)
    cc = lax.broadcasted_iota(jnp.int32, (GDN_CHUNK, GDN_CHUNK), 1)
    eye = jnp.where(rr == cc, 1.0, 0.0)
    same_blk = [(rr // b) == (cc // b) for b in (8, 16, 32, 64)]
    lane = lax.broadcasted_iota(jnp.int32, (GDN_STEP, LANES), 1)
    sub8 = lax.broadcasted_iota(jnp.int32, (HEADS, GDN_STEP), 0)

    def pick(x, j):
        return jnp.sum(jnp.where(lane[:x.shape[0]] == j, x, 0.0), axis=-1, keepdims=True)

    incl = (cc <= rr, cc >= rr)
    strict = (cc < rr, cc > rr)
    off_masks = [jnp.logical_and(same_blk[lvl], jnp.logical_not(same_blk[lvl - 1]))
                 for lvl in range(1, len(same_blk))]

    def phase1(p, hh):
        head = heads[hh]
        rows = pl.ds(pl.multiple_of(p * GDN_STEP, GDN_STEP), GDN_STEP)
        q2, k2, v2 = q_s[rows, :], k_s[rows, :], v_s[rows, :]
        col = col_ref[rows, :]
        gcols = [pick(col, d * HEADS + head) for d in range(2)]
        betas = [pick(col, (2 + d) * HEADS + head) for d in range(2)]
        grows = [jnp.sum(jnp.where(sub8 == head, rowt_ref[d * HEADS:(d + 1) * HEADS, rows], 0.0),
                         axis=0, keepdims=True) for d in range(2)]
        inst = []
        for half in range(GDN_STEP // GDN_CHUNK):
            sl = slice(half * GDN_CHUNK, (half + 1) * GDN_CHUNK)
            qc, kc, vc = q2[sl], k2[sl], v2[sl]
            kk = _dot1(kc, kc, _NT)
            qk = _dot1(qc, kc, _NT)
            for d in range(2):
                gcol, beta = gcols[d][sl], betas[d][sl]
                decay = jnp.where(incl[d], jnp.exp2(gcol - grows[d][:, sl]), 0.0)
                m = jnp.where(strict[d], beta * kk * decay, 0.0)
                inst.append((half, d, qc, kc, vc, qk, gcol, beta, decay, m))
        ms = [t[-1] for t in inst]
        pws = [jnp.where(same_blk[0], -m, 0.0) for m in ms]
        tinvs = [eye + pw for pw in pws]
        for _ in range(2):
            pws = [_dot_inv(pw, pw) for pw in pws]
            tinvs = [t + _dot_inv(t, pw) for t, pw in zip(tinvs, pws)]
        for mask in off_masks:
            tmp = [_dot_inv(jnp.where(mask, m, 0.0), t) for m, t in zip(ms, tinvs)]
            tinvs = [t - _dot_inv(t, x) for t, x in zip(tinvs, tmp)]
        uws = [_dot1(tinv, jnp.concatenate([vc * beta, kc * (beta * jnp.exp2(gcol))], axis=1)).astype(BF16)
               for (half, d, qc, kc, vc, qk, gcol, beta, decay, m), tinv in zip(inst, tinvs)]
        bas, ows = [], []
        for (half, d, qc, kc, vc, qk, gcol, beta, decay, m), uw in zip(inst, uws):
            glast = gcol[GDN_CHUNK - 1:GDN_CHUNK] if d == 0 else gcol[0:1]
            bas.append(_dg((kc * jnp.exp2(glast - gcol)).astype(BF16), uw, _TN))
            ows.append(_dg((qk * decay).astype(BF16), uw, _NN))
        for (half, d, qc, kc, vc, qk, gcol, beta, decay, m), ba, ow in zip(inst, bas, ows):
            chunk0 = p * GDN_STEP + half * GDN_CHUNK
            o_s[hh, d, pl.ds(pl.multiple_of(chunk0, GDN_CHUNK), GDN_CHUNK), :] = ow[:, :HEAD_DIM]
            b_s[hh, d, pl.ds(pl.multiple_of(2 * chunk0, HEAD_DIM), HEAD_DIM), :] = ba[:, :HEAD_DIM]
            aq_s[hh, d, pl.ds(pl.multiple_of(3 * chunk0, GDN_AQ), GDN_AQ), :] = jnp.concatenate(
                [ba[:, HEAD_DIM:], qc * jnp.exp2(gcol) - ow[:, HEAD_DIM:]], axis=0).astype(BF16)

    for hh in range(GDN_HEADS_PER_STEP):
        q_s[...] = l2norm(conv_silu(qr_ref, cwq_ref, hh)) * (HEAD_DIM ** -0.5)
        k_s[...] = l2norm(conv_silu(kr_ref, cwk_ref, hh))
        v_s[...] = conv_silu(vr_ref, cwv_ref, hh)
        for p in range(UNIT // GDN_STEP):
            phase1(p, hh)

    lane1 = lax.broadcasted_iota(jnp.int32, (1, LANES), 1)

    def advance(chains, states):
        rs = [_dg(aq_s[hh, d, pl.ds(pl.multiple_of(ch * GDN_AQ, GDN_AQ), GDN_AQ), :], s.astype(BF16), _NN)
              for (hh, d, ch), s in zip(chains, states)]
        new = []
        for (hh, d, ch), s, r in zip(chains, states, rs):
            tile0 = pl.multiple_of(ch * GDN_CHUNK + (GDN_CHUNK - 8 if d == 0 else 0), 8)
            last = col_ref[pl.ds(tile0, 8), :]
            last = last[7:8] if d == 0 else last[0:1]
            glast = jnp.sum(jnp.where(lane1 == d * HEADS + heads[hh], last, 0.0), axis=-1, keepdims=True)
            b = b_s[hh, d, pl.ds(pl.multiple_of(ch * HEAD_DIM, HEAD_DIM), HEAD_DIM), :]
            new.append(s * jnp.exp2(glast) + (b - r[:HEAD_DIM]))
            rows = pl.ds(pl.multiple_of(ch * GDN_CHUNK, GDN_CHUNK), GDN_CHUNK)
            o_s[hh, d, rows, :] = o_s[hh, d, rows, :] + r[HEAD_DIM:]
        return tuple(new)

    @pl.when(is_ctx)
    def _context_unit():
        def body(i, states):
            chains = [(hh, d, sq * chunks_per_seq + (i if d == 0 else chunks_per_seq - 1 - i))
                      for hh in range(GDN_HEADS_PER_STEP) for sq in range(SEQ_PER_UNIT) for d in range(2)]
            return advance(chains, states)
        zero = jnp.zeros((HEAD_DIM, HEAD_DIM), F32)
        final = lax.fori_loop(0, chunks_per_seq, body, (zero,) * (2 * SEQ_PER_UNIT * GDN_HEADS_PER_STEP))
        for hh in range(GDN_HEADS_PER_STEP):
            for sq in range(SEQ_PER_UNIT):
                for d in range(2):
                    s_ref[sq, state_slot, d, hh] = final[(hh * SEQ_PER_UNIT + sq) * 2 + d]

    @pl.when(jnp.logical_not(is_ctx))
    def _latent_unit():
        def body(i, states):
            chains = [(hh, d, i if d == 0 else n_chunks - 1 - i) for hh in range(GDN_HEADS_PER_STEP) for d in range(2)]
            return advance(chains, states)
        init = tuple(s0_ref[d, hh] for hh in range(GDN_HEADS_PER_STEP) for d in range(2))
        final = lax.fori_loop(0, n_chunks, body, init)
        for hh in range(GDN_HEADS_PER_STEP):
            for sq in range(SEQ_PER_UNIT):
                for d in range(2):
                    s_ref[sq, state_slot, d, hh] = final[2 * hh + d]

    for hh in range(GDN_HEADS_PER_STEP):
        o_ref[hh] = _head_rmsnorm_gate(o_s[hh, 0] + o_s[hh, 1], g_ref[...], og_ref[hh]).astype(o_ref.dtype)


def _gdn_call(z, col, rowt, conv_w, onorm_g, state, layer, new_states):
    nh = GDN_HEADS_PER_STEP
    def zcol(off):
        return pl.BlockSpec((nh, UNIT, HEAD_DIM), lambda h, g: (off // HEAD_DIM // nh + h, _scan_unit(g), 0))
    def wcol(part):
        return pl.BlockSpec((CONV_K * CONV_K, nh * HEAD_DIM), lambda h, g: (0, part * HEADS // nh + h))
    scr = lambda *shape: pltpu.VMEM(shape, F32)
    in_specs = [zcol(OFF_GQ), zcol(OFF_GK), zcol(OFF_GV), zcol(OFF_GG),
                pl.BlockSpec((UNIT, LANES), lambda h, g: (_scan_unit(g), 0)),
                pl.BlockSpec((LANES, UNIT), lambda h, g: (0, _scan_unit(g))),
                wcol(0), wcol(1), wcol(2),
                pl.BlockSpec((1, HEAD_DIM), lambda h, g: (0, 0)),
                _state_in_spec(layer, nh)]
    args = [z, z, z, z, col, rowt, conv_w, conv_w, conv_w, onorm_g.reshape(1, HEAD_DIM), state]
    first = new_states is None
    kernel, aliases = functools.partial(_gdn_kernel, state_slot=layer if first else 0), {}
    if not first:
        kernel, aliases = _without_arg(kernel, len(args)), {len(args): 1}
        in_specs.append(pl.BlockSpec(memory_space=pl.ANY))
        args.append(new_states)
    return pl.pallas_call(
        kernel,
        grid=(HEADS // nh, N_UNITS),
        in_specs=in_specs,
        out_specs=[pl.BlockSpec((nh, UNIT, HEAD_DIM), lambda h, g: (h, _scan_unit(g), 0)),
                   _state_out_spec(layer, first, nh)],
        out_shape=[jax.ShapeDtypeStruct((HEADS, N_TOK, HEAD_DIM), BF16), _STATE_SHAPE],
        input_output_aliases=aliases,
        scratch_shapes=[scr(CONV_K, UNIT + 2 * CONV_PAD, HEAD_DIM),
                        scr(UNIT, HEAD_DIM), scr(UNIT, HEAD_DIM), scr(UNIT, HEAD_DIM),
                        scr(nh, 2, UNIT, HEAD_DIM),
                        scr(nh, 2, (UNIT // GDN_CHUNK) * HEAD_DIM, HEAD_DIM),
                        pltpu.VMEM((nh, 2, (UNIT // GDN_CHUNK) * GDN_AQ, HEAD_DIM), BF16)],
        compiler_params=_cparams(("arbitrary", "arbitrary")),
        name="gdn_scan",
    )(*args)


def kernel(x_prompt, x_sample, c, state_hgrn, state_gdn, c_ctx, norm1_g, norm2_g, w_mod, b_mod, w_in, hg_lb, hg_onorm_g, cm_vnorm_g, cm_ws, cm_bs, gdn_conv, gdn_A_log, gdn_dt_bias, gdn_onorm_g, w_br_hg, w_br_cm, w_br_gdn, w_out, w_ff1, w_ff2, final_g):
    x = (x_prompt.reshape(N_CTX_TOK, D_MODEL), x_sample.reshape(N_LAT_TOK, D_MODEL))
    cvec = jnp.concatenate([c_ctx[None, :], c, jnp.zeros((MOD_ROWS - 1 - DEC_BATCH, D_MODEL), F32)], axis=0)
    mod3 = _mod_call(cvec, w_mod, b_mod).reshape(DEPTH * MOD_ROWS, 1, 6 * D_MODEL)

    lb_all = jnp.cumsum(jax.nn.softmax(hg_lb.astype(F32), axis=0), axis=0)
    lb_all = lb_all - lb_all[:1]

    w_in_t = jnp.swapaxes(w_in, 1, 2)
    w_ff2_bf = w_ff2.astype(BF16)

    new_hg = new_gdn = None
    h = _normmod_call(x, norm1_g[0], mod3, 0, 0, 1)
    for l in range(DEPTH):
        z = _mm_call(h, w_in_t, l, col0=0, n=N_MAIN, act=None, out_dtype=F32, tm=1024, tn=1024,
                     w_transposed=True, lane_tile_major=True, name="in_proj")
        ab = _mm_call(h, w_in_t, l, col0=OFF_AB, n=LANES, act=None, out_dtype=F32, tm=1024, tn=LANES,
                      w_transposed=True, name="in_proj_ab")
        o_a, new_hg = _hgrn_call(z, lb_all[l], hg_onorm_g[l], state_hgrn, l, new_hg)
        o_b = _gmlp_call(z, cm_vnorm_g[l], cm_ws[l], cm_bs[l].T)
        col, rowt = _gdn_gates_call(ab, gdn_A_log[l], gdn_dt_bias[l])
        o_c, new_gdn = _gdn_call(z, col, rowt, gdn_conv[l].reshape(CONV_K * CONV_K, 3 * HG_F),
                                 gdn_onorm_g[l], state_gdn, l, new_gdn)

        merged = _gated_merge_call(h, w_in_t, o_a, o_b, o_c, w_br_hg, w_br_cm, w_br_gdn, l)
        x, h2 = _outproj_norm_call(merged, w_out, x, norm2_g[l], mod3, l)
        up = _mm_call(h2, w_ff1, l, col0=0, n=D_FF, act="relu2", out_dtype=BF16, tm=2048, tn=1024, name="ffn_up")
        if l + 1 < DEPTH:
            x, h = _ffn_down_norm_call(up, w_ff2_bf, x, mod3, l, norm1_g[l + 1])
        else:
            x = _mm_resid_call(up, w_ff2_bf, x, mod3, l, 5, tm=512, tn=512, name="ffn_down")

    y_prompt = _final_norm_call(x, final_g, 0, N_CTX_TOK).reshape(BATCH, SEQ, D_MODEL)
    y_sample = _final_norm_call(x, final_g, N_CTX_TOK, N_LAT_TOK).reshape(DEC_BATCH, DEC_SEQ, D_MODEL)
    return (y_prompt, y_sample, new_hg, new_gdn)
```
